```python
import jax, jax.numpy as jnp
from jax import lax
import numpy as np

D_MODEL = 1024
BATCH = 8
SEQ = 8192
DEPTH = 1

PLE_DIM = 256
D_FF = 2816
HG_HEADS = 8
HG_DK = 128
HG_DV = 128
HG_WIDTH = HG_HEADS * HG_DK
HG_VWIDTH = HG_HEADS * HG_DV
CHUNK = 64
POOL_WINDOWS = (2, 4, 8, 16)
POOL_GROUPS = 4
POOL_CH = 128
POOL_WIDTH = POOL_GROUPS * POOL_CH
IN_SIZES = (HG_WIDTH, HG_WIDTH, HG_VWIDTH, HG_VWIDTH, POOL_WIDTH, D_MODEL, D_MODEL)
IN_COLS = HG_WIDTH * 2 + HG_VWIDTH * 2 + POOL_WIDTH + 2 * D_MODEL
EPS = 1e-6

kernel_name = "hybrid_hgrn2_pool_macaron_block"


def _rmsnorm(x, g):
    xf = x.astype(jnp.float32)
    y = xf * lax.rsqrt(jnp.mean(xf * xf, axis=-1, keepdims=True) + EPS)
    return (y * g.astype(jnp.float32)).astype(x.dtype)


def _swiglu(h, w1, w3, w2):
    return (jax.nn.silu(h @ w1) * (h @ w3)) @ w2


def _hgrn2_chunked(q, k, v, log_f):
    B, S, H, DK = q.shape
    DV = v.shape[-1]
    n_chunks = S // CHUNK

    def to_chunks(t):
        return t.reshape(B, n_chunks, CHUNK, H, t.shape[-1]).transpose(1, 0, 3, 2, 4)

    qc, kc, vc, gc = to_chunks(q), to_chunks(k), to_chunks(v), to_chunks(log_f)
    causal = jnp.tril(jnp.ones((CHUNK, CHUNK), dtype=bool))[:, :, None]

    def step(state, inp):
        qb, kb, vb, gb = inp
        G = jnp.cumsum(gb, axis=2)
        diff = G[:, :, :, None, :] - G[:, :, None, :, :]
        decay = jnp.exp(jnp.where(causal, diff, -jnp.inf))
        scores = jnp.einsum('bhtk,bhsk,bhtsk->bhts', qb, kb, decay)
        o_intra = jnp.einsum('bhts,bhsv->bhtv', scores, vb)
        o_inter = jnp.einsum('bhtk,bhkv->bhtv', qb * jnp.exp(G), state)
        G_last = G[:, :, -1:, :]
        k_dec = kb * jnp.exp(G_last - G)
        new_state = (jnp.exp(G_last[:, :, 0, :])[..., None] * state
                     + jnp.einsum('bhsk,bhsv->bhkv', k_dec, vb))
        return new_state, o_intra + o_inter

    state0 = jnp.zeros((B, H, DK, DV), jnp.float32)
    _, oc = lax.scan(step, state0, (qc, kc, vc, gc))
    return oc.transpose(1, 0, 3, 2, 4).reshape(B, S, H, DV)


def _causal_multiscale_pool(u):
    B, S, G, C = u.shape
    uf = u.astype(jnp.float32)
    cs = jnp.concatenate([jnp.zeros((B, 1, G, C), jnp.float32), jnp.cumsum(uf, axis=1)], axis=1)
    pos = jnp.arange(1, S + 1, dtype=jnp.float32)
    outs = []
    for g, w in enumerate(POOL_WINDOWS):
        csg = cs[:, :, g]
        upper = csg[:, 1:]
        lower = jnp.concatenate([jnp.zeros((B, w - 1, C), jnp.float32), csg[:, :S - w + 1]], axis=1)
        count = jnp.minimum(pos, float(w))[None, :, None]
        outs.append((upper - lower) / count - uf[:, :, g])
    return jnp.stack(outs, axis=2).astype(u.dtype)


def _normal(key, shape, fan_in):
    return jax.random.normal(key, shape, jnp.float32) * (fan_in ** -0.5)


def _gain(key, shape):
    return 1.0 + 0.02 * jax.random.normal(key, shape, jnp.float32)


def _fwd_setup_inputs(seed: int = 0) -> dict:
    key = jax.random.key(seed)
    ks = jax.random.split(key, 26)
    L = DEPTH
    return {
        "x": jax.random.normal(ks[0], (BATCH, SEQ, D_MODEL), jnp.float32),
        "p": jax.random.normal(ks[1], (DEPTH, BATCH, SEQ, PLE_DIM), jnp.float32),
        "ffn1_norm": _gain(ks[2], (L, D_MODEL)),
        "ffn1_w1": _normal(ks[3], (L, D_MODEL, D_FF), D_MODEL),
        "ffn1_w3": _normal(ks[4], (L, D_MODEL, D_FF), D_MODEL),
        "ffn1_w2": _normal(ks[5], (L, D_FF, D_MODEL), D_FF),
        "mix_norm": _gain(ks[6], (L, D_MODEL)),
        "w_in": _normal(ks[7], (L, D_MODEL, IN_COLS), D_MODEL),
        "hgrn_lb": 0.1 * jax.random.normal(ks[8], (L + 1, HG_WIDTH), jnp.float32),
        "hgrn_onorm": _gain(ks[9], (L, HG_VWIDTH)),
        "w_branch_a": _normal(ks[10], (L, HG_VWIDTH, D_MODEL), HG_VWIDTH),
        "pool_w": _normal(ks[11], (L, POOL_GROUPS, POOL_CH, POOL_CH), POOL_CH),
        "pool_scale": _gain(ks[12], (L, POOL_WIDTH)),
        "w_branch_b": _normal(ks[13], (L, POOL_WIDTH, D_MODEL), POOL_WIDTH),
        "w_out": _normal(ks[14], (L, D_MODEL, D_MODEL), D_MODEL),
        "ffn2_norm": _gain(ks[15], (L, D_MODEL)),
        "ffn2_w1": _normal(ks[16], (L, D_MODEL, D_FF), D_MODEL),
        "ffn2_w3": _normal(ks[17], (L, D_MODEL, D_FF), D_MODEL),
        "ffn2_w2": _normal(ks[18], (L, D_FF, D_MODEL), D_FF),
        "ple_norm": _gain(ks[19], (L, D_MODEL)),
        "ple_w_gate": _normal(ks[20], (L, D_MODEL, D_MODEL), D_MODEL),
        "ple_w_proj": _normal(ks[21], (L, PLE_DIM, D_MODEL), PLE_DIM),
        "ple_post_norm": _gain(ks[22], (L, D_MODEL)),
        "final_norm": _gain(ks[23], (D_MODEL,)),
    }


def _fwd_reference(x, p, ffn1_norm, ffn1_w1, ffn1_w3, ffn1_w2, mix_norm, w_in, hgrn_lb, hgrn_onorm,
              w_branch_a, pool_w, pool_scale, w_branch_b, w_out, ffn2_norm, ffn2_w1, ffn2_w3,
              ffn2_w2, ple_norm, ple_w_gate, ple_w_proj, ple_post_norm, final_norm):
    B, S, _ = x.shape
    split_points = [int(v) for v in np.cumsum(IN_SIZES)[:-1]]
    lb_all = jnp.cumsum(jax.nn.softmax(hgrn_lb.astype(jnp.float32), axis=0), axis=0)

    for i in range(DEPTH):
        h = _rmsnorm(x, ffn1_norm[i])
        x = x + 0.5 * _swiglu(h, ffn1_w1[i], ffn1_w3[i], ffn1_w2[i])

        h = _rmsnorm(x, mix_norm[i])
        proj = h @ w_in[i]
        q_r, f_r, i_r, og_r, pool_r, ga_r, gb_r = jnp.split(proj, split_points, axis=-1)

        lb = lb_all[i]
        f = lb + (1.0 - lb) * jax.nn.sigmoid(f_r.astype(jnp.float32))
        log_f = jnp.log(f).reshape(B, S, HG_HEADS, HG_DK)
        k = (1.0 - f).reshape(B, S, HG_HEADS, HG_DK)
        q = jax.nn.silu(q_r.astype(jnp.float32)).reshape(B, S, HG_HEADS, HG_DK)
        v = i_r.astype(jnp.float32).reshape(B, S, HG_HEADS, HG_DV)
        o = _hgrn2_chunked(q, k, v, log_f).astype(x.dtype)
        o = _rmsnorm(o, hgrn_onorm[i].reshape(HG_HEADS, HG_DV)) * jax.nn.silu(og_r.reshape(B, S, HG_HEADS, HG_DV))
        y_a = o.reshape(B, S, HG_VWIDTH) @ w_branch_a[i]

        u = pool_r.reshape(B, S, POOL_GROUPS, POOL_CH)
        pooled = _causal_multiscale_pool(u)
        mixed = jnp.einsum('bsgc,gcd->bsgd', pooled, pool_w[i]).reshape(B, S, POOL_WIDTH) * pool_scale[i]
        y_b = mixed @ w_branch_b[i]

        y = jax.nn.sigmoid(ga_r) * y_a + jax.nn.sigmoid(gb_r) * y_b
        x = x + y @ w_out[i]

        h = _rmsnorm(x, ffn2_norm[i])
        x = x + 0.5 * _swiglu(h, ffn2_w1[i], ffn2_w3[i], ffn2_w2[i])

        gate = jax.nn.sigmoid(_rmsnorm(x, ple_norm[i]) @ ple_w_gate[i])
        e = _rmsnorm(p[i] @ ple_w_proj[i], ple_post_norm[i])
        x = x + gate * e

    return _rmsnorm(x, final_norm)


import jax as _jax
import jax.numpy as _jnp

TWIN_FORMAT = 'train_step'
FWD_PARAMS = ['x', 'p', 'ffn1_norm', 'ffn1_w1', 'ffn1_w3', 'ffn1_w2', 'mix_norm', 'w_in', 'hgrn_lb', 'hgrn_onorm', 'w_branch_a', 'pool_w', 'pool_scale', 'w_branch_b', 'w_out', 'ffn2_norm', 'ffn2_w1', 'ffn2_w3', 'ffn2_w2', 'ple_norm', 'ple_w_gate', 'ple_w_proj', 'ple_post_norm', 'final_norm']
TWIN_WEIGHTS = ['ffn1_norm', 'ffn1_w1', 'ffn1_w3', 'ffn1_w2', 'mix_norm', 'w_in', 'hgrn_lb', 'hgrn_onorm', 'w_branch_a', 'pool_w', 'pool_scale', 'w_branch_b', 'w_out', 'ffn2_norm', 'ffn2_w1', 'ffn2_w3', 'ffn2_w2', 'ple_norm', 'ple_w_gate', 'ple_w_proj', 'ple_post_norm', 'final_norm']
TWIN_DIFF_INPUT = 'x'
TWIN_INPUTS = ['x', 'p', 'ffn1_norm', 'ffn1_w1', 'ffn1_w3', 'ffn1_w2', 'mix_norm', 'w_in', 'hgrn_lb', 'hgrn_onorm', 'w_branch_a', 'pool_w', 'pool_scale', 'w_branch_b', 'w_out', 'ffn2_norm', 'ffn2_w1', 'ffn2_w3', 'ffn2_w2', 'ple_norm', 'ple_w_gate', 'ple_w_proj', 'ple_post_norm', 'final_norm', 'loss_target', 'm_ffn1_norm', 'm_ffn1_w1', 'm_ffn1_w3', 'm_ffn1_w2', 'm_mix_norm', 'm_w_in', 'm_hgrn_lb', 'm_hgrn_onorm', 'm_w_branch_a', 'm_pool_w', 'm_pool_scale', 'm_w_branch_b', 'm_w_out', 'm_ffn2_norm', 'm_ffn2_w1', 'm_ffn2_w3', 'm_ffn2_w2', 'm_ple_norm', 'm_ple_w_gate', 'm_ple_w_proj', 'm_ple_post_norm', 'm_final_norm', 'v_ffn1_norm', 'v_ffn1_w1', 'v_ffn1_w3', 'v_ffn1_w2', 'v_mix_norm', 'v_w_in', 'v_hgrn_lb', 'v_hgrn_onorm', 'v_w_branch_a', 'v_pool_w', 'v_pool_scale', 'v_w_branch_b', 'v_w_out', 'v_ffn2_norm', 'v_ffn2_w1', 'v_ffn2_w3', 'v_ffn2_w2', 'v_ple_norm', 'v_ple_w_gate', 'v_ple_w_proj', 'v_ple_post_norm', 'v_final_norm']
TWIN_OUTPUTS = ['loss', 'grad_x', 'grad_ffn1_norm', 'grad_ffn1_w1', 'grad_ffn1_w3', 'grad_ffn1_w2', 'grad_mix_norm', 'grad_w_in', 'grad_hgrn_lb', 'grad_hgrn_onorm', 'grad_w_branch_a', 'grad_pool_w', 'grad_pool_scale', 'grad_w_branch_b', 'grad_w_out', 'grad_ffn2_norm', 'grad_ffn2_w1', 'grad_ffn2_w3', 'grad_ffn2_w2', 'grad_ple_norm', 'grad_ple_w_gate', 'grad_ple_w_proj', 'grad_ple_post_norm', 'grad_final_norm', 'delta_ffn1_norm', 'delta_ffn1_w1', 'delta_ffn1_w3', 'delta_ffn1_w2', 'delta_mix_norm', 'delta_w_in', 'delta_hgrn_lb', 'delta_hgrn_onorm', 'delta_w_branch_a', 'delta_pool_w', 'delta_pool_scale', 'delta_w_branch_b', 'delta_w_out', 'delta_ffn2_norm', 'delta_ffn2_w1', 'delta_ffn2_w3', 'delta_ffn2_w2', 'delta_ple_norm', 'delta_ple_w_gate', 'delta_ple_w_proj', 'delta_ple_post_norm', 'delta_final_norm', 'new_m_ffn1_norm', 'new_m_ffn1_w1', 'new_m_ffn1_w3', 'new_m_ffn1_w2', 'new_m_mix_norm', 'new_m_w_in', 'new_m_hgrn_lb', 'new_m_hgrn_onorm', 'new_m_w_branch_a', 'new_m_pool_w', 'new_m_pool_scale', 'new_m_w_branch_b', 'new_m_w_out', 'new_m_ffn2_norm', 'new_m_ffn2_w1', 'new_m_ffn2_w3', 'new_m_ffn2_w2', 'new_m_ple_norm', 'new_m_ple_w_gate', 'new_m_ple_w_proj', 'new_m_ple_post_norm', 'new_m_final_norm', 'new_v_ffn1_norm', 'new_v_ffn1_w1', 'new_v_ffn1_w3', 'new_v_ffn1_w2', 'new_v_mix_norm', 'new_v_w_in', 'new_v_hgrn_lb', 'new_v_hgrn_onorm', 'new_v_w_branch_a', 'new_v_pool_w', 'new_v_pool_scale', 'new_v_w_branch_b', 'new_v_w_out', 'new_v_ffn2_norm', 'new_v_ffn2_w1', 'new_v_ffn2_w3', 'new_v_ffn2_w2', 'new_v_ple_norm', 'new_v_ple_w_gate', 'new_v_ple_w_proj', 'new_v_ple_post_norm', 'new_v_final_norm']
TWIN_LEAF_KINDS = {'loss': 'loss', 'grad_x': 'grad_x', 'grad_ffn1_norm': 'grad_w', 'grad_ffn1_w1': 'grad_w', 'grad_ffn1_w3': 'grad_w', 'grad_ffn1_w2': 'grad_w', 'grad_mix_norm': 'grad_w', 'grad_w_in': 'grad_w', 'grad_hgrn_lb': 'grad_w', 'grad_hgrn_onorm': 'grad_w', 'grad_w_branch_a': 'grad_w', 'grad_pool_w': 'grad_w', 'grad_pool_scale': 'grad_w', 'grad_w_branch_b': 'grad_w', 'grad_w_out': 'grad_w', 'grad_ffn2_norm': 'grad_w', 'grad_ffn2_w1': 'grad_w', 'grad_ffn2_w3': 'grad_w', 'grad_ffn2_w2': 'grad_w', 'grad_ple_norm': 'grad_w', 'grad_ple_w_gate': 'grad_w', 'grad_ple_w_proj': 'grad_w', 'grad_ple_post_norm': 'grad_w', 'grad_final_norm': 'grad_w', 'delta_ffn1_norm': 'delta_w', 'delta_ffn1_w1': 'delta_w', 'delta_ffn1_w3': 'delta_w', 'delta_ffn1_w2': 'delta_w', 'delta_mix_norm': 'delta_w', 'delta_w_in': 'delta_w', 'delta_hgrn_lb': 'delta_w', 'delta_hgrn_onorm': 'delta_w', 'delta_w_branch_a': 'delta_w', 'delta_pool_w': 'delta_w', 'delta_pool_scale': 'delta_w', 'delta_w_branch_b': 'delta_w', 'delta_w_out': 'delta_w', 'delta_ffn2_norm': 'delta_w', 'delta_ffn2_w1': 'delta_w', 'delta_ffn2_w3': 'delta_w', 'delta_ffn2_w2': 'delta_w', 'delta_ple_norm': 'delta_w', 'delta_ple_w_gate': 'delta_w', 'delta_ple_w_proj': 'delta_w', 'delta_ple_post_norm': 'delta_w', 'delta_final_norm': 'delta_w', 'new_m_ffn1_norm': 'new_m', 'new_m_ffn1_w1': 'new_m', 'new_m_ffn1_w3': 'new_m', 'new_m_ffn1_w2': 'new_m', 'new_m_mix_norm': 'new_m', 'new_m_w_in': 'new_m', 'new_m_hgrn_lb': 'new_m', 'new_m_hgrn_onorm': 'new_m', 'new_m_w_branch_a': 'new_m', 'new_m_pool_w': 'new_m', 'new_m_pool_scale': 'new_m', 'new_m_w_branch_b': 'new_m', 'new_m_w_out': 'new_m', 'new_m_ffn2_norm': 'new_m', 'new_m_ffn2_w1': 'new_m', 'new_m_ffn2_w3': 'new_m', 'new_m_ffn2_w2': 'new_m', 'new_m_ple_norm': 'new_m', 'new_m_ple_w_gate': 'new_m', 'new_m_ple_w_proj': 'new_m', 'new_m_ple_post_norm': 'new_m', 'new_m_final_norm': 'new_m', 'new_v_ffn1_norm': 'new_v', 'new_v_ffn1_w1': 'new_v', 'new_v_ffn1_w3': 'new_v', 'new_v_ffn1_w2': 'new_v', 'new_v_mix_norm': 'new_v', 'new_v_w_in': 'new_v', 'new_v_hgrn_lb': 'new_v', 'new_v_hgrn_onorm': 'new_v', 'new_v_w_branch_a': 'new_v', 'new_v_pool_w': 'new_v', 'new_v_pool_scale': 'new_v', 'new_v_w_branch_b': 'new_v', 'new_v_w_out': 'new_v', 'new_v_ffn2_norm': 'new_v', 'new_v_ffn2_w1': 'new_v', 'new_v_ffn2_w3': 'new_v', 'new_v_ffn2_w2': 'new_v', 'new_v_ple_norm': 'new_v', 'new_v_ple_w_gate': 'new_v', 'new_v_ple_w_proj': 'new_v', 'new_v_ple_post_norm': 'new_v', 'new_v_final_norm': 'new_v'}


def _forward(args):
    return _fwd_reference(*[args[k] for k in FWD_PARAMS])


def _output_shape():
    def fwd():
        inp = _fwd_setup_inputs(0)
        return _fwd_reference(*[inp[k] for k in FWD_PARAMS])
    out = _jax.eval_shape(fwd)
    return out.shape, out.dtype

N_MICROBATCH = 1
ADAM_LR = 0.001
ADAM_B1 = 0.9
ADAM_B2 = 0.999
ADAM_EPS = 1e-08
ADAM_WD = 0.01
ADAM_STEP = 10
PER_EXAMPLE_BATCH_AXIS = {'x': 0, 'p': 1, 'loss_target': 0}
SHARED_INPUTS = []
_WEIGHT_DTYPES = {'ffn1_norm': _jnp.float32, 'ffn1_w1': _jnp.float32, 'ffn1_w3': _jnp.float32, 'ffn1_w2': _jnp.float32, 'mix_norm': _jnp.float32, 'w_in': _jnp.float32, 'hgrn_lb': _jnp.float32, 'hgrn_onorm': _jnp.float32, 'w_branch_a': _jnp.float32, 'pool_w': _jnp.float32, 'pool_scale': _jnp.float32, 'w_branch_b': _jnp.float32, 'w_out': _jnp.float32, 'ffn2_norm': _jnp.float32, 'ffn2_w1': _jnp.float32, 'ffn2_w3': _jnp.float32, 'ffn2_w2': _jnp.float32, 'ple_norm': _jnp.float32, 'ple_w_gate': _jnp.float32, 'ple_w_proj': _jnp.float32, 'ple_post_norm': _jnp.float32, 'final_norm': _jnp.float32}
MOMENT_SCALE = {'ffn1_norm': 1.076919e-01, 'ffn1_w1': 4.505063e-02, 'ffn1_w3': 4.368950e-02, 'ffn1_w2': 7.233202e-02, 'mix_norm': 1.437389e-01, 'w_in': 5.555853e-02, 'hgrn_lb': 6.635217e-03, 'hgrn_onorm': 6.751074e-02, 'w_branch_a': 6.509210e-02, 'pool_w': 1.353770e-01, 'pool_scale': 1.380025e-01, 'w_branch_b': 9.536175e-02, 'w_out': 1.161344e-01, 'ffn2_norm': 8.654268e-02, 'ffn2_w1': 3.520898e-02, 'ffn2_w3': 3.409844e-02, 'ffn2_w2': 5.650231e-02, 'ple_norm': 3.938302e-02, 'ple_w_gate': 3.947896e-02, 'ple_w_proj': 1.021029e-01, 'ple_post_norm': 2.895676e-01, 'final_norm': 6.391258e+01}


def _to_microbatches(a, axis):
    t = _jnp.moveaxis(a, axis, 0)
    t = t.reshape((N_MICROBATCH, t.shape[0] // N_MICROBATCH) + t.shape[1:])
    return _jnp.moveaxis(t, 1, axis + 1)


def setup_inputs(seed: int = 0) -> dict:
    inp = _fwd_setup_inputs(seed)
    key = _jax.random.fold_in(_jax.random.key(seed), 7919)
    shape, _ = _output_shape()
    out = dict(inp)
    out["loss_target"] = _jax.random.normal(_jax.random.fold_in(key, 0), shape, _jnp.float32)
    for i, name in enumerate(TWIN_WEIGHTS):
        w = inp[name].astype(_jnp.float32)
        if MOMENT_SCALE is None:
            s = _jnp.sqrt(_jnp.mean(_jnp.square(w)) + 1e-30)
        else:
            s = MOMENT_SCALE[name]
        km, kv = _jax.random.split(_jax.random.fold_in(key, i + 1))
        out[name] = w
        out["m_" + name] = s * _jax.random.normal(km, w.shape, _jnp.float32)
        out["v_" + name] = (s * s) * _jax.random.uniform(kv, w.shape, _jnp.float32, 0.5, 1.5)
    if N_MICROBATCH > 1:
        for name, axis in PER_EXAMPLE_BATCH_AXIS.items():
            out[name] = _to_microbatches(out[name], axis)
    return {'x': out['x'], 'p': out['p'], 'ffn1_norm': out['ffn1_norm'], 'ffn1_w1': out['ffn1_w1'], 'ffn1_w3': out['ffn1_w3'], 'ffn1_w2': out['ffn1_w2'], 'mix_norm': out['mix_norm'], 'w_in': out['w_in'], 'hgrn_lb': out['hgrn_lb'], 'hgrn_onorm': out['hgrn_onorm'], 'w_branch_a': out['w_branch_a'], 'pool_w': out['pool_w'], 'pool_scale': out['pool_scale'], 'w_branch_b': out['w_branch_b'], 'w_out': out['w_out'], 'ffn2_norm': out['ffn2_norm'], 'ffn2_w1': out['ffn2_w1'], 'ffn2_w3': out['ffn2_w3'], 'ffn2_w2': out['ffn2_w2'], 'ple_norm': out['ple_norm'], 'ple_w_gate': out['ple_w_gate'], 'ple_w_proj': out['ple_w_proj'], 'ple_post_norm': out['ple_post_norm'], 'final_norm': out['final_norm'], 'loss_target': out['loss_target'], 'm_ffn1_norm': out['m_ffn1_norm'], 'm_ffn1_w1': out['m_ffn1_w1'], 'm_ffn1_w3': out['m_ffn1_w3'], 'm_ffn1_w2': out['m_ffn1_w2'], 'm_mix_norm': out['m_mix_norm'], 'm_w_in': out['m_w_in'], 'm_hgrn_lb': out['m_hgrn_lb'], 'm_hgrn_onorm': out['m_hgrn_onorm'], 'm_w_branch_a': out['m_w_branch_a'], 'm_pool_w': out['m_pool_w'], 'm_pool_scale': out['m_pool_scale'], 'm_w_branch_b': out['m_w_branch_b'], 'm_w_out': out['m_w_out'], 'm_ffn2_norm': out['m_ffn2_norm'], 'm_ffn2_w1': out['m_ffn2_w1'], 'm_ffn2_w3': out['m_ffn2_w3'], 'm_ffn2_w2': out['m_ffn2_w2'], 'm_ple_norm': out['m_ple_norm'], 'm_ple_w_gate': out['m_ple_w_gate'], 'm_ple_w_proj': out['m_ple_w_proj'], 'm_ple_post_norm': out['m_ple_post_norm'], 'm_final_norm': out['m_final_norm'], 'v_ffn1_norm': out['v_ffn1_norm'], 'v_ffn1_w1': out['v_ffn1_w1'], 'v_ffn1_w3': out['v_ffn1_w3'], 'v_ffn1_w2': out['v_ffn1_w2'], 'v_mix_norm': out['v_mix_norm'], 'v_w_in': out['v_w_in'], 'v_hgrn_lb': out['v_hgrn_lb'], 'v_hgrn_onorm': out['v_hgrn_onorm'], 'v_w_branch_a': out['v_w_branch_a'], 'v_pool_w': out['v_pool_w'], 'v_pool_scale': out['v_pool_scale'], 'v_w_branch_b': out['v_w_branch_b'], 'v_w_out': out['v_w_out'], 'v_ffn2_norm': out['v_ffn2_norm'], 'v_ffn2_w1': out['v_ffn2_w1'], 'v_ffn2_w3': out['v_ffn2_w3'], 'v_ffn2_w2': out['v_ffn2_w2'], 'v_ple_norm': out['v_ple_norm'], 'v_ple_w_gate': out['v_ple_w_gate'], 'v_ple_w_proj': out['v_ple_w_proj'], 'v_ple_post_norm': out['v_ple_post_norm'], 'v_final_norm': out['v_final_norm']}


def _loss(weights, diff, rest, loss_target):
    with _jax.named_scope("forward"):
        args = {**rest, TWIN_DIFF_INPUT: diff, **{k: w.astype(_WEIGHT_DTYPES[k]) for k, w in weights.items()}}
        y = _forward(args)
    with _jax.named_scope("loss_head"):
        err = _jnp.square(y.astype(_jnp.float32) - loss_target)
        return 0.5 * _jnp.sum(_jnp.mean(err, axis=-1)) if err.ndim else 0.5 * err


def _adamw(w, g, m, v):
    m = ADAM_B1 * m + (1.0 - ADAM_B1) * g
    v = ADAM_B2 * v + (1.0 - ADAM_B2) * _jnp.square(g)
    m_hat = m / (1.0 - ADAM_B1 ** ADAM_STEP)
    v_hat = v / (1.0 - ADAM_B2 ** ADAM_STEP)
    delta = -ADAM_LR * (m_hat / (_jnp.sqrt(v_hat) + ADAM_EPS) + ADAM_WD * w)
    return delta, m, v


def reference(x, p, ffn1_norm, ffn1_w1, ffn1_w3, ffn1_w2, mix_norm, w_in, hgrn_lb, hgrn_onorm, w_branch_a, pool_w, pool_scale, w_branch_b, w_out, ffn2_norm, ffn2_w1, ffn2_w3, ffn2_w2, ple_norm, ple_w_gate, ple_w_proj, ple_post_norm, final_norm, loss_target, m_ffn1_norm, m_ffn1_w1, m_ffn1_w3, m_ffn1_w2, m_mix_norm, m_w_in, m_hgrn_lb, m_hgrn_onorm, m_w_branch_a, m_pool_w, m_pool_scale, m_w_branch_b, m_w_out, m_ffn2_norm, m_ffn2_w1, m_ffn2_w3, m_ffn2_w2, m_ple_norm, m_ple_w_gate, m_ple_w_proj, m_ple_post_norm, m_final_norm, v_ffn1_norm, v_ffn1_w1, v_ffn1_w3, v_ffn1_w2, v_mix_norm, v_w_in, v_hgrn_lb, v_hgrn_onorm, v_w_branch_a, v_pool_w, v_pool_scale, v_w_branch_b, v_w_out, v_ffn2_norm, v_ffn2_w1, v_ffn2_w3, v_ffn2_w2, v_ple_norm, v_ple_w_gate, v_ple_w_proj, v_ple_post_norm, v_final_norm):
    given = dict(x=x, p=p, ffn1_norm=ffn1_norm, ffn1_w1=ffn1_w1, ffn1_w3=ffn1_w3, ffn1_w2=ffn1_w2, mix_norm=mix_norm, w_in=w_in, hgrn_lb=hgrn_lb, hgrn_onorm=hgrn_onorm, w_branch_a=w_branch_a, pool_w=pool_w, pool_scale=pool_scale, w_branch_b=w_branch_b, w_out=w_out, ffn2_norm=ffn2_norm, ffn2_w1=ffn2_w1, ffn2_w3=ffn2_w3, ffn2_w2=ffn2_w2, ple_norm=ple_norm, ple_w_gate=ple_w_gate, ple_w_proj=ple_w_proj, ple_post_norm=ple_post_norm, final_norm=final_norm, loss_target=loss_target, m_ffn1_norm=m_ffn1_norm, m_ffn1_w1=m_ffn1_w1, m_ffn1_w3=m_ffn1_w3, m_ffn1_w2=m_ffn1_w2, m_mix_norm=m_mix_norm, m_w_in=m_w_in, m_hgrn_lb=m_hgrn_lb, m_hgrn_onorm=m_hgrn_onorm, m_w_branch_a=m_w_branch_a, m_pool_w=m_pool_w, m_pool_scale=m_pool_scale, m_w_branch_b=m_w_branch_b, m_w_out=m_w_out, m_ffn2_norm=m_ffn2_norm, m_ffn2_w1=m_ffn2_w1, m_ffn2_w3=m_ffn2_w3, m_ffn2_w2=m_ffn2_w2, m_ple_norm=m_ple_norm, m_ple_w_gate=m_ple_w_gate, m_ple_w_proj=m_ple_w_proj, m_ple_post_norm=m_ple_post_norm, m_final_norm=m_final_norm, v_ffn1_norm=v_ffn1_norm, v_ffn1_w1=v_ffn1_w1, v_ffn1_w3=v_ffn1_w3, v_ffn1_w2=v_ffn1_w2, v_mix_norm=v_mix_norm, v_w_in=v_w_in, v_hgrn_lb=v_hgrn_lb, v_hgrn_onorm=v_hgrn_onorm, v_w_branch_a=v_w_branch_a, v_pool_w=v_pool_w, v_pool_scale=v_pool_scale, v_w_branch_b=v_w_branch_b, v_w_out=v_w_out, v_ffn2_norm=v_ffn2_norm, v_ffn2_w1=v_ffn2_w1, v_ffn2_w3=v_ffn2_w3, v_ffn2_w2=v_ffn2_w2, v_ple_norm=v_ple_norm, v_ple_w_gate=v_ple_w_gate, v_ple_w_proj=v_ple_w_proj, v_ple_post_norm=v_ple_post_norm, v_final_norm=v_final_norm)
    weights = {n: given[n] for n in TWIN_WEIGHTS}
    shared = {n: given[n] for n in SHARED_INPUTS}
    per_example = {n: given[n] for n in ['x', 'p']}
    grad_fn = _jax.value_and_grad(_loss, argnums=(0, 1))

    def one_microbatch(ex, loss_target):
        ex = dict(ex)
        diff = ex.pop(TWIN_DIFF_INPUT)
        return grad_fn(weights, diff, {**shared, **ex}, loss_target)

    if N_MICROBATCH == 1:
        loss, (grad_w, grad_x) = one_microbatch(per_example, given["loss_target"])
    else:
        def body(carry, xs):
            loss_sum, grad_sum = carry
            l_k, (gw_k, gx_k) = one_microbatch(xs[0], xs[1])
            with _jax.named_scope("update"):
                return (loss_sum + l_k, _jax.tree.map(_jnp.add, grad_sum, gw_k)), gx_k

        init = (_jnp.zeros((), _jnp.float32), _jax.tree.map(_jnp.zeros_like, weights))
        (loss, grad_w), grad_x = _jax.lax.scan(body, init, (per_example, given["loss_target"]))
    with _jax.named_scope("update"):
        delta_w, new_m, new_v = {}, {}, {}
        for n in TWIN_WEIGHTS:
            delta_w[n], new_m[n], new_v[n] = _adamw(weights[n], grad_w[n], given["m_" + n], given["v_" + n])
    return (loss, grad_x, *[grad_w[n] for n in TWIN_WEIGHTS], *[delta_w[n] for n in TWIN_WEIGHTS],
            *[new_m[n] for n in TWIN_WEIGHTS], *[new_v[n] for n in TWIN_WEIGHTS])
```

```python
import jax
import jax.numpy as jnp
from jax import lax
from jax.experimental import pallas as pl
from jax.experimental.pallas import tpu as pltpu

F32 = jnp.float32
BF16 = jnp.bfloat16

N_DEV = 8
D_MODEL = 1024
HEADS = 8
HEAD_DIM = 128
POOL_WINDOWS = (2, 4, 8, 16)
POOL_CH = 128
POOL_WIDTH = 512
POOL_HALO = 16
RMS_EPS = 1e-6
CHUNK = 64
SUB = 16
NEG_BIG = -1e30

ADAM_LR = 0.001
ADAM_B1 = 0.9
ADAM_B2 = 0.999
ADAM_EPS = 1e-08
ADAM_WD = 0.01
ADAM_STEP = 10

V7X_VMEM_BYTES = 64 * 1024 * 1024
VMEM_LIMIT = (V7X_VMEM_BYTES * 3) // 4
ROW_TILE_CAP = 8192

COL_Q, COL_F, COL_I, COL_OG, COL_POOL, COL_GA, COL_GB = 0, 1024, 2048, 3072, 4096, 4608, 5632

BIG_WEIGHTS = (
    ("ffn1_w1", 1), ("ffn1_w3", 1), ("ffn1_w2", 0), ("w_in", 1), ("w_branch_a", 0), ("w_branch_b", 1),
    ("w_out", 0), ("ffn2_w1", 1), ("ffn2_w3", 1), ("ffn2_w2", 0), ("ple_w_gate", 0), ("ple_w_proj", 1),
)
SMALL_ROWS = {
    "ffn1_norm": (0, 8), "mix_norm": (8, 8), "hgrn_onorm": (16, 8), "ffn2_norm": (24, 8), "ple_norm": (32, 8),
    "ple_post_norm": (40, 8), "final_norm": (48, 8), "hgrn_lb": (56, 16), "pool_scale": (72, 4), "pool_w": (80, 512),
}
SMALL_TOTAL_ROWS = 592
WEIGHT_ORDER = (
    "ffn1_norm", "ffn1_w1", "ffn1_w3", "ffn1_w2", "mix_norm", "w_in", "hgrn_lb", "hgrn_onorm", "w_branch_a", "pool_w",
    "pool_scale", "w_branch_b", "w_out", "ffn2_norm", "ffn2_w1", "ffn2_w3", "ffn2_w2", "ple_norm", "ple_w_gate",
    "ple_w_proj", "ple_post_norm", "final_norm",
)


def _params(*sem):
    return pltpu.CompilerParams(dimension_semantics=sem if sem else None, vmem_limit_bytes=VMEM_LIMIT)


def _dot(a, b):
    return jnp.dot(a, b, preferred_element_type=F32)


def _dot_nt(a, b):
    return lax.dot_general(a, b, (((1,), (1,)), ((), ())), preferred_element_type=F32)


def _dot_tn(a, b):
    return lax.dot_general(a, b, (((0,), (0,)), ((), ())), preferred_element_type=F32)


def _sigmoid(x):
    return 1.0 / (1.0 + jnp.exp(-x))


def _tile(n, want, mult):
    if mult != 128:
        want = min(want, ROW_TILE_CAP)
    if n <= want:
        return n
    t = (want // mult) * mult
    while t > mult and n % t:
        t -= mult
    assert n % t == 0, (n, want, mult)
    return t


def _exchange(arrs, *, name, bcast):
    n = len(arrs)
    out_shape = [jax.ShapeDtypeStruct((N_DEV,) + (a.shape if bcast else a.shape[1:]), a.dtype) for a in arrs]

    def body(*refs):
        ins, outs = refs[:n], refs[n:2 * n]
        send_sems, recv_sems, local_sems = refs[2 * n:]
        x, y, c = lax.axis_index("x"), lax.axis_index("y"), lax.axis_index("c")
        me = 4 * x + 2 * y + c
        local = []
        for a in range(n):
            mine = ins[a] if bcast else ins[a].at[me]
            cp = pltpu.make_async_copy(mine, outs[a].at[me], local_sems.at[a])
            cp.start()
            local.append(cp)
        sends, recvs = [], []
        for k in range(1, N_DEV):
            px = 1 - x if k & 4 else x
            py = 1 - y if k & 2 else y
            pc = 1 - c if k & 1 else c
            peer = 4 * px + 2 * py + pc
            for a in range(n):
                s = a * (N_DEV - 1) + k - 1
                src = ins[a] if bcast else ins[a].at[peer]
                cp = pltpu.make_async_remote_copy(
                    src_ref=src, dst_ref=outs[a].at[me], send_sem=send_sems.at[s], recv_sem=recv_sems.at[s],
                    device_id=(px, py, pc), device_id_type=pl.DeviceIdType.MESH)
                cp.start()
                sends.append(cp)
                recvs.append(pltpu.make_async_remote_copy(
                    src_ref=src, dst_ref=outs[a].at[peer], send_sem=send_sems.at[s], recv_sem=recv_sems.at[s],
                    device_id=(px, py, pc), device_id_type=pl.DeviceIdType.MESH))
        for cp in recvs:
            cp.wait_recv()
        for cp in sends:
            cp.wait_send()
        for cp in local:
            cp.wait()

    hbm = pl.BlockSpec(memory_space=pltpu.HBM)
    return pl.pallas_call(
        body, name=name, out_shape=out_shape, in_specs=[hbm] * n, out_specs=[hbm] * n,
        scratch_shapes=[pltpu.SemaphoreType.DMA((n * (N_DEV - 1),)), pltpu.SemaphoreType.DMA((n * (N_DEV - 1),)),
                        pltpu.SemaphoreType.DMA((n,))],
    )(*arrs)


def _mm_nn(a, b, *, name, tn, tm, out_dtype=F32, res=None, scale=1.0):
    n, k = a.shape
    m = b.shape[1]
    tn, tm = _tile(n, tn, 16), _tile(m, tm, 128)

    def body(*refs):
        a_ref, b_ref = refs[0], refs[1]
        o_ref = refs[-1]
        acc = _dot(a_ref[...], b_ref[...])
        if scale != 1.0:
            acc = acc * scale
        if res is not None:
            acc = acc + refs[2][...]
        o_ref[...] = acc.astype(o_ref.dtype)

    in_specs = [pl.BlockSpec((tn, k), lambda i, j: (i, 0)), pl.BlockSpec((k, tm), lambda i, j: (0, j))]
    args = [a, b]
    if res is not None:
        in_specs.append(pl.BlockSpec((tn, tm), lambda i, j: (i, j)))
        args.append(res)
    return pl.pallas_call(
        body, name=name, grid=(n // tn, m // tm), in_specs=in_specs,
        out_specs=pl.BlockSpec((tn, tm), lambda i, j: (i, j)),
        out_shape=jax.ShapeDtypeStruct((n, m), out_dtype), compiler_params=_params("parallel", "parallel"),
    )(*args)


def _mm_nt(pairs, *, name, tn, tk, out_dtype=F32):
    n = pairs[0][0].shape[0]
    kk = pairs[0][1].shape[0]
    tn, tk = _tile(n, tn, 16), _tile(kk, tk, 128)
    npair = len(pairs)

    def body(*refs):
        o_ref = refs[-1]
        acc = _dot_nt(refs[0][...], refs[1][...])
        for q in range(1, npair):
            acc = acc + _dot_nt(refs[2 * q][...], refs[2 * q + 1][...])
        o_ref[...] = acc.astype(o_ref.dtype)

    in_specs, args = [], []
    for a, b in pairs:
        m = a.shape[1]
        in_specs += [pl.BlockSpec((tn, m), lambda i, j: (i, 0)), pl.BlockSpec((tk, m), lambda i, j: (j, 0))]
        args += [a, b]
    return pl.pallas_call(
        body, name=name, grid=(n // tn, kk // tk), in_specs=in_specs,
        out_specs=pl.BlockSpec((tn, tk), lambda i, j: (i, j)),
        out_shape=jax.ShapeDtypeStruct((n, kk), out_dtype), compiler_params=_params("parallel", "parallel"),
    )(*args)


def _mm_tn(a, b, *, name, tn, tm):
    n, k = a.shape
    m = b.shape[1]
    tn, tm = _tile(n, tn, 16), _tile(m, tm, 128)

    def body(a_ref, b_ref, o_ref):
        @pl.when(pl.program_id(1) == 0)
        def _():
            o_ref[...] = jnp.zeros_like(o_ref)

        o_ref[...] += _dot_tn(a_ref[...], b_ref[...])

    return pl.pallas_call(
        body, name=name, grid=(m // tm, n // tn),
        in_specs=[pl.BlockSpec((tn, k), lambda j, i: (i, 0)), pl.BlockSpec((tn, tm), lambda j, i: (i, j))],
        out_specs=pl.BlockSpec((k, tm), lambda j, i: (0, j)),
        out_shape=jax.ShapeDtypeStruct((k, m), F32), compiler_params=_params("parallel", "arbitrary"),
    )(a, b)


def _ffn_up(h, w1, w3, *, name):
    n, k = h.shape
    m = w1.shape[1]
    tn, tm = _tile(n, 1024, 16), _tile(m, 256, 128)

    def body(h_ref, w1_ref, w3_ref, a_ref, b_ref, s_ref):
        hb = h_ref[...]
        a = _dot(hb, w1_ref[...])
        b = _dot(hb, w3_ref[...])
        a_ref[...] = a
        b_ref[...] = b
        s_ref[...] = (a * _sigmoid(a) * b).astype(s_ref.dtype)

    wspec = pl.BlockSpec((k, tm), lambda i, j: (0, j))
    ospec = pl.BlockSpec((tn, tm), lambda i, j: (i, j))
    return pl.pallas_call(
        body, name=name, grid=(n // tn, m // tm),
        in_specs=[pl.BlockSpec((tn, k), lambda i, j: (i, 0)), wspec, wspec], out_specs=[ospec, ospec, ospec],
        out_shape=[jax.ShapeDtypeStruct((n, m), F32), jax.ShapeDtypeStruct((n, m), F32),
                   jax.ShapeDtypeStruct((n, m), BF16)],
        compiler_params=_params("parallel", "parallel"),
    )(h, w1, w3)


def _ffn_bwd_mid(dxs, w2, a, b, *, name):
    n, d = dxs.shape
    m = w2.shape[0]
    tn, tk = _tile(n, 1024, 16), _tile(m, 256, 128)

    def body(dx_ref, w2_ref, a_ref, b_ref, da_ref, db_ref):
        ds = _dot_nt(dx_ref[...], w2_ref[...])
        a_, b_ = a_ref[...], b_ref[...]
        sg = _sigmoid(a_)
        da_ref[...] = (ds * b_ * (sg * (1.0 + a_ * (1.0 - sg)))).astype(da_ref.dtype)
        db_ref[...] = (ds * (a_ * sg)).astype(db_ref.dtype)

    tile = pl.BlockSpec((tn, tk), lambda i, j: (i, j))
    return pl.pallas_call(
        body, name=name, grid=(n // tn, m // tk),
        in_specs=[pl.BlockSpec((tn, d), lambda i, j: (i, 0)), pl.BlockSpec((tk, d), lambda i, j: (j, 0)), tile, tile],
        out_specs=[tile, tile],
        out_shape=[jax.ShapeDtypeStruct((n, m), BF16), jax.ShapeDtypeStruct((n, m), BF16)],
        compiler_params=_params("parallel", "parallel"),
    )(dxs, w2, a, b)


def _rowwise(fn, *, name, n, tn, ncol, rows, vecs, outs, accs=()):
    tn = _tile(n, tn, 16)
    nr, nv, no = len(rows), len(vecs), len(outs)

    def body(*refs):
        first = pl.program_id(1) == 0
        vals = [r[...] for r in refs[:nr + nv]]
        res = fn(*vals)
        for ref, val in zip(refs[nr + nv:nr + nv + no], res[:no]):
            ref[...] = val.astype(ref.dtype)
        for ref, val in zip(refs[nr + nv + no:], res[no:]):
            _accumulate(ref, val, first)

    in_specs = [pl.BlockSpec((tn, w), lambda j, i, c0=c0: (i, c0 + j)) for _, w, c0 in rows]
    in_specs += [pl.BlockSpec((1, w), lambda j, i, c0=c0: (0, c0 + j)) for _, w, c0 in vecs]
    out_specs = [pl.BlockSpec((tn, w), lambda j, i: (i, j)) for _, w, _ in outs]
    out_specs += [pl.BlockSpec((1, w), lambda j, i: (0, j)) for _, w in accs]
    out_shape = [jax.ShapeDtypeStruct((n, tw), dt) for tw, _, dt in outs]
    out_shape += [jax.ShapeDtypeStruct((1, tw), F32) for tw, _ in accs]
    return pl.pallas_call(
        body, name=name, grid=(ncol, n // tn), in_specs=in_specs, out_specs=out_specs, out_shape=out_shape,
        compiler_params=_params("parallel", "arbitrary"),
    )(*[r[0] for r in rows], *[v[0] for v in vecs])


def _accumulate(ref, val, first):
    @pl.when(first)
    def _():
        ref[...] = jnp.zeros_like(ref)

    ref[...] += val


def _colsum(x):
    return jnp.sum(x, axis=0, keepdims=True)


def _rowmean(x):
    return jnp.mean(x, axis=-1, keepdims=True)


def _rms_fwd(x, g, *, name):
    def fn(x_, g_):
        r = lax.rsqrt(_rowmean(x_ * x_) + RMS_EPS)
        return (x_ * r * g_,)

    n, d = x.shape
    return _rowwise(fn, name=name, n=n, tn=512, ncol=1, rows=[(x, d, 0)], vecs=[(g, d, 0)], outs=[(d, d, BF16)])[0]


def _rms_bwd_add(dh, x, extra, g, *, name, half_scale):
    def fn(dh_, x_, e_, g_):
        r = lax.rsqrt(_rowmean(x_ * x_) + RMS_EPS)
        xh = x_ * r
        dxh = dh_ * g_
        dx = e_ + r * (dxh - xh * _rowmean(dxh * xh))
        return dx, dx * half_scale, _colsum(dh_ * xh)

    n, d = x.shape
    return _rowwise(fn, name=name, n=n, tn=256, ncol=1, rows=[(dh, d, 0), (x, d, 0), (extra, d, 0)],
                    vecs=[(g, d, 0)], outs=[(d, d, F32), (d, d, BF16)], accs=[(d, d)])


def _ple_final(x3, gpre, z, tgt, gpp, gf, *, name):
    def fn(x3_, gpre_, z_, tgt_, gpp_, gf_):
        gate = _sigmoid(gpre_)
        rz = lax.rsqrt(_rowmean(z_ * z_) + RMS_EPS)
        zh = z_ * rz
        e = zh * gpp_
        x4 = x3_ + gate * e
        r4 = lax.rsqrt(_rowmean(x4 * x4) + RMS_EPS)
        x4h = x4 * r4
        diff = x4h * gf_ - tgt_
        dout = diff * (1.0 / D_MODEL)
        dxh4 = dout * gf_
        dx4 = r4 * (dxh4 - x4h * _rowmean(dxh4 * x4h))
        dpre = dx4 * e * gate * (1.0 - gate)
        de = dx4 * gate
        dzh = de * gpp_
        dz = rz * (dzh - zh * _rowmean(dzh * zh))
        return dx4, dpre, dz, _colsum(diff * diff) * (0.5 / D_MODEL), _colsum(dout * x4h), _colsum(de * zh)

    n, d = x3.shape
    return _rowwise(fn, name=name, n=n, tn=256, ncol=1, rows=[(x3, d, 0), (gpre, d, 0), (z, d, 0), (tgt, d, 0)],
                    vecs=[(gpp, d, 0), (gf, d, 0)], outs=[(d, d, F32), (d, d, BF16), (d, d, BF16)],
                    accs=[(d, d), (d, d), (d, d)])


def _merge_fwd(proj, ya, yb, *, name):
    def fn(ga, gb, ya_, yb_):
        return (_sigmoid(ga) * ya_ + _sigmoid(gb) * yb_,)

    n = proj.shape[0]
    w = 512
    return _rowwise(fn, name=name, n=n, tn=512, ncol=D_MODEL // w,
                    rows=[(proj, w, COL_GA // w), (proj, w, COL_GB // w), (ya, w, 0), (yb, w, 0)], vecs=[],
                    outs=[(D_MODEL, w, BF16)])[0]


def _merge_bwd(dy, proj, ya, yb, *, name):
    def fn(dy_, ga, gb, ya_, yb_):
        sa, sb = _sigmoid(ga), _sigmoid(gb)
        return dy_ * sa, dy_ * sb, dy_ * ya_ * sa * (1.0 - sa), dy_ * yb_ * sb * (1.0 - sb)

    n = proj.shape[0]
    w = 512
    return _rowwise(fn, name=name, n=n, tn=512, ncol=D_MODEL // w,
                    rows=[(dy, w, 0), (proj, w, COL_GA // w), (proj, w, COL_GB // w), (ya, w, 0), (yb, w, 0)],
                    vecs=[], outs=[(D_MODEL, w, BF16)] * 4)


def _hgrn_post_fwd(o, proj, onorm, *, name):
    def fn(o_, og, gam):
        r = lax.rsqrt(_rowmean(o_ * o_) + RMS_EPS)
        return (o_ * r * gam * (og * _sigmoid(og)),)

    n = o.shape[0]
    w = HEAD_DIM
    return _rowwise(fn, name=name, n=n, tn=1024, ncol=HEADS, rows=[(o, w, 0), (proj, w, COL_OG // w)],
                    vecs=[(onorm, w, 0)], outs=[(D_MODEL, w, BF16)])[0]


def _hgrn_post_bwd(don, o, proj, onorm, *, name):
    def fn(don_, o_, og, gam):
        r = lax.rsqrt(_rowmean(o_ * o_) + RMS_EPS)
        oh = o_ * r
        sg = _sigmoid(og)
        dog = don_ * oh * gam * (sg * (1.0 + og * (1.0 - sg)))
        dn = don_ * (og * sg)
        doh = dn * gam
        do = r * (doh - oh * _rowmean(doh * oh))
        return dog, do, _colsum(dn * oh)

    n = o.shape[0]
    w = HEAD_DIM
    return _rowwise(fn, name=name, n=n, tn=1024, ncol=HEADS, rows=[(don, w, 0), (o, w, 0), (proj, w, COL_OG // w)],
                    vecs=[(onorm, w, 0)], outs=[(D_MODEL, w, BF16), (D_MODEL, w, F32)], accs=[(D_MODEL, w)])


def _split3(x):
    hi = x.astype(BF16)
    r1 = x - hi.astype(F32)
    mid = r1.astype(BF16)
    lo = (r1 - mid.astype(F32)).astype(BF16)
    return hi, mid, lo


def _tri_sum(tri, x):
    hi, mid, lo = _split3(x)
    return _dot(tri, hi) + _dot(tri, mid) + _dot(tri, lo)


def _lower_bound(lb_ref):
    return 1.0 / (1.0 + jnp.exp(lb_ref[1:2, :] - lb_ref[0:1, :]))


def _hgrn_specs(n, t, reverse):
    nt = n // t
    blk = (lambda h, i: (nt - 1 - i, h)) if reverse else (lambda h, i: (i, h))

    def sec(col):
        c0 = col // HEAD_DIM
        if reverse:
            return pl.BlockSpec((t, HEAD_DIM), lambda h, i: (nt - 1 - i, c0 + h))
        return pl.BlockSpec((t, HEAD_DIM), lambda h, i: (i, c0 + h))

    head_tile = pl.BlockSpec((t, HEAD_DIM), blk)
    if reverse:
        state = pl.BlockSpec((1, t // CHUNK, HEAD_DIM, HEAD_DIM), lambda h, i: (h, nt - 1 - i, 0, 0))
    else:
        state = pl.BlockSpec((1, t // CHUNK, HEAD_DIM, HEAD_DIM), lambda h, i: (h, i, 0, 0))
    lb = pl.BlockSpec((2, HEAD_DIM), lambda h, i: (0, h))
    return sec, head_tile, state, lb


def _hgrn_fwd(proj, hgrn_lb, *, name):
    n = proj.shape[0]
    t = _tile(n, 512, CHUNK)
    nc = t // CHUNK
    sec, head_tile, state, lbspec = _hgrn_specs(n, t, False)

    def body(q_ref, f_ref, i_ref, lb_ref, o_ref, st_ref, s_acc, g_s, a_s):
        @pl.when(pl.program_id(1) == 0)
        def _():
            s_acc[...] = jnp.zeros_like(s_acc)

        lb = _lower_bound(lb_ref)
        row = lax.broadcasted_iota(jnp.int32, (CHUNK, CHUNK), 0)
        col = lax.broadcasted_iota(jnp.int32, (CHUNK, CHUNK), 1)
        tril = row >= col
        trilb = jnp.where(tril, 1.0, 0.0).astype(BF16)
        rowk = lax.broadcasted_iota(jnp.int32, (CHUNK, HEAD_DIM), 0)

        def chunk(c, carry):
            rows = pl.ds(pl.multiple_of(c * CHUNK, CHUNK), CHUNK)
            qr, fr, v = q_ref[rows, :], f_ref[rows, :], i_ref[rows, :]
            q = qr * _sigmoid(qr)
            f = lb + (1.0 - lb) * _sigmoid(fr)
            k = 1.0 - f
            g = _tri_sum(trilb, jnp.log(f))
            g_s[...] = g
            st0 = s_acc[...]
            st_ref[0, c] = st0
            vb = v.astype(BF16)
            for blk in range(CHUNK // SUB):
                lo, hi = blk * SUB, (blk + 1) * SUB
                gref = g_s[lo - 1:lo, :] if blk else jnp.zeros((1, HEAD_DIM), F32)
                qi = (q[lo:hi] * jnp.exp(g[lo:hi] - gref)).astype(BF16)
                ki = (k * jnp.exp(jnp.where(rowk < hi, gref - g, NEG_BIG))).astype(BF16)
                a_s[lo:hi, :] = _dot_nt(qi, ki)
            a = jnp.where(tril, a_s[...], 0.0).astype(BF16)
            o_ref[rows, :] = _dot(a, vb) + _dot_nt((q * jnp.exp(g)).astype(BF16), st0.astype(BF16))
            glast = g_s[CHUNK - 1:CHUNK, :]
            kdec = (k * jnp.exp(glast - g)).astype(BF16)
            s_acc[...] = st0 * jnp.exp(glast) + _dot_tn(vb, kdec)
            return carry

        lax.fori_loop(0, nc, chunk, 0)

    return pl.pallas_call(
        body, name=name, grid=(HEADS, n // t),
        in_specs=[sec(COL_Q), sec(COL_F), sec(COL_I), lbspec], out_specs=[head_tile, state],
        out_shape=[jax.ShapeDtypeStruct((n, D_MODEL), F32),
                   jax.ShapeDtypeStruct((HEADS, n // CHUNK, HEAD_DIM, HEAD_DIM), F32)],
        scratch_shapes=[pltpu.VMEM((HEAD_DIM, HEAD_DIM), F32), pltpu.VMEM((CHUNK, HEAD_DIM), F32),
                        pltpu.VMEM((CHUNK, CHUNK), F32)],
        compiler_params=_params("parallel", "arbitrary"),
    )(proj, proj, proj, hgrn_lb)


def _hgrn_bwd(proj, hgrn_lb, do, states, *, name):
    n = proj.shape[0]
    t = _tile(n, 512, CHUNK)
    nc = t // CHUNK
    sec, head_tile, state, lbspec = _hgrn_specs(n, t, True)

    def body(q_ref, f_ref, i_ref, lb_ref, do_ref, st_ref, dq_ref, df_ref, di_ref, dlb_ref, d_acc, g_s, a_s, dq_s,
             dg_s):
        first = pl.program_id(1) == 0

        @pl.when(first)
        def _():
            d_acc[...] = jnp.zeros_like(d_acc)

        lb = _lower_bound(lb_ref)
        row = lax.broadcasted_iota(jnp.int32, (CHUNK, CHUNK), 0)
        col = lax.broadcasted_iota(jnp.int32, (CHUNK, CHUNK), 1)
        tril = row >= col
        trilb = jnp.where(tril, 1.0, 0.0).astype(BF16)
        triub = jnp.where(row <= col, 1.0, 0.0).astype(BF16)
        rowk = lax.broadcasted_iota(jnp.int32, (CHUNK, HEAD_DIM), 0)

        def chunk(j, dlb):
            c = nc - 1 - j
            rows = pl.ds(pl.multiple_of(c * CHUNK, CHUNK), CHUNK)
            qr, fr, v, dout = q_ref[rows, :], f_ref[rows, :], i_ref[rows, :], do_ref[rows, :]
            sq = _sigmoid(qr)
            q = qr * sq
            sf = _sigmoid(fr)
            f = lb + (1.0 - lb) * sf
            k = 1.0 - f
            g = _tri_sum(trilb, jnp.log(f))
            g_s[...] = g
            st0 = st_ref[0, c]
            dt = d_acc[...]
            vb, dob = v.astype(BF16), dout.astype(BF16)
            dtb, st0b = dt.astype(BF16), st0.astype(BF16)
            glast = g_s[CHUNK - 1:CHUNK, :]
            eg = jnp.exp(g)
            kdec = jnp.exp(glast - g)
            qeb, kdb = (q * eg).astype(BF16), (k * kdec).astype(BF16)
            aps = jnp.where(row > col, _dot_nt(dob, vb), 0.0)
            adiag = jnp.sum(dout * v, axis=-1, keepdims=True)
            dq_inter = _dot(dob, st0b)
            dk_inter = _dot(vb, dtb)
            dk_st = kdec * dk_inter
            dg = qeb.astype(F32) * dq_inter
            dg_minus = kdb.astype(F32) * dk_inter
            dg = dg - dg_minus
            for blk in range(CHUNK // SUB):
                lo, hi = blk * SUB, (blk + 1) * SUB
                gref = g_s[lo - 1:lo, :] if blk else jnp.zeros((1, HEAD_DIM), F32)
                qscale = jnp.exp(g[lo:hi] - gref)
                kscale = jnp.exp(jnp.where(rowk < hi, gref - g, NEG_BIG))
                qi = (q[lo:hi] * qscale).astype(BF16)
                ki = (k * kscale).astype(BF16)
                a_s[lo:hi, :] = _dot_nt(qi, ki)
                apb = aps[lo:hi].astype(BF16)
                from_k = _dot(apb, ki)
                from_q = _dot_tn(apb, qi)
                dq_s[lo:hi, :] = qscale * from_k
                dg_s[lo:hi, :] = qi.astype(F32) * from_k
                dk_st = dk_st + kscale * from_q
                dg = dg - ki.astype(F32) * from_q
            dg = dg + dg_s[...]
            a = jnp.where(tril, a_s[...], 0.0).astype(BF16)
            dv = _dot_tn(a, dob) + _dot_nt(kdb, dtb)
            dq_st = dq_s[...] + eg * dq_inter
            dq = dq_st + adiag * k
            dk = dk_st + adiag * q
            dt_dec = dt * jnp.exp(glast)
            d_acc[...] = dt_dec + _dot_tn(dob, qeb)
            later = jnp.sum(dt_dec * st0, axis=0, keepdims=True) + _colsum(dg_minus)
            dlf = later + _tri_sum(triub, dg)
            df = dlf / f - dk
            dq_ref[rows, :] = (dq * (sq * (1.0 + qr * (1.0 - sq)))).astype(dq_ref.dtype)
            df_ref[rows, :] = (df * (1.0 - lb) * sf * (1.0 - sf)).astype(df_ref.dtype)
            di_ref[rows, :] = dv.astype(di_ref.dtype)
            return dlb + jnp.sum(df * (1.0 - sf), axis=0, keepdims=True)

        dlb = lax.fori_loop(0, nc, chunk, jnp.zeros((1, HEAD_DIM), F32))
        _accumulate(dlb_ref, dlb, first)

    sect = jax.ShapeDtypeStruct((n, D_MODEL), BF16)
    return pl.pallas_call(
        body, name=name, grid=(HEADS, n // t),
        in_specs=[sec(COL_Q), sec(COL_F), sec(COL_I), lbspec, head_tile, state],
        out_specs=[head_tile, head_tile, head_tile, pl.BlockSpec((1, HEAD_DIM), lambda h, i: (0, h))],
        out_shape=[sect, sect, sect, jax.ShapeDtypeStruct((1, D_MODEL), F32)],
        scratch_shapes=[pltpu.VMEM((HEAD_DIM, HEAD_DIM), F32), pltpu.VMEM((CHUNK, HEAD_DIM), F32),
                        pltpu.VMEM((CHUNK, CHUNK), F32), pltpu.VMEM((CHUNK, HEAD_DIM), F32),
                        pltpu.VMEM((CHUNK, HEAD_DIM), F32)],
        compiler_params=_params("parallel", "arbitrary"),
    )(proj, proj, proj, hgrn_lb, do, states)


def _pool_fwd(proj, pool_w, pool_scale, *, name):
    n = proj.shape[0]
    t = _tile(n, 512, POOL_HALO)
    per = t // POOL_HALO
    c0 = COL_POOL // POOL_WIDTH

    def body(u_ref, halo_ref, pw_ref, ps_ref, pooled_ref, mixed_ref, ext):
        i = pl.program_id(0)
        u = u_ref[...]
        ext[POOL_HALO:POOL_HALO + t, :] = u
        ext[0:POOL_HALO, :] = jnp.where(i > 0, halo_ref[...], 0.0)
        pos = i * t + lax.broadcasted_iota(jnp.int32, (t, POOL_CH), 0) + 1
        for grp, win in enumerate(POOL_WINDOWS):
            cols = slice(grp * POOL_CH, (grp + 1) * POOL_CH)
            acc = u[:, cols]
            for j in range(1, win):
                acc = acc + ext[POOL_HALO - j:POOL_HALO - j + t, cols]
            pooled = (acc / jnp.minimum(pos, win).astype(F32) - u[:, cols]).astype(BF16)
            pooled_ref[:, cols] = pooled
            mixed_ref[:, cols] = (_dot(pooled, pw_ref[grp].astype(BF16)) * ps_ref[:, cols]).astype(BF16)

    tile = pl.BlockSpec((t, POOL_WIDTH), lambda i: (i, 0))
    return pl.pallas_call(
        body, name=name, grid=(n // t,),
        in_specs=[pl.BlockSpec((t, POOL_WIDTH), lambda i: (i, c0)),
                  pl.BlockSpec((POOL_HALO, POOL_WIDTH), lambda i: (jnp.maximum(i * per - 1, 0), c0)),
                  pl.BlockSpec((len(POOL_WINDOWS), POOL_CH, POOL_CH), lambda i: (0, 0, 0)),
                  pl.BlockSpec((1, POOL_WIDTH), lambda i: (0, 0))],
        out_specs=[tile, tile],
        out_shape=[jax.ShapeDtypeStruct((n, POOL_WIDTH), BF16), jax.ShapeDtypeStruct((n, POOL_WIDTH), BF16)],
        scratch_shapes=[pltpu.VMEM((t + POOL_HALO, POOL_WIDTH), F32)],
        compiler_params=_params("parallel"),
    )(proj, proj, pool_w, pool_scale)


def _pool_bwd(dmixed, pooled, pool_w, pool_scale, *, name):
    n = dmixed.shape[0]
    t = _tile(n, 512, POOL_HALO)
    per = t // POOL_HALO
    nb = n // t

    def body(dm_ref, dmh_ref, p_ref, pw_ref, ps_ref, du_ref, dpw_ref, dps_ref, ext):
        i = pl.program_id(0)

        @pl.when(i == 0)
        def _():
            dpw_ref[...] = jnp.zeros_like(dpw_ref)
            dps_ref[...] = jnp.zeros_like(dps_ref)

        dm, dmh = dm_ref[...], dmh_ref[...]
        pos = i * t + lax.broadcasted_iota(jnp.int32, (t, POOL_CH), 0) + 1
        for grp, win in enumerate(POOL_WINDOWS):
            cols = slice(grp * POOL_CH, (grp + 1) * POOL_CH)
            pwb = pw_ref[grp].astype(BF16)
            pb = p_ref[:, cols]
            scale = ps_ref[:, cols]
            dps_ref[:, cols] += _colsum(dm[:, cols] * _dot(pb, pwb))
            dpm = (dm[:, cols] * scale).astype(BF16)
            dpw_ref[grp] += _dot_tn(pb, dpm)
            dpool = _dot_nt(dpm, pwb)
            dpool_next = _dot_nt((dmh[:, cols] * scale).astype(BF16), pwb)
            ext[0:t, cols] = dpool / jnp.minimum(pos, win).astype(F32)
            ext[t:t + POOL_HALO, cols] = jnp.where(i < nb - 1, dpool_next * (1.0 / win), 0.0)
            acc = -dpool
            for j in range(win):
                acc = acc + ext[j:j + t, cols]
            du_ref[:, cols] = acc.astype(du_ref.dtype)

    tile = pl.BlockSpec((t, POOL_WIDTH), lambda i: (i, 0))
    return pl.pallas_call(
        body, name=name, grid=(nb,),
        in_specs=[tile, pl.BlockSpec((POOL_HALO, POOL_WIDTH), lambda i: (jnp.minimum((i + 1) * per, nb * per - 1), 0)),
                  tile, pl.BlockSpec((len(POOL_WINDOWS), POOL_CH, POOL_CH), lambda i: (0, 0, 0)),
                  pl.BlockSpec((1, POOL_WIDTH), lambda i: (0, 0))],
        out_specs=[tile, pl.BlockSpec((len(POOL_WINDOWS), POOL_CH, POOL_CH), lambda i: (0, 0, 0)),
                   pl.BlockSpec((1, POOL_WIDTH), lambda i: (0, 0))],
        out_shape=[jax.ShapeDtypeStruct((n, POOL_WIDTH), BF16),
                   jax.ShapeDtypeStruct((len(POOL_WINDOWS), POOL_CH, POOL_CH), F32),
                   jax.ShapeDtypeStruct((1, POOL_WIDTH), F32)],
        scratch_shapes=[pltpu.VMEM((t + POOL_HALO, POOL_WIDTH), F32)],
        compiler_params=_params("arbitrary"),
    )(dmixed, dmixed, pooled, pool_w, pool_scale)


def _adamw(w, g, m, v):
    m2 = ADAM_B1 * m + (1.0 - ADAM_B1) * g
    v2 = ADAM_B2 * v + (1.0 - ADAM_B2) * (g * g)
    m_hat = m2 * (1.0 / (1.0 - ADAM_B1 ** ADAM_STEP))
    v_hat = v2 * (1.0 / (1.0 - ADAM_B2 ** ADAM_STEP))
    delta = -ADAM_LR * (m_hat / (jnp.sqrt(v_hat) + ADAM_EPS) + ADAM_WD * w)
    return delta, m2, v2


def _adam_big(recv, w, m, v, *, name):
    r, c = w.shape
    tr = _tile(r, 256, 16)

    def body(recv_ref, w_ref, m_ref, v_ref, g_ref, d_ref, m2_ref, v2_ref):
        g = recv_ref[0].astype(F32)
        for i in range(1, N_DEV):
            g = g + recv_ref[i].astype(F32)
        delta, m2, v2 = _adamw(w_ref[...], g, m_ref[...], v_ref[...])
        g_ref[...] = g
        d_ref[...] = delta
        m2_ref[...] = m2
        v2_ref[...] = v2

    tile = pl.BlockSpec((tr, c), lambda i: (i, 0))
    out = jax.ShapeDtypeStruct((r, c), F32)
    return pl.pallas_call(
        body, name=name, grid=(r // tr,),
        in_specs=[pl.BlockSpec((N_DEV, tr, c), lambda i: (0, i, 0)), tile, tile, tile],
        out_specs=[tile] * 4, out_shape=[out] * 4, compiler_params=_params("parallel"),
    )(recv, w, m, v)


def _adam_small(parts, w, m, v, *, name):
    lb0, lbn = SMALL_ROWS["hgrn_lb"]
    half = lbn // 2

    def body(parts_ref, w_ref, m_ref, v_ref, g_ref, d_ref, m2_ref, v2_ref):
        g = parts_ref[0]
        for i in range(1, N_DEV):
            g = g + parts_ref[i]
        w_ = w_ref[...]
        s0 = 1.0 / (1.0 + jnp.exp(w_[lb0 + half:lb0 + lbn] - w_[lb0:lb0 + half]))
        ga = g[lb0:lb0 + half] * s0 * (1.0 - s0)
        g = jnp.concatenate([g[:lb0], ga, -ga, g[lb0 + lbn:]], axis=0)
        delta, m2, v2 = _adamw(w_, g, m_ref[...], v_ref[...])
        g_ref[...] = g
        d_ref[...] = delta
        m2_ref[...] = m2
        v2_ref[...] = v2

    out = jax.ShapeDtypeStruct(w.shape, F32)
    return pl.pallas_call(body, name=name, out_shape=[out] * 4, compiler_params=_params())(parts, w, m, v)


def _pack_small(vals):
    pieces, at = [], 0
    for name, (row0, nrows) in SMALL_ROWS.items():
        if row0 > at:
            pieces.append(jnp.zeros((row0 - at, 128), F32))
        pieces.append(vals[name].astype(F32).reshape(nrows, 128))
        at = row0 + nrows
    if at < SMALL_TOTAL_ROWS:
        pieces.append(jnp.zeros((SMALL_TOTAL_ROWS - at, 128), F32))
    return jnp.concatenate(pieces, axis=0)


def _unpack_small(packed, shapes):
    return {name: packed[row0:row0 + nrows].reshape(shapes[name]) for name, (row0, nrows) in SMALL_ROWS.items()}


def _ffn_forward(x, g, w1, w3, w2, tag):
    h = _rms_fwd(x, g, name=f"{tag}_rms")
    a, b, s = _ffn_up(h, w1, w3, name=f"{tag}_up")
    x_out = _mm_nn(s, w2, name=f"{tag}_down", tn=1024, tm=512, res=x, scale=0.5)
    return x_out, (h, a, b, s)


def _ffn_backward(dx, dxs, x_in, g, w1, w3, w2, saved, tag):
    h, a, b, s = saved
    da, db = _ffn_bwd_mid(dxs, w2, a, b, name=f"{tag}_bwd_mid")
    dw2 = _mm_tn(s, dxs, name=f"{tag}_dw2", tn=512, tm=512)
    dw1 = _mm_tn(h, da, name=f"{tag}_dw1", tn=512, tm=1408)
    dw3 = _mm_tn(h, db, name=f"{tag}_dw3", tn=512, tm=1408)
    dh = _mm_nt([(da, w1), (db, w3)], name=f"{tag}_dh", tn=512, tk=512)
    return dh, dw1, dw3, dw2


def _to_slots(dw, axis):
    k, m = dw.shape
    if axis == 0:
        return dw.reshape(N_DEV, k // N_DEV, m).astype(BF16)
    return dw.reshape(k, N_DEV, m // N_DEV).transpose(1, 0, 2).astype(BF16)


def _from_slots(gathered, axis):
    _, r, c = gathered.shape
    if axis == 0:
        return gathered.reshape(N_DEV * r, c)
    return gathered.transpose(1, 0, 2).reshape(r, N_DEV * c)


def kernel(x, p, ffn1_norm, ffn1_w1, ffn1_w3, ffn1_w2, mix_norm, w_in, hgrn_lb, hgrn_onorm, w_branch_a, pool_w, pool_scale, w_branch_b, w_out, ffn2_norm, ffn2_w1, ffn2_w3, ffn2_w2, ple_norm, ple_w_gate, ple_w_proj, ple_post_norm, final_norm, loss_target, m_ffn1_norm, m_ffn1_w1, m_ffn1_w3, m_ffn1_w2, m_mix_norm, m_w_in, m_hgrn_lb, m_hgrn_onorm, m_w_branch_a, m_pool_w, m_pool_scale, m_w_branch_b, m_w_out, m_ffn2_norm, m_ffn2_w1, m_ffn2_w3, m_ffn2_w2, m_ple_norm, m_ple_w_gate, m_ple_w_proj, m_ple_post_norm, m_final_norm, v_ffn1_norm, v_ffn1_w1, v_ffn1_w3, v_ffn1_w2, v_mix_norm, v_w_in, v_hgrn_lb, v_hgrn_onorm, v_w_branch_a, v_pool_w, v_pool_scale, v_w_branch_b, v_w_out, v_ffn2_norm, v_ffn2_w1, v_ffn2_w3, v_ffn2_w2, v_ple_norm, v_ple_w_gate, v_ple_w_proj, v_ple_post_norm, v_final_norm):
    weights = dict(ffn1_norm=ffn1_norm, ffn1_w1=ffn1_w1, ffn1_w3=ffn1_w3, ffn1_w2=ffn1_w2, mix_norm=mix_norm, w_in=w_in, hgrn_lb=hgrn_lb, hgrn_onorm=hgrn_onorm, w_branch_a=w_branch_a, pool_w=pool_w, pool_scale=pool_scale, w_branch_b=w_branch_b, w_out=w_out, ffn2_norm=ffn2_norm, ffn2_w1=ffn2_w1, ffn2_w3=ffn2_w3, ffn2_w2=ffn2_w2, ple_norm=ple_norm, ple_w_gate=ple_w_gate, ple_w_proj=ple_w_proj, ple_post_norm=ple_post_norm, final_norm=final_norm)
    mom1 = dict(ffn1_norm=m_ffn1_norm, ffn1_w1=m_ffn1_w1, ffn1_w3=m_ffn1_w3, ffn1_w2=m_ffn1_w2, mix_norm=m_mix_norm, w_in=m_w_in, hgrn_lb=m_hgrn_lb, hgrn_onorm=m_hgrn_onorm, w_branch_a=m_w_branch_a, pool_w=m_pool_w, pool_scale=m_pool_scale, w_branch_b=m_w_branch_b, w_out=m_w_out, ffn2_norm=m_ffn2_norm, ffn2_w1=m_ffn2_w1, ffn2_w3=m_ffn2_w3, ffn2_w2=m_ffn2_w2, ple_norm=m_ple_norm, ple_w_gate=m_ple_w_gate, ple_w_proj=m_ple_w_proj, ple_post_norm=m_ple_post_norm, final_norm=m_final_norm)
    mom2 = dict(ffn1_norm=v_ffn1_norm, ffn1_w1=v_ffn1_w1, ffn1_w3=v_ffn1_w3, ffn1_w2=v_ffn1_w2, mix_norm=v_mix_norm, w_in=v_w_in, hgrn_lb=v_hgrn_lb, hgrn_onorm=v_hgrn_onorm, w_branch_a=v_w_branch_a, pool_w=v_pool_w, pool_scale=v_pool_scale, w_branch_b=v_w_branch_b, w_out=v_w_out, ffn2_norm=v_ffn2_norm, ffn2_w1=v_ffn2_w1, ffn2_w3=v_ffn2_w3, ffn2_w2=v_ffn2_w2, ple_norm=v_ple_norm, ple_w_gate=v_ple_w_gate, ple_w_proj=v_ple_w_proj, ple_post_norm=v_ple_post_norm, final_norm=v_final_norm)

    xs = x[0]
    ps = p[0, 0].astype(BF16)
    tgt = loss_target[0]
    n = xs.shape[0]

    gathered = _exchange([weights[name][0].astype(BF16) for name, _ in BIG_WEIGHTS], name="gather_weights", bcast=True)
    full = {name: _from_slots(g, axis) for (name, axis), g in zip(BIG_WEIGHTS, gathered)}

    g_f1, g_mix, g_on, g_f2 = ffn1_norm, mix_norm, hgrn_onorm, ffn2_norm
    g_ple, g_post, g_fin = ple_norm, ple_post_norm, final_norm.reshape(1, D_MODEL)
    lb2 = hgrn_lb
    pw, pscale = pool_w[0], pool_scale

    x1, ffn1_saved = _ffn_forward(xs, g_f1, full["ffn1_w1"], full["ffn1_w3"], full["ffn1_w2"], "ffn1")
    h2 = _rms_fwd(x1, g_mix, name="mix_rms")
    proj = _mm_nn(h2, full["w_in"], name="w_in_proj", tn=1024, tm=512)
    o, states = _hgrn_fwd(proj, lb2, name="hgrn_fwd")
    on = _hgrn_post_fwd(o, proj, g_on, name="hgrn_post_fwd")
    ya = _mm_nn(on, full["w_branch_a"], name="branch_a", tn=1024, tm=512)
    pooled, mixed = _pool_fwd(proj, pw, pscale, name="pool_fwd")
    yb = _mm_nn(mixed, full["w_branch_b"], name="branch_b", tn=1024, tm=512)
    y = _merge_fwd(proj, ya, yb, name="merge_fwd")
    x2 = _mm_nn(y, full["w_out"], name="w_out_proj", tn=1024, tm=512, res=x1)
    x3, ffn2_saved = _ffn_forward(x2, g_f2, full["ffn2_w1"], full["ffn2_w3"], full["ffn2_w2"], "ffn2")
    h4 = _rms_fwd(x3, g_ple, name="ple_rms")
    gpre = _mm_nn(h4, full["ple_w_gate"], name="ple_gate", tn=1024, tm=512)
    z = _mm_nn(ps, full["ple_w_proj"], name="ple_proj", tn=1024, tm=512)
    dx4, dpre, dz, loss_part, d_fin, d_post = _ple_final(x3, gpre, z, tgt, g_post, g_fin, name="ple_final")

    dfull = {}
    dfull["ple_w_proj"] = _mm_tn(ps, dz, name="d_ple_w_proj", tn=512, tm=1024)
    dfull["ple_w_gate"] = _mm_tn(h4, dpre, name="d_ple_w_gate", tn=512, tm=1024)
    dh4 = _mm_nt([(dpre, full["ple_w_gate"])], name="d_h4", tn=1024, tk=512)
    dx3, dx3s, d_ple = _rms_bwd_add(dh4, x3, dx4, g_ple, name="ple_rms_bwd", half_scale=0.5)

    dh3, dfull["ffn2_w1"], dfull["ffn2_w3"], dfull["ffn2_w2"] = _ffn_backward(
        dx3, dx3s, x2, g_f2, full["ffn2_w1"], full["ffn2_w3"], full["ffn2_w2"], ffn2_saved, "ffn2")
    dx2, dx2b, d_f2 = _rms_bwd_add(dh3, x2, dx3, g_f2, name="ffn2_rms_bwd", half_scale=1.0)

    dfull["w_out"] = _mm_tn(y, dx2b, name="d_w_out", tn=512, tm=1024)
    dy = _mm_nt([(dx2b, full["w_out"])], name="d_y", tn=1024, tk=512)
    dya, dyb, dga, dgb = _merge_bwd(dy, proj, ya, yb, name="merge_bwd")

    dfull["w_branch_b"] = _mm_tn(mixed, dyb, name="d_w_branch_b", tn=512, tm=1024)
    dmixed = _mm_nt([(dyb, full["w_branch_b"])], name="d_mixed", tn=1024, tk=512)
    du, d_pw, d_ps = _pool_bwd(dmixed, pooled, pw, pscale, name="pool_bwd")

    dfull["w_branch_a"] = _mm_tn(on, dya, name="d_w_branch_a", tn=512, tm=1024)
    don = _mm_nt([(dya, full["w_branch_a"])], name="d_on", tn=1024, tk=512)
    dog, do, d_on = _hgrn_post_bwd(don, o, proj, g_on, name="hgrn_post_bwd")
    dqr, dfr, dir_, d_lb = _hgrn_bwd(proj, lb2, do, states, name="hgrn_bwd")

    dproj = jnp.concatenate([dqr, dfr, dir_, dog, du, dga, dgb], axis=1)
    dfull["w_in"] = _mm_tn(h2, dproj, name="d_w_in", tn=512, tm=1664)
    dh2 = _mm_nt([(dproj, full["w_in"])], name="d_h2", tn=512, tk=512)
    dx1, dx1s, d_mix = _rms_bwd_add(dh2, x1, dx2, g_mix, name="mix_rms_bwd", half_scale=0.5)

    dh1, dfull["ffn1_w1"], dfull["ffn1_w3"], dfull["ffn1_w2"] = _ffn_backward(
        dx1, dx1s, xs, g_f1, full["ffn1_w1"], full["ffn1_w3"], full["ffn1_w2"], ffn1_saved, "ffn1")
    grad_x, _, d_f1 = _rms_bwd_add(dh1, xs, dx1, g_f1, name="ffn1_rms_bwd", half_scale=1.0)

    small_part = _pack_small(dict(
        ffn1_norm=d_f1, mix_norm=d_mix, hgrn_onorm=d_on, ffn2_norm=d_f2, ple_norm=d_ple, ple_post_norm=d_post,
        final_norm=d_fin, hgrn_lb=jnp.concatenate([d_lb, jnp.zeros_like(d_lb)], axis=0), pool_scale=d_ps, pool_w=d_pw))
    received = _exchange([_to_slots(dfull[name], axis) for name, axis in BIG_WEIGHTS], name="scatter_grads", bcast=False)
    small_all = _exchange([small_part], name="gather_small_grads", bcast=True)[0]

    grads, deltas, new_m, new_v = {}, {}, {}, {}
    for (name, _), recv in zip(BIG_WEIGHTS, received):
        shape = weights[name].shape
        res = _adam_big(recv, weights[name][0], mom1[name][0], mom2[name][0], name=f"adam_{name}")
        grads[name], deltas[name], new_m[name], new_v[name] = [r.reshape(shape) for r in res]
    shapes = {name: weights[name].shape for name in SMALL_ROWS}
    res = _adam_small(small_all, _pack_small(weights), _pack_small(mom1), _pack_small(mom2), name="adam_small")
    for store, packed in zip((grads, deltas, new_m, new_v), res):
        store.update(_unpack_small(packed, shapes))

    loss = lax.psum(jnp.sum(loss_part), ("x", "y", "c"))
    return (loss, grad_x.reshape(x.shape), *[grads[k] for k in WEIGHT_ORDER], *[deltas[k] for k in WEIGHT_ORDER],
            *[new_m[k] for k in WEIGHT_ORDER], *[new_v[k] for k in WEIGHT_ORDER])
```

```python
import jax
import jax.numpy as jnp
from jax import lax
from jax.experimental import pallas as pl
from jax.experimental.pallas import tpu as pltpu

F32 = jnp.float32
BF16 = jnp.bfloat16

N_DEV = 8
D_MODEL = 1024
HEADS = 8
HEAD_DIM = 128
POOL_WINDOWS = (2, 4, 8, 16)
POOL_CH = 128
POOL_WIDTH = 512
POOL_HALO = 16
RMS_EPS = 1e-6
CHUNK = 64
SUB = 16
HGRN_HEADS_PER_STEP = 8
NEG_BIG = -1e30

ADAM_LR = 0.001
ADAM_B1 = 0.9
ADAM_B2 = 0.999
ADAM_EPS = 1e-08
ADAM_WD = 0.01
ADAM_STEP = 10

V7X_VMEM_BYTES = 64 * 1024 * 1024
VMEM_LIMIT = (V7X_VMEM_BYTES * 3) // 4
ROW_TILE_CAP = 8192

COL_Q, COL_F, COL_I, COL_OG, COL_POOL, COL_GA, COL_GB = 0, 1024, 2048, 3072, 4096, 4608, 5632

BIG_WEIGHTS = (
    ("ffn1_w1", 1), ("ffn1_w3", 1), ("ffn1_w2", 0), ("w_in", 1), ("w_branch_a", 0), ("w_branch_b", 1),
    ("w_out", 0), ("ffn2_w1", 1), ("ffn2_w3", 1), ("ffn2_w2", 0), ("ple_w_gate", 0), ("ple_w_proj", 1),
)
SMALL_ROWS = {
    "ffn1_norm": (0, 8), "mix_norm": (8, 8), "hgrn_onorm": (16, 8), "ffn2_norm": (24, 8), "ple_norm": (32, 8),
    "ple_post_norm": (40, 8), "final_norm": (48, 8), "hgrn_lb": (56, 16), "pool_scale": (72, 4), "pool_w": (80, 512),
}
SMALL_TOTAL_ROWS = 592
WEIGHT_ORDER = (
    "ffn1_norm", "ffn1_w1", "ffn1_w3", "ffn1_w2", "mix_norm", "w_in", "hgrn_lb", "hgrn_onorm", "w_branch_a", "pool_w",
    "pool_scale", "w_branch_b", "w_out", "ffn2_norm", "ffn2_w1", "ffn2_w3", "ffn2_w2", "ple_norm", "ple_w_gate",
    "ple_w_proj", "ple_post_norm", "final_norm",
)


def _params(*sem):
    return pltpu.CompilerParams(dimension_semantics=sem if sem else None, vmem_limit_bytes=VMEM_LIMIT)


def _dot(a, b):
    return jnp.dot(a, b, preferred_element_type=F32)


def _dot_nt(a, b):
    return lax.dot_general(a, b, (((1,), (1,)), ((), ())), preferred_element_type=F32)


def _dot_tn(a, b):
    return lax.dot_general(a, b, (((0,), (0,)), ((), ())), preferred_element_type=F32)


def _sigmoid(x):
    return 1.0 / (1.0 + jnp.exp(-x))


def _tile(n, want, mult):
    if mult != 128:
        want = min(want, ROW_TILE_CAP)
    if n <= want:
        return n
    t = (want // mult) * mult
    while t > mult and n % t:
        t -= mult
    assert n % t == 0, (n, want, mult)
    return t


class _Exchange:
    COPIES = N_DEV - 1

    def __init__(self, arrs, gather):
        self.arrs, self.gather, self.n = list(arrs), gather, len(arrs)
        self.out_shape = [jax.ShapeDtypeStruct((N_DEV,) + (a.shape if gather else a.shape[1:]), a.dtype) for a in arrs]
        self.scratch = [pltpu.SemaphoreType.DMA((self.n * self.COPIES,)),
                        pltpu.SemaphoreType.DMA((self.n * self.COPIES,)), pltpu.SemaphoreType.DMA((self.n,))]
        self.received = None

    @staticmethod
    def _place():
        x, y, c = lax.axis_index("x"), lax.axis_index("y"), lax.axis_index("c")
        return x, y, c

    def _copy(self, a, k, src, dst, to, sems):
        s = a * self.COPIES + k
        return pltpu.make_async_remote_copy(src_ref=src, dst_ref=dst, send_sem=sems[0].at[s], recv_sem=sems[1].at[s],
                                            device_id=to, device_id_type=pl.DeviceIdType.MESH)

    def _gather_copies(self, ins, outs, sems):
        x, y, c = self._place()
        chips = [(1 - x, y), (x, 1 - y), (1 - x, 1 - y)]
        slot = lambda px, py, pc: 4 * px + 2 * py + pc
        first, passed, arrivals = [], [], []
        for a in range(self.n):
            mine = outs[a].at[slot(x, y, c)]
            first.append(self._copy(a, 0, ins[a], mine, (x, y, 1 - c), sems))
            arrivals.append(self._copy(a, 0, ins[a], outs[a].at[slot(x, y, 1 - c)], (x, y, 1 - c), sems))
            for j, (px, py) in enumerate(chips):
                first.append(self._copy(a, 1 + j, ins[a], mine, (px, py, c), sems))
                theirs = outs[a].at[slot(px, py, c)]
                passed.append((self._copy(a, 1 + j, ins[a], theirs, (px, py, c), sems),
                               self._copy(a, 4 + j, theirs, theirs, (x, y, 1 - c), sems)))
                arrivals.append(self._copy(a, 4 + j, ins[a], outs[a].at[slot(px, py, 1 - c)], (x, y, 1 - c), sems))
        return first, passed, arrivals

    def _scatter_copies(self, ins, outs, sems):
        x, y, c = self._place()
        me = 4 * x + 2 * y + c
        sends, arrivals = [], []
        for k in range(1, N_DEV):
            px = 1 - x if k & 4 else x
            py = 1 - y if k & 2 else y
            pc = 1 - c if k & 1 else c
            peer = 4 * px + 2 * py + pc
            for a in range(self.n):
                sends.append(self._copy(a, k - 1, ins[a].at[peer], outs[a].at[me], (px, py, pc), sems))
                arrivals.append(self._copy(a, k - 1, ins[a].at[peer], outs[a].at[peer], (px, py, pc), sems))
        return sends, arrivals

    def _local(self, ins, outs, sems):
        x, y, c = self._place()
        me = 4 * x + 2 * y + c
        return [pltpu.make_async_copy(ins[a] if self.gather else ins[a].at[me], outs[a].at[me], sems[2].at[a])
                for a in range(self.n)]

    def start(self, ins, outs, sems):
        for cp in self._local(ins, outs, sems):
            cp.start()
        sends = self._gather_copies(ins, outs, sems)[0] if self.gather else self._scatter_copies(ins, outs, sems)[0]
        for cp in sends:
            cp.start()

    def finish(self, ins, outs, sems):
        if self.gather:
            first, passed, arrivals = self._gather_copies(ins, outs, sems)
            for landed, onward in passed:
                landed.wait_recv()
                onward.start()
            sends = first + [onward for _, onward in passed]
        else:
            sends, arrivals = self._scatter_copies(ins, outs, sems)
        for cp in arrivals:
            cp.wait_recv()
        for cp in sends:
            cp.wait_send()
        for cp in self._local(ins, outs, sems):
            cp.wait()


def _call(body, *, name, grid, in_specs, out_specs, out_shape, args, semantics, scratch=(), exchange=None):
    if exchange is None:
        return pl.pallas_call(
            body, name=name, grid=grid, in_specs=in_specs, out_specs=out_specs, out_shape=out_shape,
            scratch_shapes=list(scratch), compiler_params=_params(*semantics))(*args)
    ex = exchange
    n_in, n_out, n_s = len(in_specs), len(out_specs), len(scratch)

    def wrapped(*refs):
        ins, ex_in = refs[:n_in], refs[n_in:n_in + ex.n]
        o0 = n_in + ex.n
        outs, ex_out = refs[o0:o0 + n_out], refs[o0 + n_out:o0 + n_out + ex.n]
        s0 = o0 + n_out + ex.n
        scr, sems = refs[s0:s0 + n_s], refs[s0 + n_s:]
        ids = [pl.program_id(ax) for ax in range(len(grid))]
        first = ids[0] == 0
        last = ids[0] == grid[0] - 1
        for ax in range(1, len(grid)):
            first = jnp.logical_and(first, ids[ax] == 0)
            last = jnp.logical_and(last, ids[ax] == grid[ax] - 1)

        @pl.when(first)
        def _():
            ex.start(ex_in, ex_out, sems)

        body(*ins, *outs, *scr)

        @pl.when(last)
        def _():
            ex.finish(ex_in, ex_out, sems)

    hbm = pl.BlockSpec(memory_space=pltpu.HBM)
    res = pl.pallas_call(
        wrapped, name=name, grid=grid, in_specs=list(in_specs) + [hbm] * ex.n,
        out_specs=list(out_specs) + [hbm] * ex.n, out_shape=list(out_shape) + ex.out_shape,
        scratch_shapes=list(scratch) + ex.scratch, compiler_params=_params(*(["arbitrary"] * len(grid))),
    )(*args, *ex.arrs)
    ex.received = res[n_out:]
    return res[:n_out]


def _exchange_now(arrs, *, name, gather):
    ex = _Exchange(arrs, gather)
    n = ex.n

    def body(*refs):
        ex.start(refs[:n], refs[n:2 * n], refs[2 * n:])
        ex.finish(refs[:n], refs[n:2 * n], refs[2 * n:])

    hbm = pl.BlockSpec(memory_space=pltpu.HBM)
    return pl.pallas_call(body, name=name, out_shape=ex.out_shape, in_specs=[hbm] * n, out_specs=[hbm] * n,
                          scratch_shapes=ex.scratch)(*arrs)


def _mm_nn(a, b, *, name, tn, tm, out_dtype=F32, res=None, scale=1.0, exchange=None):
    n, k = a.shape
    m = b.shape[1]
    tn, tm = _tile(n, tn, 16), _tile(m, tm, 128)

    def body(*refs):
        a_ref, b_ref = refs[0], refs[1]
        o_ref = refs[-1]
        acc = _dot(a_ref[...], b_ref[...])
        if scale != 1.0:
            acc = acc * scale
        if res is not None:
            acc = acc + refs[2][...]
        o_ref[...] = acc.astype(o_ref.dtype)

    in_specs = [pl.BlockSpec((tn, k), lambda i, j: (i, 0)), pl.BlockSpec((k, tm), lambda i, j: (0, j))]
    args = [a, b]
    if res is not None:
        in_specs.append(pl.BlockSpec((tn, tm), lambda i, j: (i, j)))
        args.append(res)
    return _call(body, name=name, grid=(n // tn, m // tm), in_specs=in_specs,
                 out_specs=[pl.BlockSpec((tn, tm), lambda i, j: (i, j))],
                 out_shape=[jax.ShapeDtypeStruct((n, m), out_dtype)], args=args, semantics=("parallel", "parallel"),
                 exchange=exchange)[0]


def _mm_nt(pairs, *, name, tn, tk, out_dtype=F32, exchange=None):
    n = pairs[0][0].shape[0]
    kk = pairs[0][1].shape[0]
    tn, tk = _tile(n, tn, 16), _tile(kk, tk, 128)
    npair = len(pairs)

    def body(*refs):
        o_ref = refs[-1]
        acc = _dot_nt(refs[0][...], refs[1][...])
        for q in range(1, npair):
            acc = acc + _dot_nt(refs[2 * q][...], refs[2 * q + 1][...])
        o_ref[...] = acc.astype(o_ref.dtype)

    in_specs, args = [], []
    for a, b in pairs:
        m = a.shape[1]
        in_specs += [pl.BlockSpec((tn, m), lambda i, j: (i, 0)), pl.BlockSpec((tk, m), lambda i, j: (j, 0))]
        args += [a, b]
    return _call(body, name=name, grid=(n // tn, kk // tk), in_specs=in_specs,
                 out_specs=[pl.BlockSpec((tn, tk), lambda i, j: (i, j))],
                 out_shape=[jax.ShapeDtypeStruct((n, kk), out_dtype)], args=args, semantics=("parallel", "parallel"),
                 exchange=exchange)[0]


def _mm_tn(a, b, *, name, tn, tm, exchange=None):
    n, k = a.shape
    m = b.shape[1]
    tn, tm = _tile(n, tn, 16), _tile(m, tm, 128)

    def body(a_ref, b_ref, o_ref):
        @pl.when(pl.program_id(1) == 0)
        def _():
            o_ref[...] = jnp.zeros_like(o_ref)

        o_ref[...] += _dot_tn(a_ref[...], b_ref[...])

    return _call(body, name=name, grid=(m // tm, n // tn),
                 in_specs=[pl.BlockSpec((tn, k), lambda j, i: (i, 0)), pl.BlockSpec((tn, tm), lambda j, i: (i, j))],
                 out_specs=[pl.BlockSpec((k, tm), lambda j, i: (0, j))],
                 out_shape=[jax.ShapeDtypeStruct((k, m), F32)], args=[a, b], semantics=("parallel", "arbitrary"),
                 exchange=exchange)[0]


def _ffn_up(h, w1, w3, *, name, exchange=None):
    n, k = h.shape
    m = w1.shape[1]
    tn, tm = _tile(n, 1024, 16), _tile(m, 256, 128)

    def body(h_ref, w1_ref, w3_ref, a_ref, b_ref, s_ref):
        hb = h_ref[...]
        a = _dot(hb, w1_ref[...])
        b = _dot(hb, w3_ref[...])
        a_ref[...] = a
        b_ref[...] = b
        s_ref[...] = (a * _sigmoid(a) * b).astype(s_ref.dtype)

    wspec = pl.BlockSpec((k, tm), lambda i, j: (0, j))
    ospec = pl.BlockSpec((tn, tm), lambda i, j: (i, j))
    return _call(body, name=name, grid=(n // tn, m // tm),
                 in_specs=[pl.BlockSpec((tn, k), lambda i, j: (i, 0)), wspec, wspec], out_specs=[ospec, ospec, ospec],
                 out_shape=[jax.ShapeDtypeStruct((n, m), F32), jax.ShapeDtypeStruct((n, m), F32),
                            jax.ShapeDtypeStruct((n, m), BF16)],
                 args=[h, w1, w3], semantics=("parallel", "parallel"), exchange=exchange)


def _ffn_bwd_mid(dxs, w2, a, b, *, name, exchange=None):
    n, d = dxs.shape
    m = w2.shape[0]
    tn, tk = _tile(n, 1024, 16), _tile(m, 256, 128)

    def body(dx_ref, w2_ref, a_ref, b_ref, da_ref, db_ref):
        ds = _dot_nt(dx_ref[...], w2_ref[...])
        a_, b_ = a_ref[...], b_ref[...]
        sg = _sigmoid(a_)
        da_ref[...] = (ds * b_ * (sg * (1.0 + a_ * (1.0 - sg)))).astype(da_ref.dtype)
        db_ref[...] = (ds * (a_ * sg)).astype(db_ref.dtype)

    tile = pl.BlockSpec((tn, tk), lambda i, j: (i, j))
    return _call(body, name=name, grid=(n // tn, m // tk),
                 in_specs=[pl.BlockSpec((tn, d), lambda i, j: (i, 0)), pl.BlockSpec((tk, d), lambda i, j: (j, 0)),
                           tile, tile],
                 out_specs=[tile, tile],
                 out_shape=[jax.ShapeDtypeStruct((n, m), BF16), jax.ShapeDtypeStruct((n, m), BF16)],
                 args=[dxs, w2, a, b], semantics=("parallel", "parallel"), exchange=exchange)


def _rowwise(fn, *, name, n, tn, ncol, rows, vecs, outs, accs=()):
    tn = _tile(n, tn, 16)
    nr, nv, no = len(rows), len(vecs), len(outs)

    def body(*refs):
        first = pl.program_id(1) == 0
        vals = [r[...] for r in refs[:nr + nv]]
        res = fn(*vals)
        for ref, val in zip(refs[nr + nv:nr + nv + no], res[:no]):
            ref[...] = val.astype(ref.dtype)
        for ref, val in zip(refs[nr + nv + no:], res[no:]):
            _accumulate(ref, val, first)

    in_specs = [pl.BlockSpec((tn, w), lambda j, i, c0=c0: (i, c0 + j)) for _, w, c0 in rows]
    in_specs += [pl.BlockSpec((1, w), lambda j, i, c0=c0: (0, c0 + j)) for _, w, c0 in vecs]
    out_specs = [pl.BlockSpec((tn, w), lambda j, i: (i, j)) for _, w, _ in outs]
    out_specs += [pl.BlockSpec((1, w), lambda j, i: (0, j)) for _, w in accs]
    out_shape = [jax.ShapeDtypeStruct((n, tw), dt) for tw, _, dt in outs]
    out_shape += [jax.ShapeDtypeStruct((1, tw), F32) for tw, _ in accs]
    return pl.pallas_call(
        body, name=name, grid=(ncol, n // tn), in_specs=in_specs, out_specs=out_specs, out_shape=out_shape,
        compiler_params=_params("parallel", "arbitrary"),
    )(*[r[0] for r in rows], *[v[0] for v in vecs])


def _accumulate(ref, val, first):
    @pl.when(first)
    def _():
        ref[...] = jnp.zeros_like(ref)

    ref[...] += val


def _colsum(x):
    return jnp.sum(x, axis=0, keepdims=True)


def _rowmean(x):
    return jnp.mean(x, axis=-1, keepdims=True)


def _rms_fwd(x, g, *, name):
    def fn(x_, g_):
        r = lax.rsqrt(_rowmean(x_ * x_) + RMS_EPS)
        return (x_ * r * g_,)

    n, d = x.shape
    return _rowwise(fn, name=name, n=n, tn=512, ncol=1, rows=[(x, d, 0)], vecs=[(g, d, 0)], outs=[(d, d, BF16)])[0]


def _rms_bwd_add(dh, x, extra, g, *, name, half_scale):
    def fn(dh_, x_, e_, g_):
        r = lax.rsqrt(_rowmean(x_ * x_) + RMS_EPS)
        xh = x_ * r
        dxh = dh_ * g_
        dx = e_ + r * (dxh - xh * _rowmean(dxh * xh))
        return dx, dx * half_scale, _colsum(dh_ * xh)

    n, d = x.shape
    return _rowwise(fn, name=name, n=n, tn=256, ncol=1, rows=[(dh, d, 0), (x, d, 0), (extra, d, 0)],
                    vecs=[(g, d, 0)], outs=[(d, d, F32), (d, d, BF16)], accs=[(d, d)])


def _ple_final(x3, gpre, z, tgt, gpp, gf, *, name):
    def fn(x3_, gpre_, z_, tgt_, gpp_, gf_):
        gate = _sigmoid(gpre_)
        rz = lax.rsqrt(_rowmean(z_ * z_) + RMS_EPS)
        zh = z_ * rz
        e = zh * gpp_
        x4 = x3_ + gate * e
        r4 = lax.rsqrt(_rowmean(x4 * x4) + RMS_EPS)
        x4h = x4 * r4
        diff = x4h * gf_ - tgt_
        dout = diff * (1.0 / D_MODEL)
        dxh4 = dout * gf_
        dx4 = r4 * (dxh4 - x4h * _rowmean(dxh4 * x4h))
        dpre = dx4 * e * gate * (1.0 - gate)
        de = dx4 * gate
        dzh = de * gpp_
        dz = rz * (dzh - zh * _rowmean(dzh * zh))
        return dx4, dpre, dz, _colsum(diff * diff) * (0.5 / D_MODEL), _colsum(dout * x4h), _colsum(de * zh)

    n, d = x3.shape
    return _rowwise(fn, name=name, n=n, tn=256, ncol=1, rows=[(x3, d, 0), (gpre, d, 0), (z, d, 0), (tgt, d, 0)],
                    vecs=[(gpp, d, 0), (gf, d, 0)], outs=[(d, d, F32), (d, d, BF16), (d, d, BF16)],
                    accs=[(d, d), (d, d), (d, d)])


def _merge_fwd(proj, ya, yb, *, name):
    def fn(ga, gb, ya_, yb_):
        return (_sigmoid(ga) * ya_ + _sigmoid(gb) * yb_,)

    n = proj.shape[0]
    w = 512
    return _rowwise(fn, name=name, n=n, tn=512, ncol=D_MODEL // w,
                    rows=[(proj, w, COL_GA // w), (proj, w, COL_GB // w), (ya, w, 0), (yb, w, 0)], vecs=[],
                    outs=[(D_MODEL, w, BF16)])[0]


def _merge_bwd(dy, proj, ya, yb, *, name):
    def fn(dy_, ga, gb, ya_, yb_):
        sa, sb = _sigmoid(ga), _sigmoid(gb)
        return dy_ * sa, dy_ * sb, dy_ * ya_ * sa * (1.0 - sa), dy_ * yb_ * sb * (1.0 - sb)

    n = proj.shape[0]
    w = 512
    return _rowwise(fn, name=name, n=n, tn=512, ncol=D_MODEL // w,
                    rows=[(dy, w, 0), (proj, w, COL_GA // w), (proj, w, COL_GB // w), (ya, w, 0), (yb, w, 0)],
                    vecs=[], outs=[(D_MODEL, w, BF16)] * 4)


def _hgrn_post_fwd(o, proj, onorm, *, name):
    def fn(o_, og, gam):
        r = lax.rsqrt(_rowmean(o_ * o_) + RMS_EPS)
        return (o_ * r * gam * (og * _sigmoid(og)),)

    n = o.shape[0]
    w = HEAD_DIM
    return _rowwise(fn, name=name, n=n, tn=1024, ncol=HEADS, rows=[(o, w, 0), (proj, w, COL_OG // w)],
                    vecs=[(onorm, w, 0)], outs=[(D_MODEL, w, BF16)])[0]


def _hgrn_post_bwd(don, o, proj, onorm, *, name):
    def fn(don_, o_, og, gam):
        r = lax.rsqrt(_rowmean(o_ * o_) + RMS_EPS)
        oh = o_ * r
        sg = _sigmoid(og)
        dog = don_ * oh * gam * (sg * (1.0 + og * (1.0 - sg)))
        dn = don_ * (og * sg)
        doh = dn * gam
        do = r * (doh - oh * _rowmean(doh * oh))
        return dog, do, _colsum(dn * oh)

    n = o.shape[0]
    w = HEAD_DIM
    return _rowwise(fn, name=name, n=n, tn=1024, ncol=HEADS, rows=[(don, w, 0), (o, w, 0), (proj, w, COL_OG // w)],
                    vecs=[(onorm, w, 0)], outs=[(D_MODEL, w, BF16), (D_MODEL, w, F32)], accs=[(D_MODEL, w)])


def _split3(x):
    hi = x.astype(BF16)
    r1 = x - hi.astype(F32)
    mid = r1.astype(BF16)
    lo = (r1 - mid.astype(F32)).astype(BF16)
    return hi, mid, lo


def _tri_sum(tri, x):
    hi, mid, lo = _split3(x)
    return _dot(tri, hi) + _dot(tri, mid) + _dot(tri, lo)


def _lower_bound(lb_ref):
    return 1.0 / (1.0 + jnp.exp(lb_ref[1:2, :] - lb_ref[0:1, :]))


def _hgrn_specs(n, t, reverse):
    nt = n // t
    width = HGRN_HEADS_PER_STEP * HEAD_DIM

    def tok(i):
        return nt - 1 - i if reverse else i

    def sec(col):
        c0 = col // width
        return pl.BlockSpec((t, width), lambda h, i: (tok(i), c0 + h))

    head_tile = pl.BlockSpec((t, width), lambda h, i: (tok(i), h))
    state = pl.BlockSpec((HGRN_HEADS_PER_STEP, t // CHUNK, HEAD_DIM, HEAD_DIM), lambda h, i: (h, tok(i), 0, 0))
    lb = pl.BlockSpec((2, width), lambda h, i: (0, h))
    return sec, head_tile, state, lb


def _hgrn_fwd(proj, hgrn_lb, *, name):
    n = proj.shape[0]
    t = _tile(n, 512, CHUNK)
    nc = t // CHUNK
    hps = HGRN_HEADS_PER_STEP
    width = hps * HEAD_DIM
    lanes = [slice(h * HEAD_DIM, (h + 1) * HEAD_DIM) for h in range(hps)]
    sec, head_tile, state, lbspec = _hgrn_specs(n, t, False)

    def body(q_ref, f_ref, i_ref, lb_ref, o_ref, st_ref, s_acc, g_s, a_s):
        @pl.when(pl.program_id(1) == 0)
        def _():
            s_acc[...] = jnp.zeros_like(s_acc)

        lb = _lower_bound(lb_ref)
        row = lax.broadcasted_iota(jnp.int32, (CHUNK, CHUNK), 0)
        col = lax.broadcasted_iota(jnp.int32, (CHUNK, CHUNK), 1)
        tril = row >= col
        trilb = jnp.where(tril, 1.0, 0.0).astype(BF16)
        rowk = lax.broadcasted_iota(jnp.int32, (CHUNK, width), 0)

        def chunk(c, carry):
            rows = pl.ds(pl.multiple_of(c * CHUNK, CHUNK), CHUNK)
            qr, fr, v = q_ref[rows, :], f_ref[rows, :], i_ref[rows, :]
            q = qr * _sigmoid(qr)
            f = lb + (1.0 - lb) * _sigmoid(fr)
            k = 1.0 - f
            g = _tri_sum(trilb, jnp.log(f))
            g_s[...] = g
            st0 = [s_acc[h] for h in range(hps)]
            for h in range(hps):
                st_ref[h, c] = st0[h]
            vb = v.astype(BF16)
            for blk in range(CHUNK // SUB):
                lo, hi = blk * SUB, (blk + 1) * SUB
                gref = g_s[lo - 1:lo, :] if blk else jnp.zeros((1, width), F32)
                qi = (q[lo:hi] * jnp.exp(g[lo:hi] - gref)).astype(BF16)
                ki = (k * jnp.exp(jnp.where(rowk < hi, gref - g, NEG_BIG))).astype(BF16)
                for h, ln in enumerate(lanes):
                    a_s[h, lo:hi, :] = _dot_nt(qi[:, ln], ki[:, ln])
            qeb = (q * jnp.exp(g)).astype(BF16)
            o_ref[rows, :] = jnp.concatenate(
                [_dot(jnp.where(tril, a_s[h], 0.0).astype(BF16), vb[:, ln]) + _dot_nt(qeb[:, ln], st0[h].astype(BF16))
                 for h, ln in enumerate(lanes)], axis=1)
            glast = g_s[CHUNK - 1:CHUNK, :]
            kdb = (k * jnp.exp(glast - g)).astype(BF16)
            dec = jnp.exp(glast)
            for h, ln in enumerate(lanes):
                s_acc[h] = st0[h] * dec[:, ln] + _dot_tn(vb[:, ln], kdb[:, ln])
            return carry

        lax.fori_loop(0, nc, chunk, 0)

    return pl.pallas_call(
        body, name=name, grid=(HEADS // hps, n // t),
        in_specs=[sec(COL_Q), sec(COL_F), sec(COL_I), lbspec], out_specs=[head_tile, state],
        out_shape=[jax.ShapeDtypeStruct((n, D_MODEL), F32),
                   jax.ShapeDtypeStruct((HEADS, n // CHUNK, HEAD_DIM, HEAD_DIM), F32)],
        scratch_shapes=[pltpu.VMEM((hps, HEAD_DIM, HEAD_DIM), F32), pltpu.VMEM((CHUNK, width), F32),
                        pltpu.VMEM((hps, CHUNK, CHUNK), F32)],
        compiler_params=_params("parallel", "arbitrary"),
    )(proj, proj, proj, hgrn_lb)


def _hgrn_bwd(proj, hgrn_lb, do, states, *, name):
    n = proj.shape[0]
    t = _tile(n, 512, CHUNK)
    nc = t // CHUNK
    hps = HGRN_HEADS_PER_STEP
    width = hps * HEAD_DIM
    lanes = [slice(h * HEAD_DIM, (h + 1) * HEAD_DIM) for h in range(hps)]
    sec, head_tile, state, lbspec = _hgrn_specs(n, t, True)

    def body(q_ref, f_ref, i_ref, lb_ref, do_ref, st_ref, dq_ref, df_ref, di_ref, dlb_ref, d_acc, g_s, a_s, dq_s,
             dg_s):
        first = pl.program_id(1) == 0

        @pl.when(first)
        def _():
            d_acc[...] = jnp.zeros_like(d_acc)

        lb = _lower_bound(lb_ref)
        row = lax.broadcasted_iota(jnp.int32, (CHUNK, CHUNK), 0)
        col = lax.broadcasted_iota(jnp.int32, (CHUNK, CHUNK), 1)
        tril = row >= col
        trilb = jnp.where(tril, 1.0, 0.0).astype(BF16)
        triub = jnp.where(row <= col, 1.0, 0.0).astype(BF16)
        rowk = lax.broadcasted_iota(jnp.int32, (CHUNK, width), 0)

        def per_head(fn):
            return jnp.concatenate([fn(h, ln) for h, ln in enumerate(lanes)], axis=1)

        def chunk(j, dlb):
            c = nc - 1 - j
            rows = pl.ds(pl.multiple_of(c * CHUNK, CHUNK), CHUNK)
            qr, fr, v, dout = q_ref[rows, :], f_ref[rows, :], i_ref[rows, :], do_ref[rows, :]
            sq = _sigmoid(qr)
            q = qr * sq
            sf = _sigmoid(fr)
            f = lb + (1.0 - lb) * sf
            k = 1.0 - f
            g = _tri_sum(trilb, jnp.log(f))
            g_s[...] = g
            st0 = [st_ref[h, c] for h in range(hps)]
            dt = [d_acc[h] for h in range(hps)]
            vb, dob = v.astype(BF16), dout.astype(BF16)
            dtb = [x.astype(BF16) for x in dt]
            st0b = [x.astype(BF16) for x in st0]
            glast = g_s[CHUNK - 1:CHUNK, :]
            eg = jnp.exp(g)
            kdec = jnp.exp(glast - g)
            qeb, kdb = (q * eg).astype(BF16), (k * kdec).astype(BF16)
            aps = [jnp.where(row > col, _dot_nt(dob[:, ln], vb[:, ln]), 0.0) for ln in lanes]
            dov = dout * v
            adiag = per_head(lambda h, ln: jnp.broadcast_to(
                jnp.sum(dov[:, ln], axis=-1, keepdims=True), (CHUNK, HEAD_DIM)))
            dq_inter = per_head(lambda h, ln: _dot(dob[:, ln], st0b[h]))
            dk_inter = per_head(lambda h, ln: _dot(vb[:, ln], dtb[h]))
            dk_st = kdec * dk_inter
            dg = qeb.astype(F32) * dq_inter
            dg_minus = kdb.astype(F32) * dk_inter
            dg = dg - dg_minus
            for blk in range(CHUNK // SUB):
                lo, hi = blk * SUB, (blk + 1) * SUB
                gref = g_s[lo - 1:lo, :] if blk else jnp.zeros((1, width), F32)
                qscale = jnp.exp(g[lo:hi] - gref)
                kscale = jnp.exp(jnp.where(rowk < hi, gref - g, NEG_BIG))
                qi = (q[lo:hi] * qscale).astype(BF16)
                ki = (k * kscale).astype(BF16)
                for h, ln in enumerate(lanes):
                    a_s[h, lo:hi, :] = _dot_nt(qi[:, ln], ki[:, ln])
                apb = [x[lo:hi].astype(BF16) for x in aps]
                from_k = per_head(lambda h, ln: _dot(apb[h], ki[:, ln]))
                from_q = per_head(lambda h, ln: _dot_tn(apb[h], qi[:, ln]))
                dq_s[lo:hi, :] = qscale * from_k
                dg_s[lo:hi, :] = qi.astype(F32) * from_k
                dk_st = dk_st + kscale * from_q
                dg = dg - ki.astype(F32) * from_q
            dg = dg + dg_s[...]
            dv = per_head(lambda h, ln: _dot_tn(jnp.where(tril, a_s[h], 0.0).astype(BF16), dob[:, ln])
                          + _dot_nt(kdb[:, ln], dtb[h]))
            dq_st = dq_s[...] + eg * dq_inter
            dq = dq_st + adiag * k
            dk = dk_st + adiag * q
            dec = jnp.exp(glast)
            dt_dec = [dt[h] * dec[:, ln] for h, ln in enumerate(lanes)]
            for h, ln in enumerate(lanes):
                d_acc[h] = dt_dec[h] + _dot_tn(dob[:, ln], qeb[:, ln])
            later = per_head(lambda h, ln: _colsum(dt_dec[h] * st0[h])) + _colsum(dg_minus)
            dlf = later + _tri_sum(triub, dg)
            df = dlf / f - dk
            dq_ref[rows, :] = (dq * (sq * (1.0 + qr * (1.0 - sq)))).astype(dq_ref.dtype)
            df_ref[rows, :] = (df * (1.0 - lb) * sf * (1.0 - sf)).astype(df_ref.dtype)
            di_ref[rows, :] = dv.astype(di_ref.dtype)
            return dlb + _colsum(df * (1.0 - sf))

        dlb = lax.fori_loop(0, nc, chunk, jnp.zeros((1, width), F32))
        _accumulate(dlb_ref, dlb, first)

    sect = jax.ShapeDtypeStruct((n, D_MODEL), BF16)
    return pl.pallas_call(
        body, name=name, grid=(HEADS // hps, n // t),
        in_specs=[sec(COL_Q), sec(COL_F), sec(COL_I), lbspec, head_tile, state],
        out_specs=[head_tile, head_tile, head_tile, pl.BlockSpec((1, width), lambda h, i: (0, h))],
        out_shape=[sect, sect, sect, jax.ShapeDtypeStruct((1, D_MODEL), F32)],
        scratch_shapes=[pltpu.VMEM((hps, HEAD_DIM, HEAD_DIM), F32), pltpu.VMEM((CHUNK, width), F32),
                        pltpu.VMEM((hps, CHUNK, CHUNK), F32), pltpu.VMEM((CHUNK, width), F32),
                        pltpu.VMEM((CHUNK, width), F32)],
        compiler_params=_params("parallel", "arbitrary"),
    )(proj, proj, proj, hgrn_lb, do, states)


def _pool_fwd(proj, pool_w, pool_scale, *, name):
    n = proj.shape[0]
    t = _tile(n, 512, POOL_HALO)
    per = t // POOL_HALO
    c0 = COL_POOL // POOL_WIDTH

    def body(u_ref, halo_ref, pw_ref, ps_ref, pooled_ref, mixed_ref, ext):
        i = pl.program_id(0)
        u = u_ref[...]
        ext[POOL_HALO:POOL_HALO + t, :] = u
        ext[0:POOL_HALO, :] = jnp.where(i > 0, halo_ref[...], 0.0)
        pos = i * t + lax.broadcasted_iota(jnp.int32, (t, POOL_CH), 0) + 1
        for grp, win in enumerate(POOL_WINDOWS):
            cols = slice(grp * POOL_CH, (grp + 1) * POOL_CH)
            acc = u[:, cols]
            for j in range(1, win):
                acc = acc + ext[POOL_HALO - j:POOL_HALO - j + t, cols]
            pooled = (acc / jnp.minimum(pos, win).astype(F32) - u[:, cols]).astype(BF16)
            pooled_ref[:, cols] = pooled
            mixed_ref[:, cols] = (_dot(pooled, pw_ref[grp].astype(BF16)) * ps_ref[:, cols]).astype(BF16)

    tile = pl.BlockSpec((t, POOL_WIDTH), lambda i: (i, 0))
    return pl.pallas_call(
        body, name=name, grid=(n // t,),
        in_specs=[pl.BlockSpec((t, POOL_WIDTH), lambda i: (i, c0)),
                  pl.BlockSpec((POOL_HALO, POOL_WIDTH), lambda i: (jnp.maximum(i * per - 1, 0), c0)),
                  pl.BlockSpec((len(POOL_WINDOWS), POOL_CH, POOL_CH), lambda i: (0, 0, 0)),
                  pl.BlockSpec((1, POOL_WIDTH), lambda i: (0, 0))],
        out_specs=[tile, tile],
        out_shape=[jax.ShapeDtypeStruct((n, POOL_WIDTH), BF16), jax.ShapeDtypeStruct((n, POOL_WIDTH), BF16)],
        scratch_shapes=[pltpu.VMEM((t + POOL_HALO, POOL_WIDTH), F32)],
        compiler_params=_params("parallel"),
    )(proj, proj, pool_w, pool_scale)


def _pool_bwd(dmixed, pooled, pool_w, pool_scale, *, name):
    n = dmixed.shape[0]
    t = _tile(n, 512, POOL_HALO)
    per = t // POOL_HALO
    nb = n // t

    def body(dm_ref, dmh_ref, p_ref, pw_ref, ps_ref, du_ref, dpw_ref, dps_ref, ext):
        i = pl.program_id(0)

        @pl.when(i == 0)
        def _():
            dpw_ref[...] = jnp.zeros_like(dpw_ref)
            dps_ref[...] = jnp.zeros_like(dps_ref)

        dm, dmh = dm_ref[...], dmh_ref[...]
        pos = i * t + lax.broadcasted_iota(jnp.int32, (t, POOL_CH), 0) + 1
        for grp, win in enumerate(POOL_WINDOWS):
            cols = slice(grp * POOL_CH, (grp + 1) * POOL_CH)
            pwb = pw_ref[grp].astype(BF16)
            pb = p_ref[:, cols]
            scale = ps_ref[:, cols]
            dps_ref[:, cols] += _colsum(dm[:, cols] * _dot(pb, pwb))
            dpm = (dm[:, cols] * scale).astype(BF16)
            dpw_ref[grp] += _dot_tn(pb, dpm)
            dpool = _dot_nt(dpm, pwb)
            dpool_next = _dot_nt((dmh[:, cols] * scale).astype(BF16), pwb)
            ext[0:t, cols] = dpool / jnp.minimum(pos, win).astype(F32)
            ext[t:t + POOL_HALO, cols] = jnp.where(i < nb - 1, dpool_next * (1.0 / win), 0.0)
            acc = -dpool
            for j in range(win):
                acc = acc + ext[j:j + t, cols]
            du_ref[:, cols] = acc.astype(du_ref.dtype)

    tile = pl.BlockSpec((t, POOL_WIDTH), lambda i: (i, 0))
    return pl.pallas_call(
        body, name=name, grid=(nb,),
        in_specs=[tile, pl.BlockSpec((POOL_HALO, POOL_WIDTH), lambda i: (jnp.minimum((i + 1) * per, nb * per - 1), 0)),
                  tile, pl.BlockSpec((len(POOL_WINDOWS), POOL_CH, POOL_CH), lambda i: (0, 0, 0)),
                  pl.BlockSpec((1, POOL_WIDTH), lambda i: (0, 0))],
        out_specs=[tile, pl.BlockSpec((len(POOL_WINDOWS), POOL_CH, POOL_CH), lambda i: (0, 0, 0)),
                   pl.BlockSpec((1, POOL_WIDTH), lambda i: (0, 0))],
        out_shape=[jax.ShapeDtypeStruct((n, POOL_WIDTH), BF16),
                   jax.ShapeDtypeStruct((len(POOL_WINDOWS), POOL_CH, POOL_CH), F32),
                   jax.ShapeDtypeStruct((1, POOL_WIDTH), F32)],
        scratch_shapes=[pltpu.VMEM((t + POOL_HALO, POOL_WIDTH), F32)],
        compiler_params=_params("arbitrary"),
    )(dmixed, dmixed, pooled, pool_w, pool_scale)


def _adamw(w, g, m, v):
    m2 = ADAM_B1 * m + (1.0 - ADAM_B1) * g
    v2 = ADAM_B2 * v + (1.0 - ADAM_B2) * (g * g)
    m_hat = m2 * (1.0 / (1.0 - ADAM_B1 ** ADAM_STEP))
    v_hat = v2 * (1.0 / (1.0 - ADAM_B2 ** ADAM_STEP))
    delta = -ADAM_LR * (m_hat / (jnp.sqrt(v_hat) + ADAM_EPS) + ADAM_WD * w)
    return delta, m2, v2


def _adam_big(recv, w, m, v, *, name):
    r, c = w.shape
    tr = _tile(r, 256, 16)

    def body(recv_ref, w_ref, m_ref, v_ref, g_ref, d_ref, m2_ref, v2_ref):
        g = recv_ref[0].astype(F32)
        for i in range(1, N_DEV):
            g = g + recv_ref[i].astype(F32)
        delta, m2, v2 = _adamw(w_ref[...], g, m_ref[...], v_ref[...])
        g_ref[...] = g
        d_ref[...] = delta
        m2_ref[...] = m2
        v2_ref[...] = v2

    tile = pl.BlockSpec((tr, c), lambda i: (i, 0))
    out = jax.ShapeDtypeStruct((r, c), F32)
    return pl.pallas_call(
        body, name=name, grid=(r // tr,),
        in_specs=[pl.BlockSpec((N_DEV, tr, c), lambda i: (0, i, 0)), tile, tile, tile],
        out_specs=[tile] * 4, out_shape=[out] * 4, compiler_params=_params("parallel"),
    )(recv, w, m, v)


def _adam_small(parts, w, m, v, *, name):
    lb0, lbn = SMALL_ROWS["hgrn_lb"]
    half = lbn // 2

    def body(parts_ref, w_ref, m_ref, v_ref, g_ref, d_ref, m2_ref, v2_ref):
        g = parts_ref[0]
        for i in range(1, N_DEV):
            g = g + parts_ref[i]
        w_ = w_ref[...]
        s0 = 1.0 / (1.0 + jnp.exp(w_[lb0 + half:lb0 + lbn] - w_[lb0:lb0 + half]))
        ga = g[lb0:lb0 + half] * s0 * (1.0 - s0)
        g = jnp.concatenate([g[:lb0], ga, -ga, g[lb0 + lbn:]], axis=0)
        delta, m2, v2 = _adamw(w_, g, m_ref[...], v_ref[...])
        g_ref[...] = g
        d_ref[...] = delta
        m2_ref[...] = m2
        v2_ref[...] = v2

    out = jax.ShapeDtypeStruct(w.shape, F32)
    return pl.pallas_call(body, name=name, out_shape=[out] * 4, compiler_params=_params())(parts, w, m, v)


def _pack_small(vals):
    pieces, at = [], 0
    for name, (row0, nrows) in SMALL_ROWS.items():
        if row0 > at:
            pieces.append(jnp.zeros((row0 - at, 128), F32))
        pieces.append(vals[name].astype(F32).reshape(nrows, 128))
        at = row0 + nrows
    if at < SMALL_TOTAL_ROWS:
        pieces.append(jnp.zeros((SMALL_TOTAL_ROWS - at, 128), F32))
    return jnp.concatenate(pieces, axis=0)


def _unpack_small(packed, shapes):
    return {name: packed[row0:row0 + nrows].reshape(shapes[name]) for name, (row0, nrows) in SMALL_ROWS.items()}


SPLIT_AXIS = dict(BIG_WEIGHTS)


def _gather_of(names, weights):
    return _Exchange([weights[k][0].astype(BF16) for k in names], gather=True)


def _scatter_of(names, dfull):
    return _Exchange([_to_slots(dfull[k], SPLIT_AXIS[k]) for k in names], gather=False)


def _to_slots(dw, axis):
    k, m = dw.shape
    if axis == 0:
        return dw.reshape(N_DEV, k // N_DEV, m).astype(BF16)
    return dw.reshape(k, N_DEV, m // N_DEV).transpose(1, 0, 2).astype(BF16)


def _from_slots(gathered, axis):
    _, r, c = gathered.shape
    if axis == 0:
        return gathered.reshape(N_DEV * r, c)
    return gathered.transpose(1, 0, 2).reshape(r, N_DEV * c)


def kernel(x, p, ffn1_norm, ffn1_w1, ffn1_w3, ffn1_w2, mix_norm, w_in, hgrn_lb, hgrn_onorm, w_branch_a, pool_w, pool_scale, w_branch_b, w_out, ffn2_norm, ffn2_w1, ffn2_w3, ffn2_w2, ple_norm, ple_w_gate, ple_w_proj, ple_post_norm, final_norm, loss_target, m_ffn1_norm, m_ffn1_w1, m_ffn1_w3, m_ffn1_w2, m_mix_norm, m_w_in, m_hgrn_lb, m_hgrn_onorm, m_w_branch_a, m_pool_w, m_pool_scale, m_w_branch_b, m_w_out, m_ffn2_norm, m_ffn2_w1, m_ffn2_w3, m_ffn2_w2, m_ple_norm, m_ple_w_gate, m_ple_w_proj, m_ple_post_norm, m_final_norm, v_ffn1_norm, v_ffn1_w1, v_ffn1_w3, v_ffn1_w2, v_mix_norm, v_w_in, v_hgrn_lb, v_hgrn_onorm, v_w_branch_a, v_pool_w, v_pool_scale, v_w_branch_b, v_w_out, v_ffn2_norm, v_ffn2_w1, v_ffn2_w3, v_ffn2_w2, v_ple_norm, v_ple_w_gate, v_ple_w_proj, v_ple_post_norm, v_final_norm):
    weights = dict(ffn1_norm=ffn1_norm, ffn1_w1=ffn1_w1, ffn1_w3=ffn1_w3, ffn1_w2=ffn1_w2, mix_norm=mix_norm, w_in=w_in, hgrn_lb=hgrn_lb, hgrn_onorm=hgrn_onorm, w_branch_a=w_branch_a, pool_w=pool_w, pool_scale=pool_scale, w_branch_b=w_branch_b, w_out=w_out, ffn2_norm=ffn2_norm, ffn2_w1=ffn2_w1, ffn2_w3=ffn2_w3, ffn2_w2=ffn2_w2, ple_norm=ple_norm, ple_w_gate=ple_w_gate, ple_w_proj=ple_w_proj, ple_post_norm=ple_post_norm, final_norm=final_norm)
    mom1 = dict(ffn1_norm=m_ffn1_norm, ffn1_w1=m_ffn1_w1, ffn1_w3=m_ffn1_w3, ffn1_w2=m_ffn1_w2, mix_norm=m_mix_norm, w_in=m_w_in, hgrn_lb=m_hgrn_lb, hgrn_onorm=m_hgrn_onorm, w_branch_a=m_w_branch_a, pool_w=m_pool_w, pool_scale=m_pool_scale, w_branch_b=m_w_branch_b, w_out=m_w_out, ffn2_norm=m_ffn2_norm, ffn2_w1=m_ffn2_w1, ffn2_w3=m_ffn2_w3, ffn2_w2=m_ffn2_w2, ple_norm=m_ple_norm, ple_w_gate=m_ple_w_gate, ple_w_proj=m_ple_w_proj, ple_post_norm=m_ple_post_norm, final_norm=m_final_norm)
    mom2 = dict(ffn1_norm=v_ffn1_norm, ffn1_w1=v_ffn1_w1, ffn1_w3=v_ffn1_w3, ffn1_w2=v_ffn1_w2, mix_norm=v_mix_norm, w_in=v_w_in, hgrn_lb=v_hgrn_lb, hgrn_onorm=v_hgrn_onorm, w_branch_a=v_w_branch_a, pool_w=v_pool_w, pool_scale=v_pool_scale, w_branch_b=v_w_branch_b, w_out=v_w_out, ffn2_norm=v_ffn2_norm, ffn2_w1=v_ffn2_w1, ffn2_w3=v_ffn2_w3, ffn2_w2=v_ffn2_w2, ple_norm=v_ple_norm, ple_w_gate=v_ple_w_gate, ple_w_proj=v_ple_w_proj, ple_post_norm=v_ple_post_norm, final_norm=v_final_norm)

    xs = x[0]
    ps = p[0, 0].astype(BF16)
    tgt = loss_target[0]
    n = xs.shape[0]

    g_f1, g_mix, g_on, g_f2 = ffn1_norm, mix_norm, hgrn_onorm, ffn2_norm
    g_ple, g_post, g_fin = ple_norm, ple_post_norm, final_norm.reshape(1, D_MODEL)
    lb2 = hgrn_lb
    pw, pscale = pool_w[0], pool_scale

    full = {}

    def keep(names, gathered):
        for k, g in zip(names, gathered):
            full[k] = _from_slots(g, SPLIT_AXIS[k])

    first_names = ("ffn1_w1", "ffn1_w3")
    keep(first_names, _exchange_now([weights[k][0].astype(BF16) for k in first_names], name="gather_first", gather=True))
    h1 = _rms_fwd(xs, g_f1, name="ffn1_rms")
    names = ("ffn1_w2", "w_in")
    ex = _gather_of(names, weights)
    a1, b1, s1 = _ffn_up(h1, full["ffn1_w1"], full["ffn1_w3"], name="ffn1_up", exchange=ex)
    keep(names, ex.received)
    names = ("w_branch_a", "w_branch_b", "w_out")
    ex = _gather_of(names, weights)
    x1 = _mm_nn(s1, full["ffn1_w2"], name="ffn1_down", tn=1024, tm=512, res=xs, scale=0.5, exchange=ex)
    keep(names, ex.received)
    h2 = _rms_fwd(x1, g_mix, name="mix_rms")
    names = ("ffn2_w1", "ffn2_w3", "ffn2_w2", "ple_w_gate", "ple_w_proj")
    ex = _gather_of(names, weights)
    proj = _mm_nn(h2, full["w_in"], name="w_in_proj", tn=1024, tm=512, exchange=ex)
    keep(names, ex.received)
    o, states = _hgrn_fwd(proj, lb2, name="hgrn_fwd")
    on = _hgrn_post_fwd(o, proj, g_on, name="hgrn_post_fwd")
    ya = _mm_nn(on, full["w_branch_a"], name="branch_a", tn=1024, tm=512)
    pooled, mixed = _pool_fwd(proj, pw, pscale, name="pool_fwd")
    yb = _mm_nn(mixed, full["w_branch_b"], name="branch_b", tn=1024, tm=512)
    y = _merge_fwd(proj, ya, yb, name="merge_fwd")
    x2 = _mm_nn(y, full["w_out"], name="w_out_proj", tn=1024, tm=512, res=x1)
    h3 = _rms_fwd(x2, g_f2, name="ffn2_rms")
    a2, b2, s2 = _ffn_up(h3, full["ffn2_w1"], full["ffn2_w3"], name="ffn2_up")
    x3 = _mm_nn(s2, full["ffn2_w2"], name="ffn2_down", tn=1024, tm=512, res=x2, scale=0.5)
    h4 = _rms_fwd(x3, g_ple, name="ple_rms")
    gpre = _mm_nn(h4, full["ple_w_gate"], name="ple_gate", tn=1024, tm=512)
    z = _mm_nn(ps, full["ple_w_proj"], name="ple_proj", tn=1024, tm=512)
    dx4, dpre, dz, loss_part, d_fin, d_post = _ple_final(x3, gpre, z, tgt, g_post, g_fin, name="ple_final")

    dfull, received = {}, {}

    def sent(names, exchange):
        received.update(zip(names, exchange.received))

    dfull["ple_w_proj"] = _mm_tn(ps, dz, name="d_ple_w_proj", tn=512, tm=1024)
    dfull["ple_w_gate"] = _mm_tn(h4, dpre, name="d_ple_w_gate", tn=512, tm=1024)
    dh4 = _mm_nt([(dpre, full["ple_w_gate"])], name="d_h4", tn=1024, tk=512)
    dx3, dx3s, d_ple = _rms_bwd_add(dh4, x3, dx4, g_ple, name="ple_rms_bwd", half_scale=0.5)

    names = ("ple_w_proj", "ple_w_gate")
    ex = _scatter_of(names, dfull)
    da2, db2 = _ffn_bwd_mid(dx3s, full["ffn2_w2"], a2, b2, name="ffn2_bwd_mid", exchange=ex)
    sent(names, ex)
    dfull["ffn2_w2"] = _mm_tn(s2, dx3s, name="ffn2_dw2", tn=512, tm=512)
    dfull["ffn2_w1"] = _mm_tn(h3, da2, name="ffn2_dw1", tn=512, tm=1408)
    dfull["ffn2_w3"] = _mm_tn(h3, db2, name="ffn2_dw3", tn=512, tm=1408)
    names = ("ffn2_w2", "ffn2_w1")
    ex = _scatter_of(names, dfull)
    dh3 = _mm_nt([(da2, full["ffn2_w1"]), (db2, full["ffn2_w3"])], name="ffn2_dh", tn=512, tk=512, exchange=ex)
    sent(names, ex)
    dx2, dx2b, d_f2 = _rms_bwd_add(dh3, x2, dx3, g_f2, name="ffn2_rms_bwd", half_scale=1.0)

    dfull["w_out"] = _mm_tn(y, dx2b, name="d_w_out", tn=512, tm=1024)
    dy = _mm_nt([(dx2b, full["w_out"])], name="d_y", tn=1024, tk=512)
    dya, dyb, dga, dgb = _merge_bwd(dy, proj, ya, yb, name="merge_bwd")

    dfull["w_branch_b"] = _mm_tn(mixed, dyb, name="d_w_branch_b", tn=512, tm=1024)
    dmixed = _mm_nt([(dyb, full["w_branch_b"])], name="d_mixed", tn=1024, tk=512)
    du, d_pw, d_ps = _pool_bwd(dmixed, pooled, pw, pscale, name="pool_bwd")

    dfull["w_branch_a"] = _mm_tn(on, dya, name="d_w_branch_a", tn=512, tm=1024)
    don = _mm_nt([(dya, full["w_branch_a"])], name="d_on", tn=1024, tk=512)
    dog, do, d_on = _hgrn_post_bwd(don, o, proj, g_on, name="hgrn_post_bwd")
    dqr, dfr, dir_, d_lb = _hgrn_bwd(proj, lb2, do, states, name="hgrn_bwd")

    dproj = jnp.concatenate([dqr, dfr, dir_, dog, du, dga, dgb], axis=1)
    names = ("ffn2_w3", "w_out", "w_branch_b", "w_branch_a")
    ex = _scatter_of(names, dfull)
    dh2 = _mm_nt([(dproj, full["w_in"])], name="d_h2", tn=512, tk=512, exchange=ex)
    sent(names, ex)
    dx1, dx1s, d_mix = _rms_bwd_add(dh2, x1, dx2, g_mix, name="mix_rms_bwd", half_scale=0.5)
    dfull["w_in"] = _mm_tn(h2, dproj, name="d_w_in", tn=512, tm=1664)

    names = ("w_in",)
    ex = _scatter_of(names, dfull)
    da1, db1 = _ffn_bwd_mid(dx1s, full["ffn1_w2"], a1, b1, name="ffn1_bwd_mid", exchange=ex)
    sent(names, ex)
    dfull["ffn1_w2"] = _mm_tn(s1, dx1s, name="ffn1_dw2", tn=512, tm=512)
    names = ("ffn1_w2",)
    ex = _scatter_of(names, dfull)
    dfull["ffn1_w1"] = _mm_tn(h1, da1, name="ffn1_dw1", tn=512, tm=1408, exchange=ex)
    sent(names, ex)
    names = ("ffn1_w1",)
    ex = _scatter_of(names, dfull)
    dfull["ffn1_w3"] = _mm_tn(h1, db1, name="ffn1_dw3", tn=512, tm=1408, exchange=ex)
    sent(names, ex)
    names = ("ffn1_w3",)
    ex = _scatter_of(names, dfull)
    dh1 = _mm_nt([(da1, full["ffn1_w1"]), (db1, full["ffn1_w3"])], name="ffn1_dh", tn=512, tk=512, exchange=ex)
    sent(names, ex)
    grad_x, _, d_f1 = _rms_bwd_add(dh1, xs, dx1, g_f1, name="ffn1_rms_bwd", half_scale=1.0)

    small_part = _pack_small(dict(
        ffn1_norm=d_f1, mix_norm=d_mix, hgrn_onorm=d_on, ffn2_norm=d_f2, ple_norm=d_ple, ple_post_norm=d_post,
        final_norm=d_fin, hgrn_lb=jnp.concatenate([d_lb, jnp.zeros_like(d_lb)], axis=0), pool_scale=d_ps, pool_w=d_pw))
    small_all = _exchange_now([small_part], name="gather_small_grads", gather=True)[0]

    grads, deltas, new_m, new_v = {}, {}, {}, {}
    for name, _ in BIG_WEIGHTS:
        shape, recv = weights[name].shape, received[name]
        res = _adam_big(recv, weights[name][0], mom1[name][0], mom2[name][0], name=f"adam_{name}")
        grads[name], deltas[name], new_m[name], new_v[name] = [r.reshape(shape) for r in res]
    shapes = {name: weights[name].shape for name in SMALL_ROWS}
    res = _adam_small(small_all, _pack_small(weights), _pack_small(mom1), _pack_small(mom2), name="adam_small")
    for store, packed in zip((grads, deltas, new_m, new_v), res):
        store.update(_unpack_small(packed, shapes))

    loss = lax.psum(jnp.sum(loss_part), ("x", "y", "c"))
    return (loss, grad_x.reshape(x.shape), *[grads[k] for k in WEIGHT_ORDER], *[deltas[k] for k in WEIGHT_ORDER],
            *[new_m[k] for k in WEIGHT_ORDER], *[new_v[k] for k in WEIGHT_ORDER])
```

```python
import jax
import jax.numpy as jnp
from jax import lax
from jax.experimental import pallas as pl
from jax.experimental.pallas import tpu as pltpu

F32 = jnp.float32
BF16 = jnp.bfloat16

N_DEV = 8
D_MODEL = 1024
HEADS = 8
HEAD_DIM = 128
POOL_WINDOWS = (2, 4, 8, 16)
POOL_CH = 128
POOL_WIDTH = 512
POOL_HALO = 16
RMS_EPS = 1e-6
CHUNK = 64
SUB = 32
HGRN_HEADS_PER_STEP = 8
NEG_BIG = -1e30

ADAM_LR = 0.001
ADAM_B1 = 0.9
ADAM_B2 = 0.999
ADAM_EPS = 1e-08
ADAM_WD = 0.01
ADAM_STEP = 10

V7X_VMEM_BYTES = 64 * 1024 * 1024
VMEM_LIMIT = (V7X_VMEM_BYTES * 3) // 4
ROW_TILE_CAP = 8192

COL_Q, COL_F, COL_I, COL_OG, COL_POOL, COL_GA, COL_GB = 0, 1024, 2048, 3072, 4096, 4608, 5632

BIG_WEIGHTS = (
    ("ffn1_w1", 1), ("ffn1_w3", 1), ("ffn1_w2", 0), ("w_in", 1), ("w_branch_a", 0), ("w_branch_b", 1),
    ("w_out", 0), ("ffn2_w1", 1), ("ffn2_w3", 1), ("ffn2_w2", 0), ("ple_w_gate", 0), ("ple_w_proj", 1),
)
SMALL_ROWS = {
    "ffn1_norm": (0, 8), "mix_norm": (8, 8), "hgrn_onorm": (16, 8), "ffn2_norm": (24, 8), "ple_norm": (32, 8),
    "ple_post_norm": (40, 8), "final_norm": (48, 8), "hgrn_lb": (56, 16), "pool_scale": (72, 4), "pool_w": (80, 512),
}
SMALL_TOTAL_ROWS = 592
WEIGHT_ORDER = (
    "ffn1_norm", "ffn1_w1", "ffn1_w3", "ffn1_w2", "mix_norm", "w_in", "hgrn_lb", "hgrn_onorm", "w_branch_a", "pool_w",
    "pool_scale", "w_branch_b", "w_out", "ffn2_norm", "ffn2_w1", "ffn2_w3", "ffn2_w2", "ple_norm", "ple_w_gate",
    "ple_w_proj", "ple_post_norm", "final_norm",
)


def _params(*sem):
    return pltpu.CompilerParams(dimension_semantics=sem if sem else None, vmem_limit_bytes=VMEM_LIMIT)


def _dot(a, b):
    return jnp.dot(a, b, preferred_element_type=F32)


def _dot_nt(a, b):
    return lax.dot_general(a, b, (((1,), (1,)), ((), ())), preferred_element_type=F32)


def _dot_tn(a, b):
    return lax.dot_general(a, b, (((0,), (0,)), ((), ())), preferred_element_type=F32)


def _sigmoid(x):
    return 0.5 * jnp.tanh(0.5 * x) + 0.5


def _tile(n, want, mult):
    if mult != 128:
        want = min(want, ROW_TILE_CAP)
    if n <= want:
        return n
    t = (want // mult) * mult
    while t > mult and n % t:
        t -= mult
    assert n % t == 0, (n, want, mult)
    return t


class _Exchange:
    COPIES = N_DEV - 1

    def __init__(self, arrs, gather):
        self.arrs, self.gather, self.n = list(arrs), gather, len(arrs)
        self.out_shape = [jax.ShapeDtypeStruct((N_DEV,) + (a.shape if gather else a.shape[1:]), a.dtype) for a in arrs]
        self.scratch = [pltpu.SemaphoreType.DMA((self.n * self.COPIES,)),
                        pltpu.SemaphoreType.DMA((self.n * self.COPIES,)), pltpu.SemaphoreType.DMA((self.n,))]
        self.received = None

    @staticmethod
    def _place():
        x, y, c = lax.axis_index("x"), lax.axis_index("y"), lax.axis_index("c")
        return x, y, c

    def _copy(self, a, k, src, dst, to, sems):
        s = a * self.COPIES + k
        return pltpu.make_async_remote_copy(src_ref=src, dst_ref=dst, send_sem=sems[0].at[s], recv_sem=sems[1].at[s],
                                            device_id=to, device_id_type=pl.DeviceIdType.MESH)

    def _gather_copies(self, ins, outs, sems):
        x, y, c = self._place()
        chips = [(1 - x, y), (x, 1 - y), (1 - x, 1 - y)]
        slot = lambda px, py, pc: 4 * px + 2 * py + pc
        first, passed, arrivals = [], [], []
        for a in range(self.n):
            mine = outs[a].at[slot(x, y, c)]
            first.append(self._copy(a, 0, ins[a], mine, (x, y, 1 - c), sems))
            arrivals.append(self._copy(a, 0, ins[a], outs[a].at[slot(x, y, 1 - c)], (x, y, 1 - c), sems))
            for j, (px, py) in enumerate(chips):
                first.append(self._copy(a, 1 + j, ins[a], mine, (px, py, c), sems))
                theirs = outs[a].at[slot(px, py, c)]
                passed.append((self._copy(a, 1 + j, ins[a], theirs, (px, py, c), sems),
                               self._copy(a, 4 + j, theirs, theirs, (x, y, 1 - c), sems)))
                arrivals.append(self._copy(a, 4 + j, ins[a], outs[a].at[slot(px, py, 1 - c)], (x, y, 1 - c), sems))
        return first, passed, arrivals

    def _scatter_copies(self, ins, outs, sems):
        x, y, c = self._place()
        me = 4 * x + 2 * y + c
        sends, arrivals = [], []
        for k in range(1, N_DEV):
            px = 1 - x if k & 4 else x
            py = 1 - y if k & 2 else y
            pc = 1 - c if k & 1 else c
            peer = 4 * px + 2 * py + pc
            for a in range(self.n):
                sends.append(self._copy(a, k - 1, ins[a].at[peer], outs[a].at[me], (px, py, pc), sems))
                arrivals.append(self._copy(a, k - 1, ins[a].at[peer], outs[a].at[peer], (px, py, pc), sems))
        return sends, arrivals

    def _local(self, ins, outs, sems):
        x, y, c = self._place()
        me = 4 * x + 2 * y + c
        return [pltpu.make_async_copy(ins[a] if self.gather else ins[a].at[me], outs[a].at[me], sems[2].at[a])
                for a in range(self.n)]

    def start(self, ins, outs, sems):
        for cp in self._local(ins, outs, sems):
            cp.start()
        sends = self._gather_copies(ins, outs, sems)[0] if self.gather else self._scatter_copies(ins, outs, sems)[0]
        for cp in sends:
            cp.start()

    def finish(self, ins, outs, sems):
        if self.gather:
            first, passed, arrivals = self._gather_copies(ins, outs, sems)
            for landed, onward in passed:
                landed.wait_recv()
                onward.start()
            sends = first + [onward for _, onward in passed]
        else:
            sends, arrivals = self._scatter_copies(ins, outs, sems)
        for cp in arrivals:
            cp.wait_recv()
        for cp in sends:
            cp.wait_send()
        for cp in self._local(ins, outs, sems):
            cp.wait()


def _call(body, *, name, grid, in_specs, out_specs, out_shape, args, semantics, scratch=(), exchange=None):
    if exchange is None:
        return pl.pallas_call(
            body, name=name, grid=grid, in_specs=in_specs, out_specs=out_specs, out_shape=out_shape,
            scratch_shapes=list(scratch), compiler_params=_params(*semantics))(*args)
    ex = exchange
    n_in, n_out, n_s = len(in_specs), len(out_specs), len(scratch)

    def wrapped(*refs):
        ins, ex_in = refs[:n_in], refs[n_in:n_in + ex.n]
        o0 = n_in + ex.n
        outs, ex_out = refs[o0:o0 + n_out], refs[o0 + n_out:o0 + n_out + ex.n]
        s0 = o0 + n_out + ex.n
        scr, sems = refs[s0:s0 + n_s], refs[s0 + n_s:]
        ids = [pl.program_id(ax) for ax in range(len(grid))]
        first = ids[0] == 0
        last = ids[0] == grid[0] - 1
        for ax in range(1, len(grid)):
            first = jnp.logical_and(first, ids[ax] == 0)
            last = jnp.logical_and(last, ids[ax] == grid[ax] - 1)

        @pl.when(first)
        def _():
            ex.start(ex_in, ex_out, sems)

        body(*ins, *outs, *scr)

        @pl.when(last)
        def _():
            ex.finish(ex_in, ex_out, sems)

    hbm = pl.BlockSpec(memory_space=pltpu.HBM)
    res = pl.pallas_call(
        wrapped, name=name, grid=grid, in_specs=list(in_specs) + [hbm] * ex.n,
        out_specs=list(out_specs) + [hbm] * ex.n, out_shape=list(out_shape) + ex.out_shape,
        scratch_shapes=list(scratch) + ex.scratch, compiler_params=_params(*(["arbitrary"] * len(grid))),
    )(*args, *ex.arrs)
    ex.received = res[n_out:]
    return res[:n_out]


def _exchange_now(arrs, *, name, gather):
    ex = _Exchange(arrs, gather)
    n = ex.n

    def body(*refs):
        ex.start(refs[:n], refs[n:2 * n], refs[2 * n:])
        ex.finish(refs[:n], refs[n:2 * n], refs[2 * n:])

    hbm = pl.BlockSpec(memory_space=pltpu.HBM)
    return pl.pallas_call(body, name=name, out_shape=ex.out_shape, in_specs=[hbm] * n, out_specs=[hbm] * n,
                          scratch_shapes=ex.scratch)(*arrs)


def _mm_nn(a, b, *, name, tn, tm, out_dtype=F32, res=None, scale=1.0, exchange=None):
    n, k = a.shape
    m = b.shape[1]
    tn, tm = _tile(n, tn, 16), _tile(m, tm, 128)

    def body(*refs):
        a_ref, b_ref = refs[0], refs[1]
        o_ref = refs[-1]
        acc = _dot(a_ref[...], b_ref[...])
        if scale != 1.0:
            acc = acc * scale
        if res is not None:
            acc = acc + refs[2][...]
        o_ref[...] = acc.astype(o_ref.dtype)

    in_specs = [pl.BlockSpec((tn, k), lambda i, j: (i, 0)), pl.BlockSpec((k, tm), lambda i, j: (0, j))]
    args = [a, b]
    if res is not None:
        in_specs.append(pl.BlockSpec((tn, tm), lambda i, j: (i, j)))
        args.append(res)
    return _call(body, name=name, grid=(n // tn, m // tm), in_specs=in_specs,
                 out_specs=[pl.BlockSpec((tn, tm), lambda i, j: (i, j))],
                 out_shape=[jax.ShapeDtypeStruct((n, m), out_dtype)], args=args, semantics=("parallel", "parallel"),
                 exchange=exchange)[0]


def _mm_nt(pairs, *, name, tn, tk, out_dtype=F32, exchange=None):
    n = pairs[0][0].shape[0]
    kk = pairs[0][1].shape[0]
    tn, tk = _tile(n, tn, 16), _tile(kk, tk, 128)
    npair = len(pairs)

    def body(*refs):
        o_ref = refs[-1]
        acc = _dot_nt(refs[0][...], refs[1][...])
        for q in range(1, npair):
            acc = acc + _dot_nt(refs[2 * q][...], refs[2 * q + 1][...])
        o_ref[...] = acc.astype(o_ref.dtype)

    in_specs, args = [], []
    for a, b in pairs:
        m = a.shape[1]
        in_specs += [pl.BlockSpec((tn, m), lambda i, j: (i, 0)), pl.BlockSpec((tk, m), lambda i, j: (j, 0))]
        args += [a, b]
    return _call(body, name=name, grid=(n // tn, kk // tk), in_specs=in_specs,
                 out_specs=[pl.BlockSpec((tn, tk), lambda i, j: (i, j))],
                 out_shape=[jax.ShapeDtypeStruct((n, kk), out_dtype)], args=args, semantics=("parallel", "parallel"),
                 exchange=exchange)[0]


def _mm_tn(a, b, *, name, tn, tm, exchange=None):
    n, k = a.shape
    m = b.shape[1]
    tn, tm = _tile(n, tn, 16), _tile(m, tm, 128)
    steps = n // tn

    def body(a_ref, b_ref, o_ref, acc):
        i = pl.program_id(1)

        @pl.when(i == 0)
        def _():
            acc[...] = jnp.zeros_like(acc)

        acc[...] += _dot_tn(a_ref[...], b_ref[...])

        @pl.when(i == steps - 1)
        def _():
            o_ref[...] = acc[...].astype(o_ref.dtype)

    return _call(body, name=name, grid=(m // tm, steps),
                 in_specs=[pl.BlockSpec((tn, k), lambda j, i: (i, 0)), pl.BlockSpec((tn, tm), lambda j, i: (i, j))],
                 out_specs=[pl.BlockSpec((k, tm), lambda j, i: (0, j))],
                 out_shape=[jax.ShapeDtypeStruct((k, m), BF16)], args=[a, b], semantics=("parallel", "arbitrary"),
                 scratch=[pltpu.VMEM((k, tm), F32)], exchange=exchange)[0]


def _ffn_up(h, w1, w3, *, name, exchange=None):
    n, k = h.shape
    m = w1.shape[1]
    tn, tm = _tile(n, 2048, 16), _tile(m, 256, 128)

    def body(h_ref, w1_ref, w3_ref, a_ref, b_ref, s_ref):
        hb = h_ref[...]
        a = _dot(hb, w1_ref[...])
        b = _dot(hb, w3_ref[...])
        a_ref[...] = a.astype(a_ref.dtype)
        b_ref[...] = b.astype(b_ref.dtype)
        s_ref[...] = (a * _sigmoid(a) * b).astype(s_ref.dtype)

    wspec = pl.BlockSpec((k, tm), lambda i, j: (0, j))
    ospec = pl.BlockSpec((tn, tm), lambda i, j: (i, j))
    return _call(body, name=name, grid=(n // tn, m // tm),
                 in_specs=[pl.BlockSpec((tn, k), lambda i, j: (i, 0)), wspec, wspec], out_specs=[ospec, ospec, ospec],
                 out_shape=[jax.ShapeDtypeStruct((n, m), BF16)] * 3,
                 args=[h, w1, w3], semantics=("parallel", "parallel"), exchange=exchange)


def _ffn_bwd_mid(dxs, w2, a, b, *, name, exchange=None):
    n, d = dxs.shape
    m = w2.shape[0]
    tn, tk = _tile(n, 1024, 16), _tile(m, 256, 128)

    def body(dx_ref, w2_ref, a_ref, b_ref, da_ref, db_ref):
        ds = _dot_nt(dx_ref[...], w2_ref[...])
        a_, b_ = a_ref[...].astype(F32), b_ref[...].astype(F32)
        sg = _sigmoid(a_)
        da_ref[...] = (ds * b_ * (sg * (1.0 + a_ * (1.0 - sg)))).astype(da_ref.dtype)
        db_ref[...] = (ds * (a_ * sg)).astype(db_ref.dtype)

    tile = pl.BlockSpec((tn, tk), lambda i, j: (i, j))
    return _call(body, name=name, grid=(n // tn, m // tk),
                 in_specs=[pl.BlockSpec((tn, d), lambda i, j: (i, 0)), pl.BlockSpec((tk, d), lambda i, j: (j, 0)),
                           tile, tile],
                 out_specs=[tile, tile],
                 out_shape=[jax.ShapeDtypeStruct((n, m), BF16), jax.ShapeDtypeStruct((n, m), BF16)],
                 args=[dxs, w2, a, b], semantics=("parallel", "parallel"), exchange=exchange)


def _rowwise(fn, *, name, n, tn, ncol, rows, vecs, outs, accs=()):
    tn = _tile(n, tn, 16)
    nr, nv, no = len(rows), len(vecs), len(outs)

    def body(*refs):
        first = pl.program_id(1) == 0
        vals = [r[...].astype(F32) for r in refs[:nr + nv]]
        res = fn(*vals)
        for ref, val in zip(refs[nr + nv:nr + nv + no], res[:no]):
            ref[...] = val.astype(ref.dtype)
        for ref, val in zip(refs[nr + nv + no:], res[no:]):
            _accumulate(ref, val, first)

    in_specs = [pl.BlockSpec((tn, w), lambda j, i, c0=c0: (i, c0 + j)) for _, w, c0 in rows]
    in_specs += [pl.BlockSpec((1, w), lambda j, i, c0=c0: (0, c0 + j)) for _, w, c0 in vecs]
    out_specs = [pl.BlockSpec((tn, w), lambda j, i: (i, j)) for _, w, _ in outs]
    out_specs += [pl.BlockSpec((1, w), lambda j, i: (0, j)) for _, w in accs]
    out_shape = [jax.ShapeDtypeStruct((n, tw), dt) for tw, _, dt in outs]
    out_shape += [jax.ShapeDtypeStruct((1, tw), F32) for tw, _ in accs]
    return pl.pallas_call(
        body, name=name, grid=(ncol, n // tn), in_specs=in_specs, out_specs=out_specs, out_shape=out_shape,
        compiler_params=_params("parallel", "arbitrary"),
    )(*[r[0] for r in rows], *[v[0] for v in vecs])


def _accumulate(ref, val, first):
    @pl.when(first)
    def _():
        ref[...] = jnp.zeros_like(ref)

    ref[...] += val


def _colsum(x):
    return jnp.sum(x, axis=0, keepdims=True)


def _rowmean(x):
    return jnp.mean(x, axis=-1, keepdims=True)


def _rms_fwd(x, g, *, name):
    def fn(x_, g_):
        r = lax.rsqrt(_rowmean(x_ * x_) + RMS_EPS)
        return (x_ * r * g_,)

    n, d = x.shape
    return _rowwise(fn, name=name, n=n, tn=512, ncol=1, rows=[(x, d, 0)], vecs=[(g, d, 0)], outs=[(d, d, BF16)])[0]


def _mm_nt_rms_bwd(pairs, x, extra, g, *, name, tn, half_scale, exchange=None):
    n, d = x.shape
    tn = _tile(n, tn, 16)
    npair = len(pairs)

    def body(*refs):
        x_ref, e_ref, g_ref, dx_ref, dxs_ref, dg_ref = refs[2 * npair:]
        dh = _dot_nt(refs[0][...], refs[1][...])
        for q in range(1, npair):
            dh = dh + _dot_nt(refs[2 * q][...], refs[2 * q + 1][...])
        x_ = x_ref[...]
        r = lax.rsqrt(_rowmean(x_ * x_) + RMS_EPS)
        xh = x_ * r
        dxh = dh * g_ref[...]
        dx = e_ref[...] + r * (dxh - xh * _rowmean(dxh * xh))
        dx_ref[...] = dx
        dxs_ref[...] = (dx * half_scale).astype(dxs_ref.dtype)
        _accumulate(dg_ref, _colsum(dh * xh), pl.program_id(0) == 0)

    in_specs, args = [], []
    for a, b in pairs:
        m = a.shape[1]
        in_specs += [pl.BlockSpec((tn, m), lambda i: (i, 0)),
                     pl.BlockSpec((d, m), lambda i: (0, 0), pipeline_mode=pl.Buffered(1))]
        args += [a, b]
    row = pl.BlockSpec((tn, d), lambda i: (i, 0))
    vec = pl.BlockSpec((1, d), lambda i: (0, 0))
    return _call(body, name=name, grid=(n // tn,), in_specs=in_specs + [row, row, vec], out_specs=[row, row, vec],
                 out_shape=[jax.ShapeDtypeStruct((n, d), F32), jax.ShapeDtypeStruct((n, d), BF16),
                            jax.ShapeDtypeStruct((1, d), F32)],
                 args=args + [x, extra, g], semantics=("arbitrary",), exchange=exchange)


def _ple_final(x3, gpre, z, tgt, gpp, gf, *, name):
    def fn(x3_, gpre_, z_, tgt_, gpp_, gf_):
        gate = _sigmoid(gpre_)
        rz = lax.rsqrt(_rowmean(z_ * z_) + RMS_EPS)
        zh = z_ * rz
        e = zh * gpp_
        x4 = x3_ + gate * e
        r4 = lax.rsqrt(_rowmean(x4 * x4) + RMS_EPS)
        x4h = x4 * r4
        diff = x4h * gf_ - tgt_
        dout = diff * (1.0 / D_MODEL)
        dxh4 = dout * gf_
        dx4 = r4 * (dxh4 - x4h * _rowmean(dxh4 * x4h))
        dpre = dx4 * e * gate * (1.0 - gate)
        de = dx4 * gate
        dzh = de * gpp_
        dz = rz * (dzh - zh * _rowmean(dzh * zh))
        return dx4, dpre, dz, _colsum(diff * diff) * (0.5 / D_MODEL), _colsum(dout * x4h), _colsum(de * zh)

    n, d = x3.shape
    return _rowwise(fn, name=name, n=n, tn=256, ncol=1, rows=[(x3, d, 0), (gpre, d, 0), (z, d, 0), (tgt, d, 0)],
                    vecs=[(gpp, d, 0), (gf, d, 0)], outs=[(d, d, F32), (d, d, BF16), (d, d, BF16)],
                    accs=[(d, d), (d, d), (d, d)])


def _merge_fwd(proj, ya, yb, *, name):
    def fn(ga, gb, ya_, yb_):
        return (_sigmoid(ga) * ya_ + _sigmoid(gb) * yb_,)

    n = proj.shape[0]
    w = 512
    return _rowwise(fn, name=name, n=n, tn=512, ncol=D_MODEL // w,
                    rows=[(proj, w, COL_GA // w), (proj, w, COL_GB // w), (ya, w, 0), (yb, w, 0)], vecs=[],
                    outs=[(D_MODEL, w, BF16)])[0]


def _merge_bwd(dy, proj, ya, yb, *, name):
    def fn(dy_, ga, gb, ya_, yb_):
        sa, sb = _sigmoid(ga), _sigmoid(gb)
        return dy_ * sa, dy_ * sb, dy_ * ya_ * sa * (1.0 - sa), dy_ * yb_ * sb * (1.0 - sb)

    n = proj.shape[0]
    w = 512
    return _rowwise(fn, name=name, n=n, tn=512, ncol=D_MODEL // w,
                    rows=[(dy, w, 0), (proj, w, COL_GA // w), (proj, w, COL_GB // w), (ya, w, 0), (yb, w, 0)],
                    vecs=[], outs=[(D_MODEL, w, BF16)] * 4)


def _hgrn_post_fwd(o, proj, onorm, *, name):
    def fn(o_, og, gam):
        r = lax.rsqrt(_rowmean(o_ * o_) + RMS_EPS)
        return (o_ * r * gam * (og * _sigmoid(og)),)

    n = o.shape[0]
    w = HEAD_DIM
    return _rowwise(fn, name=name, n=n, tn=1024, ncol=HEADS, rows=[(o, w, 0), (proj, w, COL_OG // w)],
                    vecs=[(onorm, w, 0)], outs=[(D_MODEL, w, BF16)])[0]


def _hgrn_post_bwd(don, o, proj, onorm, *, name):
    def fn(don_, o_, og, gam):
        r = lax.rsqrt(_rowmean(o_ * o_) + RMS_EPS)
        oh = o_ * r
        sg = _sigmoid(og)
        dog = don_ * oh * gam * (sg * (1.0 + og * (1.0 - sg)))
        dn = don_ * (og * sg)
        doh = dn * gam
        do = r * (doh - oh * _rowmean(doh * oh))
        return dog, do, _colsum(dn * oh)

    n = o.shape[0]
    w = HEAD_DIM
    return _rowwise(fn, name=name, n=n, tn=1024, ncol=HEADS, rows=[(don, w, 0), (o, w, 0), (proj, w, COL_OG // w)],
                    vecs=[(onorm, w, 0)], outs=[(D_MODEL, w, BF16), (D_MODEL, w, F32)], accs=[(D_MODEL, w)])


def _split3(x):
    hi = x.astype(BF16)
    r1 = x - hi.astype(F32)
    mid = r1.astype(BF16)
    lo = (r1 - mid.astype(F32)).astype(BF16)
    return hi, mid, lo


def _tri_sum(tri, x):
    hi, mid, lo = _split3(x)
    return _dot(tri, hi) + _dot(tri, mid) + _dot(tri, lo)


def _lower_bound(lb_ref):
    return 1.0 / (1.0 + jnp.exp(lb_ref[1:2, :] - lb_ref[0:1, :]))


def _hgrn_specs(n, t, reverse):
    nt = n // t
    width = HGRN_HEADS_PER_STEP * HEAD_DIM

    def tok(i):
        return nt - 1 - i if reverse else i

    def sec(col):
        c0 = col // width
        return pl.BlockSpec((t, width), lambda h, i: (tok(i), c0 + h))

    head_tile = pl.BlockSpec((t, width), lambda h, i: (tok(i), h))
    state = pl.BlockSpec((HGRN_HEADS_PER_STEP, t // CHUNK, HEAD_DIM, HEAD_DIM), lambda h, i: (h, tok(i), 0, 0))
    lb = pl.BlockSpec((2, width), lambda h, i: (0, h))
    return sec, head_tile, state, lb


def _hgrn_fwd(proj, hgrn_lb, *, name):
    n = proj.shape[0]
    t = _tile(n, 512, CHUNK)
    nc = t // CHUNK
    hps = HGRN_HEADS_PER_STEP
    width = hps * HEAD_DIM
    lanes = [slice(h * HEAD_DIM, (h + 1) * HEAD_DIM) for h in range(hps)]
    sec, head_tile, state, lbspec = _hgrn_specs(n, t, False)

    def body(q_ref, f_ref, i_ref, lb_ref, o_ref, st_ref, s_acc, g_s, a_s):
        @pl.when(pl.program_id(1) == 0)
        def _():
            s_acc[...] = jnp.zeros_like(s_acc)

        lb = _lower_bound(lb_ref)
        row = lax.broadcasted_iota(jnp.int32, (CHUNK, CHUNK), 0)
        col = lax.broadcasted_iota(jnp.int32, (CHUNK, CHUNK), 1)
        tril = row >= col
        trilb = jnp.where(tril, 1.0, 0.0).astype(BF16)
        rowk = lax.broadcasted_iota(jnp.int32, (CHUNK, width), 0)

        def chunk(c, carry):
            rows = pl.ds(pl.multiple_of(c * CHUNK, CHUNK), CHUNK)
            qr, fr, v = [r[rows, :].astype(F32) for r in (q_ref, f_ref, i_ref)]
            q = qr * _sigmoid(qr)
            f = lb + (1.0 - lb) * _sigmoid(fr)
            k = 1.0 - f
            g = _tri_sum(trilb, jnp.log(f))
            g_s[...] = g
            st0 = [s_acc[h] for h in range(hps)]
            for h in range(hps):
                st_ref[h, c] = st0[h]
            vb = v.astype(BF16)
            for blk in range(CHUNK // SUB):
                lo, hi = blk * SUB, (blk + 1) * SUB
                gref = g_s[lo - 1:lo, :] if blk else jnp.zeros((1, width), F32)
                qi = (q[lo:hi] * jnp.exp(g[lo:hi] - gref)).astype(BF16)
                ki = (k * jnp.exp(jnp.where(rowk < hi, gref - g, NEG_BIG))).astype(BF16)
                for h, ln in enumerate(lanes):
                    a_s[h, lo:hi, :] = _dot_nt(qi[:, ln], ki[:, ln])
            qeb = (q * jnp.exp(g)).astype(BF16)
            o_ref[rows, :] = jnp.concatenate(
                [_dot(jnp.where(tril, a_s[h], 0.0).astype(BF16), vb[:, ln]) + _dot_nt(qeb[:, ln], st0[h].astype(BF16))
                 for h, ln in enumerate(lanes)], axis=1)
            glast = g_s[CHUNK - 1:CHUNK, :]
            kdb = (k * jnp.exp(glast - g)).astype(BF16)
            dec = jnp.exp(glast)
            for h, ln in enumerate(lanes):
                s_acc[h] = st0[h] * dec[:, ln] + _dot_tn(vb[:, ln], kdb[:, ln])
            return carry

        lax.fori_loop(0, nc, chunk, 0)

    return pl.pallas_call(
        body, name=name, grid=(HEADS // hps, n // t),
        in_specs=[sec(COL_Q), sec(COL_F), sec(COL_I), lbspec], out_specs=[head_tile, state],
        out_shape=[jax.ShapeDtypeStruct((n, D_MODEL), F32),
                   jax.ShapeDtypeStruct((HEADS, n // CHUNK, HEAD_DIM, HEAD_DIM), F32)],
        scratch_shapes=[pltpu.VMEM((hps, HEAD_DIM, HEAD_DIM), F32), pltpu.VMEM((CHUNK, width), F32),
                        pltpu.VMEM((hps, CHUNK, CHUNK), F32)],
        compiler_params=_params("parallel", "arbitrary"),
    )(proj, proj, proj, hgrn_lb)


def _hgrn_bwd(proj, hgrn_lb, do, states, *, name):
    n = proj.shape[0]
    t = _tile(n, 512, CHUNK)
    nc = t // CHUNK
    hps = HGRN_HEADS_PER_STEP
    width = hps * HEAD_DIM
    lanes = [slice(h * HEAD_DIM, (h + 1) * HEAD_DIM) for h in range(hps)]
    sec, head_tile, state, lbspec = _hgrn_specs(n, t, True)

    def body(q_ref, f_ref, i_ref, lb_ref, do_ref, st_ref, dq_ref, df_ref, di_ref, dlb_ref, d_acc, g_s, a_s, dq_s,
             dg_s):
        first = pl.program_id(1) == 0

        @pl.when(first)
        def _():
            d_acc[...] = jnp.zeros_like(d_acc)

        lb = _lower_bound(lb_ref)
        row = lax.broadcasted_iota(jnp.int32, (CHUNK, CHUNK), 0)
        col = lax.broadcasted_iota(jnp.int32, (CHUNK, CHUNK), 1)
        tril = row >= col
        trilb = jnp.where(tril, 1.0, 0.0).astype(BF16)
        triub = jnp.where(row <= col, 1.0, 0.0).astype(BF16)
        rowk = lax.broadcasted_iota(jnp.int32, (CHUNK, width), 0)

        def per_head(fn):
            return jnp.concatenate([fn(h, ln) for h, ln in enumerate(lanes)], axis=1)

        def chunk(j, dlb):
            c = nc - 1 - j
            rows = pl.ds(pl.multiple_of(c * CHUNK, CHUNK), CHUNK)
            qr, fr, v, dout = [r[rows, :].astype(F32) for r in (q_ref, f_ref, i_ref, do_ref)]
            sq = _sigmoid(qr)
            q = qr * sq
            sf = _sigmoid(fr)
            f = lb + (1.0 - lb) * sf
            k = 1.0 - f
            g = _tri_sum(trilb, jnp.log(f))
            g_s[...] = g
            st0 = [st_ref[h, c] for h in range(hps)]
            dt = [d_acc[h] for h in range(hps)]
            vb, dob = v.astype(BF16), dout.astype(BF16)
            dtb = [x.astype(BF16) for x in dt]
            st0b = [x.astype(BF16) for x in st0]
            glast = g_s[CHUNK - 1:CHUNK, :]
            eg = jnp.exp(g)
            kdec = jnp.exp(glast - g)
            qeb, kdb = (q * eg).astype(BF16), (k * kdec).astype(BF16)
            aps = [jnp.where(row > col, _dot_nt(dob[:, ln], vb[:, ln]), 0.0) for ln in lanes]
            dov = dout * v
            adiag = per_head(lambda h, ln: jnp.broadcast_to(
                jnp.sum(dov[:, ln], axis=-1, keepdims=True), (CHUNK, HEAD_DIM)))
            dq_inter = per_head(lambda h, ln: _dot(dob[:, ln], st0b[h]))
            dk_inter = per_head(lambda h, ln: _dot(vb[:, ln], dtb[h]))
            dk_st = kdec * dk_inter
            dg = qeb.astype(F32) * dq_inter
            dg_minus = kdb.astype(F32) * dk_inter
            dg = dg - dg_minus
            for blk in range(CHUNK // SUB):
                lo, hi = blk * SUB, (blk + 1) * SUB
                gref = g_s[lo - 1:lo, :] if blk else jnp.zeros((1, width), F32)
                qscale = jnp.exp(g[lo:hi] - gref)
                kscale = jnp.exp(jnp.where(rowk < hi, gref - g, NEG_BIG))
                qi = (q[lo:hi] * qscale).astype(BF16)
                ki = (k * kscale).astype(BF16)
                for h, ln in enumerate(lanes):
                    a_s[h, lo:hi, :] = _dot_nt(qi[:, ln], ki[:, ln])
                apb = [x[lo:hi].astype(BF16) for x in aps]
                from_k = per_head(lambda h, ln: _dot(apb[h], ki[:, ln]))
                from_q = per_head(lambda h, ln: _dot_tn(apb[h], qi[:, ln]))
                dq_s[lo:hi, :] = qscale * from_k
                dg_s[lo:hi, :] = qi.astype(F32) * from_k
                dk_st = dk_st + kscale * from_q
                dg = dg - ki.astype(F32) * from_q
            dg = dg + dg_s[...]
            dv = per_head(lambda h, ln: _dot_tn(jnp.where(tril, a_s[h], 0.0).astype(BF16), dob[:, ln])
                          + _dot_nt(kdb[:, ln], dtb[h]))
            dq_st = dq_s[...] + eg * dq_inter
            dq = dq_st + adiag * k
            dk = dk_st + adiag * q
            dec = jnp.exp(glast)
            dt_dec = [dt[h] * dec[:, ln] for h, ln in enumerate(lanes)]
            for h, ln in enumerate(lanes):
                d_acc[h] = dt_dec[h] + _dot_tn(dob[:, ln], qeb[:, ln])
            later = per_head(lambda h, ln: _colsum(dt_dec[h] * st0[h])) + _colsum(dg_minus)
            dlf = later + _tri_sum(triub, dg)
            df = dlf / f - dk
            dq_ref[rows, :] = (dq * (sq * (1.0 + qr * (1.0 - sq)))).astype(dq_ref.dtype)
            df_ref[rows, :] = (df * (1.0 - lb) * sf * (1.0 - sf)).astype(df_ref.dtype)
            di_ref[rows, :] = dv.astype(di_ref.dtype)
            return dlb + _colsum(df * (1.0 - sf))

        dlb = lax.fori_loop(0, nc, chunk, jnp.zeros((1, width), F32))
        _accumulate(dlb_ref, dlb, first)

    sect = jax.ShapeDtypeStruct((n, D_MODEL), BF16)
    return pl.pallas_call(
        body, name=name, grid=(HEADS // hps, n // t),
        in_specs=[sec(COL_Q), sec(COL_F), sec(COL_I), lbspec, head_tile, state],
        out_specs=[head_tile, head_tile, head_tile, pl.BlockSpec((1, width), lambda h, i: (0, h))],
        out_shape=[sect, sect, sect, jax.ShapeDtypeStruct((1, D_MODEL), F32)],
        scratch_shapes=[pltpu.VMEM((hps, HEAD_DIM, HEAD_DIM), F32), pltpu.VMEM((CHUNK, width), F32),
                        pltpu.VMEM((hps, CHUNK, CHUNK), F32), pltpu.VMEM((CHUNK, width), F32),
                        pltpu.VMEM((CHUNK, width), F32)],
        compiler_params=_params("parallel", "arbitrary"),
    )(proj, proj, proj, hgrn_lb, do, states)


def _pool_fwd(proj, pool_w, pool_scale, *, name):
    n = proj.shape[0]
    t = _tile(n, 512, POOL_HALO)
    per = t // POOL_HALO
    c0 = COL_POOL // POOL_WIDTH

    def body(u_ref, halo_ref, pw_ref, ps_ref, pooled_ref, mixed_ref, ext):
        i = pl.program_id(0)
        u = u_ref[...].astype(F32)
        ext[POOL_HALO:POOL_HALO + t, :] = u
        ext[0:POOL_HALO, :] = jnp.where(i > 0, halo_ref[...].astype(F32), 0.0)
        pos = i * t + lax.broadcasted_iota(jnp.int32, (t, POOL_CH), 0) + 1
        for grp, win in enumerate(POOL_WINDOWS):
            cols = slice(grp * POOL_CH, (grp + 1) * POOL_CH)
            acc = u[:, cols]
            for j in range(1, win):
                acc = acc + ext[POOL_HALO - j:POOL_HALO - j + t, cols]
            pooled = (acc / jnp.minimum(pos, win).astype(F32) - u[:, cols]).astype(BF16)
            pooled_ref[:, cols] = pooled
            mixed_ref[:, cols] = (_dot(pooled, pw_ref[grp].astype(BF16)) * ps_ref[:, cols]).astype(BF16)

    tile = pl.BlockSpec((t, POOL_WIDTH), lambda i: (i, 0))
    return pl.pallas_call(
        body, name=name, grid=(n // t,),
        in_specs=[pl.BlockSpec((t, POOL_WIDTH), lambda i: (i, c0)),
                  pl.BlockSpec((POOL_HALO, POOL_WIDTH), lambda i: (jnp.maximum(i * per - 1, 0), c0)),
                  pl.BlockSpec((len(POOL_WINDOWS), POOL_CH, POOL_CH), lambda i: (0, 0, 0)),
                  pl.BlockSpec((1, POOL_WIDTH), lambda i: (0, 0))],
        out_specs=[tile, tile],
        out_shape=[jax.ShapeDtypeStruct((n, POOL_WIDTH), BF16), jax.ShapeDtypeStruct((n, POOL_WIDTH), BF16)],
        scratch_shapes=[pltpu.VMEM((t + POOL_HALO, POOL_WIDTH), F32)],
        compiler_params=_params("parallel"),
    )(proj, proj, pool_w, pool_scale)


def _pool_bwd(dmixed, pooled, pool_w, pool_scale, *, name):
    n = dmixed.shape[0]
    t = _tile(n, 512, POOL_HALO)
    per = t // POOL_HALO
    nb = n // t

    def body(dm_ref, dmh_ref, p_ref, pw_ref, ps_ref, du_ref, dpw_ref, dps_ref, ext):
        i = pl.program_id(0)

        @pl.when(i == 0)
        def _():
            dpw_ref[...] = jnp.zeros_like(dpw_ref)
            dps_ref[...] = jnp.zeros_like(dps_ref)

        dm, dmh = dm_ref[...], dmh_ref[...]
        pos = i * t + lax.broadcasted_iota(jnp.int32, (t, POOL_CH), 0) + 1
        for grp, win in enumerate(POOL_WINDOWS):
            cols = slice(grp * POOL_CH, (grp + 1) * POOL_CH)
            pwb = pw_ref[grp].astype(BF16)
            pb = p_ref[:, cols]
            scale = ps_ref[:, cols]
            dps_ref[:, cols] += _colsum(dm[:, cols] * _dot(pb, pwb))
            dpm = (dm[:, cols] * scale).astype(BF16)
            dpw_ref[grp] += _dot_tn(pb, dpm)
            dpool = _dot_nt(dpm, pwb)
            dpool_next = _dot_nt((dmh[:, cols] * scale).astype(BF16), pwb)
            ext[0:t, cols] = dpool / jnp.minimum(pos, win).astype(F32)
            ext[t:t + POOL_HALO, cols] = jnp.where(i < nb - 1, dpool_next * (1.0 / win), 0.0)
            acc = -dpool
            for j in range(win):
                acc = acc + ext[j:j + t, cols]
            du_ref[:, cols] = acc.astype(du_ref.dtype)

    tile = pl.BlockSpec((t, POOL_WIDTH), lambda i: (i, 0))
    return pl.pallas_call(
        body, name=name, grid=(nb,),
        in_specs=[tile, pl.BlockSpec((POOL_HALO, POOL_WIDTH), lambda i: (jnp.minimum((i + 1) * per, nb * per - 1), 0)),
                  tile, pl.BlockSpec((len(POOL_WINDOWS), POOL_CH, POOL_CH), lambda i: (0, 0, 0)),
                  pl.BlockSpec((1, POOL_WIDTH), lambda i: (0, 0))],
        out_specs=[tile, pl.BlockSpec((len(POOL_WINDOWS), POOL_CH, POOL_CH), lambda i: (0, 0, 0)),
                   pl.BlockSpec((1, POOL_WIDTH), lambda i: (0, 0))],
        out_shape=[jax.ShapeDtypeStruct((n, POOL_WIDTH), BF16),
                   jax.ShapeDtypeStruct((len(POOL_WINDOWS), POOL_CH, POOL_CH), F32),
                   jax.ShapeDtypeStruct((1, POOL_WIDTH), F32)],
        scratch_shapes=[pltpu.VMEM((t + POOL_HALO, POOL_WIDTH), F32)],
        compiler_params=_params("arbitrary"),
    )(dmixed, dmixed, pooled, pool_w, pool_scale)


def _adamw(w, g, m, v):
    m2 = ADAM_B1 * m + (1.0 - ADAM_B1) * g
    v2 = ADAM_B2 * v + (1.0 - ADAM_B2) * (g * g)
    m_hat = m2 * (1.0 / (1.0 - ADAM_B1 ** ADAM_STEP))
    v_hat = v2 * (1.0 / (1.0 - ADAM_B2 ** ADAM_STEP))
    delta = -ADAM_LR * (m_hat / (jnp.sqrt(v_hat) + ADAM_EPS) + ADAM_WD * w)
    return delta, m2, v2


def _adam_big(recv, w, m, v, *, name):
    r, c = w.shape
    tr = _tile(r, 256, 16)

    def body(recv_ref, w_ref, m_ref, v_ref, g_ref, d_ref, m2_ref, v2_ref):
        g = recv_ref[0].astype(F32)
        for i in range(1, N_DEV):
            g = g + recv_ref[i].astype(F32)
        delta, m2, v2 = _adamw(w_ref[...], g, m_ref[...], v_ref[...])
        g_ref[...] = g
        d_ref[...] = delta
        m2_ref[...] = m2
        v2_ref[...] = v2

    tile = pl.BlockSpec((tr, c), lambda i: (i, 0))
    out = jax.ShapeDtypeStruct((r, c), F32)
    return pl.pallas_call(
        body, name=name, grid=(r // tr,),
        in_specs=[pl.BlockSpec((N_DEV, tr, c), lambda i: (0, i, 0)), tile, tile, tile],
        out_specs=[tile] * 4, out_shape=[out] * 4, compiler_params=_params("parallel"),
    )(recv, w, m, v)


def _adam_small(parts, w, m, v, *, name):
    lb0, lbn = SMALL_ROWS["hgrn_lb"]
    half = lbn // 2

    def body(parts_ref, w_ref, m_ref, v_ref, g_ref, d_ref, m2_ref, v2_ref):
        g = parts_ref[0]
        for i in range(1, N_DEV):
            g = g + parts_ref[i]
        w_ = w_ref[...]
        s0 = 1.0 / (1.0 + jnp.exp(w_[lb0 + half:lb0 + lbn] - w_[lb0:lb0 + half]))
        ga = g[lb0:lb0 + half] * s0 * (1.0 - s0)
        g = jnp.concatenate([g[:lb0], ga, -ga, g[lb0 + lbn:]], axis=0)
        delta, m2, v2 = _adamw(w_, g, m_ref[...], v_ref[...])
        g_ref[...] = g
        d_ref[...] = delta
        m2_ref[...] = m2
        v2_ref[...] = v2

    out = jax.ShapeDtypeStruct(w.shape, F32)
    return pl.pallas_call(body, name=name, out_shape=[out] * 4, compiler_params=_params())(parts, w, m, v)


def _pack_small(vals):
    pieces, at = [], 0
    for name, (row0, nrows) in SMALL_ROWS.items():
        if row0 > at:
            pieces.append(jnp.zeros((row0 - at, 128), F32))
        pieces.append(vals[name].astype(F32).reshape(nrows, 128))
        at = row0 + nrows
    if at < SMALL_TOTAL_ROWS:
        pieces.append(jnp.zeros((SMALL_TOTAL_ROWS - at, 128), F32))
    return jnp.concatenate(pieces, axis=0)


def _unpack_small(packed, shapes):
    return {name: packed[row0:row0 + nrows].reshape(shapes[name]) for name, (row0, nrows) in SMALL_ROWS.items()}


SPLIT_AXIS = dict(BIG_WEIGHTS)


def _gather_of(names, weights):
    return _Exchange([weights[k][0].astype(BF16) for k in names], gather=True)


def _scatter_of(names, dfull):
    return _Exchange([_to_slots(dfull[k], SPLIT_AXIS[k]) for k in names], gather=False)


def _to_slots(dw, axis):
    k, m = dw.shape
    if axis == 0:
        return dw.reshape(N_DEV, k // N_DEV, m)
    return dw.reshape(k, N_DEV, m // N_DEV).transpose(1, 0, 2)


def _from_slots(gathered, axis):
    _, r, c = gathered.shape
    if axis == 0:
        return gathered.reshape(N_DEV * r, c)
    return gathered.transpose(1, 0, 2).reshape(r, N_DEV * c)


def kernel(x, p, ffn1_norm, ffn1_w1, ffn1_w3, ffn1_w2, mix_norm, w_in, hgrn_lb, hgrn_onorm, w_branch_a, pool_w, pool_scale, w_branch_b, w_out, ffn2_norm, ffn2_w1, ffn2_w3, ffn2_w2, ple_norm, ple_w_gate, ple_w_proj, ple_post_norm, final_norm, loss_target, m_ffn1_norm, m_ffn1_w1, m_ffn1_w3, m_ffn1_w2, m_mix_norm, m_w_in, m_hgrn_lb, m_hgrn_onorm, m_w_branch_a, m_pool_w, m_pool_scale, m_w_branch_b, m_w_out, m_ffn2_norm, m_ffn2_w1, m_ffn2_w3, m_ffn2_w2, m_ple_norm, m_ple_w_gate, m_ple_w_proj, m_ple_post_norm, m_final_norm, v_ffn1_norm, v_ffn1_w1, v_ffn1_w3, v_ffn1_w2, v_mix_norm, v_w_in, v_hgrn_lb, v_hgrn_onorm, v_w_branch_a, v_pool_w, v_pool_scale, v_w_branch_b, v_w_out, v_ffn2_norm, v_ffn2_w1, v_ffn2_w3, v_ffn2_w2, v_ple_norm, v_ple_w_gate, v_ple_w_proj, v_ple_post_norm, v_final_norm):
    weights = dict(ffn1_norm=ffn1_norm, ffn1_w1=ffn1_w1, ffn1_w3=ffn1_w3, ffn1_w2=ffn1_w2, mix_norm=mix_norm, w_in=w_in, hgrn_lb=hgrn_lb, hgrn_onorm=hgrn_onorm, w_branch_a=w_branch_a, pool_w=pool_w, pool_scale=pool_scale, w_branch_b=w_branch_b, w_out=w_out, ffn2_norm=ffn2_norm, ffn2_w1=ffn2_w1, ffn2_w3=ffn2_w3, ffn2_w2=ffn2_w2, ple_norm=ple_norm, ple_w_gate=ple_w_gate, ple_w_proj=ple_w_proj, ple_post_norm=ple_post_norm, final_norm=final_norm)
    mom1 = dict(ffn1_norm=m_ffn1_norm, ffn1_w1=m_ffn1_w1, ffn1_w3=m_ffn1_w3, ffn1_w2=m_ffn1_w2, mix_norm=m_mix_norm, w_in=m_w_in, hgrn_lb=m_hgrn_lb, hgrn_onorm=m_hgrn_onorm, w_branch_a=m_w_branch_a, pool_w=m_pool_w, pool_scale=m_pool_scale, w_branch_b=m_w_branch_b, w_out=m_w_out, ffn2_norm=m_ffn2_norm, ffn2_w1=m_ffn2_w1, ffn2_w3=m_ffn2_w3, ffn2_w2=m_ffn2_w2, ple_norm=m_ple_norm, ple_w_gate=m_ple_w_gate, ple_w_proj=m_ple_w_proj, ple_post_norm=m_ple_post_norm, final_norm=m_final_norm)
    mom2 = dict(ffn1_norm=v_ffn1_norm, ffn1_w1=v_ffn1_w1, ffn1_w3=v_ffn1_w3, ffn1_w2=v_ffn1_w2, mix_norm=v_mix_norm, w_in=v_w_in, hgrn_lb=v_hgrn_lb, hgrn_onorm=v_hgrn_onorm, w_branch_a=v_w_branch_a, pool_w=v_pool_w, pool_scale=v_pool_scale, w_branch_b=v_w_branch_b, w_out=v_w_out, ffn2_norm=v_ffn2_norm, ffn2_w1=v_ffn2_w1, ffn2_w3=v_ffn2_w3, ffn2_w2=v_ffn2_w2, ple_norm=v_ple_norm, ple_w_gate=v_ple_w_gate, ple_w_proj=v_ple_w_proj, ple_post_norm=v_ple_post_norm, final_norm=v_final_norm)

    xs = x[0]
    ps = p[0, 0].astype(BF16)
    tgt = loss_target[0]
    n = xs.shape[0]

    g_f1, g_mix, g_on, g_f2 = ffn1_norm, mix_norm, hgrn_onorm, ffn2_norm
    g_ple, g_post, g_fin = ple_norm, ple_post_norm, final_norm.reshape(1, D_MODEL)
    lb2 = hgrn_lb
    pw, pscale = pool_w[0], pool_scale

    full = {}

    def keep(names, gathered):
        for k, g in zip(names, gathered):
            full[k] = _from_slots(g, SPLIT_AXIS[k])

    first_names = ("ffn1_w1", "ffn1_w3")
    keep(first_names, _exchange_now([weights[k][0].astype(BF16) for k in first_names], name="gather_first", gather=True))
    h1 = _rms_fwd(xs, g_f1, name="ffn1_rms")
    names = ("ffn1_w2", "w_in")
    ex = _gather_of(names, weights)
    a1, b1, s1 = _ffn_up(h1, full["ffn1_w1"], full["ffn1_w3"], name="ffn1_up", exchange=ex)
    keep(names, ex.received)
    names = ("w_branch_a", "w_branch_b", "w_out")
    ex = _gather_of(names, weights)
    x1 = _mm_nn(s1, full["ffn1_w2"], name="ffn1_down", tn=1024, tm=512, res=xs, scale=0.5, exchange=ex)
    keep(names, ex.received)
    h2 = _rms_fwd(x1, g_mix, name="mix_rms")
    names = ("ffn2_w1", "ffn2_w3", "ffn2_w2", "ple_w_gate", "ple_w_proj")
    ex = _gather_of(names, weights)
    proj = _mm_nn(h2, full["w_in"], name="w_in_proj", tn=2048, tm=512, out_dtype=BF16, exchange=ex)
    keep(names, ex.received)
    o, states = _hgrn_fwd(proj, lb2, name="hgrn_fwd")
    on = _hgrn_post_fwd(o, proj, g_on, name="hgrn_post_fwd")
    ya = _mm_nn(on, full["w_branch_a"], name="branch_a", tn=1024, tm=512)
    pooled, mixed = _pool_fwd(proj, pw, pscale, name="pool_fwd")
    yb = _mm_nn(mixed, full["w_branch_b"], name="branch_b", tn=1024, tm=512)
    y = _merge_fwd(proj, ya, yb, name="merge_fwd")
    x2 = _mm_nn(y, full["w_out"], name="w_out_proj", tn=1024, tm=512, res=x1)
    h3 = _rms_fwd(x2, g_f2, name="ffn2_rms")
    a2, b2, s2 = _ffn_up(h3, full["ffn2_w1"], full["ffn2_w3"], name="ffn2_up")
    x3 = _mm_nn(s2, full["ffn2_w2"], name="ffn2_down", tn=1024, tm=512, res=x2, scale=0.5)
    h4 = _rms_fwd(x3, g_ple, name="ple_rms")
    gpre = _mm_nn(h4, full["ple_w_gate"], name="ple_gate", tn=1024, tm=512)
    z = _mm_nn(ps, full["ple_w_proj"], name="ple_proj", tn=1024, tm=512)
    dx4, dpre, dz, loss_part, d_fin, d_post = _ple_final(x3, gpre, z, tgt, g_post, g_fin, name="ple_final")

    dfull, received = {}, {}

    def sent(names, exchange):
        received.update(zip(names, exchange.received))

    dfull["ple_w_proj"] = _mm_tn(ps, dz, name="d_ple_w_proj", tn=512, tm=1024)
    dfull["ple_w_gate"] = _mm_tn(h4, dpre, name="d_ple_w_gate", tn=512, tm=1024)
    dx3, dx3s, d_ple = _mm_nt_rms_bwd([(dpre, full["ple_w_gate"])], x3, dx4, g_ple, name="ple_rms_bwd", tn=512,
                                      half_scale=0.5)

    names = ("ple_w_proj", "ple_w_gate")
    ex = _scatter_of(names, dfull)
    da2, db2 = _ffn_bwd_mid(dx3s, full["ffn2_w2"], a2, b2, name="ffn2_bwd_mid", exchange=ex)
    sent(names, ex)
    dfull["ffn2_w2"] = _mm_tn(s2, dx3s, name="ffn2_dw2", tn=512, tm=512)
    dfull["ffn2_w1"] = _mm_tn(h3, da2, name="ffn2_dw1", tn=512, tm=1408)
    dfull["ffn2_w3"] = _mm_tn(h3, db2, name="ffn2_dw3", tn=512, tm=1408)
    names = ("ffn2_w2", "ffn2_w1")
    ex = _scatter_of(names, dfull)
    dx2, dx2b, d_f2 = _mm_nt_rms_bwd([(da2, full["ffn2_w1"]), (db2, full["ffn2_w3"])], x2, dx3, g_f2,
                                     name="ffn2_rms_bwd", tn=512, half_scale=1.0, exchange=ex)
    sent(names, ex)

    dfull["w_out"] = _mm_tn(y, dx2b, name="d_w_out", tn=512, tm=1024)
    dy = _mm_nt([(dx2b, full["w_out"])], name="d_y", tn=1024, tk=512)
    dya, dyb, dga, dgb = _merge_bwd(dy, proj, ya, yb, name="merge_bwd")

    dfull["w_branch_b"] = _mm_tn(mixed, dyb, name="d_w_branch_b", tn=512, tm=1024)
    dmixed = _mm_nt([(dyb, full["w_branch_b"])], name="d_mixed", tn=1024, tk=512)
    du, d_pw, d_ps = _pool_bwd(dmixed, pooled, pw, pscale, name="pool_bwd")

    dfull["w_branch_a"] = _mm_tn(on, dya, name="d_w_branch_a", tn=512, tm=1024)
    don = _mm_nt([(dya, full["w_branch_a"])], name="d_on", tn=1024, tk=512)
    dog, do, d_on = _hgrn_post_bwd(don, o, proj, g_on, name="hgrn_post_bwd")
    dqr, dfr, dir_, d_lb = _hgrn_bwd(proj, lb2, do, states, name="hgrn_bwd")

    dproj = jnp.concatenate([dqr, dfr, dir_, dog, du, dga, dgb], axis=1)
    names = ("ffn2_w3", "w_out", "w_branch_b", "w_branch_a")
    ex = _scatter_of(names, dfull)
    dx1, dx1s, d_mix = _mm_nt_rms_bwd([(dproj, full["w_in"])], x1, dx2, g_mix, name="mix_rms_bwd", tn=512,
                                      half_scale=0.5, exchange=ex)
    sent(names, ex)
    dfull["w_in"] = _mm_tn(h2, dproj, name="d_w_in", tn=512, tm=1664)

    names = ("w_in",)
    ex = _scatter_of(names, dfull)
    da1, db1 = _ffn_bwd_mid(dx1s, full["ffn1_w2"], a1, b1, name="ffn1_bwd_mid", exchange=ex)
    sent(names, ex)
    dfull["ffn1_w2"] = _mm_tn(s1, dx1s, name="ffn1_dw2", tn=512, tm=512)
    names = ("ffn1_w2",)
    ex = _scatter_of(names, dfull)
    dfull["ffn1_w1"] = _mm_tn(h1, da1, name="ffn1_dw1", tn=512, tm=1408, exchange=ex)
    sent(names, ex)
    names = ("ffn1_w1",)
    ex = _scatter_of(names, dfull)
    dfull["ffn1_w3"] = _mm_tn(h1, db1, name="ffn1_dw3", tn=512, tm=1408, exchange=ex)
    sent(names, ex)
    names = ("ffn1_w3",)
    ex = _scatter_of(names, dfull)
    grad_x, _, d_f1 = _mm_nt_rms_bwd([(da1, full["ffn1_w1"]), (db1, full["ffn1_w3"])], xs, dx1, g_f1,
                                     name="ffn1_rms_bwd", tn=512, half_scale=1.0, exchange=ex)
    sent(names, ex)

    small_part = _pack_small(dict(
        ffn1_norm=d_f1, mix_norm=d_mix, hgrn_onorm=d_on, ffn2_norm=d_f2, ple_norm=d_ple, ple_post_norm=d_post,
        final_norm=d_fin, hgrn_lb=jnp.concatenate([d_lb, jnp.zeros_like(d_lb)], axis=0), pool_scale=d_ps, pool_w=d_pw))
    small_all = _exchange_now([small_part], name="gather_small_grads", gather=True)[0]

    grads, deltas, new_m, new_v = {}, {}, {}, {}
    for name, _ in BIG_WEIGHTS:
        shape, recv = weights[name].shape, received[name]
        res = _adam_big(recv, weights[name][0], mom1[name][0], mom2[name][0], name=f"adam_{name}")
        grads[name], deltas[name], new_m[name], new_v[name] = [r.reshape(shape) for r in res]
    shapes = {name: weights[name].shape for name in SMALL_ROWS}
    res = _adam_small(small_all, _pack_small(weights), _pack_small(mom1), _pack_small(mom2), name="adam_small")
    for store, packed in zip((grads, deltas, new_m, new_v), res):
        store.update(_unpack_small(packed, shapes))

    loss = lax.psum(jnp.sum(loss_part), ("x", "y", "c"))
    return (loss, grad_x.reshape(x.shape), *[grads[k] for k in WEIGHT_ORDER], *[deltas[k] for k in WEIGHT_ORDER],
            *[new_m[k] for k in WEIGHT_ORDER], *[new_v[k] for k in WEIGHT_ORDER])
```

```python
import jax
import jax.numpy as jnp
from jax import lax
from jax.experimental import pallas as pl
from jax.experimental.pallas import tpu as pltpu

F32 = jnp.float32
BF16 = jnp.bfloat16

N_DEV = 8
D_MODEL = 1024
HEADS = 8
HEAD_DIM = 128
POOL_WINDOWS = (2, 4, 8, 16)
POOL_CH = 128
POOL_WIDTH = 512
POOL_HALO = 16
RMS_EPS = 1e-6
CHUNK = 64
SUB = 32
HGRN_HEADS_PER_STEP = 8
NEG_BIG = -1e30

ADAM_LR = 0.001
ADAM_B1 = 0.9
ADAM_B2 = 0.999
ADAM_EPS = 1e-08
ADAM_WD = 0.01
ADAM_STEP = 10

V7X_VMEM_BYTES = 64 * 1024 * 1024
VMEM_LIMIT = (V7X_VMEM_BYTES * 3) // 4
ROW_TILE_CAP = 8192

COL_Q, COL_F, COL_I, COL_OG, COL_POOL, COL_GA, COL_GB = 0, 1024, 2048, 3072, 4096, 4608, 5632

BIG_WEIGHTS = (
    ("ffn1_w1", 1), ("ffn1_w3", 1), ("ffn1_w2", 0), ("w_in", 1), ("w_branch_a", 0), ("w_branch_b", 1),
    ("w_out", 0), ("ffn2_w1", 1), ("ffn2_w3", 1), ("ffn2_w2", 0), ("ple_w_gate", 0), ("ple_w_proj", 1),
)
SMALL_ROWS = {
    "ffn1_norm": (0, 8), "mix_norm": (8, 8), "hgrn_onorm": (16, 8), "ffn2_norm": (24, 8), "ple_norm": (32, 8),
    "ple_post_norm": (40, 8), "final_norm": (48, 8), "hgrn_lb": (56, 16), "pool_scale": (72, 4), "pool_w": (80, 512),
}
SMALL_TOTAL_ROWS = 592
WEIGHT_ORDER = (
    "ffn1_norm", "ffn1_w1", "ffn1_w3", "ffn1_w2", "mix_norm", "w_in", "hgrn_lb", "hgrn_onorm", "w_branch_a", "pool_w",
    "pool_scale", "w_branch_b", "w_out", "ffn2_norm", "ffn2_w1", "ffn2_w3", "ffn2_w2", "ple_norm", "ple_w_gate",
    "ple_w_proj", "ple_post_norm", "final_norm",
)


def _params(*sem):
    return pltpu.CompilerParams(dimension_semantics=sem if sem else None, vmem_limit_bytes=VMEM_LIMIT)


def _dot(a, b):
    return jnp.dot(a, b, preferred_element_type=F32)


def _dot_nt(a, b):
    return lax.dot_general(a, b, (((1,), (1,)), ((), ())), preferred_element_type=F32)


def _dot_tn(a, b):
    return lax.dot_general(a, b, (((0,), (0,)), ((), ())), preferred_element_type=F32)


def _sigmoid(x):
    return 0.5 * jnp.tanh(0.5 * x) + 0.5


def _tile(n, want, mult):
    if mult != 128:
        want = min(want, ROW_TILE_CAP)
    if n <= want:
        return n
    t = (want // mult) * mult
    while t > mult and n % t:
        t -= mult
    assert n % t == 0, (n, want, mult)
    return t


class _Exchange:
    COPIES = N_DEV - 1

    def __init__(self, arrs, gather):
        self.arrs, self.gather, self.n = list(arrs), gather, len(arrs)
        self.out_shape = [jax.ShapeDtypeStruct((N_DEV,) + (a.shape if gather else a.shape[1:]), a.dtype) for a in arrs]
        self.scratch = [pltpu.SemaphoreType.DMA((self.n * self.COPIES,)),
                        pltpu.SemaphoreType.DMA((self.n * self.COPIES,)), pltpu.SemaphoreType.DMA((self.n,))]
        self.received = None

    @staticmethod
    def _place():
        x, y, c = lax.axis_index("x"), lax.axis_index("y"), lax.axis_index("c")
        return x, y, c

    def _copy(self, a, k, src, dst, to, sems):
        s = a * self.COPIES + k
        return pltpu.make_async_remote_copy(src_ref=src, dst_ref=dst, send_sem=sems[0].at[s], recv_sem=sems[1].at[s],
                                            device_id=to, device_id_type=pl.DeviceIdType.MESH)

    def _gather_copies(self, ins, outs, sems):
        x, y, c = self._place()
        chips = [(1 - x, y), (x, 1 - y), (1 - x, 1 - y)]
        slot = lambda px, py, pc: 4 * px + 2 * py + pc
        first, passed, arrivals = [], [], []
        for a in range(self.n):
            mine = outs[a].at[slot(x, y, c)]
            first.append(self._copy(a, 0, ins[a], mine, (x, y, 1 - c), sems))
            arrivals.append(self._copy(a, 0, ins[a], outs[a].at[slot(x, y, 1 - c)], (x, y, 1 - c), sems))
            for j, (px, py) in enumerate(chips):
                first.append(self._copy(a, 1 + j, ins[a], mine, (px, py, c), sems))
                theirs = outs[a].at[slot(px, py, c)]
                passed.append((self._copy(a, 1 + j, ins[a], theirs, (px, py, c), sems),
                               self._copy(a, 4 + j, theirs, theirs, (x, y, 1 - c), sems)))
                arrivals.append(self._copy(a, 4 + j, ins[a], outs[a].at[slot(px, py, 1 - c)], (x, y, 1 - c), sems))
        return first, passed, arrivals

    def _scatter_copies(self, ins, outs, sems):
        x, y, c = self._place()
        me = 4 * x + 2 * y + c
        sends, arrivals = [], []
        for k in range(1, N_DEV):
            px = 1 - x if k & 4 else x
            py = 1 - y if k & 2 else y
            pc = 1 - c if k & 1 else c
            peer = 4 * px + 2 * py + pc
            for a in range(self.n):
                sends.append(self._copy(a, k - 1, ins[a].at[peer], outs[a].at[me], (px, py, pc), sems))
                arrivals.append(self._copy(a, k - 1, ins[a].at[peer], outs[a].at[peer], (px, py, pc), sems))
        return sends, arrivals

    def _local(self, ins, outs, sems):
        x, y, c = self._place()
        me = 4 * x + 2 * y + c
        return [pltpu.make_async_copy(ins[a] if self.gather else ins[a].at[me], outs[a].at[me], sems[2].at[a])
                for a in range(self.n)]

    def start(self, ins, outs, sems):
        for cp in self._local(ins, outs, sems):
            cp.start()
        sends = self._gather_copies(ins, outs, sems)[0] if self.gather else self._scatter_copies(ins, outs, sems)[0]
        for cp in sends:
            cp.start()

    def finish(self, ins, outs, sems):
        if self.gather:
            first, passed, arrivals = self._gather_copies(ins, outs, sems)
            for landed, onward in passed:
                landed.wait_recv()
                onward.start()
            sends = first + [onward for _, onward in passed]
        else:
            sends, arrivals = self._scatter_copies(ins, outs, sems)
        for cp in arrivals:
            cp.wait_recv()
        for cp in sends:
            cp.wait_send()
        for cp in self._local(ins, outs, sems):
            cp.wait()


def _call(body, *, name, grid, in_specs, out_specs, out_shape, args, semantics, scratch=(), exchange=None):
    if exchange is None:
        return pl.pallas_call(
            body, name=name, grid=grid, in_specs=in_specs, out_specs=out_specs, out_shape=out_shape,
            scratch_shapes=list(scratch), compiler_params=_params(*semantics))(*args)
    ex = exchange
    n_in, n_out, n_s = len(in_specs), len(out_specs), len(scratch)

    def wrapped(*refs):
        ins, ex_in = refs[:n_in], refs[n_in:n_in + ex.n]
        o0 = n_in + ex.n
        outs, ex_out = refs[o0:o0 + n_out], refs[o0 + n_out:o0 + n_out + ex.n]
        s0 = o0 + n_out + ex.n
        scr, sems = refs[s0:s0 + n_s], refs[s0 + n_s:]
        ids = [pl.program_id(ax) for ax in range(len(grid))]
        first = ids[0] == 0
        last = ids[0] == grid[0] - 1
        for ax in range(1, len(grid)):
            first = jnp.logical_and(first, ids[ax] == 0)
            last = jnp.logical_and(last, ids[ax] == grid[ax] - 1)

        @pl.when(first)
        def _():
            ex.start(ex_in, ex_out, sems)

        body(*ins, *outs, *scr)

        @pl.when(last)
        def _():
            ex.finish(ex_in, ex_out, sems)

    hbm = pl.BlockSpec(memory_space=pltpu.HBM)
    res = pl.pallas_call(
        wrapped, name=name, grid=grid, in_specs=list(in_specs) + [hbm] * ex.n,
        out_specs=list(out_specs) + [hbm] * ex.n, out_shape=list(out_shape) + ex.out_shape,
        scratch_shapes=list(scratch) + ex.scratch, compiler_params=_params(*(["arbitrary"] * len(grid))),
    )(*args, *ex.arrs)
    ex.received = res[n_out:]
    return res[:n_out]


def _exchange_now(arrs, *, name, gather):
    ex = _Exchange(arrs, gather)
    n = ex.n

    def body(*refs):
        ex.start(refs[:n], refs[n:2 * n], refs[2 * n:])
        ex.finish(refs[:n], refs[n:2 * n], refs[2 * n:])

    hbm = pl.BlockSpec(memory_space=pltpu.HBM)
    return pl.pallas_call(body, name=name, out_shape=ex.out_shape, in_specs=[hbm] * n, out_specs=[hbm] * n,
                          scratch_shapes=ex.scratch)(*arrs)


def _mm_nn(a, b, *, name, tn, tm, out_dtype=F32, res=None, scale=1.0, exchange=None):
    n, k = a.shape
    m = b.shape[1]
    tn, tm = _tile(n, tn, 16), _tile(m, tm, 128)

    def body(*refs):
        a_ref, b_ref = refs[0], refs[1]
        o_ref = refs[-1]
        acc = _dot(a_ref[...], b_ref[...])
        if scale != 1.0:
            acc = acc * scale
        if res is not None:
            acc = acc + refs[2][...]
        o_ref[...] = acc.astype(o_ref.dtype)

    in_specs = [pl.BlockSpec((tn, k), lambda i, j: (i, 0)), pl.BlockSpec((k, tm), lambda i, j: (0, j))]
    args = [a, b]
    if res is not None:
        in_specs.append(pl.BlockSpec((tn, tm), lambda i, j: (i, j)))
        args.append(res)
    return _call(body, name=name, grid=(n // tn, m // tm), in_specs=in_specs,
                 out_specs=[pl.BlockSpec((tn, tm), lambda i, j: (i, j))],
                 out_shape=[jax.ShapeDtypeStruct((n, m), out_dtype)], args=args, semantics=("parallel", "parallel"),
                 exchange=exchange)[0]


def _mm_nt(pairs, *, name, tn, tk, out_dtype=F32, exchange=None):
    n = pairs[0][0].shape[0]
    kk = pairs[0][1].shape[0]
    tn, tk = _tile(n, tn, 16), _tile(kk, tk, 128)
    npair = len(pairs)

    def body(*refs):
        o_ref = refs[-1]
        acc = _dot_nt(refs[0][...], refs[1][...])
        for q in range(1, npair):
            acc = acc + _dot_nt(refs[2 * q][...], refs[2 * q + 1][...])
        o_ref[...] = acc.astype(o_ref.dtype)

    in_specs, args = [], []
    for a, b in pairs:
        m = a.shape[1]
        in_specs += [pl.BlockSpec((tn, m), lambda i, j: (i, 0)), pl.BlockSpec((tk, m), lambda i, j: (j, 0))]
        args += [a, b]
    return _call(body, name=name, grid=(n // tn, kk // tk), in_specs=in_specs,
                 out_specs=[pl.BlockSpec((tn, tk), lambda i, j: (i, j))],
                 out_shape=[jax.ShapeDtypeStruct((n, kk), out_dtype)], args=args, semantics=("parallel", "parallel"),
                 exchange=exchange)[0]


def _mm_tn(a, b, *, name, tn, tm, exchange=None):
    n, k = a.shape
    m = b.shape[1]
    tn, tm = _tile(n, tn, 16), _tile(m, tm, 128)
    steps = n // tn

    def body(a_ref, b_ref, o_ref, acc):
        i = pl.program_id(1)

        @pl.when(i == 0)
        def _():
            acc[...] = jnp.zeros_like(acc)

        acc[...] += _dot_tn(a_ref[...], b_ref[...])

        @pl.when(i == steps - 1)
        def _():
            o_ref[...] = acc[...].astype(o_ref.dtype)

    return _call(body, name=name, grid=(m // tm, steps),
                 in_specs=[pl.BlockSpec((tn, k), lambda j, i: (i, 0)), pl.BlockSpec((tn, tm), lambda j, i: (i, j))],
                 out_specs=[pl.BlockSpec((k, tm), lambda j, i: (0, j))],
                 out_shape=[jax.ShapeDtypeStruct((k, m), BF16)], args=[a, b], semantics=("parallel", "arbitrary"),
                 scratch=[pltpu.VMEM((k, tm), F32)], exchange=exchange)[0]


def _ffn_up(h, w1, w3, *, name, exchange=None):
    n, k = h.shape
    m = w1.shape[1]
    tn, tm = _tile(n, 2048, 16), _tile(m, 256, 128)

    def body(h_ref, w1_ref, w3_ref, dsda_ref, dsdb_ref, s_ref):
        hb = h_ref[...]
        a = _dot(hb, w1_ref[...])
        b = _dot(hb, w3_ref[...])
        sg = _sigmoid(a)
        silu = a * sg
        dsda_ref[...] = (b * (sg + silu * (1.0 - sg))).astype(dsda_ref.dtype)
        dsdb_ref[...] = silu.astype(dsdb_ref.dtype)
        s_ref[...] = (silu * b).astype(s_ref.dtype)

    wspec = pl.BlockSpec((k, tm), lambda i, j: (0, j))
    ospec = pl.BlockSpec((tn, tm), lambda i, j: (i, j))
    return _call(body, name=name, grid=(n // tn, m // tm),
                 in_specs=[pl.BlockSpec((tn, k), lambda i, j: (i, 0)), wspec, wspec], out_specs=[ospec, ospec, ospec],
                 out_shape=[jax.ShapeDtypeStruct((n, m), BF16)] * 3,
                 args=[h, w1, w3], semantics=("parallel", "parallel"), exchange=exchange)


def _ffn_bwd_mid(dxs, w2, dsda, dsdb, *, name, exchange=None):
    n, d = dxs.shape
    m = w2.shape[0]
    tn, tk = _tile(n, 1024, 16), _tile(m, 256, 128)

    def body(dx_ref, w2_ref, dsda_ref, dsdb_ref, da_ref, db_ref):
        ds = _dot_nt(dx_ref[...], w2_ref[...])
        da_ref[...] = (ds * dsda_ref[...].astype(F32)).astype(da_ref.dtype)
        db_ref[...] = (ds * dsdb_ref[...].astype(F32)).astype(db_ref.dtype)

    tile = pl.BlockSpec((tn, tk), lambda i, j: (i, j))
    return _call(body, name=name, grid=(n // tn, m // tk),
                 in_specs=[pl.BlockSpec((tn, d), lambda i, j: (i, 0)), pl.BlockSpec((tk, d), lambda i, j: (j, 0)),
                           tile, tile],
                 out_specs=[tile, tile],
                 out_shape=[jax.ShapeDtypeStruct((n, m), BF16), jax.ShapeDtypeStruct((n, m), BF16)],
                 args=[dxs, w2, dsda, dsdb], semantics=("parallel", "parallel"), exchange=exchange)


def _rowwise(fn, *, name, n, tn, ncol, rows, vecs, outs, accs=()):
    tn = _tile(n, tn, 16)
    nr, nv, no = len(rows), len(vecs), len(outs)

    def body(*refs):
        first = pl.program_id(1) == 0
        vals = [r[...].astype(F32) for r in refs[:nr + nv]]
        res = fn(*vals)
        for ref, val in zip(refs[nr + nv:nr + nv + no], res[:no]):
            ref[...] = val.astype(ref.dtype)
        for ref, val in zip(refs[nr + nv + no:], res[no:]):
            _accumulate(ref, val, first)

    in_specs = [pl.BlockSpec((tn, w), lambda j, i, c0=c0: (i, c0 + j)) for _, w, c0 in rows]
    in_specs += [pl.BlockSpec((1, w), lambda j, i, c0=c0: (0, c0 + j)) for _, w, c0 in vecs]
    out_specs = [pl.BlockSpec((tn, w), lambda j, i: (i, j)) for _, w, _ in outs]
    out_specs += [pl.BlockSpec((1, w), lambda j, i: (0, j)) for _, w in accs]
    out_shape = [jax.ShapeDtypeStruct((n, tw), dt) for tw, _, dt in outs]
    out_shape += [jax.ShapeDtypeStruct((1, tw), F32) for tw, _ in accs]
    return pl.pallas_call(
        body, name=name, grid=(ncol, n // tn), in_specs=in_specs, out_specs=out_specs, out_shape=out_shape,
        compiler_params=_params("parallel", "arbitrary"),
    )(*[r[0] for r in rows], *[v[0] for v in vecs])


def _accumulate(ref, val, first):
    @pl.when(first)
    def _():
        ref[...] = jnp.zeros_like(ref)

    ref[...] += val


def _colsum(x):
    return jnp.sum(x, axis=0, keepdims=True)


def _rowmean(x):
    return jnp.mean(x, axis=-1, keepdims=True)


def _rms_fwd(x, g, *, name):
    def fn(x_, g_):
        r = lax.rsqrt(_rowmean(x_ * x_) + RMS_EPS)
        return (x_ * r * g_,)

    n, d = x.shape
    return _rowwise(fn, name=name, n=n, tn=512, ncol=1, rows=[(x, d, 0)], vecs=[(g, d, 0)], outs=[(d, d, BF16)])[0]


def _mm_nt_rms_bwd(pairs, x, extra, g, *, name, tn, half_scale, exchange=None):
    n, d = x.shape
    tn = _tile(n, tn, 16)
    pairs = [(list(a) if isinstance(a, (list, tuple)) else [a], b) for a, b in pairs]
    nref = sum(len(a) + 1 for a, _ in pairs)

    def body(*refs):
        x_ref, e_ref, g_ref, dx_ref, dxs_ref, dg_ref = refs[nref:]
        dh, at = None, 0
        for parts, _ in pairs:
            b_ref = refs[at + len(parts)]
            col = 0
            for j, part in enumerate(parts):
                w = part.shape[1]
                term = _dot_nt(refs[at + j][...], b_ref[:, col:col + w])
                dh = term if dh is None else dh + term
                col += w
            at += len(parts) + 1
        x_ = x_ref[...]
        r = lax.rsqrt(_rowmean(x_ * x_) + RMS_EPS)
        xh = x_ * r
        dxh = dh * g_ref[...]
        dx = e_ref[...] + r * (dxh - xh * _rowmean(dxh * xh))
        dx_ref[...] = dx
        dxs_ref[...] = (dx * half_scale).astype(dxs_ref.dtype)
        _accumulate(dg_ref, _colsum(dh * xh), pl.program_id(0) == 0)

    in_specs, args = [], []
    for parts, b in pairs:
        assert sum(part.shape[1] for part in parts) == b.shape[1]
        in_specs += [pl.BlockSpec((tn, part.shape[1]), lambda i: (i, 0)) for part in parts]
        in_specs.append(pl.BlockSpec(b.shape, lambda i: (0, 0), pipeline_mode=pl.Buffered(1)))
        args += parts + [b]
    row = pl.BlockSpec((tn, d), lambda i: (i, 0))
    vec = pl.BlockSpec((1, d), lambda i: (0, 0))
    return _call(body, name=name, grid=(n // tn,), in_specs=in_specs + [row, row, vec], out_specs=[row, row, vec],
                 out_shape=[jax.ShapeDtypeStruct((n, d), F32), jax.ShapeDtypeStruct((n, d), BF16),
                            jax.ShapeDtypeStruct((1, d), F32)],
                 args=args + [x, extra, g], semantics=("arbitrary",), exchange=exchange)


def _ple_final(x3, gpre, z, tgt, gpp, gf, *, name):
    def fn(x3_, gpre_, z_, tgt_, gpp_, gf_):
        gate = _sigmoid(gpre_)
        rz = lax.rsqrt(_rowmean(z_ * z_) + RMS_EPS)
        zh = z_ * rz
        e = zh * gpp_
        x4 = x3_ + gate * e
        r4 = lax.rsqrt(_rowmean(x4 * x4) + RMS_EPS)
        x4h = x4 * r4
        diff = x4h * gf_ - tgt_
        dout = diff * (1.0 / D_MODEL)
        dxh4 = dout * gf_
        dx4 = r4 * (dxh4 - x4h * _rowmean(dxh4 * x4h))
        dpre = dx4 * e * gate * (1.0 - gate)
        de = dx4 * gate
        dzh = de * gpp_
        dz = rz * (dzh - zh * _rowmean(dzh * zh))
        return dx4, dpre, dz, _colsum(diff * diff) * (0.5 / D_MODEL), _colsum(dout * x4h), _colsum(de * zh)

    n, d = x3.shape
    return _rowwise(fn, name=name, n=n, tn=256, ncol=1, rows=[(x3, d, 0), (gpre, d, 0), (z, d, 0), (tgt, d, 0)],
                    vecs=[(gpp, d, 0), (gf, d, 0)], outs=[(d, d, F32), (d, d, BF16), (d, d, BF16)],
                    accs=[(d, d), (d, d), (d, d)])


def _merge_fwd(proj, ya, yb, *, name):
    def fn(ga, gb, ya_, yb_):
        return (_sigmoid(ga) * ya_ + _sigmoid(gb) * yb_,)

    n = proj.shape[0]
    w = 512
    return _rowwise(fn, name=name, n=n, tn=512, ncol=D_MODEL // w,
                    rows=[(proj, w, COL_GA // w), (proj, w, COL_GB // w), (ya, w, 0), (yb, w, 0)], vecs=[],
                    outs=[(D_MODEL, w, BF16)])[0]


def _merge_bwd(dy, proj, ya, yb, *, name):
    def fn(dy_, ga, gb, ya_, yb_):
        sa, sb = _sigmoid(ga), _sigmoid(gb)
        return dy_ * sa, dy_ * sb, dy_ * ya_ * sa * (1.0 - sa), dy_ * yb_ * sb * (1.0 - sb)

    n = proj.shape[0]
    w = 512
    return _rowwise(fn, name=name, n=n, tn=512, ncol=D_MODEL // w,
                    rows=[(dy, w, 0), (proj, w, COL_GA // w), (proj, w, COL_GB // w), (ya, w, 0), (yb, w, 0)],
                    vecs=[], outs=[(D_MODEL, w, BF16)] * 4)


def _hgrn_post_fwd(o, proj, onorm, *, name):
    def fn(o_, og, gam):
        r = lax.rsqrt(_rowmean(o_ * o_) + RMS_EPS)
        return (o_ * r * gam * (og * _sigmoid(og)),)

    n = o.shape[0]
    w = HEAD_DIM
    return _rowwise(fn, name=name, n=n, tn=1024, ncol=HEADS, rows=[(o, w, 0), (proj, w, COL_OG // w)],
                    vecs=[(onorm, w, 0)], outs=[(D_MODEL, w, BF16)])[0]


def _hgrn_post_bwd(don, o, proj, onorm, *, name):
    def fn(don_, o_, og, gam):
        r = lax.rsqrt(_rowmean(o_ * o_) + RMS_EPS)
        oh = o_ * r
        sg = _sigmoid(og)
        dog = don_ * oh * gam * (sg * (1.0 + og * (1.0 - sg)))
        dn = don_ * (og * sg)
        doh = dn * gam
        do = r * (doh - oh * _rowmean(doh * oh))
        return dog, do, _colsum(dn * oh)

    n = o.shape[0]
    w = HEAD_DIM
    return _rowwise(fn, name=name, n=n, tn=1024, ncol=HEADS, rows=[(don, w, 0), (o, w, 0), (proj, w, COL_OG // w)],
                    vecs=[(onorm, w, 0)], outs=[(D_MODEL, w, BF16), (D_MODEL, w, F32)], accs=[(D_MODEL, w)])


def _split3(x):
    hi = x.astype(BF16)
    r1 = x - hi.astype(F32)
    mid = r1.astype(BF16)
    lo = (r1 - mid.astype(F32)).astype(BF16)
    return hi, mid, lo


def _tri_sum(tri, x):
    hi, mid, lo = _split3(x)
    return _dot(tri, hi) + _dot(tri, mid) + _dot(tri, lo)


def _lower_bound(lb_ref):
    return 1.0 / (1.0 + jnp.exp(lb_ref[1:2, :] - lb_ref[0:1, :]))


def _hgrn_specs(n, t, reverse):
    nt = n // t
    width = HGRN_HEADS_PER_STEP * HEAD_DIM

    def tok(i):
        return nt - 1 - i if reverse else i

    def sec(col):
        c0 = col // width
        return pl.BlockSpec((t, width), lambda h, i: (tok(i), c0 + h))

    head_tile = pl.BlockSpec((t, width), lambda h, i: (tok(i), h))
    state = pl.BlockSpec((HGRN_HEADS_PER_STEP, t // CHUNK, HEAD_DIM, HEAD_DIM), lambda h, i: (h, tok(i), 0, 0))
    lb = pl.BlockSpec((2, width), lambda h, i: (0, h))
    return sec, head_tile, state, lb


def _hgrn_fwd(proj, hgrn_lb, *, name):
    n = proj.shape[0]
    t = _tile(n, 512, CHUNK)
    nc = t // CHUNK
    hps = HGRN_HEADS_PER_STEP
    width = hps * HEAD_DIM
    lanes = [slice(h * HEAD_DIM, (h + 1) * HEAD_DIM) for h in range(hps)]
    sec, head_tile, state, lbspec = _hgrn_specs(n, t, False)

    def body(q_ref, f_ref, i_ref, lb_ref, o_ref, st_ref, s_acc, g_s, a_s):
        @pl.when(pl.program_id(1) == 0)
        def _():
            s_acc[...] = jnp.zeros_like(s_acc)

        lb = _lower_bound(lb_ref)
        row = lax.broadcasted_iota(jnp.int32, (CHUNK, CHUNK), 0)
        col = lax.broadcasted_iota(jnp.int32, (CHUNK, CHUNK), 1)
        tril = row >= col
        trilb = jnp.where(tril, 1.0, 0.0).astype(BF16)
        rowk = lax.broadcasted_iota(jnp.int32, (CHUNK, width), 0)

        def chunk(c, carry):
            rows = pl.ds(pl.multiple_of(c * CHUNK, CHUNK), CHUNK)
            qr, fr, v = [r[rows, :].astype(F32) for r in (q_ref, f_ref, i_ref)]
            q = qr * _sigmoid(qr)
            f = lb + (1.0 - lb) * _sigmoid(fr)
            k = 1.0 - f
            g = _tri_sum(trilb, jnp.log(f))
            g_s[...] = g
            st0 = [s_acc[h] for h in range(hps)]
            for h in range(hps):
                st_ref[h, c] = st0[h]
            vb = v.astype(BF16)
            for blk in range(CHUNK // SUB):
                lo, hi = blk * SUB, (blk + 1) * SUB
                gref = g_s[lo - 1:lo, :] if blk else jnp.zeros((1, width), F32)
                qi = (q[lo:hi] * jnp.exp(g[lo:hi] - gref)).astype(BF16)
                ki = (k * jnp.exp(jnp.where(rowk < hi, gref - g, NEG_BIG))).astype(BF16)
                for h, ln in enumerate(lanes):
                    a_s[h, lo:hi, :] = _dot_nt(qi[:, ln], ki[:, ln])
            qeb = (q * jnp.exp(g)).astype(BF16)
            o_ref[rows, :] = jnp.concatenate(
                [_dot(jnp.where(tril, a_s[h], 0.0).astype(BF16), vb[:, ln]) + _dot_nt(qeb[:, ln], st0[h].astype(BF16))
                 for h, ln in enumerate(lanes)], axis=1)
            glast = g_s[CHUNK - 1:CHUNK, :]
            kdb = (k * jnp.exp(glast - g)).astype(BF16)
            dec = jnp.exp(glast)
            for h, ln in enumerate(lanes):
                s_acc[h] = st0[h] * dec[:, ln] + _dot_tn(vb[:, ln], kdb[:, ln])
            return carry

        lax.fori_loop(0, nc, chunk, 0)

    return pl.pallas_call(
        body, name=name, grid=(HEADS // hps, n // t),
        in_specs=[sec(COL_Q), sec(COL_F), sec(COL_I), lbspec], out_specs=[head_tile, state],
        out_shape=[jax.ShapeDtypeStruct((n, D_MODEL), F32),
                   jax.ShapeDtypeStruct((HEADS, n // CHUNK, HEAD_DIM, HEAD_DIM), F32)],
        scratch_shapes=[pltpu.VMEM((hps, HEAD_DIM, HEAD_DIM), F32), pltpu.VMEM((CHUNK, width), F32),
                        pltpu.VMEM((hps, CHUNK, CHUNK), F32)],
        compiler_params=_params("parallel", "arbitrary"),
    )(proj, proj, proj, hgrn_lb)


def _hgrn_bwd(proj, hgrn_lb, do, states, *, name):
    n = proj.shape[0]
    t = _tile(n, 512, CHUNK)
    nc = t // CHUNK
    hps = HGRN_HEADS_PER_STEP
    width = hps * HEAD_DIM
    lanes = [slice(h * HEAD_DIM, (h + 1) * HEAD_DIM) for h in range(hps)]
    sec, head_tile, state, lbspec = _hgrn_specs(n, t, True)

    def body(q_ref, f_ref, i_ref, lb_ref, do_ref, st_ref, dqfi_ref, dlb_ref, d_acc, g_s, a_s, dq_s,
             dg_s):
        first = pl.program_id(1) == 0

        @pl.when(first)
        def _():
            d_acc[...] = jnp.zeros_like(d_acc)

        lb = _lower_bound(lb_ref)
        row = lax.broadcasted_iota(jnp.int32, (CHUNK, CHUNK), 0)
        col = lax.broadcasted_iota(jnp.int32, (CHUNK, CHUNK), 1)
        tril = row >= col
        trilb = jnp.where(tril, 1.0, 0.0).astype(BF16)
        triub = jnp.where(row <= col, 1.0, 0.0).astype(BF16)
        rowk = lax.broadcasted_iota(jnp.int32, (CHUNK, width), 0)

        def per_head(fn):
            return jnp.concatenate([fn(h, ln) for h, ln in enumerate(lanes)], axis=1)

        def chunk(j, dlb):
            c = nc - 1 - j
            rows = pl.ds(pl.multiple_of(c * CHUNK, CHUNK), CHUNK)
            qr, fr, v, dout = [r[rows, :].astype(F32) for r in (q_ref, f_ref, i_ref, do_ref)]
            sq = _sigmoid(qr)
            q = qr * sq
            sf = _sigmoid(fr)
            f = lb + (1.0 - lb) * sf
            k = 1.0 - f
            g = _tri_sum(trilb, jnp.log(f))
            g_s[...] = g
            st0 = [st_ref[h, c] for h in range(hps)]
            dt = [d_acc[h] for h in range(hps)]
            vb, dob = v.astype(BF16), dout.astype(BF16)
            dtb = [x.astype(BF16) for x in dt]
            st0b = [x.astype(BF16) for x in st0]
            glast = g_s[CHUNK - 1:CHUNK, :]
            eg = jnp.exp(g)
            kdec = jnp.exp(glast - g)
            qeb, kdb = (q * eg).astype(BF16), (k * kdec).astype(BF16)
            aps = [jnp.where(row > col, _dot_nt(dob[:, ln], vb[:, ln]), 0.0) for ln in lanes]
            dov = dout * v
            adiag = per_head(lambda h, ln: jnp.broadcast_to(
                jnp.sum(dov[:, ln], axis=-1, keepdims=True), (CHUNK, HEAD_DIM)))
            dq_inter = per_head(lambda h, ln: _dot(dob[:, ln], st0b[h]))
            dk_inter = per_head(lambda h, ln: _dot(vb[:, ln], dtb[h]))
            dk_st = kdec * dk_inter
            dg = qeb.astype(F32) * dq_inter
            dg_minus = kdb.astype(F32) * dk_inter
            dg = dg - dg_minus
            for blk in range(CHUNK // SUB):
                lo, hi = blk * SUB, (blk + 1) * SUB
                gref = g_s[lo - 1:lo, :] if blk else jnp.zeros((1, width), F32)
                qscale = jnp.exp(g[lo:hi] - gref)
                kscale = jnp.exp(jnp.where(rowk < hi, gref - g, NEG_BIG))
                qi = (q[lo:hi] * qscale).astype(BF16)
                ki = (k * kscale).astype(BF16)
                for h, ln in enumerate(lanes):
                    a_s[h, lo:hi, :] = _dot_nt(qi[:, ln], ki[:, ln])
                apb = [x[lo:hi].astype(BF16) for x in aps]
                from_k = per_head(lambda h, ln: _dot(apb[h], ki[:, ln]))
                from_q = per_head(lambda h, ln: _dot_tn(apb[h], qi[:, ln]))
                dq_s[lo:hi, :] = qscale * from_k
                dg_s[lo:hi, :] = qi.astype(F32) * from_k
                dk_st = dk_st + kscale * from_q
                dg = dg - ki.astype(F32) * from_q
            dg = dg + dg_s[...]
            dv = per_head(lambda h, ln: _dot_tn(jnp.where(tril, a_s[h], 0.0).astype(BF16), dob[:, ln])
                          + _dot_nt(kdb[:, ln], dtb[h]))
            dq_st = dq_s[...] + eg * dq_inter
            dq = dq_st + adiag * k
            dk = dk_st + adiag * q
            dec = jnp.exp(glast)
            dt_dec = [dt[h] * dec[:, ln] for h, ln in enumerate(lanes)]
            for h, ln in enumerate(lanes):
                d_acc[h] = dt_dec[h] + _dot_tn(dob[:, ln], qeb[:, ln])
            later = per_head(lambda h, ln: _colsum(dt_dec[h] * st0[h])) + _colsum(dg_minus)
            dlf = later + _tri_sum(triub, dg)
            df = dlf / f - dk
            dqfi_ref[rows, 0:width] = (dq * (sq * (1.0 + qr * (1.0 - sq)))).astype(dqfi_ref.dtype)
            dqfi_ref[rows, width:2 * width] = (df * (1.0 - lb) * sf * (1.0 - sf)).astype(dqfi_ref.dtype)
            dqfi_ref[rows, 2 * width:3 * width] = dv.astype(dqfi_ref.dtype)
            return dlb + _colsum(df * (1.0 - sf))

        dlb = lax.fori_loop(0, nc, chunk, jnp.zeros((1, width), F32))
        _accumulate(dlb_ref, dlb, first)

    assert hps == HEADS
    nt = n // t
    return pl.pallas_call(
        body, name=name, grid=(1, nt),
        in_specs=[sec(COL_Q), sec(COL_F), sec(COL_I), lbspec, head_tile, state],
        out_specs=[pl.BlockSpec((t, 3 * width), lambda h, i: (nt - 1 - i, 0)),
                   pl.BlockSpec((1, width), lambda h, i: (0, h))],
        out_shape=[jax.ShapeDtypeStruct((n, 3 * D_MODEL), BF16), jax.ShapeDtypeStruct((1, D_MODEL), F32)],
        scratch_shapes=[pltpu.VMEM((hps, HEAD_DIM, HEAD_DIM), F32), pltpu.VMEM((CHUNK, width), F32),
                        pltpu.VMEM((hps, CHUNK, CHUNK), F32), pltpu.VMEM((CHUNK, width), F32),
                        pltpu.VMEM((CHUNK, width), F32)],
        compiler_params=_params("parallel", "arbitrary"),
    )(proj, proj, proj, hgrn_lb, do, states)


def _pool_fwd(proj, pool_w, pool_scale, *, name):
    n = proj.shape[0]
    t = _tile(n, 512, POOL_HALO)
    per = t // POOL_HALO
    c0 = COL_POOL // POOL_WIDTH

    def body(u_ref, halo_ref, pw_ref, ps_ref, pooled_ref, mixed_ref, ext):
        i = pl.program_id(0)
        u = u_ref[...].astype(F32)
        ext[POOL_HALO:POOL_HALO + t, :] = u
        ext[0:POOL_HALO, :] = jnp.where(i > 0, halo_ref[...].astype(F32), 0.0)
        pos = i * t + lax.broadcasted_iota(jnp.int32, (t, POOL_CH), 0) + 1
        for grp, win in enumerate(POOL_WINDOWS):
            cols = slice(grp * POOL_CH, (grp + 1) * POOL_CH)
            acc = u[:, cols]
            for j in range(1, win):
                acc = acc + ext[POOL_HALO - j:POOL_HALO - j + t, cols]
            pooled = (acc / jnp.minimum(pos, win).astype(F32) - u[:, cols]).astype(BF16)
            pooled_ref[:, cols] = pooled
            mixed_ref[:, cols] = (_dot(pooled, pw_ref[grp].astype(BF16)) * ps_ref[:, cols]).astype(BF16)

    tile = pl.BlockSpec((t, POOL_WIDTH), lambda i: (i, 0))
    return pl.pallas_call(
        body, name=name, grid=(n // t,),
        in_specs=[pl.BlockSpec((t, POOL_WIDTH), lambda i: (i, c0)),
                  pl.BlockSpec((POOL_HALO, POOL_WIDTH), lambda i: (jnp.maximum(i * per - 1, 0), c0)),
                  pl.BlockSpec((len(POOL_WINDOWS), POOL_CH, POOL_CH), lambda i: (0, 0, 0)),
                  pl.BlockSpec((1, POOL_WIDTH), lambda i: (0, 0))],
        out_specs=[tile, tile],
        out_shape=[jax.ShapeDtypeStruct((n, POOL_WIDTH), BF16), jax.ShapeDtypeStruct((n, POOL_WIDTH), BF16)],
        scratch_shapes=[pltpu.VMEM((t + POOL_HALO, POOL_WIDTH), F32)],
        compiler_params=_params("parallel"),
    )(proj, proj, pool_w, pool_scale)


def _pool_bwd(dmixed, pooled, pool_w, pool_scale, *, name):
    n = dmixed.shape[0]
    t = _tile(n, 512, POOL_HALO)
    per = t // POOL_HALO
    nb = n // t

    def body(dm_ref, dmh_ref, p_ref, pw_ref, ps_ref, du_ref, dpw_ref, dps_ref, ext):
        i = pl.program_id(0)

        @pl.when(i == 0)
        def _():
            dpw_ref[...] = jnp.zeros_like(dpw_ref)
            dps_ref[...] = jnp.zeros_like(dps_ref)

        dm, dmh = dm_ref[...], dmh_ref[...]
        pos = i * t + lax.broadcasted_iota(jnp.int32, (t, POOL_CH), 0) + 1
        for grp, win in enumerate(POOL_WINDOWS):
            cols = slice(grp * POOL_CH, (grp + 1) * POOL_CH)
            pwb = pw_ref[grp].astype(BF16)
            pb = p_ref[:, cols]
            scale = ps_ref[:, cols]
            dps_ref[:, cols] += _colsum(dm[:, cols] * _dot(pb, pwb))
            dpm = (dm[:, cols] * scale).astype(BF16)
            dpw_ref[grp] += _dot_tn(pb, dpm)
            dpool = _dot_nt(dpm, pwb)
            dpool_next = _dot_nt((dmh[:, cols] * scale).astype(BF16), pwb)
            ext[0:t, cols] = dpool / jnp.minimum(pos, win).astype(F32)
            ext[t:t + POOL_HALO, cols] = jnp.where(i < nb - 1, dpool_next * (1.0 / win), 0.0)
            acc = -dpool
            for j in range(win):
                acc = acc + ext[j:j + t, cols]
            du_ref[:, cols] = acc.astype(du_ref.dtype)

    tile = pl.BlockSpec((t, POOL_WIDTH), lambda i: (i, 0))
    return pl.pallas_call(
        body, name=name, grid=(nb,),
        in_specs=[tile, pl.BlockSpec((POOL_HALO, POOL_WIDTH), lambda i: (jnp.minimum((i + 1) * per, nb * per - 1), 0)),
                  tile, pl.BlockSpec((len(POOL_WINDOWS), POOL_CH, POOL_CH), lambda i: (0, 0, 0)),
                  pl.BlockSpec((1, POOL_WIDTH), lambda i: (0, 0))],
        out_specs=[tile, pl.BlockSpec((len(POOL_WINDOWS), POOL_CH, POOL_CH), lambda i: (0, 0, 0)),
                   pl.BlockSpec((1, POOL_WIDTH), lambda i: (0, 0))],
        out_shape=[jax.ShapeDtypeStruct((n, POOL_WIDTH), BF16),
                   jax.ShapeDtypeStruct((len(POOL_WINDOWS), POOL_CH, POOL_CH), F32),
                   jax.ShapeDtypeStruct((1, POOL_WIDTH), F32)],
        scratch_shapes=[pltpu.VMEM((t + POOL_HALO, POOL_WIDTH), F32)],
        compiler_params=_params("arbitrary"),
    )(dmixed, dmixed, pooled, pool_w, pool_scale)


def _adamw(w, g, m, v):
    m2 = ADAM_B1 * m + (1.0 - ADAM_B1) * g
    v2 = ADAM_B2 * v + (1.0 - ADAM_B2) * (g * g)
    m_hat = m2 * (1.0 / (1.0 - ADAM_B1 ** ADAM_STEP))
    v_hat = v2 * (1.0 / (1.0 - ADAM_B2 ** ADAM_STEP))
    delta = -ADAM_LR * (m_hat / (jnp.sqrt(v_hat) + ADAM_EPS) + ADAM_WD * w)
    return delta, m2, v2


def _adam_big(recv, w, m, v, *, name):
    r, c = w.shape
    tr = _tile(r, 256, 16)

    def body(recv_ref, w_ref, m_ref, v_ref, g_ref, d_ref, m2_ref, v2_ref):
        g = recv_ref[0].astype(F32)
        for i in range(1, N_DEV):
            g = g + recv_ref[i].astype(F32)
        delta, m2, v2 = _adamw(w_ref[...], g, m_ref[...], v_ref[...])
        g_ref[...] = g
        d_ref[...] = delta
        m2_ref[...] = m2
        v2_ref[...] = v2

    tile = pl.BlockSpec((tr, c), lambda i: (i, 0))
    out = jax.ShapeDtypeStruct((r, c), F32)
    return pl.pallas_call(
        body, name=name, grid=(r // tr,),
        in_specs=[pl.BlockSpec((N_DEV, tr, c), lambda i: (0, i, 0)), tile, tile, tile],
        out_specs=[tile] * 4, out_shape=[out] * 4, compiler_params=_params("parallel"),
    )(recv, w, m, v)


def _adam_small(parts, w, m, v, *, name):
    lb0, lbn = SMALL_ROWS["hgrn_lb"]
    half = lbn // 2

    def body(parts_ref, w_ref, m_ref, v_ref, g_ref, d_ref, m2_ref, v2_ref):
        g = parts_ref[0]
        for i in range(1, N_DEV):
            g = g + parts_ref[i]
        w_ = w_ref[...]
        s0 = 1.0 / (1.0 + jnp.exp(w_[lb0 + half:lb0 + lbn] - w_[lb0:lb0 + half]))
        ga = g[lb0:lb0 + half] * s0 * (1.0 - s0)
        g = jnp.concatenate([g[:lb0], ga, -ga, g[lb0 + lbn:]], axis=0)
        delta, m2, v2 = _adamw(w_, g, m_ref[...], v_ref[...])
        g_ref[...] = g
        d_ref[...] = delta
        m2_ref[...] = m2
        v2_ref[...] = v2

    out = jax.ShapeDtypeStruct(w.shape, F32)
    return pl.pallas_call(body, name=name, out_shape=[out] * 4, compiler_params=_params())(parts, w, m, v)


def _pack_small(vals):
    pieces, at = [], 0
    for name, (row0, nrows) in SMALL_ROWS.items():
        if row0 > at:
            pieces.append(jnp.zeros((row0 - at, 128), F32))
        pieces.append(vals[name].astype(F32).reshape(nrows, 128))
        at = row0 + nrows
    if at < SMALL_TOTAL_ROWS:
        pieces.append(jnp.zeros((SMALL_TOTAL_ROWS - at, 128), F32))
    return jnp.concatenate(pieces, axis=0)


def _unpack_small(packed, shapes):
    return {name: packed[row0:row0 + nrows].reshape(shapes[name]) for name, (row0, nrows) in SMALL_ROWS.items()}


SPLIT_AXIS = dict(BIG_WEIGHTS)


def _gather_of(names, weights):
    return _Exchange([weights[k][0].astype(BF16) for k in names], gather=True)


def _scatter_of(names, dfull):
    return _Exchange([_to_slots(dfull[k], SPLIT_AXIS[k]) for k in names], gather=False)


def _to_slots(dw, axis):
    k, m = dw.shape
    if axis == 0:
        return dw.reshape(N_DEV, k // N_DEV, m)
    return dw.reshape(k, N_DEV, m // N_DEV).transpose(1, 0, 2)


def _from_slots(gathered, axis):
    _, r, c = gathered.shape
    if axis == 0:
        return gathered.reshape(N_DEV * r, c)
    return gathered.transpose(1, 0, 2).reshape(r, N_DEV * c)


def kernel(x, p, ffn1_norm, ffn1_w1, ffn1_w3, ffn1_w2, mix_norm, w_in, hgrn_lb, hgrn_onorm, w_branch_a, pool_w, pool_scale, w_branch_b, w_out, ffn2_norm, ffn2_w1, ffn2_w3, ffn2_w2, ple_norm, ple_w_gate, ple_w_proj, ple_post_norm, final_norm, loss_target, m_ffn1_norm, m_ffn1_w1, m_ffn1_w3, m_ffn1_w2, m_mix_norm, m_w_in, m_hgrn_lb, m_hgrn_onorm, m_w_branch_a, m_pool_w, m_pool_scale, m_w_branch_b, m_w_out, m_ffn2_norm, m_ffn2_w1, m_ffn2_w3, m_ffn2_w2, m_ple_norm, m_ple_w_gate, m_ple_w_proj, m_ple_post_norm, m_final_norm, v_ffn1_norm, v_ffn1_w1, v_ffn1_w3, v_ffn1_w2, v_mix_norm, v_w_in, v_hgrn_lb, v_hgrn_onorm, v_w_branch_a, v_pool_w, v_pool_scale, v_w_branch_b, v_w_out, v_ffn2_norm, v_ffn2_w1, v_ffn2_w3, v_ffn2_w2, v_ple_norm, v_ple_w_gate, v_ple_w_proj, v_ple_post_norm, v_final_norm):
    weights = dict(ffn1_norm=ffn1_norm, ffn1_w1=ffn1_w1, ffn1_w3=ffn1_w3, ffn1_w2=ffn1_w2, mix_norm=mix_norm, w_in=w_in, hgrn_lb=hgrn_lb, hgrn_onorm=hgrn_onorm, w_branch_a=w_branch_a, pool_w=pool_w, pool_scale=pool_scale, w_branch_b=w_branch_b, w_out=w_out, ffn2_norm=ffn2_norm, ffn2_w1=ffn2_w1, ffn2_w3=ffn2_w3, ffn2_w2=ffn2_w2, ple_norm=ple_norm, ple_w_gate=ple_w_gate, ple_w_proj=ple_w_proj, ple_post_norm=ple_post_norm, final_norm=final_norm)
    mom1 = dict(ffn1_norm=m_ffn1_norm, ffn1_w1=m_ffn1_w1, ffn1_w3=m_ffn1_w3, ffn1_w2=m_ffn1_w2, mix_norm=m_mix_norm, w_in=m_w_in, hgrn_lb=m_hgrn_lb, hgrn_onorm=m_hgrn_onorm, w_branch_a=m_w_branch_a, pool_w=m_pool_w, pool_scale=m_pool_scale, w_branch_b=m_w_branch_b, w_out=m_w_out, ffn2_norm=m_ffn2_norm, ffn2_w1=m_ffn2_w1, ffn2_w3=m_ffn2_w3, ffn2_w2=m_ffn2_w2, ple_norm=m_ple_norm, ple_w_gate=m_ple_w_gate, ple_w_proj=m_ple_w_proj, ple_post_norm=m_ple_post_norm, final_norm=m_final_norm)
    mom2 = dict(ffn1_norm=v_ffn1_norm, ffn1_w1=v_ffn1_w1, ffn1_w3=v_ffn1_w3, ffn1_w2=v_ffn1_w2, mix_norm=v_mix_norm, w_in=v_w_in, hgrn_lb=v_hgrn_lb, hgrn_onorm=v_hgrn_onorm, w_branch_a=v_w_branch_a, pool_w=v_pool_w, pool_scale=v_pool_scale, w_branch_b=v_w_branch_b, w_out=v_w_out, ffn2_norm=v_ffn2_norm, ffn2_w1=v_ffn2_w1, ffn2_w3=v_ffn2_w3, ffn2_w2=v_ffn2_w2, ple_norm=v_ple_norm, ple_w_gate=v_ple_w_gate, ple_w_proj=v_ple_w_proj, ple_post_norm=v_ple_post_norm, final_norm=v_final_norm)

    xs = x[0]
    ps = p[0, 0].astype(BF16)
    tgt = loss_target[0]
    n = xs.shape[0]

    g_f1, g_mix, g_on, g_f2 = ffn1_norm, mix_norm, hgrn_onorm, ffn2_norm
    g_ple, g_post, g_fin = ple_norm, ple_post_norm, final_norm.reshape(1, D_MODEL)
    lb2 = hgrn_lb
    pw, pscale = pool_w[0], pool_scale

    full = {}

    def keep(names, gathered):
        for k, g in zip(names, gathered):
            full[k] = _from_slots(g, SPLIT_AXIS[k])

    first_names = ("ffn1_w1", "ffn1_w3")
    keep(first_names, _exchange_now([weights[k][0].astype(BF16) for k in first_names], name="gather_first", gather=True))
    h1 = _rms_fwd(xs, g_f1, name="ffn1_rms")
    names = ("ffn1_w2", "w_in")
    ex = _gather_of(names, weights)
    a1, b1, s1 = _ffn_up(h1, full["ffn1_w1"], full["ffn1_w3"], name="ffn1_up", exchange=ex)
    keep(names, ex.received)
    names = ("w_branch_a", "w_branch_b", "w_out")
    ex = _gather_of(names, weights)
    x1 = _mm_nn(s1, full["ffn1_w2"], name="ffn1_down", tn=1024, tm=512, res=xs, scale=0.5, exchange=ex)
    keep(names, ex.received)
    h2 = _rms_fwd(x1, g_mix, name="mix_rms")
    names = ("ffn2_w1", "ffn2_w3", "ffn2_w2", "ple_w_gate", "ple_w_proj")
    ex = _gather_of(names, weights)
    proj = _mm_nn(h2, full["w_in"], name="w_in_proj", tn=2048, tm=512, out_dtype=BF16, exchange=ex)
    keep(names, ex.received)
    o, states = _hgrn_fwd(proj, lb2, name="hgrn_fwd")
    on = _hgrn_post_fwd(o, proj, g_on, name="hgrn_post_fwd")
    ya = _mm_nn(on, full["w_branch_a"], name="branch_a", tn=1024, tm=512)
    pooled, mixed = _pool_fwd(proj, pw, pscale, name="pool_fwd")
    yb = _mm_nn(mixed, full["w_branch_b"], name="branch_b", tn=1024, tm=512)
    y = _merge_fwd(proj, ya, yb, name="merge_fwd")
    x2 = _mm_nn(y, full["w_out"], name="w_out_proj", tn=1024, tm=512, res=x1)
    h3 = _rms_fwd(x2, g_f2, name="ffn2_rms")
    a2, b2, s2 = _ffn_up(h3, full["ffn2_w1"], full["ffn2_w3"], name="ffn2_up")
    x3 = _mm_nn(s2, full["ffn2_w2"], name="ffn2_down", tn=1024, tm=512, res=x2, scale=0.5)
    h4 = _rms_fwd(x3, g_ple, name="ple_rms")
    gpre = _mm_nn(h4, full["ple_w_gate"], name="ple_gate", tn=1024, tm=512)
    z = _mm_nn(ps, full["ple_w_proj"], name="ple_proj", tn=1024, tm=512)
    dx4, dpre, dz, loss_part, d_fin, d_post = _ple_final(x3, gpre, z, tgt, g_post, g_fin, name="ple_final")

    dfull, received = {}, {}

    def sent(names, exchange):
        received.update(zip(names, exchange.received))

    dfull["ple_w_proj"] = _mm_tn(ps, dz, name="d_ple_w_proj", tn=512, tm=1024)
    dfull["ple_w_gate"] = _mm_tn(h4, dpre, name="d_ple_w_gate", tn=512, tm=1024)
    dx3, dx3s, d_ple = _mm_nt_rms_bwd([(dpre, full["ple_w_gate"])], x3, dx4, g_ple, name="ple_rms_bwd", tn=512,
                                      half_scale=0.5)

    names = ("ple_w_proj", "ple_w_gate")
    ex = _scatter_of(names, dfull)
    da2, db2 = _ffn_bwd_mid(dx3s, full["ffn2_w2"], a2, b2, name="ffn2_bwd_mid", exchange=ex)
    sent(names, ex)
    dfull["ffn2_w2"] = _mm_tn(s2, dx3s, name="ffn2_dw2", tn=512, tm=512)
    dfull["ffn2_w1"] = _mm_tn(h3, da2, name="ffn2_dw1", tn=512, tm=1408)
    dfull["ffn2_w3"] = _mm_tn(h3, db2, name="ffn2_dw3", tn=512, tm=1408)
    names = ("ffn2_w2", "ffn2_w1")
    ex = _scatter_of(names, dfull)
    dx2, dx2b, d_f2 = _mm_nt_rms_bwd([(da2, full["ffn2_w1"]), (db2, full["ffn2_w3"])], x2, dx3, g_f2,
                                     name="ffn2_rms_bwd", tn=512, half_scale=1.0, exchange=ex)
    sent(names, ex)

    dfull["w_out"] = _mm_tn(y, dx2b, name="d_w_out", tn=512, tm=1024)
    dy = _mm_nt([(dx2b, full["w_out"])], name="d_y", tn=1024, tk=512)
    dya, dyb, dga, dgb = _merge_bwd(dy, proj, ya, yb, name="merge_bwd")

    dfull["w_branch_b"] = _mm_tn(mixed, dyb, name="d_w_branch_b", tn=512, tm=1024)
    dmixed = _mm_nt([(dyb, full["w_branch_b"])], name="d_mixed", tn=1024, tk=512)
    du, d_pw, d_ps = _pool_bwd(dmixed, pooled, pw, pscale, name="pool_bwd")

    dfull["w_branch_a"] = _mm_tn(on, dya, name="d_w_branch_a", tn=512, tm=1024)
    don = _mm_nt([(dya, full["w_branch_a"])], name="d_on", tn=1024, tk=512)
    dog, do, d_on = _hgrn_post_bwd(don, o, proj, g_on, name="hgrn_post_bwd")
    dqfi, d_lb = _hgrn_bwd(proj, lb2, do, states, name="hgrn_bwd")
    dproj = [dqfi, dog, du, dga, dgb]
    names = ("ffn2_w3", "w_out", "w_branch_b", "w_branch_a")
    ex = _scatter_of(names, dfull)
    dx1, dx1s, d_mix = _mm_nt_rms_bwd([(dproj, full["w_in"])], x1, dx2, g_mix, name="mix_rms_bwd", tn=512,
                                      half_scale=0.5, exchange=ex)
    sent(names, ex)
    dfull["w_in"] = jnp.concatenate(
        [_mm_tn(h2, part, name=f"d_w_in_{j}", tn=512, tm=1536) for j, part in enumerate(dproj)], axis=1)

    names = ("w_in",)
    ex = _scatter_of(names, dfull)
    da1, db1 = _ffn_bwd_mid(dx1s, full["ffn1_w2"], a1, b1, name="ffn1_bwd_mid", exchange=ex)
    sent(names, ex)
    dfull["ffn1_w2"] = _mm_tn(s1, dx1s, name="ffn1_dw2", tn=512, tm=512)
    names = ("ffn1_w2",)
    ex = _scatter_of(names, dfull)
    dfull["ffn1_w1"] = _mm_tn(h1, da1, name="ffn1_dw1", tn=512, tm=1408, exchange=ex)
    sent(names, ex)
    names = ("ffn1_w1",)
    ex = _scatter_of(names, dfull)
    dfull["ffn1_w3"] = _mm_tn(h1, db1, name="ffn1_dw3", tn=512, tm=1408, exchange=ex)
    sent(names, ex)
    names = ("ffn1_w3",)
    ex = _scatter_of(names, dfull)
    grad_x, _, d_f1 = _mm_nt_rms_bwd([(da1, full["ffn1_w1"]), (db1, full["ffn1_w3"])], xs, dx1, g_f1,
                                     name="ffn1_rms_bwd", tn=512, half_scale=1.0, exchange=ex)
    sent(names, ex)

    small_part = _pack_small(dict(
        ffn1_norm=d_f1, mix_norm=d_mix, hgrn_onorm=d_on, ffn2_norm=d_f2, ple_norm=d_ple, ple_post_norm=d_post,
        final_norm=d_fin, hgrn_lb=jnp.concatenate([d_lb, jnp.zeros_like(d_lb)], axis=0), pool_scale=d_ps, pool_w=d_pw))
    small_all = _exchange_now([small_part], name="gather_small_grads", gather=True)[0]

    grads, deltas, new_m, new_v = {}, {}, {}, {}
    for name, _ in BIG_WEIGHTS:
        shape, recv = weights[name].shape, received[name]
        res = _adam_big(recv, weights[name][0], mom1[name][0], mom2[name][0], name=f"adam_{name}")
        grads[name], deltas[name], new_m[name], new_v[name] = [r.reshape(shape) for r in res]
    shapes = {name: weights[name].shape for name in SMALL_ROWS}
    res = _adam_small(small_all, _pack_small(weights), _pack_small(mom1), _pack_small(mom2), name="adam_small")
    for store, packed in zip((grads, deltas, new_m, new_v), res):
        store.update(_unpack_small(packed, shapes))

    loss = lax.psum(jnp.sum(loss_part), ("x", "y", "c"))
    return (loss, grad_x.reshape(x.shape), *[grads[k] for k in WEIGHT_ORDER], *[deltas[k] for k in WEIGHT_ORDER],
            *[new_m[k] for k in WEIGHT_ORDER], *[new_v[k] for k in WEIGHT_ORDER])
```

```python
import jax
import jax.numpy as jnp
from jax import lax
from jax.experimental import pallas as pl
from jax.experimental.pallas import tpu as pltpu

F32 = jnp.float32
BF16 = jnp.bfloat16

N_DEV = 8
D_MODEL = 1024
HEADS = 8
HEAD_DIM = 128
POOL_WINDOWS = (2, 4, 8, 16)
POOL_CH = 128
POOL_WIDTH = 512
POOL_HALO = 16
RMS_EPS = 1e-6
CHUNK = 64
SUB = 32
HGRN_HEADS_PER_STEP = 8
NEG_BIG = -1e30

ADAM_LR = 0.001
ADAM_B1 = 0.9
ADAM_B2 = 0.999
ADAM_EPS = 1e-08
ADAM_WD = 0.01
ADAM_STEP = 10

V7X_VMEM_BYTES = 64 * 1024 * 1024
VMEM_LIMIT = (V7X_VMEM_BYTES * 3) // 4
ROW_TILE_CAP = 8192

COL_Q, COL_F, COL_I, COL_OG, COL_POOL, COL_GA, COL_GB = 0, 1024, 2048, 3072, 4096, 4608, 5632

BIG_WEIGHTS = (
    ("ffn1_w1", 1), ("ffn1_w3", 1), ("ffn1_w2", 0), ("w_in", 1), ("w_branch_a", 0), ("w_branch_b", 1),
    ("w_out", 0), ("ffn2_w1", 1), ("ffn2_w3", 1), ("ffn2_w2", 0), ("ple_w_gate", 0), ("ple_w_proj", 1),
)
SMALL_ROWS = {
    "ffn1_norm": (0, 8), "mix_norm": (8, 8), "hgrn_onorm": (16, 8), "ffn2_norm": (24, 8), "ple_norm": (32, 8),
    "ple_post_norm": (40, 8), "final_norm": (48, 8), "hgrn_lb": (56, 16), "pool_scale": (72, 4), "pool_w": (80, 512),
}
SMALL_TOTAL_ROWS = 592
WEIGHT_ORDER = (
    "ffn1_norm", "ffn1_w1", "ffn1_w3", "ffn1_w2", "mix_norm", "w_in", "hgrn_lb", "hgrn_onorm", "w_branch_a", "pool_w",
    "pool_scale", "w_branch_b", "w_out", "ffn2_norm", "ffn2_w1", "ffn2_w3", "ffn2_w2", "ple_norm", "ple_w_gate",
    "ple_w_proj", "ple_post_norm", "final_norm",
)


def _params(*sem):
    return pltpu.CompilerParams(dimension_semantics=sem if sem else None, vmem_limit_bytes=VMEM_LIMIT)


COL_CHUNK = 256


def _rows(tn, width):
    return pl.BlockSpec((tn, width), lambda i: (i, 0))


def _resident(shape):
    return pl.BlockSpec(shape, lambda i: (0,) * len(shape), pipeline_mode=pl.Buffered(1))


def _dot(a, b):
    return jnp.dot(a, b, preferred_element_type=F32)


def _dot_nt(a, b):
    return lax.dot_general(a, b, (((1,), (1,)), ((), ())), preferred_element_type=F32)


def _dot_tn(a, b):
    return lax.dot_general(a, b, (((0,), (0,)), ((), ())), preferred_element_type=F32)


def _sigmoid(x):
    return 0.5 * jnp.tanh(0.5 * x) + 0.5


def _tile(n, want, mult):
    if mult != 128:
        want = min(want, ROW_TILE_CAP)
    if n <= want:
        return n
    t = (want // mult) * mult
    while t > mult and n % t:
        t -= mult
    assert n % t == 0, (n, want, mult)
    return t


class _Exchange:
    COPIES = N_DEV - 1

    def __init__(self, arrs, gather):
        self.arrs, self.gather, self.n = list(arrs), gather, len(arrs)
        self.out_shape = [jax.ShapeDtypeStruct((N_DEV,) + (a.shape if gather else a.shape[1:]), a.dtype) for a in arrs]
        self.scratch = [pltpu.SemaphoreType.DMA((self.n * self.COPIES,)),
                        pltpu.SemaphoreType.DMA((self.n * self.COPIES,)), pltpu.SemaphoreType.DMA((self.n,))]
        self.received = None

    @staticmethod
    def _place():
        x, y, c = lax.axis_index("x"), lax.axis_index("y"), lax.axis_index("c")
        return x, y, c

    def _copy(self, a, k, src, dst, to, sems):
        s = a * self.COPIES + k
        return pltpu.make_async_remote_copy(src_ref=src, dst_ref=dst, send_sem=sems[0].at[s], recv_sem=sems[1].at[s],
                                            device_id=to, device_id_type=pl.DeviceIdType.MESH)

    def _gather_copies(self, ins, outs, sems):
        x, y, c = self._place()
        chips = [(1 - x, y), (x, 1 - y), (1 - x, 1 - y)]
        slot = lambda px, py, pc: 4 * px + 2 * py + pc
        first, passed, arrivals = [], [], []
        for a in range(self.n):
            mine = outs[a].at[slot(x, y, c)]
            first.append(self._copy(a, 0, ins[a], mine, (x, y, 1 - c), sems))
            arrivals.append(self._copy(a, 0, ins[a], outs[a].at[slot(x, y, 1 - c)], (x, y, 1 - c), sems))
            for j, (px, py) in enumerate(chips):
                first.append(self._copy(a, 1 + j, ins[a], mine, (px, py, c), sems))
                theirs = outs[a].at[slot(px, py, c)]
                passed.append((self._copy(a, 1 + j, ins[a], theirs, (px, py, c), sems),
                               self._copy(a, 4 + j, theirs, theirs, (x, y, 1 - c), sems)))
                arrivals.append(self._copy(a, 4 + j, ins[a], outs[a].at[slot(px, py, 1 - c)], (x, y, 1 - c), sems))
        return first, passed, arrivals

    def _scatter_copies(self, ins, outs, sems):
        x, y, c = self._place()
        me = 4 * x + 2 * y + c
        sends, arrivals = [], []
        for k in range(1, N_DEV):
            px = 1 - x if k & 4 else x
            py = 1 - y if k & 2 else y
            pc = 1 - c if k & 1 else c
            peer = 4 * px + 2 * py + pc
            for a in range(self.n):
                sends.append(self._copy(a, k - 1, ins[a].at[peer], outs[a].at[me], (px, py, pc), sems))
                arrivals.append(self._copy(a, k - 1, ins[a].at[peer], outs[a].at[peer], (px, py, pc), sems))
        return sends, arrivals

    def _local(self, ins, outs, sems):
        x, y, c = self._place()
        me = 4 * x + 2 * y + c
        return [pltpu.make_async_copy(ins[a] if self.gather else ins[a].at[me], outs[a].at[me], sems[2].at[a])
                for a in range(self.n)]

    def start(self, ins, outs, sems):
        for cp in self._local(ins, outs, sems):
            cp.start()
        sends = self._gather_copies(ins, outs, sems)[0] if self.gather else self._scatter_copies(ins, outs, sems)[0]
        for cp in sends:
            cp.start()

    def finish(self, ins, outs, sems):
        if self.gather:
            first, passed, arrivals = self._gather_copies(ins, outs, sems)
            for landed, onward in passed:
                landed.wait_recv()
                onward.start()
            sends = first + [onward for _, onward in passed]
        else:
            sends, arrivals = self._scatter_copies(ins, outs, sems)
        for cp in arrivals:
            cp.wait_recv()
        for cp in sends:
            cp.wait_send()
        for cp in self._local(ins, outs, sems):
            cp.wait()


def _call(body, *, name, grid, in_specs, out_specs, out_shape, args, semantics, scratch=(), exchange=None):
    if exchange is None:
        return pl.pallas_call(
            body, name=name, grid=grid, in_specs=in_specs, out_specs=out_specs, out_shape=out_shape,
            scratch_shapes=list(scratch), compiler_params=_params(*semantics))(*args)
    ex = exchange
    n_in, n_out, n_s = len(in_specs), len(out_specs), len(scratch)

    def wrapped(*refs):
        ins, ex_in = refs[:n_in], refs[n_in:n_in + ex.n]
        o0 = n_in + ex.n
        outs, ex_out = refs[o0:o0 + n_out], refs[o0 + n_out:o0 + n_out + ex.n]
        s0 = o0 + n_out + ex.n
        scr, sems = refs[s0:s0 + n_s], refs[s0 + n_s:]
        ids = [pl.program_id(ax) for ax in range(len(grid))]
        first = ids[0] == 0
        last = ids[0] == grid[0] - 1
        for ax in range(1, len(grid)):
            first = jnp.logical_and(first, ids[ax] == 0)
            last = jnp.logical_and(last, ids[ax] == grid[ax] - 1)

        @pl.when(first)
        def _():
            ex.start(ex_in, ex_out, sems)

        body(*ins, *outs, *scr)

        @pl.when(last)
        def _():
            ex.finish(ex_in, ex_out, sems)

    hbm = pl.BlockSpec(memory_space=pltpu.HBM)
    res = pl.pallas_call(
        wrapped, name=name, grid=grid, in_specs=list(in_specs) + [hbm] * ex.n,
        out_specs=list(out_specs) + [hbm] * ex.n, out_shape=list(out_shape) + ex.out_shape,
        scratch_shapes=list(scratch) + ex.scratch, compiler_params=_params(*(["arbitrary"] * len(grid))),
    )(*args, *ex.arrs)
    ex.received = res[n_out:]
    return res[:n_out]


def _exchange_now(arrs, *, name, gather):
    ex = _Exchange(arrs, gather)
    n = ex.n

    def body(*refs):
        ex.start(refs[:n], refs[n:2 * n], refs[2 * n:])
        ex.finish(refs[:n], refs[n:2 * n], refs[2 * n:])

    hbm = pl.BlockSpec(memory_space=pltpu.HBM)
    return pl.pallas_call(body, name=name, out_shape=ex.out_shape, in_specs=[hbm] * n, out_specs=[hbm] * n,
                          scratch_shapes=ex.scratch)(*arrs)


def _mm_nn(a, b, *, name, tn, tm, out_dtype=F32, res=None, scale=1.0, exchange=None):
    n, k = a.shape
    m = b.shape[1]
    tn, tm = _tile(n, tn, 16), _tile(m, tm, 128)

    def body(*refs):
        a_ref, b_ref = refs[0], refs[1]
        o_ref = refs[-1]
        acc = _dot(a_ref[...], b_ref[...])
        if scale != 1.0:
            acc = acc * scale
        if res is not None:
            acc = acc + refs[2][...]
        o_ref[...] = acc.astype(o_ref.dtype)

    in_specs = [pl.BlockSpec((tn, k), lambda i, j: (i, 0)), pl.BlockSpec((k, tm), lambda i, j: (0, j))]
    args = [a, b]
    if res is not None:
        in_specs.append(pl.BlockSpec((tn, tm), lambda i, j: (i, j)))
        args.append(res)
    return _call(body, name=name, grid=(n // tn, m // tm), in_specs=in_specs,
                 out_specs=[pl.BlockSpec((tn, tm), lambda i, j: (i, j))],
                 out_shape=[jax.ShapeDtypeStruct((n, m), out_dtype)], args=args, semantics=("parallel", "parallel"),
                 exchange=exchange)[0]


def _mm_nt(pairs, *, name, tn, tk, out_dtype=F32, exchange=None):
    n = pairs[0][0].shape[0]
    kk = pairs[0][1].shape[0]
    tn, tk = _tile(n, tn, 16), _tile(kk, tk, 128)
    npair = len(pairs)

    def body(*refs):
        o_ref = refs[-1]
        acc = _dot_nt(refs[0][...], refs[1][...])
        for q in range(1, npair):
            acc = acc + _dot_nt(refs[2 * q][...], refs[2 * q + 1][...])
        o_ref[...] = acc.astype(o_ref.dtype)

    in_specs, args = [], []
    for a, b in pairs:
        m = a.shape[1]
        in_specs += [pl.BlockSpec((tn, m), lambda i, j: (i, 0)), pl.BlockSpec((tk, m), lambda i, j: (j, 0))]
        args += [a, b]
    return _call(body, name=name, grid=(n // tn, kk // tk), in_specs=in_specs,
                 out_specs=[pl.BlockSpec((tn, tk), lambda i, j: (i, j))],
                 out_shape=[jax.ShapeDtypeStruct((n, kk), out_dtype)], args=args, semantics=("parallel", "parallel"),
                 exchange=exchange)[0]


def _mm_tn(a, b, *, name, tn, tm, exchange=None):
    n, k = a.shape
    m = b.shape[1]
    tn, tm = _tile(n, tn, 16), _tile(m, tm, 128)
    steps = n // tn

    def body(a_ref, b_ref, o_ref, acc):
        i = pl.program_id(1)

        @pl.when(i == 0)
        def _():
            acc[...] = jnp.zeros_like(acc)

        acc[...] += _dot_tn(a_ref[...], b_ref[...])

        @pl.when(i == steps - 1)
        def _():
            o_ref[...] = acc[...].astype(o_ref.dtype)

    return _call(body, name=name, grid=(m // tm, steps),
                 in_specs=[pl.BlockSpec((tn, k), lambda j, i: (i, 0)), pl.BlockSpec((tn, tm), lambda j, i: (i, j))],
                 out_specs=[pl.BlockSpec((k, tm), lambda j, i: (0, j))],
                 out_shape=[jax.ShapeDtypeStruct((k, m), BF16)], args=[a, b], semantics=("parallel", "arbitrary"),
                 scratch=[pltpu.VMEM((k, tm), F32)], exchange=exchange)[0]


def _ffn_up(h, w1, w3, *, name, exchange=None):
    n, k = h.shape
    m = w1.shape[1]
    tn = _tile(n, 512, 16)

    def body(h_ref, w1_ref, w3_ref, dsda_ref, dsdb_ref, s_ref):
        for c0 in range(0, m, COL_CHUNK):
            cols = slice(c0, c0 + COL_CHUNK)
            a = _dot(h_ref[...], w1_ref[:, cols])
            b = _dot(h_ref[...], w3_ref[:, cols])
            sg = _sigmoid(a)
            silu = a * sg
            dsda_ref[:, cols] = (b * (sg + silu * (1.0 - sg))).astype(dsda_ref.dtype)
            dsdb_ref[:, cols] = silu.astype(dsdb_ref.dtype)
            s_ref[:, cols] = (silu * b).astype(s_ref.dtype)

    ospec = _rows(tn, m)
    return _call(body, name=name, grid=(n // tn,),
                 in_specs=[_rows(tn, k), _resident(w1.shape), _resident(w3.shape)], out_specs=[ospec, ospec, ospec],
                 out_shape=[jax.ShapeDtypeStruct((n, m), BF16)] * 3,
                 args=[h, w1, w3], semantics=("parallel",), exchange=exchange)


def _mm_nn_wide(a, b, *, name, out_dtype, exchange=None):
    n, k = a.shape
    m = b.shape[1]
    tn = _tile(n, 512, 16)
    chunk = 2 * COL_CHUNK

    def body(a_ref, b_ref, o_ref):
        for c0 in range(0, m, chunk):
            cols = slice(c0, c0 + chunk)
            o_ref[:, cols] = _dot(a_ref[...], b_ref[:, cols]).astype(o_ref.dtype)

    return _call(body, name=name, grid=(n // tn,), in_specs=[_rows(tn, k), _resident(b.shape)],
                 out_specs=[_rows(tn, m)], out_shape=[jax.ShapeDtypeStruct((n, m), out_dtype)], args=[a, b],
                 semantics=("parallel",), exchange=exchange)[0]


def _mm_nn_res_rms(a, b, res, g, *, name, scale, exchange=None):
    n, k = a.shape
    d = b.shape[1]
    tn = _tile(n, 512, 16)

    def body(a_ref, b_ref, r_ref, g_ref, x_ref, h_ref):
        for c0 in range(0, d, COL_CHUNK):
            cols = slice(c0, c0 + COL_CHUNK)
            x_ref[:, cols] = r_ref[:, cols] + scale * _dot(a_ref[...], b_ref[:, cols])
        x = x_ref[...]
        r = lax.rsqrt(_rowmean(x * x) + RMS_EPS)
        h_ref[...] = (x * r * g_ref[...]).astype(h_ref.dtype)

    row = _rows(tn, d)
    return _call(body, name=name, grid=(n // tn,),
                 in_specs=[_rows(tn, k), _resident(b.shape), row, pl.BlockSpec((1, d), lambda i: (0, 0))],
                 out_specs=[row, row],
                 out_shape=[jax.ShapeDtypeStruct((n, d), F32), jax.ShapeDtypeStruct((n, d), BF16)],
                 args=[a, b, res, g], semantics=("parallel",), exchange=exchange)


def _ffn_bwd_mid(dxs, w2, dsda, dsdb, *, name, exchange=None):
    n, d = dxs.shape
    m = w2.shape[0]
    tn = _tile(n, 512, 16)

    def body(dx_ref, w2_ref, dsda_ref, dsdb_ref, da_ref, db_ref):
        for c0 in range(0, m, COL_CHUNK):
            cols = slice(c0, c0 + COL_CHUNK)
            ds = _dot_nt(dx_ref[...], w2_ref[cols, :])
            da_ref[:, cols] = (ds * dsda_ref[:, cols].astype(F32)).astype(da_ref.dtype)
            db_ref[:, cols] = (ds * dsdb_ref[:, cols].astype(F32)).astype(db_ref.dtype)

    tile = _rows(tn, m)
    return _call(body, name=name, grid=(n // tn,),
                 in_specs=[_rows(tn, d), _resident(w2.shape), tile, tile], out_specs=[tile, tile],
                 out_shape=[jax.ShapeDtypeStruct((n, m), BF16), jax.ShapeDtypeStruct((n, m), BF16)],
                 args=[dxs, w2, dsda, dsdb], semantics=("parallel",), exchange=exchange)


def _rowwise(fn, *, name, n, tn, ncol, rows, vecs, outs, accs=()):
    tn = _tile(n, tn, 16)
    nr, nv, no = len(rows), len(vecs), len(outs)

    def body(*refs):
        first = pl.program_id(1) == 0
        vals = [r[...].astype(F32) for r in refs[:nr + nv]]
        res = fn(*vals)
        for ref, val in zip(refs[nr + nv:nr + nv + no], res[:no]):
            ref[...] = val.astype(ref.dtype)
        for ref, val in zip(refs[nr + nv + no:], res[no:]):
            _accumulate(ref, val, first)

    in_specs = [pl.BlockSpec((tn, w), lambda j, i, c0=c0: (i, c0 + j)) for _, w, c0 in rows]
    in_specs += [pl.BlockSpec((1, w), lambda j, i, c0=c0: (0, c0 + j)) for _, w, c0 in vecs]
    out_specs = [pl.BlockSpec((tn, w), lambda j, i: (i, j)) for _, w, _ in outs]
    out_specs += [pl.BlockSpec((1, w), lambda j, i: (0, j)) for _, w in accs]
    out_shape = [jax.ShapeDtypeStruct((n, tw), dt) for tw, _, dt in outs]
    out_shape += [jax.ShapeDtypeStruct((1, tw), F32) for tw, _ in accs]
    return pl.pallas_call(
        body, name=name, grid=(ncol, n // tn), in_specs=in_specs, out_specs=out_specs, out_shape=out_shape,
        compiler_params=_params("parallel", "arbitrary"),
    )(*[r[0] for r in rows], *[v[0] for v in vecs])


def _accumulate(ref, val, first):
    @pl.when(first)
    def _():
        ref[...] = jnp.zeros_like(ref)

    ref[...] += val


def _colsum(x):
    return jnp.sum(x, axis=0, keepdims=True)


def _rowmean(x):
    return jnp.mean(x, axis=-1, keepdims=True)


def _rms_fwd(x, g, *, name):
    def fn(x_, g_):
        r = lax.rsqrt(_rowmean(x_ * x_) + RMS_EPS)
        return (x_ * r * g_,)

    n, d = x.shape
    return _rowwise(fn, name=name, n=n, tn=512, ncol=1, rows=[(x, d, 0)], vecs=[(g, d, 0)], outs=[(d, d, BF16)])[0]


def _mm_nt_rms_bwd(pairs, x, extra, g, *, name, tn, half_scale, exchange=None):
    n, d = x.shape
    tn = _tile(n, tn, 16)
    pairs = [(list(a) if isinstance(a, (list, tuple)) else [a], b) for a, b in pairs]
    nref = sum(len(a) + 1 for a, _ in pairs)

    def body(*refs):
        x_ref, e_ref, g_ref, dx_ref, dxs_ref, dg_ref = refs[nref:]
        dh, at = None, 0
        for parts, _ in pairs:
            b_ref = refs[at + len(parts)]
            col = 0
            for j, part in enumerate(parts):
                w = part.shape[1]
                term = _dot_nt(refs[at + j][...], b_ref[:, col:col + w])
                dh = term if dh is None else dh + term
                col += w
            at += len(parts) + 1
        x_ = x_ref[...]
        r = lax.rsqrt(_rowmean(x_ * x_) + RMS_EPS)
        xh = x_ * r
        dxh = dh * g_ref[...]
        dx = e_ref[...] + r * (dxh - xh * _rowmean(dxh * xh))
        dx_ref[...] = dx
        dxs_ref[...] = (dx * half_scale).astype(dxs_ref.dtype)
        _accumulate(dg_ref, _colsum(dh * xh), pl.program_id(0) == 0)

    in_specs, args = [], []
    for parts, b in pairs:
        assert sum(part.shape[1] for part in parts) == b.shape[1]
        in_specs += [pl.BlockSpec((tn, part.shape[1]), lambda i: (i, 0)) for part in parts]
        in_specs.append(pl.BlockSpec(b.shape, lambda i: (0, 0), pipeline_mode=pl.Buffered(1)))
        args += parts + [b]
    row = pl.BlockSpec((tn, d), lambda i: (i, 0))
    vec = pl.BlockSpec((1, d), lambda i: (0, 0))
    return _call(body, name=name, grid=(n // tn,), in_specs=in_specs + [row, row, vec], out_specs=[row, row, vec],
                 out_shape=[jax.ShapeDtypeStruct((n, d), F32), jax.ShapeDtypeStruct((n, d), BF16),
                            jax.ShapeDtypeStruct((1, d), F32)],
                 args=args + [x, extra, g], semantics=("arbitrary",), exchange=exchange)


def _ple_final(x3, gpre, z, tgt, gpp, gf, *, name):
    def fn(x3_, gpre_, z_, tgt_, gpp_, gf_):
        gate = _sigmoid(gpre_)
        rz = lax.rsqrt(_rowmean(z_ * z_) + RMS_EPS)
        zh = z_ * rz
        e = zh * gpp_
        x4 = x3_ + gate * e
        r4 = lax.rsqrt(_rowmean(x4 * x4) + RMS_EPS)
        x4h = x4 * r4
        diff = x4h * gf_ - tgt_
        dout = diff * (1.0 / D_MODEL)
        dxh4 = dout * gf_
        dx4 = r4 * (dxh4 - x4h * _rowmean(dxh4 * x4h))
        dpre = dx4 * e * gate * (1.0 - gate)
        de = dx4 * gate
        dzh = de * gpp_
        dz = rz * (dzh - zh * _rowmean(dzh * zh))
        return dx4, dpre, dz, _colsum(diff * diff) * (0.5 / D_MODEL), _colsum(dout * x4h), _colsum(de * zh)

    n, d = x3.shape
    return _rowwise(fn, name=name, n=n, tn=256, ncol=1, rows=[(x3, d, 0), (gpre, d, 0), (z, d, 0), (tgt, d, 0)],
                    vecs=[(gpp, d, 0), (gf, d, 0)], outs=[(d, d, F32), (d, d, BF16), (d, d, BF16)],
                    accs=[(d, d), (d, d), (d, d)])


def _merge_fwd(proj, ya, yb, *, name):
    def fn(ga, gb, ya_, yb_):
        return (_sigmoid(ga) * ya_ + _sigmoid(gb) * yb_,)

    n = proj.shape[0]
    w = 512
    return _rowwise(fn, name=name, n=n, tn=512, ncol=D_MODEL // w,
                    rows=[(proj, w, COL_GA // w), (proj, w, COL_GB // w), (ya, w, 0), (yb, w, 0)], vecs=[],
                    outs=[(D_MODEL, w, BF16)])[0]


def _merge_bwd(dy, proj, ya, yb, *, name):
    def fn(dy_, ga, gb, ya_, yb_):
        sa, sb = _sigmoid(ga), _sigmoid(gb)
        return dy_ * sa, dy_ * sb, dy_ * ya_ * sa * (1.0 - sa), dy_ * yb_ * sb * (1.0 - sb)

    n = proj.shape[0]
    w = 512
    return _rowwise(fn, name=name, n=n, tn=512, ncol=D_MODEL // w,
                    rows=[(dy, w, 0), (proj, w, COL_GA // w), (proj, w, COL_GB // w), (ya, w, 0), (yb, w, 0)],
                    vecs=[], outs=[(D_MODEL, w, BF16)] * 4)


def _hgrn_post_fwd(o, proj, onorm, *, name):
    def fn(o_, og, gam):
        r = lax.rsqrt(_rowmean(o_ * o_) + RMS_EPS)
        return (o_ * r * gam * (og * _sigmoid(og)),)

    n = o.shape[0]
    w = HEAD_DIM
    return _rowwise(fn, name=name, n=n, tn=1024, ncol=HEADS, rows=[(o, w, 0), (proj, w, COL_OG // w)],
                    vecs=[(onorm, w, 0)], outs=[(D_MODEL, w, BF16)])[0]


def _hgrn_post_bwd(don, o, proj, onorm, *, name):
    def fn(don_, o_, og, gam):
        r = lax.rsqrt(_rowmean(o_ * o_) + RMS_EPS)
        oh = o_ * r
        sg = _sigmoid(og)
        dog = don_ * oh * gam * (sg * (1.0 + og * (1.0 - sg)))
        dn = don_ * (og * sg)
        doh = dn * gam
        do = r * (doh - oh * _rowmean(doh * oh))
        return dog, do, _colsum(dn * oh)

    n = o.shape[0]
    w = HEAD_DIM
    return _rowwise(fn, name=name, n=n, tn=1024, ncol=HEADS, rows=[(don, w, 0), (o, w, 0), (proj, w, COL_OG // w)],
                    vecs=[(onorm, w, 0)], outs=[(D_MODEL, w, BF16), (D_MODEL, w, F32)], accs=[(D_MODEL, w)])


def _split3(x):
    hi = x.astype(BF16)
    r1 = x - hi.astype(F32)
    mid = r1.astype(BF16)
    lo = (r1 - mid.astype(F32)).astype(BF16)
    return hi, mid, lo


def _tri_sum(tri, x):
    hi, mid, lo = _split3(x)
    return _dot(tri, hi) + _dot(tri, mid) + _dot(tri, lo)


def _lower_bound(lb_ref):
    return 1.0 / (1.0 + jnp.exp(lb_ref[1:2, :] - lb_ref[0:1, :]))


def _hgrn_specs(n, t, reverse):
    nt = n // t
    width = HGRN_HEADS_PER_STEP * HEAD_DIM

    def tok(i):
        return nt - 1 - i if reverse else i

    def sec(col):
        c0 = col // width
        return pl.BlockSpec((t, width), lambda h, i: (tok(i), c0 + h))

    head_tile = pl.BlockSpec((t, width), lambda h, i: (tok(i), h))
    state = pl.BlockSpec((HGRN_HEADS_PER_STEP, t // CHUNK, HEAD_DIM, HEAD_DIM), lambda h, i: (h, tok(i), 0, 0))
    lb = pl.BlockSpec((2, width), lambda h, i: (0, h))
    return sec, head_tile, state, lb


def _hgrn_fwd(proj, hgrn_lb, *, name):
    n = proj.shape[0]
    t = _tile(n, 512, CHUNK)
    nc = t // CHUNK
    hps = HGRN_HEADS_PER_STEP
    width = hps * HEAD_DIM
    lanes = [slice(h * HEAD_DIM, (h + 1) * HEAD_DIM) for h in range(hps)]
    sec, head_tile, state, lbspec = _hgrn_specs(n, t, False)

    def body(q_ref, f_ref, i_ref, lb_ref, o_ref, st_ref, s_acc, g_s, a_s):
        @pl.when(pl.program_id(1) == 0)
        def _():
            s_acc[...] = jnp.zeros_like(s_acc)

        lb = _lower_bound(lb_ref)
        row = lax.broadcasted_iota(jnp.int32, (CHUNK, CHUNK), 0)
        col = lax.broadcasted_iota(jnp.int32, (CHUNK, CHUNK), 1)
        tril = row >= col
        trilb = jnp.where(tril, 1.0, 0.0).astype(BF16)
        rowk = lax.broadcasted_iota(jnp.int32, (CHUNK, width), 0)

        def chunk(c, carry):
            rows = pl.ds(pl.multiple_of(c * CHUNK, CHUNK), CHUNK)
            qr, fr, v = [r[rows, :].astype(F32) for r in (q_ref, f_ref, i_ref)]
            q = qr * _sigmoid(qr)
            f = lb + (1.0 - lb) * _sigmoid(fr)
            k = 1.0 - f
            g = _tri_sum(trilb, jnp.log(f))
            g_s[...] = g
            st0 = [s_acc[h] for h in range(hps)]
            for h in range(hps):
                st_ref[h, c] = st0[h]
            vb = v.astype(BF16)
            for blk in range(CHUNK // SUB):
                lo, hi = blk * SUB, (blk + 1) * SUB
                gref = g_s[lo - 1:lo, :] if blk else jnp.zeros((1, width), F32)
                qi = (q[lo:hi] * jnp.exp(g[lo:hi] - gref)).astype(BF16)
                ki = (k * jnp.exp(jnp.where(rowk < hi, gref - g, NEG_BIG))).astype(BF16)
                for h, ln in enumerate(lanes):
                    a_s[h, lo:hi, :] = _dot_nt(qi[:, ln], ki[:, ln])
            qeb = (q * jnp.exp(g)).astype(BF16)
            o_ref[rows, :] = jnp.concatenate(
                [_dot(jnp.where(tril, a_s[h], 0.0).astype(BF16), vb[:, ln]) + _dot_nt(qeb[:, ln], st0[h].astype(BF16))
                 for h, ln in enumerate(lanes)], axis=1)
            glast = g_s[CHUNK - 1:CHUNK, :]
            kdb = (k * jnp.exp(glast - g)).astype(BF16)
            dec = jnp.exp(glast)
            for h, ln in enumerate(lanes):
                s_acc[h] = st0[h] * dec[:, ln] + _dot_tn(vb[:, ln], kdb[:, ln])
            return carry

        lax.fori_loop(0, nc, chunk, 0)

    return pl.pallas_call(
        body, name=name, grid=(HEADS // hps, n // t),
        in_specs=[sec(COL_Q), sec(COL_F), sec(COL_I), lbspec], out_specs=[head_tile, state],
        out_shape=[jax.ShapeDtypeStruct((n, D_MODEL), F32),
                   jax.ShapeDtypeStruct((HEADS, n // CHUNK, HEAD_DIM, HEAD_DIM), F32)],
        scratch_shapes=[pltpu.VMEM((hps, HEAD_DIM, HEAD_DIM), F32), pltpu.VMEM((CHUNK, width), F32),
                        pltpu.VMEM((hps, CHUNK, CHUNK), F32)],
        compiler_params=_params("parallel", "arbitrary"),
    )(proj, proj, proj, hgrn_lb)


def _hgrn_bwd(proj, hgrn_lb, do, states, *, name):
    n = proj.shape[0]
    t = _tile(n, 512, CHUNK)
    nc = t // CHUNK
    hps = HGRN_HEADS_PER_STEP
    width = hps * HEAD_DIM
    lanes = [slice(h * HEAD_DIM, (h + 1) * HEAD_DIM) for h in range(hps)]
    sec, head_tile, state, lbspec = _hgrn_specs(n, t, True)

    def body(q_ref, f_ref, i_ref, lb_ref, do_ref, st_ref, dqfi_ref, dlb_ref, d_acc, g_s, a_s, dq_s,
             dg_s):
        first = pl.program_id(1) == 0

        @pl.when(first)
        def _():
            d_acc[...] = jnp.zeros_like(d_acc)

        lb = _lower_bound(lb_ref)
        row = lax.broadcasted_iota(jnp.int32, (CHUNK, CHUNK), 0)
        col = lax.broadcasted_iota(jnp.int32, (CHUNK, CHUNK), 1)
        tril = row >= col
        trilb = jnp.where(tril, 1.0, 0.0).astype(BF16)
        triub = jnp.where(row <= col, 1.0, 0.0).astype(BF16)
        rowk = lax.broadcasted_iota(jnp.int32, (CHUNK, width), 0)

        def per_head(fn):
            return jnp.concatenate([fn(h, ln) for h, ln in enumerate(lanes)], axis=1)

        def chunk(j, dlb):
            c = nc - 1 - j
            rows = pl.ds(pl.multiple_of(c * CHUNK, CHUNK), CHUNK)
            qr, fr, v, dout = [r[rows, :].astype(F32) for r in (q_ref, f_ref, i_ref, do_ref)]
            sq = _sigmoid(qr)
            q = qr * sq
            sf = _sigmoid(fr)
            f = lb + (1.0 - lb) * sf
            k = 1.0 - f
            g = _tri_sum(trilb, jnp.log(f))
            g_s[...] = g
            st0 = [st_ref[h, c] for h in range(hps)]
            dt = [d_acc[h] for h in range(hps)]
            vb, dob = v.astype(BF16), dout.astype(BF16)
            dtb = [x.astype(BF16) for x in dt]
            st0b = [x.astype(BF16) for x in st0]
            glast = g_s[CHUNK - 1:CHUNK, :]
            eg = jnp.exp(g)
            kdec = jnp.exp(glast - g)
            qeb, kdb = (q * eg).astype(BF16), (k * kdec).astype(BF16)
            aps = [jnp.where(row > col, _dot_nt(dob[:, ln], vb[:, ln]), 0.0) for ln in lanes]
            dov = dout * v
            adiag = per_head(lambda h, ln: jnp.broadcast_to(
                jnp.sum(dov[:, ln], axis=-1, keepdims=True), (CHUNK, HEAD_DIM)))
            dq_inter = per_head(lambda h, ln: _dot(dob[:, ln], st0b[h]))
            dk_inter = per_head(lambda h, ln: _dot(vb[:, ln], dtb[h]))
            dk_st = kdec * dk_inter
            dg = qeb.astype(F32) * dq_inter
            dg_minus = kdb.astype(F32) * dk_inter
            dg = dg - dg_minus
            for blk in range(CHUNK // SUB):
                lo, hi = blk * SUB, (blk + 1) * SUB
                gref = g_s[lo - 1:lo, :] if blk else jnp.zeros((1, width), F32)
                qscale = jnp.exp(g[lo:hi] - gref)
                kscale = jnp.exp(jnp.where(rowk < hi, gref - g, NEG_BIG))
                qi = (q[lo:hi] * qscale).astype(BF16)
                ki = (k * kscale).astype(BF16)
                for h, ln in enumerate(lanes):
                    a_s[h, lo:hi, :] = _dot_nt(qi[:, ln], ki[:, ln])
                apb = [x[lo:hi].astype(BF16) for x in aps]
                from_k = per_head(lambda h, ln: _dot(apb[h], ki[:, ln]))
                from_q = per_head(lambda h, ln: _dot_tn(apb[h], qi[:, ln]))
                dq_s[lo:hi, :] = qscale * from_k
                dg_s[lo:hi, :] = qi.astype(F32) * from_k
                dk_st = dk_st + kscale * from_q
                dg = dg - ki.astype(F32) * from_q
            dg = dg + dg_s[...]
            dv = per_head(lambda h, ln: _dot_tn(jnp.where(tril, a_s[h], 0.0).astype(BF16), dob[:, ln])
                          + _dot_nt(kdb[:, ln], dtb[h]))
            dq_st = dq_s[...] + eg * dq_inter
            dq = dq_st + adiag * k
            dk = dk_st + adiag * q
            dec = jnp.exp(glast)
            dt_dec = [dt[h] * dec[:, ln] for h, ln in enumerate(lanes)]
            for h, ln in enumerate(lanes):
                d_acc[h] = dt_dec[h] + _dot_tn(dob[:, ln], qeb[:, ln])
            later = per_head(lambda h, ln: _colsum(dt_dec[h] * st0[h])) + _colsum(dg_minus)
            dlf = later + _tri_sum(triub, dg)
            df = dlf / f - dk
            dqfi_ref[rows, 0:width] = (dq * (sq * (1.0 + qr * (1.0 - sq)))).astype(dqfi_ref.dtype)
            dqfi_ref[rows, width:2 * width] = (df * (1.0 - lb) * sf * (1.0 - sf)).astype(dqfi_ref.dtype)
            dqfi_ref[rows, 2 * width:3 * width] = dv.astype(dqfi_ref.dtype)
            return dlb + _colsum(df * (1.0 - sf))

        dlb = lax.fori_loop(0, nc, chunk, jnp.zeros((1, width), F32))
        _accumulate(dlb_ref, dlb, first)

    assert hps == HEADS
    nt = n // t
    return pl.pallas_call(
        body, name=name, grid=(1, nt),
        in_specs=[sec(COL_Q), sec(COL_F), sec(COL_I), lbspec, head_tile, state],
        out_specs=[pl.BlockSpec((t, 3 * width), lambda h, i: (nt - 1 - i, 0)),
                   pl.BlockSpec((1, width), lambda h, i: (0, h))],
        out_shape=[jax.ShapeDtypeStruct((n, 3 * D_MODEL), BF16), jax.ShapeDtypeStruct((1, D_MODEL), F32)],
        scratch_shapes=[pltpu.VMEM((hps, HEAD_DIM, HEAD_DIM), F32), pltpu.VMEM((CHUNK, width), F32),
                        pltpu.VMEM((hps, CHUNK, CHUNK), F32), pltpu.VMEM((CHUNK, width), F32),
                        pltpu.VMEM((CHUNK, width), F32)],
        compiler_params=_params("parallel", "arbitrary"),
    )(proj, proj, proj, hgrn_lb, do, states)


def _pool_fwd(proj, pool_w, pool_scale, *, name):
    n = proj.shape[0]
    t = _tile(n, 512, POOL_HALO)
    per = t // POOL_HALO
    c0 = COL_POOL // POOL_WIDTH

    def body(u_ref, halo_ref, pw_ref, ps_ref, pooled_ref, mixed_ref, ext):
        i = pl.program_id(0)
        u = u_ref[...].astype(F32)
        ext[POOL_HALO:POOL_HALO + t, :] = u
        ext[0:POOL_HALO, :] = jnp.where(i > 0, halo_ref[...].astype(F32), 0.0)
        pos = i * t + lax.broadcasted_iota(jnp.int32, (t, POOL_CH), 0) + 1
        for grp, win in enumerate(POOL_WINDOWS):
            cols = slice(grp * POOL_CH, (grp + 1) * POOL_CH)
            acc = u[:, cols]
            for j in range(1, win):
                acc = acc + ext[POOL_HALO - j:POOL_HALO - j + t, cols]
            pooled = (acc / jnp.minimum(pos, win).astype(F32) - u[:, cols]).astype(BF16)
            pooled_ref[:, cols] = pooled
            mixed_ref[:, cols] = (_dot(pooled, pw_ref[grp].astype(BF16)) * ps_ref[:, cols]).astype(BF16)

    tile = pl.BlockSpec((t, POOL_WIDTH), lambda i: (i, 0))
    return pl.pallas_call(
        body, name=name, grid=(n // t,),
        in_specs=[pl.BlockSpec((t, POOL_WIDTH), lambda i: (i, c0)),
                  pl.BlockSpec((POOL_HALO, POOL_WIDTH), lambda i: (jnp.maximum(i * per - 1, 0), c0)),
                  pl.BlockSpec((len(POOL_WINDOWS), POOL_CH, POOL_CH), lambda i: (0, 0, 0)),
                  pl.BlockSpec((1, POOL_WIDTH), lambda i: (0, 0))],
        out_specs=[tile, tile],
        out_shape=[jax.ShapeDtypeStruct((n, POOL_WIDTH), BF16), jax.ShapeDtypeStruct((n, POOL_WIDTH), BF16)],
        scratch_shapes=[pltpu.VMEM((t + POOL_HALO, POOL_WIDTH), F32)],
        compiler_params=_params("parallel"),
    )(proj, proj, pool_w, pool_scale)


def _pool_bwd(dmixed, pooled, pool_w, pool_scale, *, name):
    n = dmixed.shape[0]
    t = _tile(n, 512, POOL_HALO)
    per = t // POOL_HALO
    nb = n // t

    def body(dm_ref, dmh_ref, p_ref, pw_ref, ps_ref, du_ref, dpw_ref, dps_ref, ext):
        i = pl.program_id(0)

        @pl.when(i == 0)
        def _():
            dpw_ref[...] = jnp.zeros_like(dpw_ref)
            dps_ref[...] = jnp.zeros_like(dps_ref)

        dm, dmh = dm_ref[...], dmh_ref[...]
        pos = i * t + lax.broadcasted_iota(jnp.int32, (t, POOL_CH), 0) + 1
        for grp, win in enumerate(POOL_WINDOWS):
            cols = slice(grp * POOL_CH, (grp + 1) * POOL_CH)
            pwb = pw_ref[grp].astype(BF16)
            pb = p_ref[:, cols]
            scale = ps_ref[:, cols]
            dps_ref[:, cols] += _colsum(dm[:, cols] * _dot(pb, pwb))
            dpm = (dm[:, cols] * scale).astype(BF16)
            dpw_ref[grp] += _dot_tn(pb, dpm)
            dpool = _dot_nt(dpm, pwb)
            dpool_next = _dot_nt((dmh[:, cols] * scale).astype(BF16), pwb)
            ext[0:t, cols] = dpool / jnp.minimum(pos, win).astype(F32)
            ext[t:t + POOL_HALO, cols] = jnp.where(i < nb - 1, dpool_next * (1.0 / win), 0.0)
            acc = -dpool
            for j in range(win):
                acc = acc + ext[j:j + t, cols]
            du_ref[:, cols] = acc.astype(du_ref.dtype)

    tile = pl.BlockSpec((t, POOL_WIDTH), lambda i: (i, 0))
    return pl.pallas_call(
        body, name=name, grid=(nb,),
        in_specs=[tile, pl.BlockSpec((POOL_HALO, POOL_WIDTH), lambda i: (jnp.minimum((i + 1) * per, nb * per - 1), 0)),
                  tile, pl.BlockSpec((len(POOL_WINDOWS), POOL_CH, POOL_CH), lambda i: (0, 0, 0)),
                  pl.BlockSpec((1, POOL_WIDTH), lambda i: (0, 0))],
        out_specs=[tile, pl.BlockSpec((len(POOL_WINDOWS), POOL_CH, POOL_CH), lambda i: (0, 0, 0)),
                   pl.BlockSpec((1, POOL_WIDTH), lambda i: (0, 0))],
        out_shape=[jax.ShapeDtypeStruct((n, POOL_WIDTH), BF16),
                   jax.ShapeDtypeStruct((len(POOL_WINDOWS), POOL_CH, POOL_CH), F32),
                   jax.ShapeDtypeStruct((1, POOL_WIDTH), F32)],
        scratch_shapes=[pltpu.VMEM((t + POOL_HALO, POOL_WIDTH), F32)],
        compiler_params=_params("arbitrary"),
    )(dmixed, dmixed, pooled, pool_w, pool_scale)


def _adamw(w, g, m, v):
    m2 = ADAM_B1 * m + (1.0 - ADAM_B1) * g
    v2 = ADAM_B2 * v + (1.0 - ADAM_B2) * (g * g)
    m_hat = m2 * (1.0 / (1.0 - ADAM_B1 ** ADAM_STEP))
    v_hat = v2 * (1.0 / (1.0 - ADAM_B2 ** ADAM_STEP))
    delta = -ADAM_LR * (m_hat / (jnp.sqrt(v_hat) + ADAM_EPS) + ADAM_WD * w)
    return delta, m2, v2


def _adam_big(recv, w, m, v, *, name):
    r, c = w.shape
    tr = _tile(r, 256, 16)

    def body(recv_ref, w_ref, m_ref, v_ref, g_ref, d_ref, m2_ref, v2_ref):
        g = recv_ref[0].astype(F32)
        for i in range(1, N_DEV):
            g = g + recv_ref[i].astype(F32)
        delta, m2, v2 = _adamw(w_ref[...], g, m_ref[...], v_ref[...])
        g_ref[...] = g
        d_ref[...] = delta
        m2_ref[...] = m2
        v2_ref[...] = v2

    tile = pl.BlockSpec((tr, c), lambda i: (i, 0))
    out = jax.ShapeDtypeStruct((r, c), F32)
    return pl.pallas_call(
        body, name=name, grid=(r // tr,),
        in_specs=[pl.BlockSpec((N_DEV, tr, c), lambda i: (0, i, 0)), tile, tile, tile],
        out_specs=[tile] * 4, out_shape=[out] * 4, compiler_params=_params("parallel"),
    )(recv, w, m, v)


def _adam_small(parts, w, m, v, *, name):
    lb0, lbn = SMALL_ROWS["hgrn_lb"]
    half = lbn // 2

    def body(parts_ref, w_ref, m_ref, v_ref, g_ref, d_ref, m2_ref, v2_ref):
        g = parts_ref[0]
        for i in range(1, N_DEV):
            g = g + parts_ref[i]
        w_ = w_ref[...]
        s0 = 1.0 / (1.0 + jnp.exp(w_[lb0 + half:lb0 + lbn] - w_[lb0:lb0 + half]))
        ga = g[lb0:lb0 + half] * s0 * (1.0 - s0)
        g = jnp.concatenate([g[:lb0], ga, -ga, g[lb0 + lbn:]], axis=0)
        delta, m2, v2 = _adamw(w_, g, m_ref[...], v_ref[...])
        g_ref[...] = g
        d_ref[...] = delta
        m2_ref[...] = m2
        v2_ref[...] = v2

    out = jax.ShapeDtypeStruct(w.shape, F32)
    return pl.pallas_call(body, name=name, out_shape=[out] * 4, compiler_params=_params())(parts, w, m, v)


def _pack_small(vals):
    pieces, at = [], 0
    for name, (row0, nrows) in SMALL_ROWS.items():
        if row0 > at:
            pieces.append(jnp.zeros((row0 - at, 128), F32))
        pieces.append(vals[name].astype(F32).reshape(nrows, 128))
        at = row0 + nrows
    if at < SMALL_TOTAL_ROWS:
        pieces.append(jnp.zeros((SMALL_TOTAL_ROWS - at, 128), F32))
    return jnp.concatenate(pieces, axis=0)


def _unpack_small(packed, shapes):
    return {name: packed[row0:row0 + nrows].reshape(shapes[name]) for name, (row0, nrows) in SMALL_ROWS.items()}


SPLIT_AXIS = dict(BIG_WEIGHTS)


def _gather_of(names, weights):
    return _Exchange([weights[k][0].astype(BF16) for k in names], gather=True)


def _scatter_of(names, dfull):
    return _Exchange([_to_slots(dfull[k], SPLIT_AXIS[k]) for k in names], gather=False)


def _to_slots(dw, axis):
    k, m = dw.shape
    if axis == 0:
        return dw.reshape(N_DEV, k // N_DEV, m)
    return dw.reshape(k, N_DEV, m // N_DEV).transpose(1, 0, 2)


def _from_slots(gathered, axis):
    _, r, c = gathered.shape
    if axis == 0:
        return gathered.reshape(N_DEV * r, c)
    return gathered.transpose(1, 0, 2).reshape(r, N_DEV * c)


def kernel(x, p, ffn1_norm, ffn1_w1, ffn1_w3, ffn1_w2, mix_norm, w_in, hgrn_lb, hgrn_onorm, w_branch_a, pool_w, pool_scale, w_branch_b, w_out, ffn2_norm, ffn2_w1, ffn2_w3, ffn2_w2, ple_norm, ple_w_gate, ple_w_proj, ple_post_norm, final_norm, loss_target, m_ffn1_norm, m_ffn1_w1, m_ffn1_w3, m_ffn1_w2, m_mix_norm, m_w_in, m_hgrn_lb, m_hgrn_onorm, m_w_branch_a, m_pool_w, m_pool_scale, m_w_branch_b, m_w_out, m_ffn2_norm, m_ffn2_w1, m_ffn2_w3, m_ffn2_w2, m_ple_norm, m_ple_w_gate, m_ple_w_proj, m_ple_post_norm, m_final_norm, v_ffn1_norm, v_ffn1_w1, v_ffn1_w3, v_ffn1_w2, v_mix_norm, v_w_in, v_hgrn_lb, v_hgrn_onorm, v_w_branch_a, v_pool_w, v_pool_scale, v_w_branch_b, v_w_out, v_ffn2_norm, v_ffn2_w1, v_ffn2_w3, v_ffn2_w2, v_ple_norm, v_ple_w_gate, v_ple_w_proj, v_ple_post_norm, v_final_norm):
    weights = dict(ffn1_norm=ffn1_norm, ffn1_w1=ffn1_w1, ffn1_w3=ffn1_w3, ffn1_w2=ffn1_w2, mix_norm=mix_norm, w_in=w_in, hgrn_lb=hgrn_lb, hgrn_onorm=hgrn_onorm, w_branch_a=w_branch_a, pool_w=pool_w, pool_scale=pool_scale, w_branch_b=w_branch_b, w_out=w_out, ffn2_norm=ffn2_norm, ffn2_w1=ffn2_w1, ffn2_w3=ffn2_w3, ffn2_w2=ffn2_w2, ple_norm=ple_norm, ple_w_gate=ple_w_gate, ple_w_proj=ple_w_proj, ple_post_norm=ple_post_norm, final_norm=final_norm)
    mom1 = dict(ffn1_norm=m_ffn1_norm, ffn1_w1=m_ffn1_w1, ffn1_w3=m_ffn1_w3, ffn1_w2=m_ffn1_w2, mix_norm=m_mix_norm, w_in=m_w_in, hgrn_lb=m_hgrn_lb, hgrn_onorm=m_hgrn_onorm, w_branch_a=m_w_branch_a, pool_w=m_pool_w, pool_scale=m_pool_scale, w_branch_b=m_w_branch_b, w_out=m_w_out, ffn2_norm=m_ffn2_norm, ffn2_w1=m_ffn2_w1, ffn2_w3=m_ffn2_w3, ffn2_w2=m_ffn2_w2, ple_norm=m_ple_norm, ple_w_gate=m_ple_w_gate, ple_w_proj=m_ple_w_proj, ple_post_norm=m_ple_post_norm, final_norm=m_final_norm)
    mom2 = dict(ffn1_norm=v_ffn1_norm, ffn1_w1=v_ffn1_w1, ffn1_w3=v_ffn1_w3, ffn1_w2=v_ffn1_w2, mix_norm=v_mix_norm, w_in=v_w_in, hgrn_lb=v_hgrn_lb, hgrn_onorm=v_hgrn_onorm, w_branch_a=v_w_branch_a, pool_w=v_pool_w, pool_scale=v_pool_scale, w_branch_b=v_w_branch_b, w_out=v_w_out, ffn2_norm=v_ffn2_norm, ffn2_w1=v_ffn2_w1, ffn2_w3=v_ffn2_w3, ffn2_w2=v_ffn2_w2, ple_norm=v_ple_norm, ple_w_gate=v_ple_w_gate, ple_w_proj=v_ple_w_proj, ple_post_norm=v_ple_post_norm, final_norm=v_final_norm)

    xs = x[0]
    ps = p[0, 0].astype(BF16)
    tgt = loss_target[0]
    n = xs.shape[0]

    g_f1, g_mix, g_on, g_f2 = ffn1_norm, mix_norm, hgrn_onorm, ffn2_norm
    g_ple, g_post, g_fin = ple_norm, ple_post_norm, final_norm.reshape(1, D_MODEL)
    lb2 = hgrn_lb
    pw, pscale = pool_w[0], pool_scale

    full = {}

    def keep(names, gathered):
        for k, g in zip(names, gathered):
            full[k] = _from_slots(g, SPLIT_AXIS[k])

    first_names = ("ffn1_w1", "ffn1_w3")
    keep(first_names, _exchange_now([weights[k][0].astype(BF16) for k in first_names], name="gather_first", gather=True))
    h1 = _rms_fwd(xs, g_f1, name="ffn1_rms")
    names = ("ffn1_w2", "w_in")
    ex = _gather_of(names, weights)
    a1, b1, s1 = _ffn_up(h1, full["ffn1_w1"], full["ffn1_w3"], name="ffn1_up", exchange=ex)
    keep(names, ex.received)
    names = ("w_branch_a", "w_branch_b", "w_out")
    ex = _gather_of(names, weights)
    x1, h2 = _mm_nn_res_rms(s1, full["ffn1_w2"], xs, g_mix, name="ffn1_down", scale=0.5, exchange=ex)
    keep(names, ex.received)
    names = ("ffn2_w1", "ffn2_w3", "ffn2_w2", "ple_w_gate", "ple_w_proj")
    ex = _gather_of(names, weights)
    proj = _mm_nn_wide(h2, full["w_in"], name="w_in_proj", out_dtype=BF16, exchange=ex)
    keep(names, ex.received)
    o, states = _hgrn_fwd(proj, lb2, name="hgrn_fwd")
    on = _hgrn_post_fwd(o, proj, g_on, name="hgrn_post_fwd")
    ya = _mm_nn(on, full["w_branch_a"], name="branch_a", tn=1024, tm=512)
    pooled, mixed = _pool_fwd(proj, pw, pscale, name="pool_fwd")
    yb = _mm_nn(mixed, full["w_branch_b"], name="branch_b", tn=1024, tm=512)
    y = _merge_fwd(proj, ya, yb, name="merge_fwd")
    x2, h3 = _mm_nn_res_rms(y, full["w_out"], x1, g_f2, name="w_out_proj", scale=1.0)
    a2, b2, s2 = _ffn_up(h3, full["ffn2_w1"], full["ffn2_w3"], name="ffn2_up")
    x3, h4 = _mm_nn_res_rms(s2, full["ffn2_w2"], x2, g_ple, name="ffn2_down", scale=0.5)
    gpre = _mm_nn(h4, full["ple_w_gate"], name="ple_gate", tn=1024, tm=512)
    z = _mm_nn(ps, full["ple_w_proj"], name="ple_proj", tn=1024, tm=512)
    dx4, dpre, dz, loss_part, d_fin, d_post = _ple_final(x3, gpre, z, tgt, g_post, g_fin, name="ple_final")

    dfull, received = {}, {}

    def sent(names, exchange):
        received.update(zip(names, exchange.received))

    dfull["ple_w_proj"] = _mm_tn(ps, dz, name="d_ple_w_proj", tn=512, tm=1024)
    dfull["ple_w_gate"] = _mm_tn(h4, dpre, name="d_ple_w_gate", tn=512, tm=1024)
    dx3, dx3s, d_ple = _mm_nt_rms_bwd([(dpre, full["ple_w_gate"])], x3, dx4, g_ple, name="ple_rms_bwd", tn=512,
                                      half_scale=0.5)

    names = ("ple_w_proj", "ple_w_gate")
    ex = _scatter_of(names, dfull)
    da2, db2 = _ffn_bwd_mid(dx3s, full["ffn2_w2"], a2, b2, name="ffn2_bwd_mid", exchange=ex)
    sent(names, ex)
    dfull["ffn2_w2"] = _mm_tn(s2, dx3s, name="ffn2_dw2", tn=512, tm=512)
    dfull["ffn2_w1"] = _mm_tn(h3, da2, name="ffn2_dw1", tn=512, tm=1408)
    dfull["ffn2_w3"] = _mm_tn(h3, db2, name="ffn2_dw3", tn=512, tm=1408)
    names = ("ffn2_w2", "ffn2_w1")
    ex = _scatter_of(names, dfull)
    dx2, dx2b, d_f2 = _mm_nt_rms_bwd([(da2, full["ffn2_w1"]), (db2, full["ffn2_w3"])], x2, dx3, g_f2,
                                     name="ffn2_rms_bwd", tn=512, half_scale=1.0, exchange=ex)
    sent(names, ex)

    dfull["w_out"] = _mm_tn(y, dx2b, name="d_w_out", tn=512, tm=1024)
    dy = _mm_nt([(dx2b, full["w_out"])], name="d_y", tn=1024, tk=512)
    dya, dyb, dga, dgb = _merge_bwd(dy, proj, ya, yb, name="merge_bwd")

    dfull["w_branch_b"] = _mm_tn(mixed, dyb, name="d_w_branch_b", tn=512, tm=1024)
    dmixed = _mm_nt([(dyb, full["w_branch_b"])], name="d_mixed", tn=1024, tk=512)
    du, d_pw, d_ps = _pool_bwd(dmixed, pooled, pw, pscale, name="pool_bwd")

    dfull["w_branch_a"] = _mm_tn(on, dya, name="d_w_branch_a", tn=512, tm=1024)
    don = _mm_nt([(dya, full["w_branch_a"])], name="d_on", tn=1024, tk=512)
    dog, do, d_on = _hgrn_post_bwd(don, o, proj, g_on, name="hgrn_post_bwd")
    dqfi, d_lb = _hgrn_bwd(proj, lb2, do, states, name="hgrn_bwd")
    dproj = [dqfi, dog, du, dga, dgb]
    names = ("ffn2_w3", "w_out", "w_branch_b", "w_branch_a")
    ex = _scatter_of(names, dfull)
    dx1, dx1s, d_mix = _mm_nt_rms_bwd([(dproj, full["w_in"])], x1, dx2, g_mix, name="mix_rms_bwd", tn=512,
                                      half_scale=0.5, exchange=ex)
    sent(names, ex)
    dfull["w_in"] = jnp.concatenate(
        [_mm_tn(h2, part, name=f"d_w_in_{j}", tn=512, tm=1536) for j, part in enumerate(dproj)], axis=1)

    names = ("w_in",)
    ex = _scatter_of(names, dfull)
    da1, db1 = _ffn_bwd_mid(dx1s, full["ffn1_w2"], a1, b1, name="ffn1_bwd_mid", exchange=ex)
    sent(names, ex)
    dfull["ffn1_w2"] = _mm_tn(s1, dx1s, name="ffn1_dw2", tn=512, tm=512)
    names = ("ffn1_w2",)
    ex = _scatter_of(names, dfull)
    dfull["ffn1_w1"] = _mm_tn(h1, da1, name="ffn1_dw1", tn=512, tm=1408, exchange=ex)
    sent(names, ex)
    names = ("ffn1_w1",)
    ex = _scatter_of(names, dfull)
    dfull["ffn1_w3"] = _mm_tn(h1, db1, name="ffn1_dw3", tn=512, tm=1408, exchange=ex)
    sent(names, ex)
    names = ("ffn1_w3",)
    ex = _scatter_of(names, dfull)
    grad_x, _, d_f1 = _mm_nt_rms_bwd([(da1, full["ffn1_w1"]), (db1, full["ffn1_w3"])], xs, dx1, g_f1,
                                     name="ffn1_rms_bwd", tn=512, half_scale=1.0, exchange=ex)
    sent(names, ex)

    small_part = _pack_small(dict(
        ffn1_norm=d_f1, mix_norm=d_mix, hgrn_onorm=d_on, ffn2_norm=d_f2, ple_norm=d_ple, ple_post_norm=d_post,
        final_norm=d_fin, hgrn_lb=jnp.concatenate([d_lb, jnp.zeros_like(d_lb)], axis=0), pool_scale=d_ps, pool_w=d_pw))
    small_all = _exchange_now([small_part], name="gather_small_grads", gather=True)[0]

    grads, deltas, new_m, new_v = {}, {}, {}, {}
    for name, _ in BIG_WEIGHTS:
        shape, recv = weights[name].shape, received[name]
        res = _adam_big(recv, weights[name][0], mom1[name][0], mom2[name][0], name=f"adam_{name}")
        grads[name], deltas[name], new_m[name], new_v[name] = [r.reshape(shape) for r in res]
    shapes = {name: weights[name].shape for name in SMALL_ROWS}
    res = _adam_small(small_all, _pack_small(weights), _pack_small(mom1), _pack_small(mom2), name="adam_small")
    for store, packed in zip((grads, deltas, new_m, new_v), res):
        store.update(_unpack_small(packed, shapes))

    loss = lax.psum(jnp.sum(loss_part), ("x", "y", "c"))
    return (loss, grad_x.reshape(x.shape), *[grads[k] for k in WEIGHT_ORDER], *[deltas[k] for k in WEIGHT_ORDER],
            *[new_m[k] for k in WEIGHT_ORDER], *[new_v[k] for k in WEIGHT_ORDER])
```

```python
import jax
import jax.numpy as jnp
from jax import lax
from jax.experimental import pallas as pl
from jax.experimental.pallas import tpu as pltpu

F32 = jnp.float32
BF16 = jnp.bfloat16

N_DEV = 8
D_MODEL = 1024
HEADS = 8
HEAD_DIM = 128
POOL_WINDOWS = (2, 4, 8, 16)
POOL_CH = 128
POOL_WIDTH = 512
POOL_HALO = 16
RMS_EPS = 1e-6
CHUNK = 64
SUB = 32
HGRN_HEADS_PER_STEP = 8
NEG_BIG = -1e30

ADAM_LR = 0.001
ADAM_B1 = 0.9
ADAM_B2 = 0.999
ADAM_EPS = 1e-08
ADAM_WD = 0.01
ADAM_STEP = 10

V7X_VMEM_BYTES = 64 * 1024 * 1024
VMEM_LIMIT = (V7X_VMEM_BYTES * 3) // 4
ROW_TILE_CAP = 8192

COL_Q, COL_F, COL_I, COL_OG, COL_POOL, COL_GA, COL_GB = 0, 1024, 2048, 3072, 4096, 4608, 5632

BIG_WEIGHTS = (
    ("ffn1_w1", 1), ("ffn1_w3", 1), ("ffn1_w2", 0), ("w_in", 1), ("w_branch_a", 0), ("w_branch_b", 1),
    ("w_out", 0), ("ffn2_w1", 1), ("ffn2_w3", 1), ("ffn2_w2", 0), ("ple_w_gate", 0), ("ple_w_proj", 1),
)
SMALL_ROWS = {
    "ffn1_norm": (0, 8), "mix_norm": (8, 8), "hgrn_onorm": (16, 8), "ffn2_norm": (24, 8), "ple_norm": (32, 8),
    "ple_post_norm": (40, 8), "final_norm": (48, 8), "hgrn_lb": (56, 16), "pool_scale": (72, 4), "pool_w": (80, 512),
}
SMALL_TOTAL_ROWS = 592
WEIGHT_ORDER = (
    "ffn1_norm", "ffn1_w1", "ffn1_w3", "ffn1_w2", "mix_norm", "w_in", "hgrn_lb", "hgrn_onorm", "w_branch_a", "pool_w",
    "pool_scale", "w_branch_b", "w_out", "ffn2_norm", "ffn2_w1", "ffn2_w3", "ffn2_w2", "ple_norm", "ple_w_gate",
    "ple_w_proj", "ple_post_norm", "final_norm",
)


def _params(*sem):
    return pltpu.CompilerParams(dimension_semantics=sem if sem else None, vmem_limit_bytes=VMEM_LIMIT)


COL_CHUNK = 256


def _rows(tn, width):
    return pl.BlockSpec((tn, width), lambda i: (i, 0))


def _resident(shape):
    return pl.BlockSpec(shape, lambda i: (0,) * len(shape), pipeline_mode=pl.Buffered(1))


def _dot(a, b):
    return jnp.dot(a, b, preferred_element_type=F32)


def _dot_nt(a, b):
    return lax.dot_general(a, b, (((1,), (1,)), ((), ())), preferred_element_type=F32)


def _dot_tn(a, b):
    return lax.dot_general(a, b, (((0,), (0,)), ((), ())), preferred_element_type=F32)


def _sigmoid(x):
    return 0.5 * jnp.tanh(0.5 * x) + 0.5


def _tile(n, want, mult):
    if mult != 128:
        want = min(want, ROW_TILE_CAP)
    if n <= want:
        return n
    t = (want // mult) * mult
    while t > mult and n % t:
        t -= mult
    assert n % t == 0, (n, want, mult)
    return t


class _Exchange:
    COPIES = N_DEV - 1

    def __init__(self, arrs, gather):
        self.arrs, self.gather, self.n = list(arrs), gather, len(arrs)
        self.out_shape = [jax.ShapeDtypeStruct((N_DEV,) + (a.shape if gather else a.shape[1:]), a.dtype) for a in arrs]
        self.scratch = [pltpu.SemaphoreType.DMA((self.n * self.COPIES,)),
                        pltpu.SemaphoreType.DMA((self.n * self.COPIES,)), pltpu.SemaphoreType.DMA((self.n,))]
        self.received = None

    @staticmethod
    def _place():
        x, y, c = lax.axis_index("x"), lax.axis_index("y"), lax.axis_index("c")
        return x, y, c

    def _copy(self, a, k, src, dst, to, sems):
        s = a * self.COPIES + k
        return pltpu.make_async_remote_copy(src_ref=src, dst_ref=dst, send_sem=sems[0].at[s], recv_sem=sems[1].at[s],
                                            device_id=to, device_id_type=pl.DeviceIdType.MESH)

    def _gather_copies(self, ins, outs, sems):
        x, y, c = self._place()
        chips = [(1 - x, y), (x, 1 - y), (1 - x, 1 - y)]
        slot = lambda px, py, pc: 4 * px + 2 * py + pc
        first, passed, arrivals = [], [], []
        for a in range(self.n):
            mine = outs[a].at[slot(x, y, c)]
            first.append(self._copy(a, 0, ins[a], mine, (x, y, 1 - c), sems))
            arrivals.append(self._copy(a, 0, ins[a], outs[a].at[slot(x, y, 1 - c)], (x, y, 1 - c), sems))
            for j, (px, py) in enumerate(chips):
                first.append(self._copy(a, 1 + j, ins[a], mine, (px, py, c), sems))
                theirs = outs[a].at[slot(px, py, c)]
                passed.append((self._copy(a, 1 + j, ins[a], theirs, (px, py, c), sems),
                               self._copy(a, 4 + j, theirs, theirs, (x, y, 1 - c), sems)))
                arrivals.append(self._copy(a, 4 + j, ins[a], outs[a].at[slot(px, py, 1 - c)], (x, y, 1 - c), sems))
        return first, passed, arrivals

    def _scatter_copies(self, ins, outs, sems):
        x, y, c = self._place()
        me = 4 * x + 2 * y + c
        sends, arrivals = [], []
        for k in range(1, N_DEV):
            px = 1 - x if k & 4 else x
            py = 1 - y if k & 2 else y
            pc = 1 - c if k & 1 else c
            peer = 4 * px + 2 * py + pc
            for a in range(self.n):
                sends.append(self._copy(a, k - 1, ins[a].at[peer], outs[a].at[me], (px, py, pc), sems))
                arrivals.append(self._copy(a, k - 1, ins[a].at[peer], outs[a].at[peer], (px, py, pc), sems))
        return sends, arrivals

    def _local(self, ins, outs, sems):
        x, y, c = self._place()
        me = 4 * x + 2 * y + c
        return [pltpu.make_async_copy(ins[a] if self.gather else ins[a].at[me], outs[a].at[me], sems[2].at[a])
                for a in range(self.n)]

    def start(self, ins, outs, sems):
        for cp in self._local(ins, outs, sems):
            cp.start()
        sends = self._gather_copies(ins, outs, sems)[0] if self.gather else self._scatter_copies(ins, outs, sems)[0]
        for cp in sends:
            cp.start()

    def finish(self, ins, outs, sems):
        if self.gather:
            first, passed, arrivals = self._gather_copies(ins, outs, sems)
            for landed, onward in passed:
                landed.wait_recv()
                onward.start()
            sends = first + [onward for _, onward in passed]
        else:
            sends, arrivals = self._scatter_copies(ins, outs, sems)
        for cp in arrivals:
            cp.wait_recv()
        for cp in sends:
            cp.wait_send()
        for cp in self._local(ins, outs, sems):
            cp.wait()


def _call(body, *, name, grid, in_specs, out_specs, out_shape, args, semantics, scratch=(), exchange=None):
    if exchange is None:
        return pl.pallas_call(
            body, name=name, grid=grid, in_specs=in_specs, out_specs=out_specs, out_shape=out_shape,
            scratch_shapes=list(scratch), compiler_params=_params(*semantics))(*args)
    ex = exchange
    n_in, n_out, n_s = len(in_specs), len(out_specs), len(scratch)

    def wrapped(*refs):
        ins, ex_in = refs[:n_in], refs[n_in:n_in + ex.n]
        o0 = n_in + ex.n
        outs, ex_out = refs[o0:o0 + n_out], refs[o0 + n_out:o0 + n_out + ex.n]
        s0 = o0 + n_out + ex.n
        scr, sems = refs[s0:s0 + n_s], refs[s0 + n_s:]
        ids = [pl.program_id(ax) for ax in range(len(grid))]
        first = ids[0] == 0
        last = ids[0] == grid[0] - 1
        for ax in range(1, len(grid)):
            first = jnp.logical_and(first, ids[ax] == 0)
            last = jnp.logical_and(last, ids[ax] == grid[ax] - 1)

        @pl.when(first)
        def _():
            ex.start(ex_in, ex_out, sems)

        body(*ins, *outs, *scr)

        @pl.when(last)
        def _():
            ex.finish(ex_in, ex_out, sems)

    hbm = pl.BlockSpec(memory_space=pltpu.HBM)
    res = pl.pallas_call(
        wrapped, name=name, grid=grid, in_specs=list(in_specs) + [hbm] * ex.n,
        out_specs=list(out_specs) + [hbm] * ex.n, out_shape=list(out_shape) + ex.out_shape,
        scratch_shapes=list(scratch) + ex.scratch, compiler_params=_params(*(["arbitrary"] * len(grid))),
    )(*args, *ex.arrs)
    ex.received = res[n_out:]
    return res[:n_out]


def _exchange_now(arrs, *, name, gather):
    ex = _Exchange(arrs, gather)
    n = ex.n

    def body(*refs):
        ex.start(refs[:n], refs[n:2 * n], refs[2 * n:])
        ex.finish(refs[:n], refs[n:2 * n], refs[2 * n:])

    hbm = pl.BlockSpec(memory_space=pltpu.HBM)
    return pl.pallas_call(body, name=name, out_shape=ex.out_shape, in_specs=[hbm] * n, out_specs=[hbm] * n,
                          scratch_shapes=ex.scratch)(*arrs)


def _mm_nn(a, b, *, name, tn, tm, out_dtype=F32, res=None, scale=1.0, exchange=None):
    n, k = a.shape
    m = b.shape[1]
    tn, tm = _tile(n, tn, 16), _tile(m, tm, 128)

    def body(*refs):
        a_ref, b_ref = refs[0], refs[1]
        o_ref = refs[-1]
        acc = _dot(a_ref[...], b_ref[...])
        if scale != 1.0:
            acc = acc * scale
        if res is not None:
            acc = acc + refs[2][...]
        o_ref[...] = acc.astype(o_ref.dtype)

    in_specs = [pl.BlockSpec((tn, k), lambda i, j: (i, 0)), pl.BlockSpec((k, tm), lambda i, j: (0, j))]
    args = [a, b]
    if res is not None:
        in_specs.append(pl.BlockSpec((tn, tm), lambda i, j: (i, j)))
        args.append(res)
    return _call(body, name=name, grid=(n // tn, m // tm), in_specs=in_specs,
                 out_specs=[pl.BlockSpec((tn, tm), lambda i, j: (i, j))],
                 out_shape=[jax.ShapeDtypeStruct((n, m), out_dtype)], args=args, semantics=("parallel", "parallel"),
                 exchange=exchange)[0]


def _mm_nt(pairs, *, name, tn, tk, out_dtype=F32, exchange=None):
    n = pairs[0][0].shape[0]
    kk = pairs[0][1].shape[0]
    tn, tk = _tile(n, tn, 16), _tile(kk, tk, 128)
    npair = len(pairs)

    def body(*refs):
        o_ref = refs[-1]
        acc = _dot_nt(refs[0][...], refs[1][...])
        for q in range(1, npair):
            acc = acc + _dot_nt(refs[2 * q][...], refs[2 * q + 1][...])
        o_ref[...] = acc.astype(o_ref.dtype)

    in_specs, args = [], []
    for a, b in pairs:
        m = a.shape[1]
        in_specs += [pl.BlockSpec((tn, m), lambda i, j: (i, 0)), pl.BlockSpec((tk, m), lambda i, j: (j, 0))]
        args += [a, b]
    return _call(body, name=name, grid=(n // tn, kk // tk), in_specs=in_specs,
                 out_specs=[pl.BlockSpec((tn, tk), lambda i, j: (i, j))],
                 out_shape=[jax.ShapeDtypeStruct((n, kk), out_dtype)], args=args, semantics=("parallel", "parallel"),
                 exchange=exchange)[0]


def _mm_tn(a, b, *, name, tn, tm, exchange=None):
    n, k = a.shape
    m = b.shape[1]
    tn, tm = _tile(n, tn, 16), _tile(m, tm, 128)
    steps = n // tn

    def body(a_ref, b_ref, o_ref, acc):
        i = pl.program_id(1)

        @pl.when(i == 0)
        def _():
            acc[...] = jnp.zeros_like(acc)

        acc[...] += _dot_tn(a_ref[...], b_ref[...])

        @pl.when(i == steps - 1)
        def _():
            o_ref[...] = acc[...].astype(o_ref.dtype)

    return _call(body, name=name, grid=(m // tm, steps),
                 in_specs=[pl.BlockSpec((tn, k), lambda j, i: (i, 0)), pl.BlockSpec((tn, tm), lambda j, i: (i, j))],
                 out_specs=[pl.BlockSpec((k, tm), lambda j, i: (0, j))],
                 out_shape=[jax.ShapeDtypeStruct((k, m), BF16)], args=[a, b], semantics=("parallel", "arbitrary"),
                 scratch=[pltpu.VMEM((k, tm), F32)], exchange=exchange)[0]


def _ffn_up(h, w1, w3, *, name, exchange=None):
    n, k = h.shape
    m = w1.shape[0]
    tn = _tile(n, 512, 16)

    def body(h_ref, w1_ref, w3_ref, dsda_ref, dsdb_ref, s_ref):
        for c0 in range(0, m, COL_CHUNK):
            cols = slice(c0, c0 + COL_CHUNK)
            a = _dot_nt(h_ref[...], w1_ref[cols, :])
            b = _dot_nt(h_ref[...], w3_ref[cols, :])
            sg = _sigmoid(a)
            silu = a * sg
            dsda_ref[:, cols] = (b * (sg + silu * (1.0 - sg))).astype(dsda_ref.dtype)
            dsdb_ref[:, cols] = silu.astype(dsdb_ref.dtype)
            s_ref[:, cols] = (silu * b).astype(s_ref.dtype)

    ospec = _rows(tn, m)
    return _call(body, name=name, grid=(n // tn,),
                 in_specs=[_rows(tn, k), _resident(w1.shape), _resident(w3.shape)], out_specs=[ospec, ospec, ospec],
                 out_shape=[jax.ShapeDtypeStruct((n, m), BF16)] * 3,
                 args=[h, w1, w3], semantics=("parallel",), exchange=exchange)


def _mm_nn_wide(a, b, *, name, out_dtype, exchange=None):
    n, k = a.shape
    m = b.shape[0]
    tn = _tile(n, 512, 16)
    chunk = 2 * COL_CHUNK

    def body(a_ref, b_ref, o_ref):
        for c0 in range(0, m, chunk):
            cols = slice(c0, c0 + chunk)
            o_ref[:, cols] = _dot_nt(a_ref[...], b_ref[cols, :]).astype(o_ref.dtype)

    return _call(body, name=name, grid=(n // tn,), in_specs=[_rows(tn, k), _resident(b.shape)],
                 out_specs=[_rows(tn, m)], out_shape=[jax.ShapeDtypeStruct((n, m), out_dtype)], args=[a, b],
                 semantics=("parallel",), exchange=exchange)[0]


def _mm_nn_res_rms(a, b, res, g, *, name, scale, exchange=None):
    n, k = a.shape
    d = b.shape[1]
    tn = _tile(n, 512, 16)

    def body(a_ref, b_ref, r_ref, g_ref, x_ref, h_ref):
        for c0 in range(0, d, COL_CHUNK):
            cols = slice(c0, c0 + COL_CHUNK)
            x_ref[:, cols] = r_ref[:, cols] + scale * _dot(a_ref[...], b_ref[:, cols])
        x = x_ref[...]
        r = lax.rsqrt(_rowmean(x * x) + RMS_EPS)
        h_ref[...] = (x * r * g_ref[...]).astype(h_ref.dtype)

    row = _rows(tn, d)
    return _call(body, name=name, grid=(n // tn,),
                 in_specs=[_rows(tn, k), _resident(b.shape), row, pl.BlockSpec((1, d), lambda i: (0, 0))],
                 out_specs=[row, row],
                 out_shape=[jax.ShapeDtypeStruct((n, d), F32), jax.ShapeDtypeStruct((n, d), BF16)],
                 args=[a, b, res, g], semantics=("parallel",), exchange=exchange)


def _ffn_bwd_mid(dxs, w2, dsda, dsdb, *, name, exchange=None):
    n, d = dxs.shape
    m = w2.shape[0]
    tn = _tile(n, 512, 16)

    def body(dx_ref, w2_ref, dsda_ref, dsdb_ref, da_ref, db_ref):
        for c0 in range(0, m, COL_CHUNK):
            cols = slice(c0, c0 + COL_CHUNK)
            ds = _dot_nt(dx_ref[...], w2_ref[cols, :])
            da_ref[:, cols] = (ds * dsda_ref[:, cols].astype(F32)).astype(da_ref.dtype)
            db_ref[:, cols] = (ds * dsdb_ref[:, cols].astype(F32)).astype(db_ref.dtype)

    tile = _rows(tn, m)
    return _call(body, name=name, grid=(n // tn,),
                 in_specs=[_rows(tn, d), _resident(w2.shape), tile, tile], out_specs=[tile, tile],
                 out_shape=[jax.ShapeDtypeStruct((n, m), BF16), jax.ShapeDtypeStruct((n, m), BF16)],
                 args=[dxs, w2, dsda, dsdb], semantics=("parallel",), exchange=exchange)


def _rowwise(fn, *, name, n, tn, ncol, rows, vecs, outs, accs=()):
    tn = _tile(n, tn, 16)
    nr, nv, no = len(rows), len(vecs), len(outs)

    def body(*refs):
        first = pl.program_id(1) == 0
        vals = [r[...].astype(F32) for r in refs[:nr + nv]]
        res = fn(*vals)
        for ref, val in zip(refs[nr + nv:nr + nv + no], res[:no]):
            ref[...] = val.astype(ref.dtype)
        for ref, val in zip(refs[nr + nv + no:], res[no:]):
            _accumulate(ref, val, first)

    in_specs = [pl.BlockSpec((tn, w), lambda j, i, c0=c0: (i, c0 + j)) for _, w, c0 in rows]
    in_specs += [pl.BlockSpec((1, w), lambda j, i, c0=c0: (0, c0 + j)) for _, w, c0 in vecs]
    out_specs = [pl.BlockSpec((tn, w), lambda j, i: (i, j)) for _, w, _ in outs]
    out_specs += [pl.BlockSpec((1, w), lambda j, i: (0, j)) for _, w in accs]
    out_shape = [jax.ShapeDtypeStruct((n, tw), dt) for tw, _, dt in outs]
    out_shape += [jax.ShapeDtypeStruct((1, tw), F32) for tw, _ in accs]
    return pl.pallas_call(
        body, name=name, grid=(ncol, n // tn), in_specs=in_specs, out_specs=out_specs, out_shape=out_shape,
        compiler_params=_params("parallel", "arbitrary"),
    )(*[r[0] for r in rows], *[v[0] for v in vecs])


def _accumulate(ref, val, first):
    @pl.when(first)
    def _():
        ref[...] = jnp.zeros_like(ref)

    ref[...] += val


def _colsum(x):
    return jnp.sum(x, axis=0, keepdims=True)


def _rowmean(x):
    return jnp.mean(x, axis=-1, keepdims=True)


def _rms_fwd(x, g, *, name):
    def fn(x_, g_):
        r = lax.rsqrt(_rowmean(x_ * x_) + RMS_EPS)
        return (x_ * r * g_,)

    n, d = x.shape
    return _rowwise(fn, name=name, n=n, tn=512, ncol=1, rows=[(x, d, 0)], vecs=[(g, d, 0)], outs=[(d, d, BF16)])[0]


def _mm_nt_rms_bwd(pairs, x, extra, g, *, name, tn, half_scale, exchange=None):
    n, d = x.shape
    tn = _tile(n, tn, 16)
    pairs = [(list(a) if isinstance(a, (list, tuple)) else [a], b, t) for a, b, t in pairs]
    nref = sum(len(a) + 1 for a, _, _ in pairs)

    def body(*refs):
        x_ref, e_ref, g_ref, dx_ref, dxs_ref, dg_ref = refs[nref:]
        dh, at = None, 0
        for parts, _, transposed in pairs:
            b_ref = refs[at + len(parts)]
            col = 0
            for j, part in enumerate(parts):
                w = part.shape[1]
                if transposed:
                    term = _dot(refs[at + j][...], b_ref[col:col + w, :])
                else:
                    term = _dot_nt(refs[at + j][...], b_ref[:, col:col + w])
                dh = term if dh is None else dh + term
                col += w
            at += len(parts) + 1
        x_ = x_ref[...]
        r = lax.rsqrt(_rowmean(x_ * x_) + RMS_EPS)
        xh = x_ * r
        dxh = dh * g_ref[...]
        dx = e_ref[...] + r * (dxh - xh * _rowmean(dxh * xh))
        dx_ref[...] = dx
        dxs_ref[...] = (dx * half_scale).astype(dxs_ref.dtype)
        _accumulate(dg_ref, _colsum(dh * xh), pl.program_id(0) == 0)

    in_specs, args = [], []
    for parts, b, transposed in pairs:
        assert sum(part.shape[1] for part in parts) == b.shape[0 if transposed else 1]
        in_specs += [pl.BlockSpec((tn, part.shape[1]), lambda i: (i, 0)) for part in parts]
        in_specs.append(pl.BlockSpec(b.shape, lambda i: (0, 0), pipeline_mode=pl.Buffered(1)))
        args += parts + [b]
    row = pl.BlockSpec((tn, d), lambda i: (i, 0))
    vec = pl.BlockSpec((1, d), lambda i: (0, 0))
    return _call(body, name=name, grid=(n // tn,), in_specs=in_specs + [row, row, vec], out_specs=[row, row, vec],
                 out_shape=[jax.ShapeDtypeStruct((n, d), F32), jax.ShapeDtypeStruct((n, d), BF16),
                            jax.ShapeDtypeStruct((1, d), F32)],
                 args=args + [x, extra, g], semantics=("arbitrary",), exchange=exchange)


def _ple_final(x3, gpre, z, tgt, gpp, gf, *, name):
    def fn(x3_, gpre_, z_, tgt_, gpp_, gf_):
        gate = _sigmoid(gpre_)
        rz = lax.rsqrt(_rowmean(z_ * z_) + RMS_EPS)
        zh = z_ * rz
        e = zh * gpp_
        x4 = x3_ + gate * e
        r4 = lax.rsqrt(_rowmean(x4 * x4) + RMS_EPS)
        x4h = x4 * r4
        diff = x4h * gf_ - tgt_
        dout = diff * (1.0 / D_MODEL)
        dxh4 = dout * gf_
        dx4 = r4 * (dxh4 - x4h * _rowmean(dxh4 * x4h))
        dpre = dx4 * e * gate * (1.0 - gate)
        de = dx4 * gate
        dzh = de * gpp_
        dz = rz * (dzh - zh * _rowmean(dzh * zh))
        return dx4, dpre, dz, _colsum(diff * diff) * (0.5 / D_MODEL), _colsum(dout * x4h), _colsum(de * zh)

    n, d = x3.shape
    return _rowwise(fn, name=name, n=n, tn=256, ncol=1, rows=[(x3, d, 0), (gpre, d, 0), (z, d, 0), (tgt, d, 0)],
                    vecs=[(gpp, d, 0), (gf, d, 0)], outs=[(d, d, F32), (d, d, BF16), (d, d, BF16)],
                    accs=[(d, d), (d, d), (d, d)])


def _merge_fwd(proj, ya, yb, *, name):
    def fn(ga, gb, ya_, yb_):
        return (_sigmoid(ga) * ya_ + _sigmoid(gb) * yb_,)

    n = proj.shape[0]
    w = 512
    return _rowwise(fn, name=name, n=n, tn=512, ncol=D_MODEL // w,
                    rows=[(proj, w, COL_GA // w), (proj, w, COL_GB // w), (ya, w, 0), (yb, w, 0)], vecs=[],
                    outs=[(D_MODEL, w, BF16)])[0]


def _merge_bwd(dy, proj, ya, yb, *, name):
    def fn(dy_, ga, gb, ya_, yb_):
        sa, sb = _sigmoid(ga), _sigmoid(gb)
        return dy_ * sa, dy_ * sb, dy_ * ya_ * sa * (1.0 - sa), dy_ * yb_ * sb * (1.0 - sb)

    n = proj.shape[0]
    w = 512
    return _rowwise(fn, name=name, n=n, tn=512, ncol=D_MODEL // w,
                    rows=[(dy, w, 0), (proj, w, COL_GA // w), (proj, w, COL_GB // w), (ya, w, 0), (yb, w, 0)],
                    vecs=[], outs=[(D_MODEL, w, BF16)] * 4)


def _head_mean(x):
    return jnp.concatenate(
        [jnp.broadcast_to(jnp.mean(x[:, h * HEAD_DIM:(h + 1) * HEAD_DIM], axis=-1, keepdims=True),
                          (x.shape[0], HEAD_DIM)) for h in range(HEADS)], axis=1)


def _hgrn_post_fwd(o, proj, onorm, *, name):
    def fn(o_, og, gam):
        r = lax.rsqrt(_head_mean(o_ * o_) + RMS_EPS)
        return (o_ * r * gam * (og * _sigmoid(og)),)

    n = o.shape[0]
    w = D_MODEL
    return _rowwise(fn, name=name, n=n, tn=256, ncol=1, rows=[(o, w, 0), (proj, w, COL_OG // w)],
                    vecs=[(onorm, w, 0)], outs=[(D_MODEL, w, BF16)])[0]


def _hgrn_post_bwd(don, o, proj, onorm, *, name):
    def fn(don_, o_, og, gam):
        r = lax.rsqrt(_head_mean(o_ * o_) + RMS_EPS)
        oh = o_ * r
        sg = _sigmoid(og)
        dog = don_ * oh * gam * (sg * (1.0 + og * (1.0 - sg)))
        dn = don_ * (og * sg)
        doh = dn * gam
        do = r * (doh - oh * _head_mean(doh * oh))
        return dog, do, _colsum(dn * oh)

    n = o.shape[0]
    w = D_MODEL
    return _rowwise(fn, name=name, n=n, tn=256, ncol=1, rows=[(don, w, 0), (o, w, 0), (proj, w, COL_OG // w)],
                    vecs=[(onorm, w, 0)], outs=[(D_MODEL, w, BF16), (D_MODEL, w, F32)], accs=[(D_MODEL, w)])


def _split3(x):
    hi = x.astype(BF16)
    r1 = x - hi.astype(F32)
    mid = r1.astype(BF16)
    lo = (r1 - mid.astype(F32)).astype(BF16)
    return hi, mid, lo


def _tri_sum(tri, x):
    hi, mid, lo = _split3(x)
    return _dot(tri, hi) + _dot(tri, mid) + _dot(tri, lo)


def _lower_bound(lb_ref):
    return 1.0 / (1.0 + jnp.exp(lb_ref[1:2, :] - lb_ref[0:1, :]))


def _hgrn_specs(n, t, reverse):
    nt = n // t
    width = HGRN_HEADS_PER_STEP * HEAD_DIM

    def tok(i):
        return nt - 1 - i if reverse else i

    def sec(col):
        c0 = col // width
        return pl.BlockSpec((t, width), lambda h, i: (tok(i), c0 + h))

    head_tile = pl.BlockSpec((t, width), lambda h, i: (tok(i), h))
    state = pl.BlockSpec((HGRN_HEADS_PER_STEP, t // CHUNK, HEAD_DIM, HEAD_DIM), lambda h, i: (h, tok(i), 0, 0))
    lb = pl.BlockSpec((2, width), lambda h, i: (0, h))
    return sec, head_tile, state, lb


def _hgrn_fwd(proj, hgrn_lb, *, name):
    n = proj.shape[0]
    t = _tile(n, 512, CHUNK)
    nc = t // CHUNK
    hps = HGRN_HEADS_PER_STEP
    width = hps * HEAD_DIM
    lanes = [slice(h * HEAD_DIM, (h + 1) * HEAD_DIM) for h in range(hps)]
    sec, head_tile, state, lbspec = _hgrn_specs(n, t, False)

    def body(q_ref, f_ref, i_ref, lb_ref, o_ref, st_ref, s_acc, g_s, a_s):
        @pl.when(pl.program_id(1) == 0)
        def _():
            s_acc[...] = jnp.zeros_like(s_acc)

        lb = _lower_bound(lb_ref)
        row = lax.broadcasted_iota(jnp.int32, (CHUNK, CHUNK), 0)
        col = lax.broadcasted_iota(jnp.int32, (CHUNK, CHUNK), 1)
        tril = row >= col
        trilb = jnp.where(tril, 1.0, 0.0).astype(BF16)
        rowk = lax.broadcasted_iota(jnp.int32, (CHUNK, width), 0)

        def chunk(c, carry):
            rows = pl.ds(pl.multiple_of(c * CHUNK, CHUNK), CHUNK)
            qr, fr, v = [r[rows, :].astype(F32) for r in (q_ref, f_ref, i_ref)]
            q = qr * _sigmoid(qr)
            f = lb + (1.0 - lb) * _sigmoid(fr)
            k = 1.0 - f
            g = _tri_sum(trilb, jnp.log(f))
            g_s[...] = g
            st0 = [s_acc[h] for h in range(hps)]
            for h in range(hps):
                st_ref[h, c] = st0[h]
            vb = v.astype(BF16)
            for blk in range(CHUNK // SUB):
                lo, hi = blk * SUB, (blk + 1) * SUB
                gref = g_s[lo - 1:lo, :] if blk else jnp.zeros((1, width), F32)
                qi = (q[lo:hi] * jnp.exp(g[lo:hi] - gref)).astype(BF16)
                ki = (k * jnp.exp(jnp.where(rowk < hi, gref - g, NEG_BIG))).astype(BF16)
                for h, ln in enumerate(lanes):
                    a_s[h, lo:hi, :] = _dot_nt(qi[:, ln], ki[:, ln])
            qeb = (q * jnp.exp(g)).astype(BF16)
            o_ref[rows, :] = jnp.concatenate(
                [_dot(jnp.where(tril, a_s[h], 0.0).astype(BF16), vb[:, ln]) + _dot_nt(qeb[:, ln], st0[h].astype(BF16))
                 for h, ln in enumerate(lanes)], axis=1)
            glast = g_s[CHUNK - 1:CHUNK, :]
            kdb = (k * jnp.exp(glast - g)).astype(BF16)
            dec = jnp.exp(glast)
            for h, ln in enumerate(lanes):
                s_acc[h] = st0[h] * dec[:, ln] + _dot_tn(vb[:, ln], kdb[:, ln])
            return carry

        lax.fori_loop(0, nc, chunk, 0)

    return pl.pallas_call(
        body, name=name, grid=(HEADS // hps, n // t),
        in_specs=[sec(COL_Q), sec(COL_F), sec(COL_I), lbspec], out_specs=[head_tile, state],
        out_shape=[jax.ShapeDtypeStruct((n, D_MODEL), F32),
                   jax.ShapeDtypeStruct((HEADS, n // CHUNK, HEAD_DIM, HEAD_DIM), F32)],
        scratch_shapes=[pltpu.VMEM((hps, HEAD_DIM, HEAD_DIM), F32), pltpu.VMEM((CHUNK, width), F32),
                        pltpu.VMEM((hps, CHUNK, CHUNK), F32)],
        compiler_params=_params("parallel", "arbitrary"),
    )(proj, proj, proj, hgrn_lb)


def _hgrn_bwd(proj, hgrn_lb, do, states, *, name):
    n = proj.shape[0]
    t = _tile(n, 512, CHUNK)
    nc = t // CHUNK
    hps = HGRN_HEADS_PER_STEP
    width = hps * HEAD_DIM
    lanes = [slice(h * HEAD_DIM, (h + 1) * HEAD_DIM) for h in range(hps)]
    sec, head_tile, state, lbspec = _hgrn_specs(n, t, True)

    def body(q_ref, f_ref, i_ref, lb_ref, do_ref, st_ref, dqfi_ref, dlb_ref, d_acc, g_s, a_s, dq_s,
             dg_s):
        first = pl.program_id(1) == 0

        @pl.when(first)
        def _():
            d_acc[...] = jnp.zeros_like(d_acc)

        lb = _lower_bound(lb_ref)
        row = lax.broadcasted_iota(jnp.int32, (CHUNK, CHUNK), 0)
        col = lax.broadcasted_iota(jnp.int32, (CHUNK, CHUNK), 1)
        tril = row >= col
        trilb = jnp.where(tril, 1.0, 0.0).astype(BF16)
        triub = jnp.where(row <= col, 1.0, 0.0).astype(BF16)
        rowk = lax.broadcasted_iota(jnp.int32, (CHUNK, width), 0)

        def per_head(fn):
            return jnp.concatenate([fn(h, ln) for h, ln in enumerate(lanes)], axis=1)

        def chunk(j, dlb):
            c = nc - 1 - j
            rows = pl.ds(pl.multiple_of(c * CHUNK, CHUNK), CHUNK)
            qr, fr, v, dout = [r[rows, :].astype(F32) for r in (q_ref, f_ref, i_ref, do_ref)]
            sq = _sigmoid(qr)
            q = qr * sq
            sf = _sigmoid(fr)
            f = lb + (1.0 - lb) * sf
            k = 1.0 - f
            g = _tri_sum(trilb, jnp.log(f))
            g_s[...] = g
            st0 = [st_ref[h, c] for h in range(hps)]
            dt = [d_acc[h] for h in range(hps)]
            vb, dob = v.astype(BF16), dout.astype(BF16)
            dtb = [x.astype(BF16) for x in dt]
            st0b = [x.astype(BF16) for x in st0]
            glast = g_s[CHUNK - 1:CHUNK, :]
            eg = jnp.exp(g)
            kdec = jnp.exp(glast - g)
            qeb, kdb = (q * eg).astype(BF16), (k * kdec).astype(BF16)
            aps = [jnp.where(row > col, _dot_nt(dob[:, ln], vb[:, ln]), 0.0) for ln in lanes]
            dov = dout * v
            adiag = per_head(lambda h, ln: jnp.broadcast_to(
                jnp.sum(dov[:, ln], axis=-1, keepdims=True), (CHUNK, HEAD_DIM)))
            dq_inter = per_head(lambda h, ln: _dot(dob[:, ln], st0b[h]))
            dk_inter = per_head(lambda h, ln: _dot(vb[:, ln], dtb[h]))
            dk_st = kdec * dk_inter
            dg = qeb.astype(F32) * dq_inter
            dg_minus = kdb.astype(F32) * dk_inter
            dg = dg - dg_minus
            for blk in range(CHUNK // SUB):
                lo, hi = blk * SUB, (blk + 1) * SUB
                gref = g_s[lo - 1:lo, :] if blk else jnp.zeros((1, width), F32)
                qscale = jnp.exp(g[lo:hi] - gref)
                kscale = jnp.exp(jnp.where(rowk < hi, gref - g, NEG_BIG))
                qi = (q[lo:hi] * qscale).astype(BF16)
                ki = (k * kscale).astype(BF16)
                for h, ln in enumerate(lanes):
                    a_s[h, lo:hi, :] = _dot_nt(qi[:, ln], ki[:, ln])
                apb = [x[lo:hi].astype(BF16) for x in aps]
                from_k = per_head(lambda h, ln: _dot(apb[h], ki[:, ln]))
                from_q = per_head(lambda h, ln: _dot_tn(apb[h], qi[:, ln]))
                dq_s[lo:hi, :] = qscale * from_k
                dg_s[lo:hi, :] = qi.astype(F32) * from_k
                dk_st = dk_st + kscale * from_q
                dg = dg - ki.astype(F32) * from_q
            dg = dg + dg_s[...]
            dv = per_head(lambda h, ln: _dot_tn(jnp.where(tril, a_s[h], 0.0).astype(BF16), dob[:, ln])
                          + _dot_nt(kdb[:, ln], dtb[h]))
            dq_st = dq_s[...] + eg * dq_inter
            dq = dq_st + adiag * k
            dk = dk_st + adiag * q
            dec = jnp.exp(glast)
            dt_dec = [dt[h] * dec[:, ln] for h, ln in enumerate(lanes)]
            for h, ln in enumerate(lanes):
                d_acc[h] = dt_dec[h] + _dot_tn(dob[:, ln], qeb[:, ln])
            later = per_head(lambda h, ln: _colsum(dt_dec[h] * st0[h])) + _colsum(dg_minus)
            dlf = later + _tri_sum(triub, dg)
            df = dlf / f - dk
            dqfi_ref[rows, 0:width] = (dq * (sq * (1.0 + qr * (1.0 - sq)))).astype(dqfi_ref.dtype)
            dqfi_ref[rows, width:2 * width] = (df * (1.0 - lb) * sf * (1.0 - sf)).astype(dqfi_ref.dtype)
            dqfi_ref[rows, 2 * width:3 * width] = dv.astype(dqfi_ref.dtype)
            return dlb + _colsum(df * (1.0 - sf))

        dlb = lax.fori_loop(0, nc, chunk, jnp.zeros((1, width), F32))
        _accumulate(dlb_ref, dlb, first)

    assert hps == HEADS
    nt = n // t
    return pl.pallas_call(
        body, name=name, grid=(1, nt),
        in_specs=[sec(COL_Q), sec(COL_F), sec(COL_I), lbspec, head_tile, state],
        out_specs=[pl.BlockSpec((t, 3 * width), lambda h, i: (nt - 1 - i, 0)),
                   pl.BlockSpec((1, width), lambda h, i: (0, h))],
        out_shape=[jax.ShapeDtypeStruct((n, 3 * D_MODEL), BF16), jax.ShapeDtypeStruct((1, D_MODEL), F32)],
        scratch_shapes=[pltpu.VMEM((hps, HEAD_DIM, HEAD_DIM), F32), pltpu.VMEM((CHUNK, width), F32),
                        pltpu.VMEM((hps, CHUNK, CHUNK), F32), pltpu.VMEM((CHUNK, width), F32),
                        pltpu.VMEM((CHUNK, width), F32)],
        compiler_params=_params("parallel", "arbitrary"),
    )(proj, proj, proj, hgrn_lb, do, states)


def _pool_fwd(proj, pool_w, pool_scale, *, name):
    n = proj.shape[0]
    t = _tile(n, 512, POOL_HALO)
    per = t // POOL_HALO
    c0 = COL_POOL // POOL_WIDTH

    def body(u_ref, halo_ref, pw_ref, ps_ref, pooled_ref, mixed_ref, ext):
        i = pl.program_id(0)
        u = u_ref[...].astype(F32)
        ext[POOL_HALO:POOL_HALO + t, :] = u
        ext[0:POOL_HALO, :] = jnp.where(i > 0, halo_ref[...].astype(F32), 0.0)
        pos = i * t + lax.broadcasted_iota(jnp.int32, (t, POOL_CH), 0) + 1
        for grp, win in enumerate(POOL_WINDOWS):
            cols = slice(grp * POOL_CH, (grp + 1) * POOL_CH)
            acc = u[:, cols]
            for j in range(1, win):
                acc = acc + ext[POOL_HALO - j:POOL_HALO - j + t, cols]
            pooled = (acc / jnp.minimum(pos, win).astype(F32) - u[:, cols]).astype(BF16)
            pooled_ref[:, cols] = pooled
            mixed_ref[:, cols] = (_dot(pooled, pw_ref[grp].astype(BF16)) * ps_ref[:, cols]).astype(BF16)

    tile = pl.BlockSpec((t, POOL_WIDTH), lambda i: (i, 0))
    return pl.pallas_call(
        body, name=name, grid=(n // t,),
        in_specs=[pl.BlockSpec((t, POOL_WIDTH), lambda i: (i, c0)),
                  pl.BlockSpec((POOL_HALO, POOL_WIDTH), lambda i: (jnp.maximum(i * per - 1, 0), c0)),
                  pl.BlockSpec((len(POOL_WINDOWS), POOL_CH, POOL_CH), lambda i: (0, 0, 0)),
                  pl.BlockSpec((1, POOL_WIDTH), lambda i: (0, 0))],
        out_specs=[tile, tile],
        out_shape=[jax.ShapeDtypeStruct((n, POOL_WIDTH), BF16), jax.ShapeDtypeStruct((n, POOL_WIDTH), BF16)],
        scratch_shapes=[pltpu.VMEM((t + POOL_HALO, POOL_WIDTH), F32)],
        compiler_params=_params("parallel"),
    )(proj, proj, pool_w, pool_scale)


def _pool_bwd(dmixed, pooled, pool_w, pool_scale, *, name):
    n = dmixed.shape[0]
    t = _tile(n, 512, POOL_HALO)
    per = t // POOL_HALO
    nb = n // t

    def body(dm_ref, dmh_ref, p_ref, pw_ref, ps_ref, du_ref, dpw_ref, dps_ref, ext):
        i = pl.program_id(0)

        @pl.when(i == 0)
        def _():
            dpw_ref[...] = jnp.zeros_like(dpw_ref)
            dps_ref[...] = jnp.zeros_like(dps_ref)

        dm, dmh = dm_ref[...], dmh_ref[...]
        pos = i * t + lax.broadcasted_iota(jnp.int32, (t, POOL_CH), 0) + 1
        for grp, win in enumerate(POOL_WINDOWS):
            cols = slice(grp * POOL_CH, (grp + 1) * POOL_CH)
            pwb = pw_ref[grp].astype(BF16)
            pb = p_ref[:, cols]
            scale = ps_ref[:, cols]
            dps_ref[:, cols] += _colsum(dm[:, cols] * _dot(pb, pwb))
            dpm = (dm[:, cols] * scale).astype(BF16)
            dpw_ref[grp] += _dot_tn(pb, dpm)
            dpool = _dot_nt(dpm, pwb)
            dpool_next = _dot_nt((dmh[:, cols] * scale).astype(BF16), pwb)
            ext[0:t, cols] = dpool / jnp.minimum(pos, win).astype(F32)
            ext[t:t + POOL_HALO, cols] = jnp.where(i < nb - 1, dpool_next * (1.0 / win), 0.0)
            acc = -dpool
            for j in range(win):
                acc = acc + ext[j:j + t, cols]
            du_ref[:, cols] = acc.astype(du_ref.dtype)

    tile = pl.BlockSpec((t, POOL_WIDTH), lambda i: (i, 0))
    return pl.pallas_call(
        body, name=name, grid=(nb,),
        in_specs=[tile, pl.BlockSpec((POOL_HALO, POOL_WIDTH), lambda i: (jnp.minimum((i + 1) * per, nb * per - 1), 0)),
                  tile, pl.BlockSpec((len(POOL_WINDOWS), POOL_CH, POOL_CH), lambda i: (0, 0, 0)),
                  pl.BlockSpec((1, POOL_WIDTH), lambda i: (0, 0))],
        out_specs=[tile, pl.BlockSpec((len(POOL_WINDOWS), POOL_CH, POOL_CH), lambda i: (0, 0, 0)),
                   pl.BlockSpec((1, POOL_WIDTH), lambda i: (0, 0))],
        out_shape=[jax.ShapeDtypeStruct((n, POOL_WIDTH), BF16),
                   jax.ShapeDtypeStruct((len(POOL_WINDOWS), POOL_CH, POOL_CH), F32),
                   jax.ShapeDtypeStruct((1, POOL_WIDTH), F32)],
        scratch_shapes=[pltpu.VMEM((t + POOL_HALO, POOL_WIDTH), F32)],
        compiler_params=_params("arbitrary"),
    )(dmixed, dmixed, pooled, pool_w, pool_scale)


def _adamw(w, g, m, v):
    m2 = ADAM_B1 * m + (1.0 - ADAM_B1) * g
    v2 = ADAM_B2 * v + (1.0 - ADAM_B2) * (g * g)
    m_hat = m2 * (1.0 / (1.0 - ADAM_B1 ** ADAM_STEP))
    v_hat = v2 * (1.0 / (1.0 - ADAM_B2 ** ADAM_STEP))
    delta = -ADAM_LR * (m_hat / (jnp.sqrt(v_hat) + ADAM_EPS) + ADAM_WD * w)
    return delta, m2, v2


def _adam_big(recv, w, m, v, *, name):
    r, c = w.shape
    tr = _tile(r, 256, 16)

    def body(recv_ref, w_ref, m_ref, v_ref, g_ref, d_ref, m2_ref, v2_ref):
        g = recv_ref[0].astype(F32)
        for i in range(1, N_DEV):
            g = g + recv_ref[i].astype(F32)
        delta, m2, v2 = _adamw(w_ref[...], g, m_ref[...], v_ref[...])
        g_ref[...] = g
        d_ref[...] = delta
        m2_ref[...] = m2
        v2_ref[...] = v2

    tile = pl.BlockSpec((tr, c), lambda i: (i, 0))
    out = jax.ShapeDtypeStruct((r, c), F32)
    return pl.pallas_call(
        body, name=name, grid=(r // tr,),
        in_specs=[pl.BlockSpec((N_DEV, tr, c), lambda i: (0, i, 0)), tile, tile, tile],
        out_specs=[tile] * 4, out_shape=[out] * 4, compiler_params=_params("parallel"),
    )(recv, w, m, v)


def _adam_small(parts, w, m, v, *, name):
    lb0, lbn = SMALL_ROWS["hgrn_lb"]
    half = lbn // 2

    def body(parts_ref, w_ref, m_ref, v_ref, g_ref, d_ref, m2_ref, v2_ref):
        g = parts_ref[0]
        for i in range(1, N_DEV):
            g = g + parts_ref[i]
        w_ = w_ref[...]
        s0 = 1.0 / (1.0 + jnp.exp(w_[lb0 + half:lb0 + lbn] - w_[lb0:lb0 + half]))
        ga = g[lb0:lb0 + half] * s0 * (1.0 - s0)
        g = jnp.concatenate([g[:lb0], ga, -ga, g[lb0 + lbn:]], axis=0)
        delta, m2, v2 = _adamw(w_, g, m_ref[...], v_ref[...])
        g_ref[...] = g
        d_ref[...] = delta
        m2_ref[...] = m2
        v2_ref[...] = v2

    out = jax.ShapeDtypeStruct(w.shape, F32)
    return pl.pallas_call(body, name=name, out_shape=[out] * 4, compiler_params=_params())(parts, w, m, v)


def _pack_small(vals):
    pieces, at = [], 0
    for name, (row0, nrows) in SMALL_ROWS.items():
        if row0 > at:
            pieces.append(jnp.zeros((row0 - at, 128), F32))
        pieces.append(vals[name].astype(F32).reshape(nrows, 128))
        at = row0 + nrows
    if at < SMALL_TOTAL_ROWS:
        pieces.append(jnp.zeros((SMALL_TOTAL_ROWS - at, 128), F32))
    return jnp.concatenate(pieces, axis=0)


def _unpack_small(packed, shapes):
    return {name: packed[row0:row0 + nrows].reshape(shapes[name]) for name, (row0, nrows) in SMALL_ROWS.items()}


SPLIT_AXIS = dict(BIG_WEIGHTS)


def _gather_of(names, weights):
    return _Exchange([_shard_to_send(weights[k][0], SPLIT_AXIS[k]) for k in names], gather=True)


def _scatter_of(names, dfull):
    return _Exchange([_to_slots(dfull[k], SPLIT_AXIS[k]) for k in names], gather=False)


def _shard_to_send(w, axis):
    return (w.T if axis == 1 else w).astype(BF16)


def _to_slots(dw, axis):
    k, m = dw.shape
    if axis == 0:
        return dw.reshape(N_DEV, k // N_DEV, m)
    return dw.reshape(k, N_DEV, m // N_DEV).transpose(1, 0, 2)


def _from_slots(gathered, axis):
    _, r, c = gathered.shape
    return gathered.reshape(N_DEV * r, c)


def kernel(x, p, ffn1_norm, ffn1_w1, ffn1_w3, ffn1_w2, mix_norm, w_in, hgrn_lb, hgrn_onorm, w_branch_a, pool_w, pool_scale, w_branch_b, w_out, ffn2_norm, ffn2_w1, ffn2_w3, ffn2_w2, ple_norm, ple_w_gate, ple_w_proj, ple_post_norm, final_norm, loss_target, m_ffn1_norm, m_ffn1_w1, m_ffn1_w3, m_ffn1_w2, m_mix_norm, m_w_in, m_hgrn_lb, m_hgrn_onorm, m_w_branch_a, m_pool_w, m_pool_scale, m_w_branch_b, m_w_out, m_ffn2_norm, m_ffn2_w1, m_ffn2_w3, m_ffn2_w2, m_ple_norm, m_ple_w_gate, m_ple_w_proj, m_ple_post_norm, m_final_norm, v_ffn1_norm, v_ffn1_w1, v_ffn1_w3, v_ffn1_w2, v_mix_norm, v_w_in, v_hgrn_lb, v_hgrn_onorm, v_w_branch_a, v_pool_w, v_pool_scale, v_w_branch_b, v_w_out, v_ffn2_norm, v_ffn2_w1, v_ffn2_w3, v_ffn2_w2, v_ple_norm, v_ple_w_gate, v_ple_w_proj, v_ple_post_norm, v_final_norm):
    weights = dict(ffn1_norm=ffn1_norm, ffn1_w1=ffn1_w1, ffn1_w3=ffn1_w3, ffn1_w2=ffn1_w2, mix_norm=mix_norm, w_in=w_in, hgrn_lb=hgrn_lb, hgrn_onorm=hgrn_onorm, w_branch_a=w_branch_a, pool_w=pool_w, pool_scale=pool_scale, w_branch_b=w_branch_b, w_out=w_out, ffn2_norm=ffn2_norm, ffn2_w1=ffn2_w1, ffn2_w3=ffn2_w3, ffn2_w2=ffn2_w2, ple_norm=ple_norm, ple_w_gate=ple_w_gate, ple_w_proj=ple_w_proj, ple_post_norm=ple_post_norm, final_norm=final_norm)
    mom1 = dict(ffn1_norm=m_ffn1_norm, ffn1_w1=m_ffn1_w1, ffn1_w3=m_ffn1_w3, ffn1_w2=m_ffn1_w2, mix_norm=m_mix_norm, w_in=m_w_in, hgrn_lb=m_hgrn_lb, hgrn_onorm=m_hgrn_onorm, w_branch_a=m_w_branch_a, pool_w=m_pool_w, pool_scale=m_pool_scale, w_branch_b=m_w_branch_b, w_out=m_w_out, ffn2_norm=m_ffn2_norm, ffn2_w1=m_ffn2_w1, ffn2_w3=m_ffn2_w3, ffn2_w2=m_ffn2_w2, ple_norm=m_ple_norm, ple_w_gate=m_ple_w_gate, ple_w_proj=m_ple_w_proj, ple_post_norm=m_ple_post_norm, final_norm=m_final_norm)
    mom2 = dict(ffn1_norm=v_ffn1_norm, ffn1_w1=v_ffn1_w1, ffn1_w3=v_ffn1_w3, ffn1_w2=v_ffn1_w2, mix_norm=v_mix_norm, w_in=v_w_in, hgrn_lb=v_hgrn_lb, hgrn_onorm=v_hgrn_onorm, w_branch_a=v_w_branch_a, pool_w=v_pool_w, pool_scale=v_pool_scale, w_branch_b=v_w_branch_b, w_out=v_w_out, ffn2_norm=v_ffn2_norm, ffn2_w1=v_ffn2_w1, ffn2_w3=v_ffn2_w3, ffn2_w2=v_ffn2_w2, ple_norm=v_ple_norm, ple_w_gate=v_ple_w_gate, ple_w_proj=v_ple_w_proj, ple_post_norm=v_ple_post_norm, final_norm=v_final_norm)

    xs = x[0]
    ps = p[0, 0].astype(BF16)
    tgt = loss_target[0]
    n = xs.shape[0]

    g_f1, g_mix, g_on, g_f2 = ffn1_norm, mix_norm, hgrn_onorm, ffn2_norm
    g_ple, g_post, g_fin = ple_norm, ple_post_norm, final_norm.reshape(1, D_MODEL)
    lb2 = hgrn_lb
    pw, pscale = pool_w[0], pool_scale

    full = {}

    def keep(names, gathered):
        for k, g in zip(names, gathered):
            full[k] = _from_slots(g, SPLIT_AXIS[k])

    first_names = ("ffn1_w1", "ffn1_w3")
    keep(first_names, _exchange_now([_shard_to_send(weights[k][0], SPLIT_AXIS[k]) for k in first_names],
                                    name="gather_first", gather=True))
    h1 = _rms_fwd(xs, g_f1, name="ffn1_rms")
    names = ("ffn1_w2", "w_in")
    ex = _gather_of(names, weights)
    a1, b1, s1 = _ffn_up(h1, full["ffn1_w1"], full["ffn1_w3"], name="ffn1_up", exchange=ex)
    keep(names, ex.received)
    names = ("w_branch_a", "w_branch_b", "w_out")
    ex = _gather_of(names, weights)
    x1, h2 = _mm_nn_res_rms(s1, full["ffn1_w2"], xs, g_mix, name="ffn1_down", scale=0.5, exchange=ex)
    keep(names, ex.received)
    names = ("ffn2_w1", "ffn2_w3", "ffn2_w2", "ple_w_gate", "ple_w_proj")
    ex = _gather_of(names, weights)
    proj = _mm_nn_wide(h2, full["w_in"], name="w_in_proj", out_dtype=BF16, exchange=ex)
    keep(names, ex.received)
    o, states = _hgrn_fwd(proj, lb2, name="hgrn_fwd")
    on = _hgrn_post_fwd(o, proj, g_on, name="hgrn_post_fwd")
    ya = _mm_nn(on, full["w_branch_a"], name="branch_a", tn=1024, tm=512)
    pooled, mixed = _pool_fwd(proj, pw, pscale, name="pool_fwd")
    yb = _mm_nt([(mixed, full["w_branch_b"])], name="branch_b", tn=1024, tk=512)
    y = _merge_fwd(proj, ya, yb, name="merge_fwd")
    x2, h3 = _mm_nn_res_rms(y, full["w_out"], x1, g_f2, name="w_out_proj", scale=1.0)
    a2, b2, s2 = _ffn_up(h3, full["ffn2_w1"], full["ffn2_w3"], name="ffn2_up")
    x3, h4 = _mm_nn_res_rms(s2, full["ffn2_w2"], x2, g_ple, name="ffn2_down", scale=0.5)
    gpre = _mm_nn(h4, full["ple_w_gate"], name="ple_gate", tn=1024, tm=512)
    z = _mm_nt([(ps, full["ple_w_proj"])], name="ple_proj", tn=1024, tk=512)
    dx4, dpre, dz, loss_part, d_fin, d_post = _ple_final(x3, gpre, z, tgt, g_post, g_fin, name="ple_final")

    dfull, received = {}, {}

    def sent(names, exchange):
        received.update(zip(names, exchange.received))

    dfull["ple_w_proj"] = _mm_tn(ps, dz, name="d_ple_w_proj", tn=1024, tm=1024)
    dfull["ple_w_gate"] = _mm_tn(h4, dpre, name="d_ple_w_gate", tn=1024, tm=1024)
    dx3, dx3s, d_ple = _mm_nt_rms_bwd([(dpre, full["ple_w_gate"], False)], x3, dx4, g_ple, name="ple_rms_bwd", tn=512,
                                      half_scale=0.5)

    names = ("ple_w_proj", "ple_w_gate")
    ex = _scatter_of(names, dfull)
    da2, db2 = _ffn_bwd_mid(dx3s, full["ffn2_w2"], a2, b2, name="ffn2_bwd_mid", exchange=ex)
    sent(names, ex)
    dfull["ffn2_w2"] = _mm_tn(s2, dx3s, name="ffn2_dw2", tn=1024, tm=512)
    dfull["ffn2_w1"] = _mm_tn(h3, da2, name="ffn2_dw1", tn=1024, tm=1408)
    dfull["ffn2_w3"] = _mm_tn(h3, db2, name="ffn2_dw3", tn=1024, tm=1408)
    names = ("ffn2_w2", "ffn2_w1")
    ex = _scatter_of(names, dfull)
    dx2, dx2b, d_f2 = _mm_nt_rms_bwd([(da2, full["ffn2_w1"], True), (db2, full["ffn2_w3"], True)], x2, dx3, g_f2,
                                     name="ffn2_rms_bwd", tn=512, half_scale=1.0, exchange=ex)
    sent(names, ex)

    dfull["w_out"] = _mm_tn(y, dx2b, name="d_w_out", tn=1024, tm=1024)
    dy = _mm_nt([(dx2b, full["w_out"])], name="d_y", tn=1024, tk=512)
    dya, dyb, dga, dgb = _merge_bwd(dy, proj, ya, yb, name="merge_bwd")

    dfull["w_branch_b"] = _mm_tn(mixed, dyb, name="d_w_branch_b", tn=1024, tm=1024)
    dmixed = _mm_nn(dyb, full["w_branch_b"], name="d_mixed", tn=1024, tm=512)
    du, d_pw, d_ps = _pool_bwd(dmixed, pooled, pw, pscale, name="pool_bwd")

    dfull["w_branch_a"] = _mm_tn(on, dya, name="d_w_branch_a", tn=1024, tm=1024)
    don = _mm_nt([(dya, full["w_branch_a"])], name="d_on", tn=1024, tk=512)
    dog, do, d_on = _hgrn_post_bwd(don, o, proj, g_on, name="hgrn_post_bwd")
    dqfi, d_lb = _hgrn_bwd(proj, lb2, do, states, name="hgrn_bwd")
    dproj = [dqfi, dog, du, dga, dgb]
    names = ("ffn2_w3", "w_out", "w_branch_b", "w_branch_a")
    ex = _scatter_of(names, dfull)
    dx1, dx1s, d_mix = _mm_nt_rms_bwd([(dproj, full["w_in"], True)], x1, dx2, g_mix, name="mix_rms_bwd", tn=512,
                                      half_scale=0.5, exchange=ex)
    sent(names, ex)
    dfull["w_in"] = jnp.concatenate(
        [_mm_tn(h2, part, name=f"d_w_in_{j}", tn=1024, tm=1536) for j, part in enumerate(dproj)], axis=1)

    half = dfull["w_in"].shape[0] // 2
    ex_top = _Exchange([_to_slots(dfull["w_in"][:half], 1)], gather=False)
    da1, db1 = _ffn_bwd_mid(dx1s, full["ffn1_w2"], a1, b1, name="ffn1_bwd_mid", exchange=ex_top)
    ex_bottom = _Exchange([_to_slots(dfull["w_in"][half:], 1)], gather=False)
    dfull["ffn1_w2"] = _mm_tn(s1, dx1s, name="ffn1_dw2", tn=1024, tm=512, exchange=ex_bottom)
    received["w_in"] = jnp.concatenate([ex_top.received[0], ex_bottom.received[0]], axis=1)
    names = ("ffn1_w2",)
    ex = _scatter_of(names, dfull)
    dfull["ffn1_w1"] = _mm_tn(h1, da1, name="ffn1_dw1", tn=1024, tm=1408, exchange=ex)
    sent(names, ex)
    names = ("ffn1_w1",)
    ex = _scatter_of(names, dfull)
    dfull["ffn1_w3"] = _mm_tn(h1, db1, name="ffn1_dw3", tn=1024, tm=1408, exchange=ex)
    sent(names, ex)
    names = ("ffn1_w3",)
    ex = _scatter_of(names, dfull)
    grad_x, _, d_f1 = _mm_nt_rms_bwd([(da1, full["ffn1_w1"], True), (db1, full["ffn1_w3"], True)], xs, dx1, g_f1,
                                     name="ffn1_rms_bwd", tn=512, half_scale=1.0, exchange=ex)
    sent(names, ex)

    small_part = _pack_small(dict(
        ffn1_norm=d_f1, mix_norm=d_mix, hgrn_onorm=d_on, ffn2_norm=d_f2, ple_norm=d_ple, ple_post_norm=d_post,
        final_norm=d_fin, hgrn_lb=jnp.concatenate([d_lb, jnp.zeros_like(d_lb)], axis=0), pool_scale=d_ps, pool_w=d_pw))
    small_all = _exchange_now([small_part], name="gather_small_grads", gather=True)[0]

    grads, deltas, new_m, new_v = {}, {}, {}, {}
    for name, _ in BIG_WEIGHTS:
        shape, recv = weights[name].shape, received[name]
        res = _adam_big(recv, weights[name][0], mom1[name][0], mom2[name][0], name=f"adam_{name}")
        grads[name], deltas[name], new_m[name], new_v[name] = [r.reshape(shape) for r in res]
    shapes = {name: weights[name].shape for name in SMALL_ROWS}
    res = _adam_small(small_all, _pack_small(weights), _pack_small(mom1), _pack_small(mom2), name="adam_small")
    for store, packed in zip((grads, deltas, new_m, new_v), res):
        store.update(_unpack_small(packed, shapes))

    loss = lax.psum(jnp.sum(loss_part), ("x", "y", "c"))
    return (loss, grad_x.reshape(x.shape), *[grads[k] for k in WEIGHT_ORDER], *[deltas[k] for k in WEIGHT_ORDER],
            *[new_m[k] for k in WEIGHT_ORDER], *[new_v[k] for k in WEIGHT_ORDER])
```

```python
import jax
import jax.numpy as jnp
from jax import lax
from jax.experimental import pallas as pl
from jax.experimental.pallas import tpu as pltpu

F32 = jnp.float32
BF16 = jnp.bfloat16

N_DEV = 8
D_MODEL = 1024
HEADS = 8
HEAD_DIM = 128
POOL_WINDOWS = (2, 4, 8, 16)
POOL_CH = 128
POOL_WIDTH = 512
POOL_HALO = 16
RMS_EPS = 1e-6
CHUNK = 64
SUB = 32
HGRN_HEADS_PER_STEP = 8
NEG_BIG = -1e30

ADAM_LR = 0.001
ADAM_B1 = 0.9
ADAM_B2 = 0.999
ADAM_EPS = 1e-08
ADAM_WD = 0.01
ADAM_STEP = 10

V7X_VMEM_BYTES = 64 * 1024 * 1024
VMEM_LIMIT = (V7X_VMEM_BYTES * 3) // 4
ROW_TILE_CAP = 8192

COL_Q, COL_F, COL_I, COL_OG, COL_POOL, COL_GA, COL_GB = 0, 1024, 2048, 3072, 4096, 4608, 5632

BIG_WEIGHTS = (
    ("ffn1_w1", 1), ("ffn1_w3", 1), ("ffn1_w2", 0), ("w_in", 1), ("w_branch_a", 0), ("w_branch_b", 1),
    ("w_out", 0), ("ffn2_w1", 1), ("ffn2_w3", 1), ("ffn2_w2", 0), ("ple_w_gate", 0), ("ple_w_proj", 1),
)
SMALL_ROWS = {
    "ffn1_norm": (0, 8), "mix_norm": (8, 8), "hgrn_onorm": (16, 8), "ffn2_norm": (24, 8), "ple_norm": (32, 8),
    "ple_post_norm": (40, 8), "final_norm": (48, 8), "hgrn_lb": (56, 16), "pool_scale": (72, 4), "pool_w": (80, 512),
}
SMALL_TOTAL_ROWS = 592
WEIGHT_ORDER = (
    "ffn1_norm", "ffn1_w1", "ffn1_w3", "ffn1_w2", "mix_norm", "w_in", "hgrn_lb", "hgrn_onorm", "w_branch_a", "pool_w",
    "pool_scale", "w_branch_b", "w_out", "ffn2_norm", "ffn2_w1", "ffn2_w3", "ffn2_w2", "ple_norm", "ple_w_gate",
    "ple_w_proj", "ple_post_norm", "final_norm",
)


def _params(*sem):
    return pltpu.CompilerParams(dimension_semantics=sem if sem else None, vmem_limit_bytes=VMEM_LIMIT)


COL_CHUNK = 256


def _rows(tn, width):
    return pl.BlockSpec((tn, width), lambda i: (i, 0))


def _resident(shape):
    return pl.BlockSpec(shape, lambda i: (0,) * len(shape), pipeline_mode=pl.Buffered(1))


def _dot(a, b):
    return jnp.dot(a, b, preferred_element_type=F32)


def _dot_nt(a, b):
    return lax.dot_general(a, b, (((1,), (1,)), ((), ())), preferred_element_type=F32)


def _dot_tn(a, b):
    return lax.dot_general(a, b, (((0,), (0,)), ((), ())), preferred_element_type=F32)


def _sigmoid(x):
    return 0.5 * jnp.tanh(0.5 * x) + 0.5


def _tile(n, want, mult):
    if mult != 128:
        want = min(want, ROW_TILE_CAP)
    if n <= want:
        return n
    t = (want // mult) * mult
    while t > mult and n % t:
        t -= mult
    assert n % t == 0, (n, want, mult)
    return t


class _Exchange:
    COPIES = N_DEV - 1

    def __init__(self, arrs, gather):
        self.arrs, self.gather, self.n = list(arrs), gather, len(arrs)
        self.out_shape = [jax.ShapeDtypeStruct((N_DEV,) + (a.shape if gather else a.shape[1:]), a.dtype) for a in arrs]
        self.scratch = [pltpu.SemaphoreType.DMA((self.n * self.COPIES,)),
                        pltpu.SemaphoreType.DMA((self.n * self.COPIES,)), pltpu.SemaphoreType.DMA((self.n,))]
        self.received = None

    @staticmethod
    def _place():
        x, y, c = lax.axis_index("x"), lax.axis_index("y"), lax.axis_index("c")
        return x, y, c

    def _copy(self, a, k, src, dst, to, sems):
        s = a * self.COPIES + k
        return pltpu.make_async_remote_copy(src_ref=src, dst_ref=dst, send_sem=sems[0].at[s], recv_sem=sems[1].at[s],
                                            device_id=to, device_id_type=pl.DeviceIdType.MESH)

    def _gather_copies(self, ins, outs, sems):
        x, y, c = self._place()
        chips = [(1 - x, y), (x, 1 - y), (1 - x, 1 - y)]
        slot = lambda px, py, pc: 4 * px + 2 * py + pc
        first, passed, arrivals = [], [], []
        for a in range(self.n):
            mine = outs[a].at[slot(x, y, c)]
            first.append(self._copy(a, 0, ins[a], mine, (x, y, 1 - c), sems))
            arrivals.append(self._copy(a, 0, ins[a], outs[a].at[slot(x, y, 1 - c)], (x, y, 1 - c), sems))
            for j, (px, py) in enumerate(chips):
                first.append(self._copy(a, 1 + j, ins[a], mine, (px, py, c), sems))
                theirs = outs[a].at[slot(px, py, c)]
                passed.append((self._copy(a, 1 + j, ins[a], theirs, (px, py, c), sems),
                               self._copy(a, 4 + j, theirs, theirs, (x, y, 1 - c), sems)))
                arrivals.append(self._copy(a, 4 + j, ins[a], outs[a].at[slot(px, py, 1 - c)], (x, y, 1 - c), sems))
        return first, passed, arrivals

    def _scatter_copies(self, ins, outs, sems):
        x, y, c = self._place()
        me = 4 * x + 2 * y + c
        sends, arrivals = [], []
        for k in range(1, N_DEV):
            px = 1 - x if k & 4 else x
            py = 1 - y if k & 2 else y
            pc = 1 - c if k & 1 else c
            peer = 4 * px + 2 * py + pc
            for a in range(self.n):
                sends.append(self._copy(a, k - 1, ins[a].at[peer], outs[a].at[me], (px, py, pc), sems))
                arrivals.append(self._copy(a, k - 1, ins[a].at[peer], outs[a].at[peer], (px, py, pc), sems))
        return sends, arrivals

    def _local(self, ins, outs, sems):
        x, y, c = self._place()
        me = 4 * x + 2 * y + c
        return [pltpu.make_async_copy(ins[a] if self.gather else ins[a].at[me], outs[a].at[me], sems[2].at[a])
                for a in range(self.n)]

    def start(self, ins, outs, sems):
        for cp in self._local(ins, outs, sems):
            cp.start()
        sends = self._gather_copies(ins, outs, sems)[0] if self.gather else self._scatter_copies(ins, outs, sems)[0]
        for cp in sends:
            cp.start()

    def finish(self, ins, outs, sems):
        if self.gather:
            first, passed, arrivals = self._gather_copies(ins, outs, sems)
            for landed, onward in passed:
                landed.wait_recv()
                onward.start()
            sends = first + [onward for _, onward in passed]
        else:
            sends, arrivals = self._scatter_copies(ins, outs, sems)
        for cp in arrivals:
            cp.wait_recv()
        for cp in sends:
            cp.wait_send()
        for cp in self._local(ins, outs, sems):
            cp.wait()


def _call(body, *, name, grid, in_specs, out_specs, out_shape, args, semantics, scratch=(), exchange=None):
    if exchange is None:
        return pl.pallas_call(
            body, name=name, grid=grid, in_specs=in_specs, out_specs=out_specs, out_shape=out_shape,
            scratch_shapes=list(scratch), compiler_params=_params(*semantics))(*args)
    ex = exchange
    n_in, n_out, n_s = len(in_specs), len(out_specs), len(scratch)

    def wrapped(*refs):
        ins, ex_in = refs[:n_in], refs[n_in:n_in + ex.n]
        o0 = n_in + ex.n
        outs, ex_out = refs[o0:o0 + n_out], refs[o0 + n_out:o0 + n_out + ex.n]
        s0 = o0 + n_out + ex.n
        scr, sems = refs[s0:s0 + n_s], refs[s0 + n_s:]
        ids = [pl.program_id(ax) for ax in range(len(grid))]
        first = ids[0] == 0
        last = ids[0] == grid[0] - 1
        for ax in range(1, len(grid)):
            first = jnp.logical_and(first, ids[ax] == 0)
            last = jnp.logical_and(last, ids[ax] == grid[ax] - 1)

        @pl.when(first)
        def _():
            ex.start(ex_in, ex_out, sems)

        body(*ins, *outs, *scr)

        @pl.when(last)
        def _():
            ex.finish(ex_in, ex_out, sems)

    hbm = pl.BlockSpec(memory_space=pltpu.HBM)
    res = pl.pallas_call(
        wrapped, name=name, grid=grid, in_specs=list(in_specs) + [hbm] * ex.n,
        out_specs=list(out_specs) + [hbm] * ex.n, out_shape=list(out_shape) + ex.out_shape,
        scratch_shapes=list(scratch) + ex.scratch, compiler_params=_params(*(["arbitrary"] * len(grid))),
    )(*args, *ex.arrs)
    ex.received = res[n_out:]
    return res[:n_out]


def _exchange_now(arrs, *, name, gather):
    ex = _Exchange(arrs, gather)
    n = ex.n

    def body(*refs):
        ex.start(refs[:n], refs[n:2 * n], refs[2 * n:])
        ex.finish(refs[:n], refs[n:2 * n], refs[2 * n:])

    hbm = pl.BlockSpec(memory_space=pltpu.HBM)
    return pl.pallas_call(body, name=name, out_shape=ex.out_shape, in_specs=[hbm] * n, out_specs=[hbm] * n,
                          scratch_shapes=ex.scratch)(*arrs)


def _mm_nn(a, b, *, name, tn, tm, out_dtype=F32, res=None, scale=1.0, exchange=None):
    n, k = a.shape
    m = b.shape[1]
    tn, tm = _tile(n, tn, 16), _tile(m, tm, 128)

    def body(*refs):
        a_ref, b_ref = refs[0], refs[1]
        o_ref = refs[-1]
        acc = _dot(a_ref[...], b_ref[...])
        if scale != 1.0:
            acc = acc * scale
        if res is not None:
            acc = acc + refs[2][...]
        o_ref[...] = acc.astype(o_ref.dtype)

    in_specs = [pl.BlockSpec((tn, k), lambda i, j: (i, 0)), pl.BlockSpec((k, tm), lambda i, j: (0, j))]
    args = [a, b]
    if res is not None:
        in_specs.append(pl.BlockSpec((tn, tm), lambda i, j: (i, j)))
        args.append(res)
    return _call(body, name=name, grid=(n // tn, m // tm), in_specs=in_specs,
                 out_specs=[pl.BlockSpec((tn, tm), lambda i, j: (i, j))],
                 out_shape=[jax.ShapeDtypeStruct((n, m), out_dtype)], args=args, semantics=("parallel", "parallel"),
                 exchange=exchange)[0]


def _mm_nt(pairs, *, name, tn, tk, out_dtype=F32, exchange=None):
    n = pairs[0][0].shape[0]
    kk = pairs[0][1].shape[0]
    tn, tk = _tile(n, tn, 16), _tile(kk, tk, 128)
    npair = len(pairs)

    def body(*refs):
        o_ref = refs[-1]
        acc = _dot_nt(refs[0][...], refs[1][...])
        for q in range(1, npair):
            acc = acc + _dot_nt(refs[2 * q][...], refs[2 * q + 1][...])
        o_ref[...] = acc.astype(o_ref.dtype)

    in_specs, args = [], []
    for a, b in pairs:
        m = a.shape[1]
        in_specs += [pl.BlockSpec((tn, m), lambda i, j: (i, 0)), pl.BlockSpec((tk, m), lambda i, j: (j, 0))]
        args += [a, b]
    return _call(body, name=name, grid=(n // tn, kk // tk), in_specs=in_specs,
                 out_specs=[pl.BlockSpec((tn, tk), lambda i, j: (i, j))],
                 out_shape=[jax.ShapeDtypeStruct((n, kk), out_dtype)], args=args, semantics=("parallel", "parallel"),
                 exchange=exchange)[0]


def _mm_tn(a, b, *, name, tn, tm, exchange=None):
    n, k = a.shape
    m = b.shape[1]
    tn, tm = _tile(n, tn, 16), _tile(m, tm, 128)
    steps = n // tn

    def body(a_ref, b_ref, o_ref, acc):
        i = pl.program_id(1)

        @pl.when(i == 0)
        def _():
            acc[...] = jnp.zeros_like(acc)

        acc[...] += _dot_tn(a_ref[...], b_ref[...])

        @pl.when(i == steps - 1)
        def _():
            o_ref[...] = acc[...].astype(o_ref.dtype)

    return _call(body, name=name, grid=(m // tm, steps),
                 in_specs=[pl.BlockSpec((tn, k), lambda j, i: (i, 0)), pl.BlockSpec((tn, tm), lambda j, i: (i, j))],
                 out_specs=[pl.BlockSpec((k, tm), lambda j, i: (0, j))],
                 out_shape=[jax.ShapeDtypeStruct((k, m), BF16)], args=[a, b], semantics=("parallel", "arbitrary"),
                 scratch=[pltpu.VMEM((k, tm), F32)], exchange=exchange)[0]


def _ffn_up(h, w1, w3, *, name, exchange=None):
    n, k = h.shape
    m = w1.shape[0]
    tn = _tile(n, 512, 16)

    def body(h_ref, w1_ref, w3_ref, dsda_ref, dsdb_ref, s_ref):
        for c0 in range(0, m, COL_CHUNK):
            cols = slice(c0, c0 + COL_CHUNK)
            a = _dot_nt(h_ref[...], w1_ref[cols, :])
            b = _dot_nt(h_ref[...], w3_ref[cols, :])
            sg = _sigmoid(a)
            silu = a * sg
            dsda_ref[:, cols] = (b * (sg + silu * (1.0 - sg))).astype(dsda_ref.dtype)
            dsdb_ref[:, cols] = silu.astype(dsdb_ref.dtype)
            s_ref[:, cols] = (silu * b).astype(s_ref.dtype)

    ospec = _rows(tn, m)
    return _call(body, name=name, grid=(n // tn,),
                 in_specs=[_rows(tn, k), _resident(w1.shape), _resident(w3.shape)], out_specs=[ospec, ospec, ospec],
                 out_shape=[jax.ShapeDtypeStruct((n, m), BF16)] * 3,
                 args=[h, w1, w3], semantics=("parallel",), exchange=exchange)


def _mm_nn_wide(a, b, *, name, out_dtype, exchange=None):
    n, k = a.shape
    m = b.shape[0]
    tn = _tile(n, 512, 16)
    chunk = 2 * COL_CHUNK

    def body(a_ref, b_ref, o_ref):
        for c0 in range(0, m, chunk):
            cols = slice(c0, c0 + chunk)
            o_ref[:, cols] = _dot_nt(a_ref[...], b_ref[cols, :]).astype(o_ref.dtype)

    return _call(body, name=name, grid=(n // tn,), in_specs=[_rows(tn, k), _resident(b.shape)],
                 out_specs=[_rows(tn, m)], out_shape=[jax.ShapeDtypeStruct((n, m), out_dtype)], args=[a, b],
                 semantics=("parallel",), exchange=exchange)[0]


def _mm_nn_res_rms(a, b, res, g, *, name, scale, exchange=None):
    n, k = a.shape
    d = b.shape[1]
    tn = _tile(n, 512, 16)

    def body(a_ref, b_ref, r_ref, g_ref, x_ref, h_ref):
        for c0 in range(0, d, COL_CHUNK):
            cols = slice(c0, c0 + COL_CHUNK)
            x_ref[:, cols] = r_ref[:, cols] + scale * _dot(a_ref[...], b_ref[:, cols])
        x = x_ref[...]
        r = lax.rsqrt(_rowmean(x * x) + RMS_EPS)
        h_ref[...] = (x * r * g_ref[...]).astype(h_ref.dtype)

    row = _rows(tn, d)
    return _call(body, name=name, grid=(n // tn,),
                 in_specs=[_rows(tn, k), _resident(b.shape), row, pl.BlockSpec((1, d), lambda i: (0, 0))],
                 out_specs=[row, row],
                 out_shape=[jax.ShapeDtypeStruct((n, d), F32), jax.ShapeDtypeStruct((n, d), BF16)],
                 args=[a, b, res, g], semantics=("parallel",), exchange=exchange)


def _ffn_bwd_mid(dxs, w2, dsda, dsdb, *, name, exchange=None):
    n, d = dxs.shape
    m = w2.shape[0]
    tn = _tile(n, 512, 16)

    def body(dx_ref, w2_ref, dsda_ref, dsdb_ref, da_ref, db_ref):
        for c0 in range(0, m, COL_CHUNK):
            cols = slice(c0, c0 + COL_CHUNK)
            ds = _dot_nt(dx_ref[...], w2_ref[cols, :])
            da_ref[:, cols] = (ds * dsda_ref[:, cols].astype(F32)).astype(da_ref.dtype)
            db_ref[:, cols] = (ds * dsdb_ref[:, cols].astype(F32)).astype(db_ref.dtype)

    tile = _rows(tn, m)
    return _call(body, name=name, grid=(n // tn,),
                 in_specs=[_rows(tn, d), _resident(w2.shape), tile, tile], out_specs=[tile, tile],
                 out_shape=[jax.ShapeDtypeStruct((n, m), BF16), jax.ShapeDtypeStruct((n, m), BF16)],
                 args=[dxs, w2, dsda, dsdb], semantics=("parallel",), exchange=exchange)


def _rowwise(fn, *, name, n, tn, ncol, rows, vecs, outs, accs=()):
    tn = _tile(n, tn, 16)
    nr, nv, no = len(rows), len(vecs), len(outs)

    def body(*refs):
        first = pl.program_id(1) == 0
        vals = [r[...].astype(F32) for r in refs[:nr + nv]]
        res = fn(*vals)
        for ref, val in zip(refs[nr + nv:nr + nv + no], res[:no]):
            ref[...] = val.astype(ref.dtype)
        for ref, val in zip(refs[nr + nv + no:], res[no:]):
            _accumulate(ref, val, first)

    in_specs = [pl.BlockSpec((tn, w), lambda j, i, c0=c0: (i, c0 + j)) for _, w, c0 in rows]
    in_specs += [pl.BlockSpec((1, w), lambda j, i, c0=c0: (0, c0 + j)) for _, w, c0 in vecs]
    out_specs = [pl.BlockSpec((tn, w), lambda j, i: (i, j)) for _, w, _ in outs]
    out_specs += [pl.BlockSpec((1, w), lambda j, i: (0, j)) for _, w in accs]
    out_shape = [jax.ShapeDtypeStruct((n, tw), dt) for tw, _, dt in outs]
    out_shape += [jax.ShapeDtypeStruct((1, tw), F32) for tw, _ in accs]
    return pl.pallas_call(
        body, name=name, grid=(ncol, n // tn), in_specs=in_specs, out_specs=out_specs, out_shape=out_shape,
        compiler_params=_params("parallel", "arbitrary"),
    )(*[r[0] for r in rows], *[v[0] for v in vecs])


def _accumulate(ref, val, first):
    @pl.when(first)
    def _():
        ref[...] = jnp.zeros_like(ref)

    ref[...] += val


def _colsum(x):
    return jnp.sum(x, axis=0, keepdims=True)


def _rowmean(x):
    return jnp.mean(x, axis=-1, keepdims=True)


def _rms_fwd(x, g, *, name):
    def fn(x_, g_):
        r = lax.rsqrt(_rowmean(x_ * x_) + RMS_EPS)
        return (x_ * r * g_,)

    n, d = x.shape
    return _rowwise(fn, name=name, n=n, tn=512, ncol=1, rows=[(x, d, 0)], vecs=[(g, d, 0)], outs=[(d, d, BF16)])[0]


def _mm_nt_rms_bwd(pairs, x, extra, g, *, name, tn, half_scale, exchange=None):
    n, d = x.shape
    tn = _tile(n, tn, 16)
    pairs = [(list(a) if isinstance(a, (list, tuple)) else [a], b, t) for a, b, t in pairs]
    nref = sum(len(a) + 1 for a, _, _ in pairs)

    def body(*refs):
        x_ref, e_ref, g_ref, dx_ref, dxs_ref, dg_ref = refs[nref:]
        dh, at = None, 0
        for parts, _, transposed in pairs:
            b_ref = refs[at + len(parts)]
            col = 0
            for j, part in enumerate(parts):
                w = part.shape[1]
                if transposed:
                    term = _dot(refs[at + j][...], b_ref[col:col + w, :])
                else:
                    term = _dot_nt(refs[at + j][...], b_ref[:, col:col + w])
                dh = term if dh is None else dh + term
                col += w
            at += len(parts) + 1
        x_ = x_ref[...]
        r = lax.rsqrt(_rowmean(x_ * x_) + RMS_EPS)
        xh = x_ * r
        dxh = dh * g_ref[...]
        dx = e_ref[...] + r * (dxh - xh * _rowmean(dxh * xh))
        dx_ref[...] = dx
        dxs_ref[...] = (dx * half_scale).astype(dxs_ref.dtype)
        _accumulate(dg_ref, _colsum(dh * xh), pl.program_id(0) == 0)

    in_specs, args = [], []
    for parts, b, transposed in pairs:
        assert sum(part.shape[1] for part in parts) == b.shape[0 if transposed else 1]
        in_specs += [pl.BlockSpec((tn, part.shape[1]), lambda i: (i, 0)) for part in parts]
        in_specs.append(pl.BlockSpec(b.shape, lambda i: (0, 0), pipeline_mode=pl.Buffered(1)))
        args += parts + [b]
    row = pl.BlockSpec((tn, d), lambda i: (i, 0))
    vec = pl.BlockSpec((1, d), lambda i: (0, 0))
    return _call(body, name=name, grid=(n // tn,), in_specs=in_specs + [row, row, vec], out_specs=[row, row, vec],
                 out_shape=[jax.ShapeDtypeStruct((n, d), F32), jax.ShapeDtypeStruct((n, d), BF16),
                            jax.ShapeDtypeStruct((1, d), F32)],
                 args=args + [x, extra, g], semantics=("arbitrary",), exchange=exchange)


def _ple_final(x3, gpre, z, tgt, gpp, gf, *, name):
    def fn(x3_, gpre_, z_, tgt_, gpp_, gf_):
        gate = _sigmoid(gpre_)
        rz = lax.rsqrt(_rowmean(z_ * z_) + RMS_EPS)
        zh = z_ * rz
        e = zh * gpp_
        x4 = x3_ + gate * e
        r4 = lax.rsqrt(_rowmean(x4 * x4) + RMS_EPS)
        x4h = x4 * r4
        diff = x4h * gf_ - tgt_
        dout = diff * (1.0 / D_MODEL)
        dxh4 = dout * gf_
        dx4 = r4 * (dxh4 - x4h * _rowmean(dxh4 * x4h))
        dpre = dx4 * e * gate * (1.0 - gate)
        de = dx4 * gate
        dzh = de * gpp_
        dz = rz * (dzh - zh * _rowmean(dzh * zh))
        return dx4, dpre, dz, _colsum(diff * diff) * (0.5 / D_MODEL), _colsum(dout * x4h), _colsum(de * zh)

    n, d = x3.shape
    return _rowwise(fn, name=name, n=n, tn=256, ncol=1, rows=[(x3, d, 0), (gpre, d, 0), (z, d, 0), (tgt, d, 0)],
                    vecs=[(gpp, d, 0), (gf, d, 0)], outs=[(d, d, F32), (d, d, BF16), (d, d, BF16)],
                    accs=[(d, d), (d, d), (d, d)])


def _merge_fwd(proj, ya, yb, *, name):
    def fn(ga, gb, ya_, yb_):
        return (_sigmoid(ga) * ya_ + _sigmoid(gb) * yb_,)

    n = proj.shape[0]
    w = 512
    return _rowwise(fn, name=name, n=n, tn=512, ncol=D_MODEL // w,
                    rows=[(proj, w, COL_GA // w), (proj, w, COL_GB // w), (ya, w, 0), (yb, w, 0)], vecs=[],
                    outs=[(D_MODEL, w, BF16)])[0]


def _merge_bwd(dy, proj, ya, yb, *, name):
    def fn(dy_, ga, gb, ya_, yb_):
        sa, sb = _sigmoid(ga), _sigmoid(gb)
        return dy_ * sa, dy_ * sb, dy_ * ya_ * sa * (1.0 - sa), dy_ * yb_ * sb * (1.0 - sb)

    n = proj.shape[0]
    w = 512
    return _rowwise(fn, name=name, n=n, tn=512, ncol=D_MODEL // w,
                    rows=[(dy, w, 0), (proj, w, COL_GA // w), (proj, w, COL_GB // w), (ya, w, 0), (yb, w, 0)],
                    vecs=[], outs=[(D_MODEL, w, BF16)] * 4)


def _head_mean(x):
    return jnp.concatenate(
        [jnp.broadcast_to(jnp.mean(x[:, h * HEAD_DIM:(h + 1) * HEAD_DIM], axis=-1, keepdims=True),
                          (x.shape[0], HEAD_DIM)) for h in range(HEADS)], axis=1)


def _hgrn_post_fwd(o, proj, onorm, *, name):
    def fn(o_, og, gam):
        r = lax.rsqrt(_head_mean(o_ * o_) + RMS_EPS)
        return (o_ * r * gam * (og * _sigmoid(og)),)

    n = o.shape[0]
    w = D_MODEL
    return _rowwise(fn, name=name, n=n, tn=256, ncol=1, rows=[(o, w, 0), (proj, w, COL_OG // w)],
                    vecs=[(onorm, w, 0)], outs=[(D_MODEL, w, BF16)])[0]


def _hgrn_post_bwd(don, o, proj, onorm, *, name):
    def fn(don_, o_, og, gam):
        r = lax.rsqrt(_head_mean(o_ * o_) + RMS_EPS)
        oh = o_ * r
        sg = _sigmoid(og)
        dog = don_ * oh * gam * (sg * (1.0 + og * (1.0 - sg)))
        dn = don_ * (og * sg)
        doh = dn * gam
        do = r * (doh - oh * _head_mean(doh * oh))
        return dog, do, _colsum(dn * oh)

    n = o.shape[0]
    w = D_MODEL
    return _rowwise(fn, name=name, n=n, tn=256, ncol=1, rows=[(don, w, 0), (o, w, 0), (proj, w, COL_OG // w)],
                    vecs=[(onorm, w, 0)], outs=[(D_MODEL, w, BF16), (D_MODEL, w, BF16)], accs=[(D_MODEL, w)])


def _split3(x):
    hi = x.astype(BF16)
    r1 = x - hi.astype(F32)
    mid = r1.astype(BF16)
    lo = (r1 - mid.astype(F32)).astype(BF16)
    return hi, mid, lo


def _tri_sum(tri, x):
    hi, mid, lo = _split3(x)
    return _dot(tri, hi) + _dot(tri, mid) + _dot(tri, lo)


def _lower_bound(lb_ref):
    return 1.0 / (1.0 + jnp.exp(lb_ref[1:2, :] - lb_ref[0:1, :]))


def _hgrn_specs(n, t, reverse):
    nt = n // t
    width = HGRN_HEADS_PER_STEP * HEAD_DIM

    def tok(i):
        return nt - 1 - i if reverse else i

    def sec(col):
        c0 = col // width
        return pl.BlockSpec((t, width), lambda h, i: (tok(i), c0 + h))

    head_tile = pl.BlockSpec((t, width), lambda h, i: (tok(i), h))
    state = pl.BlockSpec((HGRN_HEADS_PER_STEP, t // CHUNK, HEAD_DIM, HEAD_DIM), lambda h, i: (h, tok(i), 0, 0))
    lb = pl.BlockSpec((2, width), lambda h, i: (0, h))
    return sec, head_tile, state, lb


def _hgrn_fwd(proj, hgrn_lb, *, name):
    n = proj.shape[0]
    t = _tile(n, 512, CHUNK)
    nc = t // CHUNK
    hps = HGRN_HEADS_PER_STEP
    width = hps * HEAD_DIM
    lanes = [slice(h * HEAD_DIM, (h + 1) * HEAD_DIM) for h in range(hps)]
    sec, head_tile, state, lbspec = _hgrn_specs(n, t, False)

    def body(q_ref, f_ref, i_ref, lb_ref, o_ref, st_ref, s_acc, g_s, a_s):
        @pl.when(pl.program_id(1) == 0)
        def _():
            s_acc[...] = jnp.zeros_like(s_acc)

        lb = _lower_bound(lb_ref)
        row = lax.broadcasted_iota(jnp.int32, (CHUNK, CHUNK), 0)
        col = lax.broadcasted_iota(jnp.int32, (CHUNK, CHUNK), 1)
        tril = row >= col
        trilb = jnp.where(tril, 1.0, 0.0).astype(BF16)
        rowk = lax.broadcasted_iota(jnp.int32, (CHUNK, width), 0)

        def chunk(c, carry):
            rows = pl.ds(pl.multiple_of(c * CHUNK, CHUNK), CHUNK)
            qr, fr, v = [r[rows, :].astype(F32) for r in (q_ref, f_ref, i_ref)]
            q = qr * _sigmoid(qr)
            f = lb + (1.0 - lb) * _sigmoid(fr)
            k = 1.0 - f
            g = _tri_sum(trilb, jnp.log(f))
            g_s[...] = g
            st0 = [s_acc[h] for h in range(hps)]
            for h in range(hps):
                st_ref[h, c] = st0[h]
            vb = v.astype(BF16)
            for blk in range(CHUNK // SUB):
                lo, hi = blk * SUB, (blk + 1) * SUB
                gref = g_s[lo - 1:lo, :] if blk else jnp.zeros((1, width), F32)
                qi = (q[lo:hi] * jnp.exp(g[lo:hi] - gref)).astype(BF16)
                ki = (k * jnp.exp(jnp.where(rowk < hi, gref - g, NEG_BIG))).astype(BF16)
                for h, ln in enumerate(lanes):
                    a_s[h, lo:hi, :] = _dot_nt(qi[:, ln], ki[:, ln])
            qeb = (q * jnp.exp(g)).astype(BF16)
            o_ref[rows, :] = jnp.concatenate(
                [_dot(jnp.where(tril, a_s[h], 0.0).astype(BF16), vb[:, ln]) + _dot_nt(qeb[:, ln], st0[h].astype(BF16))
                 for h, ln in enumerate(lanes)], axis=1).astype(o_ref.dtype)
            glast = g_s[CHUNK - 1:CHUNK, :]
            kdb = (k * jnp.exp(glast - g)).astype(BF16)
            dec = jnp.exp(glast)
            for h, ln in enumerate(lanes):
                s_acc[h] = st0[h] * dec[:, ln] + _dot_tn(vb[:, ln], kdb[:, ln])
            return carry

        lax.fori_loop(0, nc, chunk, 0)

    return pl.pallas_call(
        body, name=name, grid=(HEADS // hps, n // t),
        in_specs=[sec(COL_Q), sec(COL_F), sec(COL_I), lbspec], out_specs=[head_tile, state],
        out_shape=[jax.ShapeDtypeStruct((n, D_MODEL), BF16),
                   jax.ShapeDtypeStruct((HEADS, n // CHUNK, HEAD_DIM, HEAD_DIM), F32)],
        scratch_shapes=[pltpu.VMEM((hps, HEAD_DIM, HEAD_DIM), F32), pltpu.VMEM((CHUNK, width), F32),
                        pltpu.VMEM((hps, CHUNK, CHUNK), F32)],
        compiler_params=_params("parallel", "arbitrary"),
    )(proj, proj, proj, hgrn_lb)


def _hgrn_bwd(proj, hgrn_lb, do, states, *, name):
    n = proj.shape[0]
    t = _tile(n, 512, CHUNK)
    nc = t // CHUNK
    hps = HGRN_HEADS_PER_STEP
    width = hps * HEAD_DIM
    lanes = [slice(h * HEAD_DIM, (h + 1) * HEAD_DIM) for h in range(hps)]
    sec, head_tile, state, lbspec = _hgrn_specs(n, t, True)

    def body(q_ref, f_ref, i_ref, lb_ref, do_ref, st_ref, dqfi_ref, dlb_ref, d_acc, g_s, a_s, dq_s,
             dg_s):
        first = pl.program_id(1) == 0

        @pl.when(first)
        def _():
            d_acc[...] = jnp.zeros_like(d_acc)

        lb = _lower_bound(lb_ref)
        row = lax.broadcasted_iota(jnp.int32, (CHUNK, CHUNK), 0)
        col = lax.broadcasted_iota(jnp.int32, (CHUNK, CHUNK), 1)
        tril = row >= col
        trilb = jnp.where(tril, 1.0, 0.0).astype(BF16)
        triub = jnp.where(row <= col, 1.0, 0.0).astype(BF16)
        rowk = lax.broadcasted_iota(jnp.int32, (CHUNK, width), 0)

        def per_head(fn):
            return jnp.concatenate([fn(h, ln) for h, ln in enumerate(lanes)], axis=1)

        def chunk(j, dlb):
            c = nc - 1 - j
            rows = pl.ds(pl.multiple_of(c * CHUNK, CHUNK), CHUNK)
            qr, fr, v, dout = [r[rows, :].astype(F32) for r in (q_ref, f_ref, i_ref, do_ref)]
            sq = _sigmoid(qr)
            q = qr * sq
            sf = _sigmoid(fr)
            f = lb + (1.0 - lb) * sf
            k = 1.0 - f
            g = _tri_sum(trilb, jnp.log(f))
            g_s[...] = g
            st0 = [st_ref[h, c] for h in range(hps)]
            dt = [d_acc[h] for h in range(hps)]
            vb, dob = v.astype(BF16), dout.astype(BF16)
            dtb = [x.astype(BF16) for x in dt]
            st0b = [x.astype(BF16) for x in st0]
            glast = g_s[CHUNK - 1:CHUNK, :]
            eg = jnp.exp(g)
            kdec = jnp.exp(glast - g)
            qeb, kdb = (q * eg).astype(BF16), (k * kdec).astype(BF16)
            aps = [jnp.where(row > col, _dot_nt(dob[:, ln], vb[:, ln]), 0.0) for ln in lanes]
            dov = dout * v
            adiag = per_head(lambda h, ln: jnp.broadcast_to(
                jnp.sum(dov[:, ln], axis=-1, keepdims=True), (CHUNK, HEAD_DIM)))
            dq_inter = per_head(lambda h, ln: _dot(dob[:, ln], st0b[h]))
            dk_inter = per_head(lambda h, ln: _dot(vb[:, ln], dtb[h]))
            dk_st = kdec * dk_inter
            dg = qeb.astype(F32) * dq_inter
            dg_minus = kdb.astype(F32) * dk_inter
            dg = dg - dg_minus
            for blk in range(CHUNK // SUB):
                lo, hi = blk * SUB, (blk + 1) * SUB
                gref = g_s[lo - 1:lo, :] if blk else jnp.zeros((1, width), F32)
                qscale = jnp.exp(g[lo:hi] - gref)
                kscale = jnp.exp(jnp.where(rowk < hi, gref - g, NEG_BIG))
                qi = (q[lo:hi] * qscale).astype(BF16)
                ki = (k * kscale).astype(BF16)
                for h, ln in enumerate(lanes):
                    a_s[h, lo:hi, :] = _dot_nt(qi[:, ln], ki[:, ln])
                apb = [x[lo:hi].astype(BF16) for x in aps]
                from_k = per_head(lambda h, ln: _dot(apb[h], ki[:, ln]))
                from_q = per_head(lambda h, ln: _dot_tn(apb[h], qi[:, ln]))
                dq_s[lo:hi, :] = qscale * from_k
                dg_s[lo:hi, :] = qi.astype(F32) * from_k
                dk_st = dk_st + kscale * from_q
                dg = dg - ki.astype(F32) * from_q
            dg = dg + dg_s[...]
            dv = per_head(lambda h, ln: _dot_tn(jnp.where(tril, a_s[h], 0.0).astype(BF16), dob[:, ln])
                          + _dot_nt(kdb[:, ln], dtb[h]))
            dq_st = dq_s[...] + eg * dq_inter
            dq = dq_st + adiag * k
            dk = dk_st + adiag * q
            dec = jnp.exp(glast)
            dt_dec = [dt[h] * dec[:, ln] for h, ln in enumerate(lanes)]
            for h, ln in enumerate(lanes):
                d_acc[h] = dt_dec[h] + _dot_tn(dob[:, ln], qeb[:, ln])
            later = per_head(lambda h, ln: _colsum(dt_dec[h] * st0[h])) + _colsum(dg_minus)
            dlf = later + _tri_sum(triub, dg)
            df = dlf / f - dk
            dqfi_ref[rows, 0:width] = (dq * (sq * (1.0 + qr * (1.0 - sq)))).astype(dqfi_ref.dtype)
            dqfi_ref[rows, width:2 * width] = (df * (1.0 - lb) * sf * (1.0 - sf)).astype(dqfi_ref.dtype)
            dqfi_ref[rows, 2 * width:3 * width] = dv.astype(dqfi_ref.dtype)
            return dlb + _colsum(df * (1.0 - sf))

        dlb = lax.fori_loop(0, nc, chunk, jnp.zeros((1, width), F32))
        _accumulate(dlb_ref, dlb, first)

    assert hps == HEADS
    nt = n // t
    return pl.pallas_call(
        body, name=name, grid=(1, nt),
        in_specs=[sec(COL_Q), sec(COL_F), sec(COL_I), lbspec, head_tile, state],
        out_specs=[pl.BlockSpec((t, 3 * width), lambda h, i: (nt - 1 - i, 0)),
                   pl.BlockSpec((1, width), lambda h, i: (0, h))],
        out_shape=[jax.ShapeDtypeStruct((n, 3 * D_MODEL), BF16), jax.ShapeDtypeStruct((1, D_MODEL), F32)],
        scratch_shapes=[pltpu.VMEM((hps, HEAD_DIM, HEAD_DIM), F32), pltpu.VMEM((CHUNK, width), F32),
                        pltpu.VMEM((hps, CHUNK, CHUNK), F32), pltpu.VMEM((CHUNK, width), F32),
                        pltpu.VMEM((CHUNK, width), F32)],
        compiler_params=_params("parallel", "arbitrary"),
    )(proj, proj, proj, hgrn_lb, do, states)


def _pool_fwd(proj, pool_w, pool_scale, *, name):
    n = proj.shape[0]
    t = _tile(n, 512, POOL_HALO)
    per = t // POOL_HALO
    c0 = COL_POOL // POOL_WIDTH

    def body(u_ref, halo_ref, pw_ref, ps_ref, pooled_ref, mixed_ref, ext):
        i = pl.program_id(0)
        u = u_ref[...].astype(F32)
        ext[POOL_HALO:POOL_HALO + t, :] = u
        ext[0:POOL_HALO, :] = jnp.where(i > 0, halo_ref[...].astype(F32), 0.0)
        pos = i * t + lax.broadcasted_iota(jnp.int32, (t, POOL_CH), 0) + 1
        for grp, win in enumerate(POOL_WINDOWS):
            cols = slice(grp * POOL_CH, (grp + 1) * POOL_CH)
            acc = u[:, cols]
            for j in range(1, win):
                acc = acc + ext[POOL_HALO - j:POOL_HALO - j + t, cols]
            pooled = (acc / jnp.minimum(pos, win).astype(F32) - u[:, cols]).astype(BF16)
            pooled_ref[:, cols] = pooled
            mixed_ref[:, cols] = (_dot(pooled, pw_ref[grp].astype(BF16)) * ps_ref[:, cols]).astype(BF16)

    tile = pl.BlockSpec((t, POOL_WIDTH), lambda i: (i, 0))
    return pl.pallas_call(
        body, name=name, grid=(n // t,),
        in_specs=[pl.BlockSpec((t, POOL_WIDTH), lambda i: (i, c0)),
                  pl.BlockSpec((POOL_HALO, POOL_WIDTH), lambda i: (jnp.maximum(i * per - 1, 0), c0)),
                  pl.BlockSpec((len(POOL_WINDOWS), POOL_CH, POOL_CH), lambda i: (0, 0, 0)),
                  pl.BlockSpec((1, POOL_WIDTH), lambda i: (0, 0))],
        out_specs=[tile, tile],
        out_shape=[jax.ShapeDtypeStruct((n, POOL_WIDTH), BF16), jax.ShapeDtypeStruct((n, POOL_WIDTH), BF16)],
        scratch_shapes=[pltpu.VMEM((t + POOL_HALO, POOL_WIDTH), F32)],
        compiler_params=_params("parallel"),
    )(proj, proj, pool_w, pool_scale)


def _pool_bwd(dmixed, pooled, pool_w, pool_scale, *, name):
    n = dmixed.shape[0]
    t = _tile(n, 512, POOL_HALO)
    per = t // POOL_HALO
    nb = n // t

    def body(dm_ref, dmh_ref, p_ref, pw_ref, ps_ref, du_ref, dpw_ref, dps_ref, ext):
        i = pl.program_id(0)

        @pl.when(i == 0)
        def _():
            dpw_ref[...] = jnp.zeros_like(dpw_ref)
            dps_ref[...] = jnp.zeros_like(dps_ref)

        dm, dmh = dm_ref[...], dmh_ref[...]
        pos = i * t + lax.broadcasted_iota(jnp.int32, (t, POOL_CH), 0) + 1
        for grp, win in enumerate(POOL_WINDOWS):
            cols = slice(grp * POOL_CH, (grp + 1) * POOL_CH)
            pwb = pw_ref[grp].astype(BF16)
            pb = p_ref[:, cols]
            scale = ps_ref[:, cols]
            dps_ref[:, cols] += _colsum(dm[:, cols] * _dot(pb, pwb))
            dpm = (dm[:, cols] * scale).astype(BF16)
            dpw_ref[grp] += _dot_tn(pb, dpm)
            dpool = _dot_nt(dpm, pwb)
            dpool_next = _dot_nt((dmh[:, cols] * scale).astype(BF16), pwb)
            ext[0:t, cols] = dpool / jnp.minimum(pos, win).astype(F32)
            ext[t:t + POOL_HALO, cols] = jnp.where(i < nb - 1, dpool_next * (1.0 / win), 0.0)
            acc = -dpool
            for j in range(win):
                acc = acc + ext[j:j + t, cols]
            du_ref[:, cols] = acc.astype(du_ref.dtype)

    tile = pl.BlockSpec((t, POOL_WIDTH), lambda i: (i, 0))
    return pl.pallas_call(
        body, name=name, grid=(nb,),
        in_specs=[tile, pl.BlockSpec((POOL_HALO, POOL_WIDTH), lambda i: (jnp.minimum((i + 1) * per, nb * per - 1), 0)),
                  tile, pl.BlockSpec((len(POOL_WINDOWS), POOL_CH, POOL_CH), lambda i: (0, 0, 0)),
                  pl.BlockSpec((1, POOL_WIDTH), lambda i: (0, 0))],
        out_specs=[tile, pl.BlockSpec((len(POOL_WINDOWS), POOL_CH, POOL_CH), lambda i: (0, 0, 0)),
                   pl.BlockSpec((1, POOL_WIDTH), lambda i: (0, 0))],
        out_shape=[jax.ShapeDtypeStruct((n, POOL_WIDTH), BF16),
                   jax.ShapeDtypeStruct((len(POOL_WINDOWS), POOL_CH, POOL_CH), F32),
                   jax.ShapeDtypeStruct((1, POOL_WIDTH), F32)],
        scratch_shapes=[pltpu.VMEM((t + POOL_HALO, POOL_WIDTH), F32)],
        compiler_params=_params("arbitrary"),
    )(dmixed, dmixed, pooled, pool_w, pool_scale)


def _adamw(w, g, m, v):
    m2 = ADAM_B1 * m + (1.0 - ADAM_B1) * g
    v2 = ADAM_B2 * v + (1.0 - ADAM_B2) * (g * g)
    m_hat = m2 * (1.0 / (1.0 - ADAM_B1 ** ADAM_STEP))
    v_hat = v2 * (1.0 / (1.0 - ADAM_B2 ** ADAM_STEP))
    delta = -ADAM_LR * (m_hat / (jnp.sqrt(v_hat) + ADAM_EPS) + ADAM_WD * w)
    return delta, m2, v2


def _adam_big(recv, w, m, v, *, name):
    r, c = w.shape
    tr = _tile(r, 256, 16)

    def body(recv_ref, w_ref, m_ref, v_ref, g_ref, d_ref, m2_ref, v2_ref):
        g = recv_ref[0].astype(F32)
        for i in range(1, N_DEV):
            g = g + recv_ref[i].astype(F32)
        delta, m2, v2 = _adamw(w_ref[...], g, m_ref[...], v_ref[...])
        g_ref[...] = g
        d_ref[...] = delta
        m2_ref[...] = m2
        v2_ref[...] = v2

    tile = pl.BlockSpec((tr, c), lambda i: (i, 0))
    out = jax.ShapeDtypeStruct((r, c), F32)
    return pl.pallas_call(
        body, name=name, grid=(r // tr,),
        in_specs=[pl.BlockSpec((N_DEV, tr, c), lambda i: (0, i, 0)), tile, tile, tile],
        out_specs=[tile] * 4, out_shape=[out] * 4, compiler_params=_params("parallel"),
    )(recv, w, m, v)


def _adam_small(parts, w, m, v, *, name):
    lb0, lbn = SMALL_ROWS["hgrn_lb"]
    half = lbn // 2

    def body(parts_ref, w_ref, m_ref, v_ref, g_ref, d_ref, m2_ref, v2_ref):
        g = parts_ref[0]
        for i in range(1, N_DEV):
            g = g + parts_ref[i]
        w_ = w_ref[...]
        s0 = 1.0 / (1.0 + jnp.exp(w_[lb0 + half:lb0 + lbn] - w_[lb0:lb0 + half]))
        ga = g[lb0:lb0 + half] * s0 * (1.0 - s0)
        g = jnp.concatenate([g[:lb0], ga, -ga, g[lb0 + lbn:]], axis=0)
        delta, m2, v2 = _adamw(w_, g, m_ref[...], v_ref[...])
        g_ref[...] = g
        d_ref[...] = delta
        m2_ref[...] = m2
        v2_ref[...] = v2

    out = jax.ShapeDtypeStruct(w.shape, F32)
    return pl.pallas_call(body, name=name, out_shape=[out] * 4, compiler_params=_params())(parts, w, m, v)


def _pack_small(vals):
    pieces, at = [], 0
    for name, (row0, nrows) in SMALL_ROWS.items():
        if row0 > at:
            pieces.append(jnp.zeros((row0 - at, 128), F32))
        pieces.append(vals[name].astype(F32).reshape(nrows, 128))
        at = row0 + nrows
    if at < SMALL_TOTAL_ROWS:
        pieces.append(jnp.zeros((SMALL_TOTAL_ROWS - at, 128), F32))
    return jnp.concatenate(pieces, axis=0)


def _unpack_small(packed, shapes):
    return {name: packed[row0:row0 + nrows].reshape(shapes[name]) for name, (row0, nrows) in SMALL_ROWS.items()}


SPLIT_AXIS = dict(BIG_WEIGHTS)


def _gather_of(names, weights):
    return _Exchange([_shard_to_send(weights[k][0], SPLIT_AXIS[k]) for k in names], gather=True)


def _scatter_of(names, dfull):
    return _Exchange([_to_slots(dfull[k], SPLIT_AXIS[k]) for k in names], gather=False)


def _shard_to_send(w, axis):
    return (w.T if axis == 1 else w).astype(BF16)


def _to_slots(dw, axis):
    k, m = dw.shape
    if axis == 0:
        return dw.reshape(N_DEV, k // N_DEV, m)
    return dw.reshape(k, N_DEV, m // N_DEV).transpose(1, 0, 2)


def _from_slots(gathered, axis):
    _, r, c = gathered.shape
    return gathered.reshape(N_DEV * r, c)


def kernel(x, p, ffn1_norm, ffn1_w1, ffn1_w3, ffn1_w2, mix_norm, w_in, hgrn_lb, hgrn_onorm, w_branch_a, pool_w, pool_scale, w_branch_b, w_out, ffn2_norm, ffn2_w1, ffn2_w3, ffn2_w2, ple_norm, ple_w_gate, ple_w_proj, ple_post_norm, final_norm, loss_target, m_ffn1_norm, m_ffn1_w1, m_ffn1_w3, m_ffn1_w2, m_mix_norm, m_w_in, m_hgrn_lb, m_hgrn_onorm, m_w_branch_a, m_pool_w, m_pool_scale, m_w_branch_b, m_w_out, m_ffn2_norm, m_ffn2_w1, m_ffn2_w3, m_ffn2_w2, m_ple_norm, m_ple_w_gate, m_ple_w_proj, m_ple_post_norm, m_final_norm, v_ffn1_norm, v_ffn1_w1, v_ffn1_w3, v_ffn1_w2, v_mix_norm, v_w_in, v_hgrn_lb, v_hgrn_onorm, v_w_branch_a, v_pool_w, v_pool_scale, v_w_branch_b, v_w_out, v_ffn2_norm, v_ffn2_w1, v_ffn2_w3, v_ffn2_w2, v_ple_norm, v_ple_w_gate, v_ple_w_proj, v_ple_post_norm, v_final_norm):
    weights = dict(ffn1_norm=ffn1_norm, ffn1_w1=ffn1_w1, ffn1_w3=ffn1_w3, ffn1_w2=ffn1_w2, mix_norm=mix_norm, w_in=w_in, hgrn_lb=hgrn_lb, hgrn_onorm=hgrn_onorm, w_branch_a=w_branch_a, pool_w=pool_w, pool_scale=pool_scale, w_branch_b=w_branch_b, w_out=w_out, ffn2_norm=ffn2_norm, ffn2_w1=ffn2_w1, ffn2_w3=ffn2_w3, ffn2_w2=ffn2_w2, ple_norm=ple_norm, ple_w_gate=ple_w_gate, ple_w_proj=ple_w_proj, ple_post_norm=ple_post_norm, final_norm=final_norm)
    mom1 = dict(ffn1_norm=m_ffn1_norm, ffn1_w1=m_ffn1_w1, ffn1_w3=m_ffn1_w3, ffn1_w2=m_ffn1_w2, mix_norm=m_mix_norm, w_in=m_w_in, hgrn_lb=m_hgrn_lb, hgrn_onorm=m_hgrn_onorm, w_branch_a=m_w_branch_a, pool_w=m_pool_w, pool_scale=m_pool_scale, w_branch_b=m_w_branch_b, w_out=m_w_out, ffn2_norm=m_ffn2_norm, ffn2_w1=m_ffn2_w1, ffn2_w3=m_ffn2_w3, ffn2_w2=m_ffn2_w2, ple_norm=m_ple_norm, ple_w_gate=m_ple_w_gate, ple_w_proj=m_ple_w_proj, ple_post_norm=m_ple_post_norm, final_norm=m_final_norm)
    mom2 = dict(ffn1_norm=v_ffn1_norm, ffn1_w1=v_ffn1_w1, ffn1_w3=v_ffn1_w3, ffn1_w2=v_ffn1_w2, mix_norm=v_mix_norm, w_in=v_w_in, hgrn_lb=v_hgrn_lb, hgrn_onorm=v_hgrn_onorm, w_branch_a=v_w_branch_a, pool_w=v_pool_w, pool_scale=v_pool_scale, w_branch_b=v_w_branch_b, w_out=v_w_out, ffn2_norm=v_ffn2_norm, ffn2_w1=v_ffn2_w1, ffn2_w3=v_ffn2_w3, ffn2_w2=v_ffn2_w2, ple_norm=v_ple_norm, ple_w_gate=v_ple_w_gate, ple_w_proj=v_ple_w_proj, ple_post_norm=v_ple_post_norm, final_norm=v_final_norm)

    xs = x[0]
    ps = p[0, 0].astype(BF16)
    tgt = loss_target[0]
    n = xs.shape[0]

    g_f1, g_mix, g_on, g_f2 = ffn1_norm, mix_norm, hgrn_onorm, ffn2_norm
    g_ple, g_post, g_fin = ple_norm, ple_post_norm, final_norm.reshape(1, D_MODEL)
    lb2 = hgrn_lb
    pw, pscale = pool_w[0], pool_scale

    full = {}

    def keep(names, gathered):
        for k, g in zip(names, gathered):
            full[k] = _from_slots(g, SPLIT_AXIS[k])

    first_names = ("ffn1_w1", "ffn1_w3")
    keep(first_names, _exchange_now([_shard_to_send(weights[k][0], SPLIT_AXIS[k]) for k in first_names],
                                    name="gather_first", gather=True))
    h1 = _rms_fwd(xs, g_f1, name="ffn1_rms")
    names = ("ffn1_w2", "w_in")
    ex = _gather_of(names, weights)
    a1, b1, s1 = _ffn_up(h1, full["ffn1_w1"], full["ffn1_w3"], name="ffn1_up", exchange=ex)
    keep(names, ex.received)
    names = ("w_branch_a", "w_branch_b", "w_out")
    ex = _gather_of(names, weights)
    x1, h2 = _mm_nn_res_rms(s1, full["ffn1_w2"], xs, g_mix, name="ffn1_down", scale=0.5, exchange=ex)
    keep(names, ex.received)
    names = ("ffn2_w1", "ffn2_w3", "ffn2_w2", "ple_w_gate", "ple_w_proj")
    ex = _gather_of(names, weights)
    proj = _mm_nn_wide(h2, full["w_in"], name="w_in_proj", out_dtype=BF16, exchange=ex)
    keep(names, ex.received)
    o, states = _hgrn_fwd(proj, lb2, name="hgrn_fwd")
    on = _hgrn_post_fwd(o, proj, g_on, name="hgrn_post_fwd")
    ya = _mm_nn(on, full["w_branch_a"], name="branch_a", tn=1024, tm=512, out_dtype=BF16)
    pooled, mixed = _pool_fwd(proj, pw, pscale, name="pool_fwd")
    yb = _mm_nt([(mixed, full["w_branch_b"])], name="branch_b", tn=1024, tk=512, out_dtype=BF16)
    y = _merge_fwd(proj, ya, yb, name="merge_fwd")
    x2, h3 = _mm_nn_res_rms(y, full["w_out"], x1, g_f2, name="w_out_proj", scale=1.0)
    a2, b2, s2 = _ffn_up(h3, full["ffn2_w1"], full["ffn2_w3"], name="ffn2_up")
    x3, h4 = _mm_nn_res_rms(s2, full["ffn2_w2"], x2, g_ple, name="ffn2_down", scale=0.5)
    gpre = _mm_nn(h4, full["ple_w_gate"], name="ple_gate", tn=1024, tm=512, out_dtype=BF16)
    z = _mm_nt([(ps, full["ple_w_proj"])], name="ple_proj", tn=1024, tk=512, out_dtype=BF16)
    dx4, dpre, dz, loss_part, d_fin, d_post = _ple_final(x3, gpre, z, tgt, g_post, g_fin, name="ple_final")

    dfull, received = {}, {}

    def sent(names, exchange):
        received.update(zip(names, exchange.received))

    dfull["ple_w_proj"] = _mm_tn(ps, dz, name="d_ple_w_proj", tn=1024, tm=1024)
    dfull["ple_w_gate"] = _mm_tn(h4, dpre, name="d_ple_w_gate", tn=1024, tm=1024)
    dx3, dx3s, d_ple = _mm_nt_rms_bwd([(dpre, full["ple_w_gate"], False)], x3, dx4, g_ple, name="ple_rms_bwd", tn=512,
                                      half_scale=0.5)

    names = ("ple_w_proj", "ple_w_gate")
    ex = _scatter_of(names, dfull)
    da2, db2 = _ffn_bwd_mid(dx3s, full["ffn2_w2"], a2, b2, name="ffn2_bwd_mid", exchange=ex)
    sent(names, ex)
    dfull["ffn2_w2"] = _mm_tn(s2, dx3s, name="ffn2_dw2", tn=1024, tm=512)
    dfull["ffn2_w1"] = _mm_tn(h3, da2, name="ffn2_dw1", tn=1024, tm=1408)
    dfull["ffn2_w3"] = _mm_tn(h3, db2, name="ffn2_dw3", tn=1024, tm=1408)
    names = ("ffn2_w2", "ffn2_w1")
    ex = _scatter_of(names, dfull)
    dx2, dx2b, d_f2 = _mm_nt_rms_bwd([(da2, full["ffn2_w1"], True), (db2, full["ffn2_w3"], True)], x2, dx3, g_f2,
                                     name="ffn2_rms_bwd", tn=512, half_scale=1.0, exchange=ex)
    sent(names, ex)

    dfull["w_out"] = _mm_tn(y, dx2b, name="d_w_out", tn=1024, tm=1024)
    dy = _mm_nt([(dx2b, full["w_out"])], name="d_y", tn=1024, tk=512, out_dtype=BF16)
    dya, dyb, dga, dgb = _merge_bwd(dy, proj, ya, yb, name="merge_bwd")

    dfull["w_branch_b"] = _mm_tn(mixed, dyb, name="d_w_branch_b", tn=1024, tm=1024)
    dmixed = _mm_nn(dyb, full["w_branch_b"], name="d_mixed", tn=1024, tm=512)
    du, d_pw, d_ps = _pool_bwd(dmixed, pooled, pw, pscale, name="pool_bwd")

    dfull["w_branch_a"] = _mm_tn(on, dya, name="d_w_branch_a", tn=1024, tm=1024)
    don = _mm_nt([(dya, full["w_branch_a"])], name="d_on", tn=1024, tk=512, out_dtype=BF16)
    dog, do, d_on = _hgrn_post_bwd(don, o, proj, g_on, name="hgrn_post_bwd")
    dqfi, d_lb = _hgrn_bwd(proj, lb2, do, states, name="hgrn_bwd")
    dproj = [dqfi, dog, du, dga, dgb]
    names = ("ffn2_w3", "w_out", "w_branch_b", "w_branch_a")
    ex = _scatter_of(names, dfull)
    dx1, dx1s, d_mix = _mm_nt_rms_bwd([(dproj, full["w_in"], True)], x1, dx2, g_mix, name="mix_rms_bwd", tn=512,
                                      half_scale=0.5, exchange=ex)
    sent(names, ex)
    dfull["w_in"] = jnp.concatenate(
        [_mm_tn(h2, part, name=f"d_w_in_{j}", tn=1024, tm=1536) for j, part in enumerate(dproj)], axis=1)

    half = dfull["w_in"].shape[0] // 2
    ex_top = _Exchange([_to_slots(dfull["w_in"][:half], 1)], gather=False)
    da1, db1 = _ffn_bwd_mid(dx1s, full["ffn1_w2"], a1, b1, name="ffn1_bwd_mid", exchange=ex_top)
    ex_bottom = _Exchange([_to_slots(dfull["w_in"][half:], 1)], gather=False)
    dfull["ffn1_w2"] = _mm_tn(s1, dx1s, name="ffn1_dw2", tn=1024, tm=512, exchange=ex_bottom)
    received["w_in"] = jnp.concatenate([ex_top.received[0], ex_bottom.received[0]], axis=1)
    names = ("ffn1_w2",)
    ex = _scatter_of(names, dfull)
    dfull["ffn1_w1"] = _mm_tn(h1, da1, name="ffn1_dw1", tn=1024, tm=1408, exchange=ex)
    sent(names, ex)
    names = ("ffn1_w1",)
    ex = _scatter_of(names, dfull)
    dfull["ffn1_w3"] = _mm_tn(h1, db1, name="ffn1_dw3", tn=1024, tm=1408, exchange=ex)
    sent(names, ex)
    names = ("ffn1_w3",)
    ex = _scatter_of(names, dfull)
    grad_x, _, d_f1 = _mm_nt_rms_bwd([(da1, full["ffn1_w1"], True), (db1, full["ffn1_w3"], True)], xs, dx1, g_f1,
                                     name="ffn1_rms_bwd", tn=512, half_scale=1.0, exchange=ex)
    sent(names, ex)

    small_part = _pack_small(dict(
        ffn1_norm=d_f1, mix_norm=d_mix, hgrn_onorm=d_on, ffn2_norm=d_f2, ple_norm=d_ple, ple_post_norm=d_post,
        final_norm=d_fin, hgrn_lb=jnp.concatenate([d_lb, jnp.zeros_like(d_lb)], axis=0), pool_scale=d_ps, pool_w=d_pw))
    small_all = _exchange_now([small_part], name="gather_small_grads", gather=True)[0]

    grads, deltas, new_m, new_v = {}, {}, {}, {}
    for name, _ in BIG_WEIGHTS:
        shape, recv = weights[name].shape, received[name]
        res = _adam_big(recv, weights[name][0], mom1[name][0], mom2[name][0], name=f"adam_{name}")
        grads[name], deltas[name], new_m[name], new_v[name] = [r.reshape(shape) for r in res]
    shapes = {name: weights[name].shape for name in SMALL_ROWS}
    res = _adam_small(small_all, _pack_small(weights), _pack_small(mom1), _pack_small(mom2), name="adam_small")
    for store, packed in zip((grads, deltas, new_m, new_v), res):
        store.update(_unpack_small(packed, shapes))

    loss = lax.psum(jnp.sum(loss_part), ("x", "y", "c"))
    return (loss, grad_x.reshape(x.shape), *[grads[k] for k in WEIGHT_ORDER], *[deltas[k] for k in WEIGHT_ORDER],
            *[new_m[k] for k in WEIGHT_ORDER], *[new_v[k] for k in WEIGHT_ORDER])
```

```python
import jax
import jax.numpy as jnp
from jax import lax
from jax.experimental import pallas as pl
from jax.experimental.pallas import tpu as pltpu

F32 = jnp.float32
BF16 = jnp.bfloat16

N_DEV = 8
D_MODEL = 1024
HEADS = 8
HEAD_DIM = 128
POOL_WINDOWS = (2, 4, 8, 16)
POOL_CH = 128
POOL_WIDTH = 512
POOL_HALO = 16
RMS_EPS = 1e-6
CHUNK = 64
SUB = 32
HGRN_HEADS_PER_STEP = 8
NEG_BIG = -1e30

ADAM_LR = 0.001
ADAM_B1 = 0.9
ADAM_B2 = 0.999
ADAM_EPS = 1e-08
ADAM_WD = 0.01
ADAM_STEP = 10

V7X_VMEM_BYTES = 64 * 1024 * 1024
VMEM_LIMIT = (V7X_VMEM_BYTES * 3) // 4
ROW_TILE_CAP = 8192

COL_Q, COL_F, COL_I, COL_OG, COL_POOL, COL_GA, COL_GB = 0, 1024, 2048, 3072, 4096, 4608, 5632

BIG_WEIGHTS = (
    ("ffn1_w1", 1), ("ffn1_w3", 1), ("ffn1_w2", 0), ("w_in", 1), ("w_branch_a", 0), ("w_branch_b", 1),
    ("w_out", 0), ("ffn2_w1", 1), ("ffn2_w3", 1), ("ffn2_w2", 0), ("ple_w_gate", 0), ("ple_w_proj", 1),
)
SMALL_ROWS = {
    "ffn1_norm": (0, 8), "mix_norm": (8, 8), "hgrn_onorm": (16, 8), "ffn2_norm": (24, 8), "ple_norm": (32, 8),
    "ple_post_norm": (40, 8), "final_norm": (48, 8), "hgrn_lb": (56, 16), "pool_scale": (72, 4), "pool_w": (80, 512),
}
SMALL_TOTAL_ROWS = 592
WEIGHT_ORDER = (
    "ffn1_norm", "ffn1_w1", "ffn1_w3", "ffn1_w2", "mix_norm", "w_in", "hgrn_lb", "hgrn_onorm", "w_branch_a", "pool_w",
    "pool_scale", "w_branch_b", "w_out", "ffn2_norm", "ffn2_w1", "ffn2_w3", "ffn2_w2", "ple_norm", "ple_w_gate",
    "ple_w_proj", "ple_post_norm", "final_norm",
)


def _params(*sem):
    return pltpu.CompilerParams(dimension_semantics=sem if sem else None, vmem_limit_bytes=VMEM_LIMIT)


COL_CHUNK = 256


def _rows(tn, width):
    return pl.BlockSpec((tn, width), lambda i: (i, 0))


def _resident(shape):
    return pl.BlockSpec(shape, lambda i: (0,) * len(shape), pipeline_mode=pl.Buffered(1))


def _dot(a, b):
    return jnp.dot(a, b, preferred_element_type=F32)


def _dot_nt(a, b):
    return lax.dot_general(a, b, (((1,), (1,)), ((), ())), preferred_element_type=F32)


def _dot_tn(a, b):
    return lax.dot_general(a, b, (((0,), (0,)), ((), ())), preferred_element_type=F32)


def _sigmoid(x):
    return 0.5 * jnp.tanh(0.5 * x) + 0.5


def _tile(n, want, mult):
    if mult != 128:
        want = min(want, ROW_TILE_CAP)
    if n <= want:
        return n
    t = (want // mult) * mult
    while t > mult and n % t:
        t -= mult
    assert n % t == 0, (n, want, mult)
    return t


class _Exchange:
    COPIES = N_DEV - 1

    def __init__(self, arrs, gather):
        self.arrs, self.gather, self.n = list(arrs), gather, len(arrs)
        self.out_shape = [jax.ShapeDtypeStruct((N_DEV,) + (a.shape if gather else a.shape[1:]), a.dtype) for a in arrs]
        self.scratch = [pltpu.SemaphoreType.DMA((self.n * self.COPIES,)),
                        pltpu.SemaphoreType.DMA((self.n * self.COPIES,)), pltpu.SemaphoreType.DMA((self.n,))]
        self.received = None

    @staticmethod
    def _place():
        x, y, c = lax.axis_index("x"), lax.axis_index("y"), lax.axis_index("c")
        return x, y, c

    def _copy(self, a, k, src, dst, to, sems):
        s = a * self.COPIES + k
        return pltpu.make_async_remote_copy(src_ref=src, dst_ref=dst, send_sem=sems[0].at[s], recv_sem=sems[1].at[s],
                                            device_id=to, device_id_type=pl.DeviceIdType.MESH)

    def _gather_copies(self, ins, outs, sems):
        x, y, c = self._place()
        chips = [(1 - x, y), (x, 1 - y), (1 - x, 1 - y)]
        slot = lambda px, py, pc: 4 * px + 2 * py + pc
        first, passed, arrivals = [], [], []
        for a in range(self.n):
            mine = outs[a].at[slot(x, y, c)]
            first.append(self._copy(a, 0, ins[a], mine, (x, y, 1 - c), sems))
            arrivals.append(self._copy(a, 0, ins[a], outs[a].at[slot(x, y, 1 - c)], (x, y, 1 - c), sems))
            for j, (px, py) in enumerate(chips):
                first.append(self._copy(a, 1 + j, ins[a], mine, (px, py, c), sems))
                theirs = outs[a].at[slot(px, py, c)]
                passed.append((self._copy(a, 1 + j, ins[a], theirs, (px, py, c), sems),
                               self._copy(a, 4 + j, theirs, theirs, (x, y, 1 - c), sems)))
                arrivals.append(self._copy(a, 4 + j, ins[a], outs[a].at[slot(px, py, 1 - c)], (x, y, 1 - c), sems))
        return first, passed, arrivals

    def _scatter_copies(self, ins, outs, sems):
        x, y, c = self._place()
        me = 4 * x + 2 * y + c
        sends, arrivals = [], []
        for k in range(1, N_DEV):
            px = 1 - x if k & 4 else x
            py = 1 - y if k & 2 else y
            pc = 1 - c if k & 1 else c
            peer = 4 * px + 2 * py + pc
            for a in range(self.n):
                sends.append(self._copy(a, k - 1, ins[a].at[peer], outs[a].at[me], (px, py, pc), sems))
                arrivals.append(self._copy(a, k - 1, ins[a].at[peer], outs[a].at[peer], (px, py, pc), sems))
        return sends, arrivals

    def _local(self, ins, outs, sems):
        x, y, c = self._place()
        me = 4 * x + 2 * y + c
        return [pltpu.make_async_copy(ins[a] if self.gather else ins[a].at[me], outs[a].at[me], sems[2].at[a])
                for a in range(self.n)]

    def start(self, ins, outs, sems):
        for cp in self._local(ins, outs, sems):
            cp.start()
        sends = self._gather_copies(ins, outs, sems)[0] if self.gather else self._scatter_copies(ins, outs, sems)[0]
        for cp in sends:
            cp.start()

    def finish(self, ins, outs, sems):
        if self.gather:
            first, passed, arrivals = self._gather_copies(ins, outs, sems)
            for landed, onward in passed:
                landed.wait_recv()
                onward.start()
            sends = first + [onward for _, onward in passed]
        else:
            sends, arrivals = self._scatter_copies(ins, outs, sems)
        for cp in arrivals:
            cp.wait_recv()
        for cp in sends:
            cp.wait_send()
        for cp in self._local(ins, outs, sems):
            cp.wait()


def _call(body, *, name, grid, in_specs, out_specs, out_shape, args, semantics, scratch=(), exchange=None):
    if exchange is None:
        return pl.pallas_call(
            body, name=name, grid=grid, in_specs=in_specs, out_specs=out_specs, out_shape=out_shape,
            scratch_shapes=list(scratch), compiler_params=_params(*semantics))(*args)
    ex = exchange
    n_in, n_out, n_s = len(in_specs), len(out_specs), len(scratch)

    def wrapped(*refs):
        ins, ex_in = refs[:n_in], refs[n_in:n_in + ex.n]
        o0 = n_in + ex.n
        outs, ex_out = refs[o0:o0 + n_out], refs[o0 + n_out:o0 + n_out + ex.n]
        s0 = o0 + n_out + ex.n
        scr, sems = refs[s0:s0 + n_s], refs[s0 + n_s:]
        ids = [pl.program_id(ax) for ax in range(len(grid))]
        first = ids[0] == 0
        last = ids[0] == grid[0] - 1
        for ax in range(1, len(grid)):
            first = jnp.logical_and(first, ids[ax] == 0)
            last = jnp.logical_and(last, ids[ax] == grid[ax] - 1)

        @pl.when(first)
        def _():
            ex.start(ex_in, ex_out, sems)

        body(*ins, *outs, *scr)

        @pl.when(last)
        def _():
            ex.finish(ex_in, ex_out, sems)

    hbm = pl.BlockSpec(memory_space=pltpu.HBM)
    res = pl.pallas_call(
        wrapped, name=name, grid=grid, in_specs=list(in_specs) + [hbm] * ex.n,
        out_specs=list(out_specs) + [hbm] * ex.n, out_shape=list(out_shape) + ex.out_shape,
        scratch_shapes=list(scratch) + ex.scratch, compiler_params=_params(*(["arbitrary"] * len(grid))),
    )(*args, *ex.arrs)
    ex.received = res[n_out:]
    return res[:n_out]


def _exchange_now(arrs, *, name, gather):
    ex = _Exchange(arrs, gather)
    n = ex.n

    def body(*refs):
        ex.start(refs[:n], refs[n:2 * n], refs[2 * n:])
        ex.finish(refs[:n], refs[n:2 * n], refs[2 * n:])

    hbm = pl.BlockSpec(memory_space=pltpu.HBM)
    return pl.pallas_call(body, name=name, out_shape=ex.out_shape, in_specs=[hbm] * n, out_specs=[hbm] * n,
                          scratch_shapes=ex.scratch)(*arrs)


def _mm_nn(a, b, *, name, tn, tm, out_dtype=F32, res=None, scale=1.0, exchange=None):
    n, k = a.shape
    m = b.shape[1]
    tn, tm = _tile(n, tn, 16), _tile(m, tm, 128)

    def body(*refs):
        a_ref, b_ref = refs[0], refs[1]
        o_ref = refs[-1]
        acc = _dot(a_ref[...], b_ref[...])
        if scale != 1.0:
            acc = acc * scale
        if res is not None:
            acc = acc + refs[2][...]
        o_ref[...] = acc.astype(o_ref.dtype)

    in_specs = [pl.BlockSpec((tn, k), lambda i, j: (i, 0)), pl.BlockSpec((k, tm), lambda i, j: (0, j))]
    args = [a, b]
    if res is not None:
        in_specs.append(pl.BlockSpec((tn, tm), lambda i, j: (i, j)))
        args.append(res)
    return _call(body, name=name, grid=(n // tn, m // tm), in_specs=in_specs,
                 out_specs=[pl.BlockSpec((tn, tm), lambda i, j: (i, j))],
                 out_shape=[jax.ShapeDtypeStruct((n, m), out_dtype)], args=args, semantics=("parallel", "parallel"),
                 exchange=exchange)[0]


def _mm_nt(pairs, *, name, tn, tk, out_dtype=F32, exchange=None):
    n = pairs[0][0].shape[0]
    kk = pairs[0][1].shape[0]
    tn, tk = _tile(n, tn, 16), _tile(kk, tk, 128)
    npair = len(pairs)

    def body(*refs):
        o_ref = refs[-1]
        acc = _dot_nt(refs[0][...], refs[1][...])
        for q in range(1, npair):
            acc = acc + _dot_nt(refs[2 * q][...], refs[2 * q + 1][...])
        o_ref[...] = acc.astype(o_ref.dtype)

    in_specs, args = [], []
    for a, b in pairs:
        m = a.shape[1]
        in_specs += [pl.BlockSpec((tn, m), lambda i, j: (i, 0)), pl.BlockSpec((tk, m), lambda i, j: (j, 0))]
        args += [a, b]
    return _call(body, name=name, grid=(n // tn, kk // tk), in_specs=in_specs,
                 out_specs=[pl.BlockSpec((tn, tk), lambda i, j: (i, j))],
                 out_shape=[jax.ShapeDtypeStruct((n, kk), out_dtype)], args=args, semantics=("parallel", "parallel"),
                 exchange=exchange)[0]


def _mm_tn(a, b, *, name, tn, tm, transpose_out=False, exchange=None):
    n, k = a.shape
    m = b.shape[1]
    tn, tm = _tile(n, tn, 16), _tile(m, tm, 128)
    steps = n // tn

    def body(a_ref, b_ref, o_ref, acc):
        i = pl.program_id(1)

        @pl.when(i == 0)
        def _():
            acc[...] = jnp.zeros_like(acc)

        acc[...] += _dot_tn(a_ref[...], b_ref[...])

        @pl.when(i == steps - 1)
        def _():
            res = acc[...]
            o_ref[...] = (res.T if transpose_out else res).astype(o_ref.dtype)

    if transpose_out:
        out_spec, out_shape = pl.BlockSpec((tm, k), lambda j, i: (j, 0)), jax.ShapeDtypeStruct((m, k), BF16)
    else:
        out_spec, out_shape = pl.BlockSpec((k, tm), lambda j, i: (0, j)), jax.ShapeDtypeStruct((k, m), BF16)
    return _call(body, name=name, grid=(m // tm, steps),
                 in_specs=[pl.BlockSpec((tn, k), lambda j, i: (i, 0)), pl.BlockSpec((tn, tm), lambda j, i: (i, j))],
                 out_specs=[out_spec], out_shape=[out_shape], args=[a, b], semantics=("parallel", "arbitrary"),
                 scratch=[pltpu.VMEM((k, tm), F32)], exchange=exchange)[0]


def _ffn_up(h, w1, w3, *, name, exchange=None):
    n, k = h.shape
    m = w1.shape[0]
    tn = _tile(n, 512, 16)

    def body(h_ref, w1_ref, w3_ref, dsda_ref, dsdb_ref, s_ref):
        for c0 in range(0, m, COL_CHUNK):
            cols = slice(c0, c0 + COL_CHUNK)
            a = _dot_nt(h_ref[...], w1_ref[cols, :])
            b = _dot_nt(h_ref[...], w3_ref[cols, :])
            sg = _sigmoid(a)
            silu = a * sg
            dsda_ref[:, cols] = (b * (sg + silu * (1.0 - sg))).astype(dsda_ref.dtype)
            dsdb_ref[:, cols] = silu.astype(dsdb_ref.dtype)
            s_ref[:, cols] = (silu * b).astype(s_ref.dtype)

    ospec = _rows(tn, m)
    return _call(body, name=name, grid=(n // tn,),
                 in_specs=[_rows(tn, k), _resident(w1.shape), _resident(w3.shape)], out_specs=[ospec, ospec, ospec],
                 out_shape=[jax.ShapeDtypeStruct((n, m), BF16)] * 3,
                 args=[h, w1, w3], semantics=("parallel",), exchange=exchange)


def _mm_nn_wide(a, b, *, name, out_dtype, exchange=None):
    n, k = a.shape
    m = b.shape[0]
    tn = _tile(n, 512, 16)
    chunk = 2 * COL_CHUNK

    def body(a_ref, b_ref, o_ref):
        for c0 in range(0, m, chunk):
            cols = slice(c0, c0 + chunk)
            o_ref[:, cols] = _dot_nt(a_ref[...], b_ref[cols, :]).astype(o_ref.dtype)

    return _call(body, name=name, grid=(n // tn,), in_specs=[_rows(tn, k), _resident(b.shape)],
                 out_specs=[_rows(tn, m)], out_shape=[jax.ShapeDtypeStruct((n, m), out_dtype)], args=[a, b],
                 semantics=("parallel",), exchange=exchange)[0]


def _mm_nn_res_rms(a, b, res, g, *, name, scale, exchange=None):
    n, k = a.shape
    d = b.shape[1]
    tn = _tile(n, 512, 16)

    def body(a_ref, b_ref, r_ref, g_ref, x_ref, h_ref):
        for c0 in range(0, d, COL_CHUNK):
            cols = slice(c0, c0 + COL_CHUNK)
            x_ref[:, cols] = r_ref[:, cols] + scale * _dot(a_ref[...], b_ref[:, cols])
        x = x_ref[...]
        r = lax.rsqrt(_rowmean(x * x) + RMS_EPS)
        h_ref[...] = (x * r * g_ref[...]).astype(h_ref.dtype)

    row = _rows(tn, d)
    return _call(body, name=name, grid=(n // tn,),
                 in_specs=[_rows(tn, k), _resident(b.shape), row, pl.BlockSpec((1, d), lambda i: (0, 0))],
                 out_specs=[row, row],
                 out_shape=[jax.ShapeDtypeStruct((n, d), F32), jax.ShapeDtypeStruct((n, d), BF16)],
                 args=[a, b, res, g], semantics=("parallel",), exchange=exchange)


def _ffn_bwd_mid(dxs, w2, dsda, dsdb, *, name, exchange=None):
    n, d = dxs.shape
    m = w2.shape[0]
    tn = _tile(n, 512, 16)

    def body(dx_ref, w2_ref, dsda_ref, dsdb_ref, da_ref, db_ref):
        for c0 in range(0, m, COL_CHUNK):
            cols = slice(c0, c0 + COL_CHUNK)
            ds = _dot_nt(dx_ref[...], w2_ref[cols, :])
            da_ref[:, cols] = (ds * dsda_ref[:, cols].astype(F32)).astype(da_ref.dtype)
            db_ref[:, cols] = (ds * dsdb_ref[:, cols].astype(F32)).astype(db_ref.dtype)

    tile = _rows(tn, m)
    return _call(body, name=name, grid=(n // tn,),
                 in_specs=[_rows(tn, d), _resident(w2.shape), tile, tile], out_specs=[tile, tile],
                 out_shape=[jax.ShapeDtypeStruct((n, m), BF16), jax.ShapeDtypeStruct((n, m), BF16)],
                 args=[dxs, w2, dsda, dsdb], semantics=("parallel",), exchange=exchange)


def _rowwise(fn, *, name, n, tn, ncol, rows, vecs, outs, accs=()):
    tn = _tile(n, tn, 16)
    nr, nv, no = len(rows), len(vecs), len(outs)

    def body(*refs):
        first = pl.program_id(1) == 0
        vals = [r[...].astype(F32) for r in refs[:nr + nv]]
        res = fn(*vals)
        for ref, val in zip(refs[nr + nv:nr + nv + no], res[:no]):
            ref[...] = val.astype(ref.dtype)
        for ref, val in zip(refs[nr + nv + no:], res[no:]):
            _accumulate(ref, val, first)

    in_specs = [pl.BlockSpec((tn, w), lambda j, i, c0=c0: (i, c0 + j)) for _, w, c0 in rows]
    in_specs += [pl.BlockSpec((1, w), lambda j, i, c0=c0: (0, c0 + j)) for _, w, c0 in vecs]
    out_specs = [pl.BlockSpec((tn, w), lambda j, i: (i, j)) for _, w, _ in outs]
    out_specs += [pl.BlockSpec((1, w), lambda j, i: (0, j)) for _, w in accs]
    out_shape = [jax.ShapeDtypeStruct((n, tw), dt) for tw, _, dt in outs]
    out_shape += [jax.ShapeDtypeStruct((1, tw), F32) for tw, _ in accs]
    return pl.pallas_call(
        body, name=name, grid=(ncol, n // tn), in_specs=in_specs, out_specs=out_specs, out_shape=out_shape,
        compiler_params=_params("parallel", "arbitrary"),
    )(*[r[0] for r in rows], *[v[0] for v in vecs])


def _accumulate(ref, val, first):
    @pl.when(first)
    def _():
        ref[...] = jnp.zeros_like(ref)

    ref[...] += val


def _colsum(x):
    return jnp.sum(x, axis=0, keepdims=True)


def _rowmean(x):
    return jnp.mean(x, axis=-1, keepdims=True)


def _rms_fwd(x, g, *, name):
    def fn(x_, g_):
        r = lax.rsqrt(_rowmean(x_ * x_) + RMS_EPS)
        return (x_ * r * g_,)

    n, d = x.shape
    return _rowwise(fn, name=name, n=n, tn=512, ncol=1, rows=[(x, d, 0)], vecs=[(g, d, 0)], outs=[(d, d, BF16)])[0]


def _mm_nt_rms_bwd(pairs, x, extra, g, *, name, tn, half_scale, exchange=None):
    n, d = x.shape
    tn = _tile(n, tn, 16)
    pairs = [(list(a) if isinstance(a, (list, tuple)) else [a], b, t) for a, b, t in pairs]
    nref = sum(len(a) + 1 for a, _, _ in pairs)

    def body(*refs):
        x_ref, e_ref, g_ref, dx_ref, dxs_ref, dg_ref = refs[nref:]
        dh, at = None, 0
        for parts, _, transposed in pairs:
            b_ref = refs[at + len(parts)]
            col = 0
            for j, part in enumerate(parts):
                w = part.shape[1]
                if transposed:
                    term = _dot(refs[at + j][...], b_ref[col:col + w, :])
                else:
                    term = _dot_nt(refs[at + j][...], b_ref[:, col:col + w])
                dh = term if dh is None else dh + term
                col += w
            at += len(parts) + 1
        x_ = x_ref[...]
        r = lax.rsqrt(_rowmean(x_ * x_) + RMS_EPS)
        xh = x_ * r
        dxh = dh * g_ref[...]
        dx = e_ref[...] + r * (dxh - xh * _rowmean(dxh * xh))
        dx_ref[...] = dx
        dxs_ref[...] = (dx * half_scale).astype(dxs_ref.dtype)
        _accumulate(dg_ref, _colsum(dh * xh), pl.program_id(0) == 0)

    in_specs, args = [], []
    for parts, b, transposed in pairs:
        assert sum(part.shape[1] for part in parts) == b.shape[0 if transposed else 1]
        in_specs += [pl.BlockSpec((tn, part.shape[1]), lambda i: (i, 0)) for part in parts]
        in_specs.append(pl.BlockSpec(b.shape, lambda i: (0, 0), pipeline_mode=pl.Buffered(1)))
        args += parts + [b]
    row = pl.BlockSpec((tn, d), lambda i: (i, 0))
    vec = pl.BlockSpec((1, d), lambda i: (0, 0))
    return _call(body, name=name, grid=(n // tn,), in_specs=in_specs + [row, row, vec], out_specs=[row, row, vec],
                 out_shape=[jax.ShapeDtypeStruct((n, d), F32), jax.ShapeDtypeStruct((n, d), BF16),
                            jax.ShapeDtypeStruct((1, d), F32)],
                 args=args + [x, extra, g], semantics=("arbitrary",), exchange=exchange)


def _ple_final(x3, gpre, z, tgt, gpp, gf, *, name):
    def fn(x3_, gpre_, z_, tgt_, gpp_, gf_):
        gate = _sigmoid(gpre_)
        rz = lax.rsqrt(_rowmean(z_ * z_) + RMS_EPS)
        zh = z_ * rz
        e = zh * gpp_
        x4 = x3_ + gate * e
        r4 = lax.rsqrt(_rowmean(x4 * x4) + RMS_EPS)
        x4h = x4 * r4
        diff = x4h * gf_ - tgt_
        dout = diff * (1.0 / D_MODEL)
        dxh4 = dout * gf_
        dx4 = r4 * (dxh4 - x4h * _rowmean(dxh4 * x4h))
        dpre = dx4 * e * gate * (1.0 - gate)
        de = dx4 * gate
        dzh = de * gpp_
        dz = rz * (dzh - zh * _rowmean(dzh * zh))
        return dx4, dpre, dz, _colsum(diff * diff) * (0.5 / D_MODEL), _colsum(dout * x4h), _colsum(de * zh)

    n, d = x3.shape
    return _rowwise(fn, name=name, n=n, tn=256, ncol=1, rows=[(x3, d, 0), (gpre, d, 0), (z, d, 0), (tgt, d, 0)],
                    vecs=[(gpp, d, 0), (gf, d, 0)], outs=[(d, d, F32), (d, d, BF16), (d, d, BF16)],
                    accs=[(d, d), (d, d), (d, d)])


def _merge_fwd(proj, ya, yb, *, name):
    def fn(ga, gb, ya_, yb_):
        return (_sigmoid(ga) * ya_ + _sigmoid(gb) * yb_,)

    n = proj.shape[0]
    w = 512
    return _rowwise(fn, name=name, n=n, tn=512, ncol=D_MODEL // w,
                    rows=[(proj, w, COL_GA // w), (proj, w, COL_GB // w), (ya, w, 0), (yb, w, 0)], vecs=[],
                    outs=[(D_MODEL, w, BF16)])[0]


def _merge_bwd(dy, proj, ya, yb, *, name):
    def fn(dy_, ga, gb, ya_, yb_):
        sa, sb = _sigmoid(ga), _sigmoid(gb)
        return dy_ * sa, dy_ * sb, dy_ * ya_ * sa * (1.0 - sa), dy_ * yb_ * sb * (1.0 - sb)

    n = proj.shape[0]
    w = 512
    return _rowwise(fn, name=name, n=n, tn=512, ncol=D_MODEL // w,
                    rows=[(dy, w, 0), (proj, w, COL_GA // w), (proj, w, COL_GB // w), (ya, w, 0), (yb, w, 0)],
                    vecs=[], outs=[(D_MODEL, w, BF16)] * 4)


def _head_mean(x):
    return jnp.concatenate(
        [jnp.broadcast_to(jnp.mean(x[:, h * HEAD_DIM:(h + 1) * HEAD_DIM], axis=-1, keepdims=True),
                          (x.shape[0], HEAD_DIM)) for h in range(HEADS)], axis=1)


def _hgrn_post_fwd(o, proj, onorm, *, name):
    def fn(o_, og, gam):
        r = lax.rsqrt(_head_mean(o_ * o_) + RMS_EPS)
        return (o_ * r * gam * (og * _sigmoid(og)),)

    n = o.shape[0]
    w = D_MODEL
    return _rowwise(fn, name=name, n=n, tn=256, ncol=1, rows=[(o, w, 0), (proj, w, COL_OG // w)],
                    vecs=[(onorm, w, 0)], outs=[(D_MODEL, w, BF16)])[0]


def _hgrn_post_bwd(don, o, proj, onorm, *, name):
    def fn(don_, o_, og, gam):
        r = lax.rsqrt(_head_mean(o_ * o_) + RMS_EPS)
        oh = o_ * r
        sg = _sigmoid(og)
        dog = don_ * oh * gam * (sg * (1.0 + og * (1.0 - sg)))
        dn = don_ * (og * sg)
        doh = dn * gam
        do = r * (doh - oh * _head_mean(doh * oh))
        return dog, do, _colsum(dn * oh)

    n = o.shape[0]
    w = D_MODEL
    return _rowwise(fn, name=name, n=n, tn=256, ncol=1, rows=[(don, w, 0), (o, w, 0), (proj, w, COL_OG // w)],
                    vecs=[(onorm, w, 0)], outs=[(D_MODEL, w, BF16), (D_MODEL, w, BF16)], accs=[(D_MODEL, w)])


def _split3(x):
    hi = x.astype(BF16)
    r1 = x - hi.astype(F32)
    mid = r1.astype(BF16)
    lo = (r1 - mid.astype(F32)).astype(BF16)
    return hi, mid, lo


def _tri_sum(tri, x):
    hi, mid, lo = _split3(x)
    return _dot(tri, hi) + _dot(tri, mid) + _dot(tri, lo)


def _lower_bound(lb_ref):
    return 1.0 / (1.0 + jnp.exp(lb_ref[1:2, :] - lb_ref[0:1, :]))


def _hgrn_specs(n, t, reverse):
    nt = n // t
    width = HGRN_HEADS_PER_STEP * HEAD_DIM

    def tok(i):
        return nt - 1 - i if reverse else i

    def sec(col):
        c0 = col // width
        return pl.BlockSpec((t, width), lambda h, i: (tok(i), c0 + h))

    head_tile = pl.BlockSpec((t, width), lambda h, i: (tok(i), h))
    state = pl.BlockSpec((HGRN_HEADS_PER_STEP, t // CHUNK, HEAD_DIM, HEAD_DIM), lambda h, i: (h, tok(i), 0, 0))
    lb = pl.BlockSpec((2, width), lambda h, i: (0, h))
    return sec, head_tile, state, lb


def _hgrn_fwd(proj, hgrn_lb, *, name):
    n = proj.shape[0]
    t = _tile(n, 512, CHUNK)
    nc = t // CHUNK
    hps = HGRN_HEADS_PER_STEP
    width = hps * HEAD_DIM
    lanes = [slice(h * HEAD_DIM, (h + 1) * HEAD_DIM) for h in range(hps)]
    sec, head_tile, state, lbspec = _hgrn_specs(n, t, False)

    def body(q_ref, f_ref, i_ref, lb_ref, o_ref, st_ref, s_acc, g_s, a_s):
        @pl.when(pl.program_id(1) == 0)
        def _():
            s_acc[...] = jnp.zeros_like(s_acc)

        lb = _lower_bound(lb_ref)
        row = lax.broadcasted_iota(jnp.int32, (CHUNK, CHUNK), 0)
        col = lax.broadcasted_iota(jnp.int32, (CHUNK, CHUNK), 1)
        tril = row >= col
        trilb = jnp.where(tril, 1.0, 0.0).astype(BF16)
        rowk = lax.broadcasted_iota(jnp.int32, (CHUNK, width), 0)

        def chunk(c, carry):
            rows = pl.ds(pl.multiple_of(c * CHUNK, CHUNK), CHUNK)
            qr, fr, v = [r[rows, :].astype(F32) for r in (q_ref, f_ref, i_ref)]
            q = qr * _sigmoid(qr)
            f = lb + (1.0 - lb) * _sigmoid(fr)
            k = 1.0 - f
            g = _tri_sum(trilb, jnp.log(f))
            g_s[...] = g
            st0 = [s_acc[h] for h in range(hps)]
            for h in range(hps):
                st_ref[h, c] = st0[h]
            vb = v.astype(BF16)
            for blk in range(CHUNK // SUB):
                lo, hi = blk * SUB, (blk + 1) * SUB
                gref = g_s[lo - 1:lo, :] if blk else jnp.zeros((1, width), F32)
                qi = (q[lo:hi] * jnp.exp(g[lo:hi] - gref)).astype(BF16)
                ki = (k * jnp.exp(jnp.where(rowk < hi, gref - g, NEG_BIG))).astype(BF16)
                for h, ln in enumerate(lanes):
                    a_s[h, lo:hi, :] = _dot_nt(qi[:, ln], ki[:, ln])
            qeb = (q * jnp.exp(g)).astype(BF16)
            o_ref[rows, :] = jnp.concatenate(
                [_dot(jnp.where(tril, a_s[h], 0.0).astype(BF16), vb[:, ln]) + _dot_nt(qeb[:, ln], st0[h].astype(BF16))
                 for h, ln in enumerate(lanes)], axis=1).astype(o_ref.dtype)
            glast = g_s[CHUNK - 1:CHUNK, :]
            kdb = (k * jnp.exp(glast - g)).astype(BF16)
            dec = jnp.exp(glast)
            for h, ln in enumerate(lanes):
                s_acc[h] = st0[h] * dec[:, ln] + _dot_tn(vb[:, ln], kdb[:, ln])
            return carry

        lax.fori_loop(0, nc, chunk, 0)

    return pl.pallas_call(
        body, name=name, grid=(HEADS // hps, n // t),
        in_specs=[sec(COL_Q), sec(COL_F), sec(COL_I), lbspec], out_specs=[head_tile, state],
        out_shape=[jax.ShapeDtypeStruct((n, D_MODEL), BF16),
                   jax.ShapeDtypeStruct((HEADS, n // CHUNK, HEAD_DIM, HEAD_DIM), F32)],
        scratch_shapes=[pltpu.VMEM((hps, HEAD_DIM, HEAD_DIM), F32), pltpu.VMEM((CHUNK, width), F32),
                        pltpu.VMEM((hps, CHUNK, CHUNK), F32)],
        compiler_params=_params("parallel", "arbitrary"),
    )(proj, proj, proj, hgrn_lb)


def _hgrn_bwd(proj, hgrn_lb, do, states, *, name, exchange=None):
    n = proj.shape[0]
    t = _tile(n, 512, CHUNK)
    nc = t // CHUNK
    hps = HGRN_HEADS_PER_STEP
    width = hps * HEAD_DIM
    lanes = [slice(h * HEAD_DIM, (h + 1) * HEAD_DIM) for h in range(hps)]
    sec, head_tile, state, lbspec = _hgrn_specs(n, t, True)

    def body(q_ref, f_ref, i_ref, lb_ref, do_ref, st_ref, dqfi_ref, dlb_ref, d_acc, g_s, a_s, dq_s,
             dg_s):
        first = pl.program_id(1) == 0

        @pl.when(first)
        def _():
            d_acc[...] = jnp.zeros_like(d_acc)

        lb = _lower_bound(lb_ref)
        row = lax.broadcasted_iota(jnp.int32, (CHUNK, CHUNK), 0)
        col = lax.broadcasted_iota(jnp.int32, (CHUNK, CHUNK), 1)
        tril = row >= col
        trilb = jnp.where(tril, 1.0, 0.0).astype(BF16)
        triub = jnp.where(row <= col, 1.0, 0.0).astype(BF16)
        rowk = lax.broadcasted_iota(jnp.int32, (CHUNK, width), 0)

        def per_head(fn):
            return jnp.concatenate([fn(h, ln) for h, ln in enumerate(lanes)], axis=1)

        def chunk(j, dlb):
            c = nc - 1 - j
            rows = pl.ds(pl.multiple_of(c * CHUNK, CHUNK), CHUNK)
            qr, fr, v, dout = [r[rows, :].astype(F32) for r in (q_ref, f_ref, i_ref, do_ref)]
            sq = _sigmoid(qr)
            q = qr * sq
            sf = _sigmoid(fr)
            f = lb + (1.0 - lb) * sf
            k = 1.0 - f
            g = _tri_sum(trilb, jnp.log(f))
            g_s[...] = g
            st0 = [st_ref[h, c] for h in range(hps)]
            dt = [d_acc[h] for h in range(hps)]
            vb, dob = v.astype(BF16), dout.astype(BF16)
            dtb = [x.astype(BF16) for x in dt]
            st0b = [x.astype(BF16) for x in st0]
            glast = g_s[CHUNK - 1:CHUNK, :]
            eg = jnp.exp(g)
            kdec = jnp.exp(glast - g)
            qeb, kdb = (q * eg).astype(BF16), (k * kdec).astype(BF16)
            aps = [jnp.where(row > col, _dot_nt(dob[:, ln], vb[:, ln]), 0.0) for ln in lanes]
            dov = dout * v
            adiag = per_head(lambda h, ln: jnp.broadcast_to(
                jnp.sum(dov[:, ln], axis=-1, keepdims=True), (CHUNK, HEAD_DIM)))
            dq_inter = per_head(lambda h, ln: _dot(dob[:, ln], st0b[h]))
            dk_inter = per_head(lambda h, ln: _dot(vb[:, ln], dtb[h]))
            dk_st = kdec * dk_inter
            dg = qeb.astype(F32) * dq_inter
            dg_minus = kdb.astype(F32) * dk_inter
            dg = dg - dg_minus
            for blk in range(CHUNK // SUB):
                lo, hi = blk * SUB, (blk + 1) * SUB
                gref = g_s[lo - 1:lo, :] if blk else jnp.zeros((1, width), F32)
                qscale = jnp.exp(g[lo:hi] - gref)
                kscale = jnp.exp(jnp.where(rowk < hi, gref - g, NEG_BIG))
                qi = (q[lo:hi] * qscale).astype(BF16)
                ki = (k * kscale).astype(BF16)
                for h, ln in enumerate(lanes):
                    a_s[h, lo:hi, :] = _dot_nt(qi[:, ln], ki[:, ln])
                apb = [x[lo:hi].astype(BF16) for x in aps]
                from_k = per_head(lambda h, ln: _dot(apb[h], ki[:, ln]))
                from_q = per_head(lambda h, ln: _dot_tn(apb[h], qi[:, ln]))
                dq_s[lo:hi, :] = qscale * from_k
                dg_s[lo:hi, :] = qi.astype(F32) * from_k
                dk_st = dk_st + kscale * from_q
                dg = dg - ki.astype(F32) * from_q
            dg = dg + dg_s[...]
            dv = per_head(lambda h, ln: _dot_tn(jnp.where(tril, a_s[h], 0.0).astype(BF16), dob[:, ln])
                          + _dot_nt(kdb[:, ln], dtb[h]))
            dq_st = dq_s[...] + eg * dq_inter
            dq = dq_st + adiag * k
            dk = dk_st + adiag * q
            dec = jnp.exp(glast)
            dt_dec = [dt[h] * dec[:, ln] for h, ln in enumerate(lanes)]
            for h, ln in enumerate(lanes):
                d_acc[h] = dt_dec[h] + _dot_tn(dob[:, ln], qeb[:, ln])
            later = per_head(lambda h, ln: _colsum(dt_dec[h] * st0[h])) + _colsum(dg_minus)
            dlf = later + _tri_sum(triub, dg)
            df = dlf / f - dk
            dqfi_ref[rows, 0:width] = (dq * (sq * (1.0 + qr * (1.0 - sq)))).astype(dqfi_ref.dtype)
            dqfi_ref[rows, width:2 * width] = (df * (1.0 - lb) * sf * (1.0 - sf)).astype(dqfi_ref.dtype)
            dqfi_ref[rows, 2 * width:3 * width] = dv.astype(dqfi_ref.dtype)
            return dlb + _colsum(df * (1.0 - sf))

        dlb = lax.fori_loop(0, nc, chunk, jnp.zeros((1, width), F32))
        _accumulate(dlb_ref, dlb, first)

    assert hps == HEADS
    nt = n // t
    return _call(
        body, name=name, grid=(1, nt),
        in_specs=[sec(COL_Q), sec(COL_F), sec(COL_I), lbspec, head_tile, state],
        out_specs=[pl.BlockSpec((t, 3 * width), lambda h, i: (nt - 1 - i, 0)),
                   pl.BlockSpec((1, width), lambda h, i: (0, h))],
        out_shape=[jax.ShapeDtypeStruct((n, 3 * D_MODEL), BF16), jax.ShapeDtypeStruct((1, D_MODEL), F32)],
        args=[proj, proj, proj, hgrn_lb, do, states], semantics=("parallel", "arbitrary"),
        scratch=[pltpu.VMEM((hps, HEAD_DIM, HEAD_DIM), F32), pltpu.VMEM((CHUNK, width), F32),
                 pltpu.VMEM((hps, CHUNK, CHUNK), F32), pltpu.VMEM((CHUNK, width), F32),
                 pltpu.VMEM((CHUNK, width), F32)],
        exchange=exchange)


def _pool_fwd(proj, pool_w, pool_scale, *, name):
    n = proj.shape[0]
    t = _tile(n, 512, POOL_HALO)
    per = t // POOL_HALO
    c0 = COL_POOL // POOL_WIDTH

    def body(u_ref, halo_ref, pw_ref, ps_ref, pooled_ref, mixed_ref, ext):
        i = pl.program_id(0)
        u = u_ref[...].astype(F32)
        ext[POOL_HALO:POOL_HALO + t, :] = u
        ext[0:POOL_HALO, :] = jnp.where(i > 0, halo_ref[...].astype(F32), 0.0)
        pos = i * t + lax.broadcasted_iota(jnp.int32, (t, POOL_CH), 0) + 1
        for grp, win in enumerate(POOL_WINDOWS):
            cols = slice(grp * POOL_CH, (grp + 1) * POOL_CH)
            acc = u[:, cols]
            for j in range(1, win):
                acc = acc + ext[POOL_HALO - j:POOL_HALO - j + t, cols]
            pooled = (acc / jnp.minimum(pos, win).astype(F32) - u[:, cols]).astype(BF16)
            pooled_ref[:, cols] = pooled
            mixed_ref[:, cols] = (_dot(pooled, pw_ref[grp].astype(BF16)) * ps_ref[:, cols]).astype(BF16)

    tile = pl.BlockSpec((t, POOL_WIDTH), lambda i: (i, 0))
    return pl.pallas_call(
        body, name=name, grid=(n // t,),
        in_specs=[pl.BlockSpec((t, POOL_WIDTH), lambda i: (i, c0)),
                  pl.BlockSpec((POOL_HALO, POOL_WIDTH), lambda i: (jnp.maximum(i * per - 1, 0), c0)),
                  pl.BlockSpec((len(POOL_WINDOWS), POOL_CH, POOL_CH), lambda i: (0, 0, 0)),
                  pl.BlockSpec((1, POOL_WIDTH), lambda i: (0, 0))],
        out_specs=[tile, tile],
        out_shape=[jax.ShapeDtypeStruct((n, POOL_WIDTH), BF16), jax.ShapeDtypeStruct((n, POOL_WIDTH), BF16)],
        scratch_shapes=[pltpu.VMEM((t + POOL_HALO, POOL_WIDTH), F32)],
        compiler_params=_params("parallel"),
    )(proj, proj, pool_w, pool_scale)


def _pool_bwd(dmixed, pooled, pool_w, pool_scale, *, name):
    n = dmixed.shape[0]
    t = _tile(n, 512, POOL_HALO)
    per = t // POOL_HALO
    nb = n // t

    def body(dm_ref, dmh_ref, p_ref, pw_ref, ps_ref, du_ref, dpw_ref, dps_ref, ext):
        i = pl.program_id(0)

        @pl.when(i == 0)
        def _():
            dpw_ref[...] = jnp.zeros_like(dpw_ref)
            dps_ref[...] = jnp.zeros_like(dps_ref)

        dm, dmh = dm_ref[...], dmh_ref[...]
        pos = i * t + lax.broadcasted_iota(jnp.int32, (t, POOL_CH), 0) + 1
        for grp, win in enumerate(POOL_WINDOWS):
            cols = slice(grp * POOL_CH, (grp + 1) * POOL_CH)
            pwb = pw_ref[grp].astype(BF16)
            pb = p_ref[:, cols]
            scale = ps_ref[:, cols]
            dps_ref[:, cols] += _colsum(dm[:, cols] * _dot(pb, pwb))
            dpm = (dm[:, cols] * scale).astype(BF16)
            dpw_ref[grp] += _dot_tn(pb, dpm)
            dpool = _dot_nt(dpm, pwb)
            dpool_next = _dot_nt((dmh[:, cols] * scale).astype(BF16), pwb)
            ext[0:t, cols] = dpool / jnp.minimum(pos, win).astype(F32)
            ext[t:t + POOL_HALO, cols] = jnp.where(i < nb - 1, dpool_next * (1.0 / win), 0.0)
            acc = -dpool
            for j in range(win):
                acc = acc + ext[j:j + t, cols]
            du_ref[:, cols] = acc.astype(du_ref.dtype)

    tile = pl.BlockSpec((t, POOL_WIDTH), lambda i: (i, 0))
    return pl.pallas_call(
        body, name=name, grid=(nb,),
        in_specs=[tile, pl.BlockSpec((POOL_HALO, POOL_WIDTH), lambda i: (jnp.minimum((i + 1) * per, nb * per - 1), 0)),
                  tile, pl.BlockSpec((len(POOL_WINDOWS), POOL_CH, POOL_CH), lambda i: (0, 0, 0)),
                  pl.BlockSpec((1, POOL_WIDTH), lambda i: (0, 0))],
        out_specs=[tile, pl.BlockSpec((len(POOL_WINDOWS), POOL_CH, POOL_CH), lambda i: (0, 0, 0)),
                   pl.BlockSpec((1, POOL_WIDTH), lambda i: (0, 0))],
        out_shape=[jax.ShapeDtypeStruct((n, POOL_WIDTH), BF16),
                   jax.ShapeDtypeStruct((len(POOL_WINDOWS), POOL_CH, POOL_CH), F32),
                   jax.ShapeDtypeStruct((1, POOL_WIDTH), F32)],
        scratch_shapes=[pltpu.VMEM((t + POOL_HALO, POOL_WIDTH), F32)],
        compiler_params=_params("arbitrary"),
    )(dmixed, dmixed, pooled, pool_w, pool_scale)


def _adamw(w, g, m, v):
    m2 = ADAM_B1 * m + (1.0 - ADAM_B1) * g
    v2 = ADAM_B2 * v + (1.0 - ADAM_B2) * (g * g)
    m_hat = m2 * (1.0 / (1.0 - ADAM_B1 ** ADAM_STEP))
    v_hat = v2 * (1.0 / (1.0 - ADAM_B2 ** ADAM_STEP))
    delta = -ADAM_LR * (m_hat / (jnp.sqrt(v_hat) + ADAM_EPS) + ADAM_WD * w)
    return delta, m2, v2


def _adam_big(recv, w, m, v, *, name):
    r, c = w.shape
    tr = _tile(r, 256, 16)

    def body(recv_ref, w_ref, m_ref, v_ref, g_ref, d_ref, m2_ref, v2_ref):
        g = recv_ref[0].astype(F32)
        for i in range(1, N_DEV):
            g = g + recv_ref[i].astype(F32)
        delta, m2, v2 = _adamw(w_ref[...], g, m_ref[...], v_ref[...])
        g_ref[...] = g
        d_ref[...] = delta
        m2_ref[...] = m2
        v2_ref[...] = v2

    tile = pl.BlockSpec((tr, c), lambda i: (i, 0))
    out = jax.ShapeDtypeStruct((r, c), F32)
    return pl.pallas_call(
        body, name=name, grid=(r // tr,),
        in_specs=[pl.BlockSpec((N_DEV, tr, c), lambda i: (0, i, 0)), tile, tile, tile],
        out_specs=[tile] * 4, out_shape=[out] * 4, compiler_params=_params("parallel"),
    )(recv, w, m, v)


def _adam_small(parts, w, m, v, *, name):
    lb0, lbn = SMALL_ROWS["hgrn_lb"]
    half = lbn // 2

    def body(parts_ref, w_ref, m_ref, v_ref, g_ref, d_ref, m2_ref, v2_ref):
        g = parts_ref[0]
        for i in range(1, N_DEV):
            g = g + parts_ref[i]
        w_ = w_ref[...]
        s0 = 1.0 / (1.0 + jnp.exp(w_[lb0 + half:lb0 + lbn] - w_[lb0:lb0 + half]))
        ga = g[lb0:lb0 + half] * s0 * (1.0 - s0)
        g = jnp.concatenate([g[:lb0], ga, -ga, g[lb0 + lbn:]], axis=0)
        delta, m2, v2 = _adamw(w_, g, m_ref[...], v_ref[...])
        g_ref[...] = g
        d_ref[...] = delta
        m2_ref[...] = m2
        v2_ref[...] = v2

    out = jax.ShapeDtypeStruct(w.shape, F32)
    return pl.pallas_call(body, name=name, out_shape=[out] * 4, compiler_params=_params())(parts, w, m, v)


def _pack_small(vals):
    pieces, at = [], 0
    for name, (row0, nrows) in SMALL_ROWS.items():
        if row0 > at:
            pieces.append(jnp.zeros((row0 - at, 128), F32))
        pieces.append(vals[name].astype(F32).reshape(nrows, 128))
        at = row0 + nrows
    if at < SMALL_TOTAL_ROWS:
        pieces.append(jnp.zeros((SMALL_TOTAL_ROWS - at, 128), F32))
    return jnp.concatenate(pieces, axis=0)


def _unpack_small(packed, shapes):
    return {name: packed[row0:row0 + nrows].reshape(shapes[name]) for name, (row0, nrows) in SMALL_ROWS.items()}


SPLIT_AXIS = dict(BIG_WEIGHTS)


def _gather_of(names, weights):
    return _Exchange([_shard_to_send(weights[k][0], SPLIT_AXIS[k]) for k in names], gather=True)


def _scatter_of(names, dfull):
    return _Exchange([_to_slots(dfull[k], SPLIT_AXIS[k]) for k in names], gather=False)


def _shard_to_send(w, axis):
    return (w.T if axis == 1 else w).astype(BF16)


def _to_slots(dw, axis):
    rows, cols = dw.shape
    return dw.reshape(N_DEV, rows // N_DEV, cols)


def _from_slots(gathered, axis):
    _, r, c = gathered.shape
    return gathered.reshape(N_DEV * r, c)


def kernel(x, p, ffn1_norm, ffn1_w1, ffn1_w3, ffn1_w2, mix_norm, w_in, hgrn_lb, hgrn_onorm, w_branch_a, pool_w, pool_scale, w_branch_b, w_out, ffn2_norm, ffn2_w1, ffn2_w3, ffn2_w2, ple_norm, ple_w_gate, ple_w_proj, ple_post_norm, final_norm, loss_target, m_ffn1_norm, m_ffn1_w1, m_ffn1_w3, m_ffn1_w2, m_mix_norm, m_w_in, m_hgrn_lb, m_hgrn_onorm, m_w_branch_a, m_pool_w, m_pool_scale, m_w_branch_b, m_w_out, m_ffn2_norm, m_ffn2_w1, m_ffn2_w3, m_ffn2_w2, m_ple_norm, m_ple_w_gate, m_ple_w_proj, m_ple_post_norm, m_final_norm, v_ffn1_norm, v_ffn1_w1, v_ffn1_w3, v_ffn1_w2, v_mix_norm, v_w_in, v_hgrn_lb, v_hgrn_onorm, v_w_branch_a, v_pool_w, v_pool_scale, v_w_branch_b, v_w_out, v_ffn2_norm, v_ffn2_w1, v_ffn2_w3, v_ffn2_w2, v_ple_norm, v_ple_w_gate, v_ple_w_proj, v_ple_post_norm, v_final_norm):
    weights = dict(ffn1_norm=ffn1_norm, ffn1_w1=ffn1_w1, ffn1_w3=ffn1_w3, ffn1_w2=ffn1_w2, mix_norm=mix_norm, w_in=w_in, hgrn_lb=hgrn_lb, hgrn_onorm=hgrn_onorm, w_branch_a=w_branch_a, pool_w=pool_w, pool_scale=pool_scale, w_branch_b=w_branch_b, w_out=w_out, ffn2_norm=ffn2_norm, ffn2_w1=ffn2_w1, ffn2_w3=ffn2_w3, ffn2_w2=ffn2_w2, ple_norm=ple_norm, ple_w_gate=ple_w_gate, ple_w_proj=ple_w_proj, ple_post_norm=ple_post_norm, final_norm=final_norm)
    mom1 = dict(ffn1_norm=m_ffn1_norm, ffn1_w1=m_ffn1_w1, ffn1_w3=m_ffn1_w3, ffn1_w2=m_ffn1_w2, mix_norm=m_mix_norm, w_in=m_w_in, hgrn_lb=m_hgrn_lb, hgrn_onorm=m_hgrn_onorm, w_branch_a=m_w_branch_a, pool_w=m_pool_w, pool_scale=m_pool_scale, w_branch_b=m_w_branch_b, w_out=m_w_out, ffn2_norm=m_ffn2_norm, ffn2_w1=m_ffn2_w1, ffn2_w3=m_ffn2_w3, ffn2_w2=m_ffn2_w2, ple_norm=m_ple_norm, ple_w_gate=m_ple_w_gate, ple_w_proj=m_ple_w_proj, ple_post_norm=m_ple_post_norm, final_norm=m_final_norm)
    mom2 = dict(ffn1_norm=v_ffn1_norm, ffn1_w1=v_ffn1_w1, ffn1_w3=v_ffn1_w3, ffn1_w2=v_ffn1_w2, mix_norm=v_mix_norm, w_in=v_w_in, hgrn_lb=v_hgrn_lb, hgrn_onorm=v_hgrn_onorm, w_branch_a=v_w_branch_a, pool_w=v_pool_w, pool_scale=v_pool_scale, w_branch_b=v_w_branch_b, w_out=v_w_out, ffn2_norm=v_ffn2_norm, ffn2_w1=v_ffn2_w1, ffn2_w3=v_ffn2_w3, ffn2_w2=v_ffn2_w2, ple_norm=v_ple_norm, ple_w_gate=v_ple_w_gate, ple_w_proj=v_ple_w_proj, ple_post_norm=v_ple_post_norm, final_norm=v_final_norm)

    xs = x[0]
    ps = p[0, 0].astype(BF16)
    tgt = loss_target[0]
    n = xs.shape[0]

    g_f1, g_mix, g_on, g_f2 = ffn1_norm, mix_norm, hgrn_onorm, ffn2_norm
    g_ple, g_post, g_fin = ple_norm, ple_post_norm, final_norm.reshape(1, D_MODEL)
    lb2 = hgrn_lb
    pw, pscale = pool_w[0], pool_scale

    full = {}

    def keep(names, gathered):
        for k, g in zip(names, gathered):
            full[k] = _from_slots(g, SPLIT_AXIS[k])

    first_names = ("ffn1_w1", "ffn1_w3")
    keep(first_names, _exchange_now([_shard_to_send(weights[k][0], SPLIT_AXIS[k]) for k in first_names],
                                    name="gather_first", gather=True))
    h1 = _rms_fwd(xs, g_f1, name="ffn1_rms")
    names = ("ffn1_w2", "w_in")
    ex = _gather_of(names, weights)
    a1, b1, s1 = _ffn_up(h1, full["ffn1_w1"], full["ffn1_w3"], name="ffn1_up", exchange=ex)
    keep(names, ex.received)
    names = ("w_branch_a", "w_branch_b", "w_out")
    ex = _gather_of(names, weights)
    x1, h2 = _mm_nn_res_rms(s1, full["ffn1_w2"], xs, g_mix, name="ffn1_down", scale=0.5, exchange=ex)
    keep(names, ex.received)
    names = ("ffn2_w1", "ffn2_w3", "ffn2_w2", "ple_w_gate", "ple_w_proj")
    ex = _gather_of(names, weights)
    proj = _mm_nn_wide(h2, full["w_in"], name="w_in_proj", out_dtype=BF16, exchange=ex)
    keep(names, ex.received)
    o, states = _hgrn_fwd(proj, lb2, name="hgrn_fwd")
    on = _hgrn_post_fwd(o, proj, g_on, name="hgrn_post_fwd")
    ya = _mm_nn(on, full["w_branch_a"], name="branch_a", tn=1024, tm=512, out_dtype=BF16)
    pooled, mixed = _pool_fwd(proj, pw, pscale, name="pool_fwd")
    yb = _mm_nt([(mixed, full["w_branch_b"])], name="branch_b", tn=1024, tk=512, out_dtype=BF16)
    y = _merge_fwd(proj, ya, yb, name="merge_fwd")
    x2, h3 = _mm_nn_res_rms(y, full["w_out"], x1, g_f2, name="w_out_proj", scale=1.0)
    a2, b2, s2 = _ffn_up(h3, full["ffn2_w1"], full["ffn2_w3"], name="ffn2_up")
    x3, h4 = _mm_nn_res_rms(s2, full["ffn2_w2"], x2, g_ple, name="ffn2_down", scale=0.5)
    gpre = _mm_nn(h4, full["ple_w_gate"], name="ple_gate", tn=1024, tm=512, out_dtype=BF16)
    z = _mm_nt([(ps, full["ple_w_proj"])], name="ple_proj", tn=1024, tk=512, out_dtype=BF16)
    dx4, dpre, dz, loss_part, d_fin, d_post = _ple_final(x3, gpre, z, tgt, g_post, g_fin, name="ple_final")

    dfull, received = {}, {}

    def sent(names, exchange):
        received.update(zip(names, exchange.received))

    dfull["ple_w_proj"] = _mm_tn(ps, dz, name="d_ple_w_proj", tn=1024, tm=1024, transpose_out=True)
    dfull["ple_w_gate"] = _mm_tn(h4, dpre, name="d_ple_w_gate", tn=1024, tm=1024)
    dx3, dx3s, d_ple = _mm_nt_rms_bwd([(dpre, full["ple_w_gate"], False)], x3, dx4, g_ple, name="ple_rms_bwd", tn=512,
                                      half_scale=0.5)

    names = ("ple_w_proj", "ple_w_gate")
    ex = _scatter_of(names, dfull)
    da2, db2 = _ffn_bwd_mid(dx3s, full["ffn2_w2"], a2, b2, name="ffn2_bwd_mid", exchange=ex)
    sent(names, ex)
    dfull["ffn2_w2"] = _mm_tn(s2, dx3s, name="ffn2_dw2", tn=1024, tm=512)
    dfull["ffn2_w1"] = _mm_tn(h3, da2, name="ffn2_dw1", tn=1024, tm=1408, transpose_out=True)
    dfull["ffn2_w3"] = _mm_tn(h3, db2, name="ffn2_dw3", tn=1024, tm=1408, transpose_out=True)
    names = ("ffn2_w2", "ffn2_w1")
    ex = _scatter_of(names, dfull)
    dx2, dx2b, d_f2 = _mm_nt_rms_bwd([(da2, full["ffn2_w1"], True), (db2, full["ffn2_w3"], True)], x2, dx3, g_f2,
                                     name="ffn2_rms_bwd", tn=512, half_scale=1.0, exchange=ex)
    sent(names, ex)

    dfull["w_out"] = _mm_tn(y, dx2b, name="d_w_out", tn=1024, tm=1024)
    dy = _mm_nt([(dx2b, full["w_out"])], name="d_y", tn=1024, tk=512, out_dtype=BF16)
    dya, dyb, dga, dgb = _merge_bwd(dy, proj, ya, yb, name="merge_bwd")

    dfull["w_branch_b"] = _mm_tn(mixed, dyb, name="d_w_branch_b", tn=1024, tm=1024, transpose_out=True)
    dmixed = _mm_nn(dyb, full["w_branch_b"], name="d_mixed", tn=1024, tm=512)
    du, d_pw, d_ps = _pool_bwd(dmixed, pooled, pw, pscale, name="pool_bwd")

    dfull["w_branch_a"] = _mm_tn(on, dya, name="d_w_branch_a", tn=1024, tm=1024)
    don = _mm_nt([(dya, full["w_branch_a"])], name="d_on", tn=1024, tk=512, out_dtype=BF16)
    dog, do, d_on = _hgrn_post_bwd(don, o, proj, g_on, name="hgrn_post_bwd")
    names = ("ffn2_w3", "w_out", "w_branch_b", "w_branch_a")
    ex = _scatter_of(names, dfull)
    dqfi, d_lb = _hgrn_bwd(proj, lb2, do, states, name="hgrn_bwd", exchange=ex)
    sent(names, ex)
    dproj = [dqfi, dog, du, dga, dgb]
    dfull["w_in"] = jnp.concatenate(
        [_mm_tn(h2, part, name=f"d_w_in_{j}", tn=1024, tm=1536, transpose_out=True) for j, part in enumerate(dproj)],
        axis=0)
    names = ("w_in",)
    ex = _scatter_of(names, dfull)
    dx1, dx1s, d_mix = _mm_nt_rms_bwd([(dproj, full["w_in"], True)], x1, dx2, g_mix, name="mix_rms_bwd", tn=512,
                                      half_scale=0.5, exchange=ex)
    sent(names, ex)

    da1, db1 = _ffn_bwd_mid(dx1s, full["ffn1_w2"], a1, b1, name="ffn1_bwd_mid")
    dfull["ffn1_w2"] = _mm_tn(s1, dx1s, name="ffn1_dw2", tn=1024, tm=512)
    names = ("ffn1_w2",)
    ex = _scatter_of(names, dfull)
    dfull["ffn1_w1"] = _mm_tn(h1, da1, name="ffn1_dw1", tn=1024, tm=1408, transpose_out=True, exchange=ex)
    sent(names, ex)
    names = ("ffn1_w1",)
    ex = _scatter_of(names, dfull)
    dfull["ffn1_w3"] = _mm_tn(h1, db1, name="ffn1_dw3", tn=1024, tm=1408, transpose_out=True, exchange=ex)
    sent(names, ex)
    names = ("ffn1_w3",)
    ex = _scatter_of(names, dfull)
    grad_x, _, d_f1 = _mm_nt_rms_bwd([(da1, full["ffn1_w1"], True), (db1, full["ffn1_w3"], True)], xs, dx1, g_f1,
                                     name="ffn1_rms_bwd", tn=512, half_scale=1.0, exchange=ex)
    sent(names, ex)

    small_part = _pack_small(dict(
        ffn1_norm=d_f1, mix_norm=d_mix, hgrn_onorm=d_on, ffn2_norm=d_f2, ple_norm=d_ple, ple_post_norm=d_post,
        final_norm=d_fin, hgrn_lb=jnp.concatenate([d_lb, jnp.zeros_like(d_lb)], axis=0), pool_scale=d_ps, pool_w=d_pw))
    small_all = _exchange_now([small_part], name="gather_small_grads", gather=True)[0]

    grads, deltas, new_m, new_v = {}, {}, {}, {}
    for name, axis in BIG_WEIGHTS:
        shape, recv = weights[name].shape, received[name]
        own = [t[name][0].T if axis == 1 else t[name][0] for t in (weights, mom1, mom2)]
        res = _adam_big(recv, *own, name=f"adam_{name}")
        grads[name], deltas[name], new_m[name], new_v[name] = [(r.T if axis == 1 else r).reshape(shape) for r in res]
    shapes = {name: weights[name].shape for name in SMALL_ROWS}
    res = _adam_small(small_all, _pack_small(weights), _pack_small(mom1), _pack_small(mom2), name="adam_small")
    for store, packed in zip((grads, deltas, new_m, new_v), res):
        store.update(_unpack_small(packed, shapes))

    loss = lax.psum(jnp.sum(loss_part), ("x", "y", "c"))
    return (loss, grad_x.reshape(x.shape), *[grads[k] for k in WEIGHT_ORDER], *[deltas[k] for k in WEIGHT_ORDER],
            *[new_m[k] for k in WEIGHT_ORDER], *[new_v[k] for k in WEIGHT_ORDER])
```

```python
import jax
import jax.numpy as jnp
from jax import lax
from jax.experimental import pallas as pl
from jax.experimental.pallas import tpu as pltpu

F32 = jnp.float32
BF16 = jnp.bfloat16

N_DEV = 8
D_MODEL = 1024
HEADS = 8
HEAD_DIM = 128
POOL_WINDOWS = (2, 4, 8, 16)
POOL_CH = 128
POOL_WIDTH = 512
POOL_HALO = 16
RMS_EPS = 1e-6
CHUNK = 64
SUB = 32
HGRN_HEADS_PER_STEP = 8
NEG_BIG = -1e30

ADAM_LR = 0.001
ADAM_B1 = 0.9
ADAM_B2 = 0.999
ADAM_EPS = 1e-08
ADAM_WD = 0.01
ADAM_STEP = 10

V7X_VMEM_BYTES = 64 * 1024 * 1024
VMEM_LIMIT = (V7X_VMEM_BYTES * 3) // 4
ROW_TILE_CAP = 8192

COL_Q, COL_F, COL_I, COL_OG, COL_POOL, COL_GA, COL_GB = 0, 1024, 2048, 3072, 4096, 4608, 5632

BIG_WEIGHTS = (
    ("ffn1_w1", 1), ("ffn1_w3", 1), ("ffn1_w2", 0), ("w_in", 1), ("w_branch_a", 0), ("w_branch_b", 1),
    ("w_out", 0), ("ffn2_w1", 1), ("ffn2_w3", 1), ("ffn2_w2", 0), ("ple_w_gate", 0), ("ple_w_proj", 1),
)
SMALL_PARAMS = ("ffn1_norm", "mix_norm", "hgrn_onorm", "ffn2_norm", "ple_norm", "ple_post_norm", "final_norm",
                "hgrn_lb", "pool_scale", "pool_w")
WEIGHT_ORDER = (
    "ffn1_norm", "ffn1_w1", "ffn1_w3", "ffn1_w2", "mix_norm", "w_in", "hgrn_lb", "hgrn_onorm", "w_branch_a", "pool_w",
    "pool_scale", "w_branch_b", "w_out", "ffn2_norm", "ffn2_w1", "ffn2_w3", "ffn2_w2", "ple_norm", "ple_w_gate",
    "ple_w_proj", "ple_post_norm", "final_norm",
)


def _params(*sem):
    return pltpu.CompilerParams(dimension_semantics=sem if sem else None, vmem_limit_bytes=VMEM_LIMIT)


COL_CHUNK = 256


def _rows(tn, width):
    return pl.BlockSpec((tn, width), lambda i: (i, 0))


def _resident(shape):
    return pl.BlockSpec(shape, lambda i: (0,) * len(shape), pipeline_mode=pl.Buffered(1))


def _dot(a, b):
    return jnp.dot(a, b, preferred_element_type=F32)


def _dot_nt(a, b):
    return lax.dot_general(a, b, (((1,), (1,)), ((), ())), preferred_element_type=F32)


def _dot_tn(a, b):
    return lax.dot_general(a, b, (((0,), (0,)), ((), ())), preferred_element_type=F32)


def _sigmoid(x):
    return 0.5 * jnp.tanh(0.5 * x) + 0.5


def _tile(n, want, mult):
    if mult != 128:
        want = min(want, ROW_TILE_CAP)
    if n <= want:
        return n
    t = (want // mult) * mult
    while t > mult and n % t:
        t -= mult
    assert n % t == 0, (n, want, mult)
    return t


class _Exchange:
    COPIES = N_DEV - 1

    def __init__(self, arrs, gather):
        self.arrs, self.gather, self.n = list(arrs), gather, len(arrs)
        self.out_shape = [jax.ShapeDtypeStruct((N_DEV,) + (a.shape if gather else a.shape[1:]), a.dtype) for a in arrs]
        self.scratch = [pltpu.SemaphoreType.DMA((self.n * self.COPIES,)),
                        pltpu.SemaphoreType.DMA((self.n * self.COPIES,)), pltpu.SemaphoreType.DMA((self.n,))]
        self.received = None

    @staticmethod
    def _place():
        x, y, c = lax.axis_index("x"), lax.axis_index("y"), lax.axis_index("c")
        return x, y, c

    def _copy(self, a, k, src, dst, to, sems):
        s = a * self.COPIES + k
        return pltpu.make_async_remote_copy(src_ref=src, dst_ref=dst, send_sem=sems[0].at[s], recv_sem=sems[1].at[s],
                                            device_id=to, device_id_type=pl.DeviceIdType.MESH)

    def _gather_copies(self, ins, outs, sems):
        x, y, c = self._place()
        chips = [(1 - x, y), (x, 1 - y), (1 - x, 1 - y)]
        slot = lambda px, py, pc: 4 * px + 2 * py + pc
        first, passed, arrivals = [], [], []
        for a in range(self.n):
            mine = outs[a].at[slot(x, y, c)]
            first.append(self._copy(a, 0, ins[a], mine, (x, y, 1 - c), sems))
            arrivals.append(self._copy(a, 0, ins[a], outs[a].at[slot(x, y, 1 - c)], (x, y, 1 - c), sems))
            for j, (px, py) in enumerate(chips):
                first.append(self._copy(a, 1 + j, ins[a], mine, (px, py, c), sems))
                theirs = outs[a].at[slot(px, py, c)]
                passed.append((self._copy(a, 1 + j, ins[a], theirs, (px, py, c), sems),
                               self._copy(a, 4 + j, theirs, theirs, (x, y, 1 - c), sems)))
                arrivals.append(self._copy(a, 4 + j, ins[a], outs[a].at[slot(px, py, 1 - c)], (x, y, 1 - c), sems))
        return first, passed, arrivals

    def _scatter_copies(self, ins, outs, sems):
        x, y, c = self._place()
        me = 4 * x + 2 * y + c
        sends, arrivals = [], []
        for k in range(1, N_DEV):
            px = 1 - x if k & 4 else x
            py = 1 - y if k & 2 else y
            pc = 1 - c if k & 1 else c
            peer = 4 * px + 2 * py + pc
            for a in range(self.n):
                sends.append(self._copy(a, k - 1, ins[a].at[peer], outs[a].at[me], (px, py, pc), sems))
                arrivals.append(self._copy(a, k - 1, ins[a].at[peer], outs[a].at[peer], (px, py, pc), sems))
        return sends, arrivals

    def _local(self, ins, outs, sems):
        x, y, c = self._place()
        me = 4 * x + 2 * y + c
        return [pltpu.make_async_copy(ins[a] if self.gather else ins[a].at[me], outs[a].at[me], sems[2].at[a])
                for a in range(self.n)]

    def start(self, ins, outs, sems):
        for cp in self._local(ins, outs, sems):
            cp.start()
        sends = self._gather_copies(ins, outs, sems)[0] if self.gather else self._scatter_copies(ins, outs, sems)[0]
        for cp in sends:
            cp.start()

    def finish(self, ins, outs, sems):
        if self.gather:
            first, passed, arrivals = self._gather_copies(ins, outs, sems)
            for landed, onward in passed:
                landed.wait_recv()
                onward.start()
            sends = first + [onward for _, onward in passed]
        else:
            sends, arrivals = self._scatter_copies(ins, outs, sems)
        for cp in arrivals:
            cp.wait_recv()
        for cp in sends:
            cp.wait_send()
        for cp in self._local(ins, outs, sems):
            cp.wait()


def _call(body, *, name, grid, in_specs, out_specs, out_shape, args, semantics, scratch=(), exchange=None):
    if exchange is None:
        return pl.pallas_call(
            body, name=name, grid=grid, in_specs=in_specs, out_specs=out_specs, out_shape=out_shape,
            scratch_shapes=list(scratch), compiler_params=_params(*semantics))(*args)
    ex = exchange
    n_in, n_out, n_s = len(in_specs), len(out_specs), len(scratch)

    def wrapped(*refs):
        ins, ex_in = refs[:n_in], refs[n_in:n_in + ex.n]
        o0 = n_in + ex.n
        outs, ex_out = refs[o0:o0 + n_out], refs[o0 + n_out:o0 + n_out + ex.n]
        s0 = o0 + n_out + ex.n
        scr, sems = refs[s0:s0 + n_s], refs[s0 + n_s:]
        ids = [pl.program_id(ax) for ax in range(len(grid))]
        first = ids[0] == 0
        last = ids[0] == grid[0] - 1
        for ax in range(1, len(grid)):
            first = jnp.logical_and(first, ids[ax] == 0)
            last = jnp.logical_and(last, ids[ax] == grid[ax] - 1)

        @pl.when(first)
        def _():
            ex.start(ex_in, ex_out, sems)

        body(*ins, *outs, *scr)

        @pl.when(last)
        def _():
            ex.finish(ex_in, ex_out, sems)

    hbm = pl.BlockSpec(memory_space=pltpu.HBM)
    res = pl.pallas_call(
        wrapped, name=name, grid=grid, in_specs=list(in_specs) + [hbm] * ex.n,
        out_specs=list(out_specs) + [hbm] * ex.n, out_shape=list(out_shape) + ex.out_shape,
        scratch_shapes=list(scratch) + ex.scratch, compiler_params=_params(*(["arbitrary"] * len(grid))),
    )(*args, *ex.arrs)
    ex.received = res[n_out:]
    return res[:n_out]


def _exchange_now(arrs, *, name, gather):
    ex = _Exchange(arrs, gather)
    n = ex.n

    def body(*refs):
        ex.start(refs[:n], refs[n:2 * n], refs[2 * n:])
        ex.finish(refs[:n], refs[n:2 * n], refs[2 * n:])

    hbm = pl.BlockSpec(memory_space=pltpu.HBM)
    return pl.pallas_call(body, name=name, out_shape=ex.out_shape, in_specs=[hbm] * n, out_specs=[hbm] * n,
                          scratch_shapes=ex.scratch)(*arrs)


def _mm_nn(a, b, *, name, tn, tm, out_dtype=F32, res=None, scale=1.0, exchange=None):
    n, k = a.shape
    m = b.shape[1]
    tn, tm = _tile(n, tn, 16), _tile(m, tm, 128)

    def body(*refs):
        a_ref, b_ref = refs[0], refs[1]
        o_ref = refs[-1]
        acc = _dot(a_ref[...], b_ref[...])
        if scale != 1.0:
            acc = acc * scale
        if res is not None:
            acc = acc + refs[2][...]
        o_ref[...] = acc.astype(o_ref.dtype)

    in_specs = [pl.BlockSpec((tn, k), lambda i, j: (i, 0)), pl.BlockSpec((k, tm), lambda i, j: (0, j))]
    args = [a, b]
    if res is not None:
        in_specs.append(pl.BlockSpec((tn, tm), lambda i, j: (i, j)))
        args.append(res)
    return _call(body, name=name, grid=(n // tn, m // tm), in_specs=in_specs,
                 out_specs=[pl.BlockSpec((tn, tm), lambda i, j: (i, j))],
                 out_shape=[jax.ShapeDtypeStruct((n, m), out_dtype)], args=args, semantics=("parallel", "parallel"),
                 exchange=exchange)[0]


def _mm_nt(pairs, *, name, tn, tk, out_dtype=F32, exchange=None):
    n = pairs[0][0].shape[0]
    kk = pairs[0][1].shape[0]
    tn, tk = _tile(n, tn, 16), _tile(kk, tk, 128)
    npair = len(pairs)

    def body(*refs):
        o_ref = refs[-1]
        acc = _dot_nt(refs[0][...], refs[1][...])
        for q in range(1, npair):
            acc = acc + _dot_nt(refs[2 * q][...], refs[2 * q + 1][...])
        o_ref[...] = acc.astype(o_ref.dtype)

    in_specs, args = [], []
    for a, b in pairs:
        m = a.shape[1]
        in_specs += [pl.BlockSpec((tn, m), lambda i, j: (i, 0)), pl.BlockSpec((tk, m), lambda i, j: (j, 0))]
        args += [a, b]
    return _call(body, name=name, grid=(n // tn, kk // tk), in_specs=in_specs,
                 out_specs=[pl.BlockSpec((tn, tk), lambda i, j: (i, j))],
                 out_shape=[jax.ShapeDtypeStruct((n, kk), out_dtype)], args=args, semantics=("parallel", "parallel"),
                 exchange=exchange)[0]


def _mm_tn(a, b, *, name, tn, tm, transpose_out=False, exchange=None):
    n, k = a.shape
    m = b.shape[1]
    tn, tm = _tile(n, tn, 16), _tile(m, tm, 128)
    steps = n // tn

    def body(a_ref, b_ref, o_ref, acc):
        i = pl.program_id(1)

        @pl.when(i == 0)
        def _():
            acc[...] = jnp.zeros_like(acc)

        acc[...] += _dot_tn(a_ref[...], b_ref[...])

        @pl.when(i == steps - 1)
        def _():
            res = acc[...]
            o_ref[...] = (res.T if transpose_out else res).astype(o_ref.dtype)

    if transpose_out:
        out_spec, out_shape = pl.BlockSpec((tm, k), lambda j, i: (j, 0)), jax.ShapeDtypeStruct((m, k), BF16)
    else:
        out_spec, out_shape = pl.BlockSpec((k, tm), lambda j, i: (0, j)), jax.ShapeDtypeStruct((k, m), BF16)
    return _call(body, name=name, grid=(m // tm, steps),
                 in_specs=[pl.BlockSpec((tn, k), lambda j, i: (i, 0)), pl.BlockSpec((tn, tm), lambda j, i: (i, j))],
                 out_specs=[out_spec], out_shape=[out_shape], args=[a, b], semantics=("parallel", "arbitrary"),
                 scratch=[pltpu.VMEM((k, tm), F32)], exchange=exchange)[0]


def _ffn_up(h, w1, w3, *, name, tn=512, exchange=None):
    n, k = h.shape
    m = w1.shape[0]
    tn = _tile(n, tn, 16)

    def body(h_ref, w1_ref, w3_ref, dsda_ref, dsdb_ref, s_ref):
        for c0 in range(0, m, COL_CHUNK):
            cols = slice(c0, c0 + COL_CHUNK)
            a = _dot_nt(h_ref[...], w1_ref[cols, :])
            b = _dot_nt(h_ref[...], w3_ref[cols, :])
            sg = _sigmoid(a)
            silu = a * sg
            dsda_ref[:, cols] = (b * (sg + silu * (1.0 - sg))).astype(dsda_ref.dtype)
            dsdb_ref[:, cols] = silu.astype(dsdb_ref.dtype)
            s_ref[:, cols] = (silu * b).astype(s_ref.dtype)

    ospec = _rows(tn, m)
    return _call(body, name=name, grid=(n // tn,),
                 in_specs=[_rows(tn, k), _resident(w1.shape), _resident(w3.shape)], out_specs=[ospec, ospec, ospec],
                 out_shape=[jax.ShapeDtypeStruct((n, m), BF16)] * 3,
                 args=[h, w1, w3], semantics=("parallel",), exchange=exchange)


def _mm_nn_wide(a, b, *, name, out_dtype, exchange=None):
    n, k = a.shape
    m = b.shape[0]
    tn = _tile(n, 512, 16)
    chunk = 2 * COL_CHUNK

    def body(a_ref, b_ref, o_ref):
        for c0 in range(0, m, chunk):
            cols = slice(c0, c0 + chunk)
            o_ref[:, cols] = _dot_nt(a_ref[...], b_ref[cols, :]).astype(o_ref.dtype)

    return _call(body, name=name, grid=(n // tn,), in_specs=[_rows(tn, k), _resident(b.shape)],
                 out_specs=[_rows(tn, m)], out_shape=[jax.ShapeDtypeStruct((n, m), out_dtype)], args=[a, b],
                 semantics=("parallel",), exchange=exchange)[0]


def _mm_nn_res_rms(a, b, res, g, *, name, scale, exchange=None):
    n, k = a.shape
    d = b.shape[1]
    tn = _tile(n, 512, 16)

    def body(a_ref, b_ref, r_ref, g_ref, x_ref, h_ref):
        for c0 in range(0, d, COL_CHUNK):
            cols = slice(c0, c0 + COL_CHUNK)
            x_ref[:, cols] = r_ref[:, cols] + scale * _dot(a_ref[...], b_ref[:, cols])
        x = x_ref[...]
        r = lax.rsqrt(_rowmean(x * x) + RMS_EPS)
        h_ref[...] = (x * r * g_ref[...]).astype(h_ref.dtype)

    row = _rows(tn, d)
    return _call(body, name=name, grid=(n // tn,),
                 in_specs=[_rows(tn, k), _resident(b.shape), row, pl.BlockSpec((1, d), lambda i: (0, 0))],
                 out_specs=[row, row],
                 out_shape=[jax.ShapeDtypeStruct((n, d), F32), jax.ShapeDtypeStruct((n, d), BF16)],
                 args=[a, b, res, g], semantics=("parallel",), exchange=exchange)


def _ffn_bwd_mid(dxs, w2, dsda, dsdb, *, name, exchange=None):
    n, d = dxs.shape
    m = w2.shape[0]
    tn = _tile(n, 512, 16)

    def body(dx_ref, w2_ref, dsda_ref, dsdb_ref, da_ref, db_ref):
        for c0 in range(0, m, COL_CHUNK):
            cols = slice(c0, c0 + COL_CHUNK)
            ds = _dot_nt(dx_ref[...], w2_ref[cols, :])
            da_ref[:, cols] = (ds * dsda_ref[:, cols].astype(F32)).astype(da_ref.dtype)
            db_ref[:, cols] = (ds * dsdb_ref[:, cols].astype(F32)).astype(db_ref.dtype)

    tile = _rows(tn, m)
    return _call(body, name=name, grid=(n // tn,),
                 in_specs=[_rows(tn, d), _resident(w2.shape), tile, tile], out_specs=[tile, tile],
                 out_shape=[jax.ShapeDtypeStruct((n, m), BF16), jax.ShapeDtypeStruct((n, m), BF16)],
                 args=[dxs, w2, dsda, dsdb], semantics=("parallel",), exchange=exchange)


def _rowwise(fn, *, name, n, tn, ncol, rows, vecs, outs, accs=(), exchange=None):
    tn = _tile(n, tn, 16)
    nr, nv, no = len(rows), len(vecs), len(outs)

    def body(*refs):
        first = pl.program_id(1) == 0
        vals = [r[...].astype(F32) for r in refs[:nr + nv]]
        res = fn(*vals)
        for ref, val in zip(refs[nr + nv:nr + nv + no], res[:no]):
            ref[...] = val.astype(ref.dtype)
        for ref, val in zip(refs[nr + nv + no:], res[no:]):
            _accumulate(ref, val, first)

    in_specs = [pl.BlockSpec((tn, w), lambda j, i, c0=c0: (i, c0 + j)) for _, w, c0 in rows]
    in_specs += [pl.BlockSpec((1, w), lambda j, i, c0=c0: (0, c0 + j)) for _, w, c0 in vecs]
    out_specs = [pl.BlockSpec((tn, w), lambda j, i: (i, j)) for _, w, _ in outs]
    out_specs += [pl.BlockSpec((1, w), lambda j, i: (0, j)) for _, w in accs]
    out_shape = [jax.ShapeDtypeStruct((n, tw), dt) for tw, _, dt in outs]
    out_shape += [jax.ShapeDtypeStruct((1, tw), F32) for tw, _ in accs]
    return _call(body, name=name, grid=(ncol, n // tn), in_specs=in_specs, out_specs=out_specs, out_shape=out_shape,
                 args=[r[0] for r in rows] + [v[0] for v in vecs], semantics=("parallel", "arbitrary"),
                 exchange=exchange)


def _accumulate(ref, val, first):
    @pl.when(first)
    def _():
        ref[...] = jnp.zeros_like(ref)

    ref[...] += val


def _colsum(x):
    return jnp.sum(x, axis=0, keepdims=True)


def _rowmean(x):
    return jnp.mean(x, axis=-1, keepdims=True)


def _rms_fwd(x, g, *, name, exchange=None):
    def fn(x_, g_):
        r = lax.rsqrt(_rowmean(x_ * x_) + RMS_EPS)
        return (x_ * r * g_,)

    n, d = x.shape
    return _rowwise(fn, name=name, n=n, tn=512, ncol=1, rows=[(x, d, 0)], vecs=[(g, d, 0)], outs=[(d, d, BF16)],
                    exchange=exchange)[0]


def _mm_nt_rms_bwd(pairs, x, extra, g, *, name, tn, half_scale, exchange=None):
    n, d = x.shape
    tn = _tile(n, tn, 16)
    pairs = [(list(a) if isinstance(a, (list, tuple)) else [a], b, t) for a, b, t in pairs]
    nref = sum(len(a) + 1 for a, _, _ in pairs)

    def body(*refs):
        x_ref, e_ref, g_ref, dx_ref, dxs_ref, dg_ref = refs[nref:]
        dh, at = None, 0
        for parts, _, transposed in pairs:
            b_ref = refs[at + len(parts)]
            col = 0
            for j, part in enumerate(parts):
                w = part.shape[1]
                if transposed:
                    term = _dot(refs[at + j][...], b_ref[col:col + w, :])
                else:
                    term = _dot_nt(refs[at + j][...], b_ref[:, col:col + w])
                dh = term if dh is None else dh + term
                col += w
            at += len(parts) + 1
        x_ = x_ref[...]
        r = lax.rsqrt(_rowmean(x_ * x_) + RMS_EPS)
        xh = x_ * r
        dxh = dh * g_ref[...]
        dx = e_ref[...] + r * (dxh - xh * _rowmean(dxh * xh))
        dx_ref[...] = dx
        dxs_ref[...] = (dx * half_scale).astype(dxs_ref.dtype)
        _accumulate(dg_ref, _colsum(dh * xh), pl.program_id(0) == 0)

    in_specs, args = [], []
    for parts, b, transposed in pairs:
        assert sum(part.shape[1] for part in parts) == b.shape[0 if transposed else 1]
        in_specs += [pl.BlockSpec((tn, part.shape[1]), lambda i: (i, 0)) for part in parts]
        in_specs.append(pl.BlockSpec(b.shape, lambda i: (0, 0), pipeline_mode=pl.Buffered(1)))
        args += parts + [b]
    row = pl.BlockSpec((tn, d), lambda i: (i, 0))
    vec = pl.BlockSpec((1, d), lambda i: (0, 0))
    return _call(body, name=name, grid=(n // tn,), in_specs=in_specs + [row, row, vec], out_specs=[row, row, vec],
                 out_shape=[jax.ShapeDtypeStruct((n, d), F32), jax.ShapeDtypeStruct((n, d), BF16),
                            jax.ShapeDtypeStruct((1, d), F32)],
                 args=args + [x, extra, g], semantics=("arbitrary",), exchange=exchange)


def _ple_final(x3, gpre, z, tgt, gpp, gf, *, name):
    def fn(x3_, gpre_, z_, tgt_, gpp_, gf_):
        gate = _sigmoid(gpre_)
        rz = lax.rsqrt(_rowmean(z_ * z_) + RMS_EPS)
        zh = z_ * rz
        e = zh * gpp_
        x4 = x3_ + gate * e
        r4 = lax.rsqrt(_rowmean(x4 * x4) + RMS_EPS)
        x4h = x4 * r4
        diff = x4h * gf_ - tgt_
        dout = diff * (1.0 / D_MODEL)
        dxh4 = dout * gf_
        dx4 = r4 * (dxh4 - x4h * _rowmean(dxh4 * x4h))
        dpre = dx4 * e * gate * (1.0 - gate)
        de = dx4 * gate
        dzh = de * gpp_
        dz = rz * (dzh - zh * _rowmean(dzh * zh))
        return dx4, dpre, dz, _colsum(diff * diff) * (0.5 / D_MODEL), _colsum(dout * x4h), _colsum(de * zh)

    n, d = x3.shape
    return _rowwise(fn, name=name, n=n, tn=256, ncol=1, rows=[(x3, d, 0), (gpre, d, 0), (z, d, 0), (tgt, d, 0)],
                    vecs=[(gpp, d, 0), (gf, d, 0)], outs=[(d, d, F32), (d, d, BF16), (d, d, BF16)],
                    accs=[(d, d), (d, d), (d, d)])


def _merge_fwd(proj, ya, yb, *, name):
    def fn(ga, gb, ya_, yb_):
        return (_sigmoid(ga) * ya_ + _sigmoid(gb) * yb_,)

    n = proj.shape[0]
    w = 512
    return _rowwise(fn, name=name, n=n, tn=512, ncol=D_MODEL // w,
                    rows=[(proj, w, COL_GA // w), (proj, w, COL_GB // w), (ya, w, 0), (yb, w, 0)], vecs=[],
                    outs=[(D_MODEL, w, BF16)])[0]


def _merge_bwd(dy, proj, ya, yb, *, name):
    def fn(dy_, ga, gb, ya_, yb_):
        sa, sb = _sigmoid(ga), _sigmoid(gb)
        return dy_ * sa, dy_ * sb, dy_ * ya_ * sa * (1.0 - sa), dy_ * yb_ * sb * (1.0 - sb)

    n = proj.shape[0]
    w = 512
    return _rowwise(fn, name=name, n=n, tn=512, ncol=D_MODEL // w,
                    rows=[(dy, w, 0), (proj, w, COL_GA // w), (proj, w, COL_GB // w), (ya, w, 0), (yb, w, 0)],
                    vecs=[], outs=[(D_MODEL, w, BF16)] * 4)


def _head_mean(x):
    return jnp.concatenate(
        [jnp.broadcast_to(jnp.mean(x[:, h * HEAD_DIM:(h + 1) * HEAD_DIM], axis=-1, keepdims=True),
                          (x.shape[0], HEAD_DIM)) for h in range(HEADS)], axis=1)


def _hgrn_post_fwd(o, proj, onorm, *, name):
    def fn(o_, og, gam):
        r = lax.rsqrt(_head_mean(o_ * o_) + RMS_EPS)
        return (o_ * r * gam * (og * _sigmoid(og)),)

    n = o.shape[0]
    w = D_MODEL
    return _rowwise(fn, name=name, n=n, tn=256, ncol=1, rows=[(o, w, 0), (proj, w, COL_OG // w)],
                    vecs=[(onorm, w, 0)], outs=[(D_MODEL, w, BF16)])[0]


def _hgrn_post_bwd(don, o, proj, onorm, *, name):
    def fn(don_, o_, og, gam):
        r = lax.rsqrt(_head_mean(o_ * o_) + RMS_EPS)
        oh = o_ * r
        sg = _sigmoid(og)
        dog = don_ * oh * gam * (sg * (1.0 + og * (1.0 - sg)))
        dn = don_ * (og * sg)
        doh = dn * gam
        do = r * (doh - oh * _head_mean(doh * oh))
        return dog, do, _colsum(dn * oh)

    n = o.shape[0]
    w = D_MODEL
    return _rowwise(fn, name=name, n=n, tn=256, ncol=1, rows=[(don, w, 0), (o, w, 0), (proj, w, COL_OG // w)],
                    vecs=[(onorm, w, 0)], outs=[(D_MODEL, w, BF16), (D_MODEL, w, BF16)], accs=[(D_MODEL, w)])


def _tri_sum(tri, x):
    hi = x.astype(BF16)
    lo = (x - hi.astype(F32)).astype(BF16)
    return _dot(tri, hi) + _dot(tri, lo)


def _lower_bound(lb_ref):
    return 1.0 / (1.0 + jnp.exp(lb_ref[1:2, :] - lb_ref[0:1, :]))


def _hgrn_specs(n, t, reverse):
    nt = n // t
    width = HGRN_HEADS_PER_STEP * HEAD_DIM

    def tok(i):
        return nt - 1 - i if reverse else i

    def sec(col):
        c0 = col // width
        return pl.BlockSpec((t, width), lambda h, i: (tok(i), c0 + h))

    head_tile = pl.BlockSpec((t, width), lambda h, i: (tok(i), h))
    state = pl.BlockSpec((HGRN_HEADS_PER_STEP, t // CHUNK, HEAD_DIM, HEAD_DIM), lambda h, i: (h, tok(i), 0, 0))
    lb = pl.BlockSpec((2, width), lambda h, i: (0, h))
    return sec, head_tile, state, lb


def _hgrn_fwd(proj, hgrn_lb, *, name):
    n = proj.shape[0]
    t = _tile(n, 512, CHUNK)
    nc = t // CHUNK
    hps = HGRN_HEADS_PER_STEP
    width = hps * HEAD_DIM
    lanes = [slice(h * HEAD_DIM, (h + 1) * HEAD_DIM) for h in range(hps)]
    sec, head_tile, state, lbspec = _hgrn_specs(n, t, False)

    def body(q_ref, f_ref, i_ref, lb_ref, o_ref, st_ref, s_acc, g_s, a_s):
        @pl.when(pl.program_id(1) == 0)
        def _():
            s_acc[...] = jnp.zeros_like(s_acc)

        lb = _lower_bound(lb_ref)
        row = lax.broadcasted_iota(jnp.int32, (CHUNK, CHUNK), 0)
        col = lax.broadcasted_iota(jnp.int32, (CHUNK, CHUNK), 1)
        tril = row >= col
        trilb = jnp.where(tril, 1.0, 0.0).astype(BF16)
        rowk = lax.broadcasted_iota(jnp.int32, (CHUNK, width), 0)

        def chunk(c, carry):
            rows = pl.ds(pl.multiple_of(c * CHUNK, CHUNK), CHUNK)
            qr, fr, v = [r[rows, :].astype(F32) for r in (q_ref, f_ref, i_ref)]
            q = qr * _sigmoid(qr)
            f = lb + (1.0 - lb) * _sigmoid(fr)
            k = 1.0 - f
            g = _tri_sum(trilb, jnp.log(f))
            g_s[...] = g
            st0 = [s_acc[h] for h in range(hps)]
            for h in range(hps):
                st_ref[h, c] = st0[h]
            vb = v.astype(BF16)
            for blk in range(CHUNK // SUB):
                lo, hi = blk * SUB, (blk + 1) * SUB
                gref = g_s[lo - 1:lo, :] if blk else jnp.zeros((1, width), F32)
                qi = (q[lo:hi] * jnp.exp(g[lo:hi] - gref)).astype(BF16)
                ki = (k * jnp.exp(jnp.where(rowk < hi, gref - g, NEG_BIG))).astype(BF16)
                for h, ln in enumerate(lanes):
                    a_s[h, lo:hi, :] = _dot_nt(qi[:, ln], ki[:, ln])
            qeb = (q * jnp.exp(g)).astype(BF16)
            o_ref[rows, :] = jnp.concatenate(
                [_dot(jnp.where(tril, a_s[h], 0.0).astype(BF16), vb[:, ln]) + _dot_nt(qeb[:, ln], st0[h].astype(BF16))
                 for h, ln in enumerate(lanes)], axis=1).astype(o_ref.dtype)
            glast = g_s[CHUNK - 1:CHUNK, :]
            kdb = (k * jnp.exp(glast - g)).astype(BF16)
            dec = jnp.exp(glast)
            for h, ln in enumerate(lanes):
                s_acc[h] = st0[h] * dec[:, ln] + _dot_tn(vb[:, ln], kdb[:, ln])
            return carry

        lax.fori_loop(0, nc, chunk, 0)

    return pl.pallas_call(
        body, name=name, grid=(HEADS // hps, n // t),
        in_specs=[sec(COL_Q), sec(COL_F), sec(COL_I), lbspec], out_specs=[head_tile, state],
        out_shape=[jax.ShapeDtypeStruct((n, D_MODEL), BF16),
                   jax.ShapeDtypeStruct((HEADS, n // CHUNK, HEAD_DIM, HEAD_DIM), F32)],
        scratch_shapes=[pltpu.VMEM((hps, HEAD_DIM, HEAD_DIM), F32), pltpu.VMEM((CHUNK, width), F32),
                        pltpu.VMEM((hps, CHUNK, CHUNK), F32)],
        compiler_params=_params("parallel", "arbitrary"),
    )(proj, proj, proj, hgrn_lb)


def _hgrn_bwd(proj, hgrn_lb, do, states, *, name, exchange=None):
    n = proj.shape[0]
    t = _tile(n, 512, CHUNK)
    nc = t // CHUNK
    hps = HGRN_HEADS_PER_STEP
    width = hps * HEAD_DIM
    lanes = [slice(h * HEAD_DIM, (h + 1) * HEAD_DIM) for h in range(hps)]
    sec, head_tile, state, lbspec = _hgrn_specs(n, t, True)

    def body(q_ref, f_ref, i_ref, lb_ref, do_ref, st_ref, dqfi_ref, dlb_ref, d_acc, g_s, a_s, dq_s,
             dg_s):
        first = pl.program_id(1) == 0

        @pl.when(first)
        def _():
            d_acc[...] = jnp.zeros_like(d_acc)

        lb = _lower_bound(lb_ref)
        row = lax.broadcasted_iota(jnp.int32, (CHUNK, CHUNK), 0)
        col = lax.broadcasted_iota(jnp.int32, (CHUNK, CHUNK), 1)
        tril = row >= col
        trilb = jnp.where(tril, 1.0, 0.0).astype(BF16)
        triub = jnp.where(row <= col, 1.0, 0.0).astype(BF16)
        rowk = lax.broadcasted_iota(jnp.int32, (CHUNK, width), 0)

        def per_head(fn):
            return jnp.concatenate([fn(h, ln) for h, ln in enumerate(lanes)], axis=1)

        def chunk(j, dlb):
            c = nc - 1 - j
            rows = pl.ds(pl.multiple_of(c * CHUNK, CHUNK), CHUNK)
            qr, fr, v, dout = [r[rows, :].astype(F32) for r in (q_ref, f_ref, i_ref, do_ref)]
            sq = _sigmoid(qr)
            q = qr * sq
            sf = _sigmoid(fr)
            f = lb + (1.0 - lb) * sf
            k = 1.0 - f
            g = _tri_sum(trilb, jnp.log(f))
            g_s[...] = g
            st0 = [st_ref[h, c] for h in range(hps)]
            dt = [d_acc[h] for h in range(hps)]
            vb, dob = v.astype(BF16), dout.astype(BF16)
            dtb = [x.astype(BF16) for x in dt]
            st0b = [x.astype(BF16) for x in st0]
            glast = g_s[CHUNK - 1:CHUNK, :]
            eg = jnp.exp(g)
            kdec = jnp.exp(glast - g)
            qeb, kdb = (q * eg).astype(BF16), (k * kdec).astype(BF16)
            aps = [jnp.where(row > col, _dot_nt(dob[:, ln], vb[:, ln]), 0.0) for ln in lanes]
            dov = dout * v
            adiag = per_head(lambda h, ln: jnp.broadcast_to(
                jnp.sum(dov[:, ln], axis=-1, keepdims=True), (CHUNK, HEAD_DIM)))
            dq_inter = per_head(lambda h, ln: _dot(dob[:, ln], st0b[h]))
            dk_inter = per_head(lambda h, ln: _dot(vb[:, ln], dtb[h]))
            dk_st = kdec * dk_inter
            dg = qeb.astype(F32) * dq_inter
            dg_minus = kdb.astype(F32) * dk_inter
            dg = dg - dg_minus
            for blk in range(CHUNK // SUB):
                lo, hi = blk * SUB, (blk + 1) * SUB
                gref = g_s[lo - 1:lo, :] if blk else jnp.zeros((1, width), F32)
                qscale = jnp.exp(g[lo:hi] - gref)
                kscale = jnp.exp(jnp.where(rowk < hi, gref - g, NEG_BIG))
                qi = (q[lo:hi] * qscale).astype(BF16)
                ki = (k * kscale).astype(BF16)
                for h, ln in enumerate(lanes):
                    a_s[h, lo:hi, :] = _dot_nt(qi[:, ln], ki[:, ln])
                apb = [x[lo:hi].astype(BF16) for x in aps]
                from_k = per_head(lambda h, ln: _dot(apb[h], ki[:, ln]))
                from_q = per_head(lambda h, ln: _dot_tn(apb[h], qi[:, ln]))
                dq_s[lo:hi, :] = qscale * from_k
                dg_s[lo:hi, :] = qi.astype(F32) * from_k
                dk_st = dk_st + kscale * from_q
                dg = dg - ki.astype(F32) * from_q
            dg = dg + dg_s[...]
            dv = per_head(lambda h, ln: _dot_tn(jnp.where(tril, a_s[h], 0.0).astype(BF16), dob[:, ln])
                          + _dot_nt(kdb[:, ln], dtb[h]))
            dq_st = dq_s[...] + eg * dq_inter
            dq = dq_st + adiag * k
            dk = dk_st + adiag * q
            dec = jnp.exp(glast)
            dt_dec = [dt[h] * dec[:, ln] for h, ln in enumerate(lanes)]
            for h, ln in enumerate(lanes):
                d_acc[h] = dt_dec[h] + _dot_tn(dob[:, ln], qeb[:, ln])
            later = per_head(lambda h, ln: _colsum(dt_dec[h] * st0[h])) + _colsum(dg_minus)
            dlf = later + _tri_sum(triub, dg)
            df = dlf / f - dk
            dqfi_ref[rows, 0:width] = (dq * (sq * (1.0 + qr * (1.0 - sq)))).astype(dqfi_ref.dtype)
            dqfi_ref[rows, width:2 * width] = (df * (1.0 - lb) * sf * (1.0 - sf)).astype(dqfi_ref.dtype)
            dqfi_ref[rows, 2 * width:3 * width] = dv.astype(dqfi_ref.dtype)
            return dlb + _colsum(df * (1.0 - sf))

        dlb = lax.fori_loop(0, nc, chunk, jnp.zeros((1, width), F32))
        _accumulate(dlb_ref, dlb, first)

    assert hps == HEADS
    nt = n // t
    return _call(
        body, name=name, grid=(1, nt),
        in_specs=[sec(COL_Q), sec(COL_F), sec(COL_I), lbspec, head_tile, state],
        out_specs=[pl.BlockSpec((t, 3 * width), lambda h, i: (nt - 1 - i, 0)),
                   pl.BlockSpec((1, width), lambda h, i: (0, h))],
        out_shape=[jax.ShapeDtypeStruct((n, 3 * D_MODEL), BF16), jax.ShapeDtypeStruct((1, D_MODEL), F32)],
        args=[proj, proj, proj, hgrn_lb, do, states], semantics=("parallel", "arbitrary"),
        scratch=[pltpu.VMEM((hps, HEAD_DIM, HEAD_DIM), F32), pltpu.VMEM((CHUNK, width), F32),
                 pltpu.VMEM((hps, CHUNK, CHUNK), F32), pltpu.VMEM((CHUNK, width), F32),
                 pltpu.VMEM((CHUNK, width), F32)],
        exchange=exchange)


def _pool_fwd(proj, pool_w, pool_scale, *, name):
    n = proj.shape[0]
    t = _tile(n, 512, POOL_HALO)
    per = t // POOL_HALO
    c0 = COL_POOL // POOL_WIDTH

    def body(u_ref, halo_ref, pw_ref, ps_ref, pooled_ref, mixed_ref, ext):
        i = pl.program_id(0)
        u = u_ref[...].astype(F32)
        ext[POOL_HALO:POOL_HALO + t, :] = u
        ext[0:POOL_HALO, :] = jnp.where(i > 0, halo_ref[...].astype(F32), 0.0)
        pos = i * t + lax.broadcasted_iota(jnp.int32, (t, POOL_CH), 0) + 1
        for grp, win in enumerate(POOL_WINDOWS):
            cols = slice(grp * POOL_CH, (grp + 1) * POOL_CH)
            acc = u[:, cols]
            for j in range(1, win):
                acc = acc + ext[POOL_HALO - j:POOL_HALO - j + t, cols]
            pooled = (acc / jnp.minimum(pos, win).astype(F32) - u[:, cols]).astype(BF16)
            pooled_ref[:, cols] = pooled
            mixed_ref[:, cols] = (_dot(pooled, pw_ref[grp].astype(BF16)) * ps_ref[:, cols]).astype(BF16)

    tile = pl.BlockSpec((t, POOL_WIDTH), lambda i: (i, 0))
    return pl.pallas_call(
        body, name=name, grid=(n // t,),
        in_specs=[pl.BlockSpec((t, POOL_WIDTH), lambda i: (i, c0)),
                  pl.BlockSpec((POOL_HALO, POOL_WIDTH), lambda i: (jnp.maximum(i * per - 1, 0), c0)),
                  pl.BlockSpec((len(POOL_WINDOWS), POOL_CH, POOL_CH), lambda i: (0, 0, 0)),
                  pl.BlockSpec((1, POOL_WIDTH), lambda i: (0, 0))],
        out_specs=[tile, tile],
        out_shape=[jax.ShapeDtypeStruct((n, POOL_WIDTH), BF16), jax.ShapeDtypeStruct((n, POOL_WIDTH), BF16)],
        scratch_shapes=[pltpu.VMEM((t + POOL_HALO, POOL_WIDTH), F32)],
        compiler_params=_params("parallel"),
    )(proj, proj, pool_w, pool_scale)


def _pool_bwd(dmixed, pooled, pool_w, pool_scale, *, name):
    n = dmixed.shape[0]
    t = _tile(n, 512, POOL_HALO)
    per = t // POOL_HALO
    nb = n // t

    def body(dm_ref, dmh_ref, p_ref, pw_ref, ps_ref, du_ref, dpw_ref, dps_ref, ext):
        i = pl.program_id(0)

        @pl.when(i == 0)
        def _():
            dpw_ref[...] = jnp.zeros_like(dpw_ref)
            dps_ref[...] = jnp.zeros_like(dps_ref)

        dm, dmh = dm_ref[...], dmh_ref[...]
        pos = i * t + lax.broadcasted_iota(jnp.int32, (t, POOL_CH), 0) + 1
        for grp, win in enumerate(POOL_WINDOWS):
            cols = slice(grp * POOL_CH, (grp + 1) * POOL_CH)
            pwb = pw_ref[grp].astype(BF16)
            pb = p_ref[:, cols]
            scale = ps_ref[:, cols]
            dps_ref[:, cols] += _colsum(dm[:, cols] * _dot(pb, pwb))
            dpm = (dm[:, cols] * scale).astype(BF16)
            dpw_ref[grp] += _dot_tn(pb, dpm)
            dpool = _dot_nt(dpm, pwb)
            dpool_next = _dot_nt((dmh[:, cols] * scale).astype(BF16), pwb)
            ext[0:t, cols] = dpool / jnp.minimum(pos, win).astype(F32)
            ext[t:t + POOL_HALO, cols] = jnp.where(i < nb - 1, dpool_next * (1.0 / win), 0.0)
            acc = -dpool
            for j in range(win):
                acc = acc + ext[j:j + t, cols]
            du_ref[:, cols] = acc.astype(du_ref.dtype)

    tile = pl.BlockSpec((t, POOL_WIDTH), lambda i: (i, 0))
    return pl.pallas_call(
        body, name=name, grid=(nb,),
        in_specs=[tile, pl.BlockSpec((POOL_HALO, POOL_WIDTH), lambda i: (jnp.minimum((i + 1) * per, nb * per - 1), 0)),
                  tile, pl.BlockSpec((len(POOL_WINDOWS), POOL_CH, POOL_CH), lambda i: (0, 0, 0)),
                  pl.BlockSpec((1, POOL_WIDTH), lambda i: (0, 0))],
        out_specs=[tile, pl.BlockSpec((len(POOL_WINDOWS), POOL_CH, POOL_CH), lambda i: (0, 0, 0)),
                   pl.BlockSpec((1, POOL_WIDTH), lambda i: (0, 0))],
        out_shape=[jax.ShapeDtypeStruct((n, POOL_WIDTH), BF16),
                   jax.ShapeDtypeStruct((len(POOL_WINDOWS), POOL_CH, POOL_CH), F32),
                   jax.ShapeDtypeStruct((1, POOL_WIDTH), F32)],
        scratch_shapes=[pltpu.VMEM((t + POOL_HALO, POOL_WIDTH), F32)],
        compiler_params=_params("arbitrary"),
    )(dmixed, dmixed, pooled, pool_w, pool_scale)


def _adamw(w, g, m, v):
    m2 = ADAM_B1 * m + (1.0 - ADAM_B1) * g
    v2 = ADAM_B2 * v + (1.0 - ADAM_B2) * (g * g)
    m_hat = m2 * (1.0 / (1.0 - ADAM_B1 ** ADAM_STEP))
    v_hat = v2 * (1.0 / (1.0 - ADAM_B2 ** ADAM_STEP))
    delta = -ADAM_LR * (m_hat / (jnp.sqrt(v_hat) + ADAM_EPS) + ADAM_WD * w)
    return delta, m2, v2


def _adam_big(recv, w, m, v, *, name):
    r, c = w.shape
    tr = _tile(r, 256, 16)

    def body(recv_ref, w_ref, m_ref, v_ref, g_ref, d_ref, m2_ref, v2_ref):
        g = recv_ref[0].astype(F32)
        for i in range(1, N_DEV):
            g = g + recv_ref[i].astype(F32)
        delta, m2, v2 = _adamw(w_ref[...], g, m_ref[...], v_ref[...])
        g_ref[...] = g
        d_ref[...] = delta
        m2_ref[...] = m2
        v2_ref[...] = v2

    tile = pl.BlockSpec((tr, c), lambda i: (i, 0))
    out = jax.ShapeDtypeStruct((r, c), F32)
    return pl.pallas_call(
        body, name=name, grid=(r // tr,),
        in_specs=[pl.BlockSpec((N_DEV, tr, c), lambda i: (0, i, 0)), tile, tile, tile],
        out_specs=[tile] * 4, out_shape=[out] * 4, compiler_params=_params("parallel"),
    )(recv, w, m, v)


def _adam_small(parts, w, m, v, *, name):
    n = len(SMALL_PARAMS)

    def body(*refs):
        parts_r, w_r, m_r, v_r = (refs[i * n:(i + 1) * n] for i in range(4))
        outs = refs[4 * n:]
        for j, key in enumerate(SMALL_PARAMS):
            g = parts_r[j][0]
            for i in range(1, N_DEV):
                g = g + parts_r[j][i]
            w_ = w_r[j][...]
            if key == "hgrn_lb":
                s0 = 1.0 / (1.0 + jnp.exp(w_[1:2] - w_[0:1]))
                ga = g * s0 * (1.0 - s0)
                sign = jnp.where(lax.broadcasted_iota(jnp.int32, w_.shape, 0) == 0, 1.0, -1.0)
                g = sign * jnp.broadcast_to(ga, w_.shape)
            delta, m2, v2 = _adamw(w_, g, m_r[j][...], v_r[j][...])
            for q, val in enumerate((g, delta, m2, v2)):
                outs[q * n + j][...] = val

    out_shape = [jax.ShapeDtypeStruct(w[k].shape, F32) for _ in range(4) for k in SMALL_PARAMS]
    res = pl.pallas_call(body, name=name, out_shape=out_shape, compiler_params=_params())(
        *[t[k] for t in (parts, w, m, v) for k in SMALL_PARAMS])
    return [res[q * n:(q + 1) * n] for q in range(4)]


def _as_2d(a):
    return a.reshape(-1, a.shape[-1])


SPLIT_AXIS = dict(BIG_WEIGHTS)


def _gather_of(names, weights):
    return _Exchange([_shard_to_send(weights[k][0], SPLIT_AXIS[k]) for k in names], gather=True)


def _scatter_of(names, dfull):
    return _Exchange([_to_slots(dfull[k], SPLIT_AXIS[k]) for k in names], gather=False)


def _shard_to_send(w, axis):
    return (w.T if axis == 1 else w).astype(BF16)


def _to_slots(dw, axis):
    rows, cols = dw.shape
    return dw.reshape(N_DEV, rows // N_DEV, cols)


def _from_slots(gathered, axis):
    _, r, c = gathered.shape
    return gathered.reshape(N_DEV * r, c)


def kernel(x, p, ffn1_norm, ffn1_w1, ffn1_w3, ffn1_w2, mix_norm, w_in, hgrn_lb, hgrn_onorm, w_branch_a, pool_w, pool_scale, w_branch_b, w_out, ffn2_norm, ffn2_w1, ffn2_w3, ffn2_w2, ple_norm, ple_w_gate, ple_w_proj, ple_post_norm, final_norm, loss_target, m_ffn1_norm, m_ffn1_w1, m_ffn1_w3, m_ffn1_w2, m_mix_norm, m_w_in, m_hgrn_lb, m_hgrn_onorm, m_w_branch_a, m_pool_w, m_pool_scale, m_w_branch_b, m_w_out, m_ffn2_norm, m_ffn2_w1, m_ffn2_w3, m_ffn2_w2, m_ple_norm, m_ple_w_gate, m_ple_w_proj, m_ple_post_norm, m_final_norm, v_ffn1_norm, v_ffn1_w1, v_ffn1_w3, v_ffn1_w2, v_mix_norm, v_w_in, v_hgrn_lb, v_hgrn_onorm, v_w_branch_a, v_pool_w, v_pool_scale, v_w_branch_b, v_w_out, v_ffn2_norm, v_ffn2_w1, v_ffn2_w3, v_ffn2_w2, v_ple_norm, v_ple_w_gate, v_ple_w_proj, v_ple_post_norm, v_final_norm):
    weights = dict(ffn1_norm=ffn1_norm, ffn1_w1=ffn1_w1, ffn1_w3=ffn1_w3, ffn1_w2=ffn1_w2, mix_norm=mix_norm, w_in=w_in, hgrn_lb=hgrn_lb, hgrn_onorm=hgrn_onorm, w_branch_a=w_branch_a, pool_w=pool_w, pool_scale=pool_scale, w_branch_b=w_branch_b, w_out=w_out, ffn2_norm=ffn2_norm, ffn2_w1=ffn2_w1, ffn2_w3=ffn2_w3, ffn2_w2=ffn2_w2, ple_norm=ple_norm, ple_w_gate=ple_w_gate, ple_w_proj=ple_w_proj, ple_post_norm=ple_post_norm, final_norm=final_norm)
    mom1 = dict(ffn1_norm=m_ffn1_norm, ffn1_w1=m_ffn1_w1, ffn1_w3=m_ffn1_w3, ffn1_w2=m_ffn1_w2, mix_norm=m_mix_norm, w_in=m_w_in, hgrn_lb=m_hgrn_lb, hgrn_onorm=m_hgrn_onorm, w_branch_a=m_w_branch_a, pool_w=m_pool_w, pool_scale=m_pool_scale, w_branch_b=m_w_branch_b, w_out=m_w_out, ffn2_norm=m_ffn2_norm, ffn2_w1=m_ffn2_w1, ffn2_w3=m_ffn2_w3, ffn2_w2=m_ffn2_w2, ple_norm=m_ple_norm, ple_w_gate=m_ple_w_gate, ple_w_proj=m_ple_w_proj, ple_post_norm=m_ple_post_norm, final_norm=m_final_norm)
    mom2 = dict(ffn1_norm=v_ffn1_norm, ffn1_w1=v_ffn1_w1, ffn1_w3=v_ffn1_w3, ffn1_w2=v_ffn1_w2, mix_norm=v_mix_norm, w_in=v_w_in, hgrn_lb=v_hgrn_lb, hgrn_onorm=v_hgrn_onorm, w_branch_a=v_w_branch_a, pool_w=v_pool_w, pool_scale=v_pool_scale, w_branch_b=v_w_branch_b, w_out=v_w_out, ffn2_norm=v_ffn2_norm, ffn2_w1=v_ffn2_w1, ffn2_w3=v_ffn2_w3, ffn2_w2=v_ffn2_w2, ple_norm=v_ple_norm, ple_w_gate=v_ple_w_gate, ple_w_proj=v_ple_w_proj, ple_post_norm=v_ple_post_norm, final_norm=v_final_norm)

    xs = x[0]
    ps = p[0, 0].astype(BF16)
    tgt = loss_target[0]
    n = xs.shape[0]

    g_f1, g_mix, g_on, g_f2 = ffn1_norm, mix_norm, hgrn_onorm, ffn2_norm
    g_ple, g_post, g_fin = ple_norm, ple_post_norm, final_norm.reshape(1, D_MODEL)
    lb2 = hgrn_lb
    pw, pscale = pool_w[0], pool_scale

    full = {}

    def keep(names, gathered):
        for k, g in zip(names, gathered):
            full[k] = _from_slots(g, SPLIT_AXIS[k])

    names = ("ffn1_w1", "ffn1_w3")
    ex = _gather_of(names, weights)
    h1 = _rms_fwd(xs, g_f1, name="ffn1_rms", exchange=ex)
    keep(names, ex.received)
    names = ("ffn1_w2", "w_in")
    ex = _gather_of(names, weights)
    a1, b1, s1 = _ffn_up(h1, full["ffn1_w1"], full["ffn1_w3"], name="ffn1_up", exchange=ex)
    keep(names, ex.received)
    names = ("w_branch_a", "w_branch_b", "w_out")
    ex = _gather_of(names, weights)
    x1, h2 = _mm_nn_res_rms(s1, full["ffn1_w2"], xs, g_mix, name="ffn1_down", scale=0.5, exchange=ex)
    keep(names, ex.received)
    names = ("ffn2_w1", "ffn2_w3", "ffn2_w2", "ple_w_gate", "ple_w_proj")
    ex = _gather_of(names, weights)
    proj = _mm_nn_wide(h2, full["w_in"], name="w_in_proj", out_dtype=BF16, exchange=ex)
    keep(names, ex.received)
    o, states = _hgrn_fwd(proj, lb2, name="hgrn_fwd")
    on = _hgrn_post_fwd(o, proj, g_on, name="hgrn_post_fwd")
    ya = _mm_nn(on, full["w_branch_a"], name="branch_a", tn=1024, tm=512, out_dtype=BF16)
    pooled, mixed = _pool_fwd(proj, pw, pscale, name="pool_fwd")
    yb = _mm_nt([(mixed, full["w_branch_b"])], name="branch_b", tn=1024, tk=512, out_dtype=BF16)
    y = _merge_fwd(proj, ya, yb, name="merge_fwd")
    x2, h3 = _mm_nn_res_rms(y, full["w_out"], x1, g_f2, name="w_out_proj", scale=1.0)
    a2, b2, s2 = _ffn_up(h3, full["ffn2_w1"], full["ffn2_w3"], name="ffn2_up", tn=256)
    x3, h4 = _mm_nn_res_rms(s2, full["ffn2_w2"], x2, g_ple, name="ffn2_down", scale=0.5)
    gpre = _mm_nn(h4, full["ple_w_gate"], name="ple_gate", tn=1024, tm=512, out_dtype=BF16)
    z = _mm_nt([(ps, full["ple_w_proj"])], name="ple_proj", tn=1024, tk=512, out_dtype=BF16)
    dx4, dpre, dz, loss_part, d_fin, d_post = _ple_final(x3, gpre, z, tgt, g_post, g_fin, name="ple_final")

    dfull, received = {}, {}

    def sent(names, exchange):
        received.update(zip(names, exchange.received))

    dfull["ple_w_proj"] = _mm_tn(ps, dz, name="d_ple_w_proj", tn=1024, tm=1024, transpose_out=True)
    dfull["ple_w_gate"] = _mm_tn(h4, dpre, name="d_ple_w_gate", tn=1024, tm=1024)
    dx3, dx3s, d_ple = _mm_nt_rms_bwd([(dpre, full["ple_w_gate"], False)], x3, dx4, g_ple, name="ple_rms_bwd", tn=512,
                                      half_scale=0.5)

    names = ("ple_w_proj", "ple_w_gate")
    ex = _scatter_of(names, dfull)
    da2, db2 = _ffn_bwd_mid(dx3s, full["ffn2_w2"], a2, b2, name="ffn2_bwd_mid", exchange=ex)
    sent(names, ex)
    dfull["ffn2_w2"] = _mm_tn(s2, dx3s, name="ffn2_dw2", tn=1024, tm=512)
    dfull["ffn2_w1"] = _mm_tn(h3, da2, name="ffn2_dw1", tn=1024, tm=1408, transpose_out=True)
    dfull["ffn2_w3"] = _mm_tn(h3, db2, name="ffn2_dw3", tn=1024, tm=1408, transpose_out=True)
    names = ("ffn2_w2", "ffn2_w1")
    ex = _scatter_of(names, dfull)
    dx2, dx2b, d_f2 = _mm_nt_rms_bwd([(da2, full["ffn2_w1"], True), (db2, full["ffn2_w3"], True)], x2, dx3, g_f2,
                                     name="ffn2_rms_bwd", tn=512, half_scale=1.0, exchange=ex)
    sent(names, ex)

    dfull["w_out"] = _mm_tn(y, dx2b, name="d_w_out", tn=1024, tm=1024)
    dy = _mm_nt([(dx2b, full["w_out"])], name="d_y", tn=1024, tk=512, out_dtype=BF16)
    dya, dyb, dga, dgb = _merge_bwd(dy, proj, ya, yb, name="merge_bwd")

    dfull["w_branch_b"] = _mm_tn(mixed, dyb, name="d_w_branch_b", tn=1024, tm=1024, transpose_out=True)
    dmixed = _mm_nn(dyb, full["w_branch_b"], name="d_mixed", tn=1024, tm=512)
    du, d_pw, d_ps = _pool_bwd(dmixed, pooled, pw, pscale, name="pool_bwd")

    dfull["w_branch_a"] = _mm_tn(on, dya, name="d_w_branch_a", tn=1024, tm=1024)
    don = _mm_nt([(dya, full["w_branch_a"])], name="d_on", tn=1024, tk=512, out_dtype=BF16)
    dog, do, d_on = _hgrn_post_bwd(don, o, proj, g_on, name="hgrn_post_bwd")
    names = ("ffn2_w3", "w_out", "w_branch_b", "w_branch_a")
    ex = _scatter_of(names, dfull)
    dqfi, d_lb = _hgrn_bwd(proj, lb2, do, states, name="hgrn_bwd", exchange=ex)
    sent(names, ex)
    dproj = [dqfi, dog, du, dga, dgb]
    dfull["w_in"] = jnp.concatenate(
        [_mm_tn(h2, part, name=f"d_w_in_{j}", tn=1024, tm=1536, transpose_out=True) for j, part in enumerate(dproj)],
        axis=0)
    names = ("w_in",)
    ex = _scatter_of(names, dfull)
    dx1, dx1s, d_mix = _mm_nt_rms_bwd([(dproj, full["w_in"], True)], x1, dx2, g_mix, name="mix_rms_bwd", tn=512,
                                      half_scale=0.5, exchange=ex)
    sent(names, ex)

    da1, db1 = _ffn_bwd_mid(dx1s, full["ffn1_w2"], a1, b1, name="ffn1_bwd_mid")
    dfull["ffn1_w2"] = _mm_tn(s1, dx1s, name="ffn1_dw2", tn=1024, tm=512)
    names = ("ffn1_w2",)
    ex = _scatter_of(names, dfull)
    dfull["ffn1_w1"] = _mm_tn(h1, da1, name="ffn1_dw1", tn=1024, tm=1408, transpose_out=True, exchange=ex)
    sent(names, ex)
    names = ("ffn1_w1",)
    ex = _scatter_of(names, dfull)
    dfull["ffn1_w3"] = _mm_tn(h1, db1, name="ffn1_dw3", tn=1024, tm=1408, transpose_out=True, exchange=ex)
    sent(names, ex)
    names = ("ffn1_w3",)
    ex = _scatter_of(names, dfull)
    grad_x, _, d_f1 = _mm_nt_rms_bwd([(da1, full["ffn1_w1"], True), (db1, full["ffn1_w3"], True)], xs, dx1, g_f1,
                                     name="ffn1_rms_bwd", tn=512, half_scale=1.0, exchange=ex)
    sent(names, ex)

    small_part = dict(ffn1_norm=d_f1, mix_norm=d_mix, hgrn_onorm=d_on, ffn2_norm=d_f2, ple_norm=d_ple,
                      ple_post_norm=d_post, final_norm=d_fin, hgrn_lb=d_lb, pool_scale=d_ps, pool_w=_as_2d(d_pw))
    small_all = dict(zip(SMALL_PARAMS, _exchange_now([small_part[k] for k in SMALL_PARAMS],
                                                     name="gather_small_grads", gather=True)))

    grads, deltas, new_m, new_v = {}, {}, {}, {}
    for name, axis in BIG_WEIGHTS:
        shape, recv = weights[name].shape, received[name]
        own = [t[name][0].T if axis == 1 else t[name][0] for t in (weights, mom1, mom2)]
        res = _adam_big(recv, *own, name=f"adam_{name}")
        grads[name], deltas[name], new_m[name], new_v[name] = [(r.T if axis == 1 else r).reshape(shape) for r in res]
    res = _adam_small(small_all, *[{k: _as_2d(t[k]) for k in SMALL_PARAMS} for t in (weights, mom1, mom2)],
                      name="adam_small")
    for store, vals in zip((grads, deltas, new_m, new_v), res):
        store.update({k: val.reshape(weights[k].shape) for k, val in zip(SMALL_PARAMS, vals)})

    loss = lax.psum(jnp.sum(loss_part), ("x", "y", "c"))
    return (loss, grad_x.reshape(x.shape), *[grads[k] for k in WEIGHT_ORDER], *[deltas[k] for k in WEIGHT_ORDER],
            *[new_m[k] for k in WEIGHT_ORDER], *[new_v[k] for k in WEIGHT_ORDER])
```

```python
import jax
import jax.numpy as jnp
from jax import lax
from jax.experimental import pallas as pl
from jax.experimental.pallas import tpu as pltpu

F32 = jnp.float32
BF16 = jnp.bfloat16

N_DEV = 8
D_MODEL = 1024
HEADS = 8
HEAD_DIM = 128
POOL_WINDOWS = (2, 4, 8, 16)
POOL_CH = 128
POOL_WIDTH = 512
POOL_HALO = 16
RMS_EPS = 1e-6
CHUNK = 64
SUB = 32
HGRN_HEADS_PER_STEP = 8
NEG_BIG = -1e30

ADAM_LR = 0.001
ADAM_B1 = 0.9
ADAM_B2 = 0.999
ADAM_EPS = 1e-08
ADAM_WD = 0.01
ADAM_STEP = 10

V7X_VMEM_BYTES = 64 * 1024 * 1024
VMEM_LIMIT = (V7X_VMEM_BYTES * 3) // 4
EXCHANGE_TAIL_STEPS = 3
ROW_TILE_CAP = 8192

COL_Q, COL_F, COL_I, COL_OG, COL_POOL, COL_GA, COL_GB = 0, 1024, 2048, 3072, 4096, 4608, 5632

BIG_WEIGHTS = (
    ("ffn1_w1", 1), ("ffn1_w3", 1), ("ffn1_w2", 0), ("w_in", 1), ("w_branch_a", 0), ("w_branch_b", 1),
    ("w_out", 0), ("ffn2_w1", 1), ("ffn2_w3", 1), ("ffn2_w2", 0), ("ple_w_gate", 0), ("ple_w_proj", 1),
)
SMALL_PARAMS = ("ffn1_norm", "mix_norm", "hgrn_onorm", "ffn2_norm", "ple_norm", "ple_post_norm", "final_norm",
                "hgrn_lb", "pool_scale", "pool_w")
WEIGHT_ORDER = (
    "ffn1_norm", "ffn1_w1", "ffn1_w3", "ffn1_w2", "mix_norm", "w_in", "hgrn_lb", "hgrn_onorm", "w_branch_a", "pool_w",
    "pool_scale", "w_branch_b", "w_out", "ffn2_norm", "ffn2_w1", "ffn2_w3", "ffn2_w2", "ple_norm", "ple_w_gate",
    "ple_w_proj", "ple_post_norm", "final_norm",
)


def _params(*sem):
    return pltpu.CompilerParams(dimension_semantics=sem if sem else None, vmem_limit_bytes=VMEM_LIMIT)


COL_CHUNK = 256


def _rows(tn, width):
    return pl.BlockSpec((tn, width), lambda i: (i, 0))


def _resident(shape):
    return pl.BlockSpec(shape, lambda i: (0,) * len(shape), pipeline_mode=pl.Buffered(1))


def _dot(a, b):
    return jnp.dot(a, b, preferred_element_type=F32)


def _dot_nt(a, b):
    return lax.dot_general(a, b, (((1,), (1,)), ((), ())), preferred_element_type=F32)


def _dot_tn(a, b):
    return lax.dot_general(a, b, (((0,), (0,)), ((), ())), preferred_element_type=F32)


def _sigmoid(x):
    return 0.5 * jnp.tanh(0.5 * x) + 0.5


def _tile(n, want, mult):
    if mult != 128:
        want = min(want, ROW_TILE_CAP)
    if n <= want:
        return n
    t = (want // mult) * mult
    while t > mult and n % t:
        t -= mult
    assert n % t == 0, (n, want, mult)
    return t


class _Exchange:
    COPIES = N_DEV - 1

    def __init__(self, arrs, gather):
        self.arrs, self.gather, self.n = list(arrs), gather, len(arrs)
        self.out_shape = [jax.ShapeDtypeStruct((N_DEV,) + (a.shape if gather else a.shape[1:]), a.dtype) for a in arrs]
        self.scratch = [pltpu.SemaphoreType.DMA((self.n * self.COPIES,)),
                        pltpu.SemaphoreType.DMA((self.n * self.COPIES,)), pltpu.SemaphoreType.DMA((self.n,))]
        self.received = None

    @staticmethod
    def _place():
        x, y, c = lax.axis_index("x"), lax.axis_index("y"), lax.axis_index("c")
        return x, y, c

    def _copy(self, a, k, src, dst, to, sems):
        s = a * self.COPIES + k
        return pltpu.make_async_remote_copy(src_ref=src, dst_ref=dst, send_sem=sems[0].at[s], recv_sem=sems[1].at[s],
                                            device_id=to, device_id_type=pl.DeviceIdType.MESH)

    def _gather_copies(self, ins, outs, sems):
        x, y, c = self._place()
        chips = [(1 - x, y), (x, 1 - y), (1 - x, 1 - y)]
        slot = lambda px, py, pc: 4 * px + 2 * py + pc
        first, passed, arrivals = [], [], []
        for a in range(self.n):
            mine = outs[a].at[slot(x, y, c)]
            first.append(self._copy(a, 0, ins[a], mine, (x, y, 1 - c), sems))
            arrivals.append(self._copy(a, 0, ins[a], outs[a].at[slot(x, y, 1 - c)], (x, y, 1 - c), sems))
            for j, (px, py) in enumerate(chips):
                first.append(self._copy(a, 1 + j, ins[a], mine, (px, py, c), sems))
                theirs = outs[a].at[slot(px, py, c)]
                passed.append((self._copy(a, 1 + j, ins[a], theirs, (px, py, c), sems),
                               self._copy(a, 4 + j, theirs, theirs, (x, y, 1 - c), sems)))
                arrivals.append(self._copy(a, 4 + j, ins[a], outs[a].at[slot(px, py, 1 - c)], (x, y, 1 - c), sems))
        return first, passed, arrivals

    def _scatter_copies(self, ins, outs, sems):
        x, y, c = self._place()
        me = 4 * x + 2 * y + c
        sends, arrivals = [], []
        for k in range(1, N_DEV):
            px = 1 - x if k & 4 else x
            py = 1 - y if k & 2 else y
            pc = 1 - c if k & 1 else c
            peer = 4 * px + 2 * py + pc
            for a in range(self.n):
                sends.append(self._copy(a, k - 1, ins[a].at[peer], outs[a].at[me], (px, py, pc), sems))
                arrivals.append(self._copy(a, k - 1, ins[a].at[peer], outs[a].at[peer], (px, py, pc), sems))
        return sends, arrivals

    def _local(self, ins, outs, sems):
        x, y, c = self._place()
        me = 4 * x + 2 * y + c
        return [pltpu.make_async_copy(ins[a] if self.gather else ins[a].at[me], outs[a].at[me], sems[2].at[a])
                for a in range(self.n)]

    def start(self, ins, outs, sems):
        for cp in self._local(ins, outs, sems):
            cp.start()
        sends = self._gather_copies(ins, outs, sems)[0] if self.gather else self._scatter_copies(ins, outs, sems)[0]
        for cp in sends:
            cp.start()

    def pass_on(self, ins, outs, sems):
        if self.gather:
            for landed, onward in self._gather_copies(ins, outs, sems)[1]:
                landed.wait_recv()
                onward.start()

    def finish(self, ins, outs, sems):
        if self.gather:
            first, passed, arrivals = self._gather_copies(ins, outs, sems)
            sends = first + [onward for _, onward in passed]
        else:
            sends, arrivals = self._scatter_copies(ins, outs, sems)
        for cp in arrivals:
            cp.wait_recv()
        for cp in sends:
            cp.wait_send()
        for cp in self._local(ins, outs, sems):
            cp.wait()


def _call(body, *, name, grid, in_specs, out_specs, out_shape, args, semantics, scratch=(), exchange=None):
    if exchange is None:
        return pl.pallas_call(
            body, name=name, grid=grid, in_specs=in_specs, out_specs=out_specs, out_shape=out_shape,
            scratch_shapes=list(scratch), compiler_params=_params(*semantics))(*args)
    ex = exchange
    n_in, n_out, n_s = len(in_specs), len(out_specs), len(scratch)

    def wrapped(*refs):
        ins, ex_in = refs[:n_in], refs[n_in:n_in + ex.n]
        o0 = n_in + ex.n
        outs, ex_out = refs[o0:o0 + n_out], refs[o0 + n_out:o0 + n_out + ex.n]
        s0 = o0 + n_out + ex.n
        scr, sems = refs[s0:s0 + n_s], refs[s0 + n_s:]
        step, steps = pl.program_id(0), grid[0]
        for ax in range(1, len(grid)):
            step, steps = step * grid[ax] + pl.program_id(ax), steps * grid[ax]

        @pl.when(step == 0)
        def _():
            ex.start(ex_in, ex_out, sems)

        body(*ins, *outs, *scr)

        @pl.when(step == max(steps - EXCHANGE_TAIL_STEPS, 0))
        def _():
            ex.pass_on(ex_in, ex_out, sems)

        @pl.when(step == steps - 1)
        def _():
            ex.finish(ex_in, ex_out, sems)

    hbm = pl.BlockSpec(memory_space=pltpu.HBM)
    res = pl.pallas_call(
        wrapped, name=name, grid=grid, in_specs=list(in_specs) + [hbm] * ex.n,
        out_specs=list(out_specs) + [hbm] * ex.n, out_shape=list(out_shape) + ex.out_shape,
        scratch_shapes=list(scratch) + ex.scratch, compiler_params=_params(*(["arbitrary"] * len(grid))),
    )(*args, *ex.arrs)
    ex.received = res[n_out:]
    return res[:n_out]


def _exchange_now(arrs, *, name, gather):
    ex = _Exchange(arrs, gather)
    n = ex.n

    def body(*refs):
        ex.start(refs[:n], refs[n:2 * n], refs[2 * n:])
        ex.pass_on(refs[:n], refs[n:2 * n], refs[2 * n:])
        ex.finish(refs[:n], refs[n:2 * n], refs[2 * n:])

    hbm = pl.BlockSpec(memory_space=pltpu.HBM)
    return pl.pallas_call(body, name=name, out_shape=ex.out_shape, in_specs=[hbm] * n, out_specs=[hbm] * n,
                          scratch_shapes=ex.scratch)(*arrs)


def _mm_nn(a, b, *, name, tn, tm, out_dtype=F32, res=None, scale=1.0, exchange=None):
    n, k = a.shape
    m = b.shape[1]
    tn, tm = _tile(n, tn, 16), _tile(m, tm, 128)

    def body(*refs):
        a_ref, b_ref = refs[0], refs[1]
        o_ref = refs[-1]
        acc = _dot(a_ref[...], b_ref[...])
        if scale != 1.0:
            acc = acc * scale
        if res is not None:
            acc = acc + refs[2][...]
        o_ref[...] = acc.astype(o_ref.dtype)

    in_specs = [pl.BlockSpec((tn, k), lambda i, j: (i, 0)), pl.BlockSpec((k, tm), lambda i, j: (0, j))]
    args = [a, b]
    if res is not None:
        in_specs.append(pl.BlockSpec((tn, tm), lambda i, j: (i, j)))
        args.append(res)
    return _call(body, name=name, grid=(n // tn, m // tm), in_specs=in_specs,
                 out_specs=[pl.BlockSpec((tn, tm), lambda i, j: (i, j))],
                 out_shape=[jax.ShapeDtypeStruct((n, m), out_dtype)], args=args, semantics=("parallel", "parallel"),
                 exchange=exchange)[0]


def _mm_nt(pairs, *, name, tn, tk, out_dtype=F32, exchange=None):
    n = pairs[0][0].shape[0]
    kk = pairs[0][1].shape[0]
    tn, tk = _tile(n, tn, 16), _tile(kk, tk, 128)
    npair = len(pairs)

    def body(*refs):
        o_ref = refs[-1]
        acc = _dot_nt(refs[0][...], refs[1][...])
        for q in range(1, npair):
            acc = acc + _dot_nt(refs[2 * q][...], refs[2 * q + 1][...])
        o_ref[...] = acc.astype(o_ref.dtype)

    in_specs, args = [], []
    for a, b in pairs:
        m = a.shape[1]
        in_specs += [pl.BlockSpec((tn, m), lambda i, j: (i, 0)), pl.BlockSpec((tk, m), lambda i, j: (j, 0))]
        args += [a, b]
    return _call(body, name=name, grid=(n // tn, kk // tk), in_specs=in_specs,
                 out_specs=[pl.BlockSpec((tn, tk), lambda i, j: (i, j))],
                 out_shape=[jax.ShapeDtypeStruct((n, kk), out_dtype)], args=args, semantics=("parallel", "parallel"),
                 exchange=exchange)[0]


def _mm_tn(a, b, *, name, tn, tm, transpose_out=False, exchange=None):
    n, k = a.shape
    m = b.shape[1]
    tn, tm = _tile(n, tn, 16), _tile(m, tm, 128)
    steps = n // tn

    def body(a_ref, b_ref, o_ref, acc):
        i = pl.program_id(1)

        @pl.when(i == 0)
        def _():
            acc[...] = jnp.zeros_like(acc)

        acc[...] += _dot_tn(a_ref[...], b_ref[...])

        @pl.when(i == steps - 1)
        def _():
            res = acc[...]
            o_ref[...] = (res.T if transpose_out else res).astype(o_ref.dtype)

    if transpose_out:
        out_spec, out_shape = pl.BlockSpec((tm, k), lambda j, i: (j, 0)), jax.ShapeDtypeStruct((m, k), BF16)
    else:
        out_spec, out_shape = pl.BlockSpec((k, tm), lambda j, i: (0, j)), jax.ShapeDtypeStruct((k, m), BF16)
    return _call(body, name=name, grid=(m // tm, steps),
                 in_specs=[pl.BlockSpec((tn, k), lambda j, i: (i, 0)), pl.BlockSpec((tn, tm), lambda j, i: (i, j))],
                 out_specs=[out_spec], out_shape=[out_shape], args=[a, b], semantics=("parallel", "arbitrary"),
                 scratch=[pltpu.VMEM((k, tm), F32)], exchange=exchange)[0]


def _ffn_up(h, w1, w3, *, name, tn=512, exchange=None):
    n, k = h.shape
    m = w1.shape[0]
    tn = _tile(n, tn, 16)

    def body(h_ref, w1_ref, w3_ref, dsda_ref, dsdb_ref, s_ref):
        for c0 in range(0, m, COL_CHUNK):
            cols = slice(c0, c0 + COL_CHUNK)
            a = _dot_nt(h_ref[...], w1_ref[cols, :])
            b = _dot_nt(h_ref[...], w3_ref[cols, :])
            sg = _sigmoid(a)
            silu = a * sg
            dsda_ref[:, cols] = (b * (sg + silu * (1.0 - sg))).astype(dsda_ref.dtype)
            dsdb_ref[:, cols] = silu.astype(dsdb_ref.dtype)
            s_ref[:, cols] = (silu * b).astype(s_ref.dtype)

    ospec = _rows(tn, m)
    return _call(body, name=name, grid=(n // tn,),
                 in_specs=[_rows(tn, k), _resident(w1.shape), _resident(w3.shape)], out_specs=[ospec, ospec, ospec],
                 out_shape=[jax.ShapeDtypeStruct((n, m), BF16)] * 3,
                 args=[h, w1, w3], semantics=("parallel",), exchange=exchange)


def _mm_nn_wide(a, b, *, name, out_dtype, exchange=None):
    n, k = a.shape
    m = b.shape[0]
    tn = _tile(n, 512, 16)
    chunk = 2 * COL_CHUNK

    def body(a_ref, b_ref, o_ref):
        for c0 in range(0, m, chunk):
            cols = slice(c0, c0 + chunk)
            o_ref[:, cols] = _dot_nt(a_ref[...], b_ref[cols, :]).astype(o_ref.dtype)

    return _call(body, name=name, grid=(n // tn,), in_specs=[_rows(tn, k), _resident(b.shape)],
                 out_specs=[_rows(tn, m)], out_shape=[jax.ShapeDtypeStruct((n, m), out_dtype)], args=[a, b],
                 semantics=("parallel",), exchange=exchange)[0]


def _mm_nn_res_rms(a, b, res, g, *, name, scale, exchange=None):
    n, k = a.shape
    d = b.shape[1]
    tn = _tile(n, 512, 16)

    def body(a_ref, b_ref, r_ref, g_ref, x_ref, h_ref):
        for c0 in range(0, d, COL_CHUNK):
            cols = slice(c0, c0 + COL_CHUNK)
            x_ref[:, cols] = r_ref[:, cols] + scale * _dot(a_ref[...], b_ref[:, cols])
        x = x_ref[...]
        r = lax.rsqrt(_rowmean(x * x) + RMS_EPS)
        h_ref[...] = (x * r * g_ref[...]).astype(h_ref.dtype)

    row = _rows(tn, d)
    return _call(body, name=name, grid=(n // tn,),
                 in_specs=[_rows(tn, k), _resident(b.shape), row, pl.BlockSpec((1, d), lambda i: (0, 0))],
                 out_specs=[row, row],
                 out_shape=[jax.ShapeDtypeStruct((n, d), F32), jax.ShapeDtypeStruct((n, d), BF16)],
                 args=[a, b, res, g], semantics=("parallel",), exchange=exchange)


def _ffn_bwd_mid(dxs, w2, dsda, dsdb, *, name, exchange=None):
    n, d = dxs.shape
    m = w2.shape[0]
    tn = _tile(n, 512, 16)

    def body(dx_ref, w2_ref, dsda_ref, dsdb_ref, da_ref, db_ref):
        for c0 in range(0, m, COL_CHUNK):
            cols = slice(c0, c0 + COL_CHUNK)
            ds = _dot_nt(dx_ref[...], w2_ref[cols, :])
            da_ref[:, cols] = (ds * dsda_ref[:, cols].astype(F32)).astype(da_ref.dtype)
            db_ref[:, cols] = (ds * dsdb_ref[:, cols].astype(F32)).astype(db_ref.dtype)

    tile = _rows(tn, m)
    return _call(body, name=name, grid=(n // tn,),
                 in_specs=[_rows(tn, d), _resident(w2.shape), tile, tile], out_specs=[tile, tile],
                 out_shape=[jax.ShapeDtypeStruct((n, m), BF16), jax.ShapeDtypeStruct((n, m), BF16)],
                 args=[dxs, w2, dsda, dsdb], semantics=("parallel",), exchange=exchange)


def _rowwise(fn, *, name, n, tn, ncol, rows, vecs, outs, accs=(), exchange=None):
    tn = _tile(n, tn, 16)
    nr, nv, no = len(rows), len(vecs), len(outs)

    def body(*refs):
        first = pl.program_id(1) == 0
        vals = [r[...].astype(F32) for r in refs[:nr + nv]]
        res = fn(*vals)
        for ref, val in zip(refs[nr + nv:nr + nv + no], res[:no]):
            ref[...] = val.astype(ref.dtype)
        for ref, val in zip(refs[nr + nv + no:], res[no:]):
            _accumulate(ref, val, first)

    in_specs = [pl.BlockSpec((tn, w), lambda j, i, c0=c0: (i, c0 + j)) for _, w, c0 in rows]
    in_specs += [pl.BlockSpec((1, w), lambda j, i, c0=c0: (0, c0 + j)) for _, w, c0 in vecs]
    out_specs = [pl.BlockSpec((tn, w), lambda j, i: (i, j)) for _, w, _ in outs]
    out_specs += [pl.BlockSpec((1, w), lambda j, i: (0, j)) for _, w in accs]
    out_shape = [jax.ShapeDtypeStruct((n, tw), dt) for tw, _, dt in outs]
    out_shape += [jax.ShapeDtypeStruct((1, tw), F32) for tw, _ in accs]
    return _call(body, name=name, grid=(ncol, n // tn), in_specs=in_specs, out_specs=out_specs, out_shape=out_shape,
                 args=[r[0] for r in rows] + [v[0] for v in vecs], semantics=("parallel", "arbitrary"),
                 exchange=exchange)


def _accumulate(ref, val, first):
    @pl.when(first)
    def _():
        ref[...] = jnp.zeros_like(ref)

    ref[...] += val


def _colsum(x):
    return jnp.sum(x, axis=0, keepdims=True)


def _rowmean(x):
    return jnp.mean(x, axis=-1, keepdims=True)


def _rms_fwd(x, g, *, name, exchange=None):
    def fn(x_, g_):
        r = lax.rsqrt(_rowmean(x_ * x_) + RMS_EPS)
        return (x_ * r * g_,)

    n, d = x.shape
    return _rowwise(fn, name=name, n=n, tn=512, ncol=1, rows=[(x, d, 0)], vecs=[(g, d, 0)], outs=[(d, d, BF16)],
                    exchange=exchange)[0]


def _mm_nt_rms_bwd(pairs, x, extra, g, *, name, tn, half_scale, exchange=None):
    n, d = x.shape
    tn = _tile(n, tn, 16)
    pairs = [(list(a) if isinstance(a, (list, tuple)) else [a], b, t) for a, b, t in pairs]
    nref = sum(len(a) + 1 for a, _, _ in pairs)

    def body(*refs):
        x_ref, e_ref, g_ref, dx_ref, dxs_ref, dg_ref = refs[nref:]
        dh, at = None, 0
        for parts, _, transposed in pairs:
            b_ref = refs[at + len(parts)]
            col = 0
            for j, part in enumerate(parts):
                w = part.shape[1]
                if transposed:
                    term = _dot(refs[at + j][...], b_ref[col:col + w, :])
                else:
                    term = _dot_nt(refs[at + j][...], b_ref[:, col:col + w])
                dh = term if dh is None else dh + term
                col += w
            at += len(parts) + 1
        x_ = x_ref[...]
        r = lax.rsqrt(_rowmean(x_ * x_) + RMS_EPS)
        xh = x_ * r
        dxh = dh * g_ref[...]
        dx = e_ref[...] + r * (dxh - xh * _rowmean(dxh * xh))
        dx_ref[...] = dx
        dxs_ref[...] = (dx * half_scale).astype(dxs_ref.dtype)
        _accumulate(dg_ref, _colsum(dh * xh), pl.program_id(0) == 0)

    in_specs, args = [], []
    for parts, b, transposed in pairs:
        assert sum(part.shape[1] for part in parts) == b.shape[0 if transposed else 1]
        in_specs += [pl.BlockSpec((tn, part.shape[1]), lambda i: (i, 0)) for part in parts]
        in_specs.append(pl.BlockSpec(b.shape, lambda i: (0, 0), pipeline_mode=pl.Buffered(1)))
        args += parts + [b]
    row = pl.BlockSpec((tn, d), lambda i: (i, 0))
    vec = pl.BlockSpec((1, d), lambda i: (0, 0))
    return _call(body, name=name, grid=(n // tn,), in_specs=in_specs + [row, row, vec], out_specs=[row, row, vec],
                 out_shape=[jax.ShapeDtypeStruct((n, d), F32), jax.ShapeDtypeStruct((n, d), BF16),
                            jax.ShapeDtypeStruct((1, d), F32)],
                 args=args + [x, extra, g], semantics=("arbitrary",), exchange=exchange)


def _ple_final(x3, gpre, z, tgt, gpp, gf, *, name):
    def fn(x3_, gpre_, z_, tgt_, gpp_, gf_):
        gate = _sigmoid(gpre_)
        rz = lax.rsqrt(_rowmean(z_ * z_) + RMS_EPS)
        zh = z_ * rz
        e = zh * gpp_
        x4 = x3_ + gate * e
        r4 = lax.rsqrt(_rowmean(x4 * x4) + RMS_EPS)
        x4h = x4 * r4
        diff = x4h * gf_ - tgt_
        dout = diff * (1.0 / D_MODEL)
        dxh4 = dout * gf_
        dx4 = r4 * (dxh4 - x4h * _rowmean(dxh4 * x4h))
        dpre = dx4 * e * gate * (1.0 - gate)
        de = dx4 * gate
        dzh = de * gpp_
        dz = rz * (dzh - zh * _rowmean(dzh * zh))
        return dx4, dpre, dz, _colsum(diff * diff) * (0.5 / D_MODEL), _colsum(dout * x4h), _colsum(de * zh)

    n, d = x3.shape
    return _rowwise(fn, name=name, n=n, tn=256, ncol=1, rows=[(x3, d, 0), (gpre, d, 0), (z, d, 0), (tgt, d, 0)],
                    vecs=[(gpp, d, 0), (gf, d, 0)], outs=[(d, d, F32), (d, d, BF16), (d, d, BF16)],
                    accs=[(d, d), (d, d), (d, d)])


def _merge_fwd(proj, ya, yb, *, name):
    def fn(ga, gb, ya_, yb_):
        return (_sigmoid(ga) * ya_ + _sigmoid(gb) * yb_,)

    n = proj.shape[0]
    w = 512
    return _rowwise(fn, name=name, n=n, tn=512, ncol=D_MODEL // w,
                    rows=[(proj, w, COL_GA // w), (proj, w, COL_GB // w), (ya, w, 0), (yb, w, 0)], vecs=[],
                    outs=[(D_MODEL, w, BF16)])[0]


def _merge_bwd(dy, proj, ya, yb, *, name):
    def fn(dy_, ga, gb, ya_, yb_):
        sa, sb = _sigmoid(ga), _sigmoid(gb)
        return dy_ * sa, dy_ * sb, dy_ * ya_ * sa * (1.0 - sa), dy_ * yb_ * sb * (1.0 - sb)

    n = proj.shape[0]
    w = 512
    return _rowwise(fn, name=name, n=n, tn=512, ncol=D_MODEL // w,
                    rows=[(dy, w, 0), (proj, w, COL_GA // w), (proj, w, COL_GB // w), (ya, w, 0), (yb, w, 0)],
                    vecs=[], outs=[(D_MODEL, w, BF16)] * 4)


def _head_mean(x):
    return jnp.concatenate(
        [jnp.broadcast_to(jnp.mean(x[:, h * HEAD_DIM:(h + 1) * HEAD_DIM], axis=-1, keepdims=True),
                          (x.shape[0], HEAD_DIM)) for h in range(HEADS)], axis=1)


def _hgrn_post_fwd(o, proj, onorm, *, name):
    def fn(o_, og, gam):
        r = lax.rsqrt(_head_mean(o_ * o_) + RMS_EPS)
        return (o_ * r * gam * (og * _sigmoid(og)),)

    n = o.shape[0]
    w = D_MODEL
    return _rowwise(fn, name=name, n=n, tn=256, ncol=1, rows=[(o, w, 0), (proj, w, COL_OG // w)],
                    vecs=[(onorm, w, 0)], outs=[(D_MODEL, w, BF16)])[0]


def _hgrn_post_bwd(don, o, proj, onorm, *, name):
    def fn(don_, o_, og, gam):
        r = lax.rsqrt(_head_mean(o_ * o_) + RMS_EPS)
        oh = o_ * r
        sg = _sigmoid(og)
        dog = don_ * oh * gam * (sg * (1.0 + og * (1.0 - sg)))
        dn = don_ * (og * sg)
        doh = dn * gam
        do = r * (doh - oh * _head_mean(doh * oh))
        return dog, do, _colsum(dn * oh)

    n = o.shape[0]
    w = D_MODEL
    return _rowwise(fn, name=name, n=n, tn=256, ncol=1, rows=[(don, w, 0), (o, w, 0), (proj, w, COL_OG // w)],
                    vecs=[(onorm, w, 0)], outs=[(D_MODEL, w, BF16), (D_MODEL, w, BF16)], accs=[(D_MODEL, w)])


def _tri_sum(tri, x):
    hi = x.astype(BF16)
    lo = (x - hi.astype(F32)).astype(BF16)
    return _dot(tri, hi) + _dot(tri, lo)


def _lower_bound(lb_ref):
    return 1.0 / (1.0 + jnp.exp(lb_ref[1:2, :] - lb_ref[0:1, :]))


def _hgrn_specs(n, t, reverse):
    nt = n // t
    width = HGRN_HEADS_PER_STEP * HEAD_DIM

    def tok(i):
        return nt - 1 - i if reverse else i

    def sec(col):
        c0 = col // width
        return pl.BlockSpec((t, width), lambda h, i: (tok(i), c0 + h))

    head_tile = pl.BlockSpec((t, width), lambda h, i: (tok(i), h))
    state = pl.BlockSpec((HGRN_HEADS_PER_STEP, t // CHUNK, HEAD_DIM, HEAD_DIM), lambda h, i: (h, tok(i), 0, 0))
    lb = pl.BlockSpec((2, width), lambda h, i: (0, h))
    return sec, head_tile, state, lb


def _hgrn_fwd(proj, hgrn_lb, *, name):
    n = proj.shape[0]
    t = _tile(n, 512, CHUNK)
    nc = t // CHUNK
    hps = HGRN_HEADS_PER_STEP
    width = hps * HEAD_DIM
    lanes = [slice(h * HEAD_DIM, (h + 1) * HEAD_DIM) for h in range(hps)]
    sec, head_tile, state, lbspec = _hgrn_specs(n, t, False)

    def body(q_ref, f_ref, i_ref, lb_ref, o_ref, st_ref, s_acc, g_s, a_s):
        @pl.when(pl.program_id(1) == 0)
        def _():
            s_acc[...] = jnp.zeros_like(s_acc)

        lb = _lower_bound(lb_ref)
        row = lax.broadcasted_iota(jnp.int32, (CHUNK, CHUNK), 0)
        col = lax.broadcasted_iota(jnp.int32, (CHUNK, CHUNK), 1)
        tril = row >= col
        trilb = jnp.where(tril, 1.0, 0.0).astype(BF16)
        rowk = lax.broadcasted_iota(jnp.int32, (CHUNK, width), 0)

        def chunk(c, carry):
            rows = pl.ds(pl.multiple_of(c * CHUNK, CHUNK), CHUNK)
            qr, fr, v = [r[rows, :].astype(F32) for r in (q_ref, f_ref, i_ref)]
            q = qr * _sigmoid(qr)
            f = lb + (1.0 - lb) * _sigmoid(fr)
            k = 1.0 - f
            g = _tri_sum(trilb, jnp.log(f))
            g_s[...] = g
            st0 = [s_acc[h] for h in range(hps)]
            for h in range(hps):
                st_ref[h, c] = st0[h]
            vb = v.astype(BF16)
            for blk in range(CHUNK // SUB):
                lo, hi = blk * SUB, (blk + 1) * SUB
                gref = g_s[lo - 1:lo, :] if blk else jnp.zeros((1, width), F32)
                qi = (q[lo:hi] * jnp.exp(g[lo:hi] - gref)).astype(BF16)
                ki = (k * jnp.exp(jnp.where(rowk < hi, gref - g, NEG_BIG))).astype(BF16)
                for h, ln in enumerate(lanes):
                    a_s[h, lo:hi, :] = _dot_nt(qi[:, ln], ki[:, ln])
            qeb = (q * jnp.exp(g)).astype(BF16)
            o_ref[rows, :] = jnp.concatenate(
                [_dot(jnp.where(tril, a_s[h], 0.0).astype(BF16), vb[:, ln]) + _dot_nt(qeb[:, ln], st0[h].astype(BF16))
                 for h, ln in enumerate(lanes)], axis=1).astype(o_ref.dtype)
            glast = g_s[CHUNK - 1:CHUNK, :]
            kdb = (k * jnp.exp(glast - g)).astype(BF16)
            dec = jnp.exp(glast)
            for h, ln in enumerate(lanes):
                s_acc[h] = st0[h] * dec[:, ln] + _dot_tn(vb[:, ln], kdb[:, ln])
            return carry

        lax.fori_loop(0, nc, chunk, 0)

    return pl.pallas_call(
        body, name=name, grid=(HEADS // hps, n // t),
        in_specs=[sec(COL_Q), sec(COL_F), sec(COL_I), lbspec], out_specs=[head_tile, state],
        out_shape=[jax.ShapeDtypeStruct((n, D_MODEL), BF16),
                   jax.ShapeDtypeStruct((HEADS, n // CHUNK, HEAD_DIM, HEAD_DIM), F32)],
        scratch_shapes=[pltpu.VMEM((hps, HEAD_DIM, HEAD_DIM), F32), pltpu.VMEM((CHUNK, width), F32),
                        pltpu.VMEM((hps, CHUNK, CHUNK), F32)],
        compiler_params=_params("parallel", "arbitrary"),
    )(proj, proj, proj, hgrn_lb)


def _hgrn_bwd(proj, hgrn_lb, do, states, *, name, exchange=None):
    n = proj.shape[0]
    t = _tile(n, 512, CHUNK)
    nc = t // CHUNK
    hps = HGRN_HEADS_PER_STEP
    width = hps * HEAD_DIM
    lanes = [slice(h * HEAD_DIM, (h + 1) * HEAD_DIM) for h in range(hps)]
    sec, head_tile, state, lbspec = _hgrn_specs(n, t, True)

    def body(q_ref, f_ref, i_ref, lb_ref, do_ref, st_ref, dqfi_ref, dlb_ref, d_acc, g_s, a_s, dq_s,
             dg_s):
        first = pl.program_id(1) == 0

        @pl.when(first)
        def _():
            d_acc[...] = jnp.zeros_like(d_acc)

        lb = _lower_bound(lb_ref)
        row = lax.broadcasted_iota(jnp.int32, (CHUNK, CHUNK), 0)
        col = lax.broadcasted_iota(jnp.int32, (CHUNK, CHUNK), 1)
        tril = row >= col
        trilb = jnp.where(tril, 1.0, 0.0).astype(BF16)
        triub = jnp.where(row <= col, 1.0, 0.0).astype(BF16)
        rowk = lax.broadcasted_iota(jnp.int32, (CHUNK, width), 0)

        def per_head(fn):
            return jnp.concatenate([fn(h, ln) for h, ln in enumerate(lanes)], axis=1)

        def chunk(j, dlb):
            c = nc - 1 - j
            rows = pl.ds(pl.multiple_of(c * CHUNK, CHUNK), CHUNK)
            qr, fr, v, dout = [r[rows, :].astype(F32) for r in (q_ref, f_ref, i_ref, do_ref)]
            sq = _sigmoid(qr)
            q = qr * sq
            sf = _sigmoid(fr)
            f = lb + (1.0 - lb) * sf
            k = 1.0 - f
            g = _tri_sum(trilb, jnp.log(f))
            g_s[...] = g
            st0 = [st_ref[h, c] for h in range(hps)]
            dt = [d_acc[h] for h in range(hps)]
            vb, dob = v.astype(BF16), dout.astype(BF16)
            dtb = [x.astype(BF16) for x in dt]
            st0b = [x.astype(BF16) for x in st0]
            glast = g_s[CHUNK - 1:CHUNK, :]
            eg = jnp.exp(g)
            kdec = jnp.exp(glast - g)
            qeb, kdb = (q * eg).astype(BF16), (k * kdec).astype(BF16)
            aps = [jnp.where(row > col, _dot_nt(dob[:, ln], vb[:, ln]), 0.0) for ln in lanes]
            dov = dout * v
            adiag = per_head(lambda h, ln: jnp.broadcast_to(
                jnp.sum(dov[:, ln], axis=-1, keepdims=True), (CHUNK, HEAD_DIM)))
            dq_inter = per_head(lambda h, ln: _dot(dob[:, ln], st0b[h]))
            dk_inter = per_head(lambda h, ln: _dot(vb[:, ln], dtb[h]))
            dk_st = kdec * dk_inter
            dg = qeb.astype(F32) * dq_inter
            dg_minus = kdb.astype(F32) * dk_inter
            dg = dg - dg_minus
            for blk in range(CHUNK // SUB):
                lo, hi = blk * SUB, (blk + 1) * SUB
                gref = g_s[lo - 1:lo, :] if blk else jnp.zeros((1, width), F32)
                qscale = jnp.exp(g[lo:hi] - gref)
                kscale = jnp.exp(jnp.where(rowk < hi, gref - g, NEG_BIG))
                qi = (q[lo:hi] * qscale).astype(BF16)
                ki = (k * kscale).astype(BF16)
                for h, ln in enumerate(lanes):
                    a_s[h, lo:hi, :] = _dot_nt(qi[:, ln], ki[:, ln])
                apb = [x[lo:hi].astype(BF16) for x in aps]
                from_k = per_head(lambda h, ln: _dot(apb[h], ki[:, ln]))
                from_q = per_head(lambda h, ln: _dot_tn(apb[h], qi[:, ln]))
                dq_s[lo:hi, :] = qscale * from_k
                dg_s[lo:hi, :] = qi.astype(F32) * from_k
                dk_st = dk_st + kscale * from_q
                dg = dg - ki.astype(F32) * from_q
            dg = dg + dg_s[...]
            dv = per_head(lambda h, ln: _dot_tn(jnp.where(tril, a_s[h], 0.0).astype(BF16), dob[:, ln])
                          + _dot_nt(kdb[:, ln], dtb[h]))
            dq_st = dq_s[...] + eg * dq_inter
            dq = dq_st + adiag * k
            dk = dk_st + adiag * q
            dec = jnp.exp(glast)
            dt_dec = [dt[h] * dec[:, ln] for h, ln in enumerate(lanes)]
            for h, ln in enumerate(lanes):
                d_acc[h] = dt_dec[h] + _dot_tn(dob[:, ln], qeb[:, ln])
            later = per_head(lambda h, ln: _colsum(dt_dec[h] * st0[h])) + _colsum(dg_minus)
            dlf = later + _tri_sum(triub, dg)
            df = dlf / f - dk
            dqfi_ref[rows, 0:width] = (dq * (sq * (1.0 + qr * (1.0 - sq)))).astype(dqfi_ref.dtype)
            dqfi_ref[rows, width:2 * width] = (df * (1.0 - lb) * sf * (1.0 - sf)).astype(dqfi_ref.dtype)
            dqfi_ref[rows, 2 * width:3 * width] = dv.astype(dqfi_ref.dtype)
            return dlb + _colsum(df * (1.0 - sf))

        dlb = lax.fori_loop(0, nc, chunk, jnp.zeros((1, width), F32))
        _accumulate(dlb_ref, dlb, first)

    assert hps == HEADS
    nt = n // t
    return _call(
        body, name=name, grid=(1, nt),
        in_specs=[sec(COL_Q), sec(COL_F), sec(COL_I), lbspec, head_tile, state],
        out_specs=[pl.BlockSpec((t, 3 * width), lambda h, i: (nt - 1 - i, 0)),
                   pl.BlockSpec((1, width), lambda h, i: (0, h))],
        out_shape=[jax.ShapeDtypeStruct((n, 3 * D_MODEL), BF16), jax.ShapeDtypeStruct((1, D_MODEL), F32)],
        args=[proj, proj, proj, hgrn_lb, do, states], semantics=("parallel", "arbitrary"),
        scratch=[pltpu.VMEM((hps, HEAD_DIM, HEAD_DIM), F32), pltpu.VMEM((CHUNK, width), F32),
                 pltpu.VMEM((hps, CHUNK, CHUNK), F32), pltpu.VMEM((CHUNK, width), F32),
                 pltpu.VMEM((CHUNK, width), F32)],
        exchange=exchange)


def _pool_fwd(proj, pool_w, pool_scale, *, name):
    n = proj.shape[0]
    t = _tile(n, 512, POOL_HALO)
    per = t // POOL_HALO
    c0 = COL_POOL // POOL_WIDTH

    def body(u_ref, halo_ref, pw_ref, ps_ref, pooled_ref, mixed_ref, ext):
        i = pl.program_id(0)
        u = u_ref[...].astype(F32)
        ext[POOL_HALO:POOL_HALO + t, :] = u
        ext[0:POOL_HALO, :] = jnp.where(i > 0, halo_ref[...].astype(F32), 0.0)
        pos = i * t + lax.broadcasted_iota(jnp.int32, (t, POOL_CH), 0) + 1
        for grp, win in enumerate(POOL_WINDOWS):
            cols = slice(grp * POOL_CH, (grp + 1) * POOL_CH)
            acc = u[:, cols]
            for j in range(1, win):
                acc = acc + ext[POOL_HALO - j:POOL_HALO - j + t, cols]
            pooled = (acc / jnp.minimum(pos, win).astype(F32) - u[:, cols]).astype(BF16)
            pooled_ref[:, cols] = pooled
            mixed_ref[:, cols] = (_dot(pooled, pw_ref[grp].astype(BF16)) * ps_ref[:, cols]).astype(BF16)

    tile = pl.BlockSpec((t, POOL_WIDTH), lambda i: (i, 0))
    return pl.pallas_call(
        body, name=name, grid=(n // t,),
        in_specs=[pl.BlockSpec((t, POOL_WIDTH), lambda i: (i, c0)),
                  pl.BlockSpec((POOL_HALO, POOL_WIDTH), lambda i: (jnp.maximum(i * per - 1, 0), c0)),
                  pl.BlockSpec((len(POOL_WINDOWS), POOL_CH, POOL_CH), lambda i: (0, 0, 0)),
                  pl.BlockSpec((1, POOL_WIDTH), lambda i: (0, 0))],
        out_specs=[tile, tile],
        out_shape=[jax.ShapeDtypeStruct((n, POOL_WIDTH), BF16), jax.ShapeDtypeStruct((n, POOL_WIDTH), BF16)],
        scratch_shapes=[pltpu.VMEM((t + POOL_HALO, POOL_WIDTH), F32)],
        compiler_params=_params("parallel"),
    )(proj, proj, pool_w, pool_scale)


def _pool_bwd(dmixed, pooled, pool_w, pool_scale, *, name):
    n = dmixed.shape[0]
    t = _tile(n, 512, POOL_HALO)
    per = t // POOL_HALO
    nb = n // t

    def body(dm_ref, dmh_ref, p_ref, pw_ref, ps_ref, du_ref, dpw_ref, dps_ref, ext):
        i = pl.program_id(0)

        @pl.when(i == 0)
        def _():
            dpw_ref[...] = jnp.zeros_like(dpw_ref)
            dps_ref[...] = jnp.zeros_like(dps_ref)

        dm, dmh = dm_ref[...], dmh_ref[...]
        pos = i * t + lax.broadcasted_iota(jnp.int32, (t, POOL_CH), 0) + 1
        for grp, win in enumerate(POOL_WINDOWS):
            cols = slice(grp * POOL_CH, (grp + 1) * POOL_CH)
            pwb = pw_ref[grp].astype(BF16)
            pb = p_ref[:, cols]
            scale = ps_ref[:, cols]
            dps_ref[:, cols] += _colsum(dm[:, cols] * _dot(pb, pwb))
            dpm = (dm[:, cols] * scale).astype(BF16)
            dpw_ref[grp] += _dot_tn(pb, dpm)
            dpool = _dot_nt(dpm, pwb)
            dpool_next = _dot_nt((dmh[:, cols] * scale).astype(BF16), pwb)
            ext[0:t, cols] = dpool / jnp.minimum(pos, win).astype(F32)
            ext[t:t + POOL_HALO, cols] = jnp.where(i < nb - 1, dpool_next * (1.0 / win), 0.0)
            acc = -dpool
            for j in range(win):
                acc = acc + ext[j:j + t, cols]
            du_ref[:, cols] = acc.astype(du_ref.dtype)

    tile = pl.BlockSpec((t, POOL_WIDTH), lambda i: (i, 0))
    return pl.pallas_call(
        body, name=name, grid=(nb,),
        in_specs=[tile, pl.BlockSpec((POOL_HALO, POOL_WIDTH), lambda i: (jnp.minimum((i + 1) * per, nb * per - 1), 0)),
                  tile, pl.BlockSpec((len(POOL_WINDOWS), POOL_CH, POOL_CH), lambda i: (0, 0, 0)),
                  pl.BlockSpec((1, POOL_WIDTH), lambda i: (0, 0))],
        out_specs=[tile, pl.BlockSpec((len(POOL_WINDOWS), POOL_CH, POOL_CH), lambda i: (0, 0, 0)),
                   pl.BlockSpec((1, POOL_WIDTH), lambda i: (0, 0))],
        out_shape=[jax.ShapeDtypeStruct((n, POOL_WIDTH), BF16),
                   jax.ShapeDtypeStruct((len(POOL_WINDOWS), POOL_CH, POOL_CH), F32),
                   jax.ShapeDtypeStruct((1, POOL_WIDTH), F32)],
        scratch_shapes=[pltpu.VMEM((t + POOL_HALO, POOL_WIDTH), F32)],
        compiler_params=_params("arbitrary"),
    )(dmixed, dmixed, pooled, pool_w, pool_scale)


def _adamw(w, g, m, v):
    m2 = ADAM_B1 * m + (1.0 - ADAM_B1) * g
    v2 = ADAM_B2 * v + (1.0 - ADAM_B2) * (g * g)
    m_hat = m2 * (1.0 / (1.0 - ADAM_B1 ** ADAM_STEP))
    v_hat = v2 * (1.0 / (1.0 - ADAM_B2 ** ADAM_STEP))
    delta = -ADAM_LR * (m_hat / (jnp.sqrt(v_hat) + ADAM_EPS) + ADAM_WD * w)
    return delta, m2, v2


def _adam_big(recv, w, m, v, *, name):
    r, c = w.shape
    tr = _tile(r, 256, 16)

    def body(recv_ref, w_ref, m_ref, v_ref, g_ref, d_ref, m2_ref, v2_ref):
        g = recv_ref[0].astype(F32)
        for i in range(1, N_DEV):
            g = g + recv_ref[i].astype(F32)
        delta, m2, v2 = _adamw(w_ref[...], g, m_ref[...], v_ref[...])
        g_ref[...] = g
        d_ref[...] = delta
        m2_ref[...] = m2
        v2_ref[...] = v2

    tile = pl.BlockSpec((tr, c), lambda i: (i, 0))
    out = jax.ShapeDtypeStruct((r, c), F32)
    return pl.pallas_call(
        body, name=name, grid=(r // tr,),
        in_specs=[pl.BlockSpec((N_DEV, tr, c), lambda i: (0, i, 0)), tile, tile, tile],
        out_specs=[tile] * 4, out_shape=[out] * 4, compiler_params=_params("parallel"),
    )(recv, w, m, v)


def _adam_small(parts, w, m, v, *, name):
    n = len(SMALL_PARAMS)

    def body(*refs):
        parts_r, w_r, m_r, v_r = (refs[i * n:(i + 1) * n] for i in range(4))
        outs = refs[4 * n:]
        for j, key in enumerate(SMALL_PARAMS):
            g = parts_r[j][0]
            for i in range(1, N_DEV):
                g = g + parts_r[j][i]
            w_ = w_r[j][...]
            if key == "hgrn_lb":
                s0 = 1.0 / (1.0 + jnp.exp(w_[1:2] - w_[0:1]))
                ga = g * s0 * (1.0 - s0)
                sign = jnp.where(lax.broadcasted_iota(jnp.int32, w_.shape, 0) == 0, 1.0, -1.0)
                g = sign * jnp.broadcast_to(ga, w_.shape)
            delta, m2, v2 = _adamw(w_, g, m_r[j][...], v_r[j][...])
            for q, val in enumerate((g, delta, m2, v2)):
                outs[q * n + j][...] = val

    out_shape = [jax.ShapeDtypeStruct(w[k].shape, F32) for _ in range(4) for k in SMALL_PARAMS]
    res = pl.pallas_call(body, name=name, out_shape=out_shape, compiler_params=_params())(
        *[t[k] for t in (parts, w, m, v) for k in SMALL_PARAMS])
    return [res[q * n:(q + 1) * n] for q in range(4)]


def _as_2d(a):
    return a.reshape(-1, a.shape[-1])


SPLIT_AXIS = dict(BIG_WEIGHTS)


def _gather_of(names, weights):
    return _Exchange([_shard_to_send(weights[k][0], SPLIT_AXIS[k]) for k in names], gather=True)


def _scatter_of(names, dfull):
    return _Exchange([_to_slots(dfull[k], SPLIT_AXIS[k]) for k in names], gather=False)


def _shard_to_send(w, axis):
    return (w.T if axis == 1 else w).astype(BF16)


def _to_slots(dw, axis):
    rows, cols = dw.shape
    return dw.reshape(N_DEV, rows // N_DEV, cols)


def _from_slots(gathered, axis):
    _, r, c = gathered.shape
    return gathered.reshape(N_DEV * r, c)


def kernel(x, p, ffn1_norm, ffn1_w1, ffn1_w3, ffn1_w2, mix_norm, w_in, hgrn_lb, hgrn_onorm, w_branch_a, pool_w, pool_scale, w_branch_b, w_out, ffn2_norm, ffn2_w1, ffn2_w3, ffn2_w2, ple_norm, ple_w_gate, ple_w_proj, ple_post_norm, final_norm, loss_target, m_ffn1_norm, m_ffn1_w1, m_ffn1_w3, m_ffn1_w2, m_mix_norm, m_w_in, m_hgrn_lb, m_hgrn_onorm, m_w_branch_a, m_pool_w, m_pool_scale, m_w_branch_b, m_w_out, m_ffn2_norm, m_ffn2_w1, m_ffn2_w3, m_ffn2_w2, m_ple_norm, m_ple_w_gate, m_ple_w_proj, m_ple_post_norm, m_final_norm, v_ffn1_norm, v_ffn1_w1, v_ffn1_w3, v_ffn1_w2, v_mix_norm, v_w_in, v_hgrn_lb, v_hgrn_onorm, v_w_branch_a, v_pool_w, v_pool_scale, v_w_branch_b, v_w_out, v_ffn2_norm, v_ffn2_w1, v_ffn2_w3, v_ffn2_w2, v_ple_norm, v_ple_w_gate, v_ple_w_proj, v_ple_post_norm, v_final_norm):
    weights = dict(ffn1_norm=ffn1_norm, ffn1_w1=ffn1_w1, ffn1_w3=ffn1_w3, ffn1_w2=ffn1_w2, mix_norm=mix_norm, w_in=w_in, hgrn_lb=hgrn_lb, hgrn_onorm=hgrn_onorm, w_branch_a=w_branch_a, pool_w=pool_w, pool_scale=pool_scale, w_branch_b=w_branch_b, w_out=w_out, ffn2_norm=ffn2_norm, ffn2_w1=ffn2_w1, ffn2_w3=ffn2_w3, ffn2_w2=ffn2_w2, ple_norm=ple_norm, ple_w_gate=ple_w_gate, ple_w_proj=ple_w_proj, ple_post_norm=ple_post_norm, final_norm=final_norm)
    mom1 = dict(ffn1_norm=m_ffn1_norm, ffn1_w1=m_ffn1_w1, ffn1_w3=m_ffn1_w3, ffn1_w2=m_ffn1_w2, mix_norm=m_mix_norm, w_in=m_w_in, hgrn_lb=m_hgrn_lb, hgrn_onorm=m_hgrn_onorm, w_branch_a=m_w_branch_a, pool_w=m_pool_w, pool_scale=m_pool_scale, w_branch_b=m_w_branch_b, w_out=m_w_out, ffn2_norm=m_ffn2_norm, ffn2_w1=m_ffn2_w1, ffn2_w3=m_ffn2_w3, ffn2_w2=m_ffn2_w2, ple_norm=m_ple_norm, ple_w_gate=m_ple_w_gate, ple_w_proj=m_ple_w_proj, ple_post_norm=m_ple_post_norm, final_norm=m_final_norm)
    mom2 = dict(ffn1_norm=v_ffn1_norm, ffn1_w1=v_ffn1_w1, ffn1_w3=v_ffn1_w3, ffn1_w2=v_ffn1_w2, mix_norm=v_mix_norm, w_in=v_w_in, hgrn_lb=v_hgrn_lb, hgrn_onorm=v_hgrn_onorm, w_branch_a=v_w_branch_a, pool_w=v_pool_w, pool_scale=v_pool_scale, w_branch_b=v_w_branch_b, w_out=v_w_out, ffn2_norm=v_ffn2_norm, ffn2_w1=v_ffn2_w1, ffn2_w3=v_ffn2_w3, ffn2_w2=v_ffn2_w2, ple_norm=v_ple_norm, ple_w_gate=v_ple_w_gate, ple_w_proj=v_ple_w_proj, ple_post_norm=v_ple_post_norm, final_norm=v_final_norm)

    xs = x[0]
    ps = p[0, 0].astype(BF16)
    tgt = loss_target[0]
    n = xs.shape[0]

    g_f1, g_mix, g_on, g_f2 = ffn1_norm, mix_norm, hgrn_onorm, ffn2_norm
    g_ple, g_post, g_fin = ple_norm, ple_post_norm, final_norm.reshape(1, D_MODEL)
    lb2 = hgrn_lb
    pw, pscale = pool_w[0], pool_scale

    full = {}

    def keep(names, gathered):
        for k, g in zip(names, gathered):
            full[k] = _from_slots(g, SPLIT_AXIS[k])

    names = ("ffn1_w1", "ffn1_w3")
    ex = _gather_of(names, weights)
    h1 = _rms_fwd(xs, g_f1, name="ffn1_rms", exchange=ex)
    keep(names, ex.received)
    names = ("ffn1_w2", "w_in")
    ex = _gather_of(names, weights)
    a1, b1, s1 = _ffn_up(h1, full["ffn1_w1"], full["ffn1_w3"], name="ffn1_up", exchange=ex)
    keep(names, ex.received)
    names = ("w_branch_a", "w_branch_b", "w_out")
    ex = _gather_of(names, weights)
    x1, h2 = _mm_nn_res_rms(s1, full["ffn1_w2"], xs, g_mix, name="ffn1_down", scale=0.5, exchange=ex)
    keep(names, ex.received)
    names = ("ffn2_w1", "ffn2_w3", "ffn2_w2", "ple_w_gate", "ple_w_proj")
    ex = _gather_of(names, weights)
    proj = _mm_nn_wide(h2, full["w_in"], name="w_in_proj", out_dtype=BF16, exchange=ex)
    keep(names, ex.received)
    o, states = _hgrn_fwd(proj, lb2, name="hgrn_fwd")
    on = _hgrn_post_fwd(o, proj, g_on, name="hgrn_post_fwd")
    ya = _mm_nn(on, full["w_branch_a"], name="branch_a", tn=1024, tm=512, out_dtype=BF16)
    pooled, mixed = _pool_fwd(proj, pw, pscale, name="pool_fwd")
    yb = _mm_nt([(mixed, full["w_branch_b"])], name="branch_b", tn=1024, tk=512, out_dtype=BF16)
    y = _merge_fwd(proj, ya, yb, name="merge_fwd")
    x2, h3 = _mm_nn_res_rms(y, full["w_out"], x1, g_f2, name="w_out_proj", scale=1.0)
    a2, b2, s2 = _ffn_up(h3, full["ffn2_w1"], full["ffn2_w3"], name="ffn2_up")
    x3, h4 = _mm_nn_res_rms(s2, full["ffn2_w2"], x2, g_ple, name="ffn2_down", scale=0.5)
    gpre = _mm_nn(h4, full["ple_w_gate"], name="ple_gate", tn=1024, tm=512, out_dtype=BF16)
    z = _mm_nt([(ps, full["ple_w_proj"])], name="ple_proj", tn=1024, tk=512, out_dtype=BF16)
    dx4, dpre, dz, loss_part, d_fin, d_post = _ple_final(x3, gpre, z, tgt, g_post, g_fin, name="ple_final")

    dfull, received = {}, {}

    def sent(names, exchange):
        received.update(zip(names, exchange.received))

    dfull["ple_w_proj"] = _mm_tn(ps, dz, name="d_ple_w_proj", tn=1024, tm=1024, transpose_out=True)
    dfull["ple_w_gate"] = _mm_tn(h4, dpre, name="d_ple_w_gate", tn=1024, tm=1024)
    dx3, dx3s, d_ple = _mm_nt_rms_bwd([(dpre, full["ple_w_gate"], False)], x3, dx4, g_ple, name="ple_rms_bwd", tn=512,
                                      half_scale=0.5)

    names = ("ple_w_proj", "ple_w_gate")
    ex = _scatter_of(names, dfull)
    da2, db2 = _ffn_bwd_mid(dx3s, full["ffn2_w2"], a2, b2, name="ffn2_bwd_mid", exchange=ex)
    sent(names, ex)
    dfull["ffn2_w2"] = _mm_tn(s2, dx3s, name="ffn2_dw2", tn=1024, tm=512)
    dfull["ffn2_w1"] = _mm_tn(h3, da2, name="ffn2_dw1", tn=1024, tm=1408, transpose_out=True)
    dfull["ffn2_w3"] = _mm_tn(h3, db2, name="ffn2_dw3", tn=1024, tm=1408, transpose_out=True)
    names = ("ffn2_w2",)
    ex = _scatter_of(names, dfull)
    dx2, dx2b, d_f2 = _mm_nt_rms_bwd([(da2, full["ffn2_w1"], True), (db2, full["ffn2_w3"], True)], x2, dx3, g_f2,
                                     name="ffn2_rms_bwd", tn=512, half_scale=1.0, exchange=ex)
    sent(names, ex)

    dfull["w_out"] = _mm_tn(y, dx2b, name="d_w_out", tn=1024, tm=1024)
    dy = _mm_nt([(dx2b, full["w_out"])], name="d_y", tn=1024, tk=512, out_dtype=BF16)
    dya, dyb, dga, dgb = _merge_bwd(dy, proj, ya, yb, name="merge_bwd")

    dfull["w_branch_b"] = _mm_tn(mixed, dyb, name="d_w_branch_b", tn=1024, tm=1024, transpose_out=True)
    dmixed = _mm_nn(dyb, full["w_branch_b"], name="d_mixed", tn=1024, tm=512)
    du, d_pw, d_ps = _pool_bwd(dmixed, pooled, pw, pscale, name="pool_bwd")

    dfull["w_branch_a"] = _mm_tn(on, dya, name="d_w_branch_a", tn=1024, tm=1024)
    don = _mm_nt([(dya, full["w_branch_a"])], name="d_on", tn=1024, tk=512, out_dtype=BF16)
    dog, do, d_on = _hgrn_post_bwd(don, o, proj, g_on, name="hgrn_post_bwd")
    names = ("ffn2_w1", "ffn2_w3", "w_out", "w_branch_b", "w_branch_a")
    ex = _scatter_of(names, dfull)
    dqfi, d_lb = _hgrn_bwd(proj, lb2, do, states, name="hgrn_bwd", exchange=ex)
    sent(names, ex)
    dproj = [dqfi, dog, du, dga, dgb]
    dfull["w_in"] = jnp.concatenate(
        [_mm_tn(h2, part, name=f"d_w_in_{j}", tn=1024, tm=1536, transpose_out=True) for j, part in enumerate(dproj)],
        axis=0)
    names = ("w_in",)
    ex = _scatter_of(names, dfull)
    dx1, dx1s, d_mix = _mm_nt_rms_bwd([(dproj, full["w_in"], True)], x1, dx2, g_mix, name="mix_rms_bwd", tn=512,
                                      half_scale=0.5, exchange=ex)
    sent(names, ex)

    da1, db1 = _ffn_bwd_mid(dx1s, full["ffn1_w2"], a1, b1, name="ffn1_bwd_mid")
    dfull["ffn1_w2"] = _mm_tn(s1, dx1s, name="ffn1_dw2", tn=1024, tm=512)
    names = ("ffn1_w2",)
    ex = _scatter_of(names, dfull)
    dfull["ffn1_w1"] = _mm_tn(h1, da1, name="ffn1_dw1", tn=1024, tm=1408, transpose_out=True, exchange=ex)
    sent(names, ex)
    names = ("ffn1_w1",)
    ex = _scatter_of(names, dfull)
    dfull["ffn1_w3"] = _mm_tn(h1, db1, name="ffn1_dw3", tn=1024, tm=1408, transpose_out=True, exchange=ex)
    sent(names, ex)
    names = ("ffn1_w3",)
    ex = _scatter_of(names, dfull)
    grad_x, _, d_f1 = _mm_nt_rms_bwd([(da1, full["ffn1_w1"], True), (db1, full["ffn1_w3"], True)], xs, dx1, g_f1,
                                     name="ffn1_rms_bwd", tn=512, half_scale=1.0, exchange=ex)
    sent(names, ex)

    small_part = dict(ffn1_norm=d_f1, mix_norm=d_mix, hgrn_onorm=d_on, ffn2_norm=d_f2, ple_norm=d_ple,
                      ple_post_norm=d_post, final_norm=d_fin, hgrn_lb=d_lb, pool_scale=d_ps, pool_w=_as_2d(d_pw))
    gathered_small = _exchange_now([small_part[k] for k in SMALL_PARAMS] + [loss_part], name="gather_small_grads",
                                   gather=True)
    small_all = dict(zip(SMALL_PARAMS, gathered_small))
    loss = jnp.sum(gathered_small[-1])

    grads, deltas, new_m, new_v = {}, {}, {}, {}
    for name, axis in BIG_WEIGHTS:
        shape, recv = weights[name].shape, received[name]
        own = [t[name][0].T if axis == 1 else t[name][0] for t in (weights, mom1, mom2)]
        res = _adam_big(recv, *own, name=f"adam_{name}")
        grads[name], deltas[name], new_m[name], new_v[name] = [(r.T if axis == 1 else r).reshape(shape) for r in res]
    res = _adam_small(small_all, *[{k: _as_2d(t[k]) for k in SMALL_PARAMS} for t in (weights, mom1, mom2)],
                      name="adam_small")
    for store, vals in zip((grads, deltas, new_m, new_v), res):
        store.update({k: val.reshape(weights[k].shape) for k, val in zip(SMALL_PARAMS, vals)})

    return (loss, grad_x.reshape(x.shape), *[grads[k] for k in WEIGHT_ORDER], *[deltas[k] for k in WEIGHT_ORDER],
            *[new_m[k] for k in WEIGHT_ORDER], *[new_v[k] for k in WEIGHT_ORDER])
```

```python
import jax
import jax.numpy as jnp
from jax import lax
from jax.experimental import pallas as pl
from jax.experimental.pallas import tpu as pltpu

F32 = jnp.float32
BF16 = jnp.bfloat16

N_DEV = 8
D_MODEL = 1024
HEADS = 8
HEAD_DIM = 128
POOL_WINDOWS = (2, 4, 8, 16)
POOL_CH = 128
POOL_WIDTH = 512
POOL_HALO = 16
RMS_EPS = 1e-6
CHUNK = 64
SUB = 32
HGRN_HEADS_PER_STEP = 8
NEG_BIG = -1e30

ADAM_LR = 0.001
ADAM_B1 = 0.9
ADAM_B2 = 0.999
ADAM_EPS = 1e-08
ADAM_WD = 0.01
ADAM_STEP = 10

V7X_VMEM_BYTES = 64 * 1024 * 1024
VMEM_LIMIT = (V7X_VMEM_BYTES * 3) // 4
EXCHANGE_TAIL_STEPS = 3
ROW_TILE_CAP = 8192

COL_Q, COL_F, COL_I, COL_OG, COL_POOL, COL_GA, COL_GB = 0, 1024, 2048, 3072, 4096, 4608, 5632

BIG_WEIGHTS = (
    ("ffn1_w1", 1), ("ffn1_w3", 1), ("ffn1_w2", 0), ("w_in", 1), ("w_branch_a", 0), ("w_branch_b", 1),
    ("w_out", 0), ("ffn2_w1", 1), ("ffn2_w3", 1), ("ffn2_w2", 0), ("ple_w_gate", 0), ("ple_w_proj", 1),
)
SMALL_PARAMS = ("ffn1_norm", "mix_norm", "hgrn_onorm", "ffn2_norm", "ple_norm", "ple_post_norm", "final_norm",
                "hgrn_lb", "pool_scale", "pool_w")
WEIGHT_ORDER = (
    "ffn1_norm", "ffn1_w1", "ffn1_w3", "ffn1_w2", "mix_norm", "w_in", "hgrn_lb", "hgrn_onorm", "w_branch_a", "pool_w",
    "pool_scale", "w_branch_b", "w_out", "ffn2_norm", "ffn2_w1", "ffn2_w3", "ffn2_w2", "ple_norm", "ple_w_gate",
    "ple_w_proj", "ple_post_norm", "final_norm",
)


def _params(*sem):
    return pltpu.CompilerParams(dimension_semantics=sem if sem else None, vmem_limit_bytes=VMEM_LIMIT)


COL_CHUNK = 256


def _rows(tn, width):
    return pl.BlockSpec((tn, width), lambda i: (i, 0))


def _resident(shape):
    return pl.BlockSpec(shape, lambda i: (0,) * len(shape), pipeline_mode=pl.Buffered(1))


def _dot(a, b):
    return jnp.dot(a, b, preferred_element_type=F32)


def _dot_nt(a, b):
    return lax.dot_general(a, b, (((1,), (1,)), ((), ())), preferred_element_type=F32)


def _dot_tn(a, b):
    return lax.dot_general(a, b, (((0,), (0,)), ((), ())), preferred_element_type=F32)


def _sigmoid(x):
    return 0.5 * jnp.tanh(0.5 * x) + 0.5


def _tile(n, want, mult):
    if mult != 128:
        want = min(want, ROW_TILE_CAP)
    if n <= want:
        return n
    t = (want // mult) * mult
    while t > mult and n % t:
        t -= mult
    assert n % t == 0, (n, want, mult)
    return t


class _Exchange:
    COPIES = N_DEV - 1

    def __init__(self, arrs, gather):
        self.arrs, self.gather, self.n = list(arrs), gather, len(arrs)
        self.out_shape = [jax.ShapeDtypeStruct((N_DEV,) + (a.shape if gather else a.shape[1:]), a.dtype) for a in arrs]
        self.scratch = [pltpu.SemaphoreType.DMA((self.n * self.COPIES,)),
                        pltpu.SemaphoreType.DMA((self.n * self.COPIES,)), pltpu.SemaphoreType.DMA((self.n,))]
        self.received = None

    @staticmethod
    def _place():
        x, y, c = lax.axis_index("x"), lax.axis_index("y"), lax.axis_index("c")
        return x, y, c

    def _copy(self, a, k, src, dst, to, sems):
        s = a * self.COPIES + k
        return pltpu.make_async_remote_copy(src_ref=src, dst_ref=dst, send_sem=sems[0].at[s], recv_sem=sems[1].at[s],
                                            device_id=to, device_id_type=pl.DeviceIdType.MESH)

    def _gather_copies(self, ins, outs, sems):
        x, y, c = self._place()
        chips = [(1 - x, y), (x, 1 - y), (1 - x, 1 - y)]
        slot = lambda px, py, pc: 4 * px + 2 * py + pc
        first, passed, arrivals = [], [], []
        for a in range(self.n):
            mine = outs[a].at[slot(x, y, c)]
            first.append(self._copy(a, 0, ins[a], mine, (x, y, 1 - c), sems))
            arrivals.append(self._copy(a, 0, ins[a], outs[a].at[slot(x, y, 1 - c)], (x, y, 1 - c), sems))
            for j, (px, py) in enumerate(chips):
                first.append(self._copy(a, 1 + j, ins[a], mine, (px, py, c), sems))
                theirs = outs[a].at[slot(px, py, c)]
                passed.append((self._copy(a, 1 + j, ins[a], theirs, (px, py, c), sems),
                               self._copy(a, 4 + j, theirs, theirs, (x, y, 1 - c), sems)))
                arrivals.append(self._copy(a, 4 + j, ins[a], outs[a].at[slot(px, py, 1 - c)], (x, y, 1 - c), sems))
        return first, passed, arrivals

    def _scatter_copies(self, ins, outs, sems):
        x, y, c = self._place()
        me = 4 * x + 2 * y + c
        sends, arrivals = [], []
        for k in range(1, N_DEV):
            px = 1 - x if k & 4 else x
            py = 1 - y if k & 2 else y
            pc = 1 - c if k & 1 else c
            peer = 4 * px + 2 * py + pc
            for a in range(self.n):
                sends.append(self._copy(a, k - 1, ins[a].at[peer], outs[a].at[me], (px, py, pc), sems))
                arrivals.append(self._copy(a, k - 1, ins[a].at[peer], outs[a].at[peer], (px, py, pc), sems))
        return sends, arrivals

    def _local(self, ins, outs, sems):
        x, y, c = self._place()
        me = 4 * x + 2 * y + c
        return [pltpu.make_async_copy(ins[a] if self.gather else ins[a].at[me], outs[a].at[me], sems[2].at[a])
                for a in range(self.n)]

    def start(self, ins, outs, sems):
        for cp in self._local(ins, outs, sems):
            cp.start()
        sends = self._gather_copies(ins, outs, sems)[0] if self.gather else self._scatter_copies(ins, outs, sems)[0]
        for cp in sends:
            cp.start()

    def pass_on(self, ins, outs, sems):
        if self.gather:
            for landed, onward in self._gather_copies(ins, outs, sems)[1]:
                landed.wait_recv()
                onward.start()

    def finish(self, ins, outs, sems):
        if self.gather:
            first, passed, arrivals = self._gather_copies(ins, outs, sems)
            sends = first + [onward for _, onward in passed]
        else:
            sends, arrivals = self._scatter_copies(ins, outs, sems)
        for cp in arrivals:
            cp.wait_recv()
        for cp in sends:
            cp.wait_send()
        for cp in self._local(ins, outs, sems):
            cp.wait()


def _call(body, *, name, grid, in_specs, out_specs, out_shape, args, semantics, scratch=(), exchange=None):
    if exchange is None:
        return pl.pallas_call(
            body, name=name, grid=grid, in_specs=in_specs, out_specs=out_specs, out_shape=out_shape,
            scratch_shapes=list(scratch), compiler_params=_params(*semantics))(*args)
    ex = exchange
    n_in, n_out, n_s = len(in_specs), len(out_specs), len(scratch)

    def wrapped(*refs):
        ins, ex_in = refs[:n_in], refs[n_in:n_in + ex.n]
        o0 = n_in + ex.n
        outs, ex_out = refs[o0:o0 + n_out], refs[o0 + n_out:o0 + n_out + ex.n]
        s0 = o0 + n_out + ex.n
        scr, sems = refs[s0:s0 + n_s], refs[s0 + n_s:]
        step, steps = pl.program_id(0), grid[0]
        for ax in range(1, len(grid)):
            step, steps = step * grid[ax] + pl.program_id(ax), steps * grid[ax]

        @pl.when(step == 0)
        def _():
            ex.start(ex_in, ex_out, sems)

        body(*ins, *outs, *scr)

        @pl.when(step == max(steps - EXCHANGE_TAIL_STEPS, 0))
        def _():
            ex.pass_on(ex_in, ex_out, sems)

        @pl.when(step == steps - 1)
        def _():
            ex.finish(ex_in, ex_out, sems)

    hbm = pl.BlockSpec(memory_space=pltpu.HBM)
    res = pl.pallas_call(
        wrapped, name=name, grid=grid, in_specs=list(in_specs) + [hbm] * ex.n,
        out_specs=list(out_specs) + [hbm] * ex.n, out_shape=list(out_shape) + ex.out_shape,
        scratch_shapes=list(scratch) + ex.scratch, compiler_params=_params(*(["arbitrary"] * len(grid))),
    )(*args, *ex.arrs)
    ex.received = res[n_out:]
    return res[:n_out]


def _exchange_now(arrs, *, name, gather):
    ex = _Exchange(arrs, gather)
    n = ex.n

    def body(*refs):
        ex.start(refs[:n], refs[n:2 * n], refs[2 * n:])
        ex.pass_on(refs[:n], refs[n:2 * n], refs[2 * n:])
        ex.finish(refs[:n], refs[n:2 * n], refs[2 * n:])

    hbm = pl.BlockSpec(memory_space=pltpu.HBM)
    return pl.pallas_call(body, name=name, out_shape=ex.out_shape, in_specs=[hbm] * n, out_specs=[hbm] * n,
                          scratch_shapes=ex.scratch)(*arrs)


def _mm_tn(a, b, *, name, tn, tm, transpose_out=False, exchange=None):
    n, k = a.shape
    m = b.shape[1]
    tn, tm = _tile(n, tn, 16), _tile(m, tm, 128)
    steps = n // tn

    def body(a_ref, b_ref, o_ref, acc):
        i = pl.program_id(1)

        @pl.when(i == 0)
        def _():
            acc[...] = jnp.zeros_like(acc)

        acc[...] += _dot_tn(a_ref[...], b_ref[...])

        @pl.when(i == steps - 1)
        def _():
            res = acc[...]
            o_ref[...] = (res.T if transpose_out else res).astype(o_ref.dtype)

    if transpose_out:
        out_spec, out_shape = pl.BlockSpec((tm, k), lambda j, i: (j, 0)), jax.ShapeDtypeStruct((m, k), BF16)
    else:
        out_spec, out_shape = pl.BlockSpec((k, tm), lambda j, i: (0, j)), jax.ShapeDtypeStruct((k, m), BF16)
    return _call(body, name=name, grid=(m // tm, steps),
                 in_specs=[pl.BlockSpec((tn, k), lambda j, i: (i, 0)), pl.BlockSpec((tn, tm), lambda j, i: (i, j))],
                 out_specs=[out_spec], out_shape=[out_shape], args=[a, b], semantics=("parallel", "arbitrary"),
                 scratch=[pltpu.VMEM((k, tm), F32)], exchange=exchange)[0]


def _ffn_up(h, w1, w3, *, name, tn=512, exchange=None):
    n, k = h.shape
    m = w1.shape[0]
    tn = _tile(n, tn, 16)

    def body(h_ref, w1_ref, w3_ref, dsda_ref, dsdb_ref, s_ref):
        for c0 in range(0, m, COL_CHUNK):
            cols = slice(c0, c0 + COL_CHUNK)
            a = _dot_nt(h_ref[...], w1_ref[cols, :])
            b = _dot_nt(h_ref[...], w3_ref[cols, :])
            sg = _sigmoid(a)
            silu = a * sg
            dsda_ref[:, cols] = (b * (sg + silu * (1.0 - sg))).astype(dsda_ref.dtype)
            dsdb_ref[:, cols] = silu.astype(dsdb_ref.dtype)
            s_ref[:, cols] = (silu * b).astype(s_ref.dtype)

    ospec = _rows(tn, m)
    return _call(body, name=name, grid=(n // tn,),
                 in_specs=[_rows(tn, k), _resident(w1.shape), _resident(w3.shape)], out_specs=[ospec, ospec, ospec],
                 out_shape=[jax.ShapeDtypeStruct((n, m), BF16)] * 3,
                 args=[h, w1, w3], semantics=("parallel",), exchange=exchange)


def _mm_nn_wide(a, b, *, name, out_dtype, b_is_km=False, exchange=None):
    n, k = a.shape
    m = b.shape[1 if b_is_km else 0]
    tn = _tile(n, 512, 16)
    chunk = min(2 * COL_CHUNK, m)

    def body(a_ref, b_ref, o_ref):
        for c0 in range(0, m, chunk):
            cols = slice(c0, c0 + chunk)
            if b_is_km:
                res = _dot(a_ref[...], b_ref[:, cols])
            else:
                res = _dot_nt(a_ref[...], b_ref[cols, :])
            o_ref[:, cols] = res.astype(o_ref.dtype)

    return _call(body, name=name, grid=(n // tn,), in_specs=[_rows(tn, k), _resident(b.shape)],
                 out_specs=[_rows(tn, m)], out_shape=[jax.ShapeDtypeStruct((n, m), out_dtype)], args=[a, b],
                 semantics=("parallel",), exchange=exchange)[0]


def _mm_nn_res_rms(a, b, res, g, *, name, scale, exchange=None):
    n, k = a.shape
    d = b.shape[1]
    tn = _tile(n, 512, 16)

    def body(a_ref, b_ref, r_ref, g_ref, x_ref, h_ref):
        for c0 in range(0, d, COL_CHUNK):
            cols = slice(c0, c0 + COL_CHUNK)
            x_ref[:, cols] = r_ref[:, cols] + scale * _dot(a_ref[...], b_ref[:, cols])
        x = x_ref[...]
        r = lax.rsqrt(_rowmean(x * x) + RMS_EPS)
        h_ref[...] = (x * r * g_ref[...]).astype(h_ref.dtype)

    row = _rows(tn, d)
    return _call(body, name=name, grid=(n // tn,),
                 in_specs=[_rows(tn, k), _resident(b.shape), row, pl.BlockSpec((1, d), lambda i: (0, 0))],
                 out_specs=[row, row],
                 out_shape=[jax.ShapeDtypeStruct((n, d), F32), jax.ShapeDtypeStruct((n, d), BF16)],
                 args=[a, b, res, g], semantics=("parallel",), exchange=exchange)


def _ffn_bwd_mid(dxs, w2, dsda, dsdb, *, name, exchange=None):
    n, d = dxs.shape
    m = w2.shape[0]
    tn = _tile(n, 512, 16)

    def body(dx_ref, w2_ref, dsda_ref, dsdb_ref, da_ref, db_ref):
        for c0 in range(0, m, COL_CHUNK):
            cols = slice(c0, c0 + COL_CHUNK)
            ds = _dot_nt(dx_ref[...], w2_ref[cols, :])
            da_ref[:, cols] = (ds * dsda_ref[:, cols].astype(F32)).astype(da_ref.dtype)
            db_ref[:, cols] = (ds * dsdb_ref[:, cols].astype(F32)).astype(db_ref.dtype)

    tile = _rows(tn, m)
    return _call(body, name=name, grid=(n // tn,),
                 in_specs=[_rows(tn, d), _resident(w2.shape), tile, tile], out_specs=[tile, tile],
                 out_shape=[jax.ShapeDtypeStruct((n, m), BF16), jax.ShapeDtypeStruct((n, m), BF16)],
                 args=[dxs, w2, dsda, dsdb], semantics=("parallel",), exchange=exchange)


def _rowwise(fn, *, name, n, tn, ncol, rows, vecs, outs, accs=(), exchange=None):
    tn = _tile(n, tn, 16)
    nr, nv, no = len(rows), len(vecs), len(outs)

    def body(*refs):
        first = pl.program_id(1) == 0
        vals = [r[...].astype(F32) for r in refs[:nr + nv]]
        res = fn(*vals)
        for ref, val in zip(refs[nr + nv:nr + nv + no], res[:no]):
            ref[...] = val.astype(ref.dtype)
        for ref, val in zip(refs[nr + nv + no:], res[no:]):
            _accumulate(ref, val, first)

    in_specs = [pl.BlockSpec((tn, w), lambda j, i, c0=c0: (i, c0 + j)) for _, w, c0 in rows]
    in_specs += [pl.BlockSpec((1, w), lambda j, i, c0=c0: (0, c0 + j)) for _, w, c0 in vecs]
    out_specs = [pl.BlockSpec((tn, w), lambda j, i: (i, j)) for _, w, _ in outs]
    out_specs += [pl.BlockSpec((1, w), lambda j, i: (0, j)) for _, w in accs]
    out_shape = [jax.ShapeDtypeStruct((n, tw), dt) for tw, _, dt in outs]
    out_shape += [jax.ShapeDtypeStruct((1, tw), F32) for tw, _ in accs]
    return _call(body, name=name, grid=(ncol, n // tn), in_specs=in_specs, out_specs=out_specs, out_shape=out_shape,
                 args=[r[0] for r in rows] + [v[0] for v in vecs], semantics=("parallel", "arbitrary"),
                 exchange=exchange)


def _accumulate(ref, val, first):
    @pl.when(first)
    def _():
        ref[...] = jnp.zeros_like(ref)

    ref[...] += val


def _colsum(x):
    return jnp.sum(x, axis=0, keepdims=True)


def _rowmean(x):
    return jnp.mean(x, axis=-1, keepdims=True)


def _rms_fwd(x, g, *, name, exchange=None):
    def fn(x_, g_):
        r = lax.rsqrt(_rowmean(x_ * x_) + RMS_EPS)
        return (x_ * r * g_,)

    n, d = x.shape
    return _rowwise(fn, name=name, n=n, tn=512, ncol=1, rows=[(x, d, 0)], vecs=[(g, d, 0)], outs=[(d, d, BF16)],
                    exchange=exchange)[0]


def _mm_nt_rms_bwd(pairs, x, extra, g, *, name, tn, half_scale, exchange=None):
    n, d = x.shape
    tn = _tile(n, tn, 16)
    pairs = [(list(a) if isinstance(a, (list, tuple)) else [a], b, t) for a, b, t in pairs]
    nref = sum(len(a) + 1 for a, _, _ in pairs)

    def body(*refs):
        x_ref, e_ref, g_ref, dx_ref, dxs_ref, dg_ref = refs[nref:]
        dh, at = None, 0
        for parts, _, transposed in pairs:
            b_ref = refs[at + len(parts)]
            col = 0
            for j, part in enumerate(parts):
                w = part.shape[1]
                if transposed:
                    term = _dot(refs[at + j][...], b_ref[col:col + w, :])
                else:
                    term = _dot_nt(refs[at + j][...], b_ref[:, col:col + w])
                dh = term if dh is None else dh + term
                col += w
            at += len(parts) + 1
        x_ = x_ref[...]
        r = lax.rsqrt(_rowmean(x_ * x_) + RMS_EPS)
        xh = x_ * r
        dxh = dh * g_ref[...]
        dx = e_ref[...] + r * (dxh - xh * _rowmean(dxh * xh))
        dx_ref[...] = dx
        dxs_ref[...] = (dx * half_scale).astype(dxs_ref.dtype)
        _accumulate(dg_ref, _colsum(dh * xh), pl.program_id(0) == 0)

    in_specs, args = [], []
    for parts, b, transposed in pairs:
        assert sum(part.shape[1] for part in parts) == b.shape[0 if transposed else 1]
        in_specs += [pl.BlockSpec((tn, part.shape[1]), lambda i: (i, 0)) for part in parts]
        in_specs.append(pl.BlockSpec(b.shape, lambda i: (0, 0), pipeline_mode=pl.Buffered(1)))
        args += parts + [b]
    row = pl.BlockSpec((tn, d), lambda i: (i, 0))
    vec = pl.BlockSpec((1, d), lambda i: (0, 0))
    return _call(body, name=name, grid=(n // tn,), in_specs=in_specs + [row, row, vec], out_specs=[row, row, vec],
                 out_shape=[jax.ShapeDtypeStruct((n, d), F32), jax.ShapeDtypeStruct((n, d), BF16),
                            jax.ShapeDtypeStruct((1, d), F32)],
                 args=args + [x, extra, g], semantics=("arbitrary",), exchange=exchange)


def _ple_final(x3, gpre, z, tgt, gpp, gf, *, name):
    def fn(x3_, gpre_, z_, tgt_, gpp_, gf_):
        gate = _sigmoid(gpre_)
        rz = lax.rsqrt(_rowmean(z_ * z_) + RMS_EPS)
        zh = z_ * rz
        e = zh * gpp_
        x4 = x3_ + gate * e
        r4 = lax.rsqrt(_rowmean(x4 * x4) + RMS_EPS)
        x4h = x4 * r4
        diff = x4h * gf_ - tgt_
        dout = diff * (1.0 / D_MODEL)
        dxh4 = dout * gf_
        dx4 = r4 * (dxh4 - x4h * _rowmean(dxh4 * x4h))
        dpre = dx4 * e * gate * (1.0 - gate)
        de = dx4 * gate
        dzh = de * gpp_
        dz = rz * (dzh - zh * _rowmean(dzh * zh))
        return dx4, dpre, dz, _colsum(diff * diff) * (0.5 / D_MODEL), _colsum(dout * x4h), _colsum(de * zh)

    n, d = x3.shape
    return _rowwise(fn, name=name, n=n, tn=256, ncol=1, rows=[(x3, d, 0), (gpre, d, 0), (z, d, 0), (tgt, d, 0)],
                    vecs=[(gpp, d, 0), (gf, d, 0)], outs=[(d, d, F32), (d, d, BF16), (d, d, BF16)],
                    accs=[(d, d), (d, d), (d, d)])


def _merge_fwd(proj, ya, yb, *, name):
    def fn(ga, gb, ya_, yb_):
        return (_sigmoid(ga) * ya_ + _sigmoid(gb) * yb_,)

    n = proj.shape[0]
    w = 512
    return _rowwise(fn, name=name, n=n, tn=512, ncol=D_MODEL // w,
                    rows=[(proj, w, COL_GA // w), (proj, w, COL_GB // w), (ya, w, 0), (yb, w, 0)], vecs=[],
                    outs=[(D_MODEL, w, BF16)])[0]


def _merge_bwd(dy, proj, ya, yb, *, name):
    def fn(dy_, ga, gb, ya_, yb_):
        sa, sb = _sigmoid(ga), _sigmoid(gb)
        return dy_ * sa, dy_ * sb, dy_ * ya_ * sa * (1.0 - sa), dy_ * yb_ * sb * (1.0 - sb)

    n = proj.shape[0]
    w = 512
    return _rowwise(fn, name=name, n=n, tn=512, ncol=D_MODEL // w,
                    rows=[(dy, w, 0), (proj, w, COL_GA // w), (proj, w, COL_GB // w), (ya, w, 0), (yb, w, 0)],
                    vecs=[], outs=[(D_MODEL, w, BF16)] * 4)


def _head_mean(x):
    return jnp.concatenate(
        [jnp.broadcast_to(jnp.mean(x[:, h * HEAD_DIM:(h + 1) * HEAD_DIM], axis=-1, keepdims=True),
                          (x.shape[0], HEAD_DIM)) for h in range(HEADS)], axis=1)


def _hgrn_post_fwd(o, proj, onorm, *, name):
    def fn(o_, og, gam):
        r = lax.rsqrt(_head_mean(o_ * o_) + RMS_EPS)
        return (o_ * r * gam * (og * _sigmoid(og)),)

    n = o.shape[0]
    w = D_MODEL
    return _rowwise(fn, name=name, n=n, tn=256, ncol=1, rows=[(o, w, 0), (proj, w, COL_OG // w)],
                    vecs=[(onorm, w, 0)], outs=[(D_MODEL, w, BF16)])[0]


def _hgrn_post_bwd(don, o, proj, onorm, *, name):
    def fn(don_, o_, og, gam):
        r = lax.rsqrt(_head_mean(o_ * o_) + RMS_EPS)
        oh = o_ * r
        sg = _sigmoid(og)
        dog = don_ * oh * gam * (sg * (1.0 + og * (1.0 - sg)))
        dn = don_ * (og * sg)
        doh = dn * gam
        do = r * (doh - oh * _head_mean(doh * oh))
        return dog, do, _colsum(dn * oh)

    n = o.shape[0]
    w = D_MODEL
    return _rowwise(fn, name=name, n=n, tn=256, ncol=1, rows=[(don, w, 0), (o, w, 0), (proj, w, COL_OG // w)],
                    vecs=[(onorm, w, 0)], outs=[(D_MODEL, w, BF16), (D_MODEL, w, BF16)], accs=[(D_MODEL, w)])


def _tri_sum(tri, x):
    hi = x.astype(BF16)
    lo = (x - hi.astype(F32)).astype(BF16)
    return _dot(tri, hi) + _dot(tri, lo)


def _lower_bound(lb_ref):
    return 1.0 / (1.0 + jnp.exp(lb_ref[1:2, :] - lb_ref[0:1, :]))


def _hgrn_specs(n, t, reverse):
    nt = n // t
    width = HGRN_HEADS_PER_STEP * HEAD_DIM

    def tok(i):
        return nt - 1 - i if reverse else i

    def sec(col):
        c0 = col // width
        return pl.BlockSpec((t, width), lambda h, i: (tok(i), c0 + h))

    head_tile = pl.BlockSpec((t, width), lambda h, i: (tok(i), h))
    state = pl.BlockSpec((HGRN_HEADS_PER_STEP, t // CHUNK, HEAD_DIM, HEAD_DIM), lambda h, i: (h, tok(i), 0, 0))
    lb = pl.BlockSpec((2, width), lambda h, i: (0, h))
    return sec, head_tile, state, lb


def _hgrn_fwd(proj, hgrn_lb, *, name):
    n = proj.shape[0]
    t = _tile(n, 512, CHUNK)
    nc = t // CHUNK
    hps = HGRN_HEADS_PER_STEP
    width = hps * HEAD_DIM
    lanes = [slice(h * HEAD_DIM, (h + 1) * HEAD_DIM) for h in range(hps)]
    sec, head_tile, state, lbspec = _hgrn_specs(n, t, False)

    def body(q_ref, f_ref, i_ref, lb_ref, o_ref, st_ref, s_acc, g_s, a_s):
        @pl.when(pl.program_id(1) == 0)
        def _():
            s_acc[...] = jnp.zeros_like(s_acc)

        lb = _lower_bound(lb_ref)
        row = lax.broadcasted_iota(jnp.int32, (CHUNK, CHUNK), 0)
        col = lax.broadcasted_iota(jnp.int32, (CHUNK, CHUNK), 1)
        tril = row >= col
        trilb = jnp.where(tril, 1.0, 0.0).astype(BF16)
        rowk = lax.broadcasted_iota(jnp.int32, (CHUNK, width), 0)

        def chunk(c, carry):
            rows = pl.ds(pl.multiple_of(c * CHUNK, CHUNK), CHUNK)
            qr, fr, v = [r[rows, :].astype(F32) for r in (q_ref, f_ref, i_ref)]
            q = qr * _sigmoid(qr)
            f = lb + (1.0 - lb) * _sigmoid(fr)
            k = 1.0 - f
            g = _tri_sum(trilb, jnp.log(f))
            g_s[...] = g
            st0 = [s_acc[h] for h in range(hps)]
            for h in range(hps):
                st_ref[h, c] = st0[h]
            vb = v.astype(BF16)
            for blk in range(CHUNK // SUB):
                lo, hi = blk * SUB, (blk + 1) * SUB
                gref = g_s[lo - 1:lo, :] if blk else jnp.zeros((1, width), F32)
                qi = (q[lo:hi] * jnp.exp(g[lo:hi] - gref)).astype(BF16)
                ki = (k * jnp.exp(jnp.where(rowk < hi, gref - g, NEG_BIG))).astype(BF16)
                for h, ln in enumerate(lanes):
                    a_s[h, lo:hi, :] = _dot_nt(qi[:, ln], ki[:, ln])
            qeb = (q * jnp.exp(g)).astype(BF16)
            o_ref[rows, :] = jnp.concatenate(
                [_dot(jnp.where(tril, a_s[h], 0.0).astype(BF16), vb[:, ln]) + _dot_nt(qeb[:, ln], st0[h].astype(BF16))
                 for h, ln in enumerate(lanes)], axis=1).astype(o_ref.dtype)
            glast = g_s[CHUNK - 1:CHUNK, :]
            kdb = (k * jnp.exp(glast - g)).astype(BF16)
            dec = jnp.exp(glast)
            for h, ln in enumerate(lanes):
                s_acc[h] = st0[h] * dec[:, ln] + _dot_tn(vb[:, ln], kdb[:, ln])
            return carry

        lax.fori_loop(0, nc, chunk, 0)

    return pl.pallas_call(
        body, name=name, grid=(HEADS // hps, n // t),
        in_specs=[sec(COL_Q), sec(COL_F), sec(COL_I), lbspec], out_specs=[head_tile, state],
        out_shape=[jax.ShapeDtypeStruct((n, D_MODEL), BF16),
                   jax.ShapeDtypeStruct((HEADS, n // CHUNK, HEAD_DIM, HEAD_DIM), F32)],
        scratch_shapes=[pltpu.VMEM((hps, HEAD_DIM, HEAD_DIM), F32), pltpu.VMEM((CHUNK, width), F32),
                        pltpu.VMEM((hps, CHUNK, CHUNK), F32)],
        compiler_params=_params("parallel", "arbitrary"),
    )(proj, proj, proj, hgrn_lb)


def _hgrn_bwd(proj, hgrn_lb, do, states, *, name, exchange=None):
    n = proj.shape[0]
    t = _tile(n, 512, CHUNK)
    nc = t // CHUNK
    hps = HGRN_HEADS_PER_STEP
    width = hps * HEAD_DIM
    lanes = [slice(h * HEAD_DIM, (h + 1) * HEAD_DIM) for h in range(hps)]
    sec, head_tile, state, lbspec = _hgrn_specs(n, t, True)

    def body(q_ref, f_ref, i_ref, lb_ref, do_ref, st_ref, dqfi_ref, dlb_ref, d_acc, g_s, a_s, dq_s,
             dg_s):
        first = pl.program_id(1) == 0

        @pl.when(first)
        def _():
            d_acc[...] = jnp.zeros_like(d_acc)

        lb = _lower_bound(lb_ref)
        row = lax.broadcasted_iota(jnp.int32, (CHUNK, CHUNK), 0)
        col = lax.broadcasted_iota(jnp.int32, (CHUNK, CHUNK), 1)
        tril = row >= col
        trilb = jnp.where(tril, 1.0, 0.0).astype(BF16)
        triub = jnp.where(row <= col, 1.0, 0.0).astype(BF16)
        rowk = lax.broadcasted_iota(jnp.int32, (CHUNK, width), 0)

        def per_head(fn):
            return jnp.concatenate([fn(h, ln) for h, ln in enumerate(lanes)], axis=1)

        def chunk(j, dlb):
            c = nc - 1 - j
            rows = pl.ds(pl.multiple_of(c * CHUNK, CHUNK), CHUNK)
            qr, fr, v, dout = [r[rows, :].astype(F32) for r in (q_ref, f_ref, i_ref, do_ref)]
            sq = _sigmoid(qr)
            q = qr * sq
            sf = _sigmoid(fr)
            f = lb + (1.0 - lb) * sf
            k = 1.0 - f
            g = _tri_sum(trilb, jnp.log(f))
            g_s[...] = g
            st0 = [st_ref[h, c] for h in range(hps)]
            dt = [d_acc[h] for h in range(hps)]
            vb, dob = v.astype(BF16), dout.astype(BF16)
            dtb = [x.astype(BF16) for x in dt]
            st0b = [x.astype(BF16) for x in st0]
            glast = g_s[CHUNK - 1:CHUNK, :]
            eg = jnp.exp(g)
            kdec = jnp.exp(glast - g)
            qeb, kdb = (q * eg).astype(BF16), (k * kdec).astype(BF16)
            aps = [jnp.where(row > col, _dot_nt(dob[:, ln], vb[:, ln]), 0.0) for ln in lanes]
            dov = dout * v
            adiag = per_head(lambda h, ln: jnp.broadcast_to(
                jnp.sum(dov[:, ln], axis=-1, keepdims=True), (CHUNK, HEAD_DIM)))
            dq_inter = per_head(lambda h, ln: _dot(dob[:, ln], st0b[h]))
            dk_inter = per_head(lambda h, ln: _dot(vb[:, ln], dtb[h]))
            dk_st = kdec * dk_inter
            dg = qeb.astype(F32) * dq_inter
            dg_minus = kdb.astype(F32) * dk_inter
            dg = dg - dg_minus
            for blk in range(CHUNK // SUB):
                lo, hi = blk * SUB, (blk + 1) * SUB
                gref = g_s[lo - 1:lo, :] if blk else jnp.zeros((1, width), F32)
                qscale = jnp.exp(g[lo:hi] - gref)
                kscale = jnp.exp(jnp.where(rowk < hi, gref - g, NEG_BIG))
                qi = (q[lo:hi] * qscale).astype(BF16)
                ki = (k * kscale).astype(BF16)
                for h, ln in enumerate(lanes):
                    a_s[h, lo:hi, :] = _dot_nt(qi[:, ln], ki[:, ln])
                apb = [x[lo:hi].astype(BF16) for x in aps]
                from_k = per_head(lambda h, ln: _dot(apb[h], ki[:, ln]))
                from_q = per_head(lambda h, ln: _dot_tn(apb[h], qi[:, ln]))
                dq_s[lo:hi, :] = qscale * from_k
                dg_s[lo:hi, :] = qi.astype(F32) * from_k
                dk_st = dk_st + kscale * from_q
                dg = dg - ki.astype(F32) * from_q
            dg = dg + dg_s[...]
            dv = per_head(lambda h, ln: _dot_tn(jnp.where(tril, a_s[h], 0.0).astype(BF16), dob[:, ln])
                          + _dot_nt(kdb[:, ln], dtb[h]))
            dq_st = dq_s[...] + eg * dq_inter
            dq = dq_st + adiag * k
            dk = dk_st + adiag * q
            dec = jnp.exp(glast)
            dt_dec = [dt[h] * dec[:, ln] for h, ln in enumerate(lanes)]
            for h, ln in enumerate(lanes):
                d_acc[h] = dt_dec[h] + _dot_tn(dob[:, ln], qeb[:, ln])
            later = per_head(lambda h, ln: _colsum(dt_dec[h] * st0[h])) + _colsum(dg_minus)
            dlf = later + _tri_sum(triub, dg)
            df = dlf / f - dk
            dqfi_ref[rows, 0:width] = (dq * (sq * (1.0 + qr * (1.0 - sq)))).astype(dqfi_ref.dtype)
            dqfi_ref[rows, width:2 * width] = (df * (1.0 - lb) * sf * (1.0 - sf)).astype(dqfi_ref.dtype)
            dqfi_ref[rows, 2 * width:3 * width] = dv.astype(dqfi_ref.dtype)
            return dlb + _colsum(df * (1.0 - sf))

        dlb = lax.fori_loop(0, nc, chunk, jnp.zeros((1, width), F32))
        _accumulate(dlb_ref, dlb, first)

    assert hps == HEADS
    nt = n // t
    return _call(
        body, name=name, grid=(1, nt),
        in_specs=[sec(COL_Q), sec(COL_F), sec(COL_I), lbspec, head_tile, state],
        out_specs=[pl.BlockSpec((t, 3 * width), lambda h, i: (nt - 1 - i, 0)),
                   pl.BlockSpec((1, width), lambda h, i: (0, h))],
        out_shape=[jax.ShapeDtypeStruct((n, 3 * D_MODEL), BF16), jax.ShapeDtypeStruct((1, D_MODEL), F32)],
        args=[proj, proj, proj, hgrn_lb, do, states], semantics=("parallel", "arbitrary"),
        scratch=[pltpu.VMEM((hps, HEAD_DIM, HEAD_DIM), F32), pltpu.VMEM((CHUNK, width), F32),
                 pltpu.VMEM((hps, CHUNK, CHUNK), F32), pltpu.VMEM((CHUNK, width), F32),
                 pltpu.VMEM((CHUNK, width), F32)],
        exchange=exchange)


def _pool_fwd(proj, pool_w, pool_scale, *, name):
    n = proj.shape[0]
    t = _tile(n, 512, POOL_HALO)
    per = t // POOL_HALO
    c0 = COL_POOL // POOL_WIDTH

    def body(u_ref, halo_ref, pw_ref, ps_ref, pooled_ref, mixed_ref, ext):
        i = pl.program_id(0)
        u = u_ref[...].astype(F32)
        ext[POOL_HALO:POOL_HALO + t, :] = u
        ext[0:POOL_HALO, :] = jnp.where(i > 0, halo_ref[...].astype(F32), 0.0)
        pos = i * t + lax.broadcasted_iota(jnp.int32, (t, POOL_CH), 0) + 1
        for grp, win in enumerate(POOL_WINDOWS):
            cols = slice(grp * POOL_CH, (grp + 1) * POOL_CH)
            acc = u[:, cols]
            for j in range(1, win):
                acc = acc + ext[POOL_HALO - j:POOL_HALO - j + t, cols]
            pooled = (acc / jnp.minimum(pos, win).astype(F32) - u[:, cols]).astype(BF16)
            pooled_ref[:, cols] = pooled
            mixed_ref[:, cols] = (_dot(pooled, pw_ref[grp].astype(BF16)) * ps_ref[:, cols]).astype(BF16)

    tile = pl.BlockSpec((t, POOL_WIDTH), lambda i: (i, 0))
    return pl.pallas_call(
        body, name=name, grid=(n // t,),
        in_specs=[pl.BlockSpec((t, POOL_WIDTH), lambda i: (i, c0)),
                  pl.BlockSpec((POOL_HALO, POOL_WIDTH), lambda i: (jnp.maximum(i * per - 1, 0), c0)),
                  pl.BlockSpec((len(POOL_WINDOWS), POOL_CH, POOL_CH), lambda i: (0, 0, 0)),
                  pl.BlockSpec((1, POOL_WIDTH), lambda i: (0, 0))],
        out_specs=[tile, tile],
        out_shape=[jax.ShapeDtypeStruct((n, POOL_WIDTH), BF16), jax.ShapeDtypeStruct((n, POOL_WIDTH), BF16)],
        scratch_shapes=[pltpu.VMEM((t + POOL_HALO, POOL_WIDTH), F32)],
        compiler_params=_params("parallel"),
    )(proj, proj, pool_w, pool_scale)


def _pool_bwd(dmixed, pooled, pool_w, pool_scale, *, name):
    n = dmixed.shape[0]
    t = _tile(n, 512, POOL_HALO)
    per = t // POOL_HALO
    nb = n // t

    def body(dm_ref, dmh_ref, p_ref, pw_ref, ps_ref, du_ref, dpw_ref, dps_ref, ext):
        i = pl.program_id(0)

        @pl.when(i == 0)
        def _():
            dpw_ref[...] = jnp.zeros_like(dpw_ref)
            dps_ref[...] = jnp.zeros_like(dps_ref)

        dm, dmh = dm_ref[...], dmh_ref[...]
        pos = i * t + lax.broadcasted_iota(jnp.int32, (t, POOL_CH), 0) + 1
        for grp, win in enumerate(POOL_WINDOWS):
            cols = slice(grp * POOL_CH, (grp + 1) * POOL_CH)
            pwb = pw_ref[grp].astype(BF16)
            pb = p_ref[:, cols]
            scale = ps_ref[:, cols]
            dps_ref[:, cols] += _colsum(dm[:, cols] * _dot(pb, pwb))
            dpm = (dm[:, cols] * scale).astype(BF16)
            dpw_ref[grp] += _dot_tn(pb, dpm)
            dpool = _dot_nt(dpm, pwb)
            dpool_next = _dot_nt((dmh[:, cols] * scale).astype(BF16), pwb)
            ext[0:t, cols] = dpool / jnp.minimum(pos, win).astype(F32)
            ext[t:t + POOL_HALO, cols] = jnp.where(i < nb - 1, dpool_next * (1.0 / win), 0.0)
            acc = -dpool
            for j in range(win):
                acc = acc + ext[j:j + t, cols]
            du_ref[:, cols] = acc.astype(du_ref.dtype)

    tile = pl.BlockSpec((t, POOL_WIDTH), lambda i: (i, 0))
    return pl.pallas_call(
        body, name=name, grid=(nb,),
        in_specs=[tile, pl.BlockSpec((POOL_HALO, POOL_WIDTH), lambda i: (jnp.minimum((i + 1) * per, nb * per - 1), 0)),
                  tile, pl.BlockSpec((len(POOL_WINDOWS), POOL_CH, POOL_CH), lambda i: (0, 0, 0)),
                  pl.BlockSpec((1, POOL_WIDTH), lambda i: (0, 0))],
        out_specs=[tile, pl.BlockSpec((len(POOL_WINDOWS), POOL_CH, POOL_CH), lambda i: (0, 0, 0)),
                   pl.BlockSpec((1, POOL_WIDTH), lambda i: (0, 0))],
        out_shape=[jax.ShapeDtypeStruct((n, POOL_WIDTH), BF16),
                   jax.ShapeDtypeStruct((len(POOL_WINDOWS), POOL_CH, POOL_CH), F32),
                   jax.ShapeDtypeStruct((1, POOL_WIDTH), F32)],
        scratch_shapes=[pltpu.VMEM((t + POOL_HALO, POOL_WIDTH), F32)],
        compiler_params=_params("arbitrary"),
    )(dmixed, dmixed, pooled, pool_w, pool_scale)


def _adamw(w, g, m, v):
    m2 = ADAM_B1 * m + (1.0 - ADAM_B1) * g
    v2 = ADAM_B2 * v + (1.0 - ADAM_B2) * (g * g)
    m_hat = m2 * (1.0 / (1.0 - ADAM_B1 ** ADAM_STEP))
    v_hat = v2 * (1.0 / (1.0 - ADAM_B2 ** ADAM_STEP))
    delta = -ADAM_LR * (m_hat / (jnp.sqrt(v_hat) + ADAM_EPS) + ADAM_WD * w)
    return delta, m2, v2


def _adam_big(recv, w, m, v, *, name):
    r, c = w.shape
    tr = _tile(r, 256, 16)

    def body(recv_ref, w_ref, m_ref, v_ref, g_ref, d_ref, m2_ref, v2_ref):
        g = recv_ref[0].astype(F32)
        for i in range(1, N_DEV):
            g = g + recv_ref[i].astype(F32)
        delta, m2, v2 = _adamw(w_ref[...], g, m_ref[...], v_ref[...])
        g_ref[...] = g
        d_ref[...] = delta
        m2_ref[...] = m2
        v2_ref[...] = v2

    tile = pl.BlockSpec((tr, c), lambda i: (i, 0))
    out = jax.ShapeDtypeStruct((r, c), F32)
    return pl.pallas_call(
        body, name=name, grid=(r // tr,),
        in_specs=[pl.BlockSpec((N_DEV, tr, c), lambda i: (0, i, 0)), tile, tile, tile],
        out_specs=[tile] * 4, out_shape=[out] * 4, compiler_params=_params("parallel"),
    )(recv, w, m, v)


def _adam_small(parts, w, m, v, *, name):
    n = len(SMALL_PARAMS)

    def body(*refs):
        parts_r, w_r, m_r, v_r = (refs[i * n:(i + 1) * n] for i in range(4))
        outs = refs[4 * n:]
        for j, key in enumerate(SMALL_PARAMS):
            g = parts_r[j][0]
            for i in range(1, N_DEV):
                g = g + parts_r[j][i]
            w_ = w_r[j][...]
            if key == "hgrn_lb":
                s0 = 1.0 / (1.0 + jnp.exp(w_[1:2] - w_[0:1]))
                ga = g * s0 * (1.0 - s0)
                sign = jnp.where(lax.broadcasted_iota(jnp.int32, w_.shape, 0) == 0, 1.0, -1.0)
                g = sign * jnp.broadcast_to(ga, w_.shape)
            delta, m2, v2 = _adamw(w_, g, m_r[j][...], v_r[j][...])
            for q, val in enumerate((g, delta, m2, v2)):
                outs[q * n + j][...] = val

    out_shape = [jax.ShapeDtypeStruct(w[k].shape, F32) for _ in range(4) for k in SMALL_PARAMS]
    res = pl.pallas_call(body, name=name, out_shape=out_shape, compiler_params=_params())(
        *[t[k] for t in (parts, w, m, v) for k in SMALL_PARAMS])
    return [res[q * n:(q + 1) * n] for q in range(4)]


def _as_2d(a):
    return a.reshape(-1, a.shape[-1])


SPLIT_AXIS = dict(BIG_WEIGHTS)


def _gather_of(names, weights):
    return _Exchange([_shard_to_send(weights[k][0], SPLIT_AXIS[k]) for k in names], gather=True)


def _scatter_of(names, dfull):
    return _Exchange([_to_slots(dfull[k], SPLIT_AXIS[k]) for k in names], gather=False)


def _shard_to_send(w, axis):
    return (w.T if axis == 1 else w).astype(BF16)


def _to_slots(dw, axis):
    rows, cols = dw.shape
    return dw.reshape(N_DEV, rows // N_DEV, cols)


def _from_slots(gathered, axis):
    _, r, c = gathered.shape
    return gathered.reshape(N_DEV * r, c)


def kernel(x, p, ffn1_norm, ffn1_w1, ffn1_w3, ffn1_w2, mix_norm, w_in, hgrn_lb, hgrn_onorm, w_branch_a, pool_w, pool_scale, w_branch_b, w_out, ffn2_norm, ffn2_w1, ffn2_w3, ffn2_w2, ple_norm, ple_w_gate, ple_w_proj, ple_post_norm, final_norm, loss_target, m_ffn1_norm, m_ffn1_w1, m_ffn1_w3, m_ffn1_w2, m_mix_norm, m_w_in, m_hgrn_lb, m_hgrn_onorm, m_w_branch_a, m_pool_w, m_pool_scale, m_w_branch_b, m_w_out, m_ffn2_norm, m_ffn2_w1, m_ffn2_w3, m_ffn2_w2, m_ple_norm, m_ple_w_gate, m_ple_w_proj, m_ple_post_norm, m_final_norm, v_ffn1_norm, v_ffn1_w1, v_ffn1_w3, v_ffn1_w2, v_mix_norm, v_w_in, v_hgrn_lb, v_hgrn_onorm, v_w_branch_a, v_pool_w, v_pool_scale, v_w_branch_b, v_w_out, v_ffn2_norm, v_ffn2_w1, v_ffn2_w3, v_ffn2_w2, v_ple_norm, v_ple_w_gate, v_ple_w_proj, v_ple_post_norm, v_final_norm):
    weights = dict(ffn1_norm=ffn1_norm, ffn1_w1=ffn1_w1, ffn1_w3=ffn1_w3, ffn1_w2=ffn1_w2, mix_norm=mix_norm, w_in=w_in, hgrn_lb=hgrn_lb, hgrn_onorm=hgrn_onorm, w_branch_a=w_branch_a, pool_w=pool_w, pool_scale=pool_scale, w_branch_b=w_branch_b, w_out=w_out, ffn2_norm=ffn2_norm, ffn2_w1=ffn2_w1, ffn2_w3=ffn2_w3, ffn2_w2=ffn2_w2, ple_norm=ple_norm, ple_w_gate=ple_w_gate, ple_w_proj=ple_w_proj, ple_post_norm=ple_post_norm, final_norm=final_norm)
    mom1 = dict(ffn1_norm=m_ffn1_norm, ffn1_w1=m_ffn1_w1, ffn1_w3=m_ffn1_w3, ffn1_w2=m_ffn1_w2, mix_norm=m_mix_norm, w_in=m_w_in, hgrn_lb=m_hgrn_lb, hgrn_onorm=m_hgrn_onorm, w_branch_a=m_w_branch_a, pool_w=m_pool_w, pool_scale=m_pool_scale, w_branch_b=m_w_branch_b, w_out=m_w_out, ffn2_norm=m_ffn2_norm, ffn2_w1=m_ffn2_w1, ffn2_w3=m_ffn2_w3, ffn2_w2=m_ffn2_w2, ple_norm=m_ple_norm, ple_w_gate=m_ple_w_gate, ple_w_proj=m_ple_w_proj, ple_post_norm=m_ple_post_norm, final_norm=m_final_norm)
    mom2 = dict(ffn1_norm=v_ffn1_norm, ffn1_w1=v_ffn1_w1, ffn1_w3=v_ffn1_w3, ffn1_w2=v_ffn1_w2, mix_norm=v_mix_norm, w_in=v_w_in, hgrn_lb=v_hgrn_lb, hgrn_onorm=v_hgrn_onorm, w_branch_a=v_w_branch_a, pool_w=v_pool_w, pool_scale=v_pool_scale, w_branch_b=v_w_branch_b, w_out=v_w_out, ffn2_norm=v_ffn2_norm, ffn2_w1=v_ffn2_w1, ffn2_w3=v_ffn2_w3, ffn2_w2=v_ffn2_w2, ple_norm=v_ple_norm, ple_w_gate=v_ple_w_gate, ple_w_proj=v_ple_w_proj, ple_post_norm=v_ple_post_norm, final_norm=v_final_norm)

    xs = x[0]
    ps = p[0, 0].astype(BF16)
    tgt = loss_target[0]
    n = xs.shape[0]

    g_f1, g_mix, g_on, g_f2 = ffn1_norm, mix_norm, hgrn_onorm, ffn2_norm
    g_ple, g_post, g_fin = ple_norm, ple_post_norm, final_norm.reshape(1, D_MODEL)
    lb2 = hgrn_lb
    pw, pscale = pool_w[0], pool_scale

    full = {}

    def keep(names, gathered):
        for k, g in zip(names, gathered):
            full[k] = _from_slots(g, SPLIT_AXIS[k])

    names = ("ffn1_w1", "ffn1_w3")
    ex = _gather_of(names, weights)
    h1 = _rms_fwd(xs, g_f1, name="ffn1_rms", exchange=ex)
    keep(names, ex.received)
    names = ("ffn1_w2", "w_in")
    ex = _gather_of(names, weights)
    a1, b1, s1 = _ffn_up(h1, full["ffn1_w1"], full["ffn1_w3"], name="ffn1_up", exchange=ex)
    keep(names, ex.received)
    names = ("w_branch_a", "w_branch_b", "w_out")
    ex = _gather_of(names, weights)
    x1, h2 = _mm_nn_res_rms(s1, full["ffn1_w2"], xs, g_mix, name="ffn1_down", scale=0.5, exchange=ex)
    keep(names, ex.received)
    names = ("ffn2_w1", "ffn2_w3", "ffn2_w2", "ple_w_gate", "ple_w_proj")
    ex = _gather_of(names, weights)
    proj = _mm_nn_wide(h2, full["w_in"], name="w_in_proj", out_dtype=BF16, exchange=ex)
    keep(names, ex.received)
    o, states = _hgrn_fwd(proj, lb2, name="hgrn_fwd")
    on = _hgrn_post_fwd(o, proj, g_on, name="hgrn_post_fwd")
    ya = _mm_nn_wide(on, full["w_branch_a"], name="branch_a", out_dtype=BF16, b_is_km=True)
    pooled, mixed = _pool_fwd(proj, pw, pscale, name="pool_fwd")
    yb = _mm_nn_wide(mixed, full["w_branch_b"], name="branch_b", out_dtype=BF16)
    y = _merge_fwd(proj, ya, yb, name="merge_fwd")
    x2, h3 = _mm_nn_res_rms(y, full["w_out"], x1, g_f2, name="w_out_proj", scale=1.0)
    a2, b2, s2 = _ffn_up(h3, full["ffn2_w1"], full["ffn2_w3"], name="ffn2_up")
    x3, h4 = _mm_nn_res_rms(s2, full["ffn2_w2"], x2, g_ple, name="ffn2_down", scale=0.5)
    gpre = _mm_nn_wide(h4, full["ple_w_gate"], name="ple_gate", out_dtype=BF16, b_is_km=True)
    z = _mm_nn_wide(ps, full["ple_w_proj"], name="ple_proj", out_dtype=BF16)
    dx4, dpre, dz, loss_part, d_fin, d_post = _ple_final(x3, gpre, z, tgt, g_post, g_fin, name="ple_final")

    dfull, received = {}, {}

    def sent(names, exchange):
        received.update(zip(names, exchange.received))

    dfull["ple_w_proj"] = _mm_tn(ps, dz, name="d_ple_w_proj", tn=1024, tm=1024, transpose_out=True)
    dfull["ple_w_gate"] = _mm_tn(h4, dpre, name="d_ple_w_gate", tn=1024, tm=1024)
    dx3, dx3s, d_ple = _mm_nt_rms_bwd([(dpre, full["ple_w_gate"], False)], x3, dx4, g_ple, name="ple_rms_bwd", tn=512,
                                      half_scale=0.5)

    names = ("ple_w_proj", "ple_w_gate")
    ex = _scatter_of(names, dfull)
    da2, db2 = _ffn_bwd_mid(dx3s, full["ffn2_w2"], a2, b2, name="ffn2_bwd_mid", exchange=ex)
    sent(names, ex)
    dfull["ffn2_w2"] = _mm_tn(s2, dx3s, name="ffn2_dw2", tn=1024, tm=512)
    dfull["ffn2_w1"] = _mm_tn(h3, da2, name="ffn2_dw1", tn=1024, tm=1408, transpose_out=True)
    dfull["ffn2_w3"] = _mm_tn(h3, db2, name="ffn2_dw3", tn=1024, tm=1408, transpose_out=True)
    names = ("ffn2_w2",)
    ex = _scatter_of(names, dfull)
    dx2, dx2b, d_f2 = _mm_nt_rms_bwd([(da2, full["ffn2_w1"], True), (db2, full["ffn2_w3"], True)], x2, dx3, g_f2,
                                     name="ffn2_rms_bwd", tn=512, half_scale=1.0, exchange=ex)
    sent(names, ex)

    dfull["w_out"] = _mm_tn(y, dx2b, name="d_w_out", tn=1024, tm=1024)
    dy = _mm_nn_wide(dx2b, full["w_out"], name="d_y", out_dtype=BF16)
    dya, dyb, dga, dgb = _merge_bwd(dy, proj, ya, yb, name="merge_bwd")

    dfull["w_branch_b"] = _mm_tn(mixed, dyb, name="d_w_branch_b", tn=1024, tm=1024, transpose_out=True)
    dmixed = _mm_nn_wide(dyb, full["w_branch_b"], name="d_mixed", out_dtype=F32, b_is_km=True)
    du, d_pw, d_ps = _pool_bwd(dmixed, pooled, pw, pscale, name="pool_bwd")

    dfull["w_branch_a"] = _mm_tn(on, dya, name="d_w_branch_a", tn=1024, tm=1024)
    don = _mm_nn_wide(dya, full["w_branch_a"], name="d_on", out_dtype=BF16)
    dog, do, d_on = _hgrn_post_bwd(don, o, proj, g_on, name="hgrn_post_bwd")
    names = ("ffn2_w1", "ffn2_w3", "w_out", "w_branch_b", "w_branch_a")
    ex = _scatter_of(names, dfull)
    dqfi, d_lb = _hgrn_bwd(proj, lb2, do, states, name="hgrn_bwd", exchange=ex)
    sent(names, ex)
    dproj = [dqfi, dog, du, dga, dgb]
    dfull["w_in"] = jnp.concatenate(
        [_mm_tn(h2, part, name=f"d_w_in_{j}", tn=1024, tm=1536, transpose_out=True) for j, part in enumerate(dproj)],
        axis=0)
    names = ("w_in",)
    ex = _scatter_of(names, dfull)
    dx1, dx1s, d_mix = _mm_nt_rms_bwd([(dproj, full["w_in"], True)], x1, dx2, g_mix, name="mix_rms_bwd", tn=512,
                                      half_scale=0.5, exchange=ex)
    sent(names, ex)

    da1, db1 = _ffn_bwd_mid(dx1s, full["ffn1_w2"], a1, b1, name="ffn1_bwd_mid")
    dfull["ffn1_w2"] = _mm_tn(s1, dx1s, name="ffn1_dw2", tn=1024, tm=512)
    names = ("ffn1_w2",)
    ex = _scatter_of(names, dfull)
    dfull["ffn1_w1"] = _mm_tn(h1, da1, name="ffn1_dw1", tn=1024, tm=1408, transpose_out=True, exchange=ex)
    sent(names, ex)
    names = ("ffn1_w1",)
    ex = _scatter_of(names, dfull)
    dfull["ffn1_w3"] = _mm_tn(h1, db1, name="ffn1_dw3", tn=1024, tm=1408, transpose_out=True, exchange=ex)
    sent(names, ex)
    names = ("ffn1_w3",)
    ex = _scatter_of(names, dfull)
    grad_x, _, d_f1 = _mm_nt_rms_bwd([(da1, full["ffn1_w1"], True), (db1, full["ffn1_w3"], True)], xs, dx1, g_f1,
                                     name="ffn1_rms_bwd", tn=512, half_scale=1.0, exchange=ex)
    sent(names, ex)

    small_part = dict(ffn1_norm=d_f1, mix_norm=d_mix, hgrn_onorm=d_on, ffn2_norm=d_f2, ple_norm=d_ple,
                      ple_post_norm=d_post, final_norm=d_fin, hgrn_lb=d_lb, pool_scale=d_ps, pool_w=_as_2d(d_pw))
    gathered_small = _exchange_now([small_part[k] for k in SMALL_PARAMS] + [loss_part], name="gather_small_grads",
                                   gather=True)
    small_all = dict(zip(SMALL_PARAMS, gathered_small))
    loss = jnp.sum(gathered_small[-1])

    grads, deltas, new_m, new_v = {}, {}, {}, {}
    for name, axis in BIG_WEIGHTS:
        shape, recv = weights[name].shape, received[name]
        own = [t[name][0].T if axis == 1 else t[name][0] for t in (weights, mom1, mom2)]
        res = _adam_big(recv, *own, name=f"adam_{name}")
        grads[name], deltas[name], new_m[name], new_v[name] = [(r.T if axis == 1 else r).reshape(shape) for r in res]
    res = _adam_small(small_all, *[{k: _as_2d(t[k]) for k in SMALL_PARAMS} for t in (weights, mom1, mom2)],
                      name="adam_small")
    for store, vals in zip((grads, deltas, new_m, new_v), res):
        store.update({k: val.reshape(weights[k].shape) for k, val in zip(SMALL_PARAMS, vals)})

    return (loss, grad_x.reshape(x.shape), *[grads[k] for k in WEIGHT_ORDER], *[deltas[k] for k in WEIGHT_ORDER],
            *[new_m[k] for k in WEIGHT_ORDER], *[new_v[k] for k in WEIGHT_ORDER])
```

```python
import jax
import jax.numpy as jnp
from jax import lax
from jax.experimental import pallas as pl
from jax.experimental.pallas import tpu as pltpu

F32 = jnp.float32
BF16 = jnp.bfloat16

N_DEV = 8
D_MODEL = 1024
HEADS = 8
HEAD_DIM = 128
POOL_WINDOWS = (2, 4, 8, 16)
POOL_CH = 128
POOL_WIDTH = 512
POOL_HALO = 16
RMS_EPS = 1e-6
CHUNK = 64
SUB = 32
HGRN_HEADS_PER_STEP = 8
NEG_BIG = -1e30

ADAM_LR = 0.001
ADAM_B1 = 0.9
ADAM_B2 = 0.999
ADAM_EPS = 1e-08
ADAM_WD = 0.01
ADAM_STEP = 10

V7X_VMEM_BYTES = 64 * 1024 * 1024
VMEM_LIMIT = (V7X_VMEM_BYTES * 3) // 4
EXCHANGE_TAIL_STEPS = 3
ROW_TILE_CAP = 8192

COL_Q, COL_F, COL_I, COL_OG, COL_POOL, COL_GA, COL_GB = 0, 1024, 2048, 3072, 4096, 4608, 5632

BIG_WEIGHTS = (
    ("ffn1_w1", 1), ("ffn1_w3", 1), ("ffn1_w2", 0), ("w_in", 1), ("w_branch_a", 0), ("w_branch_b", 1),
    ("w_out", 0), ("ffn2_w1", 1), ("ffn2_w3", 1), ("ffn2_w2", 0), ("ple_w_gate", 0), ("ple_w_proj", 1),
)
SMALL_PARAMS = ("ffn1_norm", "mix_norm", "hgrn_onorm", "ffn2_norm", "ple_norm", "ple_post_norm", "final_norm",
                "hgrn_lb", "pool_scale", "pool_w")
WEIGHT_ORDER = (
    "ffn1_norm", "ffn1_w1", "ffn1_w3", "ffn1_w2", "mix_norm", "w_in", "hgrn_lb", "hgrn_onorm", "w_branch_a", "pool_w",
    "pool_scale", "w_branch_b", "w_out", "ffn2_norm", "ffn2_w1", "ffn2_w3", "ffn2_w2", "ple_norm", "ple_w_gate",
    "ple_w_proj", "ple_post_norm", "final_norm",
)


def _params(*sem):
    return pltpu.CompilerParams(dimension_semantics=sem if sem else None, vmem_limit_bytes=VMEM_LIMIT)


COL_CHUNK = 256


def _rows(tn, width):
    return pl.BlockSpec((tn, width), lambda i: (i, 0))


def _resident(shape):
    return pl.BlockSpec(shape, lambda i: (0,) * len(shape), pipeline_mode=pl.Buffered(1))


def _dot(a, b):
    return jnp.dot(a, b, preferred_element_type=F32)


def _dot_nt(a, b):
    return lax.dot_general(a, b, (((1,), (1,)), ((), ())), preferred_element_type=F32)


def _dot_tn(a, b):
    return lax.dot_general(a, b, (((0,), (0,)), ((), ())), preferred_element_type=F32)


def _sigmoid(x):
    return 0.5 * jnp.tanh(0.5 * x) + 0.5


def _tile(n, want, mult):
    if mult != 128:
        want = min(want, ROW_TILE_CAP)
    if n <= want:
        return n
    t = (want // mult) * mult
    while t > mult and n % t:
        t -= mult
    assert n % t == 0, (n, want, mult)
    return t


class _Exchange:
    COPIES = N_DEV - 1

    def __init__(self, arrs, gather):
        self.arrs, self.gather, self.n = list(arrs), gather, len(arrs)
        self.out_shape = [jax.ShapeDtypeStruct((N_DEV,) + (a.shape if gather else a.shape[1:]), a.dtype) for a in arrs]
        self.scratch = [pltpu.SemaphoreType.DMA((self.n * self.COPIES,)),
                        pltpu.SemaphoreType.DMA((self.n * self.COPIES,)), pltpu.SemaphoreType.DMA((self.n,))]
        self.received = None

    @staticmethod
    def _place():
        x, y, c = lax.axis_index("x"), lax.axis_index("y"), lax.axis_index("c")
        return x, y, c

    def _copy(self, a, k, src, dst, to, sems):
        s = a * self.COPIES + k
        return pltpu.make_async_remote_copy(src_ref=src, dst_ref=dst, send_sem=sems[0].at[s], recv_sem=sems[1].at[s],
                                            device_id=to, device_id_type=pl.DeviceIdType.MESH)

    def _gather_copies(self, ins, outs, sems):
        x, y, c = self._place()
        chips = [(1 - x, y), (x, 1 - y), (1 - x, 1 - y)]
        slot = lambda px, py, pc: 4 * px + 2 * py + pc
        first, passed, arrivals = [], [], []
        for a in range(self.n):
            mine = outs[a].at[slot(x, y, c)]
            first.append(self._copy(a, 0, ins[a], mine, (x, y, 1 - c), sems))
            arrivals.append(self._copy(a, 0, ins[a], outs[a].at[slot(x, y, 1 - c)], (x, y, 1 - c), sems))
            for j, (px, py) in enumerate(chips):
                first.append(self._copy(a, 1 + j, ins[a], mine, (px, py, c), sems))
                theirs = outs[a].at[slot(px, py, c)]
                passed.append((self._copy(a, 1 + j, ins[a], theirs, (px, py, c), sems),
                               self._copy(a, 4 + j, theirs, theirs, (x, y, 1 - c), sems)))
                arrivals.append(self._copy(a, 4 + j, ins[a], outs[a].at[slot(px, py, 1 - c)], (x, y, 1 - c), sems))
        return first, passed, arrivals

    def _scatter_copies(self, ins, outs, sems):
        x, y, c = self._place()
        me = 4 * x + 2 * y + c
        sends, arrivals = [], []
        for k in range(1, N_DEV):
            px = 1 - x if k & 4 else x
            py = 1 - y if k & 2 else y
            pc = 1 - c if k & 1 else c
            peer = 4 * px + 2 * py + pc
            for a in range(self.n):
                sends.append(self._copy(a, k - 1, ins[a].at[peer], outs[a].at[me], (px, py, pc), sems))
                arrivals.append(self._copy(a, k - 1, ins[a].at[peer], outs[a].at[peer], (px, py, pc), sems))
        return sends, arrivals

    def _local(self, ins, outs, sems):
        x, y, c = self._place()
        me = 4 * x + 2 * y + c
        return [pltpu.make_async_copy(ins[a] if self.gather else ins[a].at[me], outs[a].at[me], sems[2].at[a])
                for a in range(self.n)]

    def start(self, ins, outs, sems):
        for cp in self._local(ins, outs, sems):
            cp.start()
        sends = self._gather_copies(ins, outs, sems)[0] if self.gather else self._scatter_copies(ins, outs, sems)[0]
        for cp in sends:
            cp.start()

    def pass_on(self, ins, outs, sems):
        if self.gather:
            for landed, onward in self._gather_copies(ins, outs, sems)[1]:
                landed.wait_recv()
                onward.start()

    def finish(self, ins, outs, sems):
        if self.gather:
            first, passed, arrivals = self._gather_copies(ins, outs, sems)
            sends = first + [onward for _, onward in passed]
        else:
            sends, arrivals = self._scatter_copies(ins, outs, sems)
        for cp in arrivals:
            cp.wait_recv()
        for cp in sends:
            cp.wait_send()
        for cp in self._local(ins, outs, sems):
            cp.wait()


def _call(body, *, name, grid, in_specs, out_specs, out_shape, args, semantics, scratch=(), exchange=None):
    if exchange is None:
        return pl.pallas_call(
            body, name=name, grid=grid, in_specs=in_specs, out_specs=out_specs, out_shape=out_shape,
            scratch_shapes=list(scratch), compiler_params=_params(*semantics))(*args)
    ex = exchange
    n_in, n_out, n_s = len(in_specs), len(out_specs), len(scratch)

    def wrapped(*refs):
        ins, ex_in = refs[:n_in], refs[n_in:n_in + ex.n]
        o0 = n_in + ex.n
        outs, ex_out = refs[o0:o0 + n_out], refs[o0 + n_out:o0 + n_out + ex.n]
        s0 = o0 + n_out + ex.n
        scr, sems = refs[s0:s0 + n_s], refs[s0 + n_s:]
        step, steps = pl.program_id(0), grid[0]
        for ax in range(1, len(grid)):
            step, steps = step * grid[ax] + pl.program_id(ax), steps * grid[ax]

        @pl.when(step == 0)
        def _():
            ex.start(ex_in, ex_out, sems)

        body(*ins, *outs, *scr)

        @pl.when(step == max(steps - EXCHANGE_TAIL_STEPS, 0))
        def _():
            ex.pass_on(ex_in, ex_out, sems)

        @pl.when(step == steps - 1)
        def _():
            ex.finish(ex_in, ex_out, sems)

    hbm = pl.BlockSpec(memory_space=pltpu.HBM)
    res = pl.pallas_call(
        wrapped, name=name, grid=grid, in_specs=list(in_specs) + [hbm] * ex.n,
        out_specs=list(out_specs) + [hbm] * ex.n, out_shape=list(out_shape) + ex.out_shape,
        scratch_shapes=list(scratch) + ex.scratch, compiler_params=_params(*(["arbitrary"] * len(grid))),
    )(*args, *ex.arrs)
    ex.received = res[n_out:]
    return res[:n_out]


def _exchange_now(arrs, *, name, gather):
    ex = _Exchange(arrs, gather)
    n = ex.n

    def body(*refs):
        ex.start(refs[:n], refs[n:2 * n], refs[2 * n:])
        ex.pass_on(refs[:n], refs[n:2 * n], refs[2 * n:])
        ex.finish(refs[:n], refs[n:2 * n], refs[2 * n:])

    hbm = pl.BlockSpec(memory_space=pltpu.HBM)
    return pl.pallas_call(body, name=name, out_shape=ex.out_shape, in_specs=[hbm] * n, out_specs=[hbm] * n,
                          scratch_shapes=ex.scratch)(*arrs)


def _mm_tn(a, b, *, name, tn, tm, transpose_out=False, exchange=None):
    n, k = a.shape
    m = b.shape[1]
    tn, tm = _tile(n, tn, 16), _tile(m, tm, 128)
    steps = n // tn

    def body(a_ref, b_ref, o_ref, acc):
        i = pl.program_id(1)

        @pl.when(i == 0)
        def _():
            acc[...] = jnp.zeros_like(acc)

        acc[...] += _dot_tn(a_ref[...], b_ref[...])

        @pl.when(i == steps - 1)
        def _():
            for c0 in range(0, tm, COL_CHUNK):
                cols = slice(c0, min(c0 + COL_CHUNK, tm))
                if transpose_out:
                    o_ref[cols, :] = acc[:, cols].T.astype(o_ref.dtype)
                else:
                    o_ref[:, cols] = acc[:, cols].astype(o_ref.dtype)

    if transpose_out:
        out_spec, out_shape = pl.BlockSpec((tm, k), lambda j, i: (j, 0)), jax.ShapeDtypeStruct((m, k), BF16)
    else:
        out_spec, out_shape = pl.BlockSpec((k, tm), lambda j, i: (0, j)), jax.ShapeDtypeStruct((k, m), BF16)
    return _call(body, name=name, grid=(m // tm, steps),
                 in_specs=[pl.BlockSpec((tn, k), lambda j, i: (i, 0)), pl.BlockSpec((tn, tm), lambda j, i: (i, j))],
                 out_specs=[out_spec], out_shape=[out_shape], args=[a, b], semantics=("parallel", "arbitrary"),
                 scratch=[pltpu.VMEM((k, tm), F32)], exchange=exchange)[0]


def _ffn_up(h, w1, w3, *, name, tn=512, exchange=None):
    n, k = h.shape
    m = w1.shape[0]
    tn = _tile(n, tn, 16)

    def body(h_ref, w1_ref, w3_ref, dsda_ref, dsdb_ref, s_ref):
        for c0 in range(0, m, COL_CHUNK):
            cols = slice(c0, c0 + COL_CHUNK)
            a = _dot_nt(h_ref[...], w1_ref[cols, :])
            b = _dot_nt(h_ref[...], w3_ref[cols, :])
            sg = _sigmoid(a)
            silu = a * sg
            dsda_ref[:, cols] = (b * (sg + silu * (1.0 - sg))).astype(dsda_ref.dtype)
            dsdb_ref[:, cols] = silu.astype(dsdb_ref.dtype)
            s_ref[:, cols] = (silu * b).astype(s_ref.dtype)

    ospec = _rows(tn, m)
    return _call(body, name=name, grid=(n // tn,),
                 in_specs=[_rows(tn, k), _resident(w1.shape), _resident(w3.shape)], out_specs=[ospec, ospec, ospec],
                 out_shape=[jax.ShapeDtypeStruct((n, m), BF16)] * 3,
                 args=[h, w1, w3], semantics=("parallel",), exchange=exchange)


def _mm_nn_wide(a, b, *, name, out_dtype, b_is_km=False, exchange=None):
    n, k = a.shape
    m = b.shape[1 if b_is_km else 0]
    tn = _tile(n, 512, 16)
    chunk = min(2 * COL_CHUNK, m)

    def body(a_ref, b_ref, o_ref):
        for c0 in range(0, m, chunk):
            cols = slice(c0, c0 + chunk)
            if b_is_km:
                res = _dot(a_ref[...], b_ref[:, cols])
            else:
                res = _dot_nt(a_ref[...], b_ref[cols, :])
            o_ref[:, cols] = res.astype(o_ref.dtype)

    return _call(body, name=name, grid=(n // tn,), in_specs=[_rows(tn, k), _resident(b.shape)],
                 out_specs=[_rows(tn, m)], out_shape=[jax.ShapeDtypeStruct((n, m), out_dtype)], args=[a, b],
                 semantics=("parallel",), exchange=exchange)[0]


def _mm_nn_res_rms(a, b, res, g, *, name, scale, exchange=None):
    n, k = a.shape
    d = b.shape[1]
    tn = _tile(n, 512, 16)

    def body(a_ref, b_ref, r_ref, g_ref, x_ref, h_ref):
        for c0 in range(0, d, COL_CHUNK):
            cols = slice(c0, c0 + COL_CHUNK)
            x_ref[:, cols] = r_ref[:, cols] + scale * _dot(a_ref[...], b_ref[:, cols])
        x = x_ref[...]
        r = lax.rsqrt(_rowmean(x * x) + RMS_EPS)
        h_ref[...] = (x * r * g_ref[...]).astype(h_ref.dtype)

    row = _rows(tn, d)
    return _call(body, name=name, grid=(n // tn,),
                 in_specs=[_rows(tn, k), _resident(b.shape), row, pl.BlockSpec((1, d), lambda i: (0, 0))],
                 out_specs=[row, row],
                 out_shape=[jax.ShapeDtypeStruct((n, d), F32), jax.ShapeDtypeStruct((n, d), BF16)],
                 args=[a, b, res, g], semantics=("parallel",), exchange=exchange)


def _ffn_bwd_mid(dxs, w2, dsda, dsdb, *, name, exchange=None):
    n, d = dxs.shape
    m = w2.shape[0]
    tn = _tile(n, 512, 16)

    def body(dx_ref, w2_ref, dsda_ref, dsdb_ref, da_ref, db_ref):
        for c0 in range(0, m, COL_CHUNK):
            cols = slice(c0, c0 + COL_CHUNK)
            ds = _dot_nt(dx_ref[...], w2_ref[cols, :])
            da_ref[:, cols] = (ds * dsda_ref[:, cols].astype(F32)).astype(da_ref.dtype)
            db_ref[:, cols] = (ds * dsdb_ref[:, cols].astype(F32)).astype(db_ref.dtype)

    tile = _rows(tn, m)
    return _call(body, name=name, grid=(n // tn,),
                 in_specs=[_rows(tn, d), _resident(w2.shape), tile, tile], out_specs=[tile, tile],
                 out_shape=[jax.ShapeDtypeStruct((n, m), BF16), jax.ShapeDtypeStruct((n, m), BF16)],
                 args=[dxs, w2, dsda, dsdb], semantics=("parallel",), exchange=exchange)


def _rowwise(fn, *, name, n, tn, ncol, rows, vecs, outs, accs=(), exchange=None):
    tn = _tile(n, tn, 16)
    nr, nv, no = len(rows), len(vecs), len(outs)

    def body(*refs):
        first = pl.program_id(1) == 0
        vals = [r[...].astype(F32) for r in refs[:nr + nv]]
        res = fn(*vals)
        for ref, val in zip(refs[nr + nv:nr + nv + no], res[:no]):
            ref[...] = val.astype(ref.dtype)
        for ref, val in zip(refs[nr + nv + no:], res[no:]):
            _accumulate(ref, val, first)

    in_specs = [pl.BlockSpec((tn, w), lambda j, i, c0=c0: (i, c0 + j)) for _, w, c0 in rows]
    in_specs += [pl.BlockSpec((1, w), lambda j, i, c0=c0: (0, c0 + j)) for _, w, c0 in vecs]
    out_specs = [pl.BlockSpec((tn, w), lambda j, i: (i, j)) for _, w, _ in outs]
    out_specs += [pl.BlockSpec((1, w), lambda j, i: (0, j)) for _, w in accs]
    out_shape = [jax.ShapeDtypeStruct((n, tw), dt) for tw, _, dt in outs]
    out_shape += [jax.ShapeDtypeStruct((1, tw), F32) for tw, _ in accs]
    return _call(body, name=name, grid=(ncol, n // tn), in_specs=in_specs, out_specs=out_specs, out_shape=out_shape,
                 args=[r[0] for r in rows] + [v[0] for v in vecs], semantics=("parallel", "arbitrary"),
                 exchange=exchange)


def _accumulate(ref, val, first):
    @pl.when(first)
    def _():
        ref[...] = jnp.zeros_like(ref)

    ref[...] += val


def _colsum(x):
    return jnp.sum(x, axis=0, keepdims=True)


def _rowmean(x):
    return jnp.mean(x, axis=-1, keepdims=True)


def _rms_fwd(x, g, *, name, exchange=None):
    def fn(x_, g_):
        r = lax.rsqrt(_rowmean(x_ * x_) + RMS_EPS)
        return (x_ * r * g_,)

    n, d = x.shape
    return _rowwise(fn, name=name, n=n, tn=512, ncol=1, rows=[(x, d, 0)], vecs=[(g, d, 0)], outs=[(d, d, BF16)],
                    exchange=exchange)[0]


def _mm_nt_rms_bwd(pairs, x, extra, g, *, name, tn, half_scale, exchange=None):
    n, d = x.shape
    tn = _tile(n, tn, 16)
    pairs = [(list(a) if isinstance(a, (list, tuple)) else [a], b, t) for a, b, t in pairs]
    nref = sum(len(a) + 1 for a, _, _ in pairs)

    def body(*refs):
        x_ref, e_ref, g_ref, dx_ref, dxs_ref, dg_ref = refs[nref:]
        dh, at = None, 0
        for parts, _, transposed in pairs:
            b_ref = refs[at + len(parts)]
            col = 0
            for j, part in enumerate(parts):
                w = part.shape[1]
                if transposed:
                    term = _dot(refs[at + j][...], b_ref[col:col + w, :])
                else:
                    term = _dot_nt(refs[at + j][...], b_ref[:, col:col + w])
                dh = term if dh is None else dh + term
                col += w
            at += len(parts) + 1
        x_ = x_ref[...]
        r = lax.rsqrt(_rowmean(x_ * x_) + RMS_EPS)
        xh = x_ * r
        dxh = dh * g_ref[...]
        dx = e_ref[...] + r * (dxh - xh * _rowmean(dxh * xh))
        dx_ref[...] = dx
        dxs_ref[...] = (dx * half_scale).astype(dxs_ref.dtype)
        _accumulate(dg_ref, _colsum(dh * xh), pl.program_id(0) == 0)

    in_specs, args = [], []
    for parts, b, transposed in pairs:
        assert sum(part.shape[1] for part in parts) == b.shape[0 if transposed else 1]
        in_specs += [pl.BlockSpec((tn, part.shape[1]), lambda i: (i, 0)) for part in parts]
        in_specs.append(pl.BlockSpec(b.shape, lambda i: (0, 0), pipeline_mode=pl.Buffered(1)))
        args += parts + [b]
    row = pl.BlockSpec((tn, d), lambda i: (i, 0))
    vec = pl.BlockSpec((1, d), lambda i: (0, 0))
    return _call(body, name=name, grid=(n // tn,), in_specs=in_specs + [row, row, vec], out_specs=[row, row, vec],
                 out_shape=[jax.ShapeDtypeStruct((n, d), F32), jax.ShapeDtypeStruct((n, d), BF16),
                            jax.ShapeDtypeStruct((1, d), F32)],
                 args=args + [x, extra, g], semantics=("arbitrary",), exchange=exchange)


def _ple_final(x3, gpre, z, tgt, gpp, gf, *, name):
    def fn(x3_, gpre_, z_, tgt_, gpp_, gf_):
        gate = _sigmoid(gpre_)
        rz = lax.rsqrt(_rowmean(z_ * z_) + RMS_EPS)
        zh = z_ * rz
        e = zh * gpp_
        x4 = x3_ + gate * e
        r4 = lax.rsqrt(_rowmean(x4 * x4) + RMS_EPS)
        x4h = x4 * r4
        diff = x4h * gf_ - tgt_
        dout = diff * (1.0 / D_MODEL)
        dxh4 = dout * gf_
        dx4 = r4 * (dxh4 - x4h * _rowmean(dxh4 * x4h))
        dpre = dx4 * e * gate * (1.0 - gate)
        de = dx4 * gate
        dzh = de * gpp_
        dz = rz * (dzh - zh * _rowmean(dzh * zh))
        return dx4, dpre, dz, _colsum(diff * diff) * (0.5 / D_MODEL), _colsum(dout * x4h), _colsum(de * zh)

    n, d = x3.shape
    return _rowwise(fn, name=name, n=n, tn=256, ncol=1, rows=[(x3, d, 0), (gpre, d, 0), (z, d, 0), (tgt, d, 0)],
                    vecs=[(gpp, d, 0), (gf, d, 0)], outs=[(d, d, F32), (d, d, BF16), (d, d, BF16)],
                    accs=[(d, d), (d, d), (d, d)])


def _merge_fwd(proj, ya, yb, *, name):
    def fn(ga, gb, ya_, yb_):
        return (_sigmoid(ga) * ya_ + _sigmoid(gb) * yb_,)

    n = proj.shape[0]
    w = 512
    return _rowwise(fn, name=name, n=n, tn=512, ncol=D_MODEL // w,
                    rows=[(proj, w, COL_GA // w), (proj, w, COL_GB // w), (ya, w, 0), (yb, w, 0)], vecs=[],
                    outs=[(D_MODEL, w, BF16)])[0]


def _merge_bwd(dy, proj, ya, yb, *, name):
    def fn(dy_, ga, gb, ya_, yb_):
        sa, sb = _sigmoid(ga), _sigmoid(gb)
        return dy_ * sa, dy_ * sb, dy_ * ya_ * sa * (1.0 - sa), dy_ * yb_ * sb * (1.0 - sb)

    n = proj.shape[0]
    w = 512
    return _rowwise(fn, name=name, n=n, tn=512, ncol=D_MODEL // w,
                    rows=[(dy, w, 0), (proj, w, COL_GA // w), (proj, w, COL_GB // w), (ya, w, 0), (yb, w, 0)],
                    vecs=[], outs=[(D_MODEL, w, BF16)] * 4)


def _head_mean(x):
    return jnp.concatenate(
        [jnp.broadcast_to(jnp.mean(x[:, h * HEAD_DIM:(h + 1) * HEAD_DIM], axis=-1, keepdims=True),
                          (x.shape[0], HEAD_DIM)) for h in range(HEADS)], axis=1)


def _hgrn_post_fwd(o, proj, onorm, *, name):
    def fn(o_, og, gam):
        r = lax.rsqrt(_head_mean(o_ * o_) + RMS_EPS)
        return (o_ * r * gam * (og * _sigmoid(og)),)

    n = o.shape[0]
    w = D_MODEL
    return _rowwise(fn, name=name, n=n, tn=256, ncol=1, rows=[(o, w, 0), (proj, w, COL_OG // w)],
                    vecs=[(onorm, w, 0)], outs=[(D_MODEL, w, BF16)])[0]


def _hgrn_post_bwd(don, o, proj, onorm, *, name):
    def fn(don_, o_, og, gam):
        r = lax.rsqrt(_head_mean(o_ * o_) + RMS_EPS)
        oh = o_ * r
        sg = _sigmoid(og)
        dog = don_ * oh * gam * (sg * (1.0 + og * (1.0 - sg)))
        dn = don_ * (og * sg)
        doh = dn * gam
        do = r * (doh - oh * _head_mean(doh * oh))
        return dog, do, _colsum(dn * oh)

    n = o.shape[0]
    w = D_MODEL
    return _rowwise(fn, name=name, n=n, tn=256, ncol=1, rows=[(don, w, 0), (o, w, 0), (proj, w, COL_OG // w)],
                    vecs=[(onorm, w, 0)], outs=[(D_MODEL, w, BF16), (D_MODEL, w, BF16)], accs=[(D_MODEL, w)])


def _tri_sum(tri, x):
    hi = x.astype(BF16)
    lo = (x - hi.astype(F32)).astype(BF16)
    return _dot(tri, hi) + _dot(tri, lo)


def _lower_bound(lb_ref):
    return 1.0 / (1.0 + jnp.exp(lb_ref[1:2, :] - lb_ref[0:1, :]))


def _hgrn_specs(n, t, reverse):
    nt = n // t
    width = HGRN_HEADS_PER_STEP * HEAD_DIM

    def tok(i):
        return nt - 1 - i if reverse else i

    def sec(col):
        c0 = col // width
        return pl.BlockSpec((t, width), lambda h, i: (tok(i), c0 + h))

    head_tile = pl.BlockSpec((t, width), lambda h, i: (tok(i), h))
    state = pl.BlockSpec((HGRN_HEADS_PER_STEP, t // CHUNK, HEAD_DIM, HEAD_DIM), lambda h, i: (h, tok(i), 0, 0))
    lb = pl.BlockSpec((2, width), lambda h, i: (0, h))
    return sec, head_tile, state, lb


def _hgrn_fwd(proj, hgrn_lb, *, name):
    n = proj.shape[0]
    t = _tile(n, 512, CHUNK)
    nc = t // CHUNK
    hps = HGRN_HEADS_PER_STEP
    width = hps * HEAD_DIM
    lanes = [slice(h * HEAD_DIM, (h + 1) * HEAD_DIM) for h in range(hps)]
    sec, head_tile, state, lbspec = _hgrn_specs(n, t, False)

    def body(q_ref, f_ref, i_ref, lb_ref, o_ref, st_ref, s_acc, g_s, a_s):
        @pl.when(pl.program_id(1) == 0)
        def _():
            s_acc[...] = jnp.zeros_like(s_acc)

        lb = _lower_bound(lb_ref)
        row = lax.broadcasted_iota(jnp.int32, (CHUNK, CHUNK), 0)
        col = lax.broadcasted_iota(jnp.int32, (CHUNK, CHUNK), 1)
        tril = row >= col
        trilb = jnp.where(tril, 1.0, 0.0).astype(BF16)
        rowk = lax.broadcasted_iota(jnp.int32, (CHUNK, width), 0)

        def chunk(c, carry):
            rows = pl.ds(pl.multiple_of(c * CHUNK, CHUNK), CHUNK)
            qr, fr, v = [r[rows, :].astype(F32) for r in (q_ref, f_ref, i_ref)]
            q = qr * _sigmoid(qr)
            f = lb + (1.0 - lb) * _sigmoid(fr)
            k = 1.0 - f
            g = _tri_sum(trilb, jnp.log(f))
            g_s[...] = g
            st0 = [s_acc[h] for h in range(hps)]
            for h in range(hps):
                st_ref[h, c] = st0[h]
            vb = v.astype(BF16)
            for blk in range(CHUNK // SUB):
                lo, hi = blk * SUB, (blk + 1) * SUB
                gref = g_s[lo - 1:lo, :] if blk else jnp.zeros((1, width), F32)
                qi = (q[lo:hi] * jnp.exp(g[lo:hi] - gref)).astype(BF16)
                ki = (k * jnp.exp(jnp.where(rowk < hi, gref - g, NEG_BIG))).astype(BF16)
                for h, ln in enumerate(lanes):
                    a_s[h, lo:hi, :] = _dot_nt(qi[:, ln], ki[:, ln])
            qeb = (q * jnp.exp(g)).astype(BF16)
            o_ref[rows, :] = jnp.concatenate(
                [_dot(jnp.where(tril, a_s[h], 0.0).astype(BF16), vb[:, ln]) + _dot_nt(qeb[:, ln], st0[h].astype(BF16))
                 for h, ln in enumerate(lanes)], axis=1).astype(o_ref.dtype)
            glast = g_s[CHUNK - 1:CHUNK, :]
            kdb = (k * jnp.exp(glast - g)).astype(BF16)
            dec = jnp.exp(glast)
            for h, ln in enumerate(lanes):
                s_acc[h] = st0[h] * dec[:, ln] + _dot_tn(vb[:, ln], kdb[:, ln])
            return carry

        lax.fori_loop(0, nc, chunk, 0)

    return pl.pallas_call(
        body, name=name, grid=(HEADS // hps, n // t),
        in_specs=[sec(COL_Q), sec(COL_F), sec(COL_I), lbspec], out_specs=[head_tile, state],
        out_shape=[jax.ShapeDtypeStruct((n, D_MODEL), BF16),
                   jax.ShapeDtypeStruct((HEADS, n // CHUNK, HEAD_DIM, HEAD_DIM), F32)],
        scratch_shapes=[pltpu.VMEM((hps, HEAD_DIM, HEAD_DIM), F32), pltpu.VMEM((CHUNK, width), F32),
                        pltpu.VMEM((hps, CHUNK, CHUNK), F32)],
        compiler_params=_params("parallel", "arbitrary"),
    )(proj, proj, proj, hgrn_lb)


def _hgrn_bwd(proj, hgrn_lb, do, states, *, name, exchange=None):
    n = proj.shape[0]
    t = _tile(n, 512, CHUNK)
    nc = t // CHUNK
    hps = HGRN_HEADS_PER_STEP
    width = hps * HEAD_DIM
    lanes = [slice(h * HEAD_DIM, (h + 1) * HEAD_DIM) for h in range(hps)]
    sec, head_tile, state, lbspec = _hgrn_specs(n, t, True)

    def body(q_ref, f_ref, i_ref, lb_ref, do_ref, st_ref, dqfi_ref, dlb_ref, d_acc, g_s, a_s, dq_s,
             dg_s):
        first = pl.program_id(1) == 0

        @pl.when(first)
        def _():
            d_acc[...] = jnp.zeros_like(d_acc)

        lb = _lower_bound(lb_ref)
        row = lax.broadcasted_iota(jnp.int32, (CHUNK, CHUNK), 0)
        col = lax.broadcasted_iota(jnp.int32, (CHUNK, CHUNK), 1)
        tril = row >= col
        trilb = jnp.where(tril, 1.0, 0.0).astype(BF16)
        triub = jnp.where(row <= col, 1.0, 0.0).astype(BF16)
        rowk = lax.broadcasted_iota(jnp.int32, (CHUNK, width), 0)

        def per_head(fn):
            return jnp.concatenate([fn(h, ln) for h, ln in enumerate(lanes)], axis=1)

        def chunk(j, dlb):
            c = nc - 1 - j
            rows = pl.ds(pl.multiple_of(c * CHUNK, CHUNK), CHUNK)
            qr, fr, v, dout = [r[rows, :].astype(F32) for r in (q_ref, f_ref, i_ref, do_ref)]
            sq = _sigmoid(qr)
            q = qr * sq
            sf = _sigmoid(fr)
            f = lb + (1.0 - lb) * sf
            k = 1.0 - f
            g = _tri_sum(trilb, jnp.log(f))
            g_s[...] = g
            st0 = [st_ref[h, c] for h in range(hps)]
            dt = [d_acc[h] for h in range(hps)]
            vb, dob = v.astype(BF16), dout.astype(BF16)
            dtb = [x.astype(BF16) for x in dt]
            st0b = [x.astype(BF16) for x in st0]
            glast = g_s[CHUNK - 1:CHUNK, :]
            eg = jnp.exp(g)
            kdec = jnp.exp(glast - g)
            qeb, kdb = (q * eg).astype(BF16), (k * kdec).astype(BF16)
            aps = [jnp.where(row > col, _dot_nt(dob[:, ln], vb[:, ln]), 0.0) for ln in lanes]
            dov = dout * v
            adiag = per_head(lambda h, ln: jnp.broadcast_to(
                jnp.sum(dov[:, ln], axis=-1, keepdims=True), (CHUNK, HEAD_DIM)))
            dq_inter = per_head(lambda h, ln: _dot(dob[:, ln], st0b[h]))
            dk_inter = per_head(lambda h, ln: _dot(vb[:, ln], dtb[h]))
            dk_st = kdec * dk_inter
            dg = qeb.astype(F32) * dq_inter
            dg_minus = kdb.astype(F32) * dk_inter
            dg = dg - dg_minus
            for blk in range(CHUNK // SUB):
                lo, hi = blk * SUB, (blk + 1) * SUB
                gref = g_s[lo - 1:lo, :] if blk else jnp.zeros((1, width), F32)
                qscale = jnp.exp(g[lo:hi] - gref)
                kscale = jnp.exp(jnp.where(rowk < hi, gref - g, NEG_BIG))
                qi = (q[lo:hi] * qscale).astype(BF16)
                ki = (k * kscale).astype(BF16)
                for h, ln in enumerate(lanes):
                    a_s[h, lo:hi, :] = _dot_nt(qi[:, ln], ki[:, ln])
                apb = [x[lo:hi].astype(BF16) for x in aps]
                from_k = per_head(lambda h, ln: _dot(apb[h], ki[:, ln]))
                from_q = per_head(lambda h, ln: _dot_tn(apb[h], qi[:, ln]))
                dq_s[lo:hi, :] = qscale * from_k
                dg_s[lo:hi, :] = qi.astype(F32) * from_k
                dk_st = dk_st + kscale * from_q
                dg = dg - ki.astype(F32) * from_q
            dg = dg + dg_s[...]
            dv = per_head(lambda h, ln: _dot_tn(jnp.where(tril, a_s[h], 0.0).astype(BF16), dob[:, ln])
                          + _dot_nt(kdb[:, ln], dtb[h]))
            dq_st = dq_s[...] + eg * dq_inter
            dq = dq_st + adiag * k
            dk = dk_st + adiag * q
            dec = jnp.exp(glast)
            dt_dec = [dt[h] * dec[:, ln] for h, ln in enumerate(lanes)]
            for h, ln in enumerate(lanes):
                d_acc[h] = dt_dec[h] + _dot_tn(dob[:, ln], qeb[:, ln])
            later = per_head(lambda h, ln: _colsum(dt_dec[h] * st0[h])) + _colsum(dg_minus)
            dlf = later + _tri_sum(triub, dg)
            df = dlf / f - dk
            dqfi_ref[rows, 0:width] = (dq * (sq * (1.0 + qr * (1.0 - sq)))).astype(dqfi_ref.dtype)
            dqfi_ref[rows, width:2 * width] = (df * (1.0 - lb) * sf * (1.0 - sf)).astype(dqfi_ref.dtype)
            dqfi_ref[rows, 2 * width:3 * width] = dv.astype(dqfi_ref.dtype)
            return dlb + _colsum(df * (1.0 - sf))

        dlb = lax.fori_loop(0, nc, chunk, jnp.zeros((1, width), F32))
        _accumulate(dlb_ref, dlb, first)

    assert hps == HEADS
    nt = n // t
    return _call(
        body, name=name, grid=(1, nt),
        in_specs=[sec(COL_Q), sec(COL_F), sec(COL_I), lbspec, head_tile, state],
        out_specs=[pl.BlockSpec((t, 3 * width), lambda h, i: (nt - 1 - i, 0)),
                   pl.BlockSpec((1, width), lambda h, i: (0, h))],
        out_shape=[jax.ShapeDtypeStruct((n, 3 * D_MODEL), BF16), jax.ShapeDtypeStruct((1, D_MODEL), F32)],
        args=[proj, proj, proj, hgrn_lb, do, states], semantics=("parallel", "arbitrary"),
        scratch=[pltpu.VMEM((hps, HEAD_DIM, HEAD_DIM), F32), pltpu.VMEM((CHUNK, width), F32),
                 pltpu.VMEM((hps, CHUNK, CHUNK), F32), pltpu.VMEM((CHUNK, width), F32),
                 pltpu.VMEM((CHUNK, width), F32)],
        exchange=exchange)


def _pool_fwd(proj, pool_w, pool_scale, *, name):
    n = proj.shape[0]
    t = _tile(n, 512, POOL_HALO)
    per = t // POOL_HALO
    c0 = COL_POOL // POOL_WIDTH

    def body(u_ref, halo_ref, pw_ref, ps_ref, pooled_ref, mixed_ref, ext):
        i = pl.program_id(0)
        u = u_ref[...].astype(F32)
        ext[POOL_HALO:POOL_HALO + t, :] = u
        ext[0:POOL_HALO, :] = jnp.where(i > 0, halo_ref[...].astype(F32), 0.0)
        pos = i * t + lax.broadcasted_iota(jnp.int32, (t, POOL_CH), 0) + 1
        for grp, win in enumerate(POOL_WINDOWS):
            cols = slice(grp * POOL_CH, (grp + 1) * POOL_CH)
            acc = u[:, cols]
            for j in range(1, win):
                acc = acc + ext[POOL_HALO - j:POOL_HALO - j + t, cols]
            pooled = (acc / jnp.minimum(pos, win).astype(F32) - u[:, cols]).astype(BF16)
            pooled_ref[:, cols] = pooled
            mixed_ref[:, cols] = (_dot(pooled, pw_ref[grp].astype(BF16)) * ps_ref[:, cols]).astype(BF16)

    tile = pl.BlockSpec((t, POOL_WIDTH), lambda i: (i, 0))
    return pl.pallas_call(
        body, name=name, grid=(n // t,),
        in_specs=[pl.BlockSpec((t, POOL_WIDTH), lambda i: (i, c0)),
                  pl.BlockSpec((POOL_HALO, POOL_WIDTH), lambda i: (jnp.maximum(i * per - 1, 0), c0)),
                  pl.BlockSpec((len(POOL_WINDOWS), POOL_CH, POOL_CH), lambda i: (0, 0, 0)),
                  pl.BlockSpec((1, POOL_WIDTH), lambda i: (0, 0))],
        out_specs=[tile, tile],
        out_shape=[jax.ShapeDtypeStruct((n, POOL_WIDTH), BF16), jax.ShapeDtypeStruct((n, POOL_WIDTH), BF16)],
        scratch_shapes=[pltpu.VMEM((t + POOL_HALO, POOL_WIDTH), F32)],
        compiler_params=_params("parallel"),
    )(proj, proj, pool_w, pool_scale)


def _pool_bwd(dmixed, pooled, pool_w, pool_scale, *, name):
    n = dmixed.shape[0]
    t = _tile(n, 512, POOL_HALO)
    per = t // POOL_HALO
    nb = n // t

    def body(dm_ref, dmh_ref, p_ref, pw_ref, ps_ref, du_ref, dpw_ref, dps_ref, ext):
        i = pl.program_id(0)

        @pl.when(i == 0)
        def _():
            dpw_ref[...] = jnp.zeros_like(dpw_ref)
            dps_ref[...] = jnp.zeros_like(dps_ref)

        dm, dmh = dm_ref[...], dmh_ref[...]
        pos = i * t + lax.broadcasted_iota(jnp.int32, (t, POOL_CH), 0) + 1
        for grp, win in enumerate(POOL_WINDOWS):
            cols = slice(grp * POOL_CH, (grp + 1) * POOL_CH)
            pwb = pw_ref[grp].astype(BF16)
            pb = p_ref[:, cols]
            scale = ps_ref[:, cols]
            dps_ref[:, cols] += _colsum(dm[:, cols] * _dot(pb, pwb))
            dpm = (dm[:, cols] * scale).astype(BF16)
            dpw_ref[grp] += _dot_tn(pb, dpm)
            dpool = _dot_nt(dpm, pwb)
            dpool_next = _dot_nt((dmh[:, cols] * scale).astype(BF16), pwb)
            ext[0:t, cols] = dpool / jnp.minimum(pos, win).astype(F32)
            ext[t:t + POOL_HALO, cols] = jnp.where(i < nb - 1, dpool_next * (1.0 / win), 0.0)
            acc = -dpool
            for j in range(win):
                acc = acc + ext[j:j + t, cols]
            du_ref[:, cols] = acc.astype(du_ref.dtype)

    tile = pl.BlockSpec((t, POOL_WIDTH), lambda i: (i, 0))
    return pl.pallas_call(
        body, name=name, grid=(nb,),
        in_specs=[tile, pl.BlockSpec((POOL_HALO, POOL_WIDTH), lambda i: (jnp.minimum((i + 1) * per, nb * per - 1), 0)),
                  tile, pl.BlockSpec((len(POOL_WINDOWS), POOL_CH, POOL_CH), lambda i: (0, 0, 0)),
                  pl.BlockSpec((1, POOL_WIDTH), lambda i: (0, 0))],
        out_specs=[tile, pl.BlockSpec((len(POOL_WINDOWS), POOL_CH, POOL_CH), lambda i: (0, 0, 0)),
                   pl.BlockSpec((1, POOL_WIDTH), lambda i: (0, 0))],
        out_shape=[jax.ShapeDtypeStruct((n, POOL_WIDTH), BF16),
                   jax.ShapeDtypeStruct((len(POOL_WINDOWS), POOL_CH, POOL_CH), F32),
                   jax.ShapeDtypeStruct((1, POOL_WIDTH), F32)],
        scratch_shapes=[pltpu.VMEM((t + POOL_HALO, POOL_WIDTH), F32)],
        compiler_params=_params("arbitrary"),
    )(dmixed, dmixed, pooled, pool_w, pool_scale)


def _adamw(w, g, m, v):
    m2 = ADAM_B1 * m + (1.0 - ADAM_B1) * g
    v2 = ADAM_B2 * v + (1.0 - ADAM_B2) * (g * g)
    m_hat = m2 * (1.0 / (1.0 - ADAM_B1 ** ADAM_STEP))
    v_hat = v2 * (1.0 / (1.0 - ADAM_B2 ** ADAM_STEP))
    delta = -ADAM_LR * (m_hat / (jnp.sqrt(v_hat) + ADAM_EPS) + ADAM_WD * w)
    return delta, m2, v2


def _adam_big(recv, w, m, v, *, name):
    r, c = w.shape
    tr = _tile(r, 256, 16)

    def body(recv_ref, w_ref, m_ref, v_ref, g_ref, d_ref, m2_ref, v2_ref):
        g = recv_ref[0].astype(F32)
        for i in range(1, N_DEV):
            g = g + recv_ref[i].astype(F32)
        delta, m2, v2 = _adamw(w_ref[...], g, m_ref[...], v_ref[...])
        g_ref[...] = g
        d_ref[...] = delta
        m2_ref[...] = m2
        v2_ref[...] = v2

    tile = pl.BlockSpec((tr, c), lambda i: (i, 0))
    out = jax.ShapeDtypeStruct((r, c), F32)
    return pl.pallas_call(
        body, name=name, grid=(r // tr,),
        in_specs=[pl.BlockSpec((N_DEV, tr, c), lambda i: (0, i, 0)), tile, tile, tile],
        out_specs=[tile] * 4, out_shape=[out] * 4, compiler_params=_params("parallel"),
    )(recv, w, m, v)


def _adam_small(parts, w, m, v, *, name):
    n = len(SMALL_PARAMS)

    def body(*refs):
        parts_r, w_r, m_r, v_r = (refs[i * n:(i + 1) * n] for i in range(4))
        outs = refs[4 * n:]
        for j, key in enumerate(SMALL_PARAMS):
            g = parts_r[j][0]
            for i in range(1, N_DEV):
                g = g + parts_r[j][i]
            w_ = w_r[j][...]
            if key == "hgrn_lb":
                s0 = 1.0 / (1.0 + jnp.exp(w_[1:2] - w_[0:1]))
                ga = g * s0 * (1.0 - s0)
                sign = jnp.where(lax.broadcasted_iota(jnp.int32, w_.shape, 0) == 0, 1.0, -1.0)
                g = sign * jnp.broadcast_to(ga, w_.shape)
            delta, m2, v2 = _adamw(w_, g, m_r[j][...], v_r[j][...])
            for q, val in enumerate((g, delta, m2, v2)):
                outs[q * n + j][...] = val

    out_shape = [jax.ShapeDtypeStruct(w[k].shape, F32) for _ in range(4) for k in SMALL_PARAMS]
    res = pl.pallas_call(body, name=name, out_shape=out_shape, compiler_params=_params())(
        *[t[k] for t in (parts, w, m, v) for k in SMALL_PARAMS])
    return [res[q * n:(q + 1) * n] for q in range(4)]


def _as_2d(a):
    return a.reshape(-1, a.shape[-1])


SPLIT_AXIS = dict(BIG_WEIGHTS)


def _gather_of(names, weights):
    return _Exchange([_shard_to_send(weights[k][0], SPLIT_AXIS[k]) for k in names], gather=True)


def _scatter_of(names, dfull):
    return _Exchange([_to_slots(dfull[k], SPLIT_AXIS[k]) for k in names], gather=False)


def _shard_to_send(w, axis):
    return (w.T if axis == 1 else w).astype(BF16)


def _to_slots(dw, axis):
    rows, cols = dw.shape
    return dw.reshape(N_DEV, rows // N_DEV, cols)


def _from_slots(gathered, axis):
    _, r, c = gathered.shape
    return gathered.reshape(N_DEV * r, c)


def kernel(x, p, ffn1_norm, ffn1_w1, ffn1_w3, ffn1_w2, mix_norm, w_in, hgrn_lb, hgrn_onorm, w_branch_a, pool_w, pool_scale, w_branch_b, w_out, ffn2_norm, ffn2_w1, ffn2_w3, ffn2_w2, ple_norm, ple_w_gate, ple_w_proj, ple_post_norm, final_norm, loss_target, m_ffn1_norm, m_ffn1_w1, m_ffn1_w3, m_ffn1_w2, m_mix_norm, m_w_in, m_hgrn_lb, m_hgrn_onorm, m_w_branch_a, m_pool_w, m_pool_scale, m_w_branch_b, m_w_out, m_ffn2_norm, m_ffn2_w1, m_ffn2_w3, m_ffn2_w2, m_ple_norm, m_ple_w_gate, m_ple_w_proj, m_ple_post_norm, m_final_norm, v_ffn1_norm, v_ffn1_w1, v_ffn1_w3, v_ffn1_w2, v_mix_norm, v_w_in, v_hgrn_lb, v_hgrn_onorm, v_w_branch_a, v_pool_w, v_pool_scale, v_w_branch_b, v_w_out, v_ffn2_norm, v_ffn2_w1, v_ffn2_w3, v_ffn2_w2, v_ple_norm, v_ple_w_gate, v_ple_w_proj, v_ple_post_norm, v_final_norm):
    weights = dict(ffn1_norm=ffn1_norm, ffn1_w1=ffn1_w1, ffn1_w3=ffn1_w3, ffn1_w2=ffn1_w2, mix_norm=mix_norm, w_in=w_in, hgrn_lb=hgrn_lb, hgrn_onorm=hgrn_onorm, w_branch_a=w_branch_a, pool_w=pool_w, pool_scale=pool_scale, w_branch_b=w_branch_b, w_out=w_out, ffn2_norm=ffn2_norm, ffn2_w1=ffn2_w1, ffn2_w3=ffn2_w3, ffn2_w2=ffn2_w2, ple_norm=ple_norm, ple_w_gate=ple_w_gate, ple_w_proj=ple_w_proj, ple_post_norm=ple_post_norm, final_norm=final_norm)
    mom1 = dict(ffn1_norm=m_ffn1_norm, ffn1_w1=m_ffn1_w1, ffn1_w3=m_ffn1_w3, ffn1_w2=m_ffn1_w2, mix_norm=m_mix_norm, w_in=m_w_in, hgrn_lb=m_hgrn_lb, hgrn_onorm=m_hgrn_onorm, w_branch_a=m_w_branch_a, pool_w=m_pool_w, pool_scale=m_pool_scale, w_branch_b=m_w_branch_b, w_out=m_w_out, ffn2_norm=m_ffn2_norm, ffn2_w1=m_ffn2_w1, ffn2_w3=m_ffn2_w3, ffn2_w2=m_ffn2_w2, ple_norm=m_ple_norm, ple_w_gate=m_ple_w_gate, ple_w_proj=m_ple_w_proj, ple_post_norm=m_ple_post_norm, final_norm=m_final_norm)
    mom2 = dict(ffn1_norm=v_ffn1_norm, ffn1_w1=v_ffn1_w1, ffn1_w3=v_ffn1_w3, ffn1_w2=v_ffn1_w2, mix_norm=v_mix_norm, w_in=v_w_in, hgrn_lb=v_hgrn_lb, hgrn_onorm=v_hgrn_onorm, w_branch_a=v_w_branch_a, pool_w=v_pool_w, pool_scale=v_pool_scale, w_branch_b=v_w_branch_b, w_out=v_w_out, ffn2_norm=v_ffn2_norm, ffn2_w1=v_ffn2_w1, ffn2_w3=v_ffn2_w3, ffn2_w2=v_ffn2_w2, ple_norm=v_ple_norm, ple_w_gate=v_ple_w_gate, ple_w_proj=v_ple_w_proj, ple_post_norm=v_ple_post_norm, final_norm=v_final_norm)

    xs = x[0]
    ps = p[0, 0].astype(BF16)
    tgt = loss_target[0]
    n = xs.shape[0]

    g_f1, g_mix, g_on, g_f2 = ffn1_norm, mix_norm, hgrn_onorm, ffn2_norm
    g_ple, g_post, g_fin = ple_norm, ple_post_norm, final_norm.reshape(1, D_MODEL)
    lb2 = hgrn_lb
    pw, pscale = pool_w[0], pool_scale

    full = {}

    def keep(names, gathered):
        for k, g in zip(names, gathered):
            full[k] = _from_slots(g, SPLIT_AXIS[k])

    names = ("ffn1_w1", "ffn1_w3")
    ex = _gather_of(names, weights)
    h1 = _rms_fwd(xs, g_f1, name="ffn1_rms", exchange=ex)
    keep(names, ex.received)
    names = ("ffn1_w2", "w_in")
    ex = _gather_of(names, weights)
    a1, b1, s1 = _ffn_up(h1, full["ffn1_w1"], full["ffn1_w3"], name="ffn1_up", exchange=ex)
    keep(names, ex.received)
    names = ("w_branch_a", "w_branch_b", "w_out")
    ex = _gather_of(names, weights)
    x1, h2 = _mm_nn_res_rms(s1, full["ffn1_w2"], xs, g_mix, name="ffn1_down", scale=0.5, exchange=ex)
    keep(names, ex.received)
    names = ("ffn2_w1", "ffn2_w3", "ffn2_w2", "ple_w_gate", "ple_w_proj")
    ex = _gather_of(names, weights)
    proj = _mm_nn_wide(h2, full["w_in"], name="w_in_proj", out_dtype=BF16, exchange=ex)
    keep(names, ex.received)
    o, states = _hgrn_fwd(proj, lb2, name="hgrn_fwd")
    on = _hgrn_post_fwd(o, proj, g_on, name="hgrn_post_fwd")
    ya = _mm_nn_wide(on, full["w_branch_a"], name="branch_a", out_dtype=BF16, b_is_km=True)
    pooled, mixed = _pool_fwd(proj, pw, pscale, name="pool_fwd")
    yb = _mm_nn_wide(mixed, full["w_branch_b"], name="branch_b", out_dtype=BF16)
    y = _merge_fwd(proj, ya, yb, name="merge_fwd")
    x2, h3 = _mm_nn_res_rms(y, full["w_out"], x1, g_f2, name="w_out_proj", scale=1.0)
    a2, b2, s2 = _ffn_up(h3, full["ffn2_w1"], full["ffn2_w3"], name="ffn2_up")
    x3, h4 = _mm_nn_res_rms(s2, full["ffn2_w2"], x2, g_ple, name="ffn2_down", scale=0.5)
    gpre = _mm_nn_wide(h4, full["ple_w_gate"], name="ple_gate", out_dtype=BF16, b_is_km=True)
    z = _mm_nn_wide(ps, full["ple_w_proj"], name="ple_proj", out_dtype=BF16)
    dx4, dpre, dz, loss_part, d_fin, d_post = _ple_final(x3, gpre, z, tgt, g_post, g_fin, name="ple_final")

    dfull, received = {}, {}

    def sent(names, exchange):
        received.update(zip(names, exchange.received))

    dfull["ple_w_proj"] = _mm_tn(ps, dz, name="d_ple_w_proj", tn=1024, tm=1024, transpose_out=True)
    dfull["ple_w_gate"] = _mm_tn(h4, dpre, name="d_ple_w_gate", tn=1024, tm=1024)
    dx3, dx3s, d_ple = _mm_nt_rms_bwd([(dpre, full["ple_w_gate"], False)], x3, dx4, g_ple, name="ple_rms_bwd", tn=512,
                                      half_scale=0.5)

    names = ("ple_w_proj", "ple_w_gate")
    ex = _scatter_of(names, dfull)
    da2, db2 = _ffn_bwd_mid(dx3s, full["ffn2_w2"], a2, b2, name="ffn2_bwd_mid", exchange=ex)
    sent(names, ex)
    dfull["ffn2_w2"] = _mm_tn(s2, dx3s, name="ffn2_dw2", tn=1024, tm=1024)
    dfull["ffn2_w1"] = _mm_tn(h3, da2, name="ffn2_dw1", tn=1024, tm=2816, transpose_out=True)
    dfull["ffn2_w3"] = _mm_tn(h3, db2, name="ffn2_dw3", tn=1024, tm=2816, transpose_out=True)
    names = ("ffn2_w2",)
    ex = _scatter_of(names, dfull)
    dx2, dx2b, d_f2 = _mm_nt_rms_bwd([(da2, full["ffn2_w1"], True), (db2, full["ffn2_w3"], True)], x2, dx3, g_f2,
                                     name="ffn2_rms_bwd", tn=512, half_scale=1.0, exchange=ex)
    sent(names, ex)

    dfull["w_out"] = _mm_tn(y, dx2b, name="d_w_out", tn=1024, tm=1024)
    dy = _mm_nn_wide(dx2b, full["w_out"], name="d_y", out_dtype=BF16)
    dya, dyb, dga, dgb = _merge_bwd(dy, proj, ya, yb, name="merge_bwd")

    dfull["w_branch_b"] = _mm_tn(mixed, dyb, name="d_w_branch_b", tn=1024, tm=1024, transpose_out=True)
    dmixed = _mm_nn_wide(dyb, full["w_branch_b"], name="d_mixed", out_dtype=F32, b_is_km=True)
    du, d_pw, d_ps = _pool_bwd(dmixed, pooled, pw, pscale, name="pool_bwd")

    dfull["w_branch_a"] = _mm_tn(on, dya, name="d_w_branch_a", tn=1024, tm=1024)
    don = _mm_nn_wide(dya, full["w_branch_a"], name="d_on", out_dtype=BF16)
    dog, do, d_on = _hgrn_post_bwd(don, o, proj, g_on, name="hgrn_post_bwd")
    names = ("ffn2_w1", "ffn2_w3", "w_out", "w_branch_b", "w_branch_a")
    ex = _scatter_of(names, dfull)
    dqfi, d_lb = _hgrn_bwd(proj, lb2, do, states, name="hgrn_bwd", exchange=ex)
    sent(names, ex)
    dproj = [dqfi, dog, du, dga, dgb]
    dfull["w_in"] = jnp.concatenate(
        [_mm_tn(h2, part, name=f"d_w_in_{j}", tn=1024, tm=3072, transpose_out=True) for j, part in enumerate(dproj)],
        axis=0)
    names = ("w_in",)
    ex = _scatter_of(names, dfull)
    dx1, dx1s, d_mix = _mm_nt_rms_bwd([(dproj, full["w_in"], True)], x1, dx2, g_mix, name="mix_rms_bwd", tn=512,
                                      half_scale=0.5, exchange=ex)
    sent(names, ex)

    da1, db1 = _ffn_bwd_mid(dx1s, full["ffn1_w2"], a1, b1, name="ffn1_bwd_mid")
    dfull["ffn1_w2"] = _mm_tn(s1, dx1s, name="ffn1_dw2", tn=1024, tm=1024)
    names = ("ffn1_w2",)
    ex = _scatter_of(names, dfull)
    dfull["ffn1_w1"] = _mm_tn(h1, da1, name="ffn1_dw1", tn=1024, tm=2816, transpose_out=True, exchange=ex)
    sent(names, ex)
    names = ("ffn1_w1",)
    ex = _scatter_of(names, dfull)
    dfull["ffn1_w3"] = _mm_tn(h1, db1, name="ffn1_dw3", tn=1024, tm=2816, transpose_out=True, exchange=ex)
    sent(names, ex)
    names = ("ffn1_w3",)
    ex = _scatter_of(names, dfull)
    grad_x, _, d_f1 = _mm_nt_rms_bwd([(da1, full["ffn1_w1"], True), (db1, full["ffn1_w3"], True)], xs, dx1, g_f1,
                                     name="ffn1_rms_bwd", tn=512, half_scale=1.0, exchange=ex)
    sent(names, ex)

    small_part = dict(ffn1_norm=d_f1, mix_norm=d_mix, hgrn_onorm=d_on, ffn2_norm=d_f2, ple_norm=d_ple,
                      ple_post_norm=d_post, final_norm=d_fin, hgrn_lb=d_lb, pool_scale=d_ps, pool_w=_as_2d(d_pw))
    gathered_small = _exchange_now([small_part[k] for k in SMALL_PARAMS] + [loss_part], name="gather_small_grads",
                                   gather=True)
    small_all = dict(zip(SMALL_PARAMS, gathered_small))
    loss = jnp.sum(gathered_small[-1])

    grads, deltas, new_m, new_v = {}, {}, {}, {}
    for name, axis in BIG_WEIGHTS:
        shape, recv = weights[name].shape, received[name]
        own = [t[name][0].T if axis == 1 else t[name][0] for t in (weights, mom1, mom2)]
        res = _adam_big(recv, *own, name=f"adam_{name}")
        grads[name], deltas[name], new_m[name], new_v[name] = [(r.T if axis == 1 else r).reshape(shape) for r in res]
    res = _adam_small(small_all, *[{k: _as_2d(t[k]) for k in SMALL_PARAMS} for t in (weights, mom1, mom2)],
                      name="adam_small")
    for store, vals in zip((grads, deltas, new_m, new_v), res):
        store.update({k: val.reshape(weights[k].shape) for k, val in zip(SMALL_PARAMS, vals)})

    return (loss, grad_x.reshape(x.shape), *[grads[k] for k in WEIGHT_ORDER], *[deltas[k] for k in WEIGHT_ORDER],
            *[new_m[k] for k in WEIGHT_ORDER], *[new_v[k] for k in WEIGHT_ORDER])
```

```python
import jax
import jax.numpy as jnp
from jax import lax
from jax.experimental import pallas as pl
from jax.experimental.pallas import tpu as pltpu

F32 = jnp.float32
BF16 = jnp.bfloat16

N_DEV = 8
D_MODEL = 1024
HEADS = 8
HEAD_DIM = 128
POOL_WINDOWS = (2, 4, 8, 16)
POOL_CH = 128
POOL_WIDTH = 512
POOL_HALO = 16
RMS_EPS = 1e-6
CHUNK = 64
SUB = 32
HGRN_HEADS_PER_STEP = 8
NEG_BIG = -1e30

ADAM_LR = 0.001
ADAM_B1 = 0.9
ADAM_B2 = 0.999
ADAM_EPS = 1e-08
ADAM_WD = 0.01
ADAM_STEP = 10

V7X_VMEM_BYTES = 64 * 1024 * 1024
VMEM_LIMIT = (V7X_VMEM_BYTES * 3) // 4
EXCHANGE_TAIL_STEPS = 3
ROW_TILE_CAP = 8192

COL_Q, COL_F, COL_I, COL_OG, COL_POOL, COL_GA, COL_GB = 0, 1024, 2048, 3072, 4096, 4608, 5632

BIG_WEIGHTS = (
    ("ffn1_w1", 1), ("ffn1_w3", 1), ("ffn1_w2", 0), ("w_in", 1), ("w_branch_a", 0), ("w_branch_b", 1),
    ("w_out", 0), ("ffn2_w1", 1), ("ffn2_w3", 1), ("ffn2_w2", 0), ("ple_w_gate", 0), ("ple_w_proj", 1),
)
SMALL_PARAMS = ("ffn1_norm", "mix_norm", "hgrn_onorm", "ffn2_norm", "ple_norm", "ple_post_norm", "final_norm",
                "hgrn_lb", "pool_scale", "pool_w")
WEIGHT_ORDER = (
    "ffn1_norm", "ffn1_w1", "ffn1_w3", "ffn1_w2", "mix_norm", "w_in", "hgrn_lb", "hgrn_onorm", "w_branch_a", "pool_w",
    "pool_scale", "w_branch_b", "w_out", "ffn2_norm", "ffn2_w1", "ffn2_w3", "ffn2_w2", "ple_norm", "ple_w_gate",
    "ple_w_proj", "ple_post_norm", "final_norm",
)


def _params(*sem):
    return pltpu.CompilerParams(dimension_semantics=sem if sem else None, vmem_limit_bytes=VMEM_LIMIT)


COL_CHUNK = 256


def _rows(tn, width):
    return pl.BlockSpec((tn, width), lambda i: (i, 0))


def _resident(shape):
    return pl.BlockSpec(shape, lambda i: (0,) * len(shape), pipeline_mode=pl.Buffered(1))


def _dot(a, b):
    return jnp.dot(a, b, preferred_element_type=F32)


def _dot_nt(a, b):
    return lax.dot_general(a, b, (((1,), (1,)), ((), ())), preferred_element_type=F32)


def _dot_tn(a, b):
    return lax.dot_general(a, b, (((0,), (0,)), ((), ())), preferred_element_type=F32)


def _sigmoid(x):
    return 0.5 * jnp.tanh(0.5 * x) + 0.5


def _tile(n, want, mult):
    if mult != 128:
        want = min(want, ROW_TILE_CAP)
    if n <= want:
        return n
    t = (want // mult) * mult
    while t > mult and n % t:
        t -= mult
    assert n % t == 0, (n, want, mult)
    return t


class _Exchange:
    COPIES = N_DEV - 1

    def __init__(self, arrs, gather):
        self.arrs, self.gather, self.n = list(arrs), gather, len(arrs)
        self.out_shape = [jax.ShapeDtypeStruct((N_DEV,) + (a.shape if gather else a.shape[1:]), a.dtype) for a in arrs]
        self.scratch = [pltpu.SemaphoreType.DMA((self.n * self.COPIES,)),
                        pltpu.SemaphoreType.DMA((self.n * self.COPIES,)), pltpu.SemaphoreType.DMA((self.n,))]
        self.received = None

    @staticmethod
    def _place():
        x, y, c = lax.axis_index("x"), lax.axis_index("y"), lax.axis_index("c")
        return x, y, c

    def _copy(self, a, k, src, dst, to, sems):
        s = a * self.COPIES + k
        return pltpu.make_async_remote_copy(src_ref=src, dst_ref=dst, send_sem=sems[0].at[s], recv_sem=sems[1].at[s],
                                            device_id=to, device_id_type=pl.DeviceIdType.MESH)

    def _gather_copies(self, role, ins, outs, sems):
        x, y, c = self._place()
        chips = [(1 - x, y), (x, 1 - y), (1 - x, 1 - y)]
        sibling = (x, y, 1 - c)

        def slot(px, py, pc):
            return 4 * px + 2 * py + pc

        copies = []
        for a in range(self.n):
            mine = outs[a].at[slot(x, y, c)]
            if role == "first":
                copies.append(self._copy(a, 0, ins[a], mine, sibling, sems))
            elif role == "last":
                copies.append(self._copy(a, 0, ins[a], outs[a].at[slot(x, y, 1 - c)], sibling, sems))
            for j, (px, py) in enumerate(chips):
                theirs = outs[a].at[slot(px, py, c)]
                if role == "first":
                    copies.append(self._copy(a, 1 + j, ins[a], mine, (px, py, c), sems))
                elif role == "landed":
                    copies.append(self._copy(a, 1 + j, ins[a], theirs, (px, py, c), sems))
                elif role == "onward":
                    copies.append(self._copy(a, 4 + j, theirs, theirs, sibling, sems))
                else:
                    copies.append(self._copy(a, 4 + j, ins[a], outs[a].at[slot(px, py, 1 - c)], sibling, sems))
        return copies

    def _scatter_copies(self, role, ins, outs, sems):
        x, y, c = self._place()
        me = 4 * x + 2 * y + c
        copies = []
        for k in range(1, N_DEV):
            px = 1 - x if k & 4 else x
            py = 1 - y if k & 2 else y
            pc = 1 - c if k & 1 else c
            peer = 4 * px + 2 * py + pc
            for a in range(self.n):
                dst = outs[a].at[me] if role == "sends" else outs[a].at[peer]
                copies.append(self._copy(a, k - 1, ins[a].at[peer], dst, (px, py, pc), sems))
        return copies

    def _local(self, ins, outs, sems):
        x, y, c = self._place()
        me = 4 * x + 2 * y + c
        return [pltpu.make_async_copy(ins[a] if self.gather else ins[a].at[me], outs[a].at[me], sems[2].at[a])
                for a in range(self.n)]

    def start(self, ins, outs, sems):
        for cp in self._local(ins, outs, sems):
            cp.start()
        sends = (self._gather_copies("first", ins, outs, sems) if self.gather
                 else self._scatter_copies("sends", ins, outs, sems))
        for cp in sends:
            cp.start()

    def pass_on(self, ins, outs, sems):
        if self.gather:
            for landed, onward in zip(self._gather_copies("landed", ins, outs, sems),
                                      self._gather_copies("onward", ins, outs, sems)):
                landed.wait_recv()
                onward.start()

    def finish(self, ins, outs, sems):
        if self.gather:
            arrivals = self._gather_copies("last", ins, outs, sems)
            sends = self._gather_copies("first", ins, outs, sems) + self._gather_copies("onward", ins, outs, sems)
        else:
            arrivals = self._scatter_copies("arrivals", ins, outs, sems)
            sends = self._scatter_copies("sends", ins, outs, sems)
        for cp in arrivals:
            cp.wait_recv()
        for cp in sends:
            cp.wait_send()
        for cp in self._local(ins, outs, sems):
            cp.wait()


def _call(body, *, name, grid, in_specs, out_specs, out_shape, args, semantics, scratch=(), exchange=None):
    if exchange is None:
        return pl.pallas_call(
            body, name=name, grid=grid, in_specs=in_specs, out_specs=out_specs, out_shape=out_shape,
            scratch_shapes=list(scratch), compiler_params=_params(*semantics))(*args)
    ex = exchange
    n_in, n_out, n_s = len(in_specs), len(out_specs), len(scratch)

    def wrapped(*refs):
        ins, ex_in = refs[:n_in], refs[n_in:n_in + ex.n]
        o0 = n_in + ex.n
        outs, ex_out = refs[o0:o0 + n_out], refs[o0 + n_out:o0 + n_out + ex.n]
        s0 = o0 + n_out + ex.n
        scr, sems = refs[s0:s0 + n_s], refs[s0 + n_s:]
        step, steps = pl.program_id(0), grid[0]
        for ax in range(1, len(grid)):
            step, steps = step * grid[ax] + pl.program_id(ax), steps * grid[ax]

        @pl.when(step == 0)
        def _():
            ex.start(ex_in, ex_out, sems)

        body(*ins, *outs, *scr)

        @pl.when(step == max(steps - EXCHANGE_TAIL_STEPS, 0))
        def _():
            ex.pass_on(ex_in, ex_out, sems)

        @pl.when(step == steps - 1)
        def _():
            ex.finish(ex_in, ex_out, sems)

    hbm = pl.BlockSpec(memory_space=pltpu.HBM)
    res = pl.pallas_call(
        wrapped, name=name, grid=grid, in_specs=list(in_specs) + [hbm] * ex.n,
        out_specs=list(out_specs) + [hbm] * ex.n, out_shape=list(out_shape) + ex.out_shape,
        scratch_shapes=list(scratch) + ex.scratch, compiler_params=_params(*(["arbitrary"] * len(grid))),
    )(*args, *ex.arrs)
    ex.received = res[n_out:]
    return res[:n_out]


def _exchange_now(arrs, *, name, gather):
    ex = _Exchange(arrs, gather)
    n = ex.n

    def body(*refs):
        ex.start(refs[:n], refs[n:2 * n], refs[2 * n:])
        ex.pass_on(refs[:n], refs[n:2 * n], refs[2 * n:])
        ex.finish(refs[:n], refs[n:2 * n], refs[2 * n:])

    hbm = pl.BlockSpec(memory_space=pltpu.HBM)
    return pl.pallas_call(body, name=name, out_shape=ex.out_shape, in_specs=[hbm] * n, out_specs=[hbm] * n,
                          scratch_shapes=ex.scratch)(*arrs)


def _mm_tn(a, b, *, name, tn, tm, transpose_out=False, exchange=None):
    n, k = a.shape
    m = b.shape[1]
    tn, tm = _tile(n, tn, 16), _tile(m, tm, 128)
    steps = n // tn

    def body(a_ref, b_ref, o_ref, acc):
        i = pl.program_id(1)

        @pl.when(i == 0)
        def _():
            acc[...] = jnp.zeros_like(acc)

        acc[...] += _dot_tn(a_ref[...], b_ref[...])

        @pl.when(i == steps - 1)
        def _():
            for c0 in range(0, tm, COL_CHUNK):
                cols = slice(c0, min(c0 + COL_CHUNK, tm))
                if transpose_out:
                    o_ref[cols, :] = acc[:, cols].T.astype(o_ref.dtype)
                else:
                    o_ref[:, cols] = acc[:, cols].astype(o_ref.dtype)

    if transpose_out:
        out_spec, out_shape = pl.BlockSpec((tm, k), lambda j, i: (j, 0)), jax.ShapeDtypeStruct((m, k), BF16)
    else:
        out_spec, out_shape = pl.BlockSpec((k, tm), lambda j, i: (0, j)), jax.ShapeDtypeStruct((k, m), BF16)
    return _call(body, name=name, grid=(m // tm, steps),
                 in_specs=[pl.BlockSpec((tn, k), lambda j, i: (i, 0)), pl.BlockSpec((tn, tm), lambda j, i: (i, j))],
                 out_specs=[out_spec], out_shape=[out_shape], args=[a, b], semantics=("parallel", "arbitrary"),
                 scratch=[pltpu.VMEM((k, tm), F32)], exchange=exchange)[0]


def _ffn_up(h, w1, w3, *, name, tn=512, exchange=None):
    n, k = h.shape
    m = w1.shape[0]
    tn = _tile(n, tn, 16)

    def body(h_ref, w1_ref, w3_ref, dsda_ref, dsdb_ref, s_ref):
        for c0 in range(0, m, COL_CHUNK):
            cols = slice(c0, c0 + COL_CHUNK)
            a = _dot_nt(h_ref[...], w1_ref[cols, :])
            b = _dot_nt(h_ref[...], w3_ref[cols, :])
            sg = _sigmoid(a)
            silu = a * sg
            dsda_ref[:, cols] = (b * (sg + silu * (1.0 - sg))).astype(dsda_ref.dtype)
            dsdb_ref[:, cols] = silu.astype(dsdb_ref.dtype)
            s_ref[:, cols] = (silu * b).astype(s_ref.dtype)

    ospec = _rows(tn, m)
    return _call(body, name=name, grid=(n // tn,),
                 in_specs=[_rows(tn, k), _resident(w1.shape), _resident(w3.shape)], out_specs=[ospec, ospec, ospec],
                 out_shape=[jax.ShapeDtypeStruct((n, m), BF16)] * 3,
                 args=[h, w1, w3], semantics=("parallel",), exchange=exchange)


def _mm_nn_wide(a, b, *, name, out_dtype, b_is_km=False, exchange=None):
    n, k = a.shape
    m = b.shape[1 if b_is_km else 0]
    tn = _tile(n, 512, 16)
    chunk = min(2 * COL_CHUNK, m)

    def body(a_ref, b_ref, o_ref):
        for c0 in range(0, m, chunk):
            cols = slice(c0, c0 + chunk)
            if b_is_km:
                res = _dot(a_ref[...], b_ref[:, cols])
            else:
                res = _dot_nt(a_ref[...], b_ref[cols, :])
            o_ref[:, cols] = res.astype(o_ref.dtype)

    return _call(body, name=name, grid=(n // tn,), in_specs=[_rows(tn, k), _resident(b.shape)],
                 out_specs=[_rows(tn, m)], out_shape=[jax.ShapeDtypeStruct((n, m), out_dtype)], args=[a, b],
                 semantics=("parallel",), exchange=exchange)[0]


def _mm_nn_res_rms(a, b, res, g, *, name, scale, exchange=None):
    n, k = a.shape
    d = b.shape[1]
    tn = _tile(n, 512, 16)

    def body(a_ref, b_ref, r_ref, g_ref, x_ref, h_ref):
        for c0 in range(0, d, COL_CHUNK):
            cols = slice(c0, c0 + COL_CHUNK)
            x_ref[:, cols] = r_ref[:, cols] + scale * _dot(a_ref[...], b_ref[:, cols])
        x = x_ref[...]
        r = lax.rsqrt(_rowmean(x * x) + RMS_EPS)
        h_ref[...] = (x * r * g_ref[...]).astype(h_ref.dtype)

    row = _rows(tn, d)
    return _call(body, name=name, grid=(n // tn,),
                 in_specs=[_rows(tn, k), _resident(b.shape), row, pl.BlockSpec((1, d), lambda i: (0, 0))],
                 out_specs=[row, row],
                 out_shape=[jax.ShapeDtypeStruct((n, d), F32), jax.ShapeDtypeStruct((n, d), BF16)],
                 args=[a, b, res, g], semantics=("parallel",), exchange=exchange)


def _ffn_bwd_mid(dxs, w2, dsda, dsdb, *, name, exchange=None):
    n, d = dxs.shape
    m = w2.shape[0]
    tn = _tile(n, 512, 16)

    def body(dx_ref, w2_ref, dsda_ref, dsdb_ref, da_ref, db_ref):
        for c0 in range(0, m, COL_CHUNK):
            cols = slice(c0, c0 + COL_CHUNK)
            ds = _dot_nt(dx_ref[...], w2_ref[cols, :])
            da_ref[:, cols] = (ds * dsda_ref[:, cols].astype(F32)).astype(da_ref.dtype)
            db_ref[:, cols] = (ds * dsdb_ref[:, cols].astype(F32)).astype(db_ref.dtype)

    tile = _rows(tn, m)
    return _call(body, name=name, grid=(n // tn,),
                 in_specs=[_rows(tn, d), _resident(w2.shape), tile, tile], out_specs=[tile, tile],
                 out_shape=[jax.ShapeDtypeStruct((n, m), BF16), jax.ShapeDtypeStruct((n, m), BF16)],
                 args=[dxs, w2, dsda, dsdb], semantics=("parallel",), exchange=exchange)


def _rowwise(fn, *, name, n, tn, ncol, rows, vecs, outs, accs=(), exchange=None):
    tn = _tile(n, tn, 16)
    nr, nv, no = len(rows), len(vecs), len(outs)

    def body(*refs):
        first = pl.program_id(1) == 0
        vals = [r[...].astype(F32) for r in refs[:nr + nv]]
        res = fn(*vals)
        for ref, val in zip(refs[nr + nv:nr + nv + no], res[:no]):
            ref[...] = val.astype(ref.dtype)
        for ref, val in zip(refs[nr + nv + no:], res[no:]):
            _accumulate(ref, val, first)

    in_specs = [pl.BlockSpec((tn, w), lambda j, i, c0=c0: (i, c0 + j)) for _, w, c0 in rows]
    in_specs += [pl.BlockSpec((1, w), lambda j, i, c0=c0: (0, c0 + j)) for _, w, c0 in vecs]
    out_specs = [pl.BlockSpec((tn, w), lambda j, i: (i, j)) for _, w, _ in outs]
    out_specs += [pl.BlockSpec((1, w), lambda j, i: (0, j)) for _, w in accs]
    out_shape = [jax.ShapeDtypeStruct((n, tw), dt) for tw, _, dt in outs]
    out_shape += [jax.ShapeDtypeStruct((1, tw), F32) for tw, _ in accs]
    return _call(body, name=name, grid=(ncol, n // tn), in_specs=in_specs, out_specs=out_specs, out_shape=out_shape,
                 args=[r[0] for r in rows] + [v[0] for v in vecs], semantics=("parallel", "arbitrary"),
                 exchange=exchange)


def _accumulate(ref, val, first):
    @pl.when(first)
    def _():
        ref[...] = jnp.zeros_like(ref)

    ref[...] += val


def _colsum(x):
    return jnp.sum(x, axis=0, keepdims=True)


def _rowmean(x):
    return jnp.mean(x, axis=-1, keepdims=True)


def _rms_fwd(x, g, *, name, exchange=None):
    def fn(x_, g_):
        r = lax.rsqrt(_rowmean(x_ * x_) + RMS_EPS)
        return (x_ * r * g_,)

    n, d = x.shape
    return _rowwise(fn, name=name, n=n, tn=512, ncol=1, rows=[(x, d, 0)], vecs=[(g, d, 0)], outs=[(d, d, BF16)],
                    exchange=exchange)[0]


def _mm_nt_rms_bwd(pairs, x, extra, g, *, name, tn, half_scale, exchange=None):
    n, d = x.shape
    tn = _tile(n, tn, 16)
    pairs = [(list(a) if isinstance(a, (list, tuple)) else [a], b, t) for a, b, t in pairs]
    nref = sum(len(a) + 1 for a, _, _ in pairs)

    def body(*refs):
        x_ref, e_ref, g_ref, dx_ref, dxs_ref, dg_ref = refs[nref:]
        dh, at = None, 0
        for parts, _, transposed in pairs:
            b_ref = refs[at + len(parts)]
            col = 0
            for j, part in enumerate(parts):
                w = part.shape[1]
                if transposed:
                    term = _dot(refs[at + j][...], b_ref[col:col + w, :])
                else:
                    term = _dot_nt(refs[at + j][...], b_ref[:, col:col + w])
                dh = term if dh is None else dh + term
                col += w
            at += len(parts) + 1
        x_ = x_ref[...]
        r = lax.rsqrt(_rowmean(x_ * x_) + RMS_EPS)
        xh = x_ * r
        dxh = dh * g_ref[...]
        dx = e_ref[...] + r * (dxh - xh * _rowmean(dxh * xh))
        dx_ref[...] = dx
        dxs_ref[...] = (dx * half_scale).astype(dxs_ref.dtype)
        _accumulate(dg_ref, _colsum(dh * xh), pl.program_id(0) == 0)

    in_specs, args = [], []
    for parts, b, transposed in pairs:
        assert sum(part.shape[1] for part in parts) == b.shape[0 if transposed else 1]
        in_specs += [pl.BlockSpec((tn, part.shape[1]), lambda i: (i, 0)) for part in parts]
        in_specs.append(pl.BlockSpec(b.shape, lambda i: (0, 0), pipeline_mode=pl.Buffered(1)))
        args += parts + [b]
    row = pl.BlockSpec((tn, d), lambda i: (i, 0))
    vec = pl.BlockSpec((1, d), lambda i: (0, 0))
    return _call(body, name=name, grid=(n // tn,), in_specs=in_specs + [row, row, vec], out_specs=[row, row, vec],
                 out_shape=[jax.ShapeDtypeStruct((n, d), F32), jax.ShapeDtypeStruct((n, d), BF16),
                            jax.ShapeDtypeStruct((1, d), F32)],
                 args=args + [x, extra, g], semantics=("arbitrary",), exchange=exchange)


def _ple_final(x3, gpre, z, tgt, gpp, gf, *, name):
    def fn(x3_, gpre_, z_, tgt_, gpp_, gf_):
        gate = _sigmoid(gpre_)
        rz = lax.rsqrt(_rowmean(z_ * z_) + RMS_EPS)
        zh = z_ * rz
        e = zh * gpp_
        x4 = x3_ + gate * e
        r4 = lax.rsqrt(_rowmean(x4 * x4) + RMS_EPS)
        x4h = x4 * r4
        diff = x4h * gf_ - tgt_
        dout = diff * (1.0 / D_MODEL)
        dxh4 = dout * gf_
        dx4 = r4 * (dxh4 - x4h * _rowmean(dxh4 * x4h))
        dpre = dx4 * e * gate * (1.0 - gate)
        de = dx4 * gate
        dzh = de * gpp_
        dz = rz * (dzh - zh * _rowmean(dzh * zh))
        return dx4, dpre, dz, _colsum(diff * diff) * (0.5 / D_MODEL), _colsum(dout * x4h), _colsum(de * zh)

    n, d = x3.shape
    return _rowwise(fn, name=name, n=n, tn=512, ncol=1, rows=[(x3, d, 0), (gpre, d, 0), (z, d, 0), (tgt, d, 0)],
                    vecs=[(gpp, d, 0), (gf, d, 0)], outs=[(d, d, F32), (d, d, BF16), (d, d, BF16)],
                    accs=[(d, d), (d, d), (d, d)])


def _merge_fwd(proj, ya, yb, *, name):
    def fn(ga, gb, ya_, yb_):
        return (_sigmoid(ga) * ya_ + _sigmoid(gb) * yb_,)

    n = proj.shape[0]
    w = 512
    return _rowwise(fn, name=name, n=n, tn=1024, ncol=D_MODEL // w,
                    rows=[(proj, w, COL_GA // w), (proj, w, COL_GB // w), (ya, w, 0), (yb, w, 0)], vecs=[],
                    outs=[(D_MODEL, w, BF16)])[0]


def _merge_bwd(dy, proj, ya, yb, *, name):
    def fn(dy_, ga, gb, ya_, yb_):
        sa, sb = _sigmoid(ga), _sigmoid(gb)
        return dy_ * sa, dy_ * sb, dy_ * ya_ * sa * (1.0 - sa), dy_ * yb_ * sb * (1.0 - sb)

    n = proj.shape[0]
    w = 512
    return _rowwise(fn, name=name, n=n, tn=1024, ncol=D_MODEL // w,
                    rows=[(dy, w, 0), (proj, w, COL_GA // w), (proj, w, COL_GB // w), (ya, w, 0), (yb, w, 0)],
                    vecs=[], outs=[(D_MODEL, w, BF16)] * 4)


def _head_mean(x):
    return jnp.concatenate(
        [jnp.broadcast_to(jnp.mean(x[:, h * HEAD_DIM:(h + 1) * HEAD_DIM], axis=-1, keepdims=True),
                          (x.shape[0], HEAD_DIM)) for h in range(HEADS)], axis=1)


def _hgrn_post_fwd(o, proj, onorm, *, name):
    def fn(o_, og, gam):
        r = lax.rsqrt(_head_mean(o_ * o_) + RMS_EPS)
        return (o_ * r * gam * (og * _sigmoid(og)),)

    n = o.shape[0]
    w = D_MODEL
    return _rowwise(fn, name=name, n=n, tn=512, ncol=1, rows=[(o, w, 0), (proj, w, COL_OG // w)],
                    vecs=[(onorm, w, 0)], outs=[(D_MODEL, w, BF16)])[0]


def _hgrn_post_bwd(don, o, proj, onorm, *, name):
    def fn(don_, o_, og, gam):
        r = lax.rsqrt(_head_mean(o_ * o_) + RMS_EPS)
        oh = o_ * r
        sg = _sigmoid(og)
        dog = don_ * oh * gam * (sg * (1.0 + og * (1.0 - sg)))
        dn = don_ * (og * sg)
        doh = dn * gam
        do = r * (doh - oh * _head_mean(doh * oh))
        return dog, do, _colsum(dn * oh)

    n = o.shape[0]
    w = D_MODEL
    return _rowwise(fn, name=name, n=n, tn=512, ncol=1, rows=[(don, w, 0), (o, w, 0), (proj, w, COL_OG // w)],
                    vecs=[(onorm, w, 0)], outs=[(D_MODEL, w, BF16), (D_MODEL, w, BF16)], accs=[(D_MODEL, w)])


def _tri_sum(tri, x):
    hi = x.astype(BF16)
    lo = (x - hi.astype(F32)).astype(BF16)
    return _dot(tri, hi) + _dot(tri, lo)


def _lower_bound(lb_ref):
    return 1.0 / (1.0 + jnp.exp(lb_ref[1:2, :] - lb_ref[0:1, :]))


def _hgrn_specs(n, t, reverse):
    nt = n // t
    width = HGRN_HEADS_PER_STEP * HEAD_DIM

    def tok(i):
        return nt - 1 - i if reverse else i

    def sec(col):
        c0 = col // width
        return pl.BlockSpec((t, width), lambda h, i: (tok(i), c0 + h))

    head_tile = pl.BlockSpec((t, width), lambda h, i: (tok(i), h))
    state = pl.BlockSpec((HGRN_HEADS_PER_STEP, t // CHUNK, HEAD_DIM, HEAD_DIM), lambda h, i: (h, tok(i), 0, 0))
    lb = pl.BlockSpec((2, width), lambda h, i: (0, h))
    return sec, head_tile, state, lb


def _hgrn_fwd(proj, hgrn_lb, *, name):
    n = proj.shape[0]
    t = _tile(n, 512, CHUNK)
    nc = t // CHUNK
    hps = HGRN_HEADS_PER_STEP
    width = hps * HEAD_DIM
    lanes = [slice(h * HEAD_DIM, (h + 1) * HEAD_DIM) for h in range(hps)]
    sec, head_tile, state, lbspec = _hgrn_specs(n, t, False)

    def body(q_ref, f_ref, i_ref, lb_ref, o_ref, st_ref, s_acc, g_s, a_s):
        @pl.when(pl.program_id(1) == 0)
        def _():
            s_acc[...] = jnp.zeros_like(s_acc)

        lb = _lower_bound(lb_ref)
        row = lax.broadcasted_iota(jnp.int32, (CHUNK, CHUNK), 0)
        col = lax.broadcasted_iota(jnp.int32, (CHUNK, CHUNK), 1)
        tril = row >= col
        trilb = jnp.where(tril, 1.0, 0.0).astype(BF16)
        rowk = lax.broadcasted_iota(jnp.int32, (CHUNK, width), 0)

        def chunk(c, carry):
            rows = pl.ds(pl.multiple_of(c * CHUNK, CHUNK), CHUNK)
            qr, fr, v = [r[rows, :].astype(F32) for r in (q_ref, f_ref, i_ref)]
            q = qr * _sigmoid(qr)
            f = lb + (1.0 - lb) * _sigmoid(fr)
            k = 1.0 - f
            g = _tri_sum(trilb, jnp.log(f))
            g_s[...] = g
            st0 = [s_acc[h] for h in range(hps)]
            for h in range(hps):
                st_ref[h, c] = st0[h]
            vb = v.astype(BF16)
            for blk in range(CHUNK // SUB):
                lo, hi = blk * SUB, (blk + 1) * SUB
                gref = g_s[lo - 1:lo, :] if blk else jnp.zeros((1, width), F32)
                qi = (q[lo:hi] * jnp.exp(g[lo:hi] - gref)).astype(BF16)
                ki = (k * jnp.exp(jnp.where(rowk < hi, gref - g, NEG_BIG))).astype(BF16)
                for h, ln in enumerate(lanes):
                    a_s[h, lo:hi, :] = _dot_nt(qi[:, ln], ki[:, ln])
            qeb = (q * jnp.exp(g)).astype(BF16)
            o_ref[rows, :] = jnp.concatenate(
                [_dot(jnp.where(tril, a_s[h], 0.0).astype(BF16), vb[:, ln]) + _dot_nt(qeb[:, ln], st0[h].astype(BF16))
                 for h, ln in enumerate(lanes)], axis=1).astype(o_ref.dtype)
            glast = g_s[CHUNK - 1:CHUNK, :]
            kdb = (k * jnp.exp(glast - g)).astype(BF16)
            dec = jnp.exp(glast)
            for h, ln in enumerate(lanes):
                s_acc[h] = st0[h] * dec[:, ln] + _dot_tn(vb[:, ln], kdb[:, ln])
            return carry

        lax.fori_loop(0, nc, chunk, 0)

    return pl.pallas_call(
        body, name=name, grid=(HEADS // hps, n // t),
        in_specs=[sec(COL_Q), sec(COL_F), sec(COL_I), lbspec], out_specs=[head_tile, state],
        out_shape=[jax.ShapeDtypeStruct((n, D_MODEL), BF16),
                   jax.ShapeDtypeStruct((HEADS, n // CHUNK, HEAD_DIM, HEAD_DIM), F32)],
        scratch_shapes=[pltpu.VMEM((hps, HEAD_DIM, HEAD_DIM), F32), pltpu.VMEM((CHUNK, width), F32),
                        pltpu.VMEM((hps, CHUNK, CHUNK), F32)],
        compiler_params=_params("parallel", "arbitrary"),
    )(proj, proj, proj, hgrn_lb)


def _hgrn_bwd(proj, hgrn_lb, do, states, *, name, exchange=None):
    n = proj.shape[0]
    t = _tile(n, 512, CHUNK)
    nc = t // CHUNK
    hps = HGRN_HEADS_PER_STEP
    width = hps * HEAD_DIM
    lanes = [slice(h * HEAD_DIM, (h + 1) * HEAD_DIM) for h in range(hps)]
    sec, head_tile, state, lbspec = _hgrn_specs(n, t, True)

    def body(q_ref, f_ref, i_ref, lb_ref, do_ref, st_ref, dqfi_ref, dlb_ref, d_acc, g_s, a_s, dq_s,
             dg_s):
        first = pl.program_id(1) == 0

        @pl.when(first)
        def _():
            d_acc[...] = jnp.zeros_like(d_acc)

        lb = _lower_bound(lb_ref)
        row = lax.broadcasted_iota(jnp.int32, (CHUNK, CHUNK), 0)
        col = lax.broadcasted_iota(jnp.int32, (CHUNK, CHUNK), 1)
        tril = row >= col
        trilb = jnp.where(tril, 1.0, 0.0).astype(BF16)
        triub = jnp.where(row <= col, 1.0, 0.0).astype(BF16)
        rowk = lax.broadcasted_iota(jnp.int32, (CHUNK, width), 0)

        def per_head(fn):
            return jnp.concatenate([fn(h, ln) for h, ln in enumerate(lanes)], axis=1)

        def chunk(j, dlb):
            c = nc - 1 - j
            rows = pl.ds(pl.multiple_of(c * CHUNK, CHUNK), CHUNK)
            qr, fr, v, dout = [r[rows, :].astype(F32) for r in (q_ref, f_ref, i_ref, do_ref)]
            sq = _sigmoid(qr)
            q = qr * sq
            sf = _sigmoid(fr)
            f = lb + (1.0 - lb) * sf
            k = 1.0 - f
            g = _tri_sum(trilb, jnp.log(f))
            g_s[...] = g
            st0 = [st_ref[h, c] for h in range(hps)]
            dt = [d_acc[h] for h in range(hps)]
            vb, dob = v.astype(BF16), dout.astype(BF16)
            dtb = [x.astype(BF16) for x in dt]
            st0b = [x.astype(BF16) for x in st0]
            glast = g_s[CHUNK - 1:CHUNK, :]
            eg = jnp.exp(g)
            kdec = jnp.exp(glast - g)
            qeb, kdb = (q * eg).astype(BF16), (k * kdec).astype(BF16)
            aps = [jnp.where(row > col, _dot_nt(dob[:, ln], vb[:, ln]), 0.0) for ln in lanes]
            dov = dout * v
            adiag = per_head(lambda h, ln: jnp.broadcast_to(
                jnp.sum(dov[:, ln], axis=-1, keepdims=True), (CHUNK, HEAD_DIM)))
            dq_inter = per_head(lambda h, ln: _dot(dob[:, ln], st0b[h]))
            dk_inter = per_head(lambda h, ln: _dot(vb[:, ln], dtb[h]))
            dk_st = kdec * dk_inter
            dg = qeb.astype(F32) * dq_inter
            dg_minus = kdb.astype(F32) * dk_inter
            dg = dg - dg_minus
            for blk in range(CHUNK // SUB):
                lo, hi = blk * SUB, (blk + 1) * SUB
                gref = g_s[lo - 1:lo, :] if blk else jnp.zeros((1, width), F32)
                qscale = jnp.exp(g[lo:hi] - gref)
                kscale = jnp.exp(jnp.where(rowk < hi, gref - g, NEG_BIG))
                qi = (q[lo:hi] * qscale).astype(BF16)
                ki = (k * kscale).astype(BF16)
                for h, ln in enumerate(lanes):
                    a_s[h, lo:hi, :] = _dot_nt(qi[:, ln], ki[:, ln])
                apb = [x[lo:hi].astype(BF16) for x in aps]
                from_k = per_head(lambda h, ln: _dot(apb[h], ki[:, ln]))
                from_q = per_head(lambda h, ln: _dot_tn(apb[h], qi[:, ln]))
                dq_s[lo:hi, :] = qscale * from_k
                dg_s[lo:hi, :] = qi.astype(F32) * from_k
                dk_st = dk_st + kscale * from_q
                dg = dg - ki.astype(F32) * from_q
            dg = dg + dg_s[...]
            dv = per_head(lambda h, ln: _dot_tn(jnp.where(tril, a_s[h], 0.0).astype(BF16), dob[:, ln])
                          + _dot_nt(kdb[:, ln], dtb[h]))
            dq_st = dq_s[...] + eg * dq_inter
            dq = dq_st + adiag * k
            dk = dk_st + adiag * q
            dec = jnp.exp(glast)
            dt_dec = [dt[h] * dec[:, ln] for h, ln in enumerate(lanes)]
            for h, ln in enumerate(lanes):
                d_acc[h] = dt_dec[h] + _dot_tn(dob[:, ln], qeb[:, ln])
            later = per_head(lambda h, ln: _colsum(dt_dec[h] * st0[h])) + _colsum(dg_minus)
            dlf = later + _tri_sum(triub, dg)
            df = dlf / f - dk
            dqfi_ref[rows, 0:width] = (dq * (sq * (1.0 + qr * (1.0 - sq)))).astype(dqfi_ref.dtype)
            dqfi_ref[rows, width:2 * width] = (df * (1.0 - lb) * sf * (1.0 - sf)).astype(dqfi_ref.dtype)
            dqfi_ref[rows, 2 * width:3 * width] = dv.astype(dqfi_ref.dtype)
            return dlb + _colsum(df * (1.0 - sf))

        dlb = lax.fori_loop(0, nc, chunk, jnp.zeros((1, width), F32))
        _accumulate(dlb_ref, dlb, first)

    assert hps == HEADS
    nt = n // t
    return _call(
        body, name=name, grid=(1, nt),
        in_specs=[sec(COL_Q), sec(COL_F), sec(COL_I), lbspec, head_tile, state],
        out_specs=[pl.BlockSpec((t, 3 * width), lambda h, i: (nt - 1 - i, 0)),
                   pl.BlockSpec((1, width), lambda h, i: (0, h))],
        out_shape=[jax.ShapeDtypeStruct((n, 3 * D_MODEL), BF16), jax.ShapeDtypeStruct((1, D_MODEL), F32)],
        args=[proj, proj, proj, hgrn_lb, do, states], semantics=("parallel", "arbitrary"),
        scratch=[pltpu.VMEM((hps, HEAD_DIM, HEAD_DIM), F32), pltpu.VMEM((CHUNK, width), F32),
                 pltpu.VMEM((hps, CHUNK, CHUNK), F32), pltpu.VMEM((CHUNK, width), F32),
                 pltpu.VMEM((CHUNK, width), F32)],
        exchange=exchange)


def _pool_fwd(proj, pool_w, pool_scale, *, name):
    n = proj.shape[0]
    t = _tile(n, 512, POOL_HALO)
    per = t // POOL_HALO
    c0 = COL_POOL // POOL_WIDTH

    def body(u_ref, halo_ref, pw_ref, ps_ref, pooled_ref, mixed_ref, ext):
        i = pl.program_id(0)
        u = u_ref[...].astype(F32)
        ext[POOL_HALO:POOL_HALO + t, :] = u
        ext[0:POOL_HALO, :] = jnp.where(i > 0, halo_ref[...].astype(F32), 0.0)
        pos = i * t + lax.broadcasted_iota(jnp.int32, (t, POOL_CH), 0) + 1
        for grp, win in enumerate(POOL_WINDOWS):
            cols = slice(grp * POOL_CH, (grp + 1) * POOL_CH)
            acc = u[:, cols]
            for j in range(1, win):
                acc = acc + ext[POOL_HALO - j:POOL_HALO - j + t, cols]
            pooled = (acc / jnp.minimum(pos, win).astype(F32) - u[:, cols]).astype(BF16)
            pooled_ref[:, cols] = pooled
            mixed_ref[:, cols] = (_dot(pooled, pw_ref[grp].astype(BF16)) * ps_ref[:, cols]).astype(BF16)

    tile = pl.BlockSpec((t, POOL_WIDTH), lambda i: (i, 0))
    return pl.pallas_call(
        body, name=name, grid=(n // t,),
        in_specs=[pl.BlockSpec((t, POOL_WIDTH), lambda i: (i, c0)),
                  pl.BlockSpec((POOL_HALO, POOL_WIDTH), lambda i: (jnp.maximum(i * per - 1, 0), c0)),
                  pl.BlockSpec((len(POOL_WINDOWS), POOL_CH, POOL_CH), lambda i: (0, 0, 0)),
                  pl.BlockSpec((1, POOL_WIDTH), lambda i: (0, 0))],
        out_specs=[tile, tile],
        out_shape=[jax.ShapeDtypeStruct((n, POOL_WIDTH), BF16), jax.ShapeDtypeStruct((n, POOL_WIDTH), BF16)],
        scratch_shapes=[pltpu.VMEM((t + POOL_HALO, POOL_WIDTH), F32)],
        compiler_params=_params("parallel"),
    )(proj, proj, pool_w, pool_scale)


def _pool_bwd(dmixed, pooled, pool_w, pool_scale, *, name):
    n = dmixed.shape[0]
    t = _tile(n, 512, POOL_HALO)
    per = t // POOL_HALO
    nb = n // t

    def body(dm_ref, dmh_ref, p_ref, pw_ref, ps_ref, du_ref, dpw_ref, dps_ref, ext):
        i = pl.program_id(0)

        @pl.when(i == 0)
        def _():
            dpw_ref[...] = jnp.zeros_like(dpw_ref)
            dps_ref[...] = jnp.zeros_like(dps_ref)

        dm, dmh = dm_ref[...], dmh_ref[...]
        pos = i * t + lax.broadcasted_iota(jnp.int32, (t, POOL_CH), 0) + 1
        for grp, win in enumerate(POOL_WINDOWS):
            cols = slice(grp * POOL_CH, (grp + 1) * POOL_CH)
            pwb = pw_ref[grp].astype(BF16)
            pb = p_ref[:, cols]
            scale = ps_ref[:, cols]
            dps_ref[:, cols] += _colsum(dm[:, cols] * _dot(pb, pwb))
            dpm = (dm[:, cols] * scale).astype(BF16)
            dpw_ref[grp] += _dot_tn(pb, dpm)
            dpool = _dot_nt(dpm, pwb)
            dpool_next = _dot_nt((dmh[:, cols] * scale).astype(BF16), pwb)
            ext[0:t, cols] = dpool / jnp.minimum(pos, win).astype(F32)
            ext[t:t + POOL_HALO, cols] = jnp.where(i < nb - 1, dpool_next * (1.0 / win), 0.0)
            acc = -dpool
            for j in range(win):
                acc = acc + ext[j:j + t, cols]
            du_ref[:, cols] = acc.astype(du_ref.dtype)

    tile = pl.BlockSpec((t, POOL_WIDTH), lambda i: (i, 0))
    return pl.pallas_call(
        body, name=name, grid=(nb,),
        in_specs=[tile, pl.BlockSpec((POOL_HALO, POOL_WIDTH), lambda i: (jnp.minimum((i + 1) * per, nb * per - 1), 0)),
                  tile, pl.BlockSpec((len(POOL_WINDOWS), POOL_CH, POOL_CH), lambda i: (0, 0, 0)),
                  pl.BlockSpec((1, POOL_WIDTH), lambda i: (0, 0))],
        out_specs=[tile, pl.BlockSpec((len(POOL_WINDOWS), POOL_CH, POOL_CH), lambda i: (0, 0, 0)),
                   pl.BlockSpec((1, POOL_WIDTH), lambda i: (0, 0))],
        out_shape=[jax.ShapeDtypeStruct((n, POOL_WIDTH), BF16),
                   jax.ShapeDtypeStruct((len(POOL_WINDOWS), POOL_CH, POOL_CH), F32),
                   jax.ShapeDtypeStruct((1, POOL_WIDTH), F32)],
        scratch_shapes=[pltpu.VMEM((t + POOL_HALO, POOL_WIDTH), F32)],
        compiler_params=_params("arbitrary"),
    )(dmixed, dmixed, pooled, pool_w, pool_scale)


def _adamw(w, g, m, v):
    m2 = ADAM_B1 * m + (1.0 - ADAM_B1) * g
    v2 = ADAM_B2 * v + (1.0 - ADAM_B2) * (g * g)
    m_hat = m2 * (1.0 / (1.0 - ADAM_B1 ** ADAM_STEP))
    v_hat = v2 * (1.0 / (1.0 - ADAM_B2 ** ADAM_STEP))
    delta = -ADAM_LR * (m_hat / (jnp.sqrt(v_hat) + ADAM_EPS) + ADAM_WD * w)
    return delta, m2, v2


def _adam_big(recv, w, m, v, *, name):
    r, c = w.shape
    tr = _tile(r, 256, 16)

    def body(recv_ref, w_ref, m_ref, v_ref, g_ref, d_ref, m2_ref, v2_ref):
        g = recv_ref[0].astype(F32)
        for i in range(1, N_DEV):
            g = g + recv_ref[i].astype(F32)
        delta, m2, v2 = _adamw(w_ref[...], g, m_ref[...], v_ref[...])
        g_ref[...] = g
        d_ref[...] = delta
        m2_ref[...] = m2
        v2_ref[...] = v2

    tile = pl.BlockSpec((tr, c), lambda i: (i, 0))
    out = jax.ShapeDtypeStruct((r, c), F32)
    return pl.pallas_call(
        body, name=name, grid=(r // tr,),
        in_specs=[pl.BlockSpec((N_DEV, tr, c), lambda i: (0, i, 0)), tile, tile, tile],
        out_specs=[tile] * 4, out_shape=[out] * 4, compiler_params=_params("parallel"),
    )(recv, w, m, v)


def _adam_small(parts, w, m, v, *, name):
    n = len(SMALL_PARAMS)

    def body(*refs):
        parts_r, w_r, m_r, v_r = (refs[i * n:(i + 1) * n] for i in range(4))
        outs = refs[4 * n:]
        for j, key in enumerate(SMALL_PARAMS):
            g = parts_r[j][0]
            for i in range(1, N_DEV):
                g = g + parts_r[j][i]
            w_ = w_r[j][...]
            if key == "hgrn_lb":
                s0 = 1.0 / (1.0 + jnp.exp(w_[1:2] - w_[0:1]))
                ga = g * s0 * (1.0 - s0)
                sign = jnp.where(lax.broadcasted_iota(jnp.int32, w_.shape, 0) == 0, 1.0, -1.0)
                g = sign * jnp.broadcast_to(ga, w_.shape)
            delta, m2, v2 = _adamw(w_, g, m_r[j][...], v_r[j][...])
            for q, val in enumerate((g, delta, m2, v2)):
                outs[q * n + j][...] = val

    out_shape = [jax.ShapeDtypeStruct(w[k].shape, F32) for _ in range(4) for k in SMALL_PARAMS]
    res = pl.pallas_call(body, name=name, out_shape=out_shape, compiler_params=_params())(
        *[t[k] for t in (parts, w, m, v) for k in SMALL_PARAMS])
    return [res[q * n:(q + 1) * n] for q in range(4)]


def _as_2d(a):
    return a.reshape(-1, a.shape[-1])


SPLIT_AXIS = dict(BIG_WEIGHTS)


def _gather_of(names, weights):
    return _Exchange([_shard_to_send(weights[k][0], SPLIT_AXIS[k]) for k in names], gather=True)


def _scatter_of(names, dfull):
    return _Exchange([_to_slots(dfull[k], SPLIT_AXIS[k]) for k in names], gather=False)


def _shard_to_send(w, axis):
    return (w.T if axis == 1 else w).astype(BF16)


def _to_slots(dw, axis):
    rows, cols = dw.shape
    return dw.reshape(N_DEV, rows // N_DEV, cols)


def _from_slots(gathered, axis):
    _, r, c = gathered.shape
    return gathered.reshape(N_DEV * r, c)


def kernel(x, p, ffn1_norm, ffn1_w1, ffn1_w3, ffn1_w2, mix_norm, w_in, hgrn_lb, hgrn_onorm, w_branch_a, pool_w, pool_scale, w_branch_b, w_out, ffn2_norm, ffn2_w1, ffn2_w3, ffn2_w2, ple_norm, ple_w_gate, ple_w_proj, ple_post_norm, final_norm, loss_target, m_ffn1_norm, m_ffn1_w1, m_ffn1_w3, m_ffn1_w2, m_mix_norm, m_w_in, m_hgrn_lb, m_hgrn_onorm, m_w_branch_a, m_pool_w, m_pool_scale, m_w_branch_b, m_w_out, m_ffn2_norm, m_ffn2_w1, m_ffn2_w3, m_ffn2_w2, m_ple_norm, m_ple_w_gate, m_ple_w_proj, m_ple_post_norm, m_final_norm, v_ffn1_norm, v_ffn1_w1, v_ffn1_w3, v_ffn1_w2, v_mix_norm, v_w_in, v_hgrn_lb, v_hgrn_onorm, v_w_branch_a, v_pool_w, v_pool_scale, v_w_branch_b, v_w_out, v_ffn2_norm, v_ffn2_w1, v_ffn2_w3, v_ffn2_w2, v_ple_norm, v_ple_w_gate, v_ple_w_proj, v_ple_post_norm, v_final_norm):
    weights = dict(ffn1_norm=ffn1_norm, ffn1_w1=ffn1_w1, ffn1_w3=ffn1_w3, ffn1_w2=ffn1_w2, mix_norm=mix_norm, w_in=w_in, hgrn_lb=hgrn_lb, hgrn_onorm=hgrn_onorm, w_branch_a=w_branch_a, pool_w=pool_w, pool_scale=pool_scale, w_branch_b=w_branch_b, w_out=w_out, ffn2_norm=ffn2_norm, ffn2_w1=ffn2_w1, ffn2_w3=ffn2_w3, ffn2_w2=ffn2_w2, ple_norm=ple_norm, ple_w_gate=ple_w_gate, ple_w_proj=ple_w_proj, ple_post_norm=ple_post_norm, final_norm=final_norm)
    mom1 = dict(ffn1_norm=m_ffn1_norm, ffn1_w1=m_ffn1_w1, ffn1_w3=m_ffn1_w3, ffn1_w2=m_ffn1_w2, mix_norm=m_mix_norm, w_in=m_w_in, hgrn_lb=m_hgrn_lb, hgrn_onorm=m_hgrn_onorm, w_branch_a=m_w_branch_a, pool_w=m_pool_w, pool_scale=m_pool_scale, w_branch_b=m_w_branch_b, w_out=m_w_out, ffn2_norm=m_ffn2_norm, ffn2_w1=m_ffn2_w1, ffn2_w3=m_ffn2_w3, ffn2_w2=m_ffn2_w2, ple_norm=m_ple_norm, ple_w_gate=m_ple_w_gate, ple_w_proj=m_ple_w_proj, ple_post_norm=m_ple_post_norm, final_norm=m_final_norm)
    mom2 = dict(ffn1_norm=v_ffn1_norm, ffn1_w1=v_ffn1_w1, ffn1_w3=v_ffn1_w3, ffn1_w2=v_ffn1_w2, mix_norm=v_mix_norm, w_in=v_w_in, hgrn_lb=v_hgrn_lb, hgrn_onorm=v_hgrn_onorm, w_branch_a=v_w_branch_a, pool_w=v_pool_w, pool_scale=v_pool_scale, w_branch_b=v_w_branch_b, w_out=v_w_out, ffn2_norm=v_ffn2_norm, ffn2_w1=v_ffn2_w1, ffn2_w3=v_ffn2_w3, ffn2_w2=v_ffn2_w2, ple_norm=v_ple_norm, ple_w_gate=v_ple_w_gate, ple_w_proj=v_ple_w_proj, ple_post_norm=v_ple_post_norm, final_norm=v_final_norm)

    xs = x[0]
    ps = p[0, 0].astype(BF16)
    tgt = loss_target[0]
    n = xs.shape[0]

    g_f1, g_mix, g_on, g_f2 = ffn1_norm, mix_norm, hgrn_onorm, ffn2_norm
    g_ple, g_post, g_fin = ple_norm, ple_post_norm, final_norm.reshape(1, D_MODEL)
    lb2 = hgrn_lb
    pw, pscale = pool_w[0], pool_scale

    full = {}

    def keep(names, gathered):
        for k, g in zip(names, gathered):
            full[k] = _from_slots(g, SPLIT_AXIS[k])

    names = ("ffn1_w1", "ffn1_w3")
    ex = _gather_of(names, weights)
    h1 = _rms_fwd(xs, g_f1, name="ffn1_rms", exchange=ex)
    keep(names, ex.received)
    names = ("ffn1_w2", "w_in")
    ex = _gather_of(names, weights)
    a1, b1, s1 = _ffn_up(h1, full["ffn1_w1"], full["ffn1_w3"], name="ffn1_up", exchange=ex)
    keep(names, ex.received)
    names = ("w_branch_a", "w_branch_b", "w_out")
    ex = _gather_of(names, weights)
    x1, h2 = _mm_nn_res_rms(s1, full["ffn1_w2"], xs, g_mix, name="ffn1_down", scale=0.5, exchange=ex)
    keep(names, ex.received)
    names = ("ffn2_w1", "ffn2_w3", "ffn2_w2", "ple_w_gate", "ple_w_proj")
    ex = _gather_of(names, weights)
    proj = _mm_nn_wide(h2, full["w_in"], name="w_in_proj", out_dtype=BF16, exchange=ex)
    keep(names, ex.received)
    o, states = _hgrn_fwd(proj, lb2, name="hgrn_fwd")
    on = _hgrn_post_fwd(o, proj, g_on, name="hgrn_post_fwd")
    ya = _mm_nn_wide(on, full["w_branch_a"], name="branch_a", out_dtype=BF16, b_is_km=True)
    pooled, mixed = _pool_fwd(proj, pw, pscale, name="pool_fwd")
    yb = _mm_nn_wide(mixed, full["w_branch_b"], name="branch_b", out_dtype=BF16)
    y = _merge_fwd(proj, ya, yb, name="merge_fwd")
    x2, h3 = _mm_nn_res_rms(y, full["w_out"], x1, g_f2, name="w_out_proj", scale=1.0)
    a2, b2, s2 = _ffn_up(h3, full["ffn2_w1"], full["ffn2_w3"], name="ffn2_up")
    x3, h4 = _mm_nn_res_rms(s2, full["ffn2_w2"], x2, g_ple, name="ffn2_down", scale=0.5)
    gpre = _mm_nn_wide(h4, full["ple_w_gate"], name="ple_gate", out_dtype=BF16, b_is_km=True)
    z = _mm_nn_wide(ps, full["ple_w_proj"], name="ple_proj", out_dtype=BF16)
    dx4, dpre, dz, loss_part, d_fin, d_post = _ple_final(x3, gpre, z, tgt, g_post, g_fin, name="ple_final")

    dfull, received = {}, {}

    def sent(names, exchange):
        received.update(zip(names, exchange.received))

    dfull["ple_w_proj"] = _mm_tn(ps, dz, name="d_ple_w_proj", tn=1024, tm=1024, transpose_out=True)
    dfull["ple_w_gate"] = _mm_tn(h4, dpre, name="d_ple_w_gate", tn=1024, tm=1024)
    dx3, dx3s, d_ple = _mm_nt_rms_bwd([(dpre, full["ple_w_gate"], False)], x3, dx4, g_ple, name="ple_rms_bwd", tn=512,
                                      half_scale=0.5)

    names = ("ple_w_proj", "ple_w_gate")
    ex = _scatter_of(names, dfull)
    da2, db2 = _ffn_bwd_mid(dx3s, full["ffn2_w2"], a2, b2, name="ffn2_bwd_mid", exchange=ex)
    sent(names, ex)
    dfull["ffn2_w2"] = _mm_tn(s2, dx3s, name="ffn2_dw2", tn=1024, tm=1024)
    dfull["ffn2_w1"] = _mm_tn(h3, da2, name="ffn2_dw1", tn=1024, tm=2816, transpose_out=True)
    dfull["ffn2_w3"] = _mm_tn(h3, db2, name="ffn2_dw3", tn=1024, tm=2816, transpose_out=True)
    names = ("ffn2_w2",)
    ex = _scatter_of(names, dfull)
    dx2, dx2b, d_f2 = _mm_nt_rms_bwd([(da2, full["ffn2_w1"], True), (db2, full["ffn2_w3"], True)], x2, dx3, g_f2,
                                     name="ffn2_rms_bwd", tn=512, half_scale=1.0, exchange=ex)
    sent(names, ex)

    dfull["w_out"] = _mm_tn(y, dx2b, name="d_w_out", tn=1024, tm=1024)
    dy = _mm_nn_wide(dx2b, full["w_out"], name="d_y", out_dtype=BF16)
    dya, dyb, dga, dgb = _merge_bwd(dy, proj, ya, yb, name="merge_bwd")

    dfull["w_branch_b"] = _mm_tn(mixed, dyb, name="d_w_branch_b", tn=1024, tm=1024, transpose_out=True)
    dmixed = _mm_nn_wide(dyb, full["w_branch_b"], name="d_mixed", out_dtype=F32, b_is_km=True)
    du, d_pw, d_ps = _pool_bwd(dmixed, pooled, pw, pscale, name="pool_bwd")

    dfull["w_branch_a"] = _mm_tn(on, dya, name="d_w_branch_a", tn=1024, tm=1024)
    don = _mm_nn_wide(dya, full["w_branch_a"], name="d_on", out_dtype=BF16)
    dog, do, d_on = _hgrn_post_bwd(don, o, proj, g_on, name="hgrn_post_bwd")
    names = ("ffn2_w1", "ffn2_w3", "w_out", "w_branch_b", "w_branch_a")
    ex = _scatter_of(names, dfull)
    dqfi, d_lb = _hgrn_bwd(proj, lb2, do, states, name="hgrn_bwd", exchange=ex)
    sent(names, ex)
    dproj = [dqfi, dog, du, dga, dgb]
    dfull["w_in"] = jnp.concatenate(
        [_mm_tn(h2, part, name=f"d_w_in_{j}", tn=1024, tm=3072, transpose_out=True) for j, part in enumerate(dproj)],
        axis=0)
    names = ("w_in",)
    ex = _scatter_of(names, dfull)
    dx1, dx1s, d_mix = _mm_nt_rms_bwd([(dproj, full["w_in"], True)], x1, dx2, g_mix, name="mix_rms_bwd", tn=512,
                                      half_scale=0.5, exchange=ex)
    sent(names, ex)

    da1, db1 = _ffn_bwd_mid(dx1s, full["ffn1_w2"], a1, b1, name="ffn1_bwd_mid")
    dfull["ffn1_w2"] = _mm_tn(s1, dx1s, name="ffn1_dw2", tn=1024, tm=1024)
    names = ("ffn1_w2",)
    ex = _scatter_of(names, dfull)
    dfull["ffn1_w1"] = _mm_tn(h1, da1, name="ffn1_dw1", tn=1024, tm=2816, transpose_out=True, exchange=ex)
    sent(names, ex)
    names = ("ffn1_w1",)
    ex = _scatter_of(names, dfull)
    dfull["ffn1_w3"] = _mm_tn(h1, db1, name="ffn1_dw3", tn=1024, tm=2816, transpose_out=True, exchange=ex)
    sent(names, ex)
    names = ("ffn1_w3",)
    ex = _scatter_of(names, dfull)
    grad_x, _, d_f1 = _mm_nt_rms_bwd([(da1, full["ffn1_w1"], True), (db1, full["ffn1_w3"], True)], xs, dx1, g_f1,
                                     name="ffn1_rms_bwd", tn=512, half_scale=1.0, exchange=ex)
    sent(names, ex)

    small_part = dict(ffn1_norm=d_f1, mix_norm=d_mix, hgrn_onorm=d_on, ffn2_norm=d_f2, ple_norm=d_ple,
                      ple_post_norm=d_post, final_norm=d_fin, hgrn_lb=d_lb, pool_scale=d_ps, pool_w=_as_2d(d_pw))
    gathered_small = _exchange_now([small_part[k] for k in SMALL_PARAMS] + [loss_part], name="gather_small_grads",
                                   gather=True)
    small_all = dict(zip(SMALL_PARAMS, gathered_small))
    loss = jnp.sum(gathered_small[-1])

    grads, deltas, new_m, new_v = {}, {}, {}, {}
    for name, axis in BIG_WEIGHTS:
        shape, recv = weights[name].shape, received[name]
        own = [t[name][0].T if axis == 1 else t[name][0] for t in (weights, mom1, mom2)]
        res = _adam_big(recv, *own, name=f"adam_{name}")
        grads[name], deltas[name], new_m[name], new_v[name] = [(r.T if axis == 1 else r).reshape(shape) for r in res]
    res = _adam_small(small_all, *[{k: _as_2d(t[k]) for k in SMALL_PARAMS} for t in (weights, mom1, mom2)],
                      name="adam_small")
    for store, vals in zip((grads, deltas, new_m, new_v), res):
        store.update({k: val.reshape(weights[k].shape) for k, val in zip(SMALL_PARAMS, vals)})

    return (loss, grad_x.reshape(x.shape), *[grads[k] for k in WEIGHT_ORDER], *[deltas[k] for k in WEIGHT_ORDER],
            *[new_m[k] for k in WEIGHT_ORDER], *[new_v[k] for k in WEIGHT_ORDER])
```

```python
import jax
import jax.numpy as jnp
from jax import lax
from jax.experimental import pallas as pl
from jax.experimental.pallas import tpu as pltpu

F32 = jnp.float32
BF16 = jnp.bfloat16

N_DEV = 8
D_MODEL = 1024
HEADS = 8
HEAD_DIM = 128
POOL_WINDOWS = (2, 4, 8, 16)
POOL_CH = 128
POOL_WIDTH = 512
POOL_HALO = 16
RMS_EPS = 1e-6
CHUNK = 64
SUB = 32
HGRN_HEADS_PER_STEP = 8
NEG_BIG = -1e30

ADAM_LR = 0.001
ADAM_B1 = 0.9
ADAM_B2 = 0.999
ADAM_EPS = 1e-08
ADAM_WD = 0.01
ADAM_STEP = 10

V7X_VMEM_BYTES = 64 * 1024 * 1024
VMEM_LIMIT = (V7X_VMEM_BYTES * 3) // 4
EXCHANGE_TAIL_STEPS = 3
ROW_TILE_CAP = 8192

COL_Q, COL_F, COL_I, COL_OG, COL_POOL, COL_GA, COL_GB = 0, 1024, 2048, 3072, 4096, 4608, 5632

BIG_WEIGHTS = (
    ("ffn1_w1", 1), ("ffn1_w3", 1), ("ffn1_w2", 0), ("w_in", 1), ("w_branch_a", 0), ("w_branch_b", 1),
    ("w_out", 0), ("ffn2_w1", 1), ("ffn2_w3", 1), ("ffn2_w2", 0), ("ple_w_gate", 0), ("ple_w_proj", 1),
)
SMALL_PARAMS = ("ffn1_norm", "mix_norm", "hgrn_onorm", "ffn2_norm", "ple_norm", "ple_post_norm", "final_norm",
                "hgrn_lb", "pool_scale", "pool_w")
WEIGHT_ORDER = (
    "ffn1_norm", "ffn1_w1", "ffn1_w3", "ffn1_w2", "mix_norm", "w_in", "hgrn_lb", "hgrn_onorm", "w_branch_a", "pool_w",
    "pool_scale", "w_branch_b", "w_out", "ffn2_norm", "ffn2_w1", "ffn2_w3", "ffn2_w2", "ple_norm", "ple_w_gate",
    "ple_w_proj", "ple_post_norm", "final_norm",
)


def _params(*sem):
    return pltpu.CompilerParams(dimension_semantics=sem if sem else None, vmem_limit_bytes=VMEM_LIMIT)


COL_CHUNK = 256


def _rows(tn, width):
    return pl.BlockSpec((tn, width), lambda i: (i, 0))


def _resident(shape):
    return pl.BlockSpec(shape, lambda i: (0,) * len(shape), pipeline_mode=pl.Buffered(1))


def _dot(a, b):
    return jnp.dot(a, b, preferred_element_type=F32)


def _dot_nt(a, b):
    return lax.dot_general(a, b, (((1,), (1,)), ((), ())), preferred_element_type=F32)


def _dot_tn(a, b):
    return lax.dot_general(a, b, (((0,), (0,)), ((), ())), preferred_element_type=F32)


def _sigmoid(x):
    return 0.5 * jnp.tanh(0.5 * x) + 0.5


def _tile(n, want, mult):
    if mult != 128:
        want = min(want, ROW_TILE_CAP)
    if n <= want:
        return n
    t = (want // mult) * mult
    while t > mult and n % t:
        t -= mult
    assert n % t == 0, (n, want, mult)
    return t


class _Exchange:
    COPIES = N_DEV - 1

    def __init__(self, arrs, gather):
        self.arrs, self.gather, self.n = list(arrs), gather, len(arrs)
        self.out_shape = [jax.ShapeDtypeStruct((N_DEV,) + (a.shape if gather else a.shape[1:]), a.dtype) for a in arrs]
        self.scratch = [pltpu.SemaphoreType.DMA((self.n * self.COPIES,)),
                        pltpu.SemaphoreType.DMA((self.n * self.COPIES,)), pltpu.SemaphoreType.DMA((self.n,))]
        self.received = None

    @staticmethod
    def _place():
        x, y, c = lax.axis_index("x"), lax.axis_index("y"), lax.axis_index("c")
        return x, y, c

    def _copy(self, a, k, src, dst, to, sems):
        s = a * self.COPIES + k
        return pltpu.make_async_remote_copy(src_ref=src, dst_ref=dst, send_sem=sems[0].at[s], recv_sem=sems[1].at[s],
                                            device_id=to, device_id_type=pl.DeviceIdType.MESH)

    def _gather_copies(self, role, ins, outs, sems):
        x, y, c = self._place()
        chips = [(1 - x, y), (x, 1 - y), (1 - x, 1 - y)]
        sibling = (x, y, 1 - c)

        def slot(px, py, pc):
            return 4 * px + 2 * py + pc

        copies = []
        for a in range(self.n):
            mine = outs[a].at[slot(x, y, c)]
            if role == "first":
                copies.append(self._copy(a, 0, ins[a], mine, sibling, sems))
            elif role == "last":
                copies.append(self._copy(a, 0, ins[a], outs[a].at[slot(x, y, 1 - c)], sibling, sems))
            for j, (px, py) in enumerate(chips):
                theirs = outs[a].at[slot(px, py, c)]
                if role == "first":
                    copies.append(self._copy(a, 1 + j, ins[a], mine, (px, py, c), sems))
                elif role == "landed":
                    copies.append(self._copy(a, 1 + j, ins[a], theirs, (px, py, c), sems))
                elif role == "onward":
                    copies.append(self._copy(a, 4 + j, theirs, theirs, sibling, sems))
                else:
                    copies.append(self._copy(a, 4 + j, ins[a], outs[a].at[slot(px, py, 1 - c)], sibling, sems))
        return copies

    def _scatter_copies(self, role, ins, outs, sems):
        x, y, c = self._place()
        me = 4 * x + 2 * y + c
        copies = []
        for k in range(1, N_DEV):
            px = 1 - x if k & 4 else x
            py = 1 - y if k & 2 else y
            pc = 1 - c if k & 1 else c
            peer = 4 * px + 2 * py + pc
            for a in range(self.n):
                dst = outs[a].at[me] if role == "sends" else outs[a].at[peer]
                copies.append(self._copy(a, k - 1, ins[a].at[peer], dst, (px, py, pc), sems))
        return copies

    def _local(self, ins, outs, sems):
        x, y, c = self._place()
        me = 4 * x + 2 * y + c
        return [pltpu.make_async_copy(ins[a] if self.gather else ins[a].at[me], outs[a].at[me], sems[2].at[a])
                for a in range(self.n)]

    def start(self, ins, outs, sems):
        for cp in self._local(ins, outs, sems):
            cp.start()
        sends = (self._gather_copies("first", ins, outs, sems) if self.gather
                 else self._scatter_copies("sends", ins, outs, sems))
        for cp in sends:
            cp.start()

    def pass_on(self, ins, outs, sems):
        if self.gather:
            for landed, onward in zip(self._gather_copies("landed", ins, outs, sems),
                                      self._gather_copies("onward", ins, outs, sems)):
                landed.wait_recv()
                onward.start()

    def finish(self, ins, outs, sems):
        if self.gather:
            arrivals = self._gather_copies("last", ins, outs, sems)
            sends = self._gather_copies("first", ins, outs, sems) + self._gather_copies("onward", ins, outs, sems)
        else:
            arrivals = self._scatter_copies("arrivals", ins, outs, sems)
            sends = self._scatter_copies("sends", ins, outs, sems)
        for cp in arrivals:
            cp.wait_recv()
        for cp in sends:
            cp.wait_send()
        for cp in self._local(ins, outs, sems):
            cp.wait()


def _call(body, *, name, grid, in_specs, out_specs, out_shape, args, semantics, scratch=(), exchange=None):
    if exchange is None:
        return pl.pallas_call(
            body, name=name, grid=grid, in_specs=in_specs, out_specs=out_specs, out_shape=out_shape,
            scratch_shapes=list(scratch), compiler_params=_params(*semantics))(*args)
    ex = exchange
    n_in, n_out, n_s = len(in_specs), len(out_specs), len(scratch)

    def wrapped(*refs):
        ins, ex_in = refs[:n_in], refs[n_in:n_in + ex.n]
        o0 = n_in + ex.n
        outs, ex_out = refs[o0:o0 + n_out], refs[o0 + n_out:o0 + n_out + ex.n]
        s0 = o0 + n_out + ex.n
        scr, sems = refs[s0:s0 + n_s], refs[s0 + n_s:]
        step, steps = pl.program_id(0), grid[0]
        for ax in range(1, len(grid)):
            step, steps = step * grid[ax] + pl.program_id(ax), steps * grid[ax]

        @pl.when(step == 0)
        def _():
            ex.start(ex_in, ex_out, sems)

        body(*ins, *outs, *scr)

        @pl.when(step == max(steps - EXCHANGE_TAIL_STEPS, 0))
        def _():
            ex.pass_on(ex_in, ex_out, sems)

        @pl.when(step == steps - 1)
        def _():
            ex.finish(ex_in, ex_out, sems)

    hbm = pl.BlockSpec(memory_space=pltpu.HBM)
    res = pl.pallas_call(
        wrapped, name=name, grid=grid, in_specs=list(in_specs) + [hbm] * ex.n,
        out_specs=list(out_specs) + [hbm] * ex.n, out_shape=list(out_shape) + ex.out_shape,
        scratch_shapes=list(scratch) + ex.scratch, compiler_params=_params(*(["arbitrary"] * len(grid))),
    )(*args, *ex.arrs)
    ex.received = res[n_out:]
    return res[:n_out]


def _exchange_now(arrs, *, name, gather):
    ex = _Exchange(arrs, gather)
    n = ex.n

    def body(*refs):
        ex.start(refs[:n], refs[n:2 * n], refs[2 * n:])
        ex.pass_on(refs[:n], refs[n:2 * n], refs[2 * n:])
        ex.finish(refs[:n], refs[n:2 * n], refs[2 * n:])

    hbm = pl.BlockSpec(memory_space=pltpu.HBM)
    return pl.pallas_call(body, name=name, out_shape=ex.out_shape, in_specs=[hbm] * n, out_specs=[hbm] * n,
                          scratch_shapes=ex.scratch)(*arrs)


def _mm_tn(a, b, *, name, tn, tm, transpose_out=False, exchange=None):
    n, k = a.shape
    m = b.shape[1]
    tn, tm = _tile(n, tn, 16), _tile(m, tm, 128)
    steps = n // tn

    def body(a_ref, b_ref, o_ref, acc):
        i = pl.program_id(1)

        @pl.when(i == 0)
        def _():
            acc[...] = jnp.zeros_like(acc)

        acc[...] += _dot_tn(a_ref[...], b_ref[...])

        @pl.when(i == steps - 1)
        def _():
            for c0 in range(0, tm, COL_CHUNK):
                cols = slice(c0, min(c0 + COL_CHUNK, tm))
                if transpose_out:
                    o_ref[cols, :] = acc[:, cols].T.astype(o_ref.dtype)
                else:
                    o_ref[:, cols] = acc[:, cols].astype(o_ref.dtype)

    if transpose_out:
        out_spec, out_shape = pl.BlockSpec((tm, k), lambda j, i: (j, 0)), jax.ShapeDtypeStruct((m, k), BF16)
    else:
        out_spec, out_shape = pl.BlockSpec((k, tm), lambda j, i: (0, j)), jax.ShapeDtypeStruct((k, m), BF16)
    return _call(body, name=name, grid=(m // tm, steps),
                 in_specs=[pl.BlockSpec((tn, k), lambda j, i: (i, 0)), pl.BlockSpec((tn, tm), lambda j, i: (i, j))],
                 out_specs=[out_spec], out_shape=[out_shape], args=[a, b], semantics=("parallel", "arbitrary"),
                 scratch=[pltpu.VMEM((k, tm), F32)], exchange=exchange)[0]


def _ffn_up(h, w1, w3, *, name, tn=512, exchange=None):
    n, k = h.shape
    m = w1.shape[0]
    tn = _tile(n, tn, 16)

    def body(h_ref, w1_ref, w3_ref, dsda_ref, dsdb_ref, s_ref):
        for c0 in range(0, m, COL_CHUNK):
            cols = slice(c0, c0 + COL_CHUNK)
            a = _dot_nt(h_ref[...], w1_ref[cols, :])
            b = _dot_nt(h_ref[...], w3_ref[cols, :])
            sg = _sigmoid(a)
            silu = a * sg
            dsda_ref[:, cols] = (b * (sg + silu * (1.0 - sg))).astype(dsda_ref.dtype)
            dsdb_ref[:, cols] = silu.astype(dsdb_ref.dtype)
            s_ref[:, cols] = (silu * b).astype(s_ref.dtype)

    ospec = _rows(tn, m)
    return _call(body, name=name, grid=(n // tn,),
                 in_specs=[_rows(tn, k), _resident(w1.shape), _resident(w3.shape)], out_specs=[ospec, ospec, ospec],
                 out_shape=[jax.ShapeDtypeStruct((n, m), BF16)] * 3,
                 args=[h, w1, w3], semantics=("parallel",), exchange=exchange)


def _mm_nn_wide(a, b, *, name, out_dtype, b_is_km=False, tn=1024, exchange=None):
    n, k = a.shape
    m = b.shape[1 if b_is_km else 0]
    tn = _tile(n, tn, 16)
    chunk = min(2 * COL_CHUNK, m)

    def body(a_ref, b_ref, o_ref):
        for c0 in range(0, m, chunk):
            cols = slice(c0, c0 + chunk)
            if b_is_km:
                res = _dot(a_ref[...], b_ref[:, cols])
            else:
                res = _dot_nt(a_ref[...], b_ref[cols, :])
            o_ref[:, cols] = res.astype(o_ref.dtype)

    return _call(body, name=name, grid=(n // tn,), in_specs=[_rows(tn, k), _resident(b.shape)],
                 out_specs=[_rows(tn, m)], out_shape=[jax.ShapeDtypeStruct((n, m), out_dtype)], args=[a, b],
                 semantics=("parallel",), exchange=exchange)[0]


def _mm_nn_res_rms(a, b, res, g, *, name, scale, exchange=None):
    n, k = a.shape
    d = b.shape[1]
    tn = _tile(n, 512, 16)

    def body(a_ref, b_ref, r_ref, g_ref, x_ref, h_ref):
        for c0 in range(0, d, COL_CHUNK):
            cols = slice(c0, c0 + COL_CHUNK)
            x_ref[:, cols] = r_ref[:, cols] + scale * _dot(a_ref[...], b_ref[:, cols])
        x = x_ref[...]
        r = lax.rsqrt(_rowmean(x * x) + RMS_EPS)
        h_ref[...] = (x * r * g_ref[...]).astype(h_ref.dtype)

    row = _rows(tn, d)
    return _call(body, name=name, grid=(n // tn,),
                 in_specs=[_rows(tn, k), _resident(b.shape), row, pl.BlockSpec((1, d), lambda i: (0, 0))],
                 out_specs=[row, row],
                 out_shape=[jax.ShapeDtypeStruct((n, d), F32), jax.ShapeDtypeStruct((n, d), BF16)],
                 args=[a, b, res, g], semantics=("parallel",), exchange=exchange)


def _ffn_bwd_mid(dxs, w2, dsda, dsdb, *, name, exchange=None):
    n, d = dxs.shape
    m = w2.shape[0]
    tn = _tile(n, 512, 16)

    def body(dx_ref, w2_ref, dsda_ref, dsdb_ref, da_ref, db_ref):
        for c0 in range(0, m, COL_CHUNK):
            cols = slice(c0, c0 + COL_CHUNK)
            ds = _dot_nt(dx_ref[...], w2_ref[cols, :])
            da_ref[:, cols] = (ds * dsda_ref[:, cols].astype(F32)).astype(da_ref.dtype)
            db_ref[:, cols] = (ds * dsdb_ref[:, cols].astype(F32)).astype(db_ref.dtype)

    tile = _rows(tn, m)
    return _call(body, name=name, grid=(n // tn,),
                 in_specs=[_rows(tn, d), _resident(w2.shape), tile, tile], out_specs=[tile, tile],
                 out_shape=[jax.ShapeDtypeStruct((n, m), BF16), jax.ShapeDtypeStruct((n, m), BF16)],
                 args=[dxs, w2, dsda, dsdb], semantics=("parallel",), exchange=exchange)


def _rowwise(fn, *, name, n, tn, ncol, rows, vecs, outs, accs=(), exchange=None):
    tn = _tile(n, tn, 16)
    nr, nv, no = len(rows), len(vecs), len(outs)

    def body(*refs):
        first = pl.program_id(1) == 0
        vals = [r[...].astype(F32) for r in refs[:nr + nv]]
        res = fn(*vals)
        for ref, val in zip(refs[nr + nv:nr + nv + no], res[:no]):
            ref[...] = val.astype(ref.dtype)
        for ref, val in zip(refs[nr + nv + no:], res[no:]):
            _accumulate(ref, val, first)

    in_specs = [pl.BlockSpec((tn, w), lambda j, i, c0=c0: (i, c0 + j)) for _, w, c0 in rows]
    in_specs += [pl.BlockSpec((1, w), lambda j, i, c0=c0: (0, c0 + j)) for _, w, c0 in vecs]
    out_specs = [pl.BlockSpec((tn, w), lambda j, i: (i, j)) for _, w, _ in outs]
    out_specs += [pl.BlockSpec((1, w), lambda j, i: (0, j)) for _, w in accs]
    out_shape = [jax.ShapeDtypeStruct((n, tw), dt) for tw, _, dt in outs]
    out_shape += [jax.ShapeDtypeStruct((1, tw), F32) for tw, _ in accs]
    return _call(body, name=name, grid=(ncol, n // tn), in_specs=in_specs, out_specs=out_specs, out_shape=out_shape,
                 args=[r[0] for r in rows] + [v[0] for v in vecs], semantics=("parallel", "arbitrary"),
                 exchange=exchange)


def _accumulate(ref, val, first):
    @pl.when(first)
    def _():
        ref[...] = jnp.zeros_like(ref)

    ref[...] += val


def _colsum(x):
    return jnp.sum(x, axis=0, keepdims=True)


def _rowmean(x):
    return jnp.mean(x, axis=-1, keepdims=True)


def _rms_fwd(x, g, *, name, exchange=None):
    def fn(x_, g_):
        r = lax.rsqrt(_rowmean(x_ * x_) + RMS_EPS)
        return (x_ * r * g_,)

    n, d = x.shape
    return _rowwise(fn, name=name, n=n, tn=512, ncol=1, rows=[(x, d, 0)], vecs=[(g, d, 0)], outs=[(d, d, BF16)],
                    exchange=exchange)[0]


def _mm_nt_rms_bwd(pairs, x, extra, g, *, name, tn, half_scale, exchange=None):
    n, d = x.shape
    tn = _tile(n, tn, 16)
    pairs = [(list(a) if isinstance(a, (list, tuple)) else [a], b, t) for a, b, t in pairs]
    nref = sum(len(a) + 1 for a, _, _ in pairs)

    def body(*refs):
        x_ref, e_ref, g_ref, dx_ref, dxs_ref, dg_ref = refs[nref:]
        dh, at = None, 0
        for parts, _, transposed in pairs:
            b_ref = refs[at + len(parts)]
            col = 0
            for j, part in enumerate(parts):
                w = part.shape[1]
                if transposed:
                    term = _dot(refs[at + j][...], b_ref[col:col + w, :])
                else:
                    term = _dot_nt(refs[at + j][...], b_ref[:, col:col + w])
                dh = term if dh is None else dh + term
                col += w
            at += len(parts) + 1
        x_ = x_ref[...]
        r = lax.rsqrt(_rowmean(x_ * x_) + RMS_EPS)
        xh = x_ * r
        dxh = dh * g_ref[...]
        dx = e_ref[...] + r * (dxh - xh * _rowmean(dxh * xh))
        dx_ref[...] = dx
        dxs_ref[...] = (dx * half_scale).astype(dxs_ref.dtype)
        _accumulate(dg_ref, _colsum(dh * xh), pl.program_id(0) == 0)

    in_specs, args = [], []
    for parts, b, transposed in pairs:
        assert sum(part.shape[1] for part in parts) == b.shape[0 if transposed else 1]
        in_specs += [pl.BlockSpec((tn, part.shape[1]), lambda i: (i, 0)) for part in parts]
        in_specs.append(pl.BlockSpec(b.shape, lambda i: (0, 0), pipeline_mode=pl.Buffered(1)))
        args += parts + [b]
    row = pl.BlockSpec((tn, d), lambda i: (i, 0))
    vec = pl.BlockSpec((1, d), lambda i: (0, 0))
    return _call(body, name=name, grid=(n // tn,), in_specs=in_specs + [row, row, vec], out_specs=[row, row, vec],
                 out_shape=[jax.ShapeDtypeStruct((n, d), F32), jax.ShapeDtypeStruct((n, d), BF16),
                            jax.ShapeDtypeStruct((1, d), F32)],
                 args=args + [x, extra, g], semantics=("arbitrary",), exchange=exchange)


def _ple_final(x3, gpre, z, tgt, gpp, gf, *, name):
    def fn(x3_, gpre_, z_, tgt_, gpp_, gf_):
        gate = _sigmoid(gpre_)
        rz = lax.rsqrt(_rowmean(z_ * z_) + RMS_EPS)
        zh = z_ * rz
        e = zh * gpp_
        x4 = x3_ + gate * e
        r4 = lax.rsqrt(_rowmean(x4 * x4) + RMS_EPS)
        x4h = x4 * r4
        diff = x4h * gf_ - tgt_
        dout = diff * (1.0 / D_MODEL)
        dxh4 = dout * gf_
        dx4 = r4 * (dxh4 - x4h * _rowmean(dxh4 * x4h))
        dpre = dx4 * e * gate * (1.0 - gate)
        de = dx4 * gate
        dzh = de * gpp_
        dz = rz * (dzh - zh * _rowmean(dzh * zh))
        return dx4, dpre, dz, _colsum(diff * diff) * (0.5 / D_MODEL), _colsum(dout * x4h), _colsum(de * zh)

    n, d = x3.shape
    return _rowwise(fn, name=name, n=n, tn=512, ncol=1, rows=[(x3, d, 0), (gpre, d, 0), (z, d, 0), (tgt, d, 0)],
                    vecs=[(gpp, d, 0), (gf, d, 0)], outs=[(d, d, F32), (d, d, BF16), (d, d, BF16)],
                    accs=[(d, d), (d, d), (d, d)])


def _merge_fwd(proj, ya, yb, *, name):
    def fn(ga, gb, ya_, yb_):
        return (_sigmoid(ga) * ya_ + _sigmoid(gb) * yb_,)

    n = proj.shape[0]
    w = 512
    return _rowwise(fn, name=name, n=n, tn=2048, ncol=D_MODEL // w,
                    rows=[(proj, w, COL_GA // w), (proj, w, COL_GB // w), (ya, w, 0), (yb, w, 0)], vecs=[],
                    outs=[(D_MODEL, w, BF16)])[0]


def _merge_bwd(dy, proj, ya, yb, *, name):
    def fn(dy_, ga, gb, ya_, yb_):
        sa, sb = _sigmoid(ga), _sigmoid(gb)
        return dy_ * sa, dy_ * sb, dy_ * ya_ * sa * (1.0 - sa), dy_ * yb_ * sb * (1.0 - sb)

    n = proj.shape[0]
    w = 512
    return _rowwise(fn, name=name, n=n, tn=2048, ncol=D_MODEL // w,
                    rows=[(dy, w, 0), (proj, w, COL_GA // w), (proj, w, COL_GB // w), (ya, w, 0), (yb, w, 0)],
                    vecs=[], outs=[(D_MODEL, w, BF16)] * 4)


def _head_mean(x):
    return jnp.concatenate(
        [jnp.broadcast_to(jnp.mean(x[:, h * HEAD_DIM:(h + 1) * HEAD_DIM], axis=-1, keepdims=True),
                          (x.shape[0], HEAD_DIM)) for h in range(HEADS)], axis=1)


def _hgrn_post_fwd(o, proj, onorm, *, name):
    def fn(o_, og, gam):
        r = lax.rsqrt(_head_mean(o_ * o_) + RMS_EPS)
        return (o_ * r * gam * (og * _sigmoid(og)),)

    n = o.shape[0]
    w = D_MODEL
    return _rowwise(fn, name=name, n=n, tn=512, ncol=1, rows=[(o, w, 0), (proj, w, COL_OG // w)],
                    vecs=[(onorm, w, 0)], outs=[(D_MODEL, w, BF16)])[0]


def _hgrn_post_bwd(don, o, proj, onorm, *, name):
    def fn(don_, o_, og, gam):
        r = lax.rsqrt(_head_mean(o_ * o_) + RMS_EPS)
        oh = o_ * r
        sg = _sigmoid(og)
        dog = don_ * oh * gam * (sg * (1.0 + og * (1.0 - sg)))
        dn = don_ * (og * sg)
        doh = dn * gam
        do = r * (doh - oh * _head_mean(doh * oh))
        return dog, do, _colsum(dn * oh)

    n = o.shape[0]
    w = D_MODEL
    return _rowwise(fn, name=name, n=n, tn=512, ncol=1, rows=[(don, w, 0), (o, w, 0), (proj, w, COL_OG // w)],
                    vecs=[(onorm, w, 0)], outs=[(D_MODEL, w, BF16), (D_MODEL, w, BF16)], accs=[(D_MODEL, w)])


def _tri_sum(tri, x):
    hi = x.astype(BF16)
    lo = (x - hi.astype(F32)).astype(BF16)
    return _dot(tri, hi) + _dot(tri, lo)


def _lower_bound(lb_ref):
    return 1.0 / (1.0 + jnp.exp(lb_ref[1:2, :] - lb_ref[0:1, :]))


def _hgrn_specs(n, t, reverse):
    nt = n // t
    width = HGRN_HEADS_PER_STEP * HEAD_DIM

    def tok(i):
        return nt - 1 - i if reverse else i

    def sec(col):
        c0 = col // width
        return pl.BlockSpec((t, width), lambda h, i: (tok(i), c0 + h))

    head_tile = pl.BlockSpec((t, width), lambda h, i: (tok(i), h))
    state = pl.BlockSpec((HGRN_HEADS_PER_STEP, t // CHUNK, HEAD_DIM, HEAD_DIM), lambda h, i: (h, tok(i), 0, 0))
    lb = pl.BlockSpec((2, width), lambda h, i: (0, h))
    return sec, head_tile, state, lb


def _hgrn_fwd(proj, hgrn_lb, *, name):
    n = proj.shape[0]
    t = _tile(n, 512, CHUNK)
    nc = t // CHUNK
    hps = HGRN_HEADS_PER_STEP
    width = hps * HEAD_DIM
    lanes = [slice(h * HEAD_DIM, (h + 1) * HEAD_DIM) for h in range(hps)]
    sec, head_tile, state, lbspec = _hgrn_specs(n, t, False)

    def body(q_ref, f_ref, i_ref, lb_ref, o_ref, st_ref, s_acc, g_s, a_s):
        @pl.when(pl.program_id(1) == 0)
        def _():
            s_acc[...] = jnp.zeros_like(s_acc)

        lb = _lower_bound(lb_ref)
        row = lax.broadcasted_iota(jnp.int32, (CHUNK, CHUNK), 0)
        col = lax.broadcasted_iota(jnp.int32, (CHUNK, CHUNK), 1)
        tril = row >= col
        trilb = jnp.where(tril, 1.0, 0.0).astype(BF16)
        rowk = lax.broadcasted_iota(jnp.int32, (CHUNK, width), 0)

        def chunk(c, carry):
            rows = pl.ds(pl.multiple_of(c * CHUNK, CHUNK), CHUNK)
            qr, fr, v = [r[rows, :].astype(F32) for r in (q_ref, f_ref, i_ref)]
            q = qr * _sigmoid(qr)
            f = lb + (1.0 - lb) * _sigmoid(fr)
            k = 1.0 - f
            g = _tri_sum(trilb, jnp.log(f))
            g_s[...] = g
            st0 = [s_acc[h] for h in range(hps)]
            for h in range(hps):
                st_ref[h, c] = st0[h]
            vb = v.astype(BF16)
            for blk in range(CHUNK // SUB):
                lo, hi = blk * SUB, (blk + 1) * SUB
                gref = g_s[lo - 1:lo, :] if blk else jnp.zeros((1, width), F32)
                qi = (q[lo:hi] * jnp.exp(g[lo:hi] - gref)).astype(BF16)
                ki = (k * jnp.exp(jnp.where(rowk < hi, gref - g, NEG_BIG))).astype(BF16)
                for h, ln in enumerate(lanes):
                    a_s[h, lo:hi, :] = _dot_nt(qi[:, ln], ki[:, ln])
            qeb = (q * jnp.exp(g)).astype(BF16)
            o_ref[rows, :] = jnp.concatenate(
                [_dot(jnp.where(tril, a_s[h], 0.0).astype(BF16), vb[:, ln]) + _dot_nt(qeb[:, ln], st0[h].astype(BF16))
                 for h, ln in enumerate(lanes)], axis=1).astype(o_ref.dtype)
            glast = g_s[CHUNK - 1:CHUNK, :]
            kdb = (k * jnp.exp(glast - g)).astype(BF16)
            dec = jnp.exp(glast)
            for h, ln in enumerate(lanes):
                s_acc[h] = st0[h] * dec[:, ln] + _dot_tn(vb[:, ln], kdb[:, ln])
            return carry

        lax.fori_loop(0, nc, chunk, 0)

    return pl.pallas_call(
        body, name=name, grid=(HEADS // hps, n // t),
        in_specs=[sec(COL_Q), sec(COL_F), sec(COL_I), lbspec], out_specs=[head_tile, state],
        out_shape=[jax.ShapeDtypeStruct((n, D_MODEL), BF16),
                   jax.ShapeDtypeStruct((HEADS, n // CHUNK, HEAD_DIM, HEAD_DIM), F32)],
        scratch_shapes=[pltpu.VMEM((hps, HEAD_DIM, HEAD_DIM), F32), pltpu.VMEM((CHUNK, width), F32),
                        pltpu.VMEM((hps, CHUNK, CHUNK), F32)],
        compiler_params=_params("parallel", "arbitrary"),
    )(proj, proj, proj, hgrn_lb)


def _hgrn_bwd(proj, hgrn_lb, do, states, *, name, exchange=None):
    n = proj.shape[0]
    t = _tile(n, 512, CHUNK)
    nc = t // CHUNK
    hps = HGRN_HEADS_PER_STEP
    width = hps * HEAD_DIM
    lanes = [slice(h * HEAD_DIM, (h + 1) * HEAD_DIM) for h in range(hps)]
    sec, head_tile, state, lbspec = _hgrn_specs(n, t, True)

    def body(q_ref, f_ref, i_ref, lb_ref, do_ref, st_ref, dqfi_ref, dlb_ref, d_acc, g_s, a_s, dq_s,
             dg_s):
        first = pl.program_id(1) == 0

        @pl.when(first)
        def _():
            d_acc[...] = jnp.zeros_like(d_acc)

        lb = _lower_bound(lb_ref)
        row = lax.broadcasted_iota(jnp.int32, (CHUNK, CHUNK), 0)
        col = lax.broadcasted_iota(jnp.int32, (CHUNK, CHUNK), 1)
        tril = row >= col
        trilb = jnp.where(tril, 1.0, 0.0).astype(BF16)
        triub = jnp.where(row <= col, 1.0, 0.0).astype(BF16)
        rowk = lax.broadcasted_iota(jnp.int32, (CHUNK, width), 0)

        def per_head(fn):
            return jnp.concatenate([fn(h, ln) for h, ln in enumerate(lanes)], axis=1)

        def chunk(j, dlb):
            c = nc - 1 - j
            rows = pl.ds(pl.multiple_of(c * CHUNK, CHUNK), CHUNK)
            qr, fr, v, dout = [r[rows, :].astype(F32) for r in (q_ref, f_ref, i_ref, do_ref)]
            sq = _sigmoid(qr)
            q = qr * sq
            sf = _sigmoid(fr)
            f = lb + (1.0 - lb) * sf
            k = 1.0 - f
            g = _tri_sum(trilb, jnp.log(f))
            g_s[...] = g
            st0 = [st_ref[h, c] for h in range(hps)]
            dt = [d_acc[h] for h in range(hps)]
            vb, dob = v.astype(BF16), dout.astype(BF16)
            dtb = [x.astype(BF16) for x in dt]
            st0b = [x.astype(BF16) for x in st0]
            glast = g_s[CHUNK - 1:CHUNK, :]
            eg = jnp.exp(g)
            kdec = jnp.exp(glast - g)
            qeb, kdb = (q * eg).astype(BF16), (k * kdec).astype(BF16)
            aps = [jnp.where(row > col, _dot_nt(dob[:, ln], vb[:, ln]), 0.0) for ln in lanes]
            dov = dout * v
            adiag = per_head(lambda h, ln: jnp.broadcast_to(
                jnp.sum(dov[:, ln], axis=-1, keepdims=True), (CHUNK, HEAD_DIM)))
            dq_inter = per_head(lambda h, ln: _dot(dob[:, ln], st0b[h]))
            dk_inter = per_head(lambda h, ln: _dot(vb[:, ln], dtb[h]))
            dk_st = kdec * dk_inter
            dg = qeb.astype(F32) * dq_inter
            dg_minus = kdb.astype(F32) * dk_inter
            dg = dg - dg_minus
            for blk in range(CHUNK // SUB):
                lo, hi = blk * SUB, (blk + 1) * SUB
                gref = g_s[lo - 1:lo, :] if blk else jnp.zeros((1, width), F32)
                qscale = jnp.exp(g[lo:hi] - gref)
                kscale = jnp.exp(jnp.where(rowk < hi, gref - g, NEG_BIG))
                qi = (q[lo:hi] * qscale).astype(BF16)
                ki = (k * kscale).astype(BF16)
                for h, ln in enumerate(lanes):
                    a_s[h, lo:hi, :] = _dot_nt(qi[:, ln], ki[:, ln])
                apb = [x[lo:hi].astype(BF16) for x in aps]
                from_k = per_head(lambda h, ln: _dot(apb[h], ki[:, ln]))
                from_q = per_head(lambda h, ln: _dot_tn(apb[h], qi[:, ln]))
                dq_s[lo:hi, :] = qscale * from_k
                dg_s[lo:hi, :] = qi.astype(F32) * from_k
                dk_st = dk_st + kscale * from_q
                dg = dg - ki.astype(F32) * from_q
            dg = dg + dg_s[...]
            dv = per_head(lambda h, ln: _dot_tn(jnp.where(tril, a_s[h], 0.0).astype(BF16), dob[:, ln])
                          + _dot_nt(kdb[:, ln], dtb[h]))
            dq_st = dq_s[...] + eg * dq_inter
            dq = dq_st + adiag * k
            dk = dk_st + adiag * q
            dec = jnp.exp(glast)
            dt_dec = [dt[h] * dec[:, ln] for h, ln in enumerate(lanes)]
            for h, ln in enumerate(lanes):
                d_acc[h] = dt_dec[h] + _dot_tn(dob[:, ln], qeb[:, ln])
            later = per_head(lambda h, ln: _colsum(dt_dec[h] * st0[h])) + _colsum(dg_minus)
            dlf = later + _tri_sum(triub, dg)
            df = dlf / f - dk
            dqfi_ref[rows, 0:width] = (dq * (sq * (1.0 + qr * (1.0 - sq)))).astype(dqfi_ref.dtype)
            dqfi_ref[rows, width:2 * width] = (df * (1.0 - lb) * sf * (1.0 - sf)).astype(dqfi_ref.dtype)
            dqfi_ref[rows, 2 * width:3 * width] = dv.astype(dqfi_ref.dtype)
            return dlb + _colsum(df * (1.0 - sf))

        dlb = lax.fori_loop(0, nc, chunk, jnp.zeros((1, width), F32))
        _accumulate(dlb_ref, dlb, first)

    assert hps == HEADS
    nt = n // t
    return _call(
        body, name=name, grid=(1, nt),
        in_specs=[sec(COL_Q), sec(COL_F), sec(COL_I), lbspec, head_tile, state],
        out_specs=[pl.BlockSpec((t, 3 * width), lambda h, i: (nt - 1 - i, 0)),
                   pl.BlockSpec((1, width), lambda h, i: (0, h))],
        out_shape=[jax.ShapeDtypeStruct((n, 3 * D_MODEL), BF16), jax.ShapeDtypeStruct((1, D_MODEL), F32)],
        args=[proj, proj, proj, hgrn_lb, do, states], semantics=("parallel", "arbitrary"),
        scratch=[pltpu.VMEM((hps, HEAD_DIM, HEAD_DIM), F32), pltpu.VMEM((CHUNK, width), F32),
                 pltpu.VMEM((hps, CHUNK, CHUNK), F32), pltpu.VMEM((CHUNK, width), F32),
                 pltpu.VMEM((CHUNK, width), F32)],
        exchange=exchange)


def _pool_fwd(proj, pool_w, pool_scale, *, name):
    n = proj.shape[0]
    t = _tile(n, 1024, POOL_HALO)
    per = t // POOL_HALO
    c0 = COL_POOL // POOL_WIDTH

    def body(u_ref, halo_ref, pw_ref, ps_ref, pooled_ref, mixed_ref, ext):
        i = pl.program_id(0)
        u = u_ref[...].astype(F32)
        ext[POOL_HALO:POOL_HALO + t, :] = u
        ext[0:POOL_HALO, :] = jnp.where(i > 0, halo_ref[...].astype(F32), 0.0)
        pos = i * t + lax.broadcasted_iota(jnp.int32, (t, POOL_CH), 0) + 1
        for grp, win in enumerate(POOL_WINDOWS):
            cols = slice(grp * POOL_CH, (grp + 1) * POOL_CH)
            acc = u[:, cols]
            for j in range(1, win):
                acc = acc + ext[POOL_HALO - j:POOL_HALO - j + t, cols]
            pooled = (acc / jnp.minimum(pos, win).astype(F32) - u[:, cols]).astype(BF16)
            pooled_ref[:, cols] = pooled
            mixed_ref[:, cols] = (_dot(pooled, pw_ref[grp].astype(BF16)) * ps_ref[:, cols]).astype(BF16)

    tile = pl.BlockSpec((t, POOL_WIDTH), lambda i: (i, 0))
    return pl.pallas_call(
        body, name=name, grid=(n // t,),
        in_specs=[pl.BlockSpec((t, POOL_WIDTH), lambda i: (i, c0)),
                  pl.BlockSpec((POOL_HALO, POOL_WIDTH), lambda i: (jnp.maximum(i * per - 1, 0), c0)),
                  pl.BlockSpec((len(POOL_WINDOWS), POOL_CH, POOL_CH), lambda i: (0, 0, 0)),
                  pl.BlockSpec((1, POOL_WIDTH), lambda i: (0, 0))],
        out_specs=[tile, tile],
        out_shape=[jax.ShapeDtypeStruct((n, POOL_WIDTH), BF16), jax.ShapeDtypeStruct((n, POOL_WIDTH), BF16)],
        scratch_shapes=[pltpu.VMEM((t + POOL_HALO, POOL_WIDTH), F32)],
        compiler_params=_params("parallel"),
    )(proj, proj, pool_w, pool_scale)


def _pool_bwd(dmixed, pooled, pool_w, pool_scale, *, name):
    n = dmixed.shape[0]
    t = _tile(n, 1024, POOL_HALO)
    per = t // POOL_HALO
    nb = n // t

    def body(dm_ref, dmh_ref, p_ref, pw_ref, ps_ref, du_ref, dpw_ref, dps_ref, ext):
        i = pl.program_id(0)

        @pl.when(i == 0)
        def _():
            dpw_ref[...] = jnp.zeros_like(dpw_ref)
            dps_ref[...] = jnp.zeros_like(dps_ref)

        dm, dmh = dm_ref[...], dmh_ref[...]
        pos = i * t + lax.broadcasted_iota(jnp.int32, (t, POOL_CH), 0) + 1
        for grp, win in enumerate(POOL_WINDOWS):
            cols = slice(grp * POOL_CH, (grp + 1) * POOL_CH)
            pwb = pw_ref[grp].astype(BF16)
            pb = p_ref[:, cols]
            scale = ps_ref[:, cols]
            dps_ref[:, cols] += _colsum(dm[:, cols] * _dot(pb, pwb))
            dpm = (dm[:, cols] * scale).astype(BF16)
            dpw_ref[grp] += _dot_tn(pb, dpm)
            dpool = _dot_nt(dpm, pwb)
            dpool_next = _dot_nt((dmh[:, cols] * scale).astype(BF16), pwb)
            ext[0:t, cols] = dpool / jnp.minimum(pos, win).astype(F32)
            ext[t:t + POOL_HALO, cols] = jnp.where(i < nb - 1, dpool_next * (1.0 / win), 0.0)
            acc = -dpool
            for j in range(win):
                acc = acc + ext[j:j + t, cols]
            du_ref[:, cols] = acc.astype(du_ref.dtype)

    tile = pl.BlockSpec((t, POOL_WIDTH), lambda i: (i, 0))
    return pl.pallas_call(
        body, name=name, grid=(nb,),
        in_specs=[tile, pl.BlockSpec((POOL_HALO, POOL_WIDTH), lambda i: (jnp.minimum((i + 1) * per, nb * per - 1), 0)),
                  tile, pl.BlockSpec((len(POOL_WINDOWS), POOL_CH, POOL_CH), lambda i: (0, 0, 0)),
                  pl.BlockSpec((1, POOL_WIDTH), lambda i: (0, 0))],
        out_specs=[tile, pl.BlockSpec((len(POOL_WINDOWS), POOL_CH, POOL_CH), lambda i: (0, 0, 0)),
                   pl.BlockSpec((1, POOL_WIDTH), lambda i: (0, 0))],
        out_shape=[jax.ShapeDtypeStruct((n, POOL_WIDTH), BF16),
                   jax.ShapeDtypeStruct((len(POOL_WINDOWS), POOL_CH, POOL_CH), F32),
                   jax.ShapeDtypeStruct((1, POOL_WIDTH), F32)],
        scratch_shapes=[pltpu.VMEM((t + POOL_HALO, POOL_WIDTH), F32)],
        compiler_params=_params("arbitrary"),
    )(dmixed, dmixed, pooled, pool_w, pool_scale)


def _adamw(w, g, m, v):
    m2 = ADAM_B1 * m + (1.0 - ADAM_B1) * g
    v2 = ADAM_B2 * v + (1.0 - ADAM_B2) * (g * g)
    m_hat = m2 * (1.0 / (1.0 - ADAM_B1 ** ADAM_STEP))
    v_hat = v2 * (1.0 / (1.0 - ADAM_B2 ** ADAM_STEP))
    delta = -ADAM_LR * (m_hat / (jnp.sqrt(v_hat) + ADAM_EPS) + ADAM_WD * w)
    return delta, m2, v2


def _adam_big(recv, w, m, v, *, name):
    r, c = w.shape
    tr = _tile(r, 256, 16)

    def body(recv_ref, w_ref, m_ref, v_ref, g_ref, d_ref, m2_ref, v2_ref):
        g = recv_ref[0].astype(F32)
        for i in range(1, N_DEV):
            g = g + recv_ref[i].astype(F32)
        delta, m2, v2 = _adamw(w_ref[...], g, m_ref[...], v_ref[...])
        g_ref[...] = g
        d_ref[...] = delta
        m2_ref[...] = m2
        v2_ref[...] = v2

    tile = pl.BlockSpec((tr, c), lambda i: (i, 0))
    out = jax.ShapeDtypeStruct((r, c), F32)
    return pl.pallas_call(
        body, name=name, grid=(r // tr,),
        in_specs=[pl.BlockSpec((N_DEV, tr, c), lambda i: (0, i, 0)), tile, tile, tile],
        out_specs=[tile] * 4, out_shape=[out] * 4, compiler_params=_params("parallel"),
    )(recv, w, m, v)


def _adam_small(parts, w, m, v, *, name):
    n = len(SMALL_PARAMS)

    def body(*refs):
        parts_r, w_r, m_r, v_r = (refs[i * n:(i + 1) * n] for i in range(4))
        outs = refs[4 * n:]
        for j, key in enumerate(SMALL_PARAMS):
            g = parts_r[j][0]
            for i in range(1, N_DEV):
                g = g + parts_r[j][i]
            w_ = w_r[j][...]
            if key == "hgrn_lb":
                s0 = 1.0 / (1.0 + jnp.exp(w_[1:2] - w_[0:1]))
                ga = g * s0 * (1.0 - s0)
                sign = jnp.where(lax.broadcasted_iota(jnp.int32, w_.shape, 0) == 0, 1.0, -1.0)
                g = sign * jnp.broadcast_to(ga, w_.shape)
            delta, m2, v2 = _adamw(w_, g, m_r[j][...], v_r[j][...])
            for q, val in enumerate((g, delta, m2, v2)):
                outs[q * n + j][...] = val

    out_shape = [jax.ShapeDtypeStruct(w[k].shape, F32) for _ in range(4) for k in SMALL_PARAMS]
    res = pl.pallas_call(body, name=name, out_shape=out_shape, compiler_params=_params())(
        *[t[k] for t in (parts, w, m, v) for k in SMALL_PARAMS])
    return [res[q * n:(q + 1) * n] for q in range(4)]


def _as_2d(a):
    return a.reshape(-1, a.shape[-1])


SPLIT_AXIS = dict(BIG_WEIGHTS)


def _gather_of(names, weights):
    return _Exchange([_shard_to_send(weights[k][0], SPLIT_AXIS[k]) for k in names], gather=True)


def _scatter_of(names, dfull):
    return _Exchange([_to_slots(dfull[k], SPLIT_AXIS[k]) for k in names], gather=False)


def _shard_to_send(w, axis):
    return (w.T if axis == 1 else w).astype(BF16)


def _to_slots(dw, axis):
    rows, cols = dw.shape
    return dw.reshape(N_DEV, rows // N_DEV, cols)


def _from_slots(gathered, axis):
    _, r, c = gathered.shape
    return gathered.reshape(N_DEV * r, c)


def kernel(x, p, ffn1_norm, ffn1_w1, ffn1_w3, ffn1_w2, mix_norm, w_in, hgrn_lb, hgrn_onorm, w_branch_a, pool_w, pool_scale, w_branch_b, w_out, ffn2_norm, ffn2_w1, ffn2_w3, ffn2_w2, ple_norm, ple_w_gate, ple_w_proj, ple_post_norm, final_norm, loss_target, m_ffn1_norm, m_ffn1_w1, m_ffn1_w3, m_ffn1_w2, m_mix_norm, m_w_in, m_hgrn_lb, m_hgrn_onorm, m_w_branch_a, m_pool_w, m_pool_scale, m_w_branch_b, m_w_out, m_ffn2_norm, m_ffn2_w1, m_ffn2_w3, m_ffn2_w2, m_ple_norm, m_ple_w_gate, m_ple_w_proj, m_ple_post_norm, m_final_norm, v_ffn1_norm, v_ffn1_w1, v_ffn1_w3, v_ffn1_w2, v_mix_norm, v_w_in, v_hgrn_lb, v_hgrn_onorm, v_w_branch_a, v_pool_w, v_pool_scale, v_w_branch_b, v_w_out, v_ffn2_norm, v_ffn2_w1, v_ffn2_w3, v_ffn2_w2, v_ple_norm, v_ple_w_gate, v_ple_w_proj, v_ple_post_norm, v_final_norm):
    weights = dict(ffn1_norm=ffn1_norm, ffn1_w1=ffn1_w1, ffn1_w3=ffn1_w3, ffn1_w2=ffn1_w2, mix_norm=mix_norm, w_in=w_in, hgrn_lb=hgrn_lb, hgrn_onorm=hgrn_onorm, w_branch_a=w_branch_a, pool_w=pool_w, pool_scale=pool_scale, w_branch_b=w_branch_b, w_out=w_out, ffn2_norm=ffn2_norm, ffn2_w1=ffn2_w1, ffn2_w3=ffn2_w3, ffn2_w2=ffn2_w2, ple_norm=ple_norm, ple_w_gate=ple_w_gate, ple_w_proj=ple_w_proj, ple_post_norm=ple_post_norm, final_norm=final_norm)
    mom1 = dict(ffn1_norm=m_ffn1_norm, ffn1_w1=m_ffn1_w1, ffn1_w3=m_ffn1_w3, ffn1_w2=m_ffn1_w2, mix_norm=m_mix_norm, w_in=m_w_in, hgrn_lb=m_hgrn_lb, hgrn_onorm=m_hgrn_onorm, w_branch_a=m_w_branch_a, pool_w=m_pool_w, pool_scale=m_pool_scale, w_branch_b=m_w_branch_b, w_out=m_w_out, ffn2_norm=m_ffn2_norm, ffn2_w1=m_ffn2_w1, ffn2_w3=m_ffn2_w3, ffn2_w2=m_ffn2_w2, ple_norm=m_ple_norm, ple_w_gate=m_ple_w_gate, ple_w_proj=m_ple_w_proj, ple_post_norm=m_ple_post_norm, final_norm=m_final_norm)
    mom2 = dict(ffn1_norm=v_ffn1_norm, ffn1_w1=v_ffn1_w1, ffn1_w3=v_ffn1_w3, ffn1_w2=v_ffn1_w2, mix_norm=v_mix_norm, w_in=v_w_in, hgrn_lb=v_hgrn_lb, hgrn_onorm=v_hgrn_onorm, w_branch_a=v_w_branch_a, pool_w=v_pool_w, pool_scale=v_pool_scale, w_branch_b=v_w_branch_b, w_out=v_w_out, ffn2_norm=v_ffn2_norm, ffn2_w1=v_ffn2_w1, ffn2_w3=v_ffn2_w3, ffn2_w2=v_ffn2_w2, ple_norm=v_ple_norm, ple_w_gate=v_ple_w_gate, ple_w_proj=v_ple_w_proj, ple_post_norm=v_ple_post_norm, final_norm=v_final_norm)

    xs = x[0]
    ps = p[0, 0].astype(BF16)
    tgt = loss_target[0]
    n = xs.shape[0]

    g_f1, g_mix, g_on, g_f2 = ffn1_norm, mix_norm, hgrn_onorm, ffn2_norm
    g_ple, g_post, g_fin = ple_norm, ple_post_norm, final_norm.reshape(1, D_MODEL)
    lb2 = hgrn_lb
    pw, pscale = pool_w[0], pool_scale

    full = {}

    def keep(names, gathered):
        for k, g in zip(names, gathered):
            full[k] = _from_slots(g, SPLIT_AXIS[k])

    names = ("ffn1_w1", "ffn1_w3")
    ex = _gather_of(names, weights)
    h1 = _rms_fwd(xs, g_f1, name="ffn1_rms", exchange=ex)
    keep(names, ex.received)
    names = ("ffn1_w2", "w_in")
    ex = _gather_of(names, weights)
    a1, b1, s1 = _ffn_up(h1, full["ffn1_w1"], full["ffn1_w3"], name="ffn1_up", exchange=ex)
    keep(names, ex.received)
    names = ("w_branch_a", "w_branch_b", "w_out")
    ex = _gather_of(names, weights)
    x1, h2 = _mm_nn_res_rms(s1, full["ffn1_w2"], xs, g_mix, name="ffn1_down", scale=0.5, exchange=ex)
    keep(names, ex.received)
    names = ("ffn2_w1", "ffn2_w3", "ffn2_w2", "ple_w_gate", "ple_w_proj")
    ex = _gather_of(names, weights)
    proj = _mm_nn_wide(h2, full["w_in"], name="w_in_proj", out_dtype=BF16, tn=512, exchange=ex)
    keep(names, ex.received)
    o, states = _hgrn_fwd(proj, lb2, name="hgrn_fwd")
    on = _hgrn_post_fwd(o, proj, g_on, name="hgrn_post_fwd")
    ya = _mm_nn_wide(on, full["w_branch_a"], name="branch_a", out_dtype=BF16, b_is_km=True)
    pooled, mixed = _pool_fwd(proj, pw, pscale, name="pool_fwd")
    yb = _mm_nn_wide(mixed, full["w_branch_b"], name="branch_b", out_dtype=BF16)
    y = _merge_fwd(proj, ya, yb, name="merge_fwd")
    x2, h3 = _mm_nn_res_rms(y, full["w_out"], x1, g_f2, name="w_out_proj", scale=1.0)
    a2, b2, s2 = _ffn_up(h3, full["ffn2_w1"], full["ffn2_w3"], name="ffn2_up")
    x3, h4 = _mm_nn_res_rms(s2, full["ffn2_w2"], x2, g_ple, name="ffn2_down", scale=0.5)
    gpre = _mm_nn_wide(h4, full["ple_w_gate"], name="ple_gate", out_dtype=BF16, b_is_km=True)
    z = _mm_nn_wide(ps, full["ple_w_proj"], name="ple_proj", out_dtype=BF16)
    dx4, dpre, dz, loss_part, d_fin, d_post = _ple_final(x3, gpre, z, tgt, g_post, g_fin, name="ple_final")

    dfull, received = {}, {}

    def sent(names, exchange):
        received.update(zip(names, exchange.received))

    dfull["ple_w_proj"] = _mm_tn(ps, dz, name="d_ple_w_proj", tn=1024, tm=1024, transpose_out=True)
    dfull["ple_w_gate"] = _mm_tn(h4, dpre, name="d_ple_w_gate", tn=1024, tm=1024)
    dx3, dx3s, d_ple = _mm_nt_rms_bwd([(dpre, full["ple_w_gate"], False)], x3, dx4, g_ple, name="ple_rms_bwd", tn=512,
                                      half_scale=0.5)

    names = ("ple_w_proj", "ple_w_gate")
    ex = _scatter_of(names, dfull)
    da2, db2 = _ffn_bwd_mid(dx3s, full["ffn2_w2"], a2, b2, name="ffn2_bwd_mid", exchange=ex)
    sent(names, ex)
    dfull["ffn2_w2"] = _mm_tn(s2, dx3s, name="ffn2_dw2", tn=1024, tm=1024)
    dfull["ffn2_w1"] = _mm_tn(h3, da2, name="ffn2_dw1", tn=1024, tm=2816, transpose_out=True)
    dfull["ffn2_w3"] = _mm_tn(h3, db2, name="ffn2_dw3", tn=1024, tm=2816, transpose_out=True)
    names = ("ffn2_w2",)
    ex = _scatter_of(names, dfull)
    dx2, dx2b, d_f2 = _mm_nt_rms_bwd([(da2, full["ffn2_w1"], True), (db2, full["ffn2_w3"], True)], x2, dx3, g_f2,
                                     name="ffn2_rms_bwd", tn=512, half_scale=1.0, exchange=ex)
    sent(names, ex)

    dfull["w_out"] = _mm_tn(y, dx2b, name="d_w_out", tn=1024, tm=1024)
    dy = _mm_nn_wide(dx2b, full["w_out"], name="d_y", out_dtype=BF16)
    dya, dyb, dga, dgb = _merge_bwd(dy, proj, ya, yb, name="merge_bwd")

    dfull["w_branch_b"] = _mm_tn(mixed, dyb, name="d_w_branch_b", tn=1024, tm=1024, transpose_out=True)
    dmixed = _mm_nn_wide(dyb, full["w_branch_b"], name="d_mixed", out_dtype=F32, b_is_km=True)
    du, d_pw, d_ps = _pool_bwd(dmixed, pooled, pw, pscale, name="pool_bwd")

    dfull["w_branch_a"] = _mm_tn(on, dya, name="d_w_branch_a", tn=1024, tm=1024)
    don = _mm_nn_wide(dya, full["w_branch_a"], name="d_on", out_dtype=BF16)
    dog, do, d_on = _hgrn_post_bwd(don, o, proj, g_on, name="hgrn_post_bwd")
    names = ("ffn2_w1", "ffn2_w3", "w_out", "w_branch_b", "w_branch_a")
    ex = _scatter_of(names, dfull)
    dqfi, d_lb = _hgrn_bwd(proj, lb2, do, states, name="hgrn_bwd", exchange=ex)
    sent(names, ex)
    dproj = [dqfi, dog, du, dga, dgb]
    dfull["w_in"] = jnp.concatenate(
        [_mm_tn(h2, part, name=f"d_w_in_{j}", tn=1024, tm=3072, transpose_out=True) for j, part in enumerate(dproj)],
        axis=0)
    names = ("w_in",)
    ex = _scatter_of(names, dfull)
    dx1, dx1s, d_mix = _mm_nt_rms_bwd([(dproj, full["w_in"], True)], x1, dx2, g_mix, name="mix_rms_bwd", tn=512,
                                      half_scale=0.5, exchange=ex)
    sent(names, ex)

    da1, db1 = _ffn_bwd_mid(dx1s, full["ffn1_w2"], a1, b1, name="ffn1_bwd_mid")
    dfull["ffn1_w2"] = _mm_tn(s1, dx1s, name="ffn1_dw2", tn=1024, tm=1024)
    names = ("ffn1_w2",)
    ex = _scatter_of(names, dfull)
    dfull["ffn1_w1"] = _mm_tn(h1, da1, name="ffn1_dw1", tn=1024, tm=2816, transpose_out=True, exchange=ex)
    sent(names, ex)
    names = ("ffn1_w1",)
    ex = _scatter_of(names, dfull)
    dfull["ffn1_w3"] = _mm_tn(h1, db1, name="ffn1_dw3", tn=1024, tm=2816, transpose_out=True, exchange=ex)
    sent(names, ex)
    names = ("ffn1_w3",)
    ex = _scatter_of(names, dfull)
    grad_x, _, d_f1 = _mm_nt_rms_bwd([(da1, full["ffn1_w1"], True), (db1, full["ffn1_w3"], True)], xs, dx1, g_f1,
                                     name="ffn1_rms_bwd", tn=512, half_scale=1.0, exchange=ex)
    sent(names, ex)

    small_part = dict(ffn1_norm=d_f1, mix_norm=d_mix, hgrn_onorm=d_on, ffn2_norm=d_f2, ple_norm=d_ple,
                      ple_post_norm=d_post, final_norm=d_fin, hgrn_lb=d_lb, pool_scale=d_ps, pool_w=_as_2d(d_pw))
    gathered_small = _exchange_now([small_part[k] for k in SMALL_PARAMS] + [loss_part], name="gather_small_grads",
                                   gather=True)
    small_all = dict(zip(SMALL_PARAMS, gathered_small))
    loss = jnp.sum(gathered_small[-1])

    grads, deltas, new_m, new_v = {}, {}, {}, {}
    for name, axis in BIG_WEIGHTS:
        shape, recv = weights[name].shape, received[name]
        own = [t[name][0].T if axis == 1 else t[name][0] for t in (weights, mom1, mom2)]
        res = _adam_big(recv, *own, name=f"adam_{name}")
        grads[name], deltas[name], new_m[name], new_v[name] = [(r.T if axis == 1 else r).reshape(shape) for r in res]
    res = _adam_small(small_all, *[{k: _as_2d(t[k]) for k in SMALL_PARAMS} for t in (weights, mom1, mom2)],
                      name="adam_small")
    for store, vals in zip((grads, deltas, new_m, new_v), res):
        store.update({k: val.reshape(weights[k].shape) for k, val in zip(SMALL_PARAMS, vals)})

    return (loss, grad_x.reshape(x.shape), *[grads[k] for k in WEIGHT_ORDER], *[deltas[k] for k in WEIGHT_ORDER],
            *[new_m[k] for k in WEIGHT_ORDER], *[new_v[k] for k in WEIGHT_ORDER])
```

```python
import jax
import jax.numpy as jnp
from jax import lax
from jax.experimental import pallas as pl
from jax.experimental.pallas import tpu as pltpu

F32 = jnp.float32
BF16 = jnp.bfloat16

N_DEV = 8
D_MODEL = 1024
HEADS = 8
HEAD_DIM = 128
POOL_WINDOWS = (2, 4, 8, 16)
POOL_CH = 128
POOL_WIDTH = 512
POOL_HALO = 16
RMS_EPS = 1e-6
CHUNK = 64
SUB = 32
HGRN_HEADS_PER_STEP = 8
NEG_BIG = -1e30

ADAM_LR = 0.001
ADAM_B1 = 0.9
ADAM_B2 = 0.999
ADAM_EPS = 1e-08
ADAM_WD = 0.01
ADAM_STEP = 10

V7X_VMEM_BYTES = 64 * 1024 * 1024
VMEM_LIMIT = (V7X_VMEM_BYTES * 3) // 4
EXCHANGE_TAIL_STEPS = 3
ROW_TILE_CAP = 8192

COL_Q, COL_F, COL_I, COL_OG, COL_POOL, COL_GA, COL_GB = 0, 1024, 2048, 3072, 4096, 4608, 5632

BIG_WEIGHTS = (
    ("ffn1_w1", 1), ("ffn1_w3", 1), ("ffn1_w2", 0), ("w_in", 1), ("w_branch_a", 0), ("w_branch_b", 1),
    ("w_out", 0), ("ffn2_w1", 1), ("ffn2_w3", 1), ("ffn2_w2", 0), ("ple_w_gate", 0), ("ple_w_proj", 1),
)
SMALL_PARAMS = ("ffn1_norm", "mix_norm", "hgrn_onorm", "ffn2_norm", "ple_norm", "ple_post_norm", "final_norm",
                "hgrn_lb", "pool_scale", "pool_w")
WEIGHT_ORDER = (
    "ffn1_norm", "ffn1_w1", "ffn1_w3", "ffn1_w2", "mix_norm", "w_in", "hgrn_lb", "hgrn_onorm", "w_branch_a", "pool_w",
    "pool_scale", "w_branch_b", "w_out", "ffn2_norm", "ffn2_w1", "ffn2_w3", "ffn2_w2", "ple_norm", "ple_w_gate",
    "ple_w_proj", "ple_post_norm", "final_norm",
)


def _params(*sem):
    return pltpu.CompilerParams(dimension_semantics=sem if sem else None, vmem_limit_bytes=VMEM_LIMIT)


COL_CHUNK = 256


def _rows(tn, width):
    return pl.BlockSpec((tn, width), lambda i: (i, 0))


def _resident(shape):
    return pl.BlockSpec(shape, lambda i: (0,) * len(shape), pipeline_mode=pl.Buffered(1))


def _dot(a, b):
    return jnp.dot(a, b, preferred_element_type=F32)


def _dot_nt(a, b):
    return lax.dot_general(a, b, (((1,), (1,)), ((), ())), preferred_element_type=F32)


def _dot_tn(a, b):
    return lax.dot_general(a, b, (((0,), (0,)), ((), ())), preferred_element_type=F32)


def _sigmoid(x):
    return 0.5 * jnp.tanh(0.5 * x) + 0.5


def _tile(n, want, mult):
    if mult != 128:
        want = min(want, ROW_TILE_CAP)
    if n <= want:
        return n
    t = (want // mult) * mult
    while t > mult and n % t:
        t -= mult
    assert n % t == 0, (n, want, mult)
    return t


class _Exchange:
    COPIES = N_DEV - 1

    def __init__(self, arrs, gather):
        self.arrs, self.gather, self.n = list(arrs), gather, len(arrs)
        self.out_shape = [jax.ShapeDtypeStruct((N_DEV,) + (a.shape if gather else a.shape[1:]), a.dtype) for a in arrs]
        self.scratch = [pltpu.SemaphoreType.DMA((self.n * self.COPIES,)),
                        pltpu.SemaphoreType.DMA((self.n * self.COPIES,)), pltpu.SemaphoreType.DMA((self.n,))]
        self.received = None

    @staticmethod
    def _place():
        x, y, c = lax.axis_index("x"), lax.axis_index("y"), lax.axis_index("c")
        return x, y, c

    def _copy(self, a, k, src, dst, to, sems):
        s = a * self.COPIES + k
        return pltpu.make_async_remote_copy(src_ref=src, dst_ref=dst, send_sem=sems[0].at[s], recv_sem=sems[1].at[s],
                                            device_id=to, device_id_type=pl.DeviceIdType.MESH)

    def _gather_copies(self, role, ins, outs, sems):
        x, y, c = self._place()
        chips = [(1 - x, y), (x, 1 - y), (1 - x, 1 - y)]
        sibling = (x, y, 1 - c)

        def slot(px, py, pc):
            return 4 * px + 2 * py + pc

        copies = []
        for a in range(self.n):
            mine = outs[a].at[slot(x, y, c)]
            if role == "first":
                copies.append(self._copy(a, 0, ins[a], mine, sibling, sems))
            elif role == "last":
                copies.append(self._copy(a, 0, ins[a], outs[a].at[slot(x, y, 1 - c)], sibling, sems))
            for j, (px, py) in enumerate(chips):
                theirs = outs[a].at[slot(px, py, c)]
                if role == "first":
                    copies.append(self._copy(a, 1 + j, ins[a], mine, (px, py, c), sems))
                elif role == "landed":
                    copies.append(self._copy(a, 1 + j, ins[a], theirs, (px, py, c), sems))
                elif role == "onward":
                    copies.append(self._copy(a, 4 + j, theirs, theirs, sibling, sems))
                else:
                    copies.append(self._copy(a, 4 + j, ins[a], outs[a].at[slot(px, py, 1 - c)], sibling, sems))
        return copies

    def _scatter_copies(self, role, ins, outs, sems):
        x, y, c = self._place()
        me = 4 * x + 2 * y + c
        copies = []
        for k in range(1, N_DEV):
            px = 1 - x if k & 4 else x
            py = 1 - y if k & 2 else y
            pc = 1 - c if k & 1 else c
            peer = 4 * px + 2 * py + pc
            for a in range(self.n):
                dst = outs[a].at[me] if role == "sends" else outs[a].at[peer]
                copies.append(self._copy(a, k - 1, ins[a].at[peer], dst, (px, py, pc), sems))
        return copies

    def _local(self, ins, outs, sems):
        x, y, c = self._place()
        me = 4 * x + 2 * y + c
        return [pltpu.make_async_copy(ins[a] if self.gather else ins[a].at[me], outs[a].at[me], sems[2].at[a])
                for a in range(self.n)]

    def start(self, ins, outs, sems):
        for cp in self._local(ins, outs, sems):
            cp.start()
        sends = (self._gather_copies("first", ins, outs, sems) if self.gather
                 else self._scatter_copies("sends", ins, outs, sems))
        for cp in sends:
            cp.start()

    def pass_on(self, ins, outs, sems):
        if self.gather:
            for landed, onward in zip(self._gather_copies("landed", ins, outs, sems),
                                      self._gather_copies("onward", ins, outs, sems)):
                landed.wait_recv()
                onward.start()

    def finish(self, ins, outs, sems):
        if self.gather:
            arrivals = self._gather_copies("last", ins, outs, sems)
            sends = self._gather_copies("first", ins, outs, sems) + self._gather_copies("onward", ins, outs, sems)
        else:
            arrivals = self._scatter_copies("arrivals", ins, outs, sems)
            sends = self._scatter_copies("sends", ins, outs, sems)
        for cp in arrivals:
            cp.wait_recv()
        for cp in sends:
            cp.wait_send()
        for cp in self._local(ins, outs, sems):
            cp.wait()


def _call(body, *, name, grid, in_specs, out_specs, out_shape, args, semantics, scratch=(), exchange=None):
    if exchange is None:
        return pl.pallas_call(
            body, name=name, grid=grid, in_specs=in_specs, out_specs=out_specs, out_shape=out_shape,
            scratch_shapes=list(scratch), compiler_params=_params(*semantics))(*args)
    ex = exchange
    n_in, n_out, n_s = len(in_specs), len(out_specs), len(scratch)

    def wrapped(*refs):
        ins, ex_in = refs[:n_in], refs[n_in:n_in + ex.n]
        o0 = n_in + ex.n
        outs, ex_out = refs[o0:o0 + n_out], refs[o0 + n_out:o0 + n_out + ex.n]
        s0 = o0 + n_out + ex.n
        scr, sems = refs[s0:s0 + n_s], refs[s0 + n_s:]
        step, steps = pl.program_id(0), grid[0]
        for ax in range(1, len(grid)):
            step, steps = step * grid[ax] + pl.program_id(ax), steps * grid[ax]

        @pl.when(step == 0)
        def _():
            ex.start(ex_in, ex_out, sems)

        body(*ins, *outs, *scr)

        @pl.when(step == max(steps - EXCHANGE_TAIL_STEPS, 0))
        def _():
            ex.pass_on(ex_in, ex_out, sems)

        @pl.when(step == steps - 1)
        def _():
            ex.finish(ex_in, ex_out, sems)

    hbm = pl.BlockSpec(memory_space=pltpu.HBM)
    res = pl.pallas_call(
        wrapped, name=name, grid=grid, in_specs=list(in_specs) + [hbm] * ex.n,
        out_specs=list(out_specs) + [hbm] * ex.n, out_shape=list(out_shape) + ex.out_shape,
        scratch_shapes=list(scratch) + ex.scratch, compiler_params=_params(*(["arbitrary"] * len(grid))),
    )(*args, *ex.arrs)
    ex.received = res[n_out:]
    return res[:n_out]


def _exchange_now(arrs, *, name, gather):
    ex = _Exchange(arrs, gather)
    n = ex.n

    def body(*refs):
        ex.start(refs[:n], refs[n:2 * n], refs[2 * n:])
        ex.pass_on(refs[:n], refs[n:2 * n], refs[2 * n:])
        ex.finish(refs[:n], refs[n:2 * n], refs[2 * n:])

    hbm = pl.BlockSpec(memory_space=pltpu.HBM)
    return pl.pallas_call(body, name=name, out_shape=ex.out_shape, in_specs=[hbm] * n, out_specs=[hbm] * n,
                          scratch_shapes=ex.scratch)(*arrs)


def _mm_tn(a, b, *, name, tn, tm, transpose_out=False, exchange=None):
    n, k = a.shape
    m = b.shape[1]
    tn, tm = _tile(n, tn, 16), _tile(m, tm, 128)
    steps = n // tn

    def body(a_ref, b_ref, o_ref, acc):
        i = pl.program_id(1)

        @pl.when(i == 0)
        def _():
            acc[...] = jnp.zeros_like(acc)

        acc[...] += _dot_tn(a_ref[...], b_ref[...])

        @pl.when(i == steps - 1)
        def _():
            for c0 in range(0, tm, COL_CHUNK):
                cols = slice(c0, min(c0 + COL_CHUNK, tm))
                if transpose_out:
                    o_ref[cols, :] = acc[:, cols].T.astype(o_ref.dtype)
                else:
                    o_ref[:, cols] = acc[:, cols].astype(o_ref.dtype)

    if transpose_out:
        out_spec, out_shape = pl.BlockSpec((tm, k), lambda j, i: (j, 0)), jax.ShapeDtypeStruct((m, k), BF16)
    else:
        out_spec, out_shape = pl.BlockSpec((k, tm), lambda j, i: (0, j)), jax.ShapeDtypeStruct((k, m), BF16)
    return _call(body, name=name, grid=(m // tm, steps),
                 in_specs=[pl.BlockSpec((tn, k), lambda j, i: (i, 0)), pl.BlockSpec((tn, tm), lambda j, i: (i, j))],
                 out_specs=[out_spec], out_shape=[out_shape], args=[a, b], semantics=("parallel", "arbitrary"),
                 scratch=[pltpu.VMEM((k, tm), F32)], exchange=exchange)[0]


def _ffn_up(h, w1, w3, *, name, tn=512, exchange=None):
    n, k = h.shape
    m = w1.shape[0]
    tn = _tile(n, tn, 16)

    def body(h_ref, w1_ref, w3_ref, dsda_ref, dsdb_ref, s_ref):
        for c0 in range(0, m, COL_CHUNK):
            cols = slice(c0, c0 + COL_CHUNK)
            a = _dot_nt(h_ref[...], w1_ref[cols, :])
            b = _dot_nt(h_ref[...], w3_ref[cols, :])
            sg = _sigmoid(a)
            silu = a * sg
            dsda_ref[:, cols] = (b * (sg + silu * (1.0 - sg))).astype(dsda_ref.dtype)
            dsdb_ref[:, cols] = silu.astype(dsdb_ref.dtype)
            s_ref[:, cols] = (silu * b).astype(s_ref.dtype)

    ospec = _rows(tn, m)
    return _call(body, name=name, grid=(n // tn,),
                 in_specs=[_rows(tn, k), _resident(w1.shape), _resident(w3.shape)], out_specs=[ospec, ospec, ospec],
                 out_shape=[jax.ShapeDtypeStruct((n, m), BF16)] * 3,
                 args=[h, w1, w3], semantics=("parallel",), exchange=exchange)


def _mm_nn_wide(a, b, *, name, out_dtype, b_is_km=False, tn=1024, exchange=None):
    n, k = a.shape
    m = b.shape[1 if b_is_km else 0]
    tn = _tile(n, tn, 16)
    chunk = min(2 * COL_CHUNK, m)

    def body(a_ref, b_ref, o_ref):
        for c0 in range(0, m, chunk):
            cols = slice(c0, c0 + chunk)
            if b_is_km:
                res = _dot(a_ref[...], b_ref[:, cols])
            else:
                res = _dot_nt(a_ref[...], b_ref[cols, :])
            o_ref[:, cols] = res.astype(o_ref.dtype)

    return _call(body, name=name, grid=(n // tn,), in_specs=[_rows(tn, k), _resident(b.shape)],
                 out_specs=[_rows(tn, m)], out_shape=[jax.ShapeDtypeStruct((n, m), out_dtype)], args=[a, b],
                 semantics=("parallel",), exchange=exchange)[0]


def _mm_nn_res_rms(a, b, res, g, *, name, scale, exchange=None):
    n, k = a.shape
    d = b.shape[1]
    tn = _tile(n, 1024, 16)

    def body(a_ref, b_ref, r_ref, g_ref, x_ref, h_ref):
        for c0 in range(0, d, COL_CHUNK):
            cols = slice(c0, c0 + COL_CHUNK)
            x_ref[:, cols] = r_ref[:, cols] + scale * _dot(a_ref[...], b_ref[:, cols])
        x = x_ref[...]
        r = lax.rsqrt(_rowmean(x * x) + RMS_EPS)
        h_ref[...] = (x * r * g_ref[...]).astype(h_ref.dtype)

    row = _rows(tn, d)
    return _call(body, name=name, grid=(n // tn,),
                 in_specs=[_rows(tn, k), _resident(b.shape), row, pl.BlockSpec((1, d), lambda i: (0, 0))],
                 out_specs=[row, row],
                 out_shape=[jax.ShapeDtypeStruct((n, d), F32), jax.ShapeDtypeStruct((n, d), BF16)],
                 args=[a, b, res, g], semantics=("parallel",), exchange=exchange)


def _ffn_bwd_mid(dxs, w2, dsda, dsdb, *, name, exchange=None):
    n, d = dxs.shape
    m = w2.shape[0]
    tn = _tile(n, 512, 16)

    def body(dx_ref, w2_ref, dsda_ref, dsdb_ref, da_ref, db_ref):
        for c0 in range(0, m, COL_CHUNK):
            cols = slice(c0, c0 + COL_CHUNK)
            ds = _dot_nt(dx_ref[...], w2_ref[cols, :])
            da_ref[:, cols] = (ds * dsda_ref[:, cols].astype(F32)).astype(da_ref.dtype)
            db_ref[:, cols] = (ds * dsdb_ref[:, cols].astype(F32)).astype(db_ref.dtype)

    tile = _rows(tn, m)
    return _call(body, name=name, grid=(n // tn,),
                 in_specs=[_rows(tn, d), _resident(w2.shape), tile, tile], out_specs=[tile, tile],
                 out_shape=[jax.ShapeDtypeStruct((n, m), BF16), jax.ShapeDtypeStruct((n, m), BF16)],
                 args=[dxs, w2, dsda, dsdb], semantics=("parallel",), exchange=exchange)


def _rowwise(fn, *, name, n, tn, ncol, rows, vecs, outs, accs=(), exchange=None):
    tn = _tile(n, tn, 16)
    nr, nv, no = len(rows), len(vecs), len(outs)

    def body(*refs):
        first = pl.program_id(1) == 0
        vals = [r[...].astype(F32) for r in refs[:nr + nv]]
        res = fn(*vals)
        for ref, val in zip(refs[nr + nv:nr + nv + no], res[:no]):
            ref[...] = val.astype(ref.dtype)
        for ref, val in zip(refs[nr + nv + no:], res[no:]):
            _accumulate(ref, val, first)

    in_specs = [pl.BlockSpec((tn, w), lambda j, i, c0=c0: (i, c0 + j)) for _, w, c0 in rows]
    in_specs += [pl.BlockSpec((1, w), lambda j, i, c0=c0: (0, c0 + j)) for _, w, c0 in vecs]
    out_specs = [pl.BlockSpec((tn, w), lambda j, i: (i, j)) for _, w, _ in outs]
    out_specs += [pl.BlockSpec((1, w), lambda j, i: (0, j)) for _, w in accs]
    out_shape = [jax.ShapeDtypeStruct((n, tw), dt) for tw, _, dt in outs]
    out_shape += [jax.ShapeDtypeStruct((1, tw), F32) for tw, _ in accs]
    return _call(body, name=name, grid=(ncol, n // tn), in_specs=in_specs, out_specs=out_specs, out_shape=out_shape,
                 args=[r[0] for r in rows] + [v[0] for v in vecs], semantics=("parallel", "arbitrary"),
                 exchange=exchange)


def _accumulate(ref, val, first):
    @pl.when(first)
    def _():
        ref[...] = jnp.zeros_like(ref)

    ref[...] += val


def _colsum(x):
    return jnp.sum(x, axis=0, keepdims=True)


def _rowmean(x):
    return jnp.mean(x, axis=-1, keepdims=True)


def _rms_fwd(x, g, *, name, exchange=None):
    def fn(x_, g_):
        r = lax.rsqrt(_rowmean(x_ * x_) + RMS_EPS)
        return (x_ * r * g_,)

    n, d = x.shape
    return _rowwise(fn, name=name, n=n, tn=512, ncol=1, rows=[(x, d, 0)], vecs=[(g, d, 0)], outs=[(d, d, BF16)],
                    exchange=exchange)[0]


def _mm_nt_rms_bwd(pairs, x, extra, g, *, name, tn, half_scale, exchange=None):
    n, d = x.shape
    tn = _tile(n, tn, 16)
    pairs = [(list(a) if isinstance(a, (list, tuple)) else [a], b, t) for a, b, t in pairs]
    nref = sum(len(a) + 1 for a, _, _ in pairs)

    def body(*refs):
        x_ref, e_ref, g_ref, dx_ref, dxs_ref, dg_ref = refs[nref:]
        dh, at = None, 0
        for parts, _, transposed in pairs:
            b_ref = refs[at + len(parts)]
            col = 0
            for j, part in enumerate(parts):
                w = part.shape[1]
                if transposed:
                    term = _dot(refs[at + j][...], b_ref[col:col + w, :])
                else:
                    term = _dot_nt(refs[at + j][...], b_ref[:, col:col + w])
                dh = term if dh is None else dh + term
                col += w
            at += len(parts) + 1
        x_ = x_ref[...]
        r = lax.rsqrt(_rowmean(x_ * x_) + RMS_EPS)
        xh = x_ * r
        dxh = dh * g_ref[...]
        dx = e_ref[...] + r * (dxh - xh * _rowmean(dxh * xh))
        dx_ref[...] = dx
        dxs_ref[...] = (dx * half_scale).astype(dxs_ref.dtype)
        _accumulate(dg_ref, _colsum(dh * xh), pl.program_id(0) == 0)

    in_specs, args = [], []
    for parts, b, transposed in pairs:
        assert sum(part.shape[1] for part in parts) == b.shape[0 if transposed else 1]
        in_specs += [pl.BlockSpec((tn, part.shape[1]), lambda i: (i, 0)) for part in parts]
        in_specs.append(pl.BlockSpec(b.shape, lambda i: (0, 0), pipeline_mode=pl.Buffered(1)))
        args += parts + [b]
    row = pl.BlockSpec((tn, d), lambda i: (i, 0))
    vec = pl.BlockSpec((1, d), lambda i: (0, 0))
    return _call(body, name=name, grid=(n // tn,), in_specs=in_specs + [row, row, vec], out_specs=[row, row, vec],
                 out_shape=[jax.ShapeDtypeStruct((n, d), F32), jax.ShapeDtypeStruct((n, d), BF16),
                            jax.ShapeDtypeStruct((1, d), F32)],
                 args=args + [x, extra, g], semantics=("arbitrary",), exchange=exchange)


def _ple_final(x3, gpre, z, tgt, gpp, gf, *, name):
    def fn(x3_, gpre_, z_, tgt_, gpp_, gf_):
        gate = _sigmoid(gpre_)
        rz = lax.rsqrt(_rowmean(z_ * z_) + RMS_EPS)
        zh = z_ * rz
        e = zh * gpp_
        x4 = x3_ + gate * e
        r4 = lax.rsqrt(_rowmean(x4 * x4) + RMS_EPS)
        x4h = x4 * r4
        diff = x4h * gf_ - tgt_
        dout = diff * (1.0 / D_MODEL)
        dxh4 = dout * gf_
        dx4 = r4 * (dxh4 - x4h * _rowmean(dxh4 * x4h))
        dpre = dx4 * e * gate * (1.0 - gate)
        de = dx4 * gate
        dzh = de * gpp_
        dz = rz * (dzh - zh * _rowmean(dzh * zh))
        return dx4, dpre, dz, _colsum(diff * diff) * (0.5 / D_MODEL), _colsum(dout * x4h), _colsum(de * zh)

    n, d = x3.shape
    return _rowwise(fn, name=name, n=n, tn=512, ncol=1, rows=[(x3, d, 0), (gpre, d, 0), (z, d, 0), (tgt, d, 0)],
                    vecs=[(gpp, d, 0), (gf, d, 0)], outs=[(d, d, F32), (d, d, BF16), (d, d, BF16)],
                    accs=[(d, d), (d, d), (d, d)])


def _merge_fwd(proj, ya, yb, *, name):
    def fn(ga, gb, ya_, yb_):
        return (_sigmoid(ga) * ya_ + _sigmoid(gb) * yb_,)

    n = proj.shape[0]
    w = 512
    return _rowwise(fn, name=name, n=n, tn=2048, ncol=D_MODEL // w,
                    rows=[(proj, w, COL_GA // w), (proj, w, COL_GB // w), (ya, w, 0), (yb, w, 0)], vecs=[],
                    outs=[(D_MODEL, w, BF16)])[0]


def _merge_bwd(dy, proj, ya, yb, *, name):
    def fn(dy_, ga, gb, ya_, yb_):
        sa, sb = _sigmoid(ga), _sigmoid(gb)
        return dy_ * sa, dy_ * sb, dy_ * ya_ * sa * (1.0 - sa), dy_ * yb_ * sb * (1.0 - sb)

    n = proj.shape[0]
    w = 512
    return _rowwise(fn, name=name, n=n, tn=2048, ncol=D_MODEL // w,
                    rows=[(dy, w, 0), (proj, w, COL_GA // w), (proj, w, COL_GB // w), (ya, w, 0), (yb, w, 0)],
                    vecs=[], outs=[(D_MODEL, w, BF16)] * 4)


def _head_mean(x):
    return jnp.concatenate(
        [jnp.broadcast_to(jnp.mean(x[:, h * HEAD_DIM:(h + 1) * HEAD_DIM], axis=-1, keepdims=True),
                          (x.shape[0], HEAD_DIM)) for h in range(HEADS)], axis=1)


def _hgrn_post_fwd(o, proj, onorm, *, name):
    def fn(o_, og, gam):
        r = lax.rsqrt(_head_mean(o_ * o_) + RMS_EPS)
        return (o_ * r * gam * (og * _sigmoid(og)),)

    n = o.shape[0]
    w = D_MODEL
    return _rowwise(fn, name=name, n=n, tn=512, ncol=1, rows=[(o, w, 0), (proj, w, COL_OG // w)],
                    vecs=[(onorm, w, 0)], outs=[(D_MODEL, w, BF16)])[0]


def _hgrn_post_bwd(don, o, proj, onorm, *, name):
    def fn(don_, o_, og, gam):
        r = lax.rsqrt(_head_mean(o_ * o_) + RMS_EPS)
        oh = o_ * r
        sg = _sigmoid(og)
        dog = don_ * oh * gam * (sg * (1.0 + og * (1.0 - sg)))
        dn = don_ * (og * sg)
        doh = dn * gam
        do = r * (doh - oh * _head_mean(doh * oh))
        return dog, do, _colsum(dn * oh)

    n = o.shape[0]
    w = D_MODEL
    return _rowwise(fn, name=name, n=n, tn=512, ncol=1, rows=[(don, w, 0), (o, w, 0), (proj, w, COL_OG // w)],
                    vecs=[(onorm, w, 0)], outs=[(D_MODEL, w, BF16), (D_MODEL, w, BF16)], accs=[(D_MODEL, w)])


def _tri_sum(tri, x):
    hi = x.astype(BF16)
    lo = (x - hi.astype(F32)).astype(BF16)
    return _dot(tri, hi) + _dot(tri, lo)


def _lower_bound(lb_ref):
    return 1.0 / (1.0 + jnp.exp(lb_ref[1:2, :] - lb_ref[0:1, :]))


def _hgrn_specs(n, t, reverse):
    nt = n // t
    width = HGRN_HEADS_PER_STEP * HEAD_DIM

    def tok(i):
        return nt - 1 - i if reverse else i

    def sec(col):
        c0 = col // width
        return pl.BlockSpec((t, width), lambda h, i: (tok(i), c0 + h))

    head_tile = pl.BlockSpec((t, width), lambda h, i: (tok(i), h))
    state = pl.BlockSpec((HGRN_HEADS_PER_STEP, t // CHUNK, HEAD_DIM, HEAD_DIM), lambda h, i: (h, tok(i), 0, 0))
    lb = pl.BlockSpec((2, width), lambda h, i: (0, h))
    return sec, head_tile, state, lb


def _hgrn_fwd(proj, hgrn_lb, *, name):
    n = proj.shape[0]
    t = _tile(n, 512, CHUNK)
    nc = t // CHUNK
    hps = HGRN_HEADS_PER_STEP
    width = hps * HEAD_DIM
    lanes = [slice(h * HEAD_DIM, (h + 1) * HEAD_DIM) for h in range(hps)]
    sec, head_tile, state, lbspec = _hgrn_specs(n, t, False)

    def body(q_ref, f_ref, i_ref, lb_ref, o_ref, st_ref, s_acc, g_s, a_s):
        @pl.when(pl.program_id(1) == 0)
        def _():
            s_acc[...] = jnp.zeros_like(s_acc)

        lb = _lower_bound(lb_ref)
        row = lax.broadcasted_iota(jnp.int32, (CHUNK, CHUNK), 0)
        col = lax.broadcasted_iota(jnp.int32, (CHUNK, CHUNK), 1)
        tril = row >= col
        trilb = jnp.where(tril, 1.0, 0.0).astype(BF16)
        rowk = lax.broadcasted_iota(jnp.int32, (CHUNK, width), 0)

        def chunk(c, carry):
            rows = pl.ds(pl.multiple_of(c * CHUNK, CHUNK), CHUNK)
            qr, fr, v = [r[rows, :].astype(F32) for r in (q_ref, f_ref, i_ref)]
            q = qr * _sigmoid(qr)
            f = lb + (1.0 - lb) * _sigmoid(fr)
            k = 1.0 - f
            g = _tri_sum(trilb, jnp.log(f))
            g_s[...] = g
            st0 = [s_acc[h] for h in range(hps)]
            for h in range(hps):
                st_ref[h, c] = st0[h]
            vb = v.astype(BF16)
            for blk in range(CHUNK // SUB):
                lo, hi = blk * SUB, (blk + 1) * SUB
                gref = g_s[lo - 1:lo, :] if blk else jnp.zeros((1, width), F32)
                qi = (q[lo:hi] * jnp.exp(g[lo:hi] - gref)).astype(BF16)
                ki = (k * jnp.exp(jnp.where(rowk < hi, gref - g, NEG_BIG))).astype(BF16)
                for h, ln in enumerate(lanes):
                    a_s[h, lo:hi, :] = _dot_nt(qi[:, ln], ki[:, ln])
            qeb = (q * jnp.exp(g)).astype(BF16)
            o_ref[rows, :] = jnp.concatenate(
                [_dot(jnp.where(tril, a_s[h], 0.0).astype(BF16), vb[:, ln]) + _dot_nt(qeb[:, ln], st0[h].astype(BF16))
                 for h, ln in enumerate(lanes)], axis=1).astype(o_ref.dtype)
            glast = g_s[CHUNK - 1:CHUNK, :]
            kdb = (k * jnp.exp(glast - g)).astype(BF16)
            dec = jnp.exp(glast)
            for h, ln in enumerate(lanes):
                s_acc[h] = st0[h] * dec[:, ln] + _dot_tn(vb[:, ln], kdb[:, ln])
            return carry

        lax.fori_loop(0, nc, chunk, 0)

    return pl.pallas_call(
        body, name=name, grid=(HEADS // hps, n // t),
        in_specs=[sec(COL_Q), sec(COL_F), sec(COL_I), lbspec], out_specs=[head_tile, state],
        out_shape=[jax.ShapeDtypeStruct((n, D_MODEL), BF16),
                   jax.ShapeDtypeStruct((HEADS, n // CHUNK, HEAD_DIM, HEAD_DIM), F32)],
        scratch_shapes=[pltpu.VMEM((hps, HEAD_DIM, HEAD_DIM), F32), pltpu.VMEM((CHUNK, width), F32),
                        pltpu.VMEM((hps, CHUNK, CHUNK), F32)],
        compiler_params=_params("parallel", "arbitrary"),
    )(proj, proj, proj, hgrn_lb)


def _hgrn_bwd(proj, hgrn_lb, do, states, *, name, exchange=None):
    n = proj.shape[0]
    t = _tile(n, 512, CHUNK)
    nc = t // CHUNK
    hps = HGRN_HEADS_PER_STEP
    width = hps * HEAD_DIM
    lanes = [slice(h * HEAD_DIM, (h + 1) * HEAD_DIM) for h in range(hps)]
    sec, head_tile, state, lbspec = _hgrn_specs(n, t, True)

    def body(q_ref, f_ref, i_ref, lb_ref, do_ref, st_ref, dqfi_ref, dlb_ref, d_acc, g_s, a_s, dq_s,
             dg_s):
        first = pl.program_id(1) == 0

        @pl.when(first)
        def _():
            d_acc[...] = jnp.zeros_like(d_acc)

        lb = _lower_bound(lb_ref)
        row = lax.broadcasted_iota(jnp.int32, (CHUNK, CHUNK), 0)
        col = lax.broadcasted_iota(jnp.int32, (CHUNK, CHUNK), 1)
        tril = row >= col
        trilb = jnp.where(tril, 1.0, 0.0).astype(BF16)
        triub = jnp.where(row <= col, 1.0, 0.0).astype(BF16)
        rowk = lax.broadcasted_iota(jnp.int32, (CHUNK, width), 0)

        def per_head(fn):
            return jnp.concatenate([fn(h, ln) for h, ln in enumerate(lanes)], axis=1)

        def chunk(j, dlb):
            c = nc - 1 - j
            rows = pl.ds(pl.multiple_of(c * CHUNK, CHUNK), CHUNK)
            qr, fr, v, dout = [r[rows, :].astype(F32) for r in (q_ref, f_ref, i_ref, do_ref)]
            sq = _sigmoid(qr)
            q = qr * sq
            sf = _sigmoid(fr)
            f = lb + (1.0 - lb) * sf
            k = 1.0 - f
            g = _tri_sum(trilb, jnp.log(f))
            g_s[...] = g
            st0 = [st_ref[h, c] for h in range(hps)]
            dt = [d_acc[h] for h in range(hps)]
            vb, dob = v.astype(BF16), dout.astype(BF16)
            dtb = [x.astype(BF16) for x in dt]
            st0b = [x.astype(BF16) for x in st0]
            glast = g_s[CHUNK - 1:CHUNK, :]
            eg = jnp.exp(g)
            kdec = jnp.exp(glast - g)
            qeb, kdb = (q * eg).astype(BF16), (k * kdec).astype(BF16)
            aps = [jnp.where(row > col, _dot_nt(dob[:, ln], vb[:, ln]), 0.0) for ln in lanes]
            dov = dout * v
            adiag = per_head(lambda h, ln: jnp.broadcast_to(
                jnp.sum(dov[:, ln], axis=-1, keepdims=True), (CHUNK, HEAD_DIM)))
            dq_inter = per_head(lambda h, ln: _dot(dob[:, ln], st0b[h]))
            dk_inter = per_head(lambda h, ln: _dot(vb[:, ln], dtb[h]))
            dk_st = kdec * dk_inter
            dg = qeb.astype(F32) * dq_inter
            dg_minus = kdb.astype(F32) * dk_inter
            dg = dg - dg_minus
            for blk in range(CHUNK // SUB):
                lo, hi = blk * SUB, (blk + 1) * SUB
                gref = g_s[lo - 1:lo, :] if blk else jnp.zeros((1, width), F32)
                qscale = jnp.exp(g[lo:hi] - gref)
                kscale = jnp.exp(jnp.where(rowk < hi, gref - g, NEG_BIG))
                qi = (q[lo:hi] * qscale).astype(BF16)
                ki = (k * kscale).astype(BF16)
                for h, ln in enumerate(lanes):
                    a_s[h, lo:hi, :] = _dot_nt(qi[:, ln], ki[:, ln])
                apb = [x[lo:hi].astype(BF16) for x in aps]
                from_k = per_head(lambda h, ln: _dot(apb[h], ki[:, ln]))
                from_q = per_head(lambda h, ln: _dot_tn(apb[h], qi[:, ln]))
                dq_s[lo:hi, :] = qscale * from_k
                dg_s[lo:hi, :] = qi.astype(F32) * from_k
                dk_st = dk_st + kscale * from_q
                dg = dg - ki.astype(F32) * from_q
            dg = dg + dg_s[...]
            dv = per_head(lambda h, ln: _dot_tn(jnp.where(tril, a_s[h], 0.0).astype(BF16), dob[:, ln])
                          + _dot_nt(kdb[:, ln], dtb[h]))
            dq_st = dq_s[...] + eg * dq_inter
            dq = dq_st + adiag * k
            dk = dk_st + adiag * q
            dec = jnp.exp(glast)
            dt_dec = [dt[h] * dec[:, ln] for h, ln in enumerate(lanes)]
            for h, ln in enumerate(lanes):
                d_acc[h] = dt_dec[h] + _dot_tn(dob[:, ln], qeb[:, ln])
            later = per_head(lambda h, ln: _colsum(dt_dec[h] * st0[h])) + _colsum(dg_minus)
            dlf = later + _tri_sum(triub, dg)
            df = dlf / f - dk
            dqfi_ref[rows, 0:width] = (dq * (sq * (1.0 + qr * (1.0 - sq)))).astype(dqfi_ref.dtype)
            dqfi_ref[rows, width:2 * width] = (df * (1.0 - lb) * sf * (1.0 - sf)).astype(dqfi_ref.dtype)
            dqfi_ref[rows, 2 * width:3 * width] = dv.astype(dqfi_ref.dtype)
            return dlb + _colsum(df * (1.0 - sf))

        dlb = lax.fori_loop(0, nc, chunk, jnp.zeros((1, width), F32))
        _accumulate(dlb_ref, dlb, first)

    assert hps == HEADS
    nt = n // t
    return _call(
        body, name=name, grid=(1, nt),
        in_specs=[sec(COL_Q), sec(COL_F), sec(COL_I), lbspec, head_tile, state],
        out_specs=[pl.BlockSpec((t, 3 * width), lambda h, i: (nt - 1 - i, 0)),
                   pl.BlockSpec((1, width), lambda h, i: (0, h))],
        out_shape=[jax.ShapeDtypeStruct((n, 3 * D_MODEL), BF16), jax.ShapeDtypeStruct((1, D_MODEL), F32)],
        args=[proj, proj, proj, hgrn_lb, do, states], semantics=("parallel", "arbitrary"),
        scratch=[pltpu.VMEM((hps, HEAD_DIM, HEAD_DIM), F32), pltpu.VMEM((CHUNK, width), F32),
                 pltpu.VMEM((hps, CHUNK, CHUNK), F32), pltpu.VMEM((CHUNK, width), F32),
                 pltpu.VMEM((CHUNK, width), F32)],
        exchange=exchange)


def _pool_fwd(proj, pool_w, pool_scale, *, name):
    n = proj.shape[0]
    t = _tile(n, 1024, POOL_HALO)
    per = t // POOL_HALO
    c0 = COL_POOL // POOL_WIDTH

    def body(u_ref, halo_ref, pw_ref, ps_ref, pooled_ref, mixed_ref, ext):
        i = pl.program_id(0)
        u = u_ref[...].astype(F32)
        ext[POOL_HALO:POOL_HALO + t, :] = u
        ext[0:POOL_HALO, :] = jnp.where(i > 0, halo_ref[...].astype(F32), 0.0)
        pos = i * t + lax.broadcasted_iota(jnp.int32, (t, POOL_CH), 0) + 1
        for grp, win in enumerate(POOL_WINDOWS):
            cols = slice(grp * POOL_CH, (grp + 1) * POOL_CH)
            acc = u[:, cols]
            for j in range(1, win):
                acc = acc + ext[POOL_HALO - j:POOL_HALO - j + t, cols]
            pooled = (acc / jnp.minimum(pos, win).astype(F32) - u[:, cols]).astype(BF16)
            pooled_ref[:, cols] = pooled
            mixed_ref[:, cols] = (_dot(pooled, pw_ref[grp].astype(BF16)) * ps_ref[:, cols]).astype(BF16)

    tile = pl.BlockSpec((t, POOL_WIDTH), lambda i: (i, 0))
    return pl.pallas_call(
        body, name=name, grid=(n // t,),
        in_specs=[pl.BlockSpec((t, POOL_WIDTH), lambda i: (i, c0)),
                  pl.BlockSpec((POOL_HALO, POOL_WIDTH), lambda i: (jnp.maximum(i * per - 1, 0), c0)),
                  pl.BlockSpec((len(POOL_WINDOWS), POOL_CH, POOL_CH), lambda i: (0, 0, 0)),
                  pl.BlockSpec((1, POOL_WIDTH), lambda i: (0, 0))],
        out_specs=[tile, tile],
        out_shape=[jax.ShapeDtypeStruct((n, POOL_WIDTH), BF16), jax.ShapeDtypeStruct((n, POOL_WIDTH), BF16)],
        scratch_shapes=[pltpu.VMEM((t + POOL_HALO, POOL_WIDTH), F32)],
        compiler_params=_params("parallel"),
    )(proj, proj, pool_w, pool_scale)


def _pool_bwd(dmixed, pooled, pool_w, pool_scale, *, name):
    n = dmixed.shape[0]
    t = _tile(n, 1024, POOL_HALO)
    per = t // POOL_HALO
    nb = n // t

    def body(dm_ref, dmh_ref, p_ref, pw_ref, ps_ref, du_ref, dpw_ref, dps_ref, ext):
        i = pl.program_id(0)

        @pl.when(i == 0)
        def _():
            dpw_ref[...] = jnp.zeros_like(dpw_ref)
            dps_ref[...] = jnp.zeros_like(dps_ref)

        dm, dmh = dm_ref[...], dmh_ref[...]
        pos = i * t + lax.broadcasted_iota(jnp.int32, (t, POOL_CH), 0) + 1
        for grp, win in enumerate(POOL_WINDOWS):
            cols = slice(grp * POOL_CH, (grp + 1) * POOL_CH)
            pwb = pw_ref[grp].astype(BF16)
            pb = p_ref[:, cols]
            scale = ps_ref[:, cols]
            dps_ref[:, cols] += _colsum(dm[:, cols] * _dot(pb, pwb))
            dpm = (dm[:, cols] * scale).astype(BF16)
            dpw_ref[grp] += _dot_tn(pb, dpm)
            dpool = _dot_nt(dpm, pwb)
            dpool_next = _dot_nt((dmh[:, cols] * scale).astype(BF16), pwb)
            ext[0:t, cols] = dpool / jnp.minimum(pos, win).astype(F32)
            ext[t:t + POOL_HALO, cols] = jnp.where(i < nb - 1, dpool_next * (1.0 / win), 0.0)
            acc = -dpool
            for j in range(win):
                acc = acc + ext[j:j + t, cols]
            du_ref[:, cols] = acc.astype(du_ref.dtype)

    tile = pl.BlockSpec((t, POOL_WIDTH), lambda i: (i, 0))
    return pl.pallas_call(
        body, name=name, grid=(nb,),
        in_specs=[tile, pl.BlockSpec((POOL_HALO, POOL_WIDTH), lambda i: (jnp.minimum((i + 1) * per, nb * per - 1), 0)),
                  tile, pl.BlockSpec((len(POOL_WINDOWS), POOL_CH, POOL_CH), lambda i: (0, 0, 0)),
                  pl.BlockSpec((1, POOL_WIDTH), lambda i: (0, 0))],
        out_specs=[tile, pl.BlockSpec((len(POOL_WINDOWS), POOL_CH, POOL_CH), lambda i: (0, 0, 0)),
                   pl.BlockSpec((1, POOL_WIDTH), lambda i: (0, 0))],
        out_shape=[jax.ShapeDtypeStruct((n, POOL_WIDTH), BF16),
                   jax.ShapeDtypeStruct((len(POOL_WINDOWS), POOL_CH, POOL_CH), F32),
                   jax.ShapeDtypeStruct((1, POOL_WIDTH), F32)],
        scratch_shapes=[pltpu.VMEM((t + POOL_HALO, POOL_WIDTH), F32)],
        compiler_params=_params("arbitrary"),
    )(dmixed, dmixed, pooled, pool_w, pool_scale)


def _adamw(w, g, m, v):
    m2 = ADAM_B1 * m + (1.0 - ADAM_B1) * g
    v2 = ADAM_B2 * v + (1.0 - ADAM_B2) * (g * g)
    m_hat = m2 * (1.0 / (1.0 - ADAM_B1 ** ADAM_STEP))
    v_hat = v2 * (1.0 / (1.0 - ADAM_B2 ** ADAM_STEP))
    delta = -ADAM_LR * (m_hat / (jnp.sqrt(v_hat) + ADAM_EPS) + ADAM_WD * w)
    return delta, m2, v2


def _adam_big(recv, w, m, v, *, name):
    r, c = w.shape
    tr = _tile(r, 256, 16)

    def body(recv_ref, w_ref, m_ref, v_ref, g_ref, d_ref, m2_ref, v2_ref):
        g = recv_ref[0].astype(F32)
        for i in range(1, N_DEV):
            g = g + recv_ref[i].astype(F32)
        delta, m2, v2 = _adamw(w_ref[...], g, m_ref[...], v_ref[...])
        g_ref[...] = g
        d_ref[...] = delta
        m2_ref[...] = m2
        v2_ref[...] = v2

    tile = pl.BlockSpec((tr, c), lambda i: (i, 0))
    out = jax.ShapeDtypeStruct((r, c), F32)
    return pl.pallas_call(
        body, name=name, grid=(r // tr,),
        in_specs=[pl.BlockSpec((N_DEV, tr, c), lambda i: (0, i, 0)), tile, tile, tile],
        out_specs=[tile] * 4, out_shape=[out] * 4, compiler_params=_params("parallel"),
    )(recv, w, m, v)


def _adam_small(parts, w, m, v, *, name):
    n = len(SMALL_PARAMS)

    def body(*refs):
        parts_r, w_r, m_r, v_r = (refs[i * n:(i + 1) * n] for i in range(4))
        outs = refs[4 * n:]
        for j, key in enumerate(SMALL_PARAMS):
            g = parts_r[j][0]
            for i in range(1, N_DEV):
                g = g + parts_r[j][i]
            w_ = w_r[j][...]
            if key == "hgrn_lb":
                s0 = 1.0 / (1.0 + jnp.exp(w_[1:2] - w_[0:1]))
                ga = g * s0 * (1.0 - s0)
                sign = jnp.where(lax.broadcasted_iota(jnp.int32, w_.shape, 0) == 0, 1.0, -1.0)
                g = sign * jnp.broadcast_to(ga, w_.shape)
            delta, m2, v2 = _adamw(w_, g, m_r[j][...], v_r[j][...])
            for q, val in enumerate((g, delta, m2, v2)):
                outs[q * n + j][...] = val

    out_shape = [jax.ShapeDtypeStruct(w[k].shape, F32) for _ in range(4) for k in SMALL_PARAMS]
    res = pl.pallas_call(body, name=name, out_shape=out_shape, compiler_params=_params())(
        *[t[k] for t in (parts, w, m, v) for k in SMALL_PARAMS])
    return [res[q * n:(q + 1) * n] for q in range(4)]


def _as_2d(a):
    return a.reshape(-1, a.shape[-1])


SPLIT_AXIS = dict(BIG_WEIGHTS)


def _gather_of(names, weights):
    return _Exchange([_shard_to_send(weights[k][0], SPLIT_AXIS[k]) for k in names], gather=True)


def _scatter_of(names, dfull):
    return _Exchange([_to_slots(dfull[k], SPLIT_AXIS[k]) for k in names], gather=False)


def _shard_to_send(w, axis):
    return (w.T if axis == 1 else w).astype(BF16)


def _to_slots(dw, axis):
    rows, cols = dw.shape
    return dw.reshape(N_DEV, rows // N_DEV, cols)


def _from_slots(gathered, axis):
    _, r, c = gathered.shape
    return gathered.reshape(N_DEV * r, c)


def kernel(x, p, ffn1_norm, ffn1_w1, ffn1_w3, ffn1_w2, mix_norm, w_in, hgrn_lb, hgrn_onorm, w_branch_a, pool_w, pool_scale, w_branch_b, w_out, ffn2_norm, ffn2_w1, ffn2_w3, ffn2_w2, ple_norm, ple_w_gate, ple_w_proj, ple_post_norm, final_norm, loss_target, m_ffn1_norm, m_ffn1_w1, m_ffn1_w3, m_ffn1_w2, m_mix_norm, m_w_in, m_hgrn_lb, m_hgrn_onorm, m_w_branch_a, m_pool_w, m_pool_scale, m_w_branch_b, m_w_out, m_ffn2_norm, m_ffn2_w1, m_ffn2_w3, m_ffn2_w2, m_ple_norm, m_ple_w_gate, m_ple_w_proj, m_ple_post_norm, m_final_norm, v_ffn1_norm, v_ffn1_w1, v_ffn1_w3, v_ffn1_w2, v_mix_norm, v_w_in, v_hgrn_lb, v_hgrn_onorm, v_w_branch_a, v_pool_w, v_pool_scale, v_w_branch_b, v_w_out, v_ffn2_norm, v_ffn2_w1, v_ffn2_w3, v_ffn2_w2, v_ple_norm, v_ple_w_gate, v_ple_w_proj, v_ple_post_norm, v_final_norm):
    weights = dict(ffn1_norm=ffn1_norm, ffn1_w1=ffn1_w1, ffn1_w3=ffn1_w3, ffn1_w2=ffn1_w2, mix_norm=mix_norm, w_in=w_in, hgrn_lb=hgrn_lb, hgrn_onorm=hgrn_onorm, w_branch_a=w_branch_a, pool_w=pool_w, pool_scale=pool_scale, w_branch_b=w_branch_b, w_out=w_out, ffn2_norm=ffn2_norm, ffn2_w1=ffn2_w1, ffn2_w3=ffn2_w3, ffn2_w2=ffn2_w2, ple_norm=ple_norm, ple_w_gate=ple_w_gate, ple_w_proj=ple_w_proj, ple_post_norm=ple_post_norm, final_norm=final_norm)
    mom1 = dict(ffn1_norm=m_ffn1_norm, ffn1_w1=m_ffn1_w1, ffn1_w3=m_ffn1_w3, ffn1_w2=m_ffn1_w2, mix_norm=m_mix_norm, w_in=m_w_in, hgrn_lb=m_hgrn_lb, hgrn_onorm=m_hgrn_onorm, w_branch_a=m_w_branch_a, pool_w=m_pool_w, pool_scale=m_pool_scale, w_branch_b=m_w_branch_b, w_out=m_w_out, ffn2_norm=m_ffn2_norm, ffn2_w1=m_ffn2_w1, ffn2_w3=m_ffn2_w3, ffn2_w2=m_ffn2_w2, ple_norm=m_ple_norm, ple_w_gate=m_ple_w_gate, ple_w_proj=m_ple_w_proj, ple_post_norm=m_ple_post_norm, final_norm=m_final_norm)
    mom2 = dict(ffn1_norm=v_ffn1_norm, ffn1_w1=v_ffn1_w1, ffn1_w3=v_ffn1_w3, ffn1_w2=v_ffn1_w2, mix_norm=v_mix_norm, w_in=v_w_in, hgrn_lb=v_hgrn_lb, hgrn_onorm=v_hgrn_onorm, w_branch_a=v_w_branch_a, pool_w=v_pool_w, pool_scale=v_pool_scale, w_branch_b=v_w_branch_b, w_out=v_w_out, ffn2_norm=v_ffn2_norm, ffn2_w1=v_ffn2_w1, ffn2_w3=v_ffn2_w3, ffn2_w2=v_ffn2_w2, ple_norm=v_ple_norm, ple_w_gate=v_ple_w_gate, ple_w_proj=v_ple_w_proj, ple_post_norm=v_ple_post_norm, final_norm=v_final_norm)

    xs = x[0]
    ps = p[0, 0].astype(BF16)
    tgt = loss_target[0]
    n = xs.shape[0]

    g_f1, g_mix, g_on, g_f2 = ffn1_norm, mix_norm, hgrn_onorm, ffn2_norm
    g_ple, g_post, g_fin = ple_norm, ple_post_norm, final_norm.reshape(1, D_MODEL)
    lb2 = hgrn_lb
    pw, pscale = pool_w[0], pool_scale

    full = {}

    def keep(names, gathered):
        for k, g in zip(names, gathered):
            full[k] = _from_slots(g, SPLIT_AXIS[k])

    names = ("ffn1_w1", "ffn1_w3")
    ex = _gather_of(names, weights)
    h1 = _rms_fwd(xs, g_f1, name="ffn1_rms", exchange=ex)
    keep(names, ex.received)
    names = ("ffn1_w2", "w_in")
    ex = _gather_of(names, weights)
    a1, b1, s1 = _ffn_up(h1, full["ffn1_w1"], full["ffn1_w3"], name="ffn1_up", exchange=ex)
    keep(names, ex.received)
    names = ("w_branch_a", "w_branch_b", "w_out")
    ex = _gather_of(names, weights)
    x1, h2 = _mm_nn_res_rms(s1, full["ffn1_w2"], xs, g_mix, name="ffn1_down", scale=0.5, exchange=ex)
    keep(names, ex.received)
    names = ("ffn2_w1", "ffn2_w3", "ffn2_w2", "ple_w_gate", "ple_w_proj")
    ex = _gather_of(names, weights)
    proj = _mm_nn_wide(h2, full["w_in"], name="w_in_proj", out_dtype=BF16, tn=512, exchange=ex)
    keep(names, ex.received)
    o, states = _hgrn_fwd(proj, lb2, name="hgrn_fwd")
    on = _hgrn_post_fwd(o, proj, g_on, name="hgrn_post_fwd")
    ya = _mm_nn_wide(on, full["w_branch_a"], name="branch_a", out_dtype=BF16, b_is_km=True)
    pooled, mixed = _pool_fwd(proj, pw, pscale, name="pool_fwd")
    yb = _mm_nn_wide(mixed, full["w_branch_b"], name="branch_b", out_dtype=BF16)
    y = _merge_fwd(proj, ya, yb, name="merge_fwd")
    x2, h3 = _mm_nn_res_rms(y, full["w_out"], x1, g_f2, name="w_out_proj", scale=1.0)
    a2, b2, s2 = _ffn_up(h3, full["ffn2_w1"], full["ffn2_w3"], name="ffn2_up")
    x3, h4 = _mm_nn_res_rms(s2, full["ffn2_w2"], x2, g_ple, name="ffn2_down", scale=0.5)
    gpre = _mm_nn_wide(h4, full["ple_w_gate"], name="ple_gate", out_dtype=BF16, b_is_km=True)
    z = _mm_nn_wide(ps, full["ple_w_proj"], name="ple_proj", out_dtype=BF16)
    dx4, dpre, dz, loss_part, d_fin, d_post = _ple_final(x3, gpre, z, tgt, g_post, g_fin, name="ple_final")

    dfull, received = {}, {}

    def sent(names, exchange):
        received.update(zip(names, exchange.received))

    dfull["ple_w_proj"] = _mm_tn(ps, dz, name="d_ple_w_proj", tn=1024, tm=1024, transpose_out=True)
    dfull["ple_w_gate"] = _mm_tn(h4, dpre, name="d_ple_w_gate", tn=1024, tm=1024)
    dx3, dx3s, d_ple = _mm_nt_rms_bwd([(dpre, full["ple_w_gate"], False)], x3, dx4, g_ple, name="ple_rms_bwd", tn=1024,
                                      half_scale=0.5)

    names = ("ple_w_proj", "ple_w_gate")
    ex = _scatter_of(names, dfull)
    da2, db2 = _ffn_bwd_mid(dx3s, full["ffn2_w2"], a2, b2, name="ffn2_bwd_mid", exchange=ex)
    sent(names, ex)
    dfull["ffn2_w2"] = _mm_tn(s2, dx3s, name="ffn2_dw2", tn=1024, tm=1024)
    dfull["ffn2_w1"] = _mm_tn(h3, da2, name="ffn2_dw1", tn=1024, tm=2816, transpose_out=True)
    dfull["ffn2_w3"] = _mm_tn(h3, db2, name="ffn2_dw3", tn=1024, tm=2816, transpose_out=True)
    names = ("ffn2_w2",)
    ex = _scatter_of(names, dfull)
    dx2, dx2b, d_f2 = _mm_nt_rms_bwd([(da2, full["ffn2_w1"], True), (db2, full["ffn2_w3"], True)], x2, dx3, g_f2,
                                     name="ffn2_rms_bwd", tn=512, half_scale=1.0, exchange=ex)
    sent(names, ex)

    dfull["w_out"] = _mm_tn(y, dx2b, name="d_w_out", tn=1024, tm=1024)
    dy = _mm_nn_wide(dx2b, full["w_out"], name="d_y", out_dtype=BF16)
    dya, dyb, dga, dgb = _merge_bwd(dy, proj, ya, yb, name="merge_bwd")

    dfull["w_branch_b"] = _mm_tn(mixed, dyb, name="d_w_branch_b", tn=1024, tm=1024, transpose_out=True)
    dmixed = _mm_nn_wide(dyb, full["w_branch_b"], name="d_mixed", out_dtype=F32, b_is_km=True)
    du, d_pw, d_ps = _pool_bwd(dmixed, pooled, pw, pscale, name="pool_bwd")

    dfull["w_branch_a"] = _mm_tn(on, dya, name="d_w_branch_a", tn=1024, tm=1024)
    don = _mm_nn_wide(dya, full["w_branch_a"], name="d_on", out_dtype=BF16)
    dog, do, d_on = _hgrn_post_bwd(don, o, proj, g_on, name="hgrn_post_bwd")
    names = ("ffn2_w1", "ffn2_w3", "w_out", "w_branch_b", "w_branch_a")
    ex = _scatter_of(names, dfull)
    dqfi, d_lb = _hgrn_bwd(proj, lb2, do, states, name="hgrn_bwd", exchange=ex)
    sent(names, ex)
    dproj = [dqfi, dog, du, dga, dgb]
    dfull["w_in"] = jnp.concatenate(
        [_mm_tn(h2, part, name=f"d_w_in_{j}", tn=1024, tm=3072, transpose_out=True) for j, part in enumerate(dproj)],
        axis=0)
    names = ("w_in",)
    ex = _scatter_of(names, dfull)
    dx1, dx1s, d_mix = _mm_nt_rms_bwd([(dproj, full["w_in"], True)], x1, dx2, g_mix, name="mix_rms_bwd", tn=512,
                                      half_scale=0.5, exchange=ex)
    sent(names, ex)

    da1, db1 = _ffn_bwd_mid(dx1s, full["ffn1_w2"], a1, b1, name="ffn1_bwd_mid")
    dfull["ffn1_w2"] = _mm_tn(s1, dx1s, name="ffn1_dw2", tn=1024, tm=1024)
    names = ("ffn1_w2",)
    ex = _scatter_of(names, dfull)
    dfull["ffn1_w1"] = _mm_tn(h1, da1, name="ffn1_dw1", tn=1024, tm=2816, transpose_out=True, exchange=ex)
    sent(names, ex)
    names = ("ffn1_w1",)
    ex = _scatter_of(names, dfull)
    dfull["ffn1_w3"] = _mm_tn(h1, db1, name="ffn1_dw3", tn=1024, tm=2816, transpose_out=True, exchange=ex)
    sent(names, ex)
    names = ("ffn1_w3",)
    ex = _scatter_of(names, dfull)
    grad_x, _, d_f1 = _mm_nt_rms_bwd([(da1, full["ffn1_w1"], True), (db1, full["ffn1_w3"], True)], xs, dx1, g_f1,
                                     name="ffn1_rms_bwd", tn=512, half_scale=1.0, exchange=ex)
    sent(names, ex)

    small_part = dict(ffn1_norm=d_f1, mix_norm=d_mix, hgrn_onorm=d_on, ffn2_norm=d_f2, ple_norm=d_ple,
                      ple_post_norm=d_post, final_norm=d_fin, hgrn_lb=d_lb, pool_scale=d_ps, pool_w=_as_2d(d_pw))
    gathered_small = _exchange_now([small_part[k] for k in SMALL_PARAMS] + [loss_part], name="gather_small_grads",
                                   gather=True)
    small_all = dict(zip(SMALL_PARAMS, gathered_small))
    loss = jnp.sum(gathered_small[-1])

    grads, deltas, new_m, new_v = {}, {}, {}, {}
    for name, axis in BIG_WEIGHTS:
        shape, recv = weights[name].shape, received[name]
        own = [t[name][0].T if axis == 1 else t[name][0] for t in (weights, mom1, mom2)]
        res = _adam_big(recv, *own, name=f"adam_{name}")
        grads[name], deltas[name], new_m[name], new_v[name] = [(r.T if axis == 1 else r).reshape(shape) for r in res]
    res = _adam_small(small_all, *[{k: _as_2d(t[k]) for k in SMALL_PARAMS} for t in (weights, mom1, mom2)],
                      name="adam_small")
    for store, vals in zip((grads, deltas, new_m, new_v), res):
        store.update({k: val.reshape(weights[k].shape) for k, val in zip(SMALL_PARAMS, vals)})

    return (loss, grad_x.reshape(x.shape), *[grads[k] for k in WEIGHT_ORDER], *[deltas[k] for k in WEIGHT_ORDER],
            *[new_m[k] for k in WEIGHT_ORDER], *[new_v[k] for k in WEIGHT_ORDER])
```

```python
import jax
import jax.numpy as jnp
from jax import lax
from jax.experimental import pallas as pl
from jax.experimental.pallas import tpu as pltpu

F32 = jnp.float32
BF16 = jnp.bfloat16

N_DEV = 8
D_MODEL = 1024
HEADS = 8
HEAD_DIM = 128
POOL_WINDOWS = (2, 4, 8, 16)
POOL_CH = 128
POOL_WIDTH = 512
POOL_HALO = 16
RMS_EPS = 1e-6
CHUNK = 64
SUB = 32
HGRN_HEADS_PER_STEP = 8
NEG_BIG = -1e30

ADAM_LR = 0.001
ADAM_B1 = 0.9
ADAM_B2 = 0.999
ADAM_EPS = 1e-08
ADAM_WD = 0.01
ADAM_STEP = 10

V7X_VMEM_BYTES = 64 * 1024 * 1024
VMEM_LIMIT = (V7X_VMEM_BYTES * 3) // 4
EXCHANGE_TAIL_STEPS = 3
ROW_TILE_CAP = 8192

COL_Q, COL_F, COL_I, COL_OG, COL_POOL, COL_GA, COL_GB = 0, 1024, 2048, 3072, 4096, 4608, 5632

BIG_WEIGHTS = (
    ("ffn1_w1", 1), ("ffn1_w3", 1), ("ffn1_w2", 0), ("w_in", 1), ("w_branch_a", 0), ("w_branch_b", 1),
    ("w_out", 0), ("ffn2_w1", 1), ("ffn2_w3", 1), ("ffn2_w2", 0), ("ple_w_gate", 0), ("ple_w_proj", 1),
)
SMALL_PARAMS = ("ffn1_norm", "mix_norm", "hgrn_onorm", "ffn2_norm", "ple_norm", "ple_post_norm", "final_norm",
                "hgrn_lb", "pool_scale", "pool_w")
WEIGHT_ORDER = (
    "ffn1_norm", "ffn1_w1", "ffn1_w3", "ffn1_w2", "mix_norm", "w_in", "hgrn_lb", "hgrn_onorm", "w_branch_a", "pool_w",
    "pool_scale", "w_branch_b", "w_out", "ffn2_norm", "ffn2_w1", "ffn2_w3", "ffn2_w2", "ple_norm", "ple_w_gate",
    "ple_w_proj", "ple_post_norm", "final_norm",
)


def _params(*sem):
    return pltpu.CompilerParams(dimension_semantics=sem if sem else None, vmem_limit_bytes=VMEM_LIMIT)


COL_CHUNK = 256


def _rows(tn, width):
    return pl.BlockSpec((tn, width), lambda i: (i, 0))


def _resident(shape):
    return pl.BlockSpec(shape, lambda i: (0,) * len(shape), pipeline_mode=pl.Buffered(1))


def _dot(a, b):
    return jnp.dot(a, b, preferred_element_type=F32)


def _dot_nt(a, b):
    return lax.dot_general(a, b, (((1,), (1,)), ((), ())), preferred_element_type=F32)


def _dot_tn(a, b):
    return lax.dot_general(a, b, (((0,), (0,)), ((), ())), preferred_element_type=F32)


def _sigmoid(x):
    return 0.5 * jnp.tanh(0.5 * x) + 0.5


def _tile(n, want, mult):
    if mult != 128:
        want = min(want, ROW_TILE_CAP)
    if n <= want:
        return n
    t = (want // mult) * mult
    while t > mult and n % t:
        t -= mult
    assert n % t == 0, (n, want, mult)
    return t


class _Exchange:
    COPIES = N_DEV - 1

    def __init__(self, arrs, gather):
        self.arrs, self.gather, self.n = list(arrs), gather, len(arrs)
        self.out_shape = [jax.ShapeDtypeStruct((N_DEV,) + (a.shape if gather else a.shape[1:]), a.dtype) for a in arrs]
        self.scratch = [pltpu.SemaphoreType.DMA((self.n * self.COPIES,)),
                        pltpu.SemaphoreType.DMA((self.n * self.COPIES,)), pltpu.SemaphoreType.DMA((self.n,))]
        self.received = None

    @staticmethod
    def _place():
        x, y, c = lax.axis_index("x"), lax.axis_index("y"), lax.axis_index("c")
        return x, y, c

    def _copy(self, a, k, src, dst, to, sems):
        s = a * self.COPIES + k
        return pltpu.make_async_remote_copy(src_ref=src, dst_ref=dst, send_sem=sems[0].at[s], recv_sem=sems[1].at[s],
                                            device_id=to, device_id_type=pl.DeviceIdType.MESH)

    def _gather_copies(self, role, ins, outs, sems):
        x, y, c = self._place()
        chips = [(1 - x, y), (x, 1 - y), (1 - x, 1 - y)]
        sibling = (x, y, 1 - c)

        def slot(px, py, pc):
            return 4 * px + 2 * py + pc

        copies = []
        for a in range(self.n):
            mine = outs[a].at[slot(x, y, c)]
            if role == "first":
                copies.append(self._copy(a, 0, ins[a], mine, sibling, sems))
            elif role == "last":
                copies.append(self._copy(a, 0, ins[a], outs[a].at[slot(x, y, 1 - c)], sibling, sems))
            for j, (px, py) in enumerate(chips):
                theirs = outs[a].at[slot(px, py, c)]
                if role == "first":
                    copies.append(self._copy(a, 1 + j, ins[a], mine, (px, py, c), sems))
                elif role == "landed":
                    copies.append(self._copy(a, 1 + j, ins[a], theirs, (px, py, c), sems))
                elif role == "onward":
                    copies.append(self._copy(a, 4 + j, theirs, theirs, sibling, sems))
                else:
                    copies.append(self._copy(a, 4 + j, ins[a], outs[a].at[slot(px, py, 1 - c)], sibling, sems))
        return copies

    def _scatter_copies(self, role, ins, outs, sems):
        x, y, c = self._place()
        me = 4 * x + 2 * y + c
        copies = []
        for k in range(1, N_DEV):
            px = 1 - x if k & 4 else x
            py = 1 - y if k & 2 else y
            pc = 1 - c if k & 1 else c
            peer = 4 * px + 2 * py + pc
            for a in range(self.n):
                dst = outs[a].at[me] if role == "sends" else outs[a].at[peer]
                copies.append(self._copy(a, k - 1, ins[a].at[peer], dst, (px, py, pc), sems))
        return copies

    def _local(self, ins, outs, sems):
        x, y, c = self._place()
        me = 4 * x + 2 * y + c
        return [pltpu.make_async_copy(ins[a] if self.gather else ins[a].at[me], outs[a].at[me], sems[2].at[a])
                for a in range(self.n)]

    def start(self, ins, outs, sems):
        for cp in self._local(ins, outs, sems):
            cp.start()
        sends = (self._gather_copies("first", ins, outs, sems) if self.gather
                 else self._scatter_copies("sends", ins, outs, sems))
        for cp in sends:
            cp.start()

    def pass_on(self, ins, outs, sems):
        if self.gather:
            for landed, onward in zip(self._gather_copies("landed", ins, outs, sems),
                                      self._gather_copies("onward", ins, outs, sems)):
                landed.wait_recv()
                onward.start()

    def finish(self, ins, outs, sems):
        if self.gather:
            arrivals = self._gather_copies("last", ins, outs, sems)
            sends = self._gather_copies("first", ins, outs, sems) + self._gather_copies("onward", ins, outs, sems)
        else:
            arrivals = self._scatter_copies("arrivals", ins, outs, sems)
            sends = self._scatter_copies("sends", ins, outs, sems)
        for cp in arrivals:
            cp.wait_recv()
        for cp in sends:
            cp.wait_send()
        for cp in self._local(ins, outs, sems):
            cp.wait()


def _call(body, *, name, grid, in_specs, out_specs, out_shape, args, semantics, scratch=(), exchange=None):
    if exchange is None:
        return pl.pallas_call(
            body, name=name, grid=grid, in_specs=in_specs, out_specs=out_specs, out_shape=out_shape,
            scratch_shapes=list(scratch), compiler_params=_params(*semantics))(*args)
    ex = exchange
    n_in, n_out, n_s = len(in_specs), len(out_specs), len(scratch)

    def wrapped(*refs):
        ins, ex_in = refs[:n_in], refs[n_in:n_in + ex.n]
        o0 = n_in + ex.n
        outs, ex_out = refs[o0:o0 + n_out], refs[o0 + n_out:o0 + n_out + ex.n]
        s0 = o0 + n_out + ex.n
        scr, sems = refs[s0:s0 + n_s], refs[s0 + n_s:]
        step, steps = pl.program_id(0), grid[0]
        for ax in range(1, len(grid)):
            step, steps = step * grid[ax] + pl.program_id(ax), steps * grid[ax]

        @pl.when(step == 0)
        def _():
            ex.start(ex_in, ex_out, sems)

        body(*ins, *outs, *scr)

        @pl.when(step == max(steps - EXCHANGE_TAIL_STEPS, 0))
        def _():
            ex.pass_on(ex_in, ex_out, sems)

        @pl.when(step == steps - 1)
        def _():
            ex.finish(ex_in, ex_out, sems)

    hbm = pl.BlockSpec(memory_space=pltpu.HBM)
    res = pl.pallas_call(
        wrapped, name=name, grid=grid, in_specs=list(in_specs) + [hbm] * ex.n,
        out_specs=list(out_specs) + [hbm] * ex.n, out_shape=list(out_shape) + ex.out_shape,
        scratch_shapes=list(scratch) + ex.scratch, compiler_params=_params(*(["arbitrary"] * len(grid))),
    )(*args, *ex.arrs)
    ex.received = res[n_out:]
    return res[:n_out]


def _exchange_now(arrs, *, name, gather):
    ex = _Exchange(arrs, gather)
    n = ex.n

    def body(*refs):
        ex.start(refs[:n], refs[n:2 * n], refs[2 * n:])
        ex.pass_on(refs[:n], refs[n:2 * n], refs[2 * n:])
        ex.finish(refs[:n], refs[n:2 * n], refs[2 * n:])

    hbm = pl.BlockSpec(memory_space=pltpu.HBM)
    return pl.pallas_call(body, name=name, out_shape=ex.out_shape, in_specs=[hbm] * n, out_specs=[hbm] * n,
                          scratch_shapes=ex.scratch)(*arrs)


def _mm_tn(a, b, *, name, tn, tm, transpose_out=False, exchange=None):
    n, k = a.shape
    m = b.shape[1]
    tn, tm = _tile(n, tn, 16), _tile(m, tm, 128)
    steps = n // tn

    def body(a_ref, b_ref, o_ref, acc):
        i = pl.program_id(1)

        @pl.when(i == 0)
        def _():
            acc[...] = jnp.zeros_like(acc)

        acc[...] += _dot_tn(a_ref[...], b_ref[...])

        @pl.when(i == steps - 1)
        def _():
            for c0 in range(0, tm, COL_CHUNK):
                cols = slice(c0, min(c0 + COL_CHUNK, tm))
                if transpose_out:
                    o_ref[cols, :] = acc[:, cols].T.astype(o_ref.dtype)
                else:
                    o_ref[:, cols] = acc[:, cols].astype(o_ref.dtype)

    if transpose_out:
        out_spec, out_shape = pl.BlockSpec((tm, k), lambda j, i: (j, 0)), jax.ShapeDtypeStruct((m, k), BF16)
    else:
        out_spec, out_shape = pl.BlockSpec((k, tm), lambda j, i: (0, j)), jax.ShapeDtypeStruct((k, m), BF16)
    return _call(body, name=name, grid=(m // tm, steps),
                 in_specs=[pl.BlockSpec((tn, k), lambda j, i: (i, 0)), pl.BlockSpec((tn, tm), lambda j, i: (i, j))],
                 out_specs=[out_spec], out_shape=[out_shape], args=[a, b], semantics=("parallel", "arbitrary"),
                 scratch=[pltpu.VMEM((k, tm), F32)], exchange=exchange)[0]


def _ffn_up(h, w1, w3, *, name, tn=512, exchange=None):
    n, k = h.shape
    m = w1.shape[0]
    tn = _tile(n, tn, 16)

    def body(h_ref, w1_ref, w3_ref, dsda_ref, dsdb_ref, s_ref):
        for c0 in range(0, m, COL_CHUNK):
            cols = slice(c0, c0 + COL_CHUNK)
            a = _dot_nt(h_ref[...], w1_ref[cols, :])
            b = _dot_nt(h_ref[...], w3_ref[cols, :])
            sg = _sigmoid(a)
            silu = a * sg
            dsda_ref[:, cols] = (b * (sg + silu * (1.0 - sg))).astype(dsda_ref.dtype)
            dsdb_ref[:, cols] = silu.astype(dsdb_ref.dtype)
            s_ref[:, cols] = (silu * b).astype(s_ref.dtype)

    ospec = _rows(tn, m)
    return _call(body, name=name, grid=(n // tn,),
                 in_specs=[_rows(tn, k), _resident(w1.shape), _resident(w3.shape)], out_specs=[ospec, ospec, ospec],
                 out_shape=[jax.ShapeDtypeStruct((n, m), BF16)] * 3,
                 args=[h, w1, w3], semantics=("parallel",), exchange=exchange)


def _mm_nn_wide(a, b, *, name, out_dtype, b_is_km=False, tn=1024, exchange=None):
    n, k = a.shape
    m = b.shape[1 if b_is_km else 0]
    tn = _tile(n, tn, 16)
    chunk = min(2 * COL_CHUNK, m)

    def body(a_ref, b_ref, o_ref):
        for c0 in range(0, m, chunk):
            cols = slice(c0, c0 + chunk)
            if b_is_km:
                res = _dot(a_ref[...], b_ref[:, cols])
            else:
                res = _dot_nt(a_ref[...], b_ref[cols, :])
            o_ref[:, cols] = res.astype(o_ref.dtype)

    return _call(body, name=name, grid=(n // tn,), in_specs=[_rows(tn, k), _resident(b.shape)],
                 out_specs=[_rows(tn, m)], out_shape=[jax.ShapeDtypeStruct((n, m), out_dtype)], args=[a, b],
                 semantics=("parallel",), exchange=exchange)[0]


def _mm_nn_res_rms(a, b, res, g, *, name, scale, exchange=None):
    n, k = a.shape
    d = b.shape[1]
    tn = _tile(n, 1024, 16)

    def body(a_ref, b_ref, r_ref, g_ref, x_ref, h_ref):
        for c0 in range(0, d, COL_CHUNK):
            cols = slice(c0, c0 + COL_CHUNK)
            x_ref[:, cols] = r_ref[:, cols] + scale * _dot(a_ref[...], b_ref[:, cols])
        x = x_ref[...]
        r = lax.rsqrt(_rowmean(x * x) + RMS_EPS)
        h_ref[...] = (x * r * g_ref[...]).astype(h_ref.dtype)

    row = _rows(tn, d)
    return _call(body, name=name, grid=(n // tn,),
                 in_specs=[_rows(tn, k), _resident(b.shape), row, pl.BlockSpec((1, d), lambda i: (0, 0))],
                 out_specs=[row, row],
                 out_shape=[jax.ShapeDtypeStruct((n, d), F32), jax.ShapeDtypeStruct((n, d), BF16)],
                 args=[a, b, res, g], semantics=("parallel",), exchange=exchange)


def _ffn_bwd_mid(dxs, w2, dsda, dsdb, *, name, exchange=None):
    n, d = dxs.shape
    m = w2.shape[0]
    tn = _tile(n, 512, 16)

    def body(dx_ref, w2_ref, dsda_ref, dsdb_ref, da_ref, db_ref):
        for c0 in range(0, m, COL_CHUNK):
            cols = slice(c0, c0 + COL_CHUNK)
            ds = _dot_nt(dx_ref[...], w2_ref[cols, :])
            da_ref[:, cols] = (ds * dsda_ref[:, cols].astype(F32)).astype(da_ref.dtype)
            db_ref[:, cols] = (ds * dsdb_ref[:, cols].astype(F32)).astype(db_ref.dtype)

    tile = _rows(tn, m)
    return _call(body, name=name, grid=(n // tn,),
                 in_specs=[_rows(tn, d), _resident(w2.shape), tile, tile], out_specs=[tile, tile],
                 out_shape=[jax.ShapeDtypeStruct((n, m), BF16), jax.ShapeDtypeStruct((n, m), BF16)],
                 args=[dxs, w2, dsda, dsdb], semantics=("parallel",), exchange=exchange)


def _rowwise(fn, *, name, n, tn, ncol, rows, vecs, outs, accs=(), exchange=None):
    tn = _tile(n, tn, 16)
    nr, nv, no = len(rows), len(vecs), len(outs)

    def body(*refs):
        first = pl.program_id(1) == 0
        vals = [r[...].astype(F32) for r in refs[:nr + nv]]
        res = fn(*vals)
        for ref, val in zip(refs[nr + nv:nr + nv + no], res[:no]):
            ref[...] = val.astype(ref.dtype)
        for ref, val in zip(refs[nr + nv + no:], res[no:]):
            _accumulate(ref, val, first)

    in_specs = [pl.BlockSpec((tn, w), lambda j, i, c0=c0: (i, c0 + j)) for _, w, c0 in rows]
    in_specs += [pl.BlockSpec((1, w), lambda j, i, c0=c0: (0, c0 + j)) for _, w, c0 in vecs]
    out_specs = [pl.BlockSpec((tn, w), lambda j, i: (i, j)) for _, w, _ in outs]
    out_specs += [pl.BlockSpec((1, w), lambda j, i: (0, j)) for _, w in accs]
    out_shape = [jax.ShapeDtypeStruct((n, tw), dt) for tw, _, dt in outs]
    out_shape += [jax.ShapeDtypeStruct((1, tw), F32) for tw, _ in accs]
    return _call(body, name=name, grid=(ncol, n // tn), in_specs=in_specs, out_specs=out_specs, out_shape=out_shape,
                 args=[r[0] for r in rows] + [v[0] for v in vecs], semantics=("parallel", "arbitrary"),
                 exchange=exchange)


def _accumulate(ref, val, first):
    @pl.when(first)
    def _():
        ref[...] = jnp.zeros_like(ref)

    ref[...] += val


def _colsum(x):
    return jnp.sum(x, axis=0, keepdims=True)


def _rowmean(x):
    return jnp.mean(x, axis=-1, keepdims=True)


def _rms_fwd(x, g, *, name, exchange=None):
    def fn(x_, g_):
        r = lax.rsqrt(_rowmean(x_ * x_) + RMS_EPS)
        return (x_ * r * g_,)

    n, d = x.shape
    return _rowwise(fn, name=name, n=n, tn=512, ncol=1, rows=[(x, d, 0)], vecs=[(g, d, 0)], outs=[(d, d, BF16)],
                    exchange=exchange)[0]


def _mm_nt_rms_bwd(pairs, x, extra, g, *, name, tn, half_scale, exchange=None):
    n, d = x.shape
    tn = _tile(n, tn, 16)
    pairs = [(list(a) if isinstance(a, (list, tuple)) else [a], b, t) for a, b, t in pairs]
    nref = sum(len(a) + 1 for a, _, _ in pairs)

    def body(*refs):
        x_ref, e_ref, g_ref, dx_ref, dxs_ref, dg_ref = refs[nref:]
        dh, at = None, 0
        for parts, _, transposed in pairs:
            b_ref = refs[at + len(parts)]
            col = 0
            for j, part in enumerate(parts):
                w = part.shape[1]
                if transposed:
                    term = _dot(refs[at + j][...], b_ref[col:col + w, :])
                else:
                    term = _dot_nt(refs[at + j][...], b_ref[:, col:col + w])
                dh = term if dh is None else dh + term
                col += w
            at += len(parts) + 1
        x_ = x_ref[...]
        r = lax.rsqrt(_rowmean(x_ * x_) + RMS_EPS)
        xh = x_ * r
        dxh = dh * g_ref[...]
        dx = e_ref[...] + r * (dxh - xh * _rowmean(dxh * xh))
        dx_ref[...] = dx
        dxs_ref[...] = (dx * half_scale).astype(dxs_ref.dtype)
        _accumulate(dg_ref, _colsum(dh * xh), pl.program_id(0) == 0)

    in_specs, args = [], []
    for parts, b, transposed in pairs:
        assert sum(part.shape[1] for part in parts) == b.shape[0 if transposed else 1]
        in_specs += [pl.BlockSpec((tn, part.shape[1]), lambda i: (i, 0)) for part in parts]
        in_specs.append(pl.BlockSpec(b.shape, lambda i: (0, 0), pipeline_mode=pl.Buffered(1)))
        args += parts + [b]
    row = pl.BlockSpec((tn, d), lambda i: (i, 0))
    vec = pl.BlockSpec((1, d), lambda i: (0, 0))
    return _call(body, name=name, grid=(n // tn,), in_specs=in_specs + [row, row, vec], out_specs=[row, row, vec],
                 out_shape=[jax.ShapeDtypeStruct((n, d), F32), jax.ShapeDtypeStruct((n, d), BF16),
                            jax.ShapeDtypeStruct((1, d), F32)],
                 args=args + [x, extra, g], semantics=("arbitrary",), exchange=exchange)


def _ple_final(x3, gpre, z, tgt, gpp, gf, *, name):
    def fn(x3_, gpre_, z_, tgt_, gpp_, gf_):
        gate = _sigmoid(gpre_)
        rz = lax.rsqrt(_rowmean(z_ * z_) + RMS_EPS)
        zh = z_ * rz
        e = zh * gpp_
        x4 = x3_ + gate * e
        r4 = lax.rsqrt(_rowmean(x4 * x4) + RMS_EPS)
        x4h = x4 * r4
        diff = x4h * gf_ - tgt_
        dout = diff * (1.0 / D_MODEL)
        dxh4 = dout * gf_
        dx4 = r4 * (dxh4 - x4h * _rowmean(dxh4 * x4h))
        dpre = dx4 * e * gate * (1.0 - gate)
        de = dx4 * gate
        dzh = de * gpp_
        dz = rz * (dzh - zh * _rowmean(dzh * zh))
        return dx4, dpre, dz, _colsum(diff * diff) * (0.5 / D_MODEL), _colsum(dout * x4h), _colsum(de * zh)

    n, d = x3.shape
    return _rowwise(fn, name=name, n=n, tn=512, ncol=1, rows=[(x3, d, 0), (gpre, d, 0), (z, d, 0), (tgt, d, 0)],
                    vecs=[(gpp, d, 0), (gf, d, 0)], outs=[(d, d, F32), (d, d, BF16), (d, d, BF16)],
                    accs=[(d, d), (d, d), (d, d)])


def _merge_fwd(proj, ya, yb, *, name):
    def fn(ga, gb, ya_, yb_):
        return (_sigmoid(ga) * ya_ + _sigmoid(gb) * yb_,)

    n = proj.shape[0]
    w = 512
    return _rowwise(fn, name=name, n=n, tn=2048, ncol=D_MODEL // w,
                    rows=[(proj, w, COL_GA // w), (proj, w, COL_GB // w), (ya, w, 0), (yb, w, 0)], vecs=[],
                    outs=[(D_MODEL, w, BF16)])[0]


def _merge_bwd(dy, proj, ya, yb, *, name):
    def fn(dy_, ga, gb, ya_, yb_):
        sa, sb = _sigmoid(ga), _sigmoid(gb)
        return dy_ * sa, dy_ * sb, dy_ * ya_ * sa * (1.0 - sa), dy_ * yb_ * sb * (1.0 - sb)

    n = proj.shape[0]
    w = 512
    return _rowwise(fn, name=name, n=n, tn=2048, ncol=D_MODEL // w,
                    rows=[(dy, w, 0), (proj, w, COL_GA // w), (proj, w, COL_GB // w), (ya, w, 0), (yb, w, 0)],
                    vecs=[], outs=[(D_MODEL, w, BF16)] * 4)


def _head_mean(x):
    return jnp.concatenate(
        [jnp.broadcast_to(jnp.mean(x[:, h * HEAD_DIM:(h + 1) * HEAD_DIM], axis=-1, keepdims=True),
                          (x.shape[0], HEAD_DIM)) for h in range(HEADS)], axis=1)


def _hgrn_post_fwd(o, proj, onorm, *, name):
    def fn(o_, og, gam):
        r = lax.rsqrt(_head_mean(o_ * o_) + RMS_EPS)
        return (o_ * r * gam * (og * _sigmoid(og)),)

    n = o.shape[0]
    w = D_MODEL
    return _rowwise(fn, name=name, n=n, tn=512, ncol=1, rows=[(o, w, 0), (proj, w, COL_OG // w)],
                    vecs=[(onorm, w, 0)], outs=[(D_MODEL, w, BF16)])[0]


def _hgrn_post_bwd(don, o, proj, onorm, *, name):
    def fn(don_, o_, og, gam):
        r = lax.rsqrt(_head_mean(o_ * o_) + RMS_EPS)
        oh = o_ * r
        sg = _sigmoid(og)
        dog = don_ * oh * gam * (sg * (1.0 + og * (1.0 - sg)))
        dn = don_ * (og * sg)
        doh = dn * gam
        do = r * (doh - oh * _head_mean(doh * oh))
        return dog, do, _colsum(dn * oh)

    n = o.shape[0]
    w = D_MODEL
    return _rowwise(fn, name=name, n=n, tn=512, ncol=1, rows=[(don, w, 0), (o, w, 0), (proj, w, COL_OG // w)],
                    vecs=[(onorm, w, 0)], outs=[(D_MODEL, w, BF16), (D_MODEL, w, BF16)], accs=[(D_MODEL, w)])


def _tri_sum(tri, x):
    hi = x.astype(BF16)
    lo = (x - hi.astype(F32)).astype(BF16)
    return _dot(tri, hi) + _dot(tri, lo)


def _lower_bound(lb_ref):
    return 1.0 / (1.0 + jnp.exp(lb_ref[1:2, :] - lb_ref[0:1, :]))


def _hgrn_specs(n, t, reverse):
    nt = n // t
    width = HGRN_HEADS_PER_STEP * HEAD_DIM

    def tok(i):
        return nt - 1 - i if reverse else i

    def sec(col):
        c0 = col // width
        return pl.BlockSpec((t, width), lambda h, i: (tok(i), c0 + h))

    head_tile = pl.BlockSpec((t, width), lambda h, i: (tok(i), h))
    state = pl.BlockSpec((HGRN_HEADS_PER_STEP, t // CHUNK, HEAD_DIM, HEAD_DIM), lambda h, i: (h, tok(i), 0, 0))
    lb = pl.BlockSpec((2, width), lambda h, i: (0, h))
    return sec, head_tile, state, lb


def _hgrn_fwd(proj, hgrn_lb, *, name):
    n = proj.shape[0]
    t = _tile(n, 1024, CHUNK)
    nc = t // CHUNK
    hps = HGRN_HEADS_PER_STEP
    width = hps * HEAD_DIM
    lanes = [slice(h * HEAD_DIM, (h + 1) * HEAD_DIM) for h in range(hps)]
    sec, head_tile, state, lbspec = _hgrn_specs(n, t, False)

    def body(q_ref, f_ref, i_ref, lb_ref, o_ref, st_ref, s_acc, g_s, a_s):
        @pl.when(pl.program_id(1) == 0)
        def _():
            s_acc[...] = jnp.zeros_like(s_acc)

        lb = _lower_bound(lb_ref)
        row = lax.broadcasted_iota(jnp.int32, (CHUNK, CHUNK), 0)
        col = lax.broadcasted_iota(jnp.int32, (CHUNK, CHUNK), 1)
        tril = row >= col
        trilb = jnp.where(tril, 1.0, 0.0).astype(BF16)
        rowk = lax.broadcasted_iota(jnp.int32, (CHUNK, width), 0)

        def chunk(c, carry):
            rows = pl.ds(pl.multiple_of(c * CHUNK, CHUNK), CHUNK)
            qr, fr, v = [r[rows, :].astype(F32) for r in (q_ref, f_ref, i_ref)]
            q = qr * _sigmoid(qr)
            f = lb + (1.0 - lb) * _sigmoid(fr)
            k = 1.0 - f
            g = _tri_sum(trilb, jnp.log(f))
            g_s[...] = g
            st0 = [s_acc[h] for h in range(hps)]
            for h in range(hps):
                st_ref[h, c] = st0[h]
            vb = v.astype(BF16)
            for blk in range(CHUNK // SUB):
                lo, hi = blk * SUB, (blk + 1) * SUB
                gref = g_s[lo - 1:lo, :] if blk else jnp.zeros((1, width), F32)
                qi = (q[lo:hi] * jnp.exp(g[lo:hi] - gref)).astype(BF16)
                ki = (k * jnp.exp(jnp.where(rowk < hi, gref - g, NEG_BIG))).astype(BF16)
                for h, ln in enumerate(lanes):
                    a_s[h, lo:hi, :] = _dot_nt(qi[:, ln], ki[:, ln])
            qeb = (q * jnp.exp(g)).astype(BF16)
            o_ref[rows, :] = jnp.concatenate(
                [_dot(jnp.where(tril, a_s[h], 0.0).astype(BF16), vb[:, ln]) + _dot_nt(qeb[:, ln], st0[h].astype(BF16))
                 for h, ln in enumerate(lanes)], axis=1).astype(o_ref.dtype)
            glast = g_s[CHUNK - 1:CHUNK, :]
            kdb = (k * jnp.exp(glast - g)).astype(BF16)
            dec = jnp.exp(glast)
            for h, ln in enumerate(lanes):
                s_acc[h] = st0[h] * dec[:, ln] + _dot_tn(vb[:, ln], kdb[:, ln])
            return carry

        lax.fori_loop(0, nc, chunk, 0)

    return pl.pallas_call(
        body, name=name, grid=(HEADS // hps, n // t),
        in_specs=[sec(COL_Q), sec(COL_F), sec(COL_I), lbspec], out_specs=[head_tile, state],
        out_shape=[jax.ShapeDtypeStruct((n, D_MODEL), BF16),
                   jax.ShapeDtypeStruct((HEADS, n // CHUNK, HEAD_DIM, HEAD_DIM), F32)],
        scratch_shapes=[pltpu.VMEM((hps, HEAD_DIM, HEAD_DIM), F32), pltpu.VMEM((CHUNK, width), F32),
                        pltpu.VMEM((hps, CHUNK, CHUNK), F32)],
        compiler_params=_params("parallel", "arbitrary"),
    )(proj, proj, proj, hgrn_lb)


def _hgrn_bwd(proj, hgrn_lb, do, states, *, name, exchange=None):
    n = proj.shape[0]
    t = _tile(n, 512, CHUNK)
    nc = t // CHUNK
    hps = HGRN_HEADS_PER_STEP
    width = hps * HEAD_DIM
    lanes = [slice(h * HEAD_DIM, (h + 1) * HEAD_DIM) for h in range(hps)]
    sec, head_tile, state, lbspec = _hgrn_specs(n, t, True)

    def body(q_ref, f_ref, i_ref, lb_ref, do_ref, st_ref, dqfi_ref, dlb_ref, d_acc, g_s, a_s, dq_s,
             dg_s):
        first = pl.program_id(1) == 0

        @pl.when(first)
        def _():
            d_acc[...] = jnp.zeros_like(d_acc)

        lb = _lower_bound(lb_ref)
        row = lax.broadcasted_iota(jnp.int32, (CHUNK, CHUNK), 0)
        col = lax.broadcasted_iota(jnp.int32, (CHUNK, CHUNK), 1)
        tril = row >= col
        trilb = jnp.where(tril, 1.0, 0.0).astype(BF16)
        triub = jnp.where(row <= col, 1.0, 0.0).astype(BF16)
        rowk = lax.broadcasted_iota(jnp.int32, (CHUNK, width), 0)

        def per_head(fn):
            return jnp.concatenate([fn(h, ln) for h, ln in enumerate(lanes)], axis=1)

        def chunk(j, dlb):
            c = nc - 1 - j
            rows = pl.ds(pl.multiple_of(c * CHUNK, CHUNK), CHUNK)
            qr, fr, v, dout = [r[rows, :].astype(F32) for r in (q_ref, f_ref, i_ref, do_ref)]
            sq = _sigmoid(qr)
            q = qr * sq
            sf = _sigmoid(fr)
            f = lb + (1.0 - lb) * sf
            k = 1.0 - f
            g = _tri_sum(trilb, jnp.log(f))
            g_s[...] = g
            st0 = [st_ref[h, c] for h in range(hps)]
            dt = [d_acc[h] for h in range(hps)]
            vb, dob = v.astype(BF16), dout.astype(BF16)
            dtb = [x.astype(BF16) for x in dt]
            st0b = [x.astype(BF16) for x in st0]
            glast = g_s[CHUNK - 1:CHUNK, :]
            eg = jnp.exp(g)
            kdec = jnp.exp(glast - g)
            qeb, kdb = (q * eg).astype(BF16), (k * kdec).astype(BF16)
            aps = [jnp.where(row > col, _dot_nt(dob[:, ln], vb[:, ln]), 0.0) for ln in lanes]
            dov = dout * v
            adiag = per_head(lambda h, ln: jnp.broadcast_to(
                jnp.sum(dov[:, ln], axis=-1, keepdims=True), (CHUNK, HEAD_DIM)))
            dq_inter = per_head(lambda h, ln: _dot(dob[:, ln], st0b[h]))
            dk_inter = per_head(lambda h, ln: _dot(vb[:, ln], dtb[h]))
            dk_st = kdec * dk_inter
            dg = qeb.astype(F32) * dq_inter
            dg_minus = kdb.astype(F32) * dk_inter
            dg = dg - dg_minus
            for blk in range(CHUNK // SUB):
                lo, hi = blk * SUB, (blk + 1) * SUB
                gref = g_s[lo - 1:lo, :] if blk else jnp.zeros((1, width), F32)
                qscale = jnp.exp(g[lo:hi] - gref)
                kscale = jnp.exp(jnp.where(rowk < hi, gref - g, NEG_BIG))
                qi = (q[lo:hi] * qscale).astype(BF16)
                ki = (k * kscale).astype(BF16)
                for h, ln in enumerate(lanes):
                    a_s[h, lo:hi, :] = _dot_nt(qi[:, ln], ki[:, ln])
                apb = [x[lo:hi].astype(BF16) for x in aps]
                from_k = per_head(lambda h, ln: _dot(apb[h], ki[:, ln]))
                from_q = per_head(lambda h, ln: _dot_tn(apb[h], qi[:, ln]))
                dq_s[lo:hi, :] = qscale * from_k
                dg_s[lo:hi, :] = qi.astype(F32) * from_k
                dk_st = dk_st + kscale * from_q
                dg = dg - ki.astype(F32) * from_q
            dg = dg + dg_s[...]
            dv = per_head(lambda h, ln: _dot_tn(jnp.where(tril, a_s[h], 0.0).astype(BF16), dob[:, ln])
                          + _dot_nt(kdb[:, ln], dtb[h]))
            dq_st = dq_s[...] + eg * dq_inter
            dq = dq_st + adiag * k
            dk = dk_st + adiag * q
            dec = jnp.exp(glast)
            dt_dec = [dt[h] * dec[:, ln] for h, ln in enumerate(lanes)]
            for h, ln in enumerate(lanes):
                d_acc[h] = dt_dec[h] + _dot_tn(dob[:, ln], qeb[:, ln])
            later = per_head(lambda h, ln: _colsum(dt_dec[h] * st0[h])) + _colsum(dg_minus)
            dlf = later + _tri_sum(triub, dg)
            df = dlf / f - dk
            dqfi_ref[rows, 0:width] = (dq * (sq * (1.0 + qr * (1.0 - sq)))).astype(dqfi_ref.dtype)
            dqfi_ref[rows, width:2 * width] = (df * (1.0 - lb) * sf * (1.0 - sf)).astype(dqfi_ref.dtype)
            dqfi_ref[rows, 2 * width:3 * width] = dv.astype(dqfi_ref.dtype)
            return dlb + _colsum(df * (1.0 - sf))

        dlb = lax.fori_loop(0, nc, chunk, jnp.zeros((1, width), F32))
        _accumulate(dlb_ref, dlb, first)

    assert hps == HEADS
    nt = n // t
    return _call(
        body, name=name, grid=(1, nt),
        in_specs=[sec(COL_Q), sec(COL_F), sec(COL_I), lbspec, head_tile, state],
        out_specs=[pl.BlockSpec((t, 3 * width), lambda h, i: (nt - 1 - i, 0)),
                   pl.BlockSpec((1, width), lambda h, i: (0, h))],
        out_shape=[jax.ShapeDtypeStruct((n, 3 * D_MODEL), BF16), jax.ShapeDtypeStruct((1, D_MODEL), F32)],
        args=[proj, proj, proj, hgrn_lb, do, states], semantics=("parallel", "arbitrary"),
        scratch=[pltpu.VMEM((hps, HEAD_DIM, HEAD_DIM), F32), pltpu.VMEM((CHUNK, width), F32),
                 pltpu.VMEM((hps, CHUNK, CHUNK), F32), pltpu.VMEM((CHUNK, width), F32),
                 pltpu.VMEM((CHUNK, width), F32)],
        exchange=exchange)


def _pool_fwd(proj, pool_w, pool_scale, *, name):
    n = proj.shape[0]
    t = _tile(n, 1024, POOL_HALO)
    per = t // POOL_HALO
    c0 = COL_POOL // POOL_WIDTH

    def body(u_ref, halo_ref, pw_ref, ps_ref, pooled_ref, mixed_ref, ext):
        i = pl.program_id(0)
        u = u_ref[...].astype(F32)
        ext[POOL_HALO:POOL_HALO + t, :] = u
        ext[0:POOL_HALO, :] = jnp.where(i > 0, halo_ref[...].astype(F32), 0.0)
        pos = i * t + lax.broadcasted_iota(jnp.int32, (t, POOL_CH), 0) + 1
        for grp, win in enumerate(POOL_WINDOWS):
            cols = slice(grp * POOL_CH, (grp + 1) * POOL_CH)
            acc = u[:, cols]
            for j in range(1, win):
                acc = acc + ext[POOL_HALO - j:POOL_HALO - j + t, cols]
            pooled = (acc / jnp.minimum(pos, win).astype(F32) - u[:, cols]).astype(BF16)
            pooled_ref[:, cols] = pooled
            mixed_ref[:, cols] = (_dot(pooled, pw_ref[grp].astype(BF16)) * ps_ref[:, cols]).astype(BF16)

    tile = pl.BlockSpec((t, POOL_WIDTH), lambda i: (i, 0))
    return pl.pallas_call(
        body, name=name, grid=(n // t,),
        in_specs=[pl.BlockSpec((t, POOL_WIDTH), lambda i: (i, c0)),
                  pl.BlockSpec((POOL_HALO, POOL_WIDTH), lambda i: (jnp.maximum(i * per - 1, 0), c0)),
                  pl.BlockSpec((len(POOL_WINDOWS), POOL_CH, POOL_CH), lambda i: (0, 0, 0)),
                  pl.BlockSpec((1, POOL_WIDTH), lambda i: (0, 0))],
        out_specs=[tile, tile],
        out_shape=[jax.ShapeDtypeStruct((n, POOL_WIDTH), BF16), jax.ShapeDtypeStruct((n, POOL_WIDTH), BF16)],
        scratch_shapes=[pltpu.VMEM((t + POOL_HALO, POOL_WIDTH), F32)],
        compiler_params=_params("parallel"),
    )(proj, proj, pool_w, pool_scale)


def _pool_bwd(dmixed, pooled, pool_w, pool_scale, *, name):
    n = dmixed.shape[0]
    t = _tile(n, 1024, POOL_HALO)
    per = t // POOL_HALO
    nb = n // t

    def body(dm_ref, dmh_ref, p_ref, pw_ref, ps_ref, du_ref, dpw_ref, dps_ref, ext):
        i = pl.program_id(0)

        @pl.when(i == 0)
        def _():
            dpw_ref[...] = jnp.zeros_like(dpw_ref)
            dps_ref[...] = jnp.zeros_like(dps_ref)

        dm, dmh = dm_ref[...], dmh_ref[...]
        pos = i * t + lax.broadcasted_iota(jnp.int32, (t, POOL_CH), 0) + 1
        for grp, win in enumerate(POOL_WINDOWS):
            cols = slice(grp * POOL_CH, (grp + 1) * POOL_CH)
            pwb = pw_ref[grp].astype(BF16)
            pb = p_ref[:, cols]
            scale = ps_ref[:, cols]
            dps_ref[:, cols] += _colsum(dm[:, cols] * _dot(pb, pwb))
            dpm = (dm[:, cols] * scale).astype(BF16)
            dpw_ref[grp] += _dot_tn(pb, dpm)
            dpool = _dot_nt(dpm, pwb)
            dpool_next = _dot_nt((dmh[:, cols] * scale).astype(BF16), pwb)
            ext[0:t, cols] = dpool / jnp.minimum(pos, win).astype(F32)
            ext[t:t + POOL_HALO, cols] = jnp.where(i < nb - 1, dpool_next * (1.0 / win), 0.0)
            acc = -dpool
            for j in range(win):
                acc = acc + ext[j:j + t, cols]
            du_ref[:, cols] = acc.astype(du_ref.dtype)

    tile = pl.BlockSpec((t, POOL_WIDTH), lambda i: (i, 0))
    return pl.pallas_call(
        body, name=name, grid=(nb,),
        in_specs=[tile, pl.BlockSpec((POOL_HALO, POOL_WIDTH), lambda i: (jnp.minimum((i + 1) * per, nb * per - 1), 0)),
                  tile, pl.BlockSpec((len(POOL_WINDOWS), POOL_CH, POOL_CH), lambda i: (0, 0, 0)),
                  pl.BlockSpec((1, POOL_WIDTH), lambda i: (0, 0))],
        out_specs=[tile, pl.BlockSpec((len(POOL_WINDOWS), POOL_CH, POOL_CH), lambda i: (0, 0, 0)),
                   pl.BlockSpec((1, POOL_WIDTH), lambda i: (0, 0))],
        out_shape=[jax.ShapeDtypeStruct((n, POOL_WIDTH), BF16),
                   jax.ShapeDtypeStruct((len(POOL_WINDOWS), POOL_CH, POOL_CH), F32),
                   jax.ShapeDtypeStruct((1, POOL_WIDTH), F32)],
        scratch_shapes=[pltpu.VMEM((t + POOL_HALO, POOL_WIDTH), F32)],
        compiler_params=_params("arbitrary"),
    )(dmixed, dmixed, pooled, pool_w, pool_scale)


def _adamw(w, g, m, v):
    m2 = ADAM_B1 * m + (1.0 - ADAM_B1) * g
    v2 = ADAM_B2 * v + (1.0 - ADAM_B2) * (g * g)
    m_hat = m2 * (1.0 / (1.0 - ADAM_B1 ** ADAM_STEP))
    v_hat = v2 * (1.0 / (1.0 - ADAM_B2 ** ADAM_STEP))
    delta = -ADAM_LR * (m_hat / (jnp.sqrt(v_hat) + ADAM_EPS) + ADAM_WD * w)
    return delta, m2, v2


def _adam_big(recv, w, m, v, *, name):
    r, c = w.shape
    tr = _tile(r, 512, 16)

    def body(recv_ref, w_ref, m_ref, v_ref, g_ref, d_ref, m2_ref, v2_ref):
        g = recv_ref[0].astype(F32)
        for i in range(1, N_DEV):
            g = g + recv_ref[i].astype(F32)
        delta, m2, v2 = _adamw(w_ref[...], g, m_ref[...], v_ref[...])
        g_ref[...] = g
        d_ref[...] = delta
        m2_ref[...] = m2
        v2_ref[...] = v2

    tile = pl.BlockSpec((tr, c), lambda i: (i, 0))
    out = jax.ShapeDtypeStruct((r, c), F32)
    return pl.pallas_call(
        body, name=name, grid=(r // tr,),
        in_specs=[pl.BlockSpec((N_DEV, tr, c), lambda i: (0, i, 0)), tile, tile, tile],
        out_specs=[tile] * 4, out_shape=[out] * 4, compiler_params=_params("parallel"),
    )(recv, w, m, v)


def _adam_small(parts, w, m, v, *, name):
    n = len(SMALL_PARAMS)

    def body(*refs):
        parts_r, w_r, m_r, v_r = (refs[i * n:(i + 1) * n] for i in range(4))
        outs = refs[4 * n:]
        for j, key in enumerate(SMALL_PARAMS):
            g = parts_r[j][0]
            for i in range(1, N_DEV):
                g = g + parts_r[j][i]
            w_ = w_r[j][...]
            if key == "hgrn_lb":
                s0 = 1.0 / (1.0 + jnp.exp(w_[1:2] - w_[0:1]))
                ga = g * s0 * (1.0 - s0)
                sign = jnp.where(lax.broadcasted_iota(jnp.int32, w_.shape, 0) == 0, 1.0, -1.0)
                g = sign * jnp.broadcast_to(ga, w_.shape)
            delta, m2, v2 = _adamw(w_, g, m_r[j][...], v_r[j][...])
            for q, val in enumerate((g, delta, m2, v2)):
                outs[q * n + j][...] = val

    out_shape = [jax.ShapeDtypeStruct(w[k].shape, F32) for _ in range(4) for k in SMALL_PARAMS]
    res = pl.pallas_call(body, name=name, out_shape=out_shape, compiler_params=_params())(
        *[t[k] for t in (parts, w, m, v) for k in SMALL_PARAMS])
    return [res[q * n:(q + 1) * n] for q in range(4)]


def _as_2d(a):
    return a.reshape(-1, a.shape[-1])


SPLIT_AXIS = dict(BIG_WEIGHTS)


def _gather_of(names, weights):
    return _Exchange([_shard_to_send(weights[k][0], SPLIT_AXIS[k]) for k in names], gather=True)


def _scatter_of(names, dfull):
    return _Exchange([_to_slots(dfull[k], SPLIT_AXIS[k]) for k in names], gather=False)


def _shard_to_send(w, axis):
    return (w.T if axis == 1 else w).astype(BF16)


def _to_slots(dw, axis):
    rows, cols = dw.shape
    return dw.reshape(N_DEV, rows // N_DEV, cols)


def _from_slots(gathered, axis):
    _, r, c = gathered.shape
    return gathered.reshape(N_DEV * r, c)


def kernel(x, p, ffn1_norm, ffn1_w1, ffn1_w3, ffn1_w2, mix_norm, w_in, hgrn_lb, hgrn_onorm, w_branch_a, pool_w, pool_scale, w_branch_b, w_out, ffn2_norm, ffn2_w1, ffn2_w3, ffn2_w2, ple_norm, ple_w_gate, ple_w_proj, ple_post_norm, final_norm, loss_target, m_ffn1_norm, m_ffn1_w1, m_ffn1_w3, m_ffn1_w2, m_mix_norm, m_w_in, m_hgrn_lb, m_hgrn_onorm, m_w_branch_a, m_pool_w, m_pool_scale, m_w_branch_b, m_w_out, m_ffn2_norm, m_ffn2_w1, m_ffn2_w3, m_ffn2_w2, m_ple_norm, m_ple_w_gate, m_ple_w_proj, m_ple_post_norm, m_final_norm, v_ffn1_norm, v_ffn1_w1, v_ffn1_w3, v_ffn1_w2, v_mix_norm, v_w_in, v_hgrn_lb, v_hgrn_onorm, v_w_branch_a, v_pool_w, v_pool_scale, v_w_branch_b, v_w_out, v_ffn2_norm, v_ffn2_w1, v_ffn2_w3, v_ffn2_w2, v_ple_norm, v_ple_w_gate, v_ple_w_proj, v_ple_post_norm, v_final_norm):
    weights = dict(ffn1_norm=ffn1_norm, ffn1_w1=ffn1_w1, ffn1_w3=ffn1_w3, ffn1_w2=ffn1_w2, mix_norm=mix_norm, w_in=w_in, hgrn_lb=hgrn_lb, hgrn_onorm=hgrn_onorm, w_branch_a=w_branch_a, pool_w=pool_w, pool_scale=pool_scale, w_branch_b=w_branch_b, w_out=w_out, ffn2_norm=ffn2_norm, ffn2_w1=ffn2_w1, ffn2_w3=ffn2_w3, ffn2_w2=ffn2_w2, ple_norm=ple_norm, ple_w_gate=ple_w_gate, ple_w_proj=ple_w_proj, ple_post_norm=ple_post_norm, final_norm=final_norm)
    mom1 = dict(ffn1_norm=m_ffn1_norm, ffn1_w1=m_ffn1_w1, ffn1_w3=m_ffn1_w3, ffn1_w2=m_ffn1_w2, mix_norm=m_mix_norm, w_in=m_w_in, hgrn_lb=m_hgrn_lb, hgrn_onorm=m_hgrn_onorm, w_branch_a=m_w_branch_a, pool_w=m_pool_w, pool_scale=m_pool_scale, w_branch_b=m_w_branch_b, w_out=m_w_out, ffn2_norm=m_ffn2_norm, ffn2_w1=m_ffn2_w1, ffn2_w3=m_ffn2_w3, ffn2_w2=m_ffn2_w2, ple_norm=m_ple_norm, ple_w_gate=m_ple_w_gate, ple_w_proj=m_ple_w_proj, ple_post_norm=m_ple_post_norm, final_norm=m_final_norm)
    mom2 = dict(ffn1_norm=v_ffn1_norm, ffn1_w1=v_ffn1_w1, ffn1_w3=v_ffn1_w3, ffn1_w2=v_ffn1_w2, mix_norm=v_mix_norm, w_in=v_w_in, hgrn_lb=v_hgrn_lb, hgrn_onorm=v_hgrn_onorm, w_branch_a=v_w_branch_a, pool_w=v_pool_w, pool_scale=v_pool_scale, w_branch_b=v_w_branch_b, w_out=v_w_out, ffn2_norm=v_ffn2_norm, ffn2_w1=v_ffn2_w1, ffn2_w3=v_ffn2_w3, ffn2_w2=v_ffn2_w2, ple_norm=v_ple_norm, ple_w_gate=v_ple_w_gate, ple_w_proj=v_ple_w_proj, ple_post_norm=v_ple_post_norm, final_norm=v_final_norm)

    xs = x[0]
    ps = p[0, 0].astype(BF16)
    tgt = loss_target[0]
    n = xs.shape[0]

    g_f1, g_mix, g_on, g_f2 = ffn1_norm, mix_norm, hgrn_onorm, ffn2_norm
    g_ple, g_post, g_fin = ple_norm, ple_post_norm, final_norm.reshape(1, D_MODEL)
    lb2 = hgrn_lb
    pw, pscale = pool_w[0], pool_scale

    full = {}

    def keep(names, gathered):
        for k, g in zip(names, gathered):
            full[k] = _from_slots(g, SPLIT_AXIS[k])

    names = ("ffn1_w1", "ffn1_w3")
    ex = _gather_of(names, weights)
    h1 = _rms_fwd(xs, g_f1, name="ffn1_rms", exchange=ex)
    keep(names, ex.received)
    names = ("ffn1_w2", "w_in")
    ex = _gather_of(names, weights)
    a1, b1, s1 = _ffn_up(h1, full["ffn1_w1"], full["ffn1_w3"], name="ffn1_up", exchange=ex)
    keep(names, ex.received)
    names = ("w_branch_a", "w_branch_b", "w_out")
    ex = _gather_of(names, weights)
    x1, h2 = _mm_nn_res_rms(s1, full["ffn1_w2"], xs, g_mix, name="ffn1_down", scale=0.5, exchange=ex)
    keep(names, ex.received)
    names = ("ffn2_w1", "ffn2_w3", "ffn2_w2", "ple_w_gate", "ple_w_proj")
    ex = _gather_of(names, weights)
    proj = _mm_nn_wide(h2, full["w_in"], name="w_in_proj", out_dtype=BF16, tn=512, exchange=ex)
    keep(names, ex.received)
    o, states = _hgrn_fwd(proj, lb2, name="hgrn_fwd")
    on = _hgrn_post_fwd(o, proj, g_on, name="hgrn_post_fwd")
    ya = _mm_nn_wide(on, full["w_branch_a"], name="branch_a", out_dtype=BF16, b_is_km=True)
    pooled, mixed = _pool_fwd(proj, pw, pscale, name="pool_fwd")
    yb = _mm_nn_wide(mixed, full["w_branch_b"], name="branch_b", out_dtype=BF16)
    y = _merge_fwd(proj, ya, yb, name="merge_fwd")
    x2, h3 = _mm_nn_res_rms(y, full["w_out"], x1, g_f2, name="w_out_proj", scale=1.0)
    a2, b2, s2 = _ffn_up(h3, full["ffn2_w1"], full["ffn2_w3"], name="ffn2_up")
    x3, h4 = _mm_nn_res_rms(s2, full["ffn2_w2"], x2, g_ple, name="ffn2_down", scale=0.5)
    gpre = _mm_nn_wide(h4, full["ple_w_gate"], name="ple_gate", out_dtype=BF16, b_is_km=True)
    z = _mm_nn_wide(ps, full["ple_w_proj"], name="ple_proj", out_dtype=BF16)
    dx4, dpre, dz, loss_part, d_fin, d_post = _ple_final(x3, gpre, z, tgt, g_post, g_fin, name="ple_final")

    dfull, received = {}, {}

    def sent(names, exchange):
        received.update(zip(names, exchange.received))

    dfull["ple_w_proj"] = _mm_tn(ps, dz, name="d_ple_w_proj", tn=1024, tm=1024, transpose_out=True)
    dfull["ple_w_gate"] = _mm_tn(h4, dpre, name="d_ple_w_gate", tn=1024, tm=1024)
    dx3, dx3s, d_ple = _mm_nt_rms_bwd([(dpre, full["ple_w_gate"], False)], x3, dx4, g_ple, name="ple_rms_bwd", tn=1024,
                                      half_scale=0.5)

    names = ("ple_w_proj", "ple_w_gate")
    ex = _scatter_of(names, dfull)
    da2, db2 = _ffn_bwd_mid(dx3s, full["ffn2_w2"], a2, b2, name="ffn2_bwd_mid", exchange=ex)
    sent(names, ex)
    dfull["ffn2_w2"] = _mm_tn(s2, dx3s, name="ffn2_dw2", tn=1024, tm=1024)
    dfull["ffn2_w1"] = _mm_tn(h3, da2, name="ffn2_dw1", tn=1024, tm=2816, transpose_out=True)
    dfull["ffn2_w3"] = _mm_tn(h3, db2, name="ffn2_dw3", tn=1024, tm=2816, transpose_out=True)
    names = ("ffn2_w2",)
    ex = _scatter_of(names, dfull)
    dx2, dx2b, d_f2 = _mm_nt_rms_bwd([(da2, full["ffn2_w1"], True), (db2, full["ffn2_w3"], True)], x2, dx3, g_f2,
                                     name="ffn2_rms_bwd", tn=512, half_scale=1.0, exchange=ex)
    sent(names, ex)

    dfull["w_out"] = _mm_tn(y, dx2b, name="d_w_out", tn=1024, tm=1024)
    dy = _mm_nn_wide(dx2b, full["w_out"], name="d_y", out_dtype=BF16)
    dya, dyb, dga, dgb = _merge_bwd(dy, proj, ya, yb, name="merge_bwd")

    dfull["w_branch_b"] = _mm_tn(mixed, dyb, name="d_w_branch_b", tn=1024, tm=1024, transpose_out=True)
    dmixed = _mm_nn_wide(dyb, full["w_branch_b"], name="d_mixed", out_dtype=F32, b_is_km=True)
    du, d_pw, d_ps = _pool_bwd(dmixed, pooled, pw, pscale, name="pool_bwd")

    dfull["w_branch_a"] = _mm_tn(on, dya, name="d_w_branch_a", tn=1024, tm=1024)
    don = _mm_nn_wide(dya, full["w_branch_a"], name="d_on", out_dtype=BF16)
    dog, do, d_on = _hgrn_post_bwd(don, o, proj, g_on, name="hgrn_post_bwd")
    names = ("ffn2_w1", "ffn2_w3", "w_out", "w_branch_b", "w_branch_a")
    ex = _scatter_of(names, dfull)
    dqfi, d_lb = _hgrn_bwd(proj, lb2, do, states, name="hgrn_bwd", exchange=ex)
    sent(names, ex)
    dproj = [dqfi, dog, du, dga, dgb]
    dfull["w_in"] = jnp.concatenate(
        [_mm_tn(h2, part, name=f"d_w_in_{j}", tn=1024, tm=3072, transpose_out=True) for j, part in enumerate(dproj)],
        axis=0)
    names = ("w_in",)
    ex = _scatter_of(names, dfull)
    dx1, dx1s, d_mix = _mm_nt_rms_bwd([(dproj, full["w_in"], True)], x1, dx2, g_mix, name="mix_rms_bwd", tn=512,
                                      half_scale=0.5, exchange=ex)
    sent(names, ex)

    da1, db1 = _ffn_bwd_mid(dx1s, full["ffn1_w2"], a1, b1, name="ffn1_bwd_mid")
    dfull["ffn1_w2"] = _mm_tn(s1, dx1s, name="ffn1_dw2", tn=1024, tm=1024)
    names = ("ffn1_w2",)
    ex = _scatter_of(names, dfull)
    dfull["ffn1_w1"] = _mm_tn(h1, da1, name="ffn1_dw1", tn=1024, tm=2816, transpose_out=True, exchange=ex)
    sent(names, ex)
    names = ("ffn1_w1",)
    ex = _scatter_of(names, dfull)
    dfull["ffn1_w3"] = _mm_tn(h1, db1, name="ffn1_dw3", tn=1024, tm=2816, transpose_out=True, exchange=ex)
    sent(names, ex)
    names = ("ffn1_w3",)
    ex = _scatter_of(names, dfull)
    grad_x, _, d_f1 = _mm_nt_rms_bwd([(da1, full["ffn1_w1"], True), (db1, full["ffn1_w3"], True)], xs, dx1, g_f1,
                                     name="ffn1_rms_bwd", tn=512, half_scale=1.0, exchange=ex)
    sent(names, ex)

    small_part = dict(ffn1_norm=d_f1, mix_norm=d_mix, hgrn_onorm=d_on, ffn2_norm=d_f2, ple_norm=d_ple,
                      ple_post_norm=d_post, final_norm=d_fin, hgrn_lb=d_lb, pool_scale=d_ps, pool_w=_as_2d(d_pw))
    gathered_small = _exchange_now([small_part[k] for k in SMALL_PARAMS] + [loss_part], name="gather_small_grads",
                                   gather=True)
    small_all = dict(zip(SMALL_PARAMS, gathered_small))
    loss = jnp.sum(gathered_small[-1])

    grads, deltas, new_m, new_v = {}, {}, {}, {}
    for name, axis in BIG_WEIGHTS:
        shape, recv = weights[name].shape, received[name]
        own = [t[name][0].T if axis == 1 else t[name][0] for t in (weights, mom1, mom2)]
        res = _adam_big(recv, *own, name=f"adam_{name}")
        grads[name], deltas[name], new_m[name], new_v[name] = [(r.T if axis == 1 else r).reshape(shape) for r in res]
    res = _adam_small(small_all, *[{k: _as_2d(t[k]) for k in SMALL_PARAMS} for t in (weights, mom1, mom2)],
                      name="adam_small")
    for store, vals in zip((grads, deltas, new_m, new_v), res):
        store.update({k: val.reshape(weights[k].shape) for k, val in zip(SMALL_PARAMS, vals)})

    return (loss, grad_x.reshape(x.shape), *[grads[k] for k in WEIGHT_ORDER], *[deltas[k] for k in WEIGHT_ORDER],
            *[new_m[k] for k in WEIGHT_ORDER], *[new_v[k] for k in WEIGHT_ORDER])
```

```python
import jax
import jax.numpy as jnp
from jax import lax
from jax.experimental import pallas as pl
from jax.experimental.pallas import tpu as pltpu

F32 = jnp.float32
BF16 = jnp.bfloat16

N_DEV = 8
D_MODEL = 1024
HEADS = 8
HEAD_DIM = 128
POOL_WINDOWS = (2, 4, 8, 16)
POOL_CH = 128
POOL_WIDTH = 512
POOL_HALO = 16
RMS_EPS = 1e-6
CHUNK = 64
SUB = 32
HGRN_BWD_LANE_GROUPS = 2
HGRN_HEADS_PER_STEP = 8
NEG_BIG = -1e30

ADAM_LR = 0.001
ADAM_B1 = 0.9
ADAM_B2 = 0.999
ADAM_EPS = 1e-08
ADAM_WD = 0.01
ADAM_STEP = 10

V7X_VMEM_BYTES = 64 * 1024 * 1024
VMEM_LIMIT = (V7X_VMEM_BYTES * 3) // 4
EXCHANGE_TAIL_STEPS = 3
ROW_TILE_CAP = 8192

COL_Q, COL_F, COL_I, COL_OG, COL_POOL, COL_GA, COL_GB = 0, 1024, 2048, 3072, 4096, 4608, 5632

BIG_WEIGHTS = (
    ("ffn1_w1", 1), ("ffn1_w3", 1), ("ffn1_w2", 0), ("w_in", 1), ("w_branch_a", 0), ("w_branch_b", 1),
    ("w_out", 0), ("ffn2_w1", 1), ("ffn2_w3", 1), ("ffn2_w2", 0), ("ple_w_gate", 0), ("ple_w_proj", 1),
)
SMALL_PARAMS = ("ffn1_norm", "mix_norm", "hgrn_onorm", "ffn2_norm", "ple_norm", "ple_post_norm", "final_norm",
                "hgrn_lb", "pool_scale", "pool_w")
WEIGHT_ORDER = (
    "ffn1_norm", "ffn1_w1", "ffn1_w3", "ffn1_w2", "mix_norm", "w_in", "hgrn_lb", "hgrn_onorm", "w_branch_a", "pool_w",
    "pool_scale", "w_branch_b", "w_out", "ffn2_norm", "ffn2_w1", "ffn2_w3", "ffn2_w2", "ple_norm", "ple_w_gate",
    "ple_w_proj", "ple_post_norm", "final_norm",
)


def _params(*sem):
    return pltpu.CompilerParams(dimension_semantics=sem if sem else None, vmem_limit_bytes=VMEM_LIMIT)


COL_CHUNK = 256


def _rows(tn, width):
    return pl.BlockSpec((tn, width), lambda i: (i, 0))


def _resident(shape):
    return pl.BlockSpec(shape, lambda i: (0,) * len(shape), pipeline_mode=pl.Buffered(1))


def _dot(a, b):
    return jnp.dot(a, b, preferred_element_type=F32)


def _dot_nt(a, b):
    return lax.dot_general(a, b, (((1,), (1,)), ((), ())), preferred_element_type=F32)


def _dot_tn(a, b):
    return lax.dot_general(a, b, (((0,), (0,)), ((), ())), preferred_element_type=F32)


def _sigmoid(x):
    return 0.5 * jnp.tanh(0.5 * x) + 0.5


def _tile(n, want, mult):
    if mult != 128:
        want = min(want, ROW_TILE_CAP)
    if n <= want:
        return n
    t = (want // mult) * mult
    while t > mult and n % t:
        t -= mult
    assert n % t == 0, (n, want, mult)
    return t


class _Exchange:
    COPIES = N_DEV - 1

    def __init__(self, arrs, gather):
        self.arrs, self.gather, self.n = list(arrs), gather, len(arrs)
        self.out_shape = [jax.ShapeDtypeStruct((N_DEV,) + (a.shape if gather else a.shape[1:]), a.dtype) for a in arrs]
        self.scratch = [pltpu.SemaphoreType.DMA((self.n * self.COPIES,)),
                        pltpu.SemaphoreType.DMA((self.n * self.COPIES,)), pltpu.SemaphoreType.DMA((self.n,))]
        self.received = None

    @staticmethod
    def _place():
        x, y, c = lax.axis_index("x"), lax.axis_index("y"), lax.axis_index("c")
        return x, y, c

    def _copy(self, a, k, src, dst, to, sems):
        s = a * self.COPIES + k
        return pltpu.make_async_remote_copy(src_ref=src, dst_ref=dst, send_sem=sems[0].at[s], recv_sem=sems[1].at[s],
                                            device_id=to, device_id_type=pl.DeviceIdType.MESH)

    def _gather_copies(self, role, ins, outs, sems):
        x, y, c = self._place()
        chips = [(1 - x, y), (x, 1 - y), (1 - x, 1 - y)]
        sibling = (x, y, 1 - c)

        def slot(px, py, pc):
            return 4 * px + 2 * py + pc

        copies = []
        for a in range(self.n):
            mine = outs[a].at[slot(x, y, c)]
            if role == "first":
                copies.append(self._copy(a, 0, ins[a], mine, sibling, sems))
            elif role == "last":
                copies.append(self._copy(a, 0, ins[a], outs[a].at[slot(x, y, 1 - c)], sibling, sems))
            for j, (px, py) in enumerate(chips):
                theirs = outs[a].at[slot(px, py, c)]
                if role == "first":
                    copies.append(self._copy(a, 1 + j, ins[a], mine, (px, py, c), sems))
                elif role == "landed":
                    copies.append(self._copy(a, 1 + j, ins[a], theirs, (px, py, c), sems))
                elif role == "onward":
                    copies.append(self._copy(a, 4 + j, theirs, theirs, sibling, sems))
                else:
                    copies.append(self._copy(a, 4 + j, ins[a], outs[a].at[slot(px, py, 1 - c)], sibling, sems))
        return copies

    def _scatter_copies(self, role, ins, outs, sems):
        x, y, c = self._place()
        me = 4 * x + 2 * y + c
        copies = []
        for k in range(1, N_DEV):
            px = 1 - x if k & 4 else x
            py = 1 - y if k & 2 else y
            pc = 1 - c if k & 1 else c
            peer = 4 * px + 2 * py + pc
            for a in range(self.n):
                dst = outs[a].at[me] if role == "sends" else outs[a].at[peer]
                copies.append(self._copy(a, k - 1, ins[a].at[peer], dst, (px, py, pc), sems))
        return copies

    def _local(self, ins, outs, sems):
        x, y, c = self._place()
        me = 4 * x + 2 * y + c
        return [pltpu.make_async_copy(ins[a] if self.gather else ins[a].at[me], outs[a].at[me], sems[2].at[a])
                for a in range(self.n)]

    def start(self, ins, outs, sems):
        for cp in self._local(ins, outs, sems):
            cp.start()
        sends = (self._gather_copies("first", ins, outs, sems) if self.gather
                 else self._scatter_copies("sends", ins, outs, sems))
        for cp in sends:
            cp.start()

    def pass_on(self, ins, outs, sems):
        if self.gather:
            for landed, onward in zip(self._gather_copies("landed", ins, outs, sems),
                                      self._gather_copies("onward", ins, outs, sems)):
                landed.wait_recv()
                onward.start()

    def finish(self, ins, outs, sems):
        if self.gather:
            arrivals = self._gather_copies("last", ins, outs, sems)
            sends = self._gather_copies("first", ins, outs, sems) + self._gather_copies("onward", ins, outs, sems)
        else:
            arrivals = self._scatter_copies("arrivals", ins, outs, sems)
            sends = self._scatter_copies("sends", ins, outs, sems)
        for cp in arrivals:
            cp.wait_recv()
        for cp in sends:
            cp.wait_send()
        for cp in self._local(ins, outs, sems):
            cp.wait()


def _call(body, *, name, grid, in_specs, out_specs, out_shape, args, semantics, scratch=(), exchange=None):
    if exchange is None:
        return pl.pallas_call(
            body, name=name, grid=grid, in_specs=in_specs, out_specs=out_specs, out_shape=out_shape,
            scratch_shapes=list(scratch), compiler_params=_params(*semantics))(*args)
    ex = exchange
    n_in, n_out, n_s = len(in_specs), len(out_specs), len(scratch)

    def wrapped(*refs):
        ins, ex_in = refs[:n_in], refs[n_in:n_in + ex.n]
        o0 = n_in + ex.n
        outs, ex_out = refs[o0:o0 + n_out], refs[o0 + n_out:o0 + n_out + ex.n]
        s0 = o0 + n_out + ex.n
        scr, sems = refs[s0:s0 + n_s], refs[s0 + n_s:]
        step, steps = pl.program_id(0), grid[0]
        for ax in range(1, len(grid)):
            step, steps = step * grid[ax] + pl.program_id(ax), steps * grid[ax]

        @pl.when(step == 0)
        def _():
            ex.start(ex_in, ex_out, sems)

        body(*ins, *outs, *scr)

        @pl.when(step == max(steps - EXCHANGE_TAIL_STEPS, 0))
        def _():
            ex.pass_on(ex_in, ex_out, sems)

        @pl.when(step == steps - 1)
        def _():
            ex.finish(ex_in, ex_out, sems)

    hbm = pl.BlockSpec(memory_space=pltpu.HBM)
    res = pl.pallas_call(
        wrapped, name=name, grid=grid, in_specs=list(in_specs) + [hbm] * ex.n,
        out_specs=list(out_specs) + [hbm] * ex.n, out_shape=list(out_shape) + ex.out_shape,
        scratch_shapes=list(scratch) + ex.scratch, compiler_params=_params(*(["arbitrary"] * len(grid))),
    )(*args, *ex.arrs)
    ex.received = res[n_out:]
    return res[:n_out]


def _exchange_now(arrs, *, name, gather):
    ex = _Exchange(arrs, gather)
    n = ex.n

    def body(*refs):
        ex.start(refs[:n], refs[n:2 * n], refs[2 * n:])
        ex.pass_on(refs[:n], refs[n:2 * n], refs[2 * n:])
        ex.finish(refs[:n], refs[n:2 * n], refs[2 * n:])

    hbm = pl.BlockSpec(memory_space=pltpu.HBM)
    return pl.pallas_call(body, name=name, out_shape=ex.out_shape, in_specs=[hbm] * n, out_specs=[hbm] * n,
                          scratch_shapes=ex.scratch)(*arrs)


def _mm_tn(a, b, *, name, tn, tm, transpose_out=False, exchange=None):
    n, k = a.shape
    m = b.shape[1]
    tn, tm = _tile(n, tn, 16), _tile(m, tm, 128)
    steps = n // tn

    def body(a_ref, b_ref, o_ref, acc):
        i = pl.program_id(1)

        @pl.when(i == 0)
        def _():
            acc[...] = jnp.zeros_like(acc)

        acc[...] += _dot_tn(a_ref[...], b_ref[...])

        @pl.when(i == steps - 1)
        def _():
            for c0 in range(0, tm, COL_CHUNK):
                cols = slice(c0, min(c0 + COL_CHUNK, tm))
                if transpose_out:
                    o_ref[cols, :] = acc[:, cols].T.astype(o_ref.dtype)
                else:
                    o_ref[:, cols] = acc[:, cols].astype(o_ref.dtype)

    if transpose_out:
        out_spec, out_shape = pl.BlockSpec((tm, k), lambda j, i: (j, 0)), jax.ShapeDtypeStruct((m, k), BF16)
    else:
        out_spec, out_shape = pl.BlockSpec((k, tm), lambda j, i: (0, j)), jax.ShapeDtypeStruct((k, m), BF16)
    return _call(body, name=name, grid=(m // tm, steps),
                 in_specs=[pl.BlockSpec((tn, k), lambda j, i: (i, 0)), pl.BlockSpec((tn, tm), lambda j, i: (i, j))],
                 out_specs=[out_spec], out_shape=[out_shape], args=[a, b], semantics=("parallel", "arbitrary"),
                 scratch=[pltpu.VMEM((k, tm), F32)], exchange=exchange)[0]


def _ffn_up(h, w1, w3, *, name, tn=512, exchange=None):
    n, k = h.shape
    m = w1.shape[0]
    tn = _tile(n, tn, 16)

    def body(h_ref, w1_ref, w3_ref, dsda_ref, dsdb_ref, s_ref):
        for c0 in range(0, m, COL_CHUNK):
            cols = slice(c0, c0 + COL_CHUNK)
            a = _dot_nt(h_ref[...], w1_ref[cols, :])
            b = _dot_nt(h_ref[...], w3_ref[cols, :])
            sg = _sigmoid(a)
            silu = a * sg
            dsda_ref[:, cols] = (b * (sg + silu * (1.0 - sg))).astype(dsda_ref.dtype)
            dsdb_ref[:, cols] = silu.astype(dsdb_ref.dtype)
            s_ref[:, cols] = (silu * b).astype(s_ref.dtype)

    ospec = _rows(tn, m)
    return _call(body, name=name, grid=(n // tn,),
                 in_specs=[_rows(tn, k), _resident(w1.shape), _resident(w3.shape)], out_specs=[ospec, ospec, ospec],
                 out_shape=[jax.ShapeDtypeStruct((n, m), BF16)] * 3,
                 args=[h, w1, w3], semantics=("parallel",), exchange=exchange)


def _mm_nn_wide(a, b, *, name, out_dtype, b_is_km=False, tn=1024, exchange=None):
    n, k = a.shape
    m = b.shape[1 if b_is_km else 0]
    tn = _tile(n, tn, 16)
    chunk = min(2 * COL_CHUNK, m)

    def body(a_ref, b_ref, o_ref):
        for c0 in range(0, m, chunk):
            cols = slice(c0, c0 + chunk)
            if b_is_km:
                res = _dot(a_ref[...], b_ref[:, cols])
            else:
                res = _dot_nt(a_ref[...], b_ref[cols, :])
            o_ref[:, cols] = res.astype(o_ref.dtype)

    return _call(body, name=name, grid=(n // tn,), in_specs=[_rows(tn, k), _resident(b.shape)],
                 out_specs=[_rows(tn, m)], out_shape=[jax.ShapeDtypeStruct((n, m), out_dtype)], args=[a, b],
                 semantics=("parallel",), exchange=exchange)[0]


def _mm_nn_res_rms(a, b, res, g, *, name, scale, exchange=None):
    n, k = a.shape
    d = b.shape[1]
    tn = _tile(n, 1024, 16)

    def body(a_ref, b_ref, r_ref, g_ref, x_ref, h_ref):
        for c0 in range(0, d, COL_CHUNK):
            cols = slice(c0, c0 + COL_CHUNK)
            x_ref[:, cols] = r_ref[:, cols] + scale * _dot(a_ref[...], b_ref[:, cols])
        x = x_ref[...]
        r = lax.rsqrt(_rowmean(x * x) + RMS_EPS)
        h_ref[...] = (x * r * g_ref[...]).astype(h_ref.dtype)

    row = _rows(tn, d)
    return _call(body, name=name, grid=(n // tn,),
                 in_specs=[_rows(tn, k), _resident(b.shape), row, pl.BlockSpec((1, d), lambda i: (0, 0))],
                 out_specs=[row, row],
                 out_shape=[jax.ShapeDtypeStruct((n, d), F32), jax.ShapeDtypeStruct((n, d), BF16)],
                 args=[a, b, res, g], semantics=("parallel",), exchange=exchange)


def _ffn_bwd_mid(dxs, w2, dsda, dsdb, *, name, exchange=None):
    n, d = dxs.shape
    m = w2.shape[0]
    tn = _tile(n, 512, 16)

    def body(dx_ref, w2_ref, dsda_ref, dsdb_ref, da_ref, db_ref):
        for c0 in range(0, m, COL_CHUNK):
            cols = slice(c0, c0 + COL_CHUNK)
            ds = _dot_nt(dx_ref[...], w2_ref[cols, :])
            da_ref[:, cols] = (ds * dsda_ref[:, cols].astype(F32)).astype(da_ref.dtype)
            db_ref[:, cols] = (ds * dsdb_ref[:, cols].astype(F32)).astype(db_ref.dtype)

    tile = _rows(tn, m)
    return _call(body, name=name, grid=(n // tn,),
                 in_specs=[_rows(tn, d), _resident(w2.shape), tile, tile], out_specs=[tile, tile],
                 out_shape=[jax.ShapeDtypeStruct((n, m), BF16), jax.ShapeDtypeStruct((n, m), BF16)],
                 args=[dxs, w2, dsda, dsdb], semantics=("parallel",), exchange=exchange)


def _rowwise(fn, *, name, n, tn, ncol, rows, vecs, outs, accs=(), exchange=None):
    tn = _tile(n, tn, 16)
    nr, nv, no = len(rows), len(vecs), len(outs)

    def body(*refs):
        first = pl.program_id(1) == 0
        vals = [r[...].astype(F32) for r in refs[:nr + nv]]
        res = fn(*vals)
        for ref, val in zip(refs[nr + nv:nr + nv + no], res[:no]):
            ref[...] = val.astype(ref.dtype)
        for ref, val in zip(refs[nr + nv + no:], res[no:]):
            _accumulate(ref, val, first)

    in_specs = [pl.BlockSpec((tn, w), lambda j, i, c0=c0: (i, c0 + j)) for _, w, c0 in rows]
    in_specs += [pl.BlockSpec((1, w), lambda j, i, c0=c0: (0, c0 + j)) for _, w, c0 in vecs]
    out_specs = [pl.BlockSpec((tn, w), lambda j, i: (i, j)) for _, w, _ in outs]
    out_specs += [pl.BlockSpec((1, w), lambda j, i: (0, j)) for _, w in accs]
    out_shape = [jax.ShapeDtypeStruct((n, tw), dt) for tw, _, dt in outs]
    out_shape += [jax.ShapeDtypeStruct((1, tw), F32) for tw, _ in accs]
    return _call(body, name=name, grid=(ncol, n // tn), in_specs=in_specs, out_specs=out_specs, out_shape=out_shape,
                 args=[r[0] for r in rows] + [v[0] for v in vecs], semantics=("parallel", "arbitrary"),
                 exchange=exchange)


def _accumulate(ref, val, first):
    @pl.when(first)
    def _():
        ref[...] = jnp.zeros_like(ref)

    ref[...] += val


def _colsum(x):
    return jnp.sum(x, axis=0, keepdims=True)


def _rowmean(x):
    return jnp.mean(x, axis=-1, keepdims=True)


def _rms_fwd(x, g, *, name, exchange=None):
    def fn(x_, g_):
        r = lax.rsqrt(_rowmean(x_ * x_) + RMS_EPS)
        return (x_ * r * g_,)

    n, d = x.shape
    return _rowwise(fn, name=name, n=n, tn=512, ncol=1, rows=[(x, d, 0)], vecs=[(g, d, 0)], outs=[(d, d, BF16)],
                    exchange=exchange)[0]


def _mm_nt_rms_bwd(pairs, x, extra, g, *, name, tn, half_scale, exchange=None):
    n, d = x.shape
    tn = _tile(n, tn, 16)
    pairs = [(list(a) if isinstance(a, (list, tuple)) else [a], b, t) for a, b, t in pairs]
    nref = sum(len(a) + 1 for a, _, _ in pairs)

    def body(*refs):
        x_ref, e_ref, g_ref, dx_ref, dxs_ref, dg_ref = refs[nref:]
        dh, at = None, 0
        for parts, _, transposed in pairs:
            b_ref = refs[at + len(parts)]
            col = 0
            for j, part in enumerate(parts):
                w = part.shape[1]
                if transposed:
                    term = _dot(refs[at + j][...], b_ref[col:col + w, :])
                else:
                    term = _dot_nt(refs[at + j][...], b_ref[:, col:col + w])
                dh = term if dh is None else dh + term
                col += w
            at += len(parts) + 1
        x_ = x_ref[...]
        r = lax.rsqrt(_rowmean(x_ * x_) + RMS_EPS)
        xh = x_ * r
        dxh = dh * g_ref[...]
        dx = e_ref[...] + r * (dxh - xh * _rowmean(dxh * xh))
        dx_ref[...] = dx
        dxs_ref[...] = (dx * half_scale).astype(dxs_ref.dtype)
        _accumulate(dg_ref, _colsum(dh * xh), pl.program_id(0) == 0)

    in_specs, args = [], []
    for parts, b, transposed in pairs:
        assert sum(part.shape[1] for part in parts) == b.shape[0 if transposed else 1]
        in_specs += [pl.BlockSpec((tn, part.shape[1]), lambda i: (i, 0)) for part in parts]
        in_specs.append(pl.BlockSpec(b.shape, lambda i: (0, 0), pipeline_mode=pl.Buffered(1)))
        args += parts + [b]
    row = pl.BlockSpec((tn, d), lambda i: (i, 0))
    vec = pl.BlockSpec((1, d), lambda i: (0, 0))
    return _call(body, name=name, grid=(n // tn,), in_specs=in_specs + [row, row, vec], out_specs=[row, row, vec],
                 out_shape=[jax.ShapeDtypeStruct((n, d), F32), jax.ShapeDtypeStruct((n, d), BF16),
                            jax.ShapeDtypeStruct((1, d), F32)],
                 args=args + [x, extra, g], semantics=("arbitrary",), exchange=exchange)


def _ple_final(x3, gpre, z, tgt, gpp, gf, *, name):
    def fn(x3_, gpre_, z_, tgt_, gpp_, gf_):
        gate = _sigmoid(gpre_)
        rz = lax.rsqrt(_rowmean(z_ * z_) + RMS_EPS)
        zh = z_ * rz
        e = zh * gpp_
        x4 = x3_ + gate * e
        r4 = lax.rsqrt(_rowmean(x4 * x4) + RMS_EPS)
        x4h = x4 * r4
        diff = x4h * gf_ - tgt_
        dout = diff * (1.0 / D_MODEL)
        dxh4 = dout * gf_
        dx4 = r4 * (dxh4 - x4h * _rowmean(dxh4 * x4h))
        dpre = dx4 * e * gate * (1.0 - gate)
        de = dx4 * gate
        dzh = de * gpp_
        dz = rz * (dzh - zh * _rowmean(dzh * zh))
        return dx4, dpre, dz, _colsum(diff * diff) * (0.5 / D_MODEL), _colsum(dout * x4h), _colsum(de * zh)

    n, d = x3.shape
    return _rowwise(fn, name=name, n=n, tn=512, ncol=1, rows=[(x3, d, 0), (gpre, d, 0), (z, d, 0), (tgt, d, 0)],
                    vecs=[(gpp, d, 0), (gf, d, 0)], outs=[(d, d, F32), (d, d, BF16), (d, d, BF16)],
                    accs=[(d, d), (d, d), (d, d)])


def _merge_fwd(proj, ya, yb, *, name):
    def fn(ga, gb, ya_, yb_):
        return (_sigmoid(ga) * ya_ + _sigmoid(gb) * yb_,)

    n = proj.shape[0]
    w = 512
    return _rowwise(fn, name=name, n=n, tn=2048, ncol=D_MODEL // w,
                    rows=[(proj, w, COL_GA // w), (proj, w, COL_GB // w), (ya, w, 0), (yb, w, 0)], vecs=[],
                    outs=[(D_MODEL, w, BF16)])[0]


def _merge_bwd(dy, proj, ya, yb, *, name):
    def fn(dy_, ga, gb, ya_, yb_):
        sa, sb = _sigmoid(ga), _sigmoid(gb)
        return dy_ * sa, dy_ * sb, dy_ * ya_ * sa * (1.0 - sa), dy_ * yb_ * sb * (1.0 - sb)

    n = proj.shape[0]
    w = 512
    return _rowwise(fn, name=name, n=n, tn=2048, ncol=D_MODEL // w,
                    rows=[(dy, w, 0), (proj, w, COL_GA // w), (proj, w, COL_GB // w), (ya, w, 0), (yb, w, 0)],
                    vecs=[], outs=[(D_MODEL, w, BF16)] * 4)


def _head_mean(x):
    return jnp.concatenate(
        [jnp.broadcast_to(jnp.mean(x[:, h * HEAD_DIM:(h + 1) * HEAD_DIM], axis=-1, keepdims=True),
                          (x.shape[0], HEAD_DIM)) for h in range(HEADS)], axis=1)


def _hgrn_post_fwd(o, proj, onorm, *, name):
    def fn(o_, og, gam):
        r = lax.rsqrt(_head_mean(o_ * o_) + RMS_EPS)
        return (o_ * r * gam * (og * _sigmoid(og)),)

    n = o.shape[0]
    w = D_MODEL
    return _rowwise(fn, name=name, n=n, tn=512, ncol=1, rows=[(o, w, 0), (proj, w, COL_OG // w)],
                    vecs=[(onorm, w, 0)], outs=[(D_MODEL, w, BF16)])[0]


def _hgrn_post_bwd(don, o, proj, onorm, *, name):
    def fn(don_, o_, og, gam):
        r = lax.rsqrt(_head_mean(o_ * o_) + RMS_EPS)
        oh = o_ * r
        sg = _sigmoid(og)
        dog = don_ * oh * gam * (sg * (1.0 + og * (1.0 - sg)))
        dn = don_ * (og * sg)
        doh = dn * gam
        do = r * (doh - oh * _head_mean(doh * oh))
        return dog, do, _colsum(dn * oh)

    n = o.shape[0]
    w = D_MODEL
    return _rowwise(fn, name=name, n=n, tn=512, ncol=1, rows=[(don, w, 0), (o, w, 0), (proj, w, COL_OG // w)],
                    vecs=[(onorm, w, 0)], outs=[(D_MODEL, w, BF16), (D_MODEL, w, BF16)], accs=[(D_MODEL, w)])


def _tri_sum(tri, x):
    hi = x.astype(BF16)
    lo = (x - hi.astype(F32)).astype(BF16)
    return _dot(tri, hi) + _dot(tri, lo)


def _lower_bound(lb_ref):
    return 1.0 / (1.0 + jnp.exp(lb_ref[1:2, :] - lb_ref[0:1, :]))


def _hgrn_specs(n, t, reverse):
    nt = n // t
    width = HGRN_HEADS_PER_STEP * HEAD_DIM

    def tok(i):
        return nt - 1 - i if reverse else i

    def sec(col):
        c0 = col // width
        return pl.BlockSpec((t, width), lambda h, i: (tok(i), c0 + h))

    head_tile = pl.BlockSpec((t, width), lambda h, i: (tok(i), h))
    state = pl.BlockSpec((HGRN_HEADS_PER_STEP, t // CHUNK, HEAD_DIM, HEAD_DIM), lambda h, i: (h, tok(i), 0, 0))
    lb = pl.BlockSpec((2, width), lambda h, i: (0, h))
    return sec, head_tile, state, lb


def _hgrn_fwd(proj, hgrn_lb, *, name):
    n = proj.shape[0]
    t = _tile(n, 512, CHUNK)
    nc = t // CHUNK
    hps = HGRN_HEADS_PER_STEP
    width = hps * HEAD_DIM
    lanes = [slice(h * HEAD_DIM, (h + 1) * HEAD_DIM) for h in range(hps)]
    sec, head_tile, state, lbspec = _hgrn_specs(n, t, False)

    def body(q_ref, f_ref, i_ref, lb_ref, o_ref, st_ref, s_acc, g_s, a_s):
        @pl.when(pl.program_id(1) == 0)
        def _():
            s_acc[...] = jnp.zeros_like(s_acc)

        lb = _lower_bound(lb_ref)
        row = lax.broadcasted_iota(jnp.int32, (CHUNK, CHUNK), 0)
        col = lax.broadcasted_iota(jnp.int32, (CHUNK, CHUNK), 1)
        tril = row >= col
        trilb = jnp.where(tril, 1.0, 0.0).astype(BF16)
        rowk = lax.broadcasted_iota(jnp.int32, (CHUNK, width), 0)

        def chunk(c, carry):
            rows = pl.ds(pl.multiple_of(c * CHUNK, CHUNK), CHUNK)
            qr, fr, v = [r[rows, :].astype(F32) for r in (q_ref, f_ref, i_ref)]
            q = qr * _sigmoid(qr)
            f = lb + (1.0 - lb) * _sigmoid(fr)
            k = 1.0 - f
            g = _tri_sum(trilb, jnp.log(f))
            g_s[...] = g
            st0 = [s_acc[h] for h in range(hps)]
            for h in range(hps):
                st_ref[h, c] = st0[h]
            vb = v.astype(BF16)
            for blk in range(CHUNK // SUB):
                lo, hi = blk * SUB, (blk + 1) * SUB
                gref = g_s[lo - 1:lo, :] if blk else jnp.zeros((1, width), F32)
                qi = (q[lo:hi] * jnp.exp(g[lo:hi] - gref)).astype(BF16)
                ki = (k * jnp.exp(jnp.where(rowk < hi, gref - g, NEG_BIG))).astype(BF16)
                for h, ln in enumerate(lanes):
                    a_s[h, lo:hi, :] = _dot_nt(qi[:, ln], ki[:, ln])
            qeb = (q * jnp.exp(g)).astype(BF16)
            o_ref[rows, :] = jnp.concatenate(
                [_dot(jnp.where(tril, a_s[h], 0.0).astype(BF16), vb[:, ln]) + _dot_nt(qeb[:, ln], st0[h].astype(BF16))
                 for h, ln in enumerate(lanes)], axis=1).astype(o_ref.dtype)
            glast = g_s[CHUNK - 1:CHUNK, :]
            kdb = (k * jnp.exp(glast - g)).astype(BF16)
            dec = jnp.exp(glast)
            for h, ln in enumerate(lanes):
                s_acc[h] = st0[h] * dec[:, ln] + _dot_tn(vb[:, ln], kdb[:, ln])
            return carry

        lax.fori_loop(0, nc, chunk, 0)

    return pl.pallas_call(
        body, name=name, grid=(HEADS // hps, n // t),
        in_specs=[sec(COL_Q), sec(COL_F), sec(COL_I), lbspec], out_specs=[head_tile, state],
        out_shape=[jax.ShapeDtypeStruct((n, D_MODEL), BF16),
                   jax.ShapeDtypeStruct((HEADS, n // CHUNK, HEAD_DIM, HEAD_DIM), F32)],
        scratch_shapes=[pltpu.VMEM((hps, HEAD_DIM, HEAD_DIM), F32), pltpu.VMEM((CHUNK, width), F32),
                        pltpu.VMEM((hps, CHUNK, CHUNK), F32)],
        compiler_params=_params("parallel", "arbitrary"),
    )(proj, proj, proj, hgrn_lb)


def _hgrn_bwd(proj, hgrn_lb, do, states, *, name, exchange=None):
    n = proj.shape[0]
    t = _tile(n, 512, CHUNK)
    nc = t // CHUNK
    hps = HGRN_HEADS_PER_STEP
    width = hps * HEAD_DIM
    lanes = [slice(h * HEAD_DIM, (h + 1) * HEAD_DIM) for h in range(hps)]
    sec, head_tile, state, lbspec = _hgrn_specs(n, t, True)

    def body(q_ref, f_ref, i_ref, lb_ref, do_ref, st_ref, dqfi_ref, dlb_ref, d_acc, g_s, a_s, dq_s,
             dg_s):
        first = pl.program_id(1) == 0

        @pl.when(first)
        def _():
            d_acc[...] = jnp.zeros_like(d_acc)

        lb_all = _lower_bound(lb_ref)
        row = lax.broadcasted_iota(jnp.int32, (CHUNK, CHUNK), 0)
        col = lax.broadcasted_iota(jnp.int32, (CHUNK, CHUNK), 1)
        tril = row >= col
        trilb = jnp.where(tril, 1.0, 0.0).astype(BF16)
        triub = jnp.where(row <= col, 1.0, 0.0).astype(BF16)
        groups = HGRN_BWD_LANE_GROUPS
        hg = hps // groups
        gw = hg * HEAD_DIM
        rowk = lax.broadcasted_iota(jnp.int32, (CHUNK, gw), 0)

        def per_head(fn):
            return jnp.concatenate([fn(h, ln) for h, ln in enumerate(lanes[:hg])], axis=1)

        def chunk(j, dlb):
            c = nc - 1 - j
            rows = pl.ds(pl.multiple_of(c * CHUNK, CHUNK), CHUNK)
            return dlb + jnp.concatenate([group(gi, c, rows) for gi in range(groups)], axis=1)

        def group(gi, c, rows):
            gl = slice(gi * gw, (gi + 1) * gw)
            h0 = gi * hg
            lb = lb_all[:, gl]
            qr, fr, v, dout = [r[rows, gl].astype(F32) for r in (q_ref, f_ref, i_ref, do_ref)]
            sq = _sigmoid(qr)
            q = qr * sq
            sf = _sigmoid(fr)
            f = lb + (1.0 - lb) * sf
            k = 1.0 - f
            g = _tri_sum(trilb, jnp.log(f))
            g_s[:, gl] = g
            st0 = [st_ref[h0 + h, c] for h in range(hg)]
            dt = [d_acc[h0 + h] for h in range(hg)]
            vb, dob = v.astype(BF16), dout.astype(BF16)
            dtb = [x.astype(BF16) for x in dt]
            st0b = [x.astype(BF16) for x in st0]
            glast = g_s[CHUNK - 1:CHUNK, gl]
            eg = jnp.exp(g)
            kdec = jnp.exp(glast - g)
            qeb, kdb = (q * eg).astype(BF16), (k * kdec).astype(BF16)
            aps = [jnp.where(row > col, _dot_nt(dob[:, ln], vb[:, ln]), 0.0) for ln in lanes[:hg]]
            dov = dout * v
            adiag = per_head(lambda h, ln: jnp.broadcast_to(
                jnp.sum(dov[:, ln], axis=-1, keepdims=True), (CHUNK, HEAD_DIM)))
            dq_inter = per_head(lambda h, ln: _dot(dob[:, ln], st0b[h]))
            dk_inter = per_head(lambda h, ln: _dot(vb[:, ln], dtb[h]))
            dk_st = kdec * dk_inter
            dg = qeb.astype(F32) * dq_inter
            dg_minus = kdb.astype(F32) * dk_inter
            dg = dg - dg_minus
            for blk in range(CHUNK // SUB):
                lo, hi = blk * SUB, (blk + 1) * SUB
                gref = g_s[lo - 1:lo, gl] if blk else jnp.zeros((1, gw), F32)
                qscale = jnp.exp(g[lo:hi] - gref)
                kscale = jnp.exp(jnp.where(rowk < hi, gref - g, NEG_BIG))
                qi = (q[lo:hi] * qscale).astype(BF16)
                ki = (k * kscale).astype(BF16)
                for h, ln in enumerate(lanes[:hg]):
                    a_s[h0 + h, lo:hi, :] = _dot_nt(qi[:, ln], ki[:, ln])
                apb = [x[lo:hi].astype(BF16) for x in aps]
                from_k = per_head(lambda h, ln: _dot(apb[h], ki[:, ln]))
                from_q = per_head(lambda h, ln: _dot_tn(apb[h], qi[:, ln]))
                dq_s[lo:hi, gl] = qscale * from_k
                dg_s[lo:hi, gl] = qi.astype(F32) * from_k
                dk_st = dk_st + kscale * from_q
                dg = dg - ki.astype(F32) * from_q
            dg = dg + dg_s[:, gl]
            dv = per_head(lambda h, ln: _dot_tn(jnp.where(tril, a_s[h0 + h], 0.0).astype(BF16), dob[:, ln])
                          + _dot_nt(kdb[:, ln], dtb[h]))
            dq_st = dq_s[:, gl] + eg * dq_inter
            dq = dq_st + adiag * k
            dk = dk_st + adiag * q
            dec = jnp.exp(glast)
            dt_dec = [dt[h] * dec[:, ln] for h, ln in enumerate(lanes[:hg])]
            for h, ln in enumerate(lanes[:hg]):
                d_acc[h0 + h] = dt_dec[h] + _dot_tn(dob[:, ln], qeb[:, ln])
            later = per_head(lambda h, ln: _colsum(dt_dec[h] * st0[h])) + _colsum(dg_minus)
            dlf = later + _tri_sum(triub, dg)
            df = dlf / f - dk
            for sec, val in enumerate((dq * (sq * (1.0 + qr * (1.0 - sq))), df * (1.0 - lb) * sf * (1.0 - sf), dv)):
                dqfi_ref[rows, sec * width + gi * gw:sec * width + (gi + 1) * gw] = val.astype(dqfi_ref.dtype)
            return _colsum(df * (1.0 - sf))

        dlb = lax.fori_loop(0, nc, chunk, jnp.zeros((1, width), F32))
        _accumulate(dlb_ref, dlb, first)

    assert hps == HEADS
    nt = n // t
    return _call(
        body, name=name, grid=(1, nt),
        in_specs=[sec(COL_Q), sec(COL_F), sec(COL_I), lbspec, head_tile, state],
        out_specs=[pl.BlockSpec((t, 3 * width), lambda h, i: (nt - 1 - i, 0)),
                   pl.BlockSpec((1, width), lambda h, i: (0, h))],
        out_shape=[jax.ShapeDtypeStruct((n, 3 * D_MODEL), BF16), jax.ShapeDtypeStruct((1, D_MODEL), F32)],
        args=[proj, proj, proj, hgrn_lb, do, states], semantics=("parallel", "arbitrary"),
        scratch=[pltpu.VMEM((hps, HEAD_DIM, HEAD_DIM), F32), pltpu.VMEM((CHUNK, width), F32),
                 pltpu.VMEM((hps, CHUNK, CHUNK), F32), pltpu.VMEM((CHUNK, width), F32),
                 pltpu.VMEM((CHUNK, width), F32)],
        exchange=exchange)


def _pool_fwd(proj, pool_w, pool_scale, *, name):
    n = proj.shape[0]
    t = _tile(n, 1024, POOL_HALO)
    per = t // POOL_HALO
    c0 = COL_POOL // POOL_WIDTH

    def body(u_ref, halo_ref, pw_ref, ps_ref, pooled_ref, mixed_ref, ext):
        i = pl.program_id(0)
        u = u_ref[...].astype(F32)
        ext[POOL_HALO:POOL_HALO + t, :] = u
        ext[0:POOL_HALO, :] = jnp.where(i > 0, halo_ref[...].astype(F32), 0.0)
        pos = i * t + lax.broadcasted_iota(jnp.int32, (t, POOL_CH), 0) + 1
        for grp, win in enumerate(POOL_WINDOWS):
            cols = slice(grp * POOL_CH, (grp + 1) * POOL_CH)
            acc = u[:, cols]
            for j in range(1, win):
                acc = acc + ext[POOL_HALO - j:POOL_HALO - j + t, cols]
            pooled = (acc / jnp.minimum(pos, win).astype(F32) - u[:, cols]).astype(BF16)
            pooled_ref[:, cols] = pooled
            mixed_ref[:, cols] = (_dot(pooled, pw_ref[grp].astype(BF16)) * ps_ref[:, cols]).astype(BF16)

    tile = pl.BlockSpec((t, POOL_WIDTH), lambda i: (i, 0))
    return pl.pallas_call(
        body, name=name, grid=(n // t,),
        in_specs=[pl.BlockSpec((t, POOL_WIDTH), lambda i: (i, c0)),
                  pl.BlockSpec((POOL_HALO, POOL_WIDTH), lambda i: (jnp.maximum(i * per - 1, 0), c0)),
                  pl.BlockSpec((len(POOL_WINDOWS), POOL_CH, POOL_CH), lambda i: (0, 0, 0)),
                  pl.BlockSpec((1, POOL_WIDTH), lambda i: (0, 0))],
        out_specs=[tile, tile],
        out_shape=[jax.ShapeDtypeStruct((n, POOL_WIDTH), BF16), jax.ShapeDtypeStruct((n, POOL_WIDTH), BF16)],
        scratch_shapes=[pltpu.VMEM((t + POOL_HALO, POOL_WIDTH), F32)],
        compiler_params=_params("parallel"),
    )(proj, proj, pool_w, pool_scale)


def _pool_bwd(dmixed, pooled, pool_w, pool_scale, *, name):
    n = dmixed.shape[0]
    t = _tile(n, 1024, POOL_HALO)
    per = t // POOL_HALO
    nb = n // t

    def body(dm_ref, dmh_ref, p_ref, pw_ref, ps_ref, du_ref, dpw_ref, dps_ref, ext):
        i = pl.program_id(0)

        @pl.when(i == 0)
        def _():
            dpw_ref[...] = jnp.zeros_like(dpw_ref)
            dps_ref[...] = jnp.zeros_like(dps_ref)

        dm, dmh = dm_ref[...], dmh_ref[...]
        pos = i * t + lax.broadcasted_iota(jnp.int32, (t, POOL_CH), 0) + 1
        for grp, win in enumerate(POOL_WINDOWS):
            cols = slice(grp * POOL_CH, (grp + 1) * POOL_CH)
            pwb = pw_ref[grp].astype(BF16)
            pb = p_ref[:, cols]
            scale = ps_ref[:, cols]
            dps_ref[:, cols] += _colsum(dm[:, cols] * _dot(pb, pwb))
            dpm = (dm[:, cols] * scale).astype(BF16)
            dpw_ref[grp] += _dot_tn(pb, dpm)
            dpool = _dot_nt(dpm, pwb)
            dpool_next = _dot_nt((dmh[:, cols] * scale).astype(BF16), pwb)
            ext[0:t, cols] = dpool / jnp.minimum(pos, win).astype(F32)
            ext[t:t + POOL_HALO, cols] = jnp.where(i < nb - 1, dpool_next * (1.0 / win), 0.0)
            acc = -dpool
            for j in range(win):
                acc = acc + ext[j:j + t, cols]
            du_ref[:, cols] = acc.astype(du_ref.dtype)

    tile = pl.BlockSpec((t, POOL_WIDTH), lambda i: (i, 0))
    return pl.pallas_call(
        body, name=name, grid=(nb,),
        in_specs=[tile, pl.BlockSpec((POOL_HALO, POOL_WIDTH), lambda i: (jnp.minimum((i + 1) * per, nb * per - 1), 0)),
                  tile, pl.BlockSpec((len(POOL_WINDOWS), POOL_CH, POOL_CH), lambda i: (0, 0, 0)),
                  pl.BlockSpec((1, POOL_WIDTH), lambda i: (0, 0))],
        out_specs=[tile, pl.BlockSpec((len(POOL_WINDOWS), POOL_CH, POOL_CH), lambda i: (0, 0, 0)),
                   pl.BlockSpec((1, POOL_WIDTH), lambda i: (0, 0))],
        out_shape=[jax.ShapeDtypeStruct((n, POOL_WIDTH), BF16),
                   jax.ShapeDtypeStruct((len(POOL_WINDOWS), POOL_CH, POOL_CH), F32),
                   jax.ShapeDtypeStruct((1, POOL_WIDTH), F32)],
        scratch_shapes=[pltpu.VMEM((t + POOL_HALO, POOL_WIDTH), F32)],
        compiler_params=_params("arbitrary"),
    )(dmixed, dmixed, pooled, pool_w, pool_scale)


def _adamw(w, g, m, v):
    m2 = ADAM_B1 * m + (1.0 - ADAM_B1) * g
    v2 = ADAM_B2 * v + (1.0 - ADAM_B2) * (g * g)
    m_hat = m2 * (1.0 / (1.0 - ADAM_B1 ** ADAM_STEP))
    v_hat = v2 * (1.0 / (1.0 - ADAM_B2 ** ADAM_STEP))
    delta = -ADAM_LR * (m_hat / (jnp.sqrt(v_hat) + ADAM_EPS) + ADAM_WD * w)
    return delta, m2, v2


def _adam_big(recv, w, m, v, *, name):
    r, c = w.shape
    tr = _tile(r, 256, 16)

    def body(recv_ref, w_ref, m_ref, v_ref, g_ref, d_ref, m2_ref, v2_ref):
        g = recv_ref[0].astype(F32)
        for i in range(1, N_DEV):
            g = g + recv_ref[i].astype(F32)
        delta, m2, v2 = _adamw(w_ref[...], g, m_ref[...], v_ref[...])
        g_ref[...] = g
        d_ref[...] = delta
        m2_ref[...] = m2
        v2_ref[...] = v2

    tile = pl.BlockSpec((tr, c), lambda i: (i, 0))
    out = jax.ShapeDtypeStruct((r, c), F32)
    return pl.pallas_call(
        body, name=name, grid=(r // tr,),
        in_specs=[pl.BlockSpec((N_DEV, tr, c), lambda i: (0, i, 0)), tile, tile, tile],
        out_specs=[tile] * 4, out_shape=[out] * 4, compiler_params=_params("parallel"),
    )(recv, w, m, v)


def _adam_small(parts, w, m, v, *, name):
    n = len(SMALL_PARAMS)

    def body(*refs):
        parts_r, w_r, m_r, v_r = (refs[i * n:(i + 1) * n] for i in range(4))
        outs = refs[4 * n:]
        for j, key in enumerate(SMALL_PARAMS):
            g = parts_r[j][0]
            for i in range(1, N_DEV):
                g = g + parts_r[j][i]
            w_ = w_r[j][...]
            if key == "hgrn_lb":
                s0 = 1.0 / (1.0 + jnp.exp(w_[1:2] - w_[0:1]))
                ga = g * s0 * (1.0 - s0)
                sign = jnp.where(lax.broadcasted_iota(jnp.int32, w_.shape, 0) == 0, 1.0, -1.0)
                g = sign * jnp.broadcast_to(ga, w_.shape)
            delta, m2, v2 = _adamw(w_, g, m_r[j][...], v_r[j][...])
            for q, val in enumerate((g, delta, m2, v2)):
                outs[q * n + j][...] = val

    out_shape = [jax.ShapeDtypeStruct(w[k].shape, F32) for _ in range(4) for k in SMALL_PARAMS]
    res = pl.pallas_call(body, name=name, out_shape=out_shape, compiler_params=_params())(
        *[t[k] for t in (parts, w, m, v) for k in SMALL_PARAMS])
    return [res[q * n:(q + 1) * n] for q in range(4)]


def _as_2d(a):
    return a.reshape(-1, a.shape[-1])


SPLIT_AXIS = dict(BIG_WEIGHTS)


def _gather_of(names, weights):
    return _Exchange([_shard_to_send(weights[k][0], SPLIT_AXIS[k]) for k in names], gather=True)


def _scatter_of(names, dfull):
    return _Exchange([_to_slots(dfull[k], SPLIT_AXIS[k]) for k in names], gather=False)


def _shard_to_send(w, axis):
    return (w.T if axis == 1 else w).astype(BF16)


def _to_slots(dw, axis):
    rows, cols = dw.shape
    return dw.reshape(N_DEV, rows // N_DEV, cols)


def _from_slots(gathered, axis):
    _, r, c = gathered.shape
    return gathered.reshape(N_DEV * r, c)


def kernel(x, p, ffn1_norm, ffn1_w1, ffn1_w3, ffn1_w2, mix_norm, w_in, hgrn_lb, hgrn_onorm, w_branch_a, pool_w, pool_scale, w_branch_b, w_out, ffn2_norm, ffn2_w1, ffn2_w3, ffn2_w2, ple_norm, ple_w_gate, ple_w_proj, ple_post_norm, final_norm, loss_target, m_ffn1_norm, m_ffn1_w1, m_ffn1_w3, m_ffn1_w2, m_mix_norm, m_w_in, m_hgrn_lb, m_hgrn_onorm, m_w_branch_a, m_pool_w, m_pool_scale, m_w_branch_b, m_w_out, m_ffn2_norm, m_ffn2_w1, m_ffn2_w3, m_ffn2_w2, m_ple_norm, m_ple_w_gate, m_ple_w_proj, m_ple_post_norm, m_final_norm, v_ffn1_norm, v_ffn1_w1, v_ffn1_w3, v_ffn1_w2, v_mix_norm, v_w_in, v_hgrn_lb, v_hgrn_onorm, v_w_branch_a, v_pool_w, v_pool_scale, v_w_branch_b, v_w_out, v_ffn2_norm, v_ffn2_w1, v_ffn2_w3, v_ffn2_w2, v_ple_norm, v_ple_w_gate, v_ple_w_proj, v_ple_post_norm, v_final_norm):
    weights = dict(ffn1_norm=ffn1_norm, ffn1_w1=ffn1_w1, ffn1_w3=ffn1_w3, ffn1_w2=ffn1_w2, mix_norm=mix_norm, w_in=w_in, hgrn_lb=hgrn_lb, hgrn_onorm=hgrn_onorm, w_branch_a=w_branch_a, pool_w=pool_w, pool_scale=pool_scale, w_branch_b=w_branch_b, w_out=w_out, ffn2_norm=ffn2_norm, ffn2_w1=ffn2_w1, ffn2_w3=ffn2_w3, ffn2_w2=ffn2_w2, ple_norm=ple_norm, ple_w_gate=ple_w_gate, ple_w_proj=ple_w_proj, ple_post_norm=ple_post_norm, final_norm=final_norm)
    mom1 = dict(ffn1_norm=m_ffn1_norm, ffn1_w1=m_ffn1_w1, ffn1_w3=m_ffn1_w3, ffn1_w2=m_ffn1_w2, mix_norm=m_mix_norm, w_in=m_w_in, hgrn_lb=m_hgrn_lb, hgrn_onorm=m_hgrn_onorm, w_branch_a=m_w_branch_a, pool_w=m_pool_w, pool_scale=m_pool_scale, w_branch_b=m_w_branch_b, w_out=m_w_out, ffn2_norm=m_ffn2_norm, ffn2_w1=m_ffn2_w1, ffn2_w3=m_ffn2_w3, ffn2_w2=m_ffn2_w2, ple_norm=m_ple_norm, ple_w_gate=m_ple_w_gate, ple_w_proj=m_ple_w_proj, ple_post_norm=m_ple_post_norm, final_norm=m_final_norm)
    mom2 = dict(ffn1_norm=v_ffn1_norm, ffn1_w1=v_ffn1_w1, ffn1_w3=v_ffn1_w3, ffn1_w2=v_ffn1_w2, mix_norm=v_mix_norm, w_in=v_w_in, hgrn_lb=v_hgrn_lb, hgrn_onorm=v_hgrn_onorm, w_branch_a=v_w_branch_a, pool_w=v_pool_w, pool_scale=v_pool_scale, w_branch_b=v_w_branch_b, w_out=v_w_out, ffn2_norm=v_ffn2_norm, ffn2_w1=v_ffn2_w1, ffn2_w3=v_ffn2_w3, ffn2_w2=v_ffn2_w2, ple_norm=v_ple_norm, ple_w_gate=v_ple_w_gate, ple_w_proj=v_ple_w_proj, ple_post_norm=v_ple_post_norm, final_norm=v_final_norm)

    xs = x[0]
    ps = p[0, 0].astype(BF16)
    tgt = loss_target[0]
    n = xs.shape[0]

    g_f1, g_mix, g_on, g_f2 = ffn1_norm, mix_norm, hgrn_onorm, ffn2_norm
    g_ple, g_post, g_fin = ple_norm, ple_post_norm, final_norm.reshape(1, D_MODEL)
    lb2 = hgrn_lb
    pw, pscale = pool_w[0], pool_scale

    full = {}

    def keep(names, gathered):
        for k, g in zip(names, gathered):
            full[k] = _from_slots(g, SPLIT_AXIS[k])

    names = ("ffn1_w1", "ffn1_w3")
    ex = _gather_of(names, weights)
    h1 = _rms_fwd(xs, g_f1, name="ffn1_rms", exchange=ex)
    keep(names, ex.received)
    names = ("ffn1_w2", "w_in")
    ex = _gather_of(names, weights)
    a1, b1, s1 = _ffn_up(h1, full["ffn1_w1"], full["ffn1_w3"], name="ffn1_up", exchange=ex)
    keep(names, ex.received)
    names = ("w_branch_a", "w_branch_b", "w_out")
    ex = _gather_of(names, weights)
    x1, h2 = _mm_nn_res_rms(s1, full["ffn1_w2"], xs, g_mix, name="ffn1_down", scale=0.5, exchange=ex)
    keep(names, ex.received)
    names = ("ffn2_w1", "ffn2_w3", "ffn2_w2", "ple_w_gate", "ple_w_proj")
    ex = _gather_of(names, weights)
    proj = _mm_nn_wide(h2, full["w_in"], name="w_in_proj", out_dtype=BF16, tn=512, exchange=ex)
    keep(names, ex.received)
    o, states = _hgrn_fwd(proj, lb2, name="hgrn_fwd")
    on = _hgrn_post_fwd(o, proj, g_on, name="hgrn_post_fwd")
    ya = _mm_nn_wide(on, full["w_branch_a"], name="branch_a", out_dtype=BF16, b_is_km=True)
    pooled, mixed = _pool_fwd(proj, pw, pscale, name="pool_fwd")
    yb = _mm_nn_wide(mixed, full["w_branch_b"], name="branch_b", out_dtype=BF16)
    y = _merge_fwd(proj, ya, yb, name="merge_fwd")
    x2, h3 = _mm_nn_res_rms(y, full["w_out"], x1, g_f2, name="w_out_proj", scale=1.0)
    a2, b2, s2 = _ffn_up(h3, full["ffn2_w1"], full["ffn2_w3"], name="ffn2_up")
    x3, h4 = _mm_nn_res_rms(s2, full["ffn2_w2"], x2, g_ple, name="ffn2_down", scale=0.5)
    gpre = _mm_nn_wide(h4, full["ple_w_gate"], name="ple_gate", out_dtype=BF16, b_is_km=True)
    z = _mm_nn_wide(ps, full["ple_w_proj"], name="ple_proj", out_dtype=BF16)
    dx4, dpre, dz, loss_part, d_fin, d_post = _ple_final(x3, gpre, z, tgt, g_post, g_fin, name="ple_final")

    dfull, received = {}, {}

    def sent(names, exchange):
        received.update(zip(names, exchange.received))

    dfull["ple_w_proj"] = _mm_tn(ps, dz, name="d_ple_w_proj", tn=1024, tm=1024, transpose_out=True)
    dfull["ple_w_gate"] = _mm_tn(h4, dpre, name="d_ple_w_gate", tn=1024, tm=1024)
    dx3, dx3s, d_ple = _mm_nt_rms_bwd([(dpre, full["ple_w_gate"], False)], x3, dx4, g_ple, name="ple_rms_bwd", tn=1024,
                                      half_scale=0.5)

    names = ("ple_w_proj", "ple_w_gate")
    ex = _scatter_of(names, dfull)
    da2, db2 = _ffn_bwd_mid(dx3s, full["ffn2_w2"], a2, b2, name="ffn2_bwd_mid", exchange=ex)
    sent(names, ex)
    dfull["ffn2_w2"] = _mm_tn(s2, dx3s, name="ffn2_dw2", tn=1024, tm=1024)
    dfull["ffn2_w1"] = _mm_tn(h3, da2, name="ffn2_dw1", tn=1024, tm=2816, transpose_out=True)
    dfull["ffn2_w3"] = _mm_tn(h3, db2, name="ffn2_dw3", tn=1024, tm=2816, transpose_out=True)
    names = ("ffn2_w2",)
    ex = _scatter_of(names, dfull)
    dx2, dx2b, d_f2 = _mm_nt_rms_bwd([(da2, full["ffn2_w1"], True), (db2, full["ffn2_w3"], True)], x2, dx3, g_f2,
                                     name="ffn2_rms_bwd", tn=512, half_scale=1.0, exchange=ex)
    sent(names, ex)

    dfull["w_out"] = _mm_tn(y, dx2b, name="d_w_out", tn=1024, tm=1024)
    dy = _mm_nn_wide(dx2b, full["w_out"], name="d_y", out_dtype=BF16)
    dya, dyb, dga, dgb = _merge_bwd(dy, proj, ya, yb, name="merge_bwd")

    dfull["w_branch_b"] = _mm_tn(mixed, dyb, name="d_w_branch_b", tn=1024, tm=1024, transpose_out=True)
    dmixed = _mm_nn_wide(dyb, full["w_branch_b"], name="d_mixed", out_dtype=F32, b_is_km=True)
    du, d_pw, d_ps = _pool_bwd(dmixed, pooled, pw, pscale, name="pool_bwd")

    dfull["w_branch_a"] = _mm_tn(on, dya, name="d_w_branch_a", tn=1024, tm=1024)
    don = _mm_nn_wide(dya, full["w_branch_a"], name="d_on", out_dtype=BF16)
    dog, do, d_on = _hgrn_post_bwd(don, o, proj, g_on, name="hgrn_post_bwd")
    names = ("ffn2_w1", "ffn2_w3", "w_out", "w_branch_b", "w_branch_a")
    ex = _scatter_of(names, dfull)
    dqfi, d_lb = _hgrn_bwd(proj, lb2, do, states, name="hgrn_bwd", exchange=ex)
    sent(names, ex)
    dproj = [dqfi, dog, du, dga, dgb]
    dfull["w_in"] = jnp.concatenate(
        [_mm_tn(h2, part, name=f"d_w_in_{j}", tn=1024, tm=3072, transpose_out=True) for j, part in enumerate(dproj)],
        axis=0)
    names = ("w_in",)
    ex = _scatter_of(names, dfull)
    dx1, dx1s, d_mix = _mm_nt_rms_bwd([(dproj, full["w_in"], True)], x1, dx2, g_mix, name="mix_rms_bwd", tn=512,
                                      half_scale=0.5, exchange=ex)
    sent(names, ex)

    da1, db1 = _ffn_bwd_mid(dx1s, full["ffn1_w2"], a1, b1, name="ffn1_bwd_mid")
    dfull["ffn1_w2"] = _mm_tn(s1, dx1s, name="ffn1_dw2", tn=1024, tm=1024)
    names = ("ffn1_w2",)
    ex = _scatter_of(names, dfull)
    dfull["ffn1_w1"] = _mm_tn(h1, da1, name="ffn1_dw1", tn=1024, tm=2816, transpose_out=True, exchange=ex)
    sent(names, ex)
    names = ("ffn1_w1",)
    ex = _scatter_of(names, dfull)
    dfull["ffn1_w3"] = _mm_tn(h1, db1, name="ffn1_dw3", tn=1024, tm=2816, transpose_out=True, exchange=ex)
    sent(names, ex)
    names = ("ffn1_w3",)
    ex = _scatter_of(names, dfull)
    grad_x, _, d_f1 = _mm_nt_rms_bwd([(da1, full["ffn1_w1"], True), (db1, full["ffn1_w3"], True)], xs, dx1, g_f1,
                                     name="ffn1_rms_bwd", tn=512, half_scale=1.0, exchange=ex)
    sent(names, ex)

    small_part = dict(ffn1_norm=d_f1, mix_norm=d_mix, hgrn_onorm=d_on, ffn2_norm=d_f2, ple_norm=d_ple,
                      ple_post_norm=d_post, final_norm=d_fin, hgrn_lb=d_lb, pool_scale=d_ps, pool_w=_as_2d(d_pw))
    gathered_small = _exchange_now([small_part[k] for k in SMALL_PARAMS] + [loss_part], name="gather_small_grads",
                                   gather=True)
    small_all = dict(zip(SMALL_PARAMS, gathered_small))
    loss = jnp.sum(gathered_small[-1])

    grads, deltas, new_m, new_v = {}, {}, {}, {}
    for name, axis in BIG_WEIGHTS:
        shape, recv = weights[name].shape, received[name]
        own = [t[name][0].T if axis == 1 else t[name][0] for t in (weights, mom1, mom2)]
        res = _adam_big(recv, *own, name=f"adam_{name}")
        grads[name], deltas[name], new_m[name], new_v[name] = [(r.T if axis == 1 else r).reshape(shape) for r in res]
    res = _adam_small(small_all, *[{k: _as_2d(t[k]) for k in SMALL_PARAMS} for t in (weights, mom1, mom2)],
                      name="adam_small")
    for store, vals in zip((grads, deltas, new_m, new_v), res):
        store.update({k: val.reshape(weights[k].shape) for k, val in zip(SMALL_PARAMS, vals)})

    return (loss, grad_x.reshape(x.shape), *[grads[k] for k in WEIGHT_ORDER], *[deltas[k] for k in WEIGHT_ORDER],
            *[new_m[k] for k in WEIGHT_ORDER], *[new_v[k] for k in WEIGHT_ORDER])
```
